```python
import jax, jax.numpy as jnp
from jax import lax
import numpy as np

D_MODEL = 2048
BATCH = 8
SEQ = 8192
DEPTH = 1

D_MIX = 2 * D_MODEL
D_A = D_MIX // 2
CHUNK_A = 128
N_GROUPS_A = 16
D_HEAD_A = D_A // N_GROUPS_A
D_SSM = D_MIX - D_A
SSM_HEAD_DIM = 64
N_SSM_HEADS = D_SSM // SSM_HEAD_DIM
N_SSM_GROUPS = 8
HEADS_PER_GROUP = N_SSM_HEADS // N_SSM_GROUPS
D_STATE = 128
SSM_CONV = 4
SSD_CHUNK = 128
D_XBC = D_SSM + 2 * N_SSM_GROUPS * D_STATE
D_IN = 2 * D_A + D_SSM + D_XBC + N_SSM_HEADS
D_FF = 5632
FFN_CONV = 3
D_PLE = 256
EPS = 1e-6

kernel_name = "hybrid_gmlp_ssd_convffn_ple"


def rms_norm(x, g):
    xf = x.astype(jnp.float32)
    y = xf * lax.rsqrt(jnp.mean(xf * xf, axis=-1, keepdims=True) + EPS)
    return (y * g.astype(jnp.float32)).astype(x.dtype)


def causal_dwconv(x, w, b):
    k_w = w.shape[0]
    s = x.shape[1]
    xp = jnp.pad(x, ((0, 0), (k_w - 1, 0), (0, 0)))
    y = b
    for k in range(k_w):
        y = y + xp[:, k:k + s] * w[k]
    return y


def gmlp_chunk_mixer(uv, ln_g, ln_b, w_s, b_s):
    bn, s, _ = uv.shape
    uv = jax.nn.gelu(uv)
    u, v = jnp.split(uv, 2, axis=-1)
    v = v.reshape(bn, s // CHUNK_A, CHUNK_A, N_GROUPS_A, D_HEAD_A)
    vf = v.astype(jnp.float32)
    mu = jnp.mean(vf, axis=-1, keepdims=True)
    var = jnp.mean(jnp.square(vf - mu), axis=-1, keepdims=True)
    g = ln_g.reshape(N_GROUPS_A, D_HEAD_A).astype(jnp.float32)
    bb = ln_b.reshape(N_GROUPS_A, D_HEAD_A).astype(jnp.float32)
    vn = ((vf - mu) * lax.rsqrt(var + EPS) * g + bb).astype(uv.dtype)
    mask = jnp.tril(jnp.ones((CHUNK_A, CHUNK_A), dtype=bool))
    ws = jnp.where(mask[None], w_s, jnp.zeros_like(w_s))
    sg = jnp.einsum('gts,bcsgd->bctgd', ws, vn) + b_s.T[:, :, None]
    return u * sg.reshape(bn, s, D_A)


def ssd_chunked(xs, dt, a, bm, cm, d_skip):
    bn, s, _ = xs.shape
    nc = s // SSD_CHUNK
    shp = (bn, nc, SSD_CHUNK, N_SSM_GROUPS, HEADS_PER_GROUP)
    x5 = xs.reshape(shp + (SSM_HEAD_DIM,))
    dt5 = dt.reshape(shp)
    b5 = bm.reshape(bn, nc, SSD_CHUNK, N_SSM_GROUPS, D_STATE)
    c5 = cm.reshape(bn, nc, SSD_CHUNK, N_SSM_GROUPS, D_STATE)
    a_dt = dt5 * a.reshape(N_SSM_GROUPS, HEADS_PER_GROUP)
    x_dt = x5 * dt5[..., None]
    a_cs = jnp.cumsum(a_dt, axis=2)
    a_t = jnp.moveaxis(a_cs, 2, -1)
    diff = a_t[..., :, None] - a_t[..., None, :]
    mask = jnp.tril(jnp.ones((SSD_CHUNK, SSD_CHUNK), dtype=bool))
    l_mat = jnp.exp(jnp.where(mask, diff, -jnp.inf))
    scores = jnp.einsum('bclgn,bcsgn->bcgls', c5, b5)
    y_diag = jnp.einsum('bcgrls,bcsgrp->bclgrp', scores[:, :, :, None] * l_mat, x_dt)
    decay_states = jnp.exp(a_cs[:, :, -1:] - a_cs)
    states = jnp.einsum('bclgn,bclgrp->bcgrpn', b5, x_dt * decay_states[..., None])
    chunk_decay = jnp.exp(a_cs[:, :, -1])

    def step(h, inp):
        dec, st = inp
        return dec[..., None, None] * h + st, h

    h0 = jnp.zeros((bn, N_SSM_GROUPS, HEADS_PER_GROUP, SSM_HEAD_DIM, D_STATE), jnp.float32)
    _, h_in = lax.scan(step, h0, (jnp.moveaxis(chunk_decay, 1, 0), jnp.moveaxis(states, 1, 0)))
    h_in = jnp.moveaxis(h_in, 0, 1)
    y_off = jnp.einsum('bclgn,bcgrpn->bclgrp', c5, h_in) * jnp.exp(a_cs)[..., None]
    y = y_diag + y_off + d_skip.reshape(N_SSM_GROUPS, HEADS_PER_GROUP, 1) * x5
    return y.reshape(bn, s, D_SSM)


def mamba2_mixer(z, xbc, dt_raw, conv_w, conv_b, dt_bias, a_log, d_skip, norm_g):
    bn, s, _ = z.shape
    xbc = jax.nn.silu(causal_dwconv(xbc, conv_w, conv_b))
    n_bc = N_SSM_GROUPS * D_STATE
    xs = xbc[..., :D_SSM].astype(jnp.float32)
    bm = xbc[..., D_SSM:D_SSM + n_bc].astype(jnp.float32)
    cm = xbc[..., D_SSM + n_bc:].astype(jnp.float32)
    dt = jax.nn.softplus(dt_raw.astype(jnp.float32) + dt_bias.astype(jnp.float32))
    a = -jnp.exp(a_log.astype(jnp.float32))
    y = ssd_chunked(xs, dt, a, bm, cm, d_skip.astype(jnp.float32))
    y = y * jax.nn.silu(z.astype(jnp.float32))
    yg = y.reshape(bn, s, N_SSM_GROUPS, D_SSM // N_SSM_GROUPS)
    yg = yg * lax.rsqrt(jnp.mean(yg * yg, axis=-1, keepdims=True) + EPS)
    y = yg.reshape(bn, s, D_SSM) * norm_g.astype(jnp.float32)
    return y.astype(z.dtype)


def _fwd_setup_inputs(seed: int = 0) -> dict:
    key = jax.random.key(seed)
    ks = jax.random.split(key, 32)
    f32 = jnp.float32
    nrm = lambda k, shape, scale: jax.random.normal(k, shape, f32) * scale
    gain = lambda k, n: 1.0 + 0.05 * jax.random.normal(k, (DEPTH, n), f32)
    dt0 = jnp.exp(jax.random.uniform(ks[5], (DEPTH, N_SSM_HEADS), f32,
                                     np.log(1e-3).astype(np.float32), np.log(1e-1).astype(np.float32)))
    dt_bias = dt0 + jnp.log(-jnp.expm1(-dt0))
    return {
        "x": jax.random.normal(ks[0], (BATCH, SEQ, D_MODEL), f32),
        "p": jax.random.normal(ks[1], (DEPTH, BATCH, SEQ, D_PLE), f32),
        "norm_mix_g": gain(ks[2], D_MODEL),
        "w_in": nrm(ks[3], (DEPTH, D_MODEL, D_IN), D_MODEL ** -0.5),
        "ln_a_g": gain(ks[4], D_A),
        "ln_a_b": nrm(ks[6], (DEPTH, D_A), 0.02),
        "w_s": nrm(ks[7], (DEPTH, N_GROUPS_A, CHUNK_A, CHUNK_A), CHUNK_A ** -0.5),
        "b_s": 1.0 + nrm(ks[8], (DEPTH, N_GROUPS_A, CHUNK_A), 0.1),
        "norm_a_g": gain(ks[9], D_A),
        "conv_ssm_w": nrm(ks[10], (DEPTH, SSM_CONV, D_XBC), SSM_CONV ** -0.5),
        "conv_ssm_b": nrm(ks[11], (DEPTH, D_XBC), 0.02),
        "dt_bias": dt_bias,
        "a_log": jnp.log(jax.random.uniform(ks[12], (DEPTH, N_SSM_HEADS), f32, 1.0, 16.0)),
        "d_skip": 1.0 + nrm(ks[13], (DEPTH, N_SSM_HEADS), 0.1),
        "ssm_norm_g": gain(ks[14], D_SSM),
        "w_out": nrm(ks[15], (DEPTH, D_MIX, D_MODEL), D_MIX ** -0.5),
        "norm_ffn_g": gain(ks[16], D_MODEL),
        "w_up": nrm(ks[17], (DEPTH, D_MODEL, 2 * D_FF), D_MODEL ** -0.5),
        "conv_ffn_w": nrm(ks[18], (DEPTH, FFN_CONV, 2 * D_FF), FFN_CONV ** -0.5),
        "conv_ffn_b": nrm(ks[19], (DEPTH, 2 * D_FF), 0.02),
        "w_down": nrm(ks[20], (DEPTH, D_FF, D_MODEL), D_FF ** -0.5),
        "norm_ple_g": gain(ks[21], D_MODEL),
        "w_ple_gate": nrm(ks[22], (DEPTH, D_MODEL, D_MODEL), D_MODEL ** -0.5),
        "w_ple": nrm(ks[23], (DEPTH, D_PLE, D_MODEL), D_PLE ** -0.5),
        "norm_final_g": 1.0 + 0.05 * jax.random.normal(ks[24], (D_MODEL,), f32),
    }


def _fwd_reference(x, p, norm_mix_g, w_in, ln_a_g, ln_a_b, w_s, b_s, norm_a_g,
              conv_ssm_w, conv_ssm_b, dt_bias, a_log, d_skip, ssm_norm_g, w_out,
              norm_ffn_g, w_up, conv_ffn_w, conv_ffn_b, w_down,
              norm_ple_g, w_ple_gate, w_ple, norm_final_g):
    h = x
    o_z = 2 * D_A
    o_xbc = o_z + D_SSM
    o_dt = o_xbc + D_XBC
    for i in range(DEPTH):
        a = rms_norm(h, norm_mix_g[i])
        proj = a @ w_in[i]
        y_a = gmlp_chunk_mixer(proj[..., :o_z], ln_a_g[i], ln_a_b[i], w_s[i], b_s[i])
        y_a = rms_norm(y_a, norm_a_g[i])
        y_b = mamba2_mixer(proj[..., o_z:o_xbc], proj[..., o_xbc:o_dt], proj[..., o_dt:],
                           conv_ssm_w[i], conv_ssm_b[i], dt_bias[i], a_log[i], d_skip[i],
                           ssm_norm_g[i])
        h = h + jnp.concatenate([y_a, y_b], axis=-1) @ w_out[i]
        f = rms_norm(h, norm_ffn_g[i])
        hid = causal_dwconv(f @ w_up[i], conv_ffn_w[i], conv_ffn_b[i])
        gate, up = jnp.split(hid, 2, axis=-1)
        h = h + (jax.nn.silu(gate) * up) @ w_down[i]
        g_ple = jax.nn.sigmoid(rms_norm(h, norm_ple_g[i]) @ w_ple_gate[i])
        h = h + g_ple * (p[i] @ w_ple[i])
    return rms_norm(h, norm_final_g)


import jax as _jax
import jax.numpy as _jnp

TWIN_FORMAT = 'train_step'
FWD_PARAMS = ['x', 'p', 'norm_mix_g', 'w_in', 'ln_a_g', 'ln_a_b', 'w_s', 'b_s', 'norm_a_g', 'conv_ssm_w', 'conv_ssm_b', 'dt_bias', 'a_log', 'd_skip', 'ssm_norm_g', 'w_out', 'norm_ffn_g', 'w_up', 'conv_ffn_w', 'conv_ffn_b', 'w_down', 'norm_ple_g', 'w_ple_gate', 'w_ple', 'norm_final_g']
TWIN_WEIGHTS = ['norm_mix_g', 'w_in', 'ln_a_g', 'ln_a_b', 'w_s', 'b_s', 'norm_a_g', 'conv_ssm_w', 'conv_ssm_b', 'dt_bias', 'a_log', 'd_skip', 'ssm_norm_g', 'w_out', 'norm_ffn_g', 'w_up', 'conv_ffn_w', 'conv_ffn_b', 'w_down', 'norm_ple_g', 'w_ple_gate', 'w_ple', 'norm_final_g']
TWIN_DIFF_INPUT = 'x'
TWIN_INPUTS = ['x', 'p', 'norm_mix_g', 'w_in', 'ln_a_g', 'ln_a_b', 'w_s', 'b_s', 'norm_a_g', 'conv_ssm_w', 'conv_ssm_b', 'dt_bias', 'a_log', 'd_skip', 'ssm_norm_g', 'w_out', 'norm_ffn_g', 'w_up', 'conv_ffn_w', 'conv_ffn_b', 'w_down', 'norm_ple_g', 'w_ple_gate', 'w_ple', 'norm_final_g', 'loss_target', 'm_norm_mix_g', 'm_w_in', 'm_ln_a_g', 'm_ln_a_b', 'm_w_s', 'm_b_s', 'm_norm_a_g', 'm_conv_ssm_w', 'm_conv_ssm_b', 'm_dt_bias', 'm_a_log', 'm_d_skip', 'm_ssm_norm_g', 'm_w_out', 'm_norm_ffn_g', 'm_w_up', 'm_conv_ffn_w', 'm_conv_ffn_b', 'm_w_down', 'm_norm_ple_g', 'm_w_ple_gate', 'm_w_ple', 'm_norm_final_g', 'v_norm_mix_g', 'v_w_in', 'v_ln_a_g', 'v_ln_a_b', 'v_w_s', 'v_b_s', 'v_norm_a_g', 'v_conv_ssm_w', 'v_conv_ssm_b', 'v_dt_bias', 'v_a_log', 'v_d_skip', 'v_ssm_norm_g', 'v_w_out', 'v_norm_ffn_g', 'v_w_up', 'v_conv_ffn_w', 'v_conv_ffn_b', 'v_w_down', 'v_norm_ple_g', 'v_w_ple_gate', 'v_w_ple', 'v_norm_final_g']
TWIN_OUTPUTS = ['loss', 'grad_x', 'grad_norm_mix_g', 'grad_w_in', 'grad_ln_a_g', 'grad_ln_a_b', 'grad_w_s', 'grad_b_s', 'grad_norm_a_g', 'grad_conv_ssm_w', 'grad_conv_ssm_b', 'grad_dt_bias', 'grad_a_log', 'grad_d_skip', 'grad_ssm_norm_g', 'grad_w_out', 'grad_norm_ffn_g', 'grad_w_up', 'grad_conv_ffn_w', 'grad_conv_ffn_b', 'grad_w_down', 'grad_norm_ple_g', 'grad_w_ple_gate', 'grad_w_ple', 'grad_norm_final_g', 'delta_norm_mix_g', 'delta_w_in', 'delta_ln_a_g', 'delta_ln_a_b', 'delta_w_s', 'delta_b_s', 'delta_norm_a_g', 'delta_conv_ssm_w', 'delta_conv_ssm_b', 'delta_dt_bias', 'delta_a_log', 'delta_d_skip', 'delta_ssm_norm_g', 'delta_w_out', 'delta_norm_ffn_g', 'delta_w_up', 'delta_conv_ffn_w', 'delta_conv_ffn_b', 'delta_w_down', 'delta_norm_ple_g', 'delta_w_ple_gate', 'delta_w_ple', 'delta_norm_final_g', 'new_m_norm_mix_g', 'new_m_w_in', 'new_m_ln_a_g', 'new_m_ln_a_b', 'new_m_w_s', 'new_m_b_s', 'new_m_norm_a_g', 'new_m_conv_ssm_w', 'new_m_conv_ssm_b', 'new_m_dt_bias', 'new_m_a_log', 'new_m_d_skip', 'new_m_ssm_norm_g', 'new_m_w_out', 'new_m_norm_ffn_g', 'new_m_w_up', 'new_m_conv_ffn_w', 'new_m_conv_ffn_b', 'new_m_w_down', 'new_m_norm_ple_g', 'new_m_w_ple_gate', 'new_m_w_ple', 'new_m_norm_final_g', 'new_v_norm_mix_g', 'new_v_w_in', 'new_v_ln_a_g', 'new_v_ln_a_b', 'new_v_w_s', 'new_v_b_s', 'new_v_norm_a_g', 'new_v_conv_ssm_w', 'new_v_conv_ssm_b', 'new_v_dt_bias', 'new_v_a_log', 'new_v_d_skip', 'new_v_ssm_norm_g', 'new_v_w_out', 'new_v_norm_ffn_g', 'new_v_w_up', 'new_v_conv_ffn_w', 'new_v_conv_ffn_b', 'new_v_w_down', 'new_v_norm_ple_g', 'new_v_w_ple_gate', 'new_v_w_ple', 'new_v_norm_final_g']
TWIN_LEAF_KINDS = {'loss': 'loss', 'grad_x': 'grad_x', 'grad_norm_mix_g': 'grad_w', 'grad_w_in': 'grad_w', 'grad_ln_a_g': 'grad_w', 'grad_ln_a_b': 'grad_w', 'grad_w_s': 'grad_w', 'grad_b_s': 'grad_w', 'grad_norm_a_g': 'grad_w', 'grad_conv_ssm_w': 'grad_w', 'grad_conv_ssm_b': 'grad_w', 'grad_dt_bias': 'grad_w', 'grad_a_log': 'grad_w', 'grad_d_skip': 'grad_w', 'grad_ssm_norm_g': 'grad_w', 'grad_w_out': 'grad_w', 'grad_norm_ffn_g': 'grad_w', 'grad_w_up': 'grad_w', 'grad_conv_ffn_w': 'grad_w', 'grad_conv_ffn_b': 'grad_w', 'grad_w_down': 'grad_w', 'grad_norm_ple_g': 'grad_w', 'grad_w_ple_gate': 'grad_w', 'grad_w_ple': 'grad_w', 'grad_norm_final_g': 'grad_w', 'delta_norm_mix_g': 'delta_w', 'delta_w_in': 'delta_w', 'delta_ln_a_g': 'delta_w', 'delta_ln_a_b': 'delta_w', 'delta_w_s': 'delta_w', 'delta_b_s': 'delta_w', 'delta_norm_a_g': 'delta_w', 'delta_conv_ssm_w': 'delta_w', 'delta_conv_ssm_b': 'delta_w', 'delta_dt_bias': 'delta_w', 'delta_a_log': 'delta_w', 'delta_d_skip': 'delta_w', 'delta_ssm_norm_g': 'delta_w', 'delta_w_out': 'delta_w', 'delta_norm_ffn_g': 'delta_w', 'delta_w_up': 'delta_w', 'delta_conv_ffn_w': 'delta_w', 'delta_conv_ffn_b': 'delta_w', 'delta_w_down': 'delta_w', 'delta_norm_ple_g': 'delta_w', 'delta_w_ple_gate': 'delta_w', 'delta_w_ple': 'delta_w', 'delta_norm_final_g': 'delta_w', 'new_m_norm_mix_g': 'new_m', 'new_m_w_in': 'new_m', 'new_m_ln_a_g': 'new_m', 'new_m_ln_a_b': 'new_m', 'new_m_w_s': 'new_m', 'new_m_b_s': 'new_m', 'new_m_norm_a_g': 'new_m', 'new_m_conv_ssm_w': 'new_m', 'new_m_conv_ssm_b': 'new_m', 'new_m_dt_bias': 'new_m', 'new_m_a_log': 'new_m', 'new_m_d_skip': 'new_m', 'new_m_ssm_norm_g': 'new_m', 'new_m_w_out': 'new_m', 'new_m_norm_ffn_g': 'new_m', 'new_m_w_up': 'new_m', 'new_m_conv_ffn_w': 'new_m', 'new_m_conv_ffn_b': 'new_m', 'new_m_w_down': 'new_m', 'new_m_norm_ple_g': 'new_m', 'new_m_w_ple_gate': 'new_m', 'new_m_w_ple': 'new_m', 'new_m_norm_final_g': 'new_m', 'new_v_norm_mix_g': 'new_v', 'new_v_w_in': 'new_v', 'new_v_ln_a_g': 'new_v', 'new_v_ln_a_b': 'new_v', 'new_v_w_s': 'new_v', 'new_v_b_s': 'new_v', 'new_v_norm_a_g': 'new_v', 'new_v_conv_ssm_w': 'new_v', 'new_v_conv_ssm_b': 'new_v', 'new_v_dt_bias': 'new_v', 'new_v_a_log': 'new_v', 'new_v_d_skip': 'new_v', 'new_v_ssm_norm_g': 'new_v', 'new_v_w_out': 'new_v', 'new_v_norm_ffn_g': 'new_v', 'new_v_w_up': 'new_v', 'new_v_conv_ffn_w': 'new_v', 'new_v_conv_ffn_b': 'new_v', 'new_v_w_down': 'new_v', 'new_v_norm_ple_g': 'new_v', 'new_v_w_ple_gate': 'new_v', 'new_v_w_ple': 'new_v', 'new_v_norm_final_g': 'new_v'}


def _forward(args):
    return _fwd_reference(*[args[k] for k in FWD_PARAMS])


def _output_shape():
    def fwd():
        inp = _fwd_setup_inputs(0)
        return _fwd_reference(*[inp[k] for k in FWD_PARAMS])
    out = _jax.eval_shape(fwd)
    return out.shape, out.dtype

N_MICROBATCH = 1
ADAM_LR = 0.001
ADAM_B1 = 0.9
ADAM_B2 = 0.999
ADAM_EPS = 1e-08
ADAM_WD = 0.01
ADAM_STEP = 10
PER_EXAMPLE_BATCH_AXIS = {'x': 0, 'p': 1, 'loss_target': 0}
SHARED_INPUTS = []
_WEIGHT_DTYPES = {'norm_mix_g': _jnp.float32, 'w_in': _jnp.float32, 'ln_a_g': _jnp.float32, 'ln_a_b': _jnp.float32, 'w_s': _jnp.float32, 'b_s': _jnp.float32, 'norm_a_g': _jnp.float32, 'conv_ssm_w': _jnp.float32, 'conv_ssm_b': _jnp.float32, 'dt_bias': _jnp.float32, 'a_log': _jnp.float32, 'd_skip': _jnp.float32, 'ssm_norm_g': _jnp.float32, 'w_out': _jnp.float32, 'norm_ffn_g': _jnp.float32, 'w_up': _jnp.float32, 'conv_ffn_w': _jnp.float32, 'conv_ffn_b': _jnp.float32, 'w_down': _jnp.float32, 'norm_ple_g': _jnp.float32, 'w_ple_gate': _jnp.float32, 'w_ple': _jnp.float32, 'norm_final_g': _jnp.float32}
MOMENT_SCALE = {'norm_mix_g': 1.320927e-01, 'w_in': 5.650447e-02, 'ln_a_g': 3.582107e-02, 'ln_a_b': 3.659116e-02, 'w_s': 3.581477e-02, 'b_s': 5.352051e-02, 'norm_a_g': 8.084470e-02, 'conv_ssm_w': 5.038235e-02, 'conv_ssm_b': 7.532673e-02, 'dt_bias': 1.534795e-01, 'a_log': 2.824011e-01, 'd_skip': 3.009826e-01, 'ssm_norm_g': 6.972693e-02, 'w_out': 1.058361e-01, 'norm_ffn_g': 7.359970e-02, 'w_up': 2.908115e-02, 'conv_ffn_w': 3.005914e-02, 'conv_ffn_b': 3.578089e-02, 'w_down': 4.833547e-02, 'norm_ple_g': 2.197069e-02, 'w_ple_gate': 2.084309e-02, 'w_ple': 4.346622e-02, 'norm_final_g': 3.207786e+01}


def _to_microbatches(a, axis):
    t = _jnp.moveaxis(a, axis, 0)
    t = t.reshape((N_MICROBATCH, t.shape[0] // N_MICROBATCH) + t.shape[1:])
    return _jnp.moveaxis(t, 1, axis + 1)


def setup_inputs(seed: int = 0) -> dict:
    inp = _fwd_setup_inputs(seed)
    key = _jax.random.fold_in(_jax.random.key(seed), 7919)
    shape, _ = _output_shape()
    out = dict(inp)
    out["loss_target"] = _jax.random.normal(_jax.random.fold_in(key, 0), shape, _jnp.float32)
    for i, name in enumerate(TWIN_WEIGHTS):
        w = inp[name].astype(_jnp.float32)
        if MOMENT_SCALE is None:
            s = _jnp.sqrt(_jnp.mean(_jnp.square(w)) + 1e-30)
        else:
            s = MOMENT_SCALE[name]
        km, kv = _jax.random.split(_jax.random.fold_in(key, i + 1))
        out[name] = w
        out["m_" + name] = s * _jax.random.normal(km, w.shape, _jnp.float32)
        out["v_" + name] = (s * s) * _jax.random.uniform(kv, w.shape, _jnp.float32, 0.5, 1.5)
    if N_MICROBATCH > 1:
        for name, axis in PER_EXAMPLE_BATCH_AXIS.items():
            out[name] = _to_microbatches(out[name], axis)
    return {'x': out['x'], 'p': out['p'], 'norm_mix_g': out['norm_mix_g'], 'w_in': out['w_in'], 'ln_a_g': out['ln_a_g'], 'ln_a_b': out['ln_a_b'], 'w_s': out['w_s'], 'b_s': out['b_s'], 'norm_a_g': out['norm_a_g'], 'conv_ssm_w': out['conv_ssm_w'], 'conv_ssm_b': out['conv_ssm_b'], 'dt_bias': out['dt_bias'], 'a_log': out['a_log'], 'd_skip': out['d_skip'], 'ssm_norm_g': out['ssm_norm_g'], 'w_out': out['w_out'], 'norm_ffn_g': out['norm_ffn_g'], 'w_up': out['w_up'], 'conv_ffn_w': out['conv_ffn_w'], 'conv_ffn_b': out['conv_ffn_b'], 'w_down': out['w_down'], 'norm_ple_g': out['norm_ple_g'], 'w_ple_gate': out['w_ple_gate'], 'w_ple': out['w_ple'], 'norm_final_g': out['norm_final_g'], 'loss_target': out['loss_target'], 'm_norm_mix_g': out['m_norm_mix_g'], 'm_w_in': out['m_w_in'], 'm_ln_a_g': out['m_ln_a_g'], 'm_ln_a_b': out['m_ln_a_b'], 'm_w_s': out['m_w_s'], 'm_b_s': out['m_b_s'], 'm_norm_a_g': out['m_norm_a_g'], 'm_conv_ssm_w': out['m_conv_ssm_w'], 'm_conv_ssm_b': out['m_conv_ssm_b'], 'm_dt_bias': out['m_dt_bias'], 'm_a_log': out['m_a_log'], 'm_d_skip': out['m_d_skip'], 'm_ssm_norm_g': out['m_ssm_norm_g'], 'm_w_out': out['m_w_out'], 'm_norm_ffn_g': out['m_norm_ffn_g'], 'm_w_up': out['m_w_up'], 'm_conv_ffn_w': out['m_conv_ffn_w'], 'm_conv_ffn_b': out['m_conv_ffn_b'], 'm_w_down': out['m_w_down'], 'm_norm_ple_g': out['m_norm_ple_g'], 'm_w_ple_gate': out['m_w_ple_gate'], 'm_w_ple': out['m_w_ple'], 'm_norm_final_g': out['m_norm_final_g'], 'v_norm_mix_g': out['v_norm_mix_g'], 'v_w_in': out['v_w_in'], 'v_ln_a_g': out['v_ln_a_g'], 'v_ln_a_b': out['v_ln_a_b'], 'v_w_s': out['v_w_s'], 'v_b_s': out['v_b_s'], 'v_norm_a_g': out['v_norm_a_g'], 'v_conv_ssm_w': out['v_conv_ssm_w'], 'v_conv_ssm_b': out['v_conv_ssm_b'], 'v_dt_bias': out['v_dt_bias'], 'v_a_log': out['v_a_log'], 'v_d_skip': out['v_d_skip'], 'v_ssm_norm_g': out['v_ssm_norm_g'], 'v_w_out': out['v_w_out'], 'v_norm_ffn_g': out['v_norm_ffn_g'], 'v_w_up': out['v_w_up'], 'v_conv_ffn_w': out['v_conv_ffn_w'], 'v_conv_ffn_b': out['v_conv_ffn_b'], 'v_w_down': out['v_w_down'], 'v_norm_ple_g': out['v_norm_ple_g'], 'v_w_ple_gate': out['v_w_ple_gate'], 'v_w_ple': out['v_w_ple'], 'v_norm_final_g': out['v_norm_final_g']}


def _loss(weights, diff, rest, loss_target):
    with _jax.named_scope("forward"):
        args = {**rest, TWIN_DIFF_INPUT: diff, **{k: w.astype(_WEIGHT_DTYPES[k]) for k, w in weights.items()}}
        y = _forward(args)
    with _jax.named_scope("loss_head"):
        err = _jnp.square(y.astype(_jnp.float32) - loss_target)
        return 0.5 * _jnp.sum(_jnp.mean(err, axis=-1)) if err.ndim else 0.5 * err


def _adamw(w, g, m, v):
    m = ADAM_B1 * m + (1.0 - ADAM_B1) * g
    v = ADAM_B2 * v + (1.0 - ADAM_B2) * _jnp.square(g)
    m_hat = m / (1.0 - ADAM_B1 ** ADAM_STEP)
    v_hat = v / (1.0 - ADAM_B2 ** ADAM_STEP)
    delta = -ADAM_LR * (m_hat / (_jnp.sqrt(v_hat) + ADAM_EPS) + ADAM_WD * w)
    return delta, m, v


def reference(x, p, norm_mix_g, w_in, ln_a_g, ln_a_b, w_s, b_s, norm_a_g, conv_ssm_w, conv_ssm_b, dt_bias, a_log, d_skip, ssm_norm_g, w_out, norm_ffn_g, w_up, conv_ffn_w, conv_ffn_b, w_down, norm_ple_g, w_ple_gate, w_ple, norm_final_g, loss_target, m_norm_mix_g, m_w_in, m_ln_a_g, m_ln_a_b, m_w_s, m_b_s, m_norm_a_g, m_conv_ssm_w, m_conv_ssm_b, m_dt_bias, m_a_log, m_d_skip, m_ssm_norm_g, m_w_out, m_norm_ffn_g, m_w_up, m_conv_ffn_w, m_conv_ffn_b, m_w_down, m_norm_ple_g, m_w_ple_gate, m_w_ple, m_norm_final_g, v_norm_mix_g, v_w_in, v_ln_a_g, v_ln_a_b, v_w_s, v_b_s, v_norm_a_g, v_conv_ssm_w, v_conv_ssm_b, v_dt_bias, v_a_log, v_d_skip, v_ssm_norm_g, v_w_out, v_norm_ffn_g, v_w_up, v_conv_ffn_w, v_conv_ffn_b, v_w_down, v_norm_ple_g, v_w_ple_gate, v_w_ple, v_norm_final_g):
    given = dict(x=x, p=p, norm_mix_g=norm_mix_g, w_in=w_in, ln_a_g=ln_a_g, ln_a_b=ln_a_b, w_s=w_s, b_s=b_s, norm_a_g=norm_a_g, conv_ssm_w=conv_ssm_w, conv_ssm_b=conv_ssm_b, dt_bias=dt_bias, a_log=a_log, d_skip=d_skip, ssm_norm_g=ssm_norm_g, w_out=w_out, norm_ffn_g=norm_ffn_g, w_up=w_up, conv_ffn_w=conv_ffn_w, conv_ffn_b=conv_ffn_b, w_down=w_down, norm_ple_g=norm_ple_g, w_ple_gate=w_ple_gate, w_ple=w_ple, norm_final_g=norm_final_g, loss_target=loss_target, m_norm_mix_g=m_norm_mix_g, m_w_in=m_w_in, m_ln_a_g=m_ln_a_g, m_ln_a_b=m_ln_a_b, m_w_s=m_w_s, m_b_s=m_b_s, m_norm_a_g=m_norm_a_g, m_conv_ssm_w=m_conv_ssm_w, m_conv_ssm_b=m_conv_ssm_b, m_dt_bias=m_dt_bias, m_a_log=m_a_log, m_d_skip=m_d_skip, m_ssm_norm_g=m_ssm_norm_g, m_w_out=m_w_out, m_norm_ffn_g=m_norm_ffn_g, m_w_up=m_w_up, m_conv_ffn_w=m_conv_ffn_w, m_conv_ffn_b=m_conv_ffn_b, m_w_down=m_w_down, m_norm_ple_g=m_norm_ple_g, m_w_ple_gate=m_w_ple_gate, m_w_ple=m_w_ple, m_norm_final_g=m_norm_final_g, v_norm_mix_g=v_norm_mix_g, v_w_in=v_w_in, v_ln_a_g=v_ln_a_g, v_ln_a_b=v_ln_a_b, v_w_s=v_w_s, v_b_s=v_b_s, v_norm_a_g=v_norm_a_g, v_conv_ssm_w=v_conv_ssm_w, v_conv_ssm_b=v_conv_ssm_b, v_dt_bias=v_dt_bias, v_a_log=v_a_log, v_d_skip=v_d_skip, v_ssm_norm_g=v_ssm_norm_g, v_w_out=v_w_out, v_norm_ffn_g=v_norm_ffn_g, v_w_up=v_w_up, v_conv_ffn_w=v_conv_ffn_w, v_conv_ffn_b=v_conv_ffn_b, v_w_down=v_w_down, v_norm_ple_g=v_norm_ple_g, v_w_ple_gate=v_w_ple_gate, v_w_ple=v_w_ple, v_norm_final_g=v_norm_final_g)
    weights = {n: given[n] for n in TWIN_WEIGHTS}
    shared = {n: given[n] for n in SHARED_INPUTS}
    per_example = {n: given[n] for n in ['x', 'p']}
    grad_fn = _jax.value_and_grad(_loss, argnums=(0, 1))

    def one_microbatch(ex, loss_target):
        ex = dict(ex)
        diff = ex.pop(TWIN_DIFF_INPUT)
        return grad_fn(weights, diff, {**shared, **ex}, loss_target)

    if N_MICROBATCH == 1:
        loss, (grad_w, grad_x) = one_microbatch(per_example, given["loss_target"])
    else:
        def body(carry, xs):
            loss_sum, grad_sum = carry
            l_k, (gw_k, gx_k) = one_microbatch(xs[0], xs[1])
            with _jax.named_scope("update"):
                return (loss_sum + l_k, _jax.tree.map(_jnp.add, grad_sum, gw_k)), gx_k

        init = (_jnp.zeros((), _jnp.float32), _jax.tree.map(_jnp.zeros_like, weights))
        (loss, grad_w), grad_x = _jax.lax.scan(body, init, (per_example, given["loss_target"]))
    with _jax.named_scope("update"):
        delta_w, new_m, new_v = {}, {}, {}
        for n in TWIN_WEIGHTS:
            delta_w[n], new_m[n], new_v[n] = _adamw(weights[n], grad_w[n], given["m_" + n], given["v_" + n])
    return (loss, grad_x, *[grad_w[n] for n in TWIN_WEIGHTS], *[delta_w[n] for n in TWIN_WEIGHTS],
            *[new_m[n] for n in TWIN_WEIGHTS], *[new_v[n] for n in TWIN_WEIGHTS])
```

```python
import functools

import jax
import jax.numpy as jnp
from jax import lax
from jax.experimental import pallas as pl
from jax.experimental.pallas import tpu as pltpu

F32 = jnp.float32
MXU_DTYPE = jnp.bfloat16
ACT_DTYPE = jnp.bfloat16
XBC_DTYPE = jnp.float32
WIRE_DTYPE = jnp.bfloat16
EPS = 1e-6
CHUNK = 128
D_STATE = 128
HEAD_DIM = 64
HEADS_PER_GROUP = 4
GROUP_CH = HEAD_DIM * HEADS_PER_GROUP
HALO = 16
N_DEV = 8
VMEM_LIMIT = 48 * 1024 * 1024

ADAM_LR = 0.001
ADAM_B1 = 0.9
ADAM_B2 = 0.999
ADAM_EPS = 1e-08
ADAM_WD = 0.01
ADAM_STEP = 10

WEIGHTS = ['norm_mix_g', 'w_in', 'ln_a_g', 'ln_a_b', 'w_s', 'b_s', 'norm_a_g', 'conv_ssm_w', 'conv_ssm_b', 'dt_bias',
           'a_log', 'd_skip', 'ssm_norm_g', 'w_out', 'norm_ffn_g', 'w_up', 'conv_ffn_w', 'conv_ffn_b', 'w_down',
           'norm_ple_g', 'w_ple_gate', 'w_ple', 'norm_final_g']
SHARDED = {'w_in': 'col', 'conv_ssm_w': 'col', 'w_out': 'row', 'w_up': 'col', 'conv_ffn_w': 'col', 'w_down': 'row',
           'w_ple_gate': 'row', 'w_ple': 'col'}
F32_ON_WIRE = ('conv_ssm_w', 'conv_ffn_w')
REPLICATED = [n for n in WEIGHTS if n not in SHARDED]

S = jax.ShapeDtypeStruct


def _pick(dim, cands):
    for c in cands:
        if c <= dim and dim % c == 0:
            return c
    return dim


def _cp(sem, vmem=VMEM_LIMIT):
    return pltpu.CompilerParams(dimension_semantics=sem, vmem_limit_bytes=vmem)


def _mx(v):
    return v.astype(MXU_DTYPE)


def _rms(v, g):
    return v * lax.rsqrt(jnp.mean(v * v, axis=-1, keepdims=True) + EPS) * g


def _mm_nn(a, b, *, out_dtype, name, res=None):
    m, k = a.shape
    _, n = b.shape
    tm, tn, tk = _pick(m, (1024, 512, 256, 128)), _pick(n, (512, 256, 128)), _pick(k, (2048, 1408, 1024, 512, 256, 128))
    nk = k // tk

    def body(*refs):
        if res is None:
            a_ref, b_ref, o_ref, acc_ref = refs
            r_ref = None
        else:
            a_ref, b_ref, r_ref, o_ref, acc_ref = refs
        kk = pl.program_id(2)
        d = jnp.dot(a_ref[...], b_ref[...], preferred_element_type=F32)

        def fin(acc):
            if r_ref is not None:
                acc = acc + r_ref[...]
            o_ref[...] = acc.astype(o_ref.dtype)

        if nk == 1:
            fin(d)
        else:
            @pl.when(kk == 0)
            def _():
                acc_ref[...] = d

            @pl.when(kk > 0)
            def _():
                acc_ref[...] += d

            @pl.when(kk == nk - 1)
            def _():
                fin(acc_ref[...])

    in_specs = [pl.BlockSpec((tm, tk), lambda i, j, kk: (i, kk)), pl.BlockSpec((tk, tn), lambda i, j, kk: (kk, j))]
    args = [a, b]
    if res is not None:
        in_specs.append(pl.BlockSpec((tm, tn), lambda i, j, kk: (i, j)))
        args.append(res)
    return pl.pallas_call(
        body, grid=(m // tm, n // tn, nk), in_specs=in_specs,
        out_specs=pl.BlockSpec((tm, tn), lambda i, j, kk: (i, j)), out_shape=S((m, n), out_dtype),
        scratch_shapes=[pltpu.VMEM((tm, tn), F32)], name=name,
        compiler_params=_cp(("parallel", "parallel", "arbitrary")))(*args)


def _mm_nt(a, b, *, out_dtype, name, res=None, a_split=1):
    if a_split == 1:
        m, k = a.shape
    else:
        _, m, ks = a.shape
        k = ks * a_split
    n, _ = b.shape
    tm, tn = _pick(m, (1024, 512, 256, 128)), _pick(n, (512, 256, 128))
    tk = _pick(k // a_split, (2048, 1408, 1024, 512, 256, 128))
    nk = k // tk
    nks = nk // a_split

    def body(*refs):
        if res is None:
            a_ref, b_ref, o_ref, acc_ref = refs
            r_ref = None
        else:
            a_ref, b_ref, r_ref, o_ref, acc_ref = refs
        kk = pl.program_id(2)
        d = lax.dot_general(a_ref[...], b_ref[...], (((1,), (1,)), ((), ())), preferred_element_type=F32)

        def fin(acc):
            if r_ref is not None:
                acc = acc + r_ref[...]
            o_ref[...] = acc.astype(o_ref.dtype)

        if nk == 1:
            fin(d)
        else:
            @pl.when(kk == 0)
            def _():
                acc_ref[...] = d

            @pl.when(kk > 0)
            def _():
                acc_ref[...] += d

            @pl.when(kk == nk - 1)
            def _():
                fin(acc_ref[...])

    if a_split == 1:
        a_spec = pl.BlockSpec((tm, tk), lambda i, j, kk: (i, kk))
    else:
        a_spec = pl.BlockSpec((None, tm, tk), lambda i, j, kk: (kk // nks, i, kk % nks))
    in_specs = [a_spec, pl.BlockSpec((tn, tk), lambda i, j, kk: (j, kk))]
    args = [a, b]
    if res is not None:
        in_specs.append(pl.BlockSpec((tm, tn), lambda i, j, kk: (i, j)))
        args.append(res)
    return pl.pallas_call(
        body, grid=(m // tm, n // tn, nk), in_specs=in_specs,
        out_specs=pl.BlockSpec((tm, tn), lambda i, j, kk: (i, j)), out_shape=S((m, n), out_dtype),
        scratch_shapes=[pltpu.VMEM((tm, tn), F32)], name=name,
        compiler_params=_cp(("parallel", "parallel", "arbitrary")))(*args)


def _mm_tn(a, b, *, out_dtype, name, b_split=1):
    k, m = a.shape
    if b_split == 1:
        n = b.shape[1]
    else:
        n = b.shape[2] * b_split
    tm = _pick(m, (1024, 512, 256, 128))
    tn = _pick(n // b_split, (1024, 512, 256, 128))
    tk = _pick(k, (512, 256, 128))
    nk = k // tk
    njs = (n // b_split) // tn

    def body(a_ref, b_ref, o_ref, acc_ref):
        kk = pl.program_id(2)
        d = lax.dot_general(a_ref[...], b_ref[...], (((0,), (0,)), ((), ())), preferred_element_type=F32)
        if nk == 1:
            o_ref[...] = d.astype(o_ref.dtype)
        else:
            @pl.when(kk == 0)
            def _():
                acc_ref[...] = d

            @pl.when(kk > 0)
            def _():
                acc_ref[...] += d

            @pl.when(kk == nk - 1)
            def _():
                o_ref[...] = acc_ref[...].astype(o_ref.dtype)

    if b_split == 1:
        b_spec = pl.BlockSpec((tk, tn), lambda i, j, kk: (kk, j))
    else:
        b_spec = pl.BlockSpec((None, tk, tn), lambda i, j, kk: (j // njs, kk, j % njs))
    return pl.pallas_call(
        body, grid=(m // tm, n // tn, nk),
        in_specs=[pl.BlockSpec((tk, tm), lambda i, j, kk: (kk, i)), b_spec],
        out_specs=pl.BlockSpec((tm, tn), lambda i, j, kk: (i, j)), out_shape=S((m, n), out_dtype),
        scratch_shapes=[pltpu.VMEM((tm, tn), F32)], name=name,
        compiler_params=_cp(("parallel", "parallel", "arbitrary")))(a, b)


def _rms_fwd(x, g, name):
    t, d = x.shape
    tr = _pick(t, (512, 256, 128))

    def body(x_ref, g_ref, o_ref):
        o_ref[...] = _rms(x_ref[...], g_ref[...]).astype(o_ref.dtype)

    return pl.pallas_call(
        body, grid=(t // tr,),
        in_specs=[pl.BlockSpec((tr, d), lambda i: (i, 0)), pl.BlockSpec((1, d), lambda i: (0, 0))],
        out_specs=pl.BlockSpec((tr, d), lambda i: (i, 0)), out_shape=S((t, d), ACT_DTYPE), name=name,
        compiler_params=_cp(("parallel",)))(x, g)


def _rms_bwd(xin, g, dn, dres, name):
    t, d = xin.shape
    tr = _pick(t, (256, 128))

    def body(x_ref, g_ref, dn_ref, dr_ref, dx_ref, dg_ref):
        @pl.when(pl.program_id(0) == 0)
        def _():
            dg_ref[...] = jnp.zeros_like(dg_ref)

        _, vjp = jax.vjp(_rms, x_ref[...], g_ref[...])
        dx, dg = vjp(dn_ref[...].astype(F32))
        dx_ref[...] = dr_ref[...] + dx
        dg_ref[0:1, :] += dg

    row = pl.BlockSpec((tr, d), lambda i: (i, 0))
    return pl.pallas_call(
        body, grid=(t // tr,),
        in_specs=[row, pl.BlockSpec((1, d), lambda i: (0, 0)), row, row],
        out_specs=[row, pl.BlockSpec((8, d), lambda i: (0, 0))],
        out_shape=[S((t, d), F32), S((8, d), F32)], name=name,
        compiler_params=_cp(("arbitrary",)))(xin, g, dn, dres)


def _head(h2, q, pe, tgt, gf, name):
    t, d = h2.shape
    tr = _pick(t, (256, 128))

    def f(h2v, qv, pev, gfv, tv):
        h3 = h2v + jax.nn.sigmoid(qv) * pev
        y = _rms(h3, gfv)
        return 0.5 * jnp.sum(jnp.mean(jnp.square(y - tv), axis=-1))

    def body(h2_ref, q_ref, pe_ref, t_ref, g_ref, loss_ref, dh_ref, dq_ref, dpe_ref, dg_ref):
        @pl.when(pl.program_id(0) == 0)
        def _():
            loss_ref[...] = jnp.zeros_like(loss_ref)
            dg_ref[...] = jnp.zeros_like(dg_ref)

        tv = t_ref[...]
        loss, vjp = jax.vjp(lambda a, b, c, e: f(a, b, c, e, tv), h2_ref[...], q_ref[...].astype(F32),
                            pe_ref[...].astype(F32), g_ref[...])
        dh, dq, dpe, dg = vjp(jnp.ones((), F32))
        loss_ref[...] += jnp.full(loss_ref.shape, loss, F32)
        dh_ref[...] = dh
        dq_ref[...] = dq.astype(dq_ref.dtype)
        dpe_ref[...] = dpe.astype(dpe_ref.dtype)
        dg_ref[0:1, :] += dg

    row = pl.BlockSpec((tr, d), lambda i: (i, 0))
    return pl.pallas_call(
        body, grid=(t // tr,),
        in_specs=[row, row, row, row, pl.BlockSpec((1, d), lambda i: (0, 0))],
        out_specs=[pl.BlockSpec((8, 128), lambda i: (0, 0)), row, row, row, pl.BlockSpec((8, d), lambda i: (0, 0))],
        out_shape=[S((8, 128), F32), S((t, d), F32), S((t, d), MXU_DTYPE), S((t, d), MXU_DTYPE), S((8, d), F32)],
        name=name, compiler_params=_cp(("arbitrary",)))(h2, q, pe, tgt, gf)


def _gmlp_block(us, vs, lng, lnb, wss, bss, ng):
    n = us[0].shape[0]
    row = lax.broadcasted_iota(jnp.int32, (n, n), 0)
    col = lax.broadcasted_iota(jnp.int32, (n, n), 1)
    outs = []
    for u0, v0, lg, lb, ws, bs in zip(us, vs, lng, lnb, wss, bss):
        u = jax.nn.gelu(u0)
        v = jax.nn.gelu(v0)
        mu = jnp.mean(v, axis=-1, keepdims=True)
        var = jnp.mean(jnp.square(v - mu), axis=-1, keepdims=True)
        vn = (v - mu) * lax.rsqrt(var + EPS) * lg + lb
        w = jnp.where(row >= col, ws, 0.0)
        sg = jnp.dot(_mx(w), _mx(vn), preferred_element_type=F32) + bs
        outs.append(u * sg)
    return _rms(jnp.concatenate(outs, axis=1), ng)


def _gmlp_load(proj_ref, lng_ref, lnb_ref, ws_ref, bst_ref, d_a, ng):
    sl = lambda g: slice(CHUNK * g, CHUNK * (g + 1))
    us = [proj_ref[:, sl(g)].astype(F32) for g in range(ng)]
    vs = [proj_ref[:, d_a + CHUNK * g: d_a + CHUNK * (g + 1)].astype(F32) for g in range(ng)]
    lng = [lng_ref[:, sl(g)] for g in range(ng)]
    lnb = [lnb_ref[:, sl(g)] for g in range(ng)]
    wss = [ws_ref[g] for g in range(ng)]
    bss = [bst_ref[:, g:g + 1] for g in range(ng)]
    return us, vs, lng, lnb, wss, bss


def _gmlp_fwd(proj, ln_g, ln_b, w_s, bst, norm_g, d_mix, name):
    t = proj.shape[0]
    ng = w_s.shape[0]
    d_a = ng * CHUNK

    def body(proj_ref, lng_ref, lnb_ref, ws_ref, bst_ref, ng_ref, o_ref):
        args = _gmlp_load(proj_ref, lng_ref, lnb_ref, ws_ref, bst_ref, d_a, ng)
        o_ref[...] = _gmlp_block(*args, ng_ref[...]).astype(o_ref.dtype)

    vec = pl.BlockSpec((1, d_a), lambda c: (0, 0))
    return pl.pallas_call(
        body, grid=(t // CHUNK,),
        in_specs=[pl.BlockSpec((CHUNK, 2 * d_a), lambda c: (c, 0)), vec, vec,
                  pl.BlockSpec((ng, CHUNK, CHUNK), lambda c: (0, 0, 0)), pl.BlockSpec((CHUNK, 128), lambda c: (0, 0)), vec],
        out_specs=pl.BlockSpec((CHUNK, d_a), lambda c: (c, 0)), out_shape=S((t, d_mix), ACT_DTYPE),
        name=name, compiler_params=_cp(("parallel",)))(proj, ln_g, ln_b, w_s, bst, norm_g)


def _gmlp_bwd(proj, ln_g, ln_b, w_s, bst, norm_g, dcat, d_proj, name):
    t = proj.shape[0]
    ng = w_s.shape[0]
    d_a = ng * CHUNK

    def body(proj_ref, lng_ref, lnb_ref, ws_ref, bst_ref, ng_ref, dy_ref,
             dp_ref, dlng_ref, dlnb_ref, dws_ref, dbst_ref, dng_ref):
        @pl.when(pl.program_id(0) == 0)
        def _():
            for r in (dlng_ref, dlnb_ref, dws_ref, dbst_ref, dng_ref):
                r[...] = jnp.zeros_like(r)

        args = _gmlp_load(proj_ref, lng_ref, lnb_ref, ws_ref, bst_ref, d_a, ng)
        _, vjp = jax.vjp(_gmlp_block, *args, ng_ref[...])
        dus, dvs, dlng, dlnb, dwss, dbss, dng = vjp(dy_ref[...].astype(F32))
        lane = lax.broadcasted_iota(jnp.int32, (1, 128), 1)
        dbst = jnp.zeros((CHUNK, 128), F32)
        for g in range(ng):
            dp_ref[:, CHUNK * g:CHUNK * (g + 1)] = dus[g].astype(dp_ref.dtype)
            dp_ref[:, d_a + CHUNK * g:d_a + CHUNK * (g + 1)] = dvs[g].astype(dp_ref.dtype)
            dlng_ref[0:1, CHUNK * g:CHUNK * (g + 1)] += dlng[g]
            dlnb_ref[0:1, CHUNK * g:CHUNK * (g + 1)] += dlnb[g]
            dws_ref[g] += dwss[g]
            dbst = dbst + dbss[g] * (lane == g).astype(F32)
        dbst_ref[...] += dbst
        dng_ref[0:1, :] += dng

    vec = pl.BlockSpec((1, d_a), lambda c: (0, 0))
    acc = pl.BlockSpec((8, d_a), lambda c: (0, 0))
    wspec = pl.BlockSpec((ng, CHUNK, CHUNK), lambda c: (0, 0, 0))
    bspec = pl.BlockSpec((CHUNK, 128), lambda c: (0, 0))
    return pl.pallas_call(
        body, grid=(t // CHUNK,),
        in_specs=[pl.BlockSpec((CHUNK, 2 * d_a), lambda c: (c, 0)), vec, vec, wspec, bspec, vec,
                  pl.BlockSpec((CHUNK, d_a), lambda c: (c, 0))],
        out_specs=[pl.BlockSpec((CHUNK, 2 * d_a), lambda c: (c, 0)), acc, acc, wspec, bspec, acc],
        out_shape=[S((t, d_proj), ACT_DTYPE), S((8, d_a), F32), S((8, d_a), F32), S((ng, CHUNK, CHUNK), F32),
                   S((CHUNK, 128), F32), S((8, d_a), F32)],
        name=name, compiler_params=_cp(("arbitrary",)))(proj, ln_g, ln_b, w_s, bst, norm_g, dcat)


def _silu_grad(c):
    s = jax.nn.sigmoid(c)
    return s * (1.0 + c * (1.0 - s))


def _fill_prev_main(s_ref, prev_ref, main_ref, i, tt):
    s_ref[pl.ds(0, HALO), :] = jnp.where(i > 0, prev_ref[...].astype(F32), 0.0)
    s_ref[pl.ds(HALO, tt), :] = main_ref[...].astype(F32)


def _taps(s_ref, w_ref, kw, rows):
    acc = w_ref[0:1, :] * s_ref[pl.ds(HALO - (kw - 1), rows), :]
    for k in range(1, kw):
        acc = acc + w_ref[k:k + 1, :] * s_ref[pl.ds(HALO - (kw - 1) + k, rows), :]
    return acc


def _prev_spec(tt, tc, joff):
    return pl.BlockSpec((HALO, tc), lambda j, i: (jnp.maximum(i * (tt // HALO) - 1, 0), j + joff))


def _next_spec(tt, tc, joff, t):
    return pl.BlockSpec((HALO, tc), lambda j, i: (jnp.minimum((i + 1) * (tt // HALO), t // HALO - 1), j + joff))


def _conv_ssm_fwd(proj, col0, w, b, name):
    t = proj.shape[0]
    kw, c = w.shape
    tt, tc = _pick(t, (512, 256, 128)), _pick(c, (512, 256, 128))
    joff = col0 // tc
    assert col0 % tc == 0

    def body(x_ref, xp_ref, w_ref, b_ref, o_ref, s_ref):
        _fill_prev_main(s_ref, xp_ref, x_ref, pl.program_id(1), tt)
        o_ref[...] = jax.nn.silu(_taps(s_ref, w_ref, kw, tt) + b_ref[...]).astype(o_ref.dtype)

    return pl.pallas_call(
        body, grid=(c // tc, t // tt),
        in_specs=[pl.BlockSpec((tt, tc), lambda j, i: (i, j + joff)), _prev_spec(tt, tc, joff),
                  pl.BlockSpec((kw, tc), lambda j, i: (0, j)), pl.BlockSpec((1, tc), lambda j, i: (0, j))],
        out_specs=pl.BlockSpec((tt, tc), lambda j, i: (i, j)), out_shape=S((t, c), XBC_DTYPE),
        scratch_shapes=[pltpu.VMEM((HALO + tt, tc), F32)], name=name,
        compiler_params=_cp(("parallel", "arbitrary")))(proj, proj, w, b)


def _conv_ssm_bwd(proj, col0, w, b, wcol0, dact, dproj, out_col0, name):
    t = proj.shape[0]
    kw = w.shape[0]
    c = dact.shape[1]
    tt, tc = _pick(t, (512, 256, 128)), _pick(c, (512, 256, 128))
    assert col0 % tc == 0 and wcol0 % tc == 0 and out_col0 % tc == 0
    joff, wj, oj = col0 // tc, wcol0 // tc, out_col0 // tc
    nt = t // tt

    def body(x_ref, xp_ref, xn_ref, w_ref, b_ref, d_ref, dn_ref, dp_in, dx_ref, dw_ref, db_ref, s_ref, sd_ref):
        del dp_in
        i = pl.program_id(1)

        @pl.when(i == 0)
        def _():
            dw_ref[...] = jnp.zeros_like(dw_ref)
            db_ref[...] = jnp.zeros_like(db_ref)

        _fill_prev_main(s_ref, xp_ref, x_ref, i, tt)
        s_ref[pl.ds(HALO + tt, HALO), :] = xn_ref[...].astype(F32)
        cpre = _taps(s_ref, w_ref, kw, tt + HALO) + b_ref[...]
        sd_ref[pl.ds(0, tt), :] = d_ref[...].astype(F32)
        sd_ref[pl.ds(tt, HALO), :] = jnp.where(i < nt - 1, dn_ref[...].astype(F32), 0.0)
        sd_ref[...] = sd_ref[...] * _silu_grad(cpre)
        dx = w_ref[kw - 1:kw, :] * sd_ref[pl.ds(0, tt), :]
        for k in range(kw - 1):
            dx = dx + w_ref[k:k + 1, :] * sd_ref[pl.ds(kw - 1 - k, tt), :]
        dx_ref[...] = dx.astype(dx_ref.dtype)
        dmain = sd_ref[pl.ds(0, tt), :]
        for k in range(kw):
            dw_ref[k:k + 1, :] += jnp.sum(dmain * s_ref[pl.ds(HALO - (kw - 1) + k, tt), :], axis=0, keepdims=True)
        db_ref[0:1, :] += jnp.sum(dmain, axis=0, keepdims=True)

    acc = pl.BlockSpec((8, tc), lambda j, i: (0, j))
    return pl.pallas_call(
        body, grid=(c // tc, nt),
        in_specs=[pl.BlockSpec((tt, tc), lambda j, i: (i, j + joff)), _prev_spec(tt, tc, joff), _next_spec(tt, tc, joff, t),
                  pl.BlockSpec((kw, tc), lambda j, i: (0, j + wj)), pl.BlockSpec((1, tc), lambda j, i: (0, j + wj)),
                  pl.BlockSpec((tt, tc), lambda j, i: (i, j)), _next_spec(tt, tc, 0, t),
                  pl.BlockSpec(memory_space=pl.ANY)],
        out_specs=[pl.BlockSpec((tt, tc), lambda j, i: (i, j + oj)), acc, acc],
        out_shape=[S(dproj.shape, dproj.dtype), S((8, c), F32), S((8, c), F32)],
        scratch_shapes=[pltpu.VMEM((HALO + tt + HALO, tc), F32), pltpu.VMEM((tt + HALO, tc), F32)],
        input_output_aliases={7: 0}, name=name,
        compiler_params=_cp(("parallel", "arbitrary")))(proj, proj, proj, w, b, dact, dact, dproj)


def _conv_ffn_fwd(hid, w, b, name):
    t, f2 = hid.shape
    f = f2 // 2
    kw = w.shape[0]
    tt, tc = _pick(t, (512, 256, 128)), _pick(f, (512, 256, 128))
    nj = f // tc

    def body(g_ref, gp_ref, u_ref, up_ref, wg_ref, wu_ref, bg_ref, bu_ref, o_ref, sg_ref, su_ref):
        i = pl.program_id(1)
        _fill_prev_main(sg_ref, gp_ref, g_ref, i, tt)
        _fill_prev_main(su_ref, up_ref, u_ref, i, tt)
        gate = _taps(sg_ref, wg_ref, kw, tt) + bg_ref[...]
        up = _taps(su_ref, wu_ref, kw, tt) + bu_ref[...]
        o_ref[...] = (jax.nn.silu(gate) * up).astype(o_ref.dtype)

    return pl.pallas_call(
        body, grid=(nj, t // tt),
        in_specs=[pl.BlockSpec((tt, tc), lambda j, i: (i, j)), _prev_spec(tt, tc, 0),
                  pl.BlockSpec((tt, tc), lambda j, i: (i, j + nj)), _prev_spec(tt, tc, nj),
                  pl.BlockSpec((kw, tc), lambda j, i: (0, j)), pl.BlockSpec((kw, tc), lambda j, i: (0, j + nj)),
                  pl.BlockSpec((1, tc), lambda j, i: (0, j)), pl.BlockSpec((1, tc), lambda j, i: (0, j + nj))],
        out_specs=pl.BlockSpec((tt, tc), lambda j, i: (i, j)), out_shape=S((t, f), ACT_DTYPE),
        scratch_shapes=[pltpu.VMEM((HALO + tt, tc), F32), pltpu.VMEM((HALO + tt, tc), F32)], name=name,
        compiler_params=_cp(("parallel", "arbitrary")))(hid, hid, hid, hid, w, w, b, b)


def _conv_ffn_bwd(hid, w, b, dact, name):
    t, f2 = hid.shape
    f = f2 // 2
    kw = w.shape[0]
    tt, tc = _pick(t, (512, 256, 128)), _pick(f, (512, 256, 128))
    nj = f // tc
    nt = t // tt

    def body(g_ref, gp_ref, gn_ref, u_ref, up_ref, un_ref, wg_ref, wu_ref, bg_ref, bu_ref, d_ref, dn_ref,
             dh_ref, dw_ref, db_ref, sg_ref, su_ref, dg_ref, du_ref):
        i = pl.program_id(1)

        @pl.when(i == 0)
        def _():
            dw_ref[...] = jnp.zeros_like(dw_ref)
            db_ref[...] = jnp.zeros_like(db_ref)

        _fill_prev_main(sg_ref, gp_ref, g_ref, i, tt)
        sg_ref[pl.ds(HALO + tt, HALO), :] = gn_ref[...].astype(F32)
        _fill_prev_main(su_ref, up_ref, u_ref, i, tt)
        su_ref[pl.ds(HALO + tt, HALO), :] = un_ref[...].astype(F32)
        gate = _taps(sg_ref, wg_ref, kw, tt + HALO) + bg_ref[...]
        up = _taps(su_ref, wu_ref, kw, tt + HALO) + bu_ref[...]
        dg_ref[pl.ds(0, tt), :] = d_ref[...].astype(F32)
        dg_ref[pl.ds(tt, HALO), :] = jnp.where(i < nt - 1, dn_ref[...].astype(F32), 0.0)
        dact_e = dg_ref[...]
        du_ref[...] = dact_e * jax.nn.silu(gate)
        dg_ref[...] = dact_e * up * _silu_grad(gate)
        for s, (sd_ref, sx_ref, w_ref) in enumerate(((dg_ref, sg_ref, wg_ref), (du_ref, su_ref, wu_ref))):
            dx = w_ref[kw - 1:kw, :] * sd_ref[pl.ds(0, tt), :]
            for k in range(kw - 1):
                dx = dx + w_ref[k:k + 1, :] * sd_ref[pl.ds(kw - 1 - k, tt), :]
            dh_ref[s] = dx.astype(dh_ref.dtype)
            dmain = sd_ref[pl.ds(0, tt), :]
            for k in range(kw):
                dw_ref[s, k:k + 1, :] += jnp.sum(dmain * sx_ref[pl.ds(HALO - (kw - 1) + k, tt), :], axis=0, keepdims=True)
            db_ref[s, 0:1, :] += jnp.sum(dmain, axis=0, keepdims=True)

    acc = pl.BlockSpec((2, 8, tc), lambda j, i: (0, 0, j))
    ext = pltpu.VMEM((HALO + tt + HALO, tc), F32)
    dsc = pltpu.VMEM((tt + HALO, tc), F32)
    return pl.pallas_call(
        body, grid=(nj, nt),
        in_specs=[pl.BlockSpec((tt, tc), lambda j, i: (i, j)), _prev_spec(tt, tc, 0), _next_spec(tt, tc, 0, t),
                  pl.BlockSpec((tt, tc), lambda j, i: (i, j + nj)), _prev_spec(tt, tc, nj), _next_spec(tt, tc, nj, t),
                  pl.BlockSpec((kw, tc), lambda j, i: (0, j)), pl.BlockSpec((kw, tc), lambda j, i: (0, j + nj)),
                  pl.BlockSpec((1, tc), lambda j, i: (0, j)), pl.BlockSpec((1, tc), lambda j, i: (0, j + nj)),
                  pl.BlockSpec((tt, tc), lambda j, i: (i, j)), _next_spec(tt, tc, 0, t)],
        out_specs=[pl.BlockSpec((2, tt, tc), lambda j, i: (0, i, j)), acc, acc],
        out_shape=[S((2, t, f), MXU_DTYPE), S((2, 8, f), F32), S((2, 8, f), F32)],
        scratch_shapes=[ext, ext, dsc, dsc], name=name,
        compiler_params=_cp(("parallel", "arbitrary")))(hid, hid, hid, hid, hid, hid, w, w, b, b, dact, dact)


def _ssd_chunk(xs, bm, cm, dtraw, z, hins, bias, alog, dskips, normg):
    n = bm.shape[0]
    row = lax.broadcasted_iota(jnp.int32, (n, n), 0)
    col = lax.broadcasted_iota(jnp.int32, (n, n), 1)
    causal = row >= col
    tri = causal.astype(F32)
    lane = lax.broadcasted_iota(jnp.int32, (1, 128), 1)
    sub = lax.broadcasted_iota(jnp.int32, (128, 1), 0)
    last = (lax.broadcasted_iota(jnp.int32, (n, 1), 0) == n - 1).astype(F32)
    dt = jax.nn.softplus(dtraw + bias)
    adt = dt * (-jnp.exp(alog))
    acs = jnp.dot(tri, adt, preferred_element_type=F32, precision=lax.Precision.HIGHEST)
    acs_t = acs.T
    scores = lax.dot_general(_mx(cm), _mx(bm), (((1,), (1,)), ((), ())), preferred_element_type=F32)
    ys, houts = [], []
    for r in range(HEADS_PER_GROUP):
        pick = (lane == r).astype(F32)
        acol = jnp.sum(acs * pick, axis=1, keepdims=True)
        arow = jnp.sum(acs_t * (sub == r).astype(F32), axis=0, keepdims=True)
        dtc = jnp.sum(dt * pick, axis=1, keepdims=True)
        alast = jnp.sum(acol * last, axis=0, keepdims=True)
        lm = jnp.exp(jnp.where(causal, acol - arow, -1e30))
        xt = xs[r] * dtc
        yd = jnp.dot(_mx(scores * lm), _mx(xt), preferred_element_type=F32)
        yo = jnp.exp(acol) * lax.dot_general(_mx(cm), _mx(hins[r]), (((1,), (1,)), ((), ())), preferred_element_type=F32)
        ys.append(yd + yo + dskips[r] * xs[r])
        st = lax.dot_general(_mx(xt * jnp.exp(alast - acol)), _mx(bm), (((0,), (0,)), ((), ())), preferred_element_type=F32)
        houts.append(jnp.exp(alast) * hins[r] + st)
    y = jnp.concatenate(ys, axis=1) * jax.nn.silu(z)
    return _rms(y, normg), houts


def _ssd_specs(d_ssm, ngrp, zcol0, rev, nc):
    cc = (lambda c: nc - 1 - c) if rev else (lambda c: c)
    xj, bj, cj, zj = 0, d_ssm // 128, d_ssm // 128 + ngrp, zcol0 // GROUP_CH
    const = lambda w: pl.BlockSpec((None, 8, w), lambda g, c: (g, 0, 0))
    return cc, [
        pl.BlockSpec((CHUNK, GROUP_CH), lambda g, c: (cc(c), xj + g)),
        pl.BlockSpec((CHUNK, 128), lambda g, c: (cc(c), bj + g)),
        pl.BlockSpec((CHUNK, 128), lambda g, c: (cc(c), cj + g)),
        pl.BlockSpec((CHUNK, 128), lambda g, c: (cc(c), g)),
        pl.BlockSpec((CHUNK, GROUP_CH), lambda g, c: (cc(c), zj + g)),
        const(128), const(128), const(GROUP_CH), const(GROUP_CH)]


def _ssd_load(x_ref, b_ref, c_ref, dt_ref, z_ref, bias_ref, alog_ref, dsk_ref, ng_ref):
    xs = [x_ref[:, HEAD_DIM * r:HEAD_DIM * (r + 1)].astype(F32) for r in range(HEADS_PER_GROUP)]
    dskips = [dsk_ref[0:1, HEAD_DIM * r:HEAD_DIM * (r + 1)] for r in range(HEADS_PER_GROUP)]
    return (xs, b_ref[...].astype(F32), c_ref[...].astype(F32), dt_ref[...], z_ref[...].astype(F32)), \
           (bias_ref[0:1, :], alog_ref[0:1, :], dskips, ng_ref[0:1, :])


def _ssd_fwd(xbc, dtp, proj, zcol0, bias_p, alog_p, dskip_x, normg_x, cat, name):
    t = xbc.shape[0]
    ngrp = bias_p.shape[0]
    d_ssm = ngrp * GROUP_CH
    nc = t // CHUNK
    d_a = cat.shape[1] - d_ssm
    assert d_a % GROUP_CH == 0 and zcol0 % GROUP_CH == 0
    _, specs = _ssd_specs(d_ssm, ngrp, zcol0, False, nc)

    def body(x_ref, b_ref, c_ref, dt_ref, z_ref, bias_ref, alog_ref, dsk_ref, ng_ref, cat_in, y_ref, hs_ref, h_ref):
        del cat_in

        @pl.when(pl.program_id(1) == 0)
        def _():
            h_ref[...] = jnp.zeros_like(h_ref)

        data, consts = _ssd_load(x_ref, b_ref, c_ref, dt_ref, z_ref, bias_ref, alog_ref, dsk_ref, ng_ref)
        hins = [h_ref[r] for r in range(HEADS_PER_GROUP)]
        hs_ref[...] = h_ref[...]
        y, houts = _ssd_chunk(*data, hins, *consts)
        y_ref[...] = y.astype(y_ref.dtype)
        for r in range(HEADS_PER_GROUP):
            h_ref[r] = houts[r]

    return pl.pallas_call(
        body, grid=(ngrp, nc), in_specs=specs + [pl.BlockSpec(memory_space=pl.ANY)],
        out_specs=[pl.BlockSpec((CHUNK, GROUP_CH), lambda g, c: (c, d_a // GROUP_CH + g)),
                   pl.BlockSpec((None, HEADS_PER_GROUP, HEAD_DIM, D_STATE), lambda g, c: (c, g, 0, 0))],
        out_shape=[S(cat.shape, cat.dtype), S((nc, ngrp * HEADS_PER_GROUP, HEAD_DIM, D_STATE), F32)],
        scratch_shapes=[pltpu.VMEM((HEADS_PER_GROUP, HEAD_DIM, D_STATE), F32)],
        input_output_aliases={9: 0}, name=name,
        compiler_params=_cp(("parallel", "arbitrary")))(xbc, xbc, xbc, dtp, proj, bias_p, alog_p, dskip_x, normg_x, cat)


def _ssd_bwd(xbc, dtp, proj, zcol0, bias_p, alog_p, dskip_x, normg_x, hs, dcat, dproj, name):
    t = xbc.shape[0]
    ngrp = bias_p.shape[0]
    d_ssm = ngrp * GROUP_CH
    nc = t // CHUNK
    d_a = dcat.shape[1] - d_ssm
    cc, specs = _ssd_specs(d_ssm, ngrp, zcol0, True, nc)
    hp = HEADS_PER_GROUP

    def body(x_ref, b_ref, c_ref, dt_ref, z_ref, bias_ref, alog_ref, dsk_ref, ng_ref, hs_ref, dy_ref, dp_in,
             dz_ref, dx_ref, db_ref, dc_ref, ddt_ref, dbias_ref, dalog_ref, ddsk_ref, dng_ref, dh_ref):
        del dp_in

        @pl.when(pl.program_id(1) == 0)
        def _():
            dh_ref[...] = jnp.zeros_like(dh_ref)
            for r in (dbias_ref, dalog_ref, ddsk_ref, dng_ref):
                r[...] = jnp.zeros_like(r)

        data, consts = _ssd_load(x_ref, b_ref, c_ref, dt_ref, z_ref, bias_ref, alog_ref, dsk_ref, ng_ref)
        hins = [hs_ref[r] for r in range(hp)]
        _, vjp = jax.vjp(_ssd_chunk, *data, hins, *consts)
        (dxs, dbm, dcm, ddt, dz, dhins, dbias, dalog, ddsk, dng) = vjp(
            (dy_ref[...].astype(F32), [dh_ref[r] for r in range(hp)]))
        for r in range(hp):
            dx_ref[:, HEAD_DIM * r:HEAD_DIM * (r + 1)] = dxs[r].astype(dx_ref.dtype)
            dh_ref[r] = dhins[r]
            ddsk_ref[0:1, HEAD_DIM * r:HEAD_DIM * (r + 1)] += ddsk[r]
        db_ref[...] = dbm.astype(db_ref.dtype)
        dc_ref[...] = dcm.astype(dc_ref.dtype)
        ddt_ref[...] = ddt
        dz_ref[...] = dz.astype(dz_ref.dtype)
        dbias_ref[0:1, :] += dbias
        dalog_ref[0:1, :] += dalog
        dng_ref[0:1, :] += dng

    acc = lambda w: pl.BlockSpec((None, 8, w), lambda g, c: (g, 0, 0))
    blk = lambda w: pl.BlockSpec((CHUNK, w), lambda g, c: (cc(c), g))
    return pl.pallas_call(
        body, grid=(ngrp, nc),
        in_specs=specs + [pl.BlockSpec((None, hp, HEAD_DIM, D_STATE), lambda g, c: (cc(c), g, 0, 0)),
                          pl.BlockSpec((CHUNK, GROUP_CH), lambda g, c: (cc(c), d_a // GROUP_CH + g)),
                          pl.BlockSpec(memory_space=pl.ANY)],
        out_specs=[pl.BlockSpec((CHUNK, GROUP_CH), lambda g, c: (cc(c), zcol0 // GROUP_CH + g)),
                   blk(GROUP_CH), blk(128), blk(128), blk(128), acc(128), acc(128), acc(GROUP_CH), acc(GROUP_CH)],
        out_shape=[S(dproj.shape, dproj.dtype), S((t, d_ssm), XBC_DTYPE), S((t, ngrp * 128), XBC_DTYPE),
                   S((t, ngrp * 128), XBC_DTYPE), S((t, ngrp * 128), F32), S((ngrp, 8, 128), F32),
                   S((ngrp, 8, 128), F32), S((ngrp, 8, GROUP_CH), F32), S((ngrp, 8, GROUP_CH), F32)],
        scratch_shapes=[pltpu.VMEM((hp, HEAD_DIM, D_STATE), F32)],
        input_output_aliases={11: 0}, name=name,
        compiler_params=_cp(("parallel", "arbitrary")))(xbc, xbc, xbc, dtp, proj, bias_p, alog_p, dskip_x, normg_x, hs, dcat, dproj)


def _adamw(parts, w, m, v, name):
    r, c = w.shape
    tr = _pick(r, (256, 128, 64, 32, 16, 8)) if c * 4 * 256 <= 4 * 1024 * 1024 else _pick(r, (64, 32, 16, 8))

    def body(p_ref, w_ref, m_ref, v_ref, g_ref, d_ref, nm_ref, nv_ref):
        g = p_ref[0].astype(F32)
        for k in range(1, N_DEV):
            g = g + p_ref[k].astype(F32)
        mm = ADAM_B1 * m_ref[...] + (1.0 - ADAM_B1) * g
        vv = ADAM_B2 * v_ref[...] + (1.0 - ADAM_B2) * jnp.square(g)
        m_hat = mm / (1.0 - ADAM_B1 ** ADAM_STEP)
        v_hat = vv / (1.0 - ADAM_B2 ** ADAM_STEP)
        g_ref[...] = g
        d_ref[...] = -ADAM_LR * (m_hat / (jnp.sqrt(v_hat) + ADAM_EPS) + ADAM_WD * w_ref[...])
        nm_ref[...] = mm
        nv_ref[...] = vv

    blk = pl.BlockSpec((tr, c), lambda i: (i, 0))
    return pl.pallas_call(
        body, grid=(r // tr,),
        in_specs=[pl.BlockSpec((N_DEV, tr, c), lambda i: (0, i, 0)), blk, blk, blk],
        out_specs=[blk, blk, blk, blk], out_shape=[S((r, c), F32)] * 4, name=name,
        compiler_params=_cp(("parallel",)))(parts, w, m, v)


def _exchange(items, name):
    n = len(items)
    kinds = [k for k, _ in items]
    arrs = [a for _, a in items]
    outs = [S((N_DEV,) + a.shape, a.dtype) if k == 'gather' else S(a.shape, a.dtype) for k, a in items]

    def body(*refs):
        srcs, dsts = refs[:n], refs[n:2 * n]
        send_sems, recv_sems, loc_sems = refs[2 * n:]
        x, y, c = lax.axis_index("x"), lax.axis_index("y"), lax.axis_index("c")
        me = 4 * x + 2 * y + c
        copies = []
        for a in range(n):
            for d in range(1, N_DEV):
                px = 1 - x if (d >> 2) & 1 else x
                py = 1 - y if (d >> 1) & 1 else y
                pc = 1 - c if d & 1 else c
                src = srcs[a] if kinds[a] == 'gather' else srcs[a].at[4 * px + 2 * py + pc]
                cp = pltpu.make_async_remote_copy(
                    src_ref=src, dst_ref=dsts[a].at[me], send_sem=send_sems.at[a, d - 1], recv_sem=recv_sems.at[a, d - 1],
                    device_id=(px, py, pc), device_id_type=pl.DeviceIdType.MESH)
                cp.start()
                copies.append(cp)
            src = srcs[a] if kinds[a] == 'gather' else srcs[a].at[me]
            cp = pltpu.make_async_copy(src, dsts[a].at[me], loc_sems.at[a])
            cp.start()
            copies.append(cp)
        for cp in copies:
            cp.wait()

    hbm = pl.BlockSpec(memory_space=pl.ANY)
    return pl.pallas_call(
        body, in_specs=[hbm] * n, out_specs=[hbm] * n, out_shape=outs,
        scratch_shapes=[pltpu.SemaphoreType.DMA((n, N_DEV - 1)), pltpu.SemaphoreType.DMA((n, N_DEV - 1)),
                        pltpu.SemaphoreType.DMA((n,))],
        name=name, compiler_params=pltpu.CompilerParams(has_side_effects=True))(*arrs)


def _stack_to_full(kind, st):
    if kind == 'row':
        return st.reshape(st.shape[0] * st.shape[1], st.shape[2])
    return jnp.transpose(st, (1, 0, 2)).reshape(st.shape[1], st.shape[0] * st.shape[2])


def _full_to_stack(kind, full):
    r, c = full.shape
    if kind == 'row':
        return full.reshape(N_DEV, r // N_DEV, c)
    return jnp.transpose(full.reshape(r, N_DEV, c // N_DEV), (1, 0, 2))


def _pack_small(named):
    rows, layout = [], []
    for a in named:
        flat = a.reshape(-1).astype(F32)
        n = flat.shape[0]
        nr = -(-n // 128)
        rows.append(jnp.pad(flat, (0, nr * 128 - n)).reshape(nr, 128))
        layout.append((a.shape, n, nr))
    tot = sum(l[2] for l in layout)
    pad = -tot % 8
    if pad:
        rows.append(jnp.zeros((pad, 128), F32))
    return jnp.concatenate(rows, axis=0), layout


def _unpack_small(packed, layout):
    out, r0 = [], 0
    for shape, n, nr in layout:
        out.append(packed[r0:r0 + nr].reshape(-1)[:n].reshape(shape))
        r0 += nr
    return out


def _row0(acc):
    return acc[0]


def _local_step(x, p, tgt, wf, sm):
    t, d = x.shape
    h_n = sm['dt_bias'].shape[-1]
    ngrp = h_n // HEADS_PER_GROUP
    d_ssm = h_n * HEAD_DIM
    d_a = sm['ln_a_g'].shape[-1]
    d_mix = d_a + d_ssm
    d_xbc = sm['conv_ssm_b'].shape[-1]
    d_main = 2 * d_a + d_ssm + d_xbc
    assert d_xbc == d_ssm + 2 * ngrp * D_STATE and h_n <= 128
    zcol0, xcol0 = 2 * d_a, 2 * d_a + d_ssm
    vec = lambda v: v.reshape(1, -1)

    w_main = wf['w_in'][:, :d_main]
    w_dt = jnp.pad(wf['w_in'][:, d_main:], ((0, 0), (0, 128 - h_n)))
    bst = jnp.pad(sm['b_s'].T, ((0, 0), (0, 128 - sm['b_s'].shape[0])))
    grp = lambda v, w: jnp.broadcast_to(jnp.pad(v.reshape(ngrp, 1, -1), ((0, 0), (0, 0), (0, w - v.size // ngrp))), (ngrp, 8, w))
    bias_p, alog_p = grp(sm['dt_bias'], 128), grp(sm['a_log'], 128)
    dskip_x = grp(jnp.repeat(sm['d_skip'], HEAD_DIM), GROUP_CH)
    normg_x = grp(sm['ssm_norm_g'], GROUP_CH)
    pad_dt = lambda v: jnp.pad(v[:, :h_n].reshape(t, ngrp, HEADS_PER_GROUP),
                               ((0, 0), (0, 0), (0, 128 - HEADS_PER_GROUP))).reshape(t, ngrp * 128)

    a_n = _rms_fwd(x, vec(sm['norm_mix_g']), "rms_mix")
    proj = _mm_nn(a_n, w_main, out_dtype=ACT_DTYPE, name="mm_in")
    dtp = pad_dt(_mm_nn(a_n, w_dt, out_dtype=F32, name="mm_dt"))
    cat = _gmlp_fwd(proj, vec(sm['ln_a_g']), vec(sm['ln_a_b']), sm['w_s'], bst, vec(sm['norm_a_g']), d_mix, "gmlp_fwd")
    xbc = _conv_ssm_fwd(proj, xcol0, wf['conv_ssm_w'], vec(sm['conv_ssm_b']), "conv_ssm_fwd")
    cat, hs = _ssd_fwd(xbc, dtp, proj, zcol0, bias_p, alog_p, dskip_x, normg_x, cat, "ssd_fwd")
    h1 = _mm_nn(cat, wf['w_out'], out_dtype=F32, name="mm_out", res=x)
    f_n = _rms_fwd(h1, vec(sm['norm_ffn_g']), "rms_ffn")
    hid = _mm_nn(f_n, wf['w_up'], out_dtype=ACT_DTYPE, name="mm_up")
    act = _conv_ffn_fwd(hid, wf['conv_ffn_w'], vec(sm['conv_ffn_b']), "conv_ffn_fwd")
    h2 = _mm_nn(act, wf['w_down'], out_dtype=F32, name="mm_down", res=h1)
    r_n = _rms_fwd(h2, vec(sm['norm_ple_g']), "rms_ple")
    q = _mm_nn(r_n, wf['w_ple_gate'], out_dtype=F32, name="mm_pg")
    p_m = p.astype(MXU_DTYPE)
    pe = _mm_nn(p_m, wf['w_ple'], out_dtype=F32, name="mm_ple")

    loss, dh3, dq, dpe, dgf = _head(h2, q, pe, tgt, vec(sm['norm_final_g']), "head")
    gw, gs = {}, {}
    gs['norm_final_g'] = _row0(dgf)
    gw['w_ple'] = _mm_tn(p_m, dpe, out_dtype=WIRE_DTYPE, name="wg_ple")
    gw['w_ple_gate'] = _mm_tn(r_n, dq, out_dtype=WIRE_DTYPE, name="wg_pg")
    dr = _mm_nt(dq, wf['w_ple_gate'], out_dtype=F32, name="dg_pg")
    dh2, dg = _rms_bwd(h2, vec(sm['norm_ple_g']), dr, dh3, "rms_ple_bwd")
    gs['norm_ple_g'] = _row0(dg)
    dh2m = dh2.astype(MXU_DTYPE)
    gw['w_down'] = _mm_tn(act, dh2m, out_dtype=WIRE_DTYPE, name="wg_down")
    dact = _mm_nt(dh2m, wf['w_down'], out_dtype=ACT_DTYPE, name="dg_down")
    dhid, dcw, dcb = _conv_ffn_bwd(hid, wf['conv_ffn_w'], vec(sm['conv_ffn_b']), dact, "conv_ffn_bwd")
    kf = wf['conv_ffn_w'].shape[0]
    gw['conv_ffn_w'] = jnp.concatenate([dcw[0, :kf], dcw[1, :kf]], axis=1)
    gs['conv_ffn_b'] = jnp.concatenate([dcb[0, 0], dcb[1, 0]], axis=0)
    gw['w_up'] = _mm_tn(f_n, dhid, out_dtype=WIRE_DTYPE, name="wg_up", b_split=2)
    df = _mm_nt(dhid, wf['w_up'], out_dtype=F32, name="dg_up", a_split=2)
    dh1, dg = _rms_bwd(h1, vec(sm['norm_ffn_g']), df, dh2, "rms_ffn_bwd")
    gs['norm_ffn_g'] = _row0(dg)
    dh1m = dh1.astype(MXU_DTYPE)
    gw['w_out'] = _mm_tn(cat, dh1m, out_dtype=WIRE_DTYPE, name="wg_out")
    dcat = _mm_nt(dh1m, wf['w_out'], out_dtype=ACT_DTYPE, name="dg_out")
    dproj, dlng, dlnb, dws, dbst, dng = _gmlp_bwd(proj, vec(sm['ln_a_g']), vec(sm['ln_a_b']), sm['w_s'], bst,
                                                  vec(sm['norm_a_g']), dcat, d_main, "gmlp_bwd")
    gs['ln_a_g'], gs['ln_a_b'], gs['w_s'], gs['norm_a_g'] = _row0(dlng), _row0(dlnb), dws, _row0(dng)
    gs['b_s'] = dbst[:, :sm['b_s'].shape[0]].T
    dproj, dxs, dbm, dcm, ddtp, dbias, dalog, ddsk, dsng = _ssd_bwd(
        xbc, dtp, proj, zcol0, bias_p, alog_p, dskip_x, normg_x, hs, dcat, dproj, "ssd_bwd")
    gs['dt_bias'] = dbias[:, 0, :HEADS_PER_GROUP].reshape(h_n)
    gs['a_log'] = dalog[:, 0, :HEADS_PER_GROUP].reshape(h_n)
    gs['d_skip'] = ddsk[:, 0, :].reshape(h_n, HEAD_DIM).sum(axis=-1)
    gs['ssm_norm_g'] = dsng[:, 0, :].reshape(d_ssm)
    dws_c, dbs_c = [], []
    off = 0
    for nm, dpart in (("x", dxs), ("b", dbm), ("c", dcm)):
        dproj, dw_c, db_c = _conv_ssm_bwd(proj, xcol0 + off, wf['conv_ssm_w'], vec(sm['conv_ssm_b']), off, dpart, dproj,
                                          xcol0 + off, "conv_ssm_bwd_" + nm)
        dws_c.append(dw_c[:wf['conv_ssm_w'].shape[0]])
        dbs_c.append(db_c[0])
        off += dpart.shape[1]
    gw['conv_ssm_w'] = jnp.concatenate(dws_c, axis=1)
    gs['conv_ssm_b'] = jnp.concatenate(dbs_c, axis=0)
    ddt = jnp.pad(ddtp.reshape(t, ngrp, 128)[:, :, :HEADS_PER_GROUP].reshape(t, h_n), ((0, 0), (0, 128 - h_n))).astype(MXU_DTYPE)
    gw_main = _mm_tn(a_n, dproj, out_dtype=WIRE_DTYPE, name="wg_in")
    gw_dt = _mm_tn(a_n, ddt, out_dtype=WIRE_DTYPE, name="wg_dt")
    gw['w_in'] = jnp.concatenate([gw_main, gw_dt[:, :h_n]], axis=1)
    da = _mm_nt(ddt, w_dt, out_dtype=F32, name="dg_dt")
    da = _mm_nt(dproj, w_main, out_dtype=F32, name="dg_in", res=da)
    dx, dg = _rms_bwd(x, vec(sm['norm_mix_g']), da, dh1, "rms_mix_bwd")
    gs['norm_mix_g'] = _row0(dg)
    return loss, dx, gw, gs


def kernel(x, p, norm_mix_g, w_in, ln_a_g, ln_a_b, w_s, b_s, norm_a_g, conv_ssm_w, conv_ssm_b, dt_bias, a_log, d_skip, ssm_norm_g, w_out, norm_ffn_g, w_up, conv_ffn_w, conv_ffn_b, w_down, norm_ple_g, w_ple_gate, w_ple, norm_final_g, loss_target, m_norm_mix_g, m_w_in, m_ln_a_g, m_ln_a_b, m_w_s, m_b_s, m_norm_a_g, m_conv_ssm_w, m_conv_ssm_b, m_dt_bias, m_a_log, m_d_skip, m_ssm_norm_g, m_w_out, m_norm_ffn_g, m_w_up, m_conv_ffn_w, m_conv_ffn_b, m_w_down, m_norm_ple_g, m_w_ple_gate, m_w_ple, m_norm_final_g, v_norm_mix_g, v_w_in, v_ln_a_g, v_ln_a_b, v_w_s, v_b_s, v_norm_a_g, v_conv_ssm_w, v_conv_ssm_b, v_dt_bias, v_a_log, v_d_skip, v_ssm_norm_g, v_w_out, v_norm_ffn_g, v_w_up, v_conv_ffn_w, v_conv_ffn_b, v_w_down, v_norm_ple_g, v_w_ple_gate, v_w_ple, v_norm_final_g):
    given = dict(locals())
    wts = {n: given[n] for n in WEIGHTS}
    ms = {n: given["m_" + n] for n in WEIGHTS}
    vs = {n: given["v_" + n] for n in WEIGHTS}
    sh_names = list(SHARDED)
    wire = lambda n: F32 if n in F32_ON_WIRE else WIRE_DTYPE

    blocks = [wts[n][0].astype(wire(n)) for n in sh_names]
    stacks = _exchange([('gather', b) for b in blocks], "gather_weights")
    wf = {n: _stack_to_full(SHARDED[n], st) for n, st in zip(sh_names, stacks)}
    sm = {n: (wts[n][0] if wts[n].ndim > 1 else wts[n]) for n in REPLICATED}

    loss, dx, gw, gs = _local_step(x[0], p[0, 0], loss_target[0], wf, sm)

    small, layout = _pack_small([gs[n] for n in REPLICATED] + [loss[0, 0:1]])
    sends = [('scatter', _full_to_stack(SHARDED[n], gw[n].astype(wire(n)))) for n in sh_names]
    recv = _exchange(sends + [('gather', small)], "exchange_grads")

    out = {}
    for n, parts in zip(sh_names, recv[:-1]):
        out[n] = _adamw(parts, wts[n][0], ms[n][0], vs[n][0], "adamw_" + n)
    wsm, _ = _pack_small([wts[n] for n in REPLICATED] + [jnp.zeros((1,), F32)])
    msm, _ = _pack_small([ms[n] for n in REPLICATED] + [jnp.zeros((1,), F32)])
    vsm, _ = _pack_small([vs[n] for n in REPLICATED] + [jnp.ones((1,), F32)])
    res_small = [_unpack_small(o, layout) for o in _adamw(recv[-1], wsm, msm, vsm, "adamw_small")]
    loss_out = res_small[0][-1].reshape(())
    for i, n in enumerate(REPLICATED):
        out[n] = [res_small[k][i].reshape(wts[n].shape) for k in range(4)]
    for n in sh_names:
        out[n] = [o.reshape(wts[n].shape) for o in out[n]]
    return (loss_out, dx[None], *[out[n][0] for n in WEIGHTS], *[out[n][1] for n in WEIGHTS],
            *[out[n][2] for n in WEIGHTS], *[out[n][3] for n in WEIGHTS])
```

```python
import functools

import jax
import jax.numpy as jnp
from jax import lax
from jax.experimental import pallas as pl
from jax.experimental.pallas import tpu as pltpu

F32 = jnp.float32
MXU_DTYPE = jnp.bfloat16
ACT_DTYPE = jnp.bfloat16
XBC_DTYPE = jnp.float32
WIRE_DTYPE = jnp.bfloat16
EPS = 1e-6
CHUNK = 128
D_STATE = 128
HEAD_DIM = 64
HEADS_PER_GROUP = 4
GROUP_CH = HEAD_DIM * HEADS_PER_GROUP
HALO = 16
N_DEV = 8
VMEM_LIMIT = 48 * 1024 * 1024

ADAM_LR = 0.001
ADAM_B1 = 0.9
ADAM_B2 = 0.999
ADAM_EPS = 1e-08
ADAM_WD = 0.01
ADAM_STEP = 10

WEIGHTS = ['norm_mix_g', 'w_in', 'ln_a_g', 'ln_a_b', 'w_s', 'b_s', 'norm_a_g', 'conv_ssm_w', 'conv_ssm_b', 'dt_bias',
           'a_log', 'd_skip', 'ssm_norm_g', 'w_out', 'norm_ffn_g', 'w_up', 'conv_ffn_w', 'conv_ffn_b', 'w_down',
           'norm_ple_g', 'w_ple_gate', 'w_ple', 'norm_final_g']
SHARDED = {'w_in': 'col', 'conv_ssm_w': 'col', 'w_out': 'row', 'w_up': 'col', 'conv_ffn_w': 'col', 'w_down': 'row',
           'w_ple_gate': 'row', 'w_ple': 'col'}
F32_ON_WIRE = ('conv_ssm_w', 'conv_ffn_w')
REPLICATED = [n for n in WEIGHTS if n not in SHARDED]

S = jax.ShapeDtypeStruct


def _pick(dim, cands):
    for c in cands:
        if c <= dim and dim % c == 0:
            return c
    return dim


def _cp(sem, vmem=VMEM_LIMIT):
    return pltpu.CompilerParams(dimension_semantics=sem, vmem_limit_bytes=vmem)


def _mx(v):
    return v.astype(MXU_DTYPE)


def _rms(v, g):
    return v * lax.rsqrt(jnp.mean(v * v, axis=-1, keepdims=True) + EPS) * g


MM_VMEM_BUDGET = 34 * 1024 * 1024


def _mm_tiles(m, n, k, out_bytes, has_res, tn_cands=(512, 256, 128), k_mult=1):
    tm = _pick(m, (1024, 512, 256, 128))
    tn = _pick(n, tn_cands)
    ks = k // k_mult
    for nk in range(1, ks // 128 + 1):
        if ks % nk or (ks // nk) % 128:
            continue
        tk = ks // nk
        need = 2 * 2 * (tm * tk + tk * tn) + tm * tn * (4 + 2 * out_bytes + (8 if has_res else 0))
        if need <= MM_VMEM_BUDGET:
            return tm, tn, tk
    return tm, tn, _pick(ks, (128,))


def _mm_body(dot, nk, has_res):
    def body(*refs):
        if has_res:
            a_ref, b_ref, r_ref, o_ref, acc_ref = refs
        else:
            a_ref, b_ref, o_ref, acc_ref = refs
            r_ref = None
        kk = pl.program_id(2)
        d = dot(a_ref[...], b_ref[...])

        def fin(acc):
            if r_ref is not None:
                acc = acc + r_ref[...]
            o_ref[...] = acc.astype(o_ref.dtype)

        if nk == 1:
            fin(d)
        else:
            @pl.when(kk == 0)
            def _():
                acc_ref[...] = d

            @pl.when(kk > 0)
            def _():
                acc_ref[...] += d

            @pl.when(kk == nk - 1)
            def _():
                fin(acc_ref[...])

    return body


def _mm_call(body, grid, a_spec, b_spec, tm, tn, m, n, out_dtype, name, args, res):
    in_specs = [a_spec, b_spec]
    if res is not None:
        in_specs.append(pl.BlockSpec((tm, tn), lambda i, j, kk: (i, j)))
        args = args + [res]
    return pl.pallas_call(
        body, grid=grid, in_specs=in_specs,
        out_specs=pl.BlockSpec((tm, tn), lambda i, j, kk: (i, j)), out_shape=S((m, n), out_dtype),
        scratch_shapes=[pltpu.VMEM((tm, tn), F32)], name=name,
        compiler_params=_cp(("parallel", "parallel", "arbitrary")))(*args)


def _mm_nn(a, b, *, out_dtype, name, res=None, b_split=1, wide=False):
    m, k = a.shape
    n = b.shape[1] if b_split == 1 else b.shape[2] * b_split
    tm, tn, tk = _mm_tiles(m, n // b_split, k, jnp.dtype(out_dtype).itemsize, res is not None,
                           tn_cands=(1024, 512, 256, 128) if wide else (512, 256, 128))
    nk = k // tk
    njs = (n // b_split) // tn
    body = _mm_body(lambda x, y: jnp.dot(x, y, preferred_element_type=F32), nk, res is not None)
    if b_split == 1:
        b_spec = pl.BlockSpec((tk, tn), lambda i, j, kk: (kk, j))
    else:
        b_spec = pl.BlockSpec((None, tk, tn), lambda i, j, kk: (j // njs, kk, j % njs))
    return _mm_call(body, (m // tm, n // tn, nk), pl.BlockSpec((tm, tk), lambda i, j, kk: (i, kk)), b_spec,
                    tm, tn, m, n, out_dtype, name, [a, b], res)


def _mm_nt(a, b, *, out_dtype, name, res=None, a_split=1):
    if a_split == 1:
        m, k = a.shape
    else:
        m, k = a.shape[1], a.shape[2] * a_split
    n = b.shape[0]
    tm, tn, tk = _mm_tiles(m, n, k, jnp.dtype(out_dtype).itemsize, res is not None, k_mult=a_split)
    nk = k // tk
    nks = nk // a_split
    body = _mm_body(lambda x, y: lax.dot_general(x, y, (((1,), (1,)), ((), ())), preferred_element_type=F32),
                    nk, res is not None)
    if a_split == 1:
        a_spec = pl.BlockSpec((tm, tk), lambda i, j, kk: (i, kk))
    else:
        a_spec = pl.BlockSpec((None, tm, tk), lambda i, j, kk: (kk // nks, i, kk % nks))
    return _mm_call(body, (m // tm, n // tn, nk), a_spec, pl.BlockSpec((tn, tk), lambda i, j, kk: (j, kk)),
                    tm, tn, m, n, out_dtype, name, [a, b], res)


def _rms_fwd(x, g, name):
    t, d = x.shape
    tr = _pick(t, (512, 256, 128))

    def body(x_ref, g_ref, o_ref):
        o_ref[...] = _rms(x_ref[...], g_ref[...]).astype(o_ref.dtype)

    return pl.pallas_call(
        body, grid=(t // tr,),
        in_specs=[pl.BlockSpec((tr, d), lambda i: (i, 0)), pl.BlockSpec((1, d), lambda i: (0, 0))],
        out_specs=pl.BlockSpec((tr, d), lambda i: (i, 0)), out_shape=S((t, d), ACT_DTYPE), name=name,
        compiler_params=_cp(("parallel",)))(x, g)


def _rms_bwd(xin, g, dn, dres, name):
    t, d = xin.shape
    tr = _pick(t, (256, 128))

    def body(x_ref, g_ref, dn_ref, dr_ref, dx_ref, dg_ref):
        @pl.when(pl.program_id(0) == 0)
        def _():
            dg_ref[...] = jnp.zeros_like(dg_ref)

        _, vjp = jax.vjp(_rms, x_ref[...], g_ref[...])
        dx, dg = vjp(dn_ref[...].astype(F32))
        dx_ref[...] = dr_ref[...] + dx
        dg_ref[0:1, :] += dg

    row = pl.BlockSpec((tr, d), lambda i: (i, 0))
    return pl.pallas_call(
        body, grid=(t // tr,),
        in_specs=[row, pl.BlockSpec((1, d), lambda i: (0, 0)), row, row],
        out_specs=[row, pl.BlockSpec((8, d), lambda i: (0, 0))],
        out_shape=[S((t, d), F32), S((8, d), F32)], name=name,
        compiler_params=_cp(("arbitrary",)))(xin, g, dn, dres)


def _head(h2, q, pe, tgt, gf, name):
    t, d = h2.shape
    tr = _pick(t, (256, 128))

    def f(h2v, qv, pev, gfv, tv):
        h3 = h2v + jax.nn.sigmoid(qv) * pev
        y = _rms(h3, gfv)
        return 0.5 * jnp.sum(jnp.mean(jnp.square(y - tv), axis=-1))

    def body(h2_ref, q_ref, pe_ref, t_ref, g_ref, loss_ref, dh_ref, dq_ref, dpe_ref, dg_ref):
        @pl.when(pl.program_id(0) == 0)
        def _():
            loss_ref[...] = jnp.zeros_like(loss_ref)
            dg_ref[...] = jnp.zeros_like(dg_ref)

        tv = t_ref[...]
        loss, vjp = jax.vjp(lambda a, b, c, e: f(a, b, c, e, tv), h2_ref[...], q_ref[...].astype(F32),
                            pe_ref[...].astype(F32), g_ref[...])
        dh, dq, dpe, dg = vjp(jnp.ones((), F32))
        loss_ref[...] += jnp.full(loss_ref.shape, loss, F32)
        dh_ref[...] = dh
        dq_ref[...] = dq.astype(dq_ref.dtype)
        dpe_ref[...] = dpe.astype(dpe_ref.dtype)
        dg_ref[0:1, :] += dg

    row = pl.BlockSpec((tr, d), lambda i: (i, 0))
    return pl.pallas_call(
        body, grid=(t // tr,),
        in_specs=[row, row, row, row, pl.BlockSpec((1, d), lambda i: (0, 0))],
        out_specs=[pl.BlockSpec((8, 128), lambda i: (0, 0)), row, row, row, pl.BlockSpec((8, d), lambda i: (0, 0))],
        out_shape=[S((8, 128), F32), S((t, d), F32), S((t, d), MXU_DTYPE), S((t, d), MXU_DTYPE), S((8, d), F32)],
        name=name, compiler_params=_cp(("arbitrary",)))(h2, q, pe, tgt, gf)


def _gmlp_block(us, vs, lng, lnb, wss, bss, ng):
    n = us[0].shape[0]
    row = lax.broadcasted_iota(jnp.int32, (n, n), 0)
    col = lax.broadcasted_iota(jnp.int32, (n, n), 1)
    outs = []
    for u0, v0, lg, lb, ws, bs in zip(us, vs, lng, lnb, wss, bss):
        u = jax.nn.gelu(u0)
        v = jax.nn.gelu(v0)
        mu = jnp.mean(v, axis=-1, keepdims=True)
        var = jnp.mean(jnp.square(v - mu), axis=-1, keepdims=True)
        vn = (v - mu) * lax.rsqrt(var + EPS) * lg + lb
        w = jnp.where(row >= col, ws, 0.0)
        sg = jnp.dot(_mx(w), _mx(vn), preferred_element_type=F32) + bs
        outs.append(u * sg)
    return _rms(jnp.concatenate(outs, axis=1), ng)


def _gmlp_load(proj_ref, lng_ref, lnb_ref, ws_ref, bst_ref, d_a, ng):
    sl = lambda g: slice(CHUNK * g, CHUNK * (g + 1))
    us = [proj_ref[:, sl(g)].astype(F32) for g in range(ng)]
    vs = [proj_ref[:, d_a + CHUNK * g: d_a + CHUNK * (g + 1)].astype(F32) for g in range(ng)]
    lng = [lng_ref[:, sl(g)] for g in range(ng)]
    lnb = [lnb_ref[:, sl(g)] for g in range(ng)]
    wss = [ws_ref[g] for g in range(ng)]
    bss = [bst_ref[:, g:g + 1] for g in range(ng)]
    return us, vs, lng, lnb, wss, bss


def _gmlp_fwd(proj, ln_g, ln_b, w_s, bst, norm_g, d_mix, name):
    t = proj.shape[0]
    ng = w_s.shape[0]
    d_a = ng * CHUNK

    def body(proj_ref, lng_ref, lnb_ref, ws_ref, bst_ref, ng_ref, o_ref):
        args = _gmlp_load(proj_ref, lng_ref, lnb_ref, ws_ref, bst_ref, d_a, ng)
        o_ref[...] = _gmlp_block(*args, ng_ref[...]).astype(o_ref.dtype)

    vec = pl.BlockSpec((1, d_a), lambda c: (0, 0))
    return pl.pallas_call(
        body, grid=(t // CHUNK,),
        in_specs=[pl.BlockSpec((CHUNK, 2 * d_a), lambda c: (c, 0)), vec, vec,
                  pl.BlockSpec((ng, CHUNK, CHUNK), lambda c: (0, 0, 0)), pl.BlockSpec((CHUNK, 128), lambda c: (0, 0)), vec],
        out_specs=pl.BlockSpec((CHUNK, d_a), lambda c: (c, 0)), out_shape=S((t, d_mix), ACT_DTYPE),
        name=name, compiler_params=_cp(("parallel",)))(proj, ln_g, ln_b, w_s, bst, norm_g)


def _gmlp_bwd(proj, ln_g, ln_b, w_s, bst, norm_g, dcat, d_proj, name):
    t = proj.shape[0]
    ng = w_s.shape[0]
    d_a = ng * CHUNK

    def body(proj_ref, lng_ref, lnb_ref, ws_ref, bst_ref, ng_ref, dy_ref,
             dp_ref, dlng_ref, dlnb_ref, dws_ref, dbst_ref, dng_ref):
        @pl.when(pl.program_id(0) == 0)
        def _():
            for r in (dlng_ref, dlnb_ref, dws_ref, dbst_ref, dng_ref):
                r[...] = jnp.zeros_like(r)

        args = _gmlp_load(proj_ref, lng_ref, lnb_ref, ws_ref, bst_ref, d_a, ng)
        _, vjp = jax.vjp(_gmlp_block, *args, ng_ref[...])
        dus, dvs, dlng, dlnb, dwss, dbss, dng = vjp(dy_ref[...].astype(F32))
        lane = lax.broadcasted_iota(jnp.int32, (1, 128), 1)
        dbst = jnp.zeros((CHUNK, 128), F32)
        for g in range(ng):
            dp_ref[:, CHUNK * g:CHUNK * (g + 1)] = dus[g].astype(dp_ref.dtype)
            dp_ref[:, d_a + CHUNK * g:d_a + CHUNK * (g + 1)] = dvs[g].astype(dp_ref.dtype)
            dlng_ref[0:1, CHUNK * g:CHUNK * (g + 1)] += dlng[g]
            dlnb_ref[0:1, CHUNK * g:CHUNK * (g + 1)] += dlnb[g]
            dws_ref[g] += dwss[g]
            dbst = dbst + dbss[g] * (lane == g).astype(F32)
        dbst_ref[...] += dbst
        dng_ref[0:1, :] += dng

    vec = pl.BlockSpec((1, d_a), lambda c: (0, 0))
    acc = pl.BlockSpec((8, d_a), lambda c: (0, 0))
    wspec = pl.BlockSpec((ng, CHUNK, CHUNK), lambda c: (0, 0, 0))
    bspec = pl.BlockSpec((CHUNK, 128), lambda c: (0, 0))
    return pl.pallas_call(
        body, grid=(t // CHUNK,),
        in_specs=[pl.BlockSpec((CHUNK, 2 * d_a), lambda c: (c, 0)), vec, vec, wspec, bspec, vec,
                  pl.BlockSpec((CHUNK, d_a), lambda c: (c, 0))],
        out_specs=[pl.BlockSpec((CHUNK, 2 * d_a), lambda c: (c, 0)), acc, acc, wspec, bspec, acc],
        out_shape=[S((t, d_proj), ACT_DTYPE), S((8, d_a), F32), S((8, d_a), F32), S((ng, CHUNK, CHUNK), F32),
                   S((CHUNK, 128), F32), S((8, d_a), F32)],
        name=name, compiler_params=_cp(("arbitrary",)))(proj, ln_g, ln_b, w_s, bst, norm_g, dcat)


def _silu_grad(c):
    s = jax.nn.sigmoid(c)
    return s * (1.0 + c * (1.0 - s))


def _fill_prev_main(s_ref, prev_ref, main_ref, i, tt):
    s_ref[pl.ds(0, HALO), :] = jnp.where(i > 0, prev_ref[...].astype(F32), 0.0)
    s_ref[pl.ds(HALO, tt), :] = main_ref[...].astype(F32)


def _taps(s_ref, w_ref, kw, rows):
    acc = w_ref[0:1, :] * s_ref[pl.ds(HALO - (kw - 1), rows), :]
    for k in range(1, kw):
        acc = acc + w_ref[k:k + 1, :] * s_ref[pl.ds(HALO - (kw - 1) + k, rows), :]
    return acc


def _prev_spec(tt, tc, joff):
    return pl.BlockSpec((HALO, tc), lambda j, i: (jnp.maximum(i * (tt // HALO) - 1, 0), j + joff))


def _next_spec(tt, tc, joff, t):
    return pl.BlockSpec((HALO, tc), lambda j, i: (jnp.minimum((i + 1) * (tt // HALO), t // HALO - 1), j + joff))


def _conv_ssm_fwd(proj, col0, w, b, name):
    t = proj.shape[0]
    kw, c = w.shape
    tt, tc = _pick(t, (512, 256, 128)), _pick(c, (512, 256, 128))
    joff = col0 // tc
    assert col0 % tc == 0

    def body(x_ref, xp_ref, w_ref, b_ref, o_ref, s_ref):
        _fill_prev_main(s_ref, xp_ref, x_ref, pl.program_id(1), tt)
        o_ref[...] = jax.nn.silu(_taps(s_ref, w_ref, kw, tt) + b_ref[...]).astype(o_ref.dtype)

    return pl.pallas_call(
        body, grid=(c // tc, t // tt),
        in_specs=[pl.BlockSpec((tt, tc), lambda j, i: (i, j + joff)), _prev_spec(tt, tc, joff),
                  pl.BlockSpec((kw, tc), lambda j, i: (0, j)), pl.BlockSpec((1, tc), lambda j, i: (0, j))],
        out_specs=pl.BlockSpec((tt, tc), lambda j, i: (i, j)), out_shape=S((t, c), XBC_DTYPE),
        scratch_shapes=[pltpu.VMEM((HALO + tt, tc), F32)], name=name,
        compiler_params=_cp(("parallel", "arbitrary")))(proj, proj, w, b)


def _conv_ssm_bwd(proj, col0, w, b, wcol0, dact, dproj, out_col0, name):
    t = proj.shape[0]
    kw = w.shape[0]
    c = dact.shape[1]
    tt, tc = _pick(t, (512, 256, 128)), _pick(c, (512, 256, 128))
    assert col0 % tc == 0 and wcol0 % tc == 0 and out_col0 % tc == 0
    joff, wj, oj = col0 // tc, wcol0 // tc, out_col0 // tc
    nt = t // tt

    def body(x_ref, xp_ref, xn_ref, w_ref, b_ref, d_ref, dn_ref, dp_in, dx_ref, dw_ref, db_ref, s_ref, sd_ref):
        del dp_in
        i = pl.program_id(1)

        @pl.when(i == 0)
        def _():
            dw_ref[...] = jnp.zeros_like(dw_ref)
            db_ref[...] = jnp.zeros_like(db_ref)

        _fill_prev_main(s_ref, xp_ref, x_ref, i, tt)
        s_ref[pl.ds(HALO + tt, HALO), :] = xn_ref[...].astype(F32)
        cpre = _taps(s_ref, w_ref, kw, tt + HALO) + b_ref[...]
        sd_ref[pl.ds(0, tt), :] = d_ref[...].astype(F32)
        sd_ref[pl.ds(tt, HALO), :] = jnp.where(i < nt - 1, dn_ref[...].astype(F32), 0.0)
        sd_ref[...] = sd_ref[...] * _silu_grad(cpre)
        dx = w_ref[kw - 1:kw, :] * sd_ref[pl.ds(0, tt), :]
        for k in range(kw - 1):
            dx = dx + w_ref[k:k + 1, :] * sd_ref[pl.ds(kw - 1 - k, tt), :]
        dx_ref[...] = dx.astype(dx_ref.dtype)
        dmain = sd_ref[pl.ds(0, tt), :]
        for k in range(kw):
            dw_ref[k:k + 1, :] += jnp.sum(dmain * s_ref[pl.ds(HALO - (kw - 1) + k, tt), :], axis=0, keepdims=True)
        db_ref[0:1, :] += jnp.sum(dmain, axis=0, keepdims=True)

    acc = pl.BlockSpec((8, tc), lambda j, i: (0, j))
    return pl.pallas_call(
        body, grid=(c // tc, nt),
        in_specs=[pl.BlockSpec((tt, tc), lambda j, i: (i, j + joff)), _prev_spec(tt, tc, joff), _next_spec(tt, tc, joff, t),
                  pl.BlockSpec((kw, tc), lambda j, i: (0, j + wj)), pl.BlockSpec((1, tc), lambda j, i: (0, j + wj)),
                  pl.BlockSpec((tt, tc), lambda j, i: (i, j)), _next_spec(tt, tc, 0, t),
                  pl.BlockSpec(memory_space=pl.ANY)],
        out_specs=[pl.BlockSpec((tt, tc), lambda j, i: (i, j + oj)), acc, acc],
        out_shape=[S(dproj.shape, dproj.dtype), S((8, c), F32), S((8, c), F32)],
        scratch_shapes=[pltpu.VMEM((HALO + tt + HALO, tc), F32), pltpu.VMEM((tt + HALO, tc), F32)],
        input_output_aliases={7: 0}, name=name,
        compiler_params=_cp(("parallel", "arbitrary")))(proj, proj, proj, w, b, dact, dact, dproj)


def _conv_ffn_fwd(hid, w, b, name):
    t, f2 = hid.shape
    f = f2 // 2
    kw = w.shape[0]
    tt, tc = _pick(t, (512, 256, 128)), _pick(f, (512, 256, 128))
    nj = f // tc

    def body(g_ref, gp_ref, u_ref, up_ref, wg_ref, wu_ref, bg_ref, bu_ref, o_ref, sg_ref, su_ref):
        i = pl.program_id(1)
        _fill_prev_main(sg_ref, gp_ref, g_ref, i, tt)
        _fill_prev_main(su_ref, up_ref, u_ref, i, tt)
        gate = _taps(sg_ref, wg_ref, kw, tt) + bg_ref[...]
        up = _taps(su_ref, wu_ref, kw, tt) + bu_ref[...]
        o_ref[...] = (jax.nn.silu(gate) * up).astype(o_ref.dtype)

    return pl.pallas_call(
        body, grid=(nj, t // tt),
        in_specs=[pl.BlockSpec((tt, tc), lambda j, i: (i, j)), _prev_spec(tt, tc, 0),
                  pl.BlockSpec((tt, tc), lambda j, i: (i, j + nj)), _prev_spec(tt, tc, nj),
                  pl.BlockSpec((kw, tc), lambda j, i: (0, j)), pl.BlockSpec((kw, tc), lambda j, i: (0, j + nj)),
                  pl.BlockSpec((1, tc), lambda j, i: (0, j)), pl.BlockSpec((1, tc), lambda j, i: (0, j + nj))],
        out_specs=pl.BlockSpec((tt, tc), lambda j, i: (i, j)), out_shape=S((t, f), ACT_DTYPE),
        scratch_shapes=[pltpu.VMEM((HALO + tt, tc), F32), pltpu.VMEM((HALO + tt, tc), F32)], name=name,
        compiler_params=_cp(("parallel", "arbitrary")))(hid, hid, hid, hid, w, w, b, b)


def _conv_ffn_bwd(hid, w, b, dact, name):
    t, f2 = hid.shape
    f = f2 // 2
    kw = w.shape[0]
    tt, tc = _pick(t, (512, 256, 128)), _pick(f, (512, 256, 128))
    nj = f // tc
    nt = t // tt

    def body(g_ref, gp_ref, gn_ref, u_ref, up_ref, un_ref, wg_ref, wu_ref, bg_ref, bu_ref, d_ref, dn_ref,
             dh_ref, dw_ref, db_ref, sg_ref, su_ref, dg_ref, du_ref):
        i = pl.program_id(1)

        @pl.when(i == 0)
        def _():
            dw_ref[...] = jnp.zeros_like(dw_ref)
            db_ref[...] = jnp.zeros_like(db_ref)

        _fill_prev_main(sg_ref, gp_ref, g_ref, i, tt)
        sg_ref[pl.ds(HALO + tt, HALO), :] = gn_ref[...].astype(F32)
        _fill_prev_main(su_ref, up_ref, u_ref, i, tt)
        su_ref[pl.ds(HALO + tt, HALO), :] = un_ref[...].astype(F32)
        gate = _taps(sg_ref, wg_ref, kw, tt + HALO) + bg_ref[...]
        up = _taps(su_ref, wu_ref, kw, tt + HALO) + bu_ref[...]
        dg_ref[pl.ds(0, tt), :] = d_ref[...].astype(F32)
        dg_ref[pl.ds(tt, HALO), :] = jnp.where(i < nt - 1, dn_ref[...].astype(F32), 0.0)
        dact_e = dg_ref[...]
        du_ref[...] = dact_e * jax.nn.silu(gate)
        dg_ref[...] = dact_e * up * _silu_grad(gate)
        for s, (sd_ref, sx_ref, w_ref) in enumerate(((dg_ref, sg_ref, wg_ref), (du_ref, su_ref, wu_ref))):
            dx = w_ref[kw - 1:kw, :] * sd_ref[pl.ds(0, tt), :]
            for k in range(kw - 1):
                dx = dx + w_ref[k:k + 1, :] * sd_ref[pl.ds(kw - 1 - k, tt), :]
            dh_ref[s] = dx.astype(dh_ref.dtype)
            dmain = sd_ref[pl.ds(0, tt), :]
            for k in range(kw):
                dw_ref[s, k:k + 1, :] += jnp.sum(dmain * sx_ref[pl.ds(HALO - (kw - 1) + k, tt), :], axis=0, keepdims=True)
            db_ref[s, 0:1, :] += jnp.sum(dmain, axis=0, keepdims=True)

    acc = pl.BlockSpec((2, 8, tc), lambda j, i: (0, 0, j))
    ext = pltpu.VMEM((HALO + tt + HALO, tc), F32)
    dsc = pltpu.VMEM((tt + HALO, tc), F32)
    return pl.pallas_call(
        body, grid=(nj, nt),
        in_specs=[pl.BlockSpec((tt, tc), lambda j, i: (i, j)), _prev_spec(tt, tc, 0), _next_spec(tt, tc, 0, t),
                  pl.BlockSpec((tt, tc), lambda j, i: (i, j + nj)), _prev_spec(tt, tc, nj), _next_spec(tt, tc, nj, t),
                  pl.BlockSpec((kw, tc), lambda j, i: (0, j)), pl.BlockSpec((kw, tc), lambda j, i: (0, j + nj)),
                  pl.BlockSpec((1, tc), lambda j, i: (0, j)), pl.BlockSpec((1, tc), lambda j, i: (0, j + nj)),
                  pl.BlockSpec((tt, tc), lambda j, i: (i, j)), _next_spec(tt, tc, 0, t)],
        out_specs=[pl.BlockSpec((2, tt, tc), lambda j, i: (0, i, j)), acc, acc],
        out_shape=[S((2, t, f), MXU_DTYPE), S((2, 8, f), F32), S((2, 8, f), F32)],
        scratch_shapes=[ext, ext, dsc, dsc], name=name,
        compiler_params=_cp(("parallel", "arbitrary")))(hid, hid, hid, hid, hid, hid, w, w, b, b, dact, dact)


def _ssd_chunk(xs, bm, cm, dtraw, z, hins, bias, alog, dskips, normg):
    n = bm.shape[0]
    row = lax.broadcasted_iota(jnp.int32, (n, n), 0)
    col = lax.broadcasted_iota(jnp.int32, (n, n), 1)
    causal = row >= col
    tri = causal.astype(F32)
    lane = lax.broadcasted_iota(jnp.int32, (1, 128), 1)
    sub = lax.broadcasted_iota(jnp.int32, (128, 1), 0)
    last = (lax.broadcasted_iota(jnp.int32, (n, 1), 0) == n - 1).astype(F32)
    dt = jax.nn.softplus(dtraw + bias)
    adt = dt * (-jnp.exp(alog))
    acs = jnp.dot(tri, adt, preferred_element_type=F32, precision=lax.Precision.HIGHEST)
    acs_t = acs.T
    scores = lax.dot_general(_mx(cm), _mx(bm), (((1,), (1,)), ((), ())), preferred_element_type=F32)
    ys, houts = [], []
    for r in range(HEADS_PER_GROUP):
        pick = (lane == r).astype(F32)
        acol = jnp.sum(acs * pick, axis=1, keepdims=True)
        arow = jnp.sum(acs_t * (sub == r).astype(F32), axis=0, keepdims=True)
        dtc = jnp.sum(dt * pick, axis=1, keepdims=True)
        alast = jnp.sum(acol * last, axis=0, keepdims=True)
        lm = jnp.exp(jnp.where(causal, acol - arow, -1e30))
        xt = xs[r] * dtc
        yd = jnp.dot(_mx(scores * lm), _mx(xt), preferred_element_type=F32)
        yo = jnp.exp(acol) * lax.dot_general(_mx(cm), _mx(hins[r]), (((1,), (1,)), ((), ())), preferred_element_type=F32)
        ys.append(yd + yo + dskips[r] * xs[r])
        st = lax.dot_general(_mx(xt * jnp.exp(alast - acol)), _mx(bm), (((0,), (0,)), ((), ())), preferred_element_type=F32)
        houts.append(jnp.exp(alast) * hins[r] + st)
    y = jnp.concatenate(ys, axis=1) * jax.nn.silu(z)
    return _rms(y, normg), houts


def _ssd_specs(d_ssm, ngrp, zcol0, rev, nc):
    cc = (lambda c: nc - 1 - c) if rev else (lambda c: c)
    xj, bj, cj, zj = 0, d_ssm // 128, d_ssm // 128 + ngrp, zcol0 // GROUP_CH
    const = lambda w: pl.BlockSpec((None, 8, w), lambda g, c: (g, 0, 0))
    return cc, [
        pl.BlockSpec((CHUNK, GROUP_CH), lambda g, c: (cc(c), xj + g)),
        pl.BlockSpec((CHUNK, 128), lambda g, c: (cc(c), bj + g)),
        pl.BlockSpec((CHUNK, 128), lambda g, c: (cc(c), cj + g)),
        pl.BlockSpec((CHUNK, 128), lambda g, c: (cc(c), g)),
        pl.BlockSpec((CHUNK, GROUP_CH), lambda g, c: (cc(c), zj + g)),
        const(128), const(128), const(GROUP_CH), const(GROUP_CH)]


def _ssd_load(x_ref, b_ref, c_ref, dt_ref, z_ref, bias_ref, alog_ref, dsk_ref, ng_ref):
    xs = [x_ref[:, HEAD_DIM * r:HEAD_DIM * (r + 1)].astype(F32) for r in range(HEADS_PER_GROUP)]
    dskips = [dsk_ref[0:1, HEAD_DIM * r:HEAD_DIM * (r + 1)] for r in range(HEADS_PER_GROUP)]
    return (xs, b_ref[...].astype(F32), c_ref[...].astype(F32), dt_ref[...], z_ref[...].astype(F32)), \
           (bias_ref[0:1, :], alog_ref[0:1, :], dskips, ng_ref[0:1, :])


def _ssd_fwd(xbc, dtp, proj, zcol0, bias_p, alog_p, dskip_x, normg_x, cat, name):
    t = xbc.shape[0]
    ngrp = bias_p.shape[0]
    d_ssm = ngrp * GROUP_CH
    nc = t // CHUNK
    d_a = cat.shape[1] - d_ssm
    assert d_a % GROUP_CH == 0 and zcol0 % GROUP_CH == 0
    _, specs = _ssd_specs(d_ssm, ngrp, zcol0, False, nc)

    def body(x_ref, b_ref, c_ref, dt_ref, z_ref, bias_ref, alog_ref, dsk_ref, ng_ref, cat_in, y_ref, hs_ref, h_ref):
        del cat_in

        @pl.when(pl.program_id(1) == 0)
        def _():
            h_ref[...] = jnp.zeros_like(h_ref)

        data, consts = _ssd_load(x_ref, b_ref, c_ref, dt_ref, z_ref, bias_ref, alog_ref, dsk_ref, ng_ref)
        hins = [h_ref[r] for r in range(HEADS_PER_GROUP)]
        hs_ref[...] = h_ref[...]
        y, houts = _ssd_chunk(*data, hins, *consts)
        y_ref[...] = y.astype(y_ref.dtype)
        for r in range(HEADS_PER_GROUP):
            h_ref[r] = houts[r]

    return pl.pallas_call(
        body, grid=(ngrp, nc), in_specs=specs + [pl.BlockSpec(memory_space=pl.ANY)],
        out_specs=[pl.BlockSpec((CHUNK, GROUP_CH), lambda g, c: (c, d_a // GROUP_CH + g)),
                   pl.BlockSpec((None, HEADS_PER_GROUP, HEAD_DIM, D_STATE), lambda g, c: (c, g, 0, 0))],
        out_shape=[S(cat.shape, cat.dtype), S((nc, ngrp * HEADS_PER_GROUP, HEAD_DIM, D_STATE), F32)],
        scratch_shapes=[pltpu.VMEM((HEADS_PER_GROUP, HEAD_DIM, D_STATE), F32)],
        input_output_aliases={9: 0}, name=name,
        compiler_params=_cp(("parallel", "arbitrary")))(xbc, xbc, xbc, dtp, proj, bias_p, alog_p, dskip_x, normg_x, cat)


def _ssd_bwd(xbc, dtp, proj, zcol0, bias_p, alog_p, dskip_x, normg_x, hs, dcat, dproj, name):
    t = xbc.shape[0]
    ngrp = bias_p.shape[0]
    d_ssm = ngrp * GROUP_CH
    nc = t // CHUNK
    d_a = dcat.shape[1] - d_ssm
    cc, specs = _ssd_specs(d_ssm, ngrp, zcol0, True, nc)
    hp = HEADS_PER_GROUP

    def body(x_ref, b_ref, c_ref, dt_ref, z_ref, bias_ref, alog_ref, dsk_ref, ng_ref, hs_ref, dy_ref, dp_in,
             dz_ref, dx_ref, db_ref, dc_ref, ddt_ref, dbias_ref, dalog_ref, ddsk_ref, dng_ref, dh_ref):
        del dp_in

        @pl.when(pl.program_id(1) == 0)
        def _():
            dh_ref[...] = jnp.zeros_like(dh_ref)
            for r in (dbias_ref, dalog_ref, ddsk_ref, dng_ref):
                r[...] = jnp.zeros_like(r)

        data, consts = _ssd_load(x_ref, b_ref, c_ref, dt_ref, z_ref, bias_ref, alog_ref, dsk_ref, ng_ref)
        hins = [hs_ref[r] for r in range(hp)]
        _, vjp = jax.vjp(_ssd_chunk, *data, hins, *consts)
        (dxs, dbm, dcm, ddt, dz, dhins, dbias, dalog, ddsk, dng) = vjp(
            (dy_ref[...].astype(F32), [dh_ref[r] for r in range(hp)]))
        for r in range(hp):
            dx_ref[:, HEAD_DIM * r:HEAD_DIM * (r + 1)] = dxs[r].astype(dx_ref.dtype)
            dh_ref[r] = dhins[r]
            ddsk_ref[0:1, HEAD_DIM * r:HEAD_DIM * (r + 1)] += ddsk[r]
        db_ref[...] = dbm.astype(db_ref.dtype)
        dc_ref[...] = dcm.astype(dc_ref.dtype)
        ddt_ref[...] = ddt
        dz_ref[...] = dz.astype(dz_ref.dtype)
        dbias_ref[0:1, :] += dbias
        dalog_ref[0:1, :] += dalog
        dng_ref[0:1, :] += dng

    acc = lambda w: pl.BlockSpec((None, 8, w), lambda g, c: (g, 0, 0))
    blk = lambda w: pl.BlockSpec((CHUNK, w), lambda g, c: (cc(c), g))
    return pl.pallas_call(
        body, grid=(ngrp, nc),
        in_specs=specs + [pl.BlockSpec((None, hp, HEAD_DIM, D_STATE), lambda g, c: (cc(c), g, 0, 0)),
                          pl.BlockSpec((CHUNK, GROUP_CH), lambda g, c: (cc(c), d_a // GROUP_CH + g)),
                          pl.BlockSpec(memory_space=pl.ANY)],
        out_specs=[pl.BlockSpec((CHUNK, GROUP_CH), lambda g, c: (cc(c), zcol0 // GROUP_CH + g)),
                   blk(GROUP_CH), blk(128), blk(128), blk(128), acc(128), acc(128), acc(GROUP_CH), acc(GROUP_CH)],
        out_shape=[S(dproj.shape, dproj.dtype), S((t, d_ssm), XBC_DTYPE), S((t, ngrp * 128), XBC_DTYPE),
                   S((t, ngrp * 128), XBC_DTYPE), S((t, ngrp * 128), F32), S((ngrp, 8, 128), F32),
                   S((ngrp, 8, 128), F32), S((ngrp, 8, GROUP_CH), F32), S((ngrp, 8, GROUP_CH), F32)],
        scratch_shapes=[pltpu.VMEM((hp, HEAD_DIM, D_STATE), F32)],
        input_output_aliases={11: 0}, name=name,
        compiler_params=_cp(("parallel", "arbitrary")))(xbc, xbc, xbc, dtp, proj, bias_p, alog_p, dskip_x, normg_x, hs, dcat, dproj)


def _adamw(parts, w, m, v, name):
    r, c = w.shape
    tr = _pick(r, (256, 128, 64, 32, 16, 8)) if c * 4 * 256 <= 4 * 1024 * 1024 else _pick(r, (64, 32, 16, 8))

    def body(p_ref, w_ref, m_ref, v_ref, g_ref, d_ref, nm_ref, nv_ref):
        g = p_ref[0].astype(F32)
        for k in range(1, N_DEV):
            g = g + p_ref[k].astype(F32)
        mm = ADAM_B1 * m_ref[...] + (1.0 - ADAM_B1) * g
        vv = ADAM_B2 * v_ref[...] + (1.0 - ADAM_B2) * jnp.square(g)
        m_hat = mm / (1.0 - ADAM_B1 ** ADAM_STEP)
        v_hat = vv / (1.0 - ADAM_B2 ** ADAM_STEP)
        g_ref[...] = g
        d_ref[...] = -ADAM_LR * (m_hat / (jnp.sqrt(v_hat) + ADAM_EPS) + ADAM_WD * w_ref[...])
        nm_ref[...] = mm
        nv_ref[...] = vv

    blk = pl.BlockSpec((tr, c), lambda i: (i, 0))
    return pl.pallas_call(
        body, grid=(r // tr,),
        in_specs=[pl.BlockSpec((N_DEV, tr, c), lambda i: (0, i, 0)), blk, blk, blk],
        out_specs=[blk, blk, blk, blk], out_shape=[S((r, c), F32)] * 4, name=name,
        compiler_params=_cp(("parallel",)))(parts, w, m, v)


def _mesh_pos():
    return lax.axis_index("x"), lax.axis_index("y"), lax.axis_index("c")


def _peer(d, x, y, c):
    return (1 - x if (d >> 2) & 1 else x, 1 - y if (d >> 1) & 1 else y, 1 - c if d & 1 else c)


def _xfer_start(items, name, after=None):
    n = len(items)
    kinds = [k for k, _ in items]
    srcs = [pltpu.with_memory_space_constraint(a, pltpu.HBM) for _, a in items]
    land_shapes = [((N_DEV,) + a.shape if k == 'gather' else a.shape, a.dtype) for k, a in items]
    lands = [pltpu.with_memory_space_constraint(lax.empty(s, dt), pltpu.HBM) for s, dt in land_shapes]
    extra = [] if after is None else [after]

    def body(*refs):
        src_refs, land_refs = refs[:n], refs[n:2 * n]
        outs = refs[2 * n + len(extra):]
        sems = outs[:2 * n]
        token = outs[4 * n]
        x, y, c = _mesh_pos()
        me = 4 * x + 2 * y + c
        for a in range(n):
            for d in range(1, N_DEV):
                px, py, pc = _peer(d, x, y, c)
                src = src_refs[a] if kinds[a] == 'gather' else src_refs[a].at[4 * px + 2 * py + pc]
                pltpu.make_async_remote_copy(
                    src_ref=src, dst_ref=land_refs[a].at[me], send_sem=sems[2 * a].at[d - 1],
                    recv_sem=sems[2 * a + 1].at[d - 1], device_id=(px, py, pc),
                    device_id_type=pl.DeviceIdType.MESH).start()
        token[...] = jnp.zeros_like(token)

    hbm = pl.BlockSpec(memory_space=pltpu.HBM)
    sem = pl.BlockSpec(memory_space=pltpu.SEMAPHORE)
    out_shape = ([pltpu.SemaphoreType.DMA((N_DEV - 1,))] * (2 * n)
                 + [pltpu.HBM(a.shape, a.dtype) for a in srcs] + [pltpu.HBM(s, dt) for s, dt in land_shapes]
                 + [S((8, 128), F32)])
    res = pl.pallas_call(
        body, name=name, out_shape=out_shape,
        in_specs=[hbm] * (2 * n) + [pl.BlockSpec(memory_space=pl.ANY)] * len(extra),
        out_specs=[sem] * (2 * n) + [hbm] * (2 * n) + [pl.BlockSpec(memory_space=pltpu.VMEM)],
        input_output_aliases={**{a: 2 * n + a for a in range(n)}, **{n + a: 3 * n + a for a in range(n)}},
        compiler_params=pltpu.CompilerParams(has_side_effects=pltpu.SideEffectType.DATAFLOW_SIDE_EFFECTING),
    )(*srcs, *lands, *extra)
    return (kinds, res[:2 * n], res[2 * n:3 * n], res[3 * n:4 * n]), res[4 * n]


def _xfer_wait(handle, after, name):
    kinds, sems, src_thru, land_thru = handle
    n = len(kinds)

    def body(*refs):
        land_refs = refs[n:2 * n]
        sem_refs = refs[2 * n:4 * n]
        x, y, c = _mesh_pos()
        me = 4 * x + 2 * y + c
        for a in range(n):
            for d in range(1, N_DEV):
                slab = land_refs[a].at[me]
                cp = pltpu.make_async_remote_copy(
                    src_ref=slab, dst_ref=slab, send_sem=sem_refs[2 * a].at[d - 1], recv_sem=sem_refs[2 * a + 1].at[d - 1],
                    device_id=_peer(d, x, y, c), device_id_type=pl.DeviceIdType.MESH)
                cp.wait_send()
                cp.wait_recv()

    hbm = pl.BlockSpec(memory_space=pltpu.HBM)
    sem = pl.BlockSpec(memory_space=pltpu.SEMAPHORE)
    res = pl.pallas_call(
        body, name=name,
        out_shape=[pltpu.HBM(a.shape, a.dtype) for a in src_thru] + [pltpu.HBM(a.shape, a.dtype) for a in land_thru],
        in_specs=[hbm] * (2 * n) + [sem] * (2 * n) + [pl.BlockSpec(memory_space=pl.ANY)],
        out_specs=[hbm] * (2 * n), input_output_aliases={a: a for a in range(2 * n)},
        compiler_params=pltpu.CompilerParams(has_side_effects=pltpu.SideEffectType.DATAFLOW_SIDE_EFFECTING),
    )(*src_thru, *land_thru, *sems, after)
    x, y, c = _mesh_pos()
    me = 4 * x + 2 * y + c
    out = []
    for a in range(n):
        src = res[a]
        own = src[None] if kinds[a] == 'gather' else lax.dynamic_index_in_dim(src, me, 0, keepdims=True)
        out.append(lax.dynamic_update_index_in_dim(res[n + a], own, me, 0))
    return out


def _stack_to_full(kind, st):
    if kind == 'row':
        return st.reshape(st.shape[0] * st.shape[1], st.shape[2])
    return jnp.transpose(st, (1, 0, 2)).reshape(st.shape[1], st.shape[0] * st.shape[2])


def _full_to_stack(kind, full):
    r, c = full.shape
    if kind == 'row':
        return full.reshape(N_DEV, r // N_DEV, c)
    return jnp.transpose(full.reshape(r, N_DEV, c // N_DEV), (1, 0, 2))


def _pack_small(named):
    layout = [(a.shape, a.size, -(-a.size // 1024) * 8) for a in named]
    total = sum(nr for _, _, nr in layout) * 128
    packed, off = None, 0
    for a, (_, n, nr) in zip(named, layout):
        part = jnp.pad(a.reshape(-1).astype(F32), (off, total - off - n))
        packed = part if packed is None else packed + part
        off += nr * 128
    return packed.reshape(total // 128, 128), layout


def _unpack_small(packed, layout):
    out, r0 = [], 0
    for shape, n, nr in layout:
        out.append(packed[r0:r0 + nr].reshape(-1)[:n].reshape(shape))
        r0 += nr
    return out


def _row0(acc):
    return acc[0]


def _local_step(x, p, tgt, sm, comm):
    t, d = x.shape
    h_n = sm['dt_bias'].shape[-1]
    ngrp = h_n // HEADS_PER_GROUP
    d_ssm = h_n * HEAD_DIM
    d_a = sm['ln_a_g'].shape[-1]
    d_mix = d_a + d_ssm
    d_xbc = sm['conv_ssm_b'].shape[-1]
    d_main = 2 * d_a + d_ssm + d_xbc
    assert d_xbc == d_ssm + 2 * ngrp * D_STATE and h_n <= 128
    zcol0, xcol0 = 2 * d_a, 2 * d_a + d_ssm
    vec = lambda v: v.reshape(1, -1)

    bst = jnp.pad(sm['b_s'].T, ((0, 0), (0, 128 - sm['b_s'].shape[0])))
    grp = lambda v, w: jnp.broadcast_to(jnp.pad(v.reshape(ngrp, 1, -1), ((0, 0), (0, 0), (0, w - v.size // ngrp))), (ngrp, 8, w))
    bias_p, alog_p = grp(sm['dt_bias'], 128), grp(sm['a_log'], 128)
    dskip_x = grp(jnp.repeat(sm['d_skip'], HEAD_DIM), GROUP_CH)
    normg_x = grp(sm['ssm_norm_g'], GROUP_CH)
    pad_dt = lambda v: jnp.pad(v[:, :h_n].reshape(t, ngrp, HEADS_PER_GROUP),
                               ((0, 0), (0, 0), (0, 128 - HEADS_PER_GROUP))).reshape(t, ngrp * 128)

    g_mix = vec(sm['norm_mix_g']) + comm.tok0
    a_n = _rms_fwd(x, g_mix, "rms_mix")
    wf = comm.weights('a', a_n)
    w_main = wf['w_in'][:, :d_main]
    w_dt = jnp.pad(wf['w_in'][:, d_main:], ((0, 0), (0, 128 - h_n)))
    proj = _mm_nn(a_n, w_main, out_dtype=ACT_DTYPE, name="mm_in")
    dtp = pad_dt(_mm_nn(a_n, w_dt, out_dtype=F32, name="mm_dt"))
    cat = _gmlp_fwd(proj, vec(sm['ln_a_g']), vec(sm['ln_a_b']), sm['w_s'], bst, vec(sm['norm_a_g']), d_mix, "gmlp_fwd")
    xbc = _conv_ssm_fwd(proj, xcol0, wf['conv_ssm_w'], vec(sm['conv_ssm_b']), "conv_ssm_fwd")
    cat, hs = _ssd_fwd(xbc, dtp, proj, zcol0, bias_p, alog_p, dskip_x, normg_x, cat, "ssd_fwd")
    wf.update(comm.weights('b', hs))
    h1 = _mm_nn(cat, wf['w_out'], out_dtype=F32, name="mm_out", res=x)
    f_n = _rms_fwd(h1, vec(sm['norm_ffn_g']), "rms_ffn")
    hid = _mm_nn(f_n, wf['w_up'], out_dtype=ACT_DTYPE, name="mm_up")
    act = _conv_ffn_fwd(hid, wf['conv_ffn_w'], vec(sm['conv_ffn_b']), "conv_ffn_fwd")
    h2 = _mm_nn(act, wf['w_down'], out_dtype=F32, name="mm_down", res=h1)
    r_n = _rms_fwd(h2, vec(sm['norm_ple_g']), "rms_ple")
    q = _mm_nn(r_n, wf['w_ple_gate'], out_dtype=F32, name="mm_pg")
    p_m = p.astype(MXU_DTYPE)
    pe = _mm_nn(p_m, wf['w_ple'], out_dtype=F32, name="mm_ple")

    loss, dh3, dq, dpe, dgf = _head(h2, q, pe, tgt, vec(sm['norm_final_g']), "head")
    wgrad = lambda act_t, g, name, **kw: _mm_nn(act_t, g, out_dtype=WIRE_DTYPE, name=name, wide=True, **kw)
    gs = {}
    gs['norm_final_g'] = _row0(dgf)
    g_ple = wgrad(p_m.T, dpe, "wg_ple")
    g_pg = wgrad(r_n.T, dq, "wg_pg")
    dr = _mm_nt(dq, wf['w_ple_gate'], out_dtype=F32, name="dg_pg")
    dh2, dg = _rms_bwd(h2, vec(sm['norm_ple_g']), dr, dh3, "rms_ple_bwd")
    gs['norm_ple_g'] = _row0(dg)
    dh2m = dh2.astype(MXU_DTYPE)
    g_down = wgrad(act.T, dh2m, "wg_down")
    tok = comm.send('1', {'w_ple': g_ple, 'w_ple_gate': g_pg, 'w_down': g_down})
    dact = _mm_nt(dh2m, wf['w_down'], out_dtype=ACT_DTYPE, name="dg_down")
    dhid, dcw, dcb = _conv_ffn_bwd(hid, wf['conv_ffn_w'], vec(sm['conv_ffn_b']) + tok, dact, "conv_ffn_bwd")
    kf = wf['conv_ffn_w'].shape[0]
    g_cf = jnp.concatenate([dcw[0, :kf], dcw[1, :kf]], axis=1)
    gs['conv_ffn_b'] = jnp.concatenate([dcb[0, 0], dcb[1, 0]], axis=0)
    g_up = wgrad(f_n.T, dhid, "wg_up", b_split=2)
    df = _mm_nt(dhid, wf['w_up'], out_dtype=F32, name="dg_up", a_split=2)
    dh1, dg = _rms_bwd(h1, vec(sm['norm_ffn_g']), df, dh2, "rms_ffn_bwd")
    gs['norm_ffn_g'] = _row0(dg)
    dh1m = dh1.astype(MXU_DTYPE)
    g_out = wgrad(cat.T, dh1m, "wg_out")
    tok = comm.send('2', {'conv_ffn_w': g_cf, 'w_up': g_up, 'w_out': g_out})
    dcat = _mm_nt(dh1m, wf['w_out'], out_dtype=ACT_DTYPE, name="dg_out")
    dproj, dlng, dlnb, dws, dbst, dng = _gmlp_bwd(proj, vec(sm['ln_a_g']) + tok, vec(sm['ln_a_b']), sm['w_s'], bst,
                                                  vec(sm['norm_a_g']), dcat, d_main, "gmlp_bwd")
    gs['ln_a_g'], gs['ln_a_b'], gs['w_s'], gs['norm_a_g'] = _row0(dlng), _row0(dlnb), dws, _row0(dng)
    gs['b_s'] = dbst[:, :sm['b_s'].shape[0]].T
    dproj, dxs, dbm, dcm, ddtp, dbias, dalog, ddsk, dsng = _ssd_bwd(
        xbc, dtp, proj, zcol0, bias_p, alog_p, dskip_x, normg_x, hs, dcat, dproj, "ssd_bwd")
    gs['dt_bias'] = dbias[:, 0, :HEADS_PER_GROUP].reshape(h_n)
    gs['a_log'] = dalog[:, 0, :HEADS_PER_GROUP].reshape(h_n)
    gs['d_skip'] = ddsk[:, 0, :].reshape(h_n, HEAD_DIM).sum(axis=-1)
    gs['ssm_norm_g'] = dsng[:, 0, :].reshape(d_ssm)
    dws_c, dbs_c = [], []
    off = 0
    for nm, dpart in (("x", dxs), ("b", dbm), ("c", dcm)):
        dproj, dw_c, db_c = _conv_ssm_bwd(proj, xcol0 + off, wf['conv_ssm_w'], vec(sm['conv_ssm_b']), off, dpart, dproj,
                                          xcol0 + off, "conv_ssm_bwd_" + nm)
        dws_c.append(dw_c[:wf['conv_ssm_w'].shape[0]])
        dbs_c.append(db_c[0])
        off += dpart.shape[1]
    g_cs = jnp.concatenate(dws_c, axis=1)
    gs['conv_ssm_b'] = jnp.concatenate(dbs_c, axis=0)
    ddt = jnp.pad(ddtp.reshape(t, ngrp, 128)[:, :, :HEADS_PER_GROUP].reshape(t, h_n), ((0, 0), (0, 128 - h_n))).astype(MXU_DTYPE)
    a_t = a_n.T
    g_in = jnp.concatenate([wgrad(a_t, dproj, "wg_in"), wgrad(a_t, ddt, "wg_dt")[:, :h_n]], axis=1)
    tok = comm.send('3', {'conv_ssm_w': g_cs, 'w_in': g_in}, [(n, gs[n]) for n in REPLICATED if n != 'norm_mix_g'])
    da = _mm_nt(ddt, w_dt, out_dtype=F32, name="dg_dt")
    da = _mm_nt(dproj, w_main, out_dtype=F32, name="dg_in", res=da)
    dx, dg = _rms_bwd(x, g_mix + tok, da, dh1, "rms_mix_bwd")
    comm.send('4', {}, [('norm_mix_g', _row0(dg)), ('loss', loss[0, 0:1])])
    return dx


def kernel(x, p, norm_mix_g, w_in, ln_a_g, ln_a_b, w_s, b_s, norm_a_g, conv_ssm_w, conv_ssm_b, dt_bias, a_log, d_skip, ssm_norm_g, w_out, norm_ffn_g, w_up, conv_ffn_w, conv_ffn_b, w_down, norm_ple_g, w_ple_gate, w_ple, norm_final_g, loss_target, m_norm_mix_g, m_w_in, m_ln_a_g, m_ln_a_b, m_w_s, m_b_s, m_norm_a_g, m_conv_ssm_w, m_conv_ssm_b, m_dt_bias, m_a_log, m_d_skip, m_ssm_norm_g, m_w_out, m_norm_ffn_g, m_w_up, m_conv_ffn_w, m_conv_ffn_b, m_w_down, m_norm_ple_g, m_w_ple_gate, m_w_ple, m_norm_final_g, v_norm_mix_g, v_w_in, v_ln_a_g, v_ln_a_b, v_w_s, v_b_s, v_norm_a_g, v_conv_ssm_w, v_conv_ssm_b, v_dt_bias, v_a_log, v_d_skip, v_ssm_norm_g, v_w_out, v_norm_ffn_g, v_w_up, v_conv_ffn_w, v_conv_ffn_b, v_w_down, v_norm_ple_g, v_w_ple_gate, v_w_ple, v_norm_final_g):
    given = dict(locals())
    wts = {n: given[n] for n in WEIGHTS}
    ms = {n: given["m_" + n] for n in WEIGHTS}
    vs = {n: given["v_" + n] for n in WEIGHTS}
    sm = {n: (wts[n][0] if wts[n].ndim > 1 else wts[n]) for n in REPLICATED}
    comm = _Comm({n: wts[n][0] for n in SHARDED})
    dx = _local_step(x[0], p[0, 0], loss_target[0], sm, comm)

    out, loss_out, after = {}, None, dx
    for tag, names, small_names, layout, handle in comm.sent:
        recv = _xfer_wait(handle, after, "grads_%s_wait" % tag)
        for n, parts in zip(names, recv):
            out[n] = _adamw(parts, wts[n][0], ms[n][0], vs[n][0], "adamw_" + n)
            after = out[n][1]
        if small_names:
            pick = lambda src, fill: _pack_small([src[n] if n in src else jnp.full((1,), fill, F32) for n in small_names])[0]
            res = _adamw(recv[-1], pick(wts, 0.0), pick(ms, 0.0), pick(vs, 1.0), "adamw_small_" + tag)
            res = [_unpack_small(o, layout) for o in res]
            after = res[1][0]
            for i, n in enumerate(small_names):
                if n == 'loss':
                    loss_out = res[0][i].reshape(())
                else:
                    out[n] = [res[k][i] for k in range(4)]
    return (loss_out, dx[None], *[out[n][k].reshape(wts[n].shape) for k in range(4) for n in WEIGHTS])


class _Comm:
    GATHER_GROUPS = {'a': ('w_in', 'conv_ssm_w'), 'b': ('w_out', 'w_up', 'conv_ffn_w', 'w_down', 'w_ple_gate', 'w_ple')}

    def __init__(self, blocks):
        self.handles, tok = {}, None
        for grp, names in self.GATHER_GROUPS.items():
            items = [('gather', blocks[n].astype(_wire(n))) for n in names]
            self.handles[grp], t = _xfer_start(items, "gather_%s_start" % grp, after=tok)
            tok = t
            self.tok0 = t[0, 0] if grp == 'a' else self.tok0 + t[0, 0]
        self.sent = []

    def weights(self, grp, after):
        stacks = _xfer_wait(self.handles[grp], after, "gather_%s_wait" % grp)
        return {n: _stack_to_full(SHARDED[n], st) for n, st in zip(self.GATHER_GROUPS[grp], stacks)}

    def send(self, tag, gw, small=None):
        items = [('scatter', _full_to_stack(SHARDED[n], g.astype(_wire(n)))) for n, g in gw.items()]
        layout, small_names = None, []
        if small:
            packed, layout = _pack_small([a for _, a in small])
            small_names = [n for n, _ in small]
            items.append(('gather', packed))
        handle, tok = _xfer_start(items, "grads_%s_start" % tag)
        self.sent.append((tag, list(gw), small_names, layout, handle))
        return tok[0, 0]


def _wire(name):
    return F32 if name in F32_ON_WIRE else WIRE_DTYPE
```

```python
import functools

import jax
import jax.numpy as jnp
from jax import lax
from jax.experimental import pallas as pl
from jax.experimental.pallas import tpu as pltpu

F32 = jnp.float32
MXU_DTYPE = jnp.bfloat16
ACT_DTYPE = jnp.bfloat16
XBC_DTYPE = jnp.bfloat16
WIRE_DTYPE = jnp.bfloat16
EPS = 1e-6
CHUNK = 128
D_STATE = 128
HEAD_DIM = 64
HEADS_PER_GROUP = 4
GROUP_CH = HEAD_DIM * HEADS_PER_GROUP
HALO = 16
N_DEV = 8
VMEM_LIMIT = 48 * 1024 * 1024

ADAM_LR = 0.001
ADAM_B1 = 0.9
ADAM_B2 = 0.999
ADAM_EPS = 1e-08
ADAM_WD = 0.01
ADAM_STEP = 10

WEIGHTS = ['norm_mix_g', 'w_in', 'ln_a_g', 'ln_a_b', 'w_s', 'b_s', 'norm_a_g', 'conv_ssm_w', 'conv_ssm_b', 'dt_bias',
           'a_log', 'd_skip', 'ssm_norm_g', 'w_out', 'norm_ffn_g', 'w_up', 'conv_ffn_w', 'conv_ffn_b', 'w_down',
           'norm_ple_g', 'w_ple_gate', 'w_ple', 'norm_final_g']
SHARDED = {'w_in': 'col', 'conv_ssm_w': 'col', 'w_out': 'row', 'w_up': 'col', 'conv_ffn_w': 'col', 'w_down': 'row',
           'w_ple_gate': 'row', 'w_ple': 'col'}
F32_ON_WIRE = ('conv_ssm_w', 'conv_ffn_w')
REPLICATED = [n for n in WEIGHTS if n not in SHARDED]

S = jax.ShapeDtypeStruct


def _pick(dim, cands):
    for c in cands:
        if c <= dim and dim % c == 0:
            return c
    return dim


def _cp(sem, vmem=VMEM_LIMIT):
    return pltpu.CompilerParams(dimension_semantics=sem, vmem_limit_bytes=vmem)


def _mx(v):
    return v.astype(MXU_DTYPE)


def _rms(v, g):
    return v * lax.rsqrt(jnp.mean(v * v, axis=-1, keepdims=True) + EPS) * g


MM_VMEM_BUDGET = 34 * 1024 * 1024


def _mm_tiles(m, n, k, out_bytes, has_res, tn_cands=(512, 256, 128), k_mult=1):
    tm = _pick(m, (1024, 512, 256, 128))
    tn = _pick(n, tn_cands)
    ks = k // k_mult
    for nk in range(1, ks // 128 + 1):
        if ks % nk or (ks // nk) % 128:
            continue
        tk = ks // nk
        need = 2 * 2 * (tm * tk + tk * tn) + tm * tn * (4 + 2 * out_bytes + (8 if has_res else 0))
        if need <= MM_VMEM_BUDGET:
            return tm, tn, tk
    return tm, tn, _pick(ks, (128,))


def _mm_body(dot, nk, has_res):
    def body(*refs):
        if has_res:
            a_ref, b_ref, r_ref, o_ref, acc_ref = refs
        else:
            a_ref, b_ref, o_ref, acc_ref = refs
            r_ref = None
        kk = pl.program_id(2)
        d = dot(a_ref[...], b_ref[...])

        def fin(acc):
            if r_ref is not None:
                acc = acc + r_ref[...]
            o_ref[...] = acc.astype(o_ref.dtype)

        if nk == 1:
            fin(d)
        else:
            @pl.when(kk == 0)
            def _():
                acc_ref[...] = d

            @pl.when(kk > 0)
            def _():
                acc_ref[...] += d

            @pl.when(kk == nk - 1)
            def _():
                fin(acc_ref[...])

    return body


def _mm_call(body, grid, a_spec, b_spec, tm, tn, m, n, out_dtype, name, args, res):
    in_specs = [a_spec, b_spec]
    if res is not None:
        in_specs.append(pl.BlockSpec((tm, tn), lambda i, j, kk: (i, j)))
        args = args + [res]
    return pl.pallas_call(
        body, grid=grid, in_specs=in_specs,
        out_specs=pl.BlockSpec((tm, tn), lambda i, j, kk: (i, j)), out_shape=S((m, n), out_dtype),
        scratch_shapes=[pltpu.VMEM((tm, tn), F32)], name=name,
        compiler_params=_cp(("parallel", "parallel", "arbitrary")))(*args)


def _mm_nn(a, b, *, out_dtype, name, res=None, b_split=1, wide=False):
    m, k = a.shape
    n = b.shape[1] if b_split == 1 else b.shape[2] * b_split
    tm, tn, tk = _mm_tiles(m, n // b_split, k, jnp.dtype(out_dtype).itemsize, res is not None,
                           tn_cands=(1024, 512, 256, 128) if wide else (512, 256, 128))
    nk = k // tk
    njs = (n // b_split) // tn
    body = _mm_body(lambda x, y: jnp.dot(x, y, preferred_element_type=F32), nk, res is not None)
    if b_split == 1:
        b_spec = pl.BlockSpec((tk, tn), lambda i, j, kk: (kk, j))
    else:
        b_spec = pl.BlockSpec((None, tk, tn), lambda i, j, kk: (j // njs, kk, j % njs))
    return _mm_call(body, (m // tm, n // tn, nk), pl.BlockSpec((tm, tk), lambda i, j, kk: (i, kk)), b_spec,
                    tm, tn, m, n, out_dtype, name, [a, b], res)


def _mm_nt(a, b, *, out_dtype, name, res=None, a_split=1):
    if a_split == 1:
        m, k = a.shape
    else:
        m, k = a.shape[1], a.shape[2] * a_split
    n = b.shape[0]
    tm, tn, tk = _mm_tiles(m, n, k, jnp.dtype(out_dtype).itemsize, res is not None, k_mult=a_split)
    nk = k // tk
    nks = nk // a_split
    body = _mm_body(lambda x, y: lax.dot_general(x, y, (((1,), (1,)), ((), ())), preferred_element_type=F32),
                    nk, res is not None)
    if a_split == 1:
        a_spec = pl.BlockSpec((tm, tk), lambda i, j, kk: (i, kk))
    else:
        a_spec = pl.BlockSpec((None, tm, tk), lambda i, j, kk: (kk // nks, i, kk % nks))
    return _mm_call(body, (m // tm, n // tn, nk), a_spec, pl.BlockSpec((tn, tk), lambda i, j, kk: (j, kk)),
                    tm, tn, m, n, out_dtype, name, [a, b], res)


def _rms_fwd(x, g, name):
    t, d = x.shape
    tr = _pick(t, (512, 256, 128))

    def body(x_ref, g_ref, o_ref):
        o_ref[...] = _rms(x_ref[...], g_ref[...]).astype(o_ref.dtype)

    return pl.pallas_call(
        body, grid=(t // tr,),
        in_specs=[pl.BlockSpec((tr, d), lambda i: (i, 0)), pl.BlockSpec((1, d), lambda i: (0, 0))],
        out_specs=pl.BlockSpec((tr, d), lambda i: (i, 0)), out_shape=S((t, d), ACT_DTYPE), name=name,
        compiler_params=_cp(("parallel",)))(x, g)


def _rms_bwd(xin, g, dn, dres, name):
    t, d = xin.shape
    tr = _pick(t, (256, 128))

    def body(x_ref, g_ref, dn_ref, dr_ref, dx_ref, dxm_ref, dg_ref):
        @pl.when(pl.program_id(0) == 0)
        def _():
            dg_ref[...] = jnp.zeros_like(dg_ref)

        _, vjp = jax.vjp(_rms, x_ref[...], g_ref[...])
        dx, dg = vjp(dn_ref[...].astype(F32))
        dx = dr_ref[...] + dx
        dx_ref[...] = dx
        dxm_ref[...] = dx.astype(dxm_ref.dtype)
        dg_ref[0:1, :] += dg

    row = pl.BlockSpec((tr, d), lambda i: (i, 0))
    return pl.pallas_call(
        body, grid=(t // tr,),
        in_specs=[row, pl.BlockSpec((1, d), lambda i: (0, 0)), row, row],
        out_specs=[row, row, pl.BlockSpec((8, d), lambda i: (0, 0))],
        out_shape=[S((t, d), F32), S((t, d), MXU_DTYPE), S((8, d), F32)], name=name,
        compiler_params=_cp(("arbitrary",)))(xin, g, dn, dres)


def _head(h2, q, pe, tgt, gf, name):
    t, d = h2.shape
    tr = _pick(t, (256, 128))

    def f(h2v, qv, pev, gfv, tv):
        h3 = h2v + jax.nn.sigmoid(qv) * pev
        y = _rms(h3, gfv)
        return 0.5 * jnp.sum(jnp.mean(jnp.square(y - tv), axis=-1))

    def body(h2_ref, q_ref, pe_ref, t_ref, g_ref, loss_ref, dh_ref, dq_ref, dpe_ref, dg_ref):
        @pl.when(pl.program_id(0) == 0)
        def _():
            loss_ref[...] = jnp.zeros_like(loss_ref)
            dg_ref[...] = jnp.zeros_like(dg_ref)

        tv = t_ref[...]
        loss, vjp = jax.vjp(lambda a, b, c, e: f(a, b, c, e, tv), h2_ref[...], q_ref[...].astype(F32),
                            pe_ref[...].astype(F32), g_ref[...])
        dh, dq, dpe, dg = vjp(jnp.ones((), F32))
        loss_ref[...] += jnp.full(loss_ref.shape, loss, F32)
        dh_ref[...] = dh
        dq_ref[...] = dq.astype(dq_ref.dtype)
        dpe_ref[...] = dpe.astype(dpe_ref.dtype)
        dg_ref[0:1, :] += dg

    row = pl.BlockSpec((tr, d), lambda i: (i, 0))
    return pl.pallas_call(
        body, grid=(t // tr,),
        in_specs=[row, row, row, row, pl.BlockSpec((1, d), lambda i: (0, 0))],
        out_specs=[pl.BlockSpec((8, 128), lambda i: (0, 0)), row, row, row, pl.BlockSpec((8, d), lambda i: (0, 0))],
        out_shape=[S((8, 128), F32), S((t, d), F32), S((t, d), MXU_DTYPE), S((t, d), MXU_DTYPE), S((8, d), F32)],
        name=name, compiler_params=_cp(("arbitrary",)))(h2, q, pe, tgt, gf)


def _gmlp_block(us, vs, lng, lnb, wss, bss, ng):
    n = us[0].shape[0]
    row = lax.broadcasted_iota(jnp.int32, (n, n), 0)
    col = lax.broadcasted_iota(jnp.int32, (n, n), 1)
    outs = []
    for u0, v0, lg, lb, ws, bs in zip(us, vs, lng, lnb, wss, bss):
        u = jax.nn.gelu(u0)
        v = jax.nn.gelu(v0)
        mu = jnp.mean(v, axis=-1, keepdims=True)
        var = jnp.mean(jnp.square(v - mu), axis=-1, keepdims=True)
        vn = (v - mu) * lax.rsqrt(var + EPS) * lg + lb
        w = jnp.where(row >= col, ws, 0.0)
        sg = jnp.dot(_mx(w), _mx(vn), preferred_element_type=F32) + bs
        outs.append(u * sg)
    return _rms(jnp.concatenate(outs, axis=1), ng)


def _gmlp_load(proj_ref, lng_ref, lnb_ref, ws_ref, bst_ref, d_a, ng):
    sl = lambda g: slice(CHUNK * g, CHUNK * (g + 1))
    us = [proj_ref[:, sl(g)].astype(F32) for g in range(ng)]
    vs = [proj_ref[:, d_a + CHUNK * g: d_a + CHUNK * (g + 1)].astype(F32) for g in range(ng)]
    lng = [lng_ref[:, sl(g)] for g in range(ng)]
    lnb = [lnb_ref[:, sl(g)] for g in range(ng)]
    wss = [ws_ref[g] for g in range(ng)]
    bss = [bst_ref[:, g:g + 1] for g in range(ng)]
    return us, vs, lng, lnb, wss, bss


def _gmlp_fwd(proj, ln_g, ln_b, w_s, bst, norm_g, d_mix, name):
    t = proj.shape[0]
    ng = w_s.shape[0]
    d_a = ng * CHUNK

    def body(proj_ref, lng_ref, lnb_ref, ws_ref, bst_ref, ng_ref, o_ref):
        args = _gmlp_load(proj_ref, lng_ref, lnb_ref, ws_ref, bst_ref, d_a, ng)
        o_ref[...] = _gmlp_block(*args, ng_ref[...]).astype(o_ref.dtype)

    vec = pl.BlockSpec((1, d_a), lambda c: (0, 0))
    return pl.pallas_call(
        body, grid=(t // CHUNK,),
        in_specs=[pl.BlockSpec((CHUNK, 2 * d_a), lambda c: (c, 0)), vec, vec,
                  pl.BlockSpec((ng, CHUNK, CHUNK), lambda c: (0, 0, 0)), pl.BlockSpec((CHUNK, 128), lambda c: (0, 0)), vec],
        out_specs=pl.BlockSpec((CHUNK, d_a), lambda c: (c, 0)), out_shape=S((t, d_mix), ACT_DTYPE),
        name=name, compiler_params=_cp(("parallel",)))(proj, ln_g, ln_b, w_s, bst, norm_g)


def _gmlp_bwd(proj, ln_g, ln_b, w_s, bst, norm_g, dcat, d_proj, name):
    t = proj.shape[0]
    ng = w_s.shape[0]
    d_a = ng * CHUNK

    def body(proj_ref, lng_ref, lnb_ref, ws_ref, bst_ref, ng_ref, dy_ref,
             dp_ref, dlng_ref, dlnb_ref, dws_ref, dbst_ref, dng_ref):
        @pl.when(pl.program_id(0) == 0)
        def _():
            for r in (dlng_ref, dlnb_ref, dws_ref, dbst_ref, dng_ref):
                r[...] = jnp.zeros_like(r)

        args = _gmlp_load(proj_ref, lng_ref, lnb_ref, ws_ref, bst_ref, d_a, ng)
        _, vjp = jax.vjp(_gmlp_block, *args, ng_ref[...])
        dus, dvs, dlng, dlnb, dwss, dbss, dng = vjp(dy_ref[...].astype(F32))
        lane = lax.broadcasted_iota(jnp.int32, (1, 128), 1)
        dbst = jnp.zeros((CHUNK, 128), F32)
        for g in range(ng):
            dp_ref[:, CHUNK * g:CHUNK * (g + 1)] = dus[g].astype(dp_ref.dtype)
            dp_ref[:, d_a + CHUNK * g:d_a + CHUNK * (g + 1)] = dvs[g].astype(dp_ref.dtype)
            dlng_ref[0:1, CHUNK * g:CHUNK * (g + 1)] += dlng[g]
            dlnb_ref[0:1, CHUNK * g:CHUNK * (g + 1)] += dlnb[g]
            dws_ref[g] += dwss[g]
            dbst = dbst + dbss[g] * (lane == g).astype(F32)
        dbst_ref[...] += dbst
        dng_ref[0:1, :] += dng

    vec = pl.BlockSpec((1, d_a), lambda c: (0, 0))
    acc = pl.BlockSpec((8, d_a), lambda c: (0, 0))
    wspec = pl.BlockSpec((ng, CHUNK, CHUNK), lambda c: (0, 0, 0))
    bspec = pl.BlockSpec((CHUNK, 128), lambda c: (0, 0))
    return pl.pallas_call(
        body, grid=(t // CHUNK,),
        in_specs=[pl.BlockSpec((CHUNK, 2 * d_a), lambda c: (c, 0)), vec, vec, wspec, bspec, vec,
                  pl.BlockSpec((CHUNK, d_a), lambda c: (c, 0))],
        out_specs=[pl.BlockSpec((CHUNK, 2 * d_a), lambda c: (c, 0)), acc, acc, wspec, bspec, acc],
        out_shape=[S((t, d_proj), ACT_DTYPE), S((8, d_a), F32), S((8, d_a), F32), S((ng, CHUNK, CHUNK), F32),
                   S((CHUNK, 128), F32), S((8, d_a), F32)],
        name=name, compiler_params=_cp(("arbitrary",)))(proj, ln_g, ln_b, w_s, bst, norm_g, dcat)


def _silu_grad(c):
    s = jax.nn.sigmoid(c)
    return s * (1.0 + c * (1.0 - s))


def _fill_prev_main(s_ref, prev_ref, main_ref, i, tt):
    s_ref[pl.ds(0, HALO), :] = jnp.where(i > 0, prev_ref[...].astype(F32), 0.0)
    s_ref[pl.ds(HALO, tt), :] = main_ref[...].astype(F32)


def _taps(s_ref, w_ref, kw, rows):
    acc = w_ref[0:1, :] * s_ref[pl.ds(HALO - (kw - 1), rows), :]
    for k in range(1, kw):
        acc = acc + w_ref[k:k + 1, :] * s_ref[pl.ds(HALO - (kw - 1) + k, rows), :]
    return acc


def _prev_spec(tt, tc, joff):
    return pl.BlockSpec((HALO, tc), lambda j, i: (jnp.maximum(i * (tt // HALO) - 1, 0), j + joff))


def _next_spec(tt, tc, joff, t):
    return pl.BlockSpec((HALO, tc), lambda j, i: (jnp.minimum((i + 1) * (tt // HALO), t // HALO - 1), j + joff))


def _conv_ssm_fwd(proj, col0, w, b, name):
    t = proj.shape[0]
    kw, c = w.shape
    tt, tc = _pick(t, (512, 256, 128)), _pick(c, (512, 256, 128))
    joff = col0 // tc
    assert col0 % tc == 0

    def body(x_ref, xp_ref, w_ref, b_ref, o_ref, s_ref):
        _fill_prev_main(s_ref, xp_ref, x_ref, pl.program_id(1), tt)
        o_ref[...] = jax.nn.silu(_taps(s_ref, w_ref, kw, tt) + b_ref[...]).astype(o_ref.dtype)

    return pl.pallas_call(
        body, grid=(c // tc, t // tt),
        in_specs=[pl.BlockSpec((tt, tc), lambda j, i: (i, j + joff)), _prev_spec(tt, tc, joff),
                  pl.BlockSpec((kw, tc), lambda j, i: (0, j)), pl.BlockSpec((1, tc), lambda j, i: (0, j))],
        out_specs=pl.BlockSpec((tt, tc), lambda j, i: (i, j)), out_shape=S((t, c), XBC_DTYPE),
        scratch_shapes=[pltpu.VMEM((HALO + tt, tc), F32)], name=name,
        compiler_params=_cp(("parallel", "arbitrary")))(proj, proj, w, b)


def _conv_ssm_bwd(proj, col0, w, b, wcol0, dact, dproj, out_col0, name):
    t = proj.shape[0]
    kw = w.shape[0]
    c = dact.shape[1]
    tt, tc = _pick(t, (512, 256, 128)), _pick(c, (512, 256, 128))
    assert col0 % tc == 0 and wcol0 % tc == 0 and out_col0 % tc == 0
    joff, wj, oj = col0 // tc, wcol0 // tc, out_col0 // tc
    nt = t // tt

    def body(x_ref, xp_ref, xn_ref, w_ref, b_ref, d_ref, dn_ref, dp_in, dx_ref, dw_ref, db_ref, s_ref, sd_ref):
        del dp_in
        i = pl.program_id(1)

        @pl.when(i == 0)
        def _():
            dw_ref[...] = jnp.zeros_like(dw_ref)
            db_ref[...] = jnp.zeros_like(db_ref)

        _fill_prev_main(s_ref, xp_ref, x_ref, i, tt)
        s_ref[pl.ds(HALO + tt, HALO), :] = xn_ref[...].astype(F32)
        cpre = _taps(s_ref, w_ref, kw, tt + HALO) + b_ref[...]
        sd_ref[pl.ds(0, tt), :] = d_ref[...].astype(F32)
        sd_ref[pl.ds(tt, HALO), :] = jnp.where(i < nt - 1, dn_ref[...].astype(F32), 0.0)
        sd_ref[...] = sd_ref[...] * _silu_grad(cpre)
        dx = w_ref[kw - 1:kw, :] * sd_ref[pl.ds(0, tt), :]
        for k in range(kw - 1):
            dx = dx + w_ref[k:k + 1, :] * sd_ref[pl.ds(kw - 1 - k, tt), :]
        dx_ref[...] = dx.astype(dx_ref.dtype)
        dmain = sd_ref[pl.ds(0, tt), :]
        for k in range(kw):
            dw_ref[k:k + 1, :] += jnp.sum(dmain * s_ref[pl.ds(HALO - (kw - 1) + k, tt), :], axis=0, keepdims=True)
        db_ref[0:1, :] += jnp.sum(dmain, axis=0, keepdims=True)

    acc = pl.BlockSpec((8, tc), lambda j, i: (0, j))
    return pl.pallas_call(
        body, grid=(c // tc, nt),
        in_specs=[pl.BlockSpec((tt, tc), lambda j, i: (i, j + joff)), _prev_spec(tt, tc, joff), _next_spec(tt, tc, joff, t),
                  pl.BlockSpec((kw, tc), lambda j, i: (0, j + wj)), pl.BlockSpec((1, tc), lambda j, i: (0, j + wj)),
                  pl.BlockSpec((tt, tc), lambda j, i: (i, j)), _next_spec(tt, tc, 0, t),
                  pl.BlockSpec(memory_space=pl.ANY)],
        out_specs=[pl.BlockSpec((tt, tc), lambda j, i: (i, j + oj)), acc, acc],
        out_shape=[S(dproj.shape, dproj.dtype), S((8, c), F32), S((8, c), F32)],
        scratch_shapes=[pltpu.VMEM((HALO + tt + HALO, tc), F32), pltpu.VMEM((tt + HALO, tc), F32)],
        input_output_aliases={7: 0}, name=name,
        compiler_params=_cp(("parallel", "arbitrary")))(proj, proj, proj, w, b, dact, dact, dproj)


def _conv_ffn_fwd(hid, w, b, name):
    t, f2 = hid.shape
    f = f2 // 2
    kw = w.shape[0]
    tt, tc = _pick(t, (512, 256, 128)), _pick(f, (512, 256, 128))
    nj = f // tc

    def body(g_ref, gp_ref, u_ref, up_ref, wg_ref, wu_ref, bg_ref, bu_ref, o_ref, sg_ref, su_ref):
        i = pl.program_id(1)
        _fill_prev_main(sg_ref, gp_ref, g_ref, i, tt)
        _fill_prev_main(su_ref, up_ref, u_ref, i, tt)
        gate = _taps(sg_ref, wg_ref, kw, tt) + bg_ref[...]
        up = _taps(su_ref, wu_ref, kw, tt) + bu_ref[...]
        o_ref[...] = (jax.nn.silu(gate) * up).astype(o_ref.dtype)

    return pl.pallas_call(
        body, grid=(nj, t // tt),
        in_specs=[pl.BlockSpec((tt, tc), lambda j, i: (i, j)), _prev_spec(tt, tc, 0),
                  pl.BlockSpec((tt, tc), lambda j, i: (i, j + nj)), _prev_spec(tt, tc, nj),
                  pl.BlockSpec((kw, tc), lambda j, i: (0, j)), pl.BlockSpec((kw, tc), lambda j, i: (0, j + nj)),
                  pl.BlockSpec((1, tc), lambda j, i: (0, j)), pl.BlockSpec((1, tc), lambda j, i: (0, j + nj))],
        out_specs=pl.BlockSpec((tt, tc), lambda j, i: (i, j)), out_shape=S((t, f), ACT_DTYPE),
        scratch_shapes=[pltpu.VMEM((HALO + tt, tc), F32), pltpu.VMEM((HALO + tt, tc), F32)], name=name,
        compiler_params=_cp(("parallel", "arbitrary")))(hid, hid, hid, hid, w, w, b, b)


def _conv_ffn_bwd(hid, w, b, dact, name):
    t, f2 = hid.shape
    f = f2 // 2
    kw = w.shape[0]
    tt, tc = _pick(t, (512, 256, 128)), _pick(f, (512, 256, 128))
    nj = f // tc
    nt = t // tt

    def body(g_ref, gp_ref, gn_ref, u_ref, up_ref, un_ref, wg_ref, wu_ref, bg_ref, bu_ref, d_ref, dn_ref,
             dh_ref, dw_ref, db_ref, sg_ref, su_ref, dg_ref, du_ref):
        i = pl.program_id(1)

        @pl.when(i == 0)
        def _():
            dw_ref[...] = jnp.zeros_like(dw_ref)
            db_ref[...] = jnp.zeros_like(db_ref)

        _fill_prev_main(sg_ref, gp_ref, g_ref, i, tt)
        sg_ref[pl.ds(HALO + tt, HALO), :] = gn_ref[...].astype(F32)
        _fill_prev_main(su_ref, up_ref, u_ref, i, tt)
        su_ref[pl.ds(HALO + tt, HALO), :] = un_ref[...].astype(F32)
        gate = _taps(sg_ref, wg_ref, kw, tt + HALO) + bg_ref[...]
        up = _taps(su_ref, wu_ref, kw, tt + HALO) + bu_ref[...]
        dg_ref[pl.ds(0, tt), :] = d_ref[...].astype(F32)
        dg_ref[pl.ds(tt, HALO), :] = jnp.where(i < nt - 1, dn_ref[...].astype(F32), 0.0)
        dact_e = dg_ref[...]
        sg = jax.nn.sigmoid(gate)
        du_ref[...] = dact_e * (gate * sg)
        dg_ref[...] = dact_e * up * (sg * (1.0 + gate * (1.0 - sg)))
        for s, (sd_ref, sx_ref, w_ref) in enumerate(((dg_ref, sg_ref, wg_ref), (du_ref, su_ref, wu_ref))):
            dx = w_ref[kw - 1:kw, :] * sd_ref[pl.ds(0, tt), :]
            for k in range(kw - 1):
                dx = dx + w_ref[k:k + 1, :] * sd_ref[pl.ds(kw - 1 - k, tt), :]
            dh_ref[s] = dx.astype(dh_ref.dtype)
            dmain = sd_ref[pl.ds(0, tt), :]
            for k in range(kw):
                dw_ref[s, k:k + 1, :] += jnp.sum(dmain * sx_ref[pl.ds(HALO - (kw - 1) + k, tt), :], axis=0, keepdims=True)
            db_ref[s, 0:1, :] += jnp.sum(dmain, axis=0, keepdims=True)

    acc = pl.BlockSpec((2, 8, tc), lambda j, i: (0, 0, j))
    ext = pltpu.VMEM((HALO + tt + HALO, tc), F32)
    dsc = pltpu.VMEM((tt + HALO, tc), F32)
    return pl.pallas_call(
        body, grid=(nj, nt),
        in_specs=[pl.BlockSpec((tt, tc), lambda j, i: (i, j)), _prev_spec(tt, tc, 0), _next_spec(tt, tc, 0, t),
                  pl.BlockSpec((tt, tc), lambda j, i: (i, j + nj)), _prev_spec(tt, tc, nj), _next_spec(tt, tc, nj, t),
                  pl.BlockSpec((kw, tc), lambda j, i: (0, j)), pl.BlockSpec((kw, tc), lambda j, i: (0, j + nj)),
                  pl.BlockSpec((1, tc), lambda j, i: (0, j)), pl.BlockSpec((1, tc), lambda j, i: (0, j + nj)),
                  pl.BlockSpec((tt, tc), lambda j, i: (i, j)), _next_spec(tt, tc, 0, t)],
        out_specs=[pl.BlockSpec((2, tt, tc), lambda j, i: (0, i, j)), acc, acc],
        out_shape=[S((2, t, f), MXU_DTYPE), S((2, 8, f), F32), S((2, 8, f), F32)],
        scratch_shapes=[ext, ext, dsc, dsc], name=name,
        compiler_params=_cp(("parallel", "arbitrary")))(hid, hid, hid, hid, hid, hid, w, w, b, b, dact, dact)


SSD_SUB = 2


def _ssd_chunk(xs, bm, cm, dtraw, z, hin, bias, alog, dskip, normg):
    n = bm.shape[0]
    row = lax.broadcasted_iota(jnp.int32, (n, n), 0)
    col = lax.broadcasted_iota(jnp.int32, (n, n), 1)
    causal = row >= col
    tri = causal.astype(F32)
    lane = lax.broadcasted_iota(jnp.int32, (1, 128), 1)
    sub = lax.broadcasted_iota(jnp.int32, (128, 1), 0)
    last = (lax.broadcasted_iota(jnp.int32, (n, 1), 0) == n - 1).astype(F32)
    dt = jax.nn.softplus(dtraw + bias)
    adt = dt * (-jnp.exp(alog))
    acs = jnp.dot(tri, adt, preferred_element_type=F32, precision=lax.Precision.HIGHEST)
    ch = lax.broadcasted_iota(jnp.int32, (128, GROUP_CH), 1)
    hd = lax.broadcasted_iota(jnp.int32, (128, GROUP_CH), 0) * HEAD_DIM
    expand = ((ch >= hd) & (ch < hd + HEAD_DIM)).astype(F32)
    acs_x = jnp.dot(acs, expand, preferred_element_type=F32, precision=lax.Precision.HIGHEST)
    alast_x = jnp.sum(acs_x * last, axis=0, keepdims=True)
    acs_t = acs.T
    scores = lax.dot_general(_mx(cm), _mx(bm), (((1,), (1,)), ((), ())), preferred_element_type=F32)
    yds, xts = [], []
    for r in range(HEADS_PER_GROUP):
        pick = (lane == r).astype(F32)
        acol = jnp.sum(acs * pick, axis=1, keepdims=True)
        arow = jnp.sum(acs_t * (sub == r).astype(F32), axis=0, keepdims=True)
        dtc = jnp.sum(dt * pick, axis=1, keepdims=True)
        lm = jnp.exp(jnp.where(causal, acol - arow, -1e30))
        xts.append(xs[r] * dtc)
        yds.append(jnp.dot(_mx(scores * lm), _mx(xts[r]), preferred_element_type=F32))
    xt = jnp.concatenate(xts, axis=1)
    yo = jnp.exp(acs_x) * jnp.dot(_mx(cm), _mx(hin), preferred_element_type=F32)
    st = lax.dot_general(_mx(bm), _mx(xt * jnp.exp(alast_x - acs_x)), (((0,), (0,)), ((), ())), preferred_element_type=F32)
    hout = jnp.exp(alast_x) * hin + st
    y = (jnp.concatenate(yds, axis=1) + yo + dskip * jnp.concatenate(xs, axis=1)) * jax.nn.silu(z)
    return _rms(y, normg), hout


def _ssd_block(datas, hin, consts):
    ys = []
    for data in datas:
        y, hin = _ssd_chunk(*data, hin, *consts)
        ys.append(y)
    return ys, hin


def _ssd_specs(d_ssm, ngrp, zcol0, rev, nb):
    cc = (lambda c: nb - 1 - c) if rev else (lambda c: c)
    rows = SSD_SUB * CHUNK
    xj, bj, cj, zj = 0, d_ssm // 128, d_ssm // 128 + ngrp, zcol0 // GROUP_CH
    const = lambda w: pl.BlockSpec((None, 8, w), lambda g, c: (g, 0, 0))
    return cc, [
        pl.BlockSpec((rows, GROUP_CH), lambda g, c: (cc(c), xj + g)),
        pl.BlockSpec((rows, 128), lambda g, c: (cc(c), bj + g)),
        pl.BlockSpec((rows, 128), lambda g, c: (cc(c), cj + g)),
        pl.BlockSpec((rows, 128), lambda g, c: (cc(c), g)),
        pl.BlockSpec((rows, GROUP_CH), lambda g, c: (cc(c), zj + g)),
        const(128), const(128), const(GROUP_CH), const(GROUP_CH)]


def _sub_rows(s):
    return slice(CHUNK * s, CHUNK * (s + 1))


def _ssd_load(x_ref, b_ref, c_ref, dt_ref, z_ref, bias_ref, alog_ref, dsk_ref, ng_ref):
    datas = []
    for s in range(SSD_SUB):
        rows = _sub_rows(s)
        xs = [x_ref[rows, HEAD_DIM * r:HEAD_DIM * (r + 1)].astype(F32) for r in range(HEADS_PER_GROUP)]
        datas.append((xs, b_ref[rows, :].astype(F32), c_ref[rows, :].astype(F32), dt_ref[rows, :],
                      z_ref[rows, :].astype(F32)))
    return datas, (bias_ref[0:1, :], alog_ref[0:1, :], dsk_ref[0:1, :], ng_ref[0:1, :])


def _ssd_fwd(xbc, dtp, proj, zcol0, bias_p, alog_p, dskip_x, normg_x, cat, name):
    t = xbc.shape[0]
    ngrp = bias_p.shape[0]
    d_ssm = ngrp * GROUP_CH
    rows = SSD_SUB * CHUNK
    nb = t // rows
    d_a = cat.shape[1] - d_ssm
    assert d_a % GROUP_CH == 0 and zcol0 % GROUP_CH == 0 and t % rows == 0
    _, specs = _ssd_specs(d_ssm, ngrp, zcol0, False, nb)

    def body(x_ref, b_ref, c_ref, dt_ref, z_ref, bias_ref, alog_ref, dsk_ref, ng_ref, cat_in, y_ref, hs_ref, h_ref):
        del cat_in

        @pl.when(pl.program_id(1) == 0)
        def _():
            h_ref[...] = jnp.zeros_like(h_ref)

        datas, consts = _ssd_load(x_ref, b_ref, c_ref, dt_ref, z_ref, bias_ref, alog_ref, dsk_ref, ng_ref)
        hs_ref[...] = h_ref[...]
        ys, hout = _ssd_block(datas, h_ref[...], consts)
        for s in range(SSD_SUB):
            y_ref[_sub_rows(s), :] = ys[s].astype(y_ref.dtype)
        h_ref[...] = hout

    return pl.pallas_call(
        body, grid=(ngrp, nb), in_specs=specs + [pl.BlockSpec(memory_space=pl.ANY)],
        out_specs=[pl.BlockSpec((rows, GROUP_CH), lambda g, c: (c, d_a // GROUP_CH + g)),
                   pl.BlockSpec((None, None, D_STATE, GROUP_CH), lambda g, c: (c, g, 0, 0))],
        out_shape=[S(cat.shape, cat.dtype), S((nb, ngrp, D_STATE, GROUP_CH), F32)],
        scratch_shapes=[pltpu.VMEM((D_STATE, GROUP_CH), F32)],
        input_output_aliases={9: 0}, name=name,
        compiler_params=_cp(("parallel", "arbitrary")))(xbc, xbc, xbc, dtp, proj, bias_p, alog_p, dskip_x, normg_x, cat)


def _ssd_bwd(xbc, dtp, proj, zcol0, bias_p, alog_p, dskip_x, normg_x, hs, dcat, dproj, name):
    t = xbc.shape[0]
    ngrp = bias_p.shape[0]
    d_ssm = ngrp * GROUP_CH
    rows = SSD_SUB * CHUNK
    nb = t // rows
    d_a = dcat.shape[1] - d_ssm
    cc, specs = _ssd_specs(d_ssm, ngrp, zcol0, True, nb)

    def body(x_ref, b_ref, c_ref, dt_ref, z_ref, bias_ref, alog_ref, dsk_ref, ng_ref, hs_ref, dy_ref, dp_in,
             dz_ref, dx_ref, db_ref, dc_ref, ddt_ref, dbias_ref, dalog_ref, ddsk_ref, dng_ref, dh_ref):
        del dp_in

        @pl.when(pl.program_id(1) == 0)
        def _():
            dh_ref[...] = jnp.zeros_like(dh_ref)
            for r in (dbias_ref, dalog_ref, ddsk_ref, dng_ref):
                r[...] = jnp.zeros_like(r)

        datas, consts = _ssd_load(x_ref, b_ref, c_ref, dt_ref, z_ref, bias_ref, alog_ref, dsk_ref, ng_ref)
        _, vjp = jax.vjp(_ssd_block, datas, hs_ref[...], consts)
        dys = [dy_ref[_sub_rows(s), :].astype(F32) for s in range(SSD_SUB)]
        ddatas, dhin, (dbias, dalog, ddsk, dng) = vjp((dys, dh_ref[...]))
        for s, (dxs, dbm, dcm, ddt, dz) in enumerate(ddatas):
            rws = _sub_rows(s)
            for r in range(HEADS_PER_GROUP):
                dx_ref[rws, HEAD_DIM * r:HEAD_DIM * (r + 1)] = dxs[r].astype(dx_ref.dtype)
            db_ref[rws, :] = dbm.astype(db_ref.dtype)
            dc_ref[rws, :] = dcm.astype(dc_ref.dtype)
            ddt_ref[rws, :] = ddt
            dz_ref[rws, :] = dz.astype(dz_ref.dtype)
        dh_ref[...] = dhin
        dbias_ref[0:1, :] += dbias
        dalog_ref[0:1, :] += dalog
        ddsk_ref[0:1, :] += ddsk
        dng_ref[0:1, :] += dng

    acc = lambda w: pl.BlockSpec((None, 8, w), lambda g, c: (g, 0, 0))
    blk = lambda w: pl.BlockSpec((rows, w), lambda g, c: (cc(c), g))
    return pl.pallas_call(
        body, grid=(ngrp, nb),
        in_specs=specs + [pl.BlockSpec((None, None, D_STATE, GROUP_CH), lambda g, c: (cc(c), g, 0, 0)),
                          pl.BlockSpec((rows, GROUP_CH), lambda g, c: (cc(c), d_a // GROUP_CH + g)),
                          pl.BlockSpec(memory_space=pl.ANY)],
        out_specs=[pl.BlockSpec((rows, GROUP_CH), lambda g, c: (cc(c), zcol0 // GROUP_CH + g)),
                   blk(GROUP_CH), blk(128), blk(128), blk(128), acc(128), acc(128), acc(GROUP_CH), acc(GROUP_CH)],
        out_shape=[S(dproj.shape, dproj.dtype), S((t, d_ssm), XBC_DTYPE), S((t, ngrp * 128), XBC_DTYPE),
                   S((t, ngrp * 128), XBC_DTYPE), S((t, ngrp * 128), F32), S((ngrp, 8, 128), F32),
                   S((ngrp, 8, 128), F32), S((ngrp, 8, GROUP_CH), F32), S((ngrp, 8, GROUP_CH), F32)],
        scratch_shapes=[pltpu.VMEM((D_STATE, GROUP_CH), F32)],
        input_output_aliases={11: 0}, name=name,
        compiler_params=_cp(("parallel", "arbitrary")))(xbc, xbc, xbc, dtp, proj, bias_p, alog_p, dskip_x, normg_x, hs, dcat, dproj)


def _adamw(parts, w, m, v, name):
    r, c = w.shape
    tr = _pick(r, (256, 128, 64, 32, 16, 8)) if c * 4 * 256 <= 4 * 1024 * 1024 else _pick(r, (64, 32, 16, 8))

    def body(p_ref, w_ref, m_ref, v_ref, g_ref, d_ref, nm_ref, nv_ref):
        g = p_ref[0].astype(F32)
        for k in range(1, N_DEV):
            g = g + p_ref[k].astype(F32)
        mm = ADAM_B1 * m_ref[...] + (1.0 - ADAM_B1) * g
        vv = ADAM_B2 * v_ref[...] + (1.0 - ADAM_B2) * jnp.square(g)
        m_hat = mm / (1.0 - ADAM_B1 ** ADAM_STEP)
        v_hat = vv / (1.0 - ADAM_B2 ** ADAM_STEP)
        g_ref[...] = g
        d_ref[...] = -ADAM_LR * (m_hat / (jnp.sqrt(v_hat) + ADAM_EPS) + ADAM_WD * w_ref[...])
        nm_ref[...] = mm
        nv_ref[...] = vv

    blk = pl.BlockSpec((tr, c), lambda i: (i, 0))
    return pl.pallas_call(
        body, grid=(r // tr,),
        in_specs=[pl.BlockSpec((N_DEV, tr, c), lambda i: (0, i, 0)), blk, blk, blk],
        out_specs=[blk, blk, blk, blk], out_shape=[S((r, c), F32)] * 4, name=name,
        compiler_params=_cp(("parallel",)))(parts, w, m, v)


def _mesh_pos():
    return lax.axis_index("x"), lax.axis_index("y"), lax.axis_index("c")


def _peer(d, x, y, c):
    return (1 - x if (d >> 2) & 1 else x, 1 - y if (d >> 1) & 1 else y, 1 - c if d & 1 else c)


def _xfer_start(items, name, after=None):
    n = len(items)
    kinds = [k for k, _ in items]
    srcs = [pltpu.with_memory_space_constraint(a, pltpu.HBM) for _, a in items]
    land_shapes = [((N_DEV,) + a.shape if k == 'gather' else a.shape, a.dtype) for k, a in items]
    lands = [pltpu.with_memory_space_constraint(lax.empty(s, dt), pltpu.HBM) for s, dt in land_shapes]
    extra = [] if after is None else [after]

    def body(*refs):
        src_refs, land_refs = refs[:n], refs[n:2 * n]
        outs = refs[2 * n + len(extra):]
        sems = outs[:2 * n]
        token = outs[4 * n]
        x, y, c = _mesh_pos()
        me = 4 * x + 2 * y + c
        for a in range(n):
            for d in range(1, N_DEV):
                px, py, pc = _peer(d, x, y, c)
                src = src_refs[a] if kinds[a] == 'gather' else src_refs[a].at[4 * px + 2 * py + pc]
                pltpu.make_async_remote_copy(
                    src_ref=src, dst_ref=land_refs[a].at[me], send_sem=sems[2 * a].at[d - 1],
                    recv_sem=sems[2 * a + 1].at[d - 1], device_id=(px, py, pc),
                    device_id_type=pl.DeviceIdType.MESH).start()
        token[...] = jnp.zeros_like(token)

    hbm = pl.BlockSpec(memory_space=pltpu.HBM)
    sem = pl.BlockSpec(memory_space=pltpu.SEMAPHORE)
    out_shape = ([pltpu.SemaphoreType.DMA((N_DEV - 1,))] * (2 * n)
                 + [pltpu.HBM(a.shape, a.dtype) for a in srcs] + [pltpu.HBM(s, dt) for s, dt in land_shapes]
                 + [S((8, 128), F32)])
    res = pl.pallas_call(
        body, name=name, out_shape=out_shape,
        in_specs=[hbm] * (2 * n) + [pl.BlockSpec(memory_space=pl.ANY)] * len(extra),
        out_specs=[sem] * (2 * n) + [hbm] * (2 * n) + [pl.BlockSpec(memory_space=pltpu.VMEM)],
        input_output_aliases={**{a: 2 * n + a for a in range(n)}, **{n + a: 3 * n + a for a in range(n)}},
        compiler_params=pltpu.CompilerParams(has_side_effects=pltpu.SideEffectType.DATAFLOW_SIDE_EFFECTING),
    )(*srcs, *lands, *extra)
    return (kinds, res[:2 * n], res[2 * n:3 * n], res[3 * n:4 * n]), res[4 * n]


def _xfer_wait(handle, after, name):
    kinds, sems, src_thru, land_thru = handle
    n = len(kinds)

    def body(*refs):
        land_refs = refs[n:2 * n]
        sem_refs = refs[2 * n:4 * n]
        x, y, c = _mesh_pos()
        me = 4 * x + 2 * y + c
        for a in range(n):
            for d in range(1, N_DEV):
                slab = land_refs[a].at[me]
                cp = pltpu.make_async_remote_copy(
                    src_ref=slab, dst_ref=slab, send_sem=sem_refs[2 * a].at[d - 1], recv_sem=sem_refs[2 * a + 1].at[d - 1],
                    device_id=_peer(d, x, y, c), device_id_type=pl.DeviceIdType.MESH)
                cp.wait_send()
                cp.wait_recv()

    hbm = pl.BlockSpec(memory_space=pltpu.HBM)
    sem = pl.BlockSpec(memory_space=pltpu.SEMAPHORE)
    res = pl.pallas_call(
        body, name=name,
        out_shape=[pltpu.HBM(a.shape, a.dtype) for a in src_thru] + [pltpu.HBM(a.shape, a.dtype) for a in land_thru],
        in_specs=[hbm] * (2 * n) + [sem] * (2 * n) + [pl.BlockSpec(memory_space=pl.ANY)],
        out_specs=[hbm] * (2 * n), input_output_aliases={a: a for a in range(2 * n)},
        compiler_params=pltpu.CompilerParams(has_side_effects=pltpu.SideEffectType.DATAFLOW_SIDE_EFFECTING),
    )(*src_thru, *land_thru, *sems, after)
    x, y, c = _mesh_pos()
    me = 4 * x + 2 * y + c
    out = []
    for a in range(n):
        src = res[a]
        own = src[None] if kinds[a] == 'gather' else lax.dynamic_index_in_dim(src, me, 0, keepdims=True)
        out.append(lax.dynamic_update_index_in_dim(res[n + a], own, me, 0))
    return out


def _stack_to_full(kind, st):
    if kind == 'row':
        return st.reshape(st.shape[0] * st.shape[1], st.shape[2])
    return jnp.transpose(st, (1, 0, 2)).reshape(st.shape[1], st.shape[0] * st.shape[2])


def _full_to_stack(kind, full):
    r, c = full.shape
    if kind == 'row':
        return full.reshape(N_DEV, r // N_DEV, c)
    return jnp.transpose(full.reshape(r, N_DEV, c // N_DEV), (1, 0, 2))


def _pack_small(named):
    layout = [(a.shape, a.size, -(-a.size // 1024) * 8) for a in named]
    total = sum(nr for _, _, nr in layout) * 128
    packed, off = None, 0
    for a, (_, n, nr) in zip(named, layout):
        part = jnp.pad(a.reshape(-1).astype(F32), (off, total - off - n))
        packed = part if packed is None else packed + part
        off += nr * 128
    return packed.reshape(total // 128, 128), layout


def _unpack_small(packed, layout):
    out, r0 = [], 0
    for shape, n, nr in layout:
        out.append(packed[r0:r0 + nr].reshape(-1)[:n].reshape(shape))
        r0 += nr
    return out


def _row0(acc):
    return acc[0]


def _local_step(x, p, tgt, sm, comm):
    t, d = x.shape
    h_n = sm['dt_bias'].shape[-1]
    ngrp = h_n // HEADS_PER_GROUP
    d_ssm = h_n * HEAD_DIM
    d_a = sm['ln_a_g'].shape[-1]
    d_mix = d_a + d_ssm
    d_xbc = sm['conv_ssm_b'].shape[-1]
    d_main = 2 * d_a + d_ssm + d_xbc
    assert d_xbc == d_ssm + 2 * ngrp * D_STATE and h_n <= 128
    zcol0, xcol0 = 2 * d_a, 2 * d_a + d_ssm
    vec = lambda v: v.reshape(1, -1)

    bst = jnp.pad(sm['b_s'].T, ((0, 0), (0, 128 - sm['b_s'].shape[0])))
    grp = lambda v, w: jnp.broadcast_to(jnp.pad(v.reshape(ngrp, 1, -1), ((0, 0), (0, 0), (0, w - v.size // ngrp))), (ngrp, 8, w))
    bias_p, alog_p = grp(sm['dt_bias'], 128), grp(sm['a_log'], 128)
    dskip_x = grp(jnp.repeat(sm['d_skip'], HEAD_DIM), GROUP_CH)
    normg_x = grp(sm['ssm_norm_g'], GROUP_CH)
    pad_dt = lambda v: jnp.pad(v[:, :h_n].reshape(t, ngrp, HEADS_PER_GROUP),
                               ((0, 0), (0, 0), (0, 128 - HEADS_PER_GROUP))).reshape(t, ngrp * 128)

    g_mix = vec(sm['norm_mix_g']) + comm.tok0
    a_n = _rms_fwd(x, g_mix, "rms_mix")
    wf = comm.weights('a', a_n)
    w_main = wf['w_in'][:, :d_main]
    w_dt = jnp.pad(wf['w_in'][:, d_main:], ((0, 0), (0, 128 - h_n)))
    proj = _mm_nn(a_n, w_main, out_dtype=ACT_DTYPE, name="mm_in")
    dtp = pad_dt(_mm_nn(a_n, w_dt, out_dtype=F32, name="mm_dt"))
    cat = _gmlp_fwd(proj, vec(sm['ln_a_g']), vec(sm['ln_a_b']), sm['w_s'], bst, vec(sm['norm_a_g']), d_mix, "gmlp_fwd")
    xbc = _conv_ssm_fwd(proj, xcol0, wf['conv_ssm_w'], vec(sm['conv_ssm_b']), "conv_ssm_fwd")
    cat, hs = _ssd_fwd(xbc, dtp, proj, zcol0, bias_p, alog_p, dskip_x, normg_x, cat, "ssd_fwd")
    wf.update(comm.weights('b', hs))
    h1 = _mm_nn(cat, wf['w_out'], out_dtype=F32, name="mm_out", res=x)
    f_n = _rms_fwd(h1, vec(sm['norm_ffn_g']), "rms_ffn")
    hid = _mm_nn(f_n, wf['w_up'], out_dtype=ACT_DTYPE, name="mm_up")
    act = _conv_ffn_fwd(hid, wf['conv_ffn_w'], vec(sm['conv_ffn_b']), "conv_ffn_fwd")
    h2 = _mm_nn(act, wf['w_down'], out_dtype=F32, name="mm_down", res=h1)
    r_n = _rms_fwd(h2, vec(sm['norm_ple_g']), "rms_ple")
    q = _mm_nn(r_n, wf['w_ple_gate'], out_dtype=F32, name="mm_pg")
    p_m = p.astype(MXU_DTYPE)
    pe = _mm_nn(p_m, wf['w_ple'], out_dtype=F32, name="mm_ple")

    loss, dh3, dq, dpe, dgf = _head(h2, q, pe, tgt, vec(sm['norm_final_g']), "head")
    wgrad = lambda act_t, g, name, **kw: _mm_nn(act_t, g, out_dtype=WIRE_DTYPE, name=name, wide=True, **kw)
    gs = {}
    gs['norm_final_g'] = _row0(dgf)
    g_ple = wgrad(p_m.T, dpe, "wg_ple")
    g_pg = wgrad(r_n.T, dq, "wg_pg")
    dr = _mm_nt(dq, wf['w_ple_gate'], out_dtype=F32, name="dg_pg")
    dh2, dh2m, dg = _rms_bwd(h2, vec(sm['norm_ple_g']), dr, dh3, "rms_ple_bwd")
    gs['norm_ple_g'] = _row0(dg)
    g_down = wgrad(act.T, dh2m, "wg_down")
    tok = comm.send('1', {'w_ple': g_ple, 'w_ple_gate': g_pg, 'w_down': g_down})
    dact = _mm_nt(dh2m, wf['w_down'], out_dtype=ACT_DTYPE, name="dg_down")
    dhid, dcw, dcb = _conv_ffn_bwd(hid, wf['conv_ffn_w'], vec(sm['conv_ffn_b']) + tok, dact, "conv_ffn_bwd")
    kf = wf['conv_ffn_w'].shape[0]
    g_cf = jnp.concatenate([dcw[0, :kf], dcw[1, :kf]], axis=1)
    gs['conv_ffn_b'] = jnp.concatenate([dcb[0, 0], dcb[1, 0]], axis=0)
    g_up = wgrad(f_n.T, dhid, "wg_up", b_split=2)
    df = _mm_nt(dhid, wf['w_up'], out_dtype=F32, name="dg_up", a_split=2)
    dh1, dh1m, dg = _rms_bwd(h1, vec(sm['norm_ffn_g']), df, dh2, "rms_ffn_bwd")
    gs['norm_ffn_g'] = _row0(dg)
    g_out = wgrad(cat.T, dh1m, "wg_out")
    tok = comm.send('2', {'conv_ffn_w': g_cf, 'w_up': g_up, 'w_out': g_out})
    dcat = _mm_nt(dh1m, wf['w_out'], out_dtype=ACT_DTYPE, name="dg_out")
    dproj, dlng, dlnb, dws, dbst, dng = _gmlp_bwd(proj, vec(sm['ln_a_g']) + tok, vec(sm['ln_a_b']), sm['w_s'], bst,
                                                  vec(sm['norm_a_g']), dcat, d_main, "gmlp_bwd")
    gs['ln_a_g'], gs['ln_a_b'], gs['w_s'], gs['norm_a_g'] = _row0(dlng), _row0(dlnb), dws, _row0(dng)
    gs['b_s'] = dbst[:, :sm['b_s'].shape[0]].T
    dproj, dxs, dbm, dcm, ddtp, dbias, dalog, ddsk, dsng = _ssd_bwd(
        xbc, dtp, proj, zcol0, bias_p, alog_p, dskip_x, normg_x, hs, dcat, dproj, "ssd_bwd")
    gs['dt_bias'] = dbias[:, 0, :HEADS_PER_GROUP].reshape(h_n)
    gs['a_log'] = dalog[:, 0, :HEADS_PER_GROUP].reshape(h_n)
    gs['d_skip'] = ddsk[:, 0, :].reshape(h_n, HEAD_DIM).sum(axis=-1)
    gs['ssm_norm_g'] = dsng[:, 0, :].reshape(d_ssm)
    dws_c, dbs_c = [], []
    off = 0
    for nm, dpart in (("x", dxs), ("b", dbm), ("c", dcm)):
        dproj, dw_c, db_c = _conv_ssm_bwd(proj, xcol0 + off, wf['conv_ssm_w'], vec(sm['conv_ssm_b']), off, dpart, dproj,
                                          xcol0 + off, "conv_ssm_bwd_" + nm)
        dws_c.append(dw_c[:wf['conv_ssm_w'].shape[0]])
        dbs_c.append(db_c[0])
        off += dpart.shape[1]
    g_cs = jnp.concatenate(dws_c, axis=1)
    gs['conv_ssm_b'] = jnp.concatenate(dbs_c, axis=0)
    ddt = jnp.pad(ddtp.reshape(t, ngrp, 128)[:, :, :HEADS_PER_GROUP].reshape(t, h_n), ((0, 0), (0, 128 - h_n))).astype(MXU_DTYPE)
    a_t = a_n.T
    g_in = jnp.concatenate([wgrad(a_t, dproj, "wg_in"), wgrad(a_t, ddt, "wg_dt")[:, :h_n]], axis=1)
    tok = comm.send('3', {'conv_ssm_w': g_cs, 'w_in': g_in}, [(n, gs[n]) for n in REPLICATED if n != 'norm_mix_g'])
    da = _mm_nt(ddt + tok.astype(ddt.dtype), w_dt, out_dtype=F32, name="dg_dt")
    da = _mm_nt(dproj, w_main, out_dtype=F32, name="dg_in", res=da)
    dx, _, dg = _rms_bwd(x, g_mix + tok, da, dh1, "rms_mix_bwd")
    comm.send('4', {}, [('norm_mix_g', _row0(dg)), ('loss', loss[0, 0:1])])
    return dx


def kernel(x, p, norm_mix_g, w_in, ln_a_g, ln_a_b, w_s, b_s, norm_a_g, conv_ssm_w, conv_ssm_b, dt_bias, a_log, d_skip, ssm_norm_g, w_out, norm_ffn_g, w_up, conv_ffn_w, conv_ffn_b, w_down, norm_ple_g, w_ple_gate, w_ple, norm_final_g, loss_target, m_norm_mix_g, m_w_in, m_ln_a_g, m_ln_a_b, m_w_s, m_b_s, m_norm_a_g, m_conv_ssm_w, m_conv_ssm_b, m_dt_bias, m_a_log, m_d_skip, m_ssm_norm_g, m_w_out, m_norm_ffn_g, m_w_up, m_conv_ffn_w, m_conv_ffn_b, m_w_down, m_norm_ple_g, m_w_ple_gate, m_w_ple, m_norm_final_g, v_norm_mix_g, v_w_in, v_ln_a_g, v_ln_a_b, v_w_s, v_b_s, v_norm_a_g, v_conv_ssm_w, v_conv_ssm_b, v_dt_bias, v_a_log, v_d_skip, v_ssm_norm_g, v_w_out, v_norm_ffn_g, v_w_up, v_conv_ffn_w, v_conv_ffn_b, v_w_down, v_norm_ple_g, v_w_ple_gate, v_w_ple, v_norm_final_g):
    given = dict(locals())
    wts = {n: given[n] for n in WEIGHTS}
    ms = {n: given["m_" + n] for n in WEIGHTS}
    vs = {n: given["v_" + n] for n in WEIGHTS}
    sm = {n: (wts[n][0] if wts[n].ndim > 1 else wts[n]) for n in REPLICATED}
    comm = _Comm({n: wts[n][0] for n in SHARDED})
    dx = _local_step(x[0], p[0, 0], loss_target[0], sm, comm)

    out, loss_out, after = {}, None, dx
    for tag, names, small_names, layout, handle in comm.sent:
        recv = _xfer_wait(handle, after, "grads_%s_wait" % tag)
        for n, parts in zip(names, recv):
            out[n] = _adamw(parts, wts[n][0], ms[n][0], vs[n][0], "adamw_" + n)
            after = out[n][1]
        if small_names:
            pick = lambda src, fill: _pack_small([src[n] if n in src else jnp.full((1,), fill, F32) for n in small_names])[0]
            res = _adamw(recv[-1], pick(wts, 0.0), pick(ms, 0.0), pick(vs, 1.0), "adamw_small_" + tag)
            res = [_unpack_small(o, layout) for o in res]
            after = res[1][0]
            for i, n in enumerate(small_names):
                if n == 'loss':
                    loss_out = res[0][i].reshape(())
                else:
                    out[n] = [res[k][i] for k in range(4)]
    return (loss_out, dx[None], *[out[n][k].reshape(wts[n].shape) for k in range(4) for n in WEIGHTS])


class _Comm:
    GATHER_GROUPS = {'a': ('w_in', 'conv_ssm_w'), 'b': ('w_out', 'w_up', 'conv_ffn_w', 'w_down', 'w_ple_gate', 'w_ple')}

    def __init__(self, blocks):
        self.handles, tok = {}, None
        for grp, names in self.GATHER_GROUPS.items():
            items = [('gather', blocks[n].astype(_wire(n))) for n in names]
            self.handles[grp], t = _xfer_start(items, "gather_%s_start" % grp, after=tok)
            tok = t
            self.tok0 = t[0, 0] if grp == 'a' else self.tok0 + t[0, 0]
        self.sent = []

    def weights(self, grp, after):
        stacks = _xfer_wait(self.handles[grp], after, "gather_%s_wait" % grp)
        return {n: _stack_to_full(SHARDED[n], st) for n, st in zip(self.GATHER_GROUPS[grp], stacks)}

    def send(self, tag, gw, small=None):
        items = [('scatter', _full_to_stack(SHARDED[n], g.astype(_wire(n)))) for n, g in gw.items()]
        layout, small_names = None, []
        if small:
            packed, layout = _pack_small([a for _, a in small])
            small_names = [n for n, _ in small]
            items.append(('gather', packed))
        handle, tok = _xfer_start(items, "grads_%s_start" % tag)
        self.sent.append((tag, list(gw), small_names, layout, handle))
        return tok[0, 0]


def _wire(name):
    return F32 if name in F32_ON_WIRE else WIRE_DTYPE
```

```python
import functools

import jax
import jax.numpy as jnp
from jax import lax
from jax.experimental import pallas as pl
from jax.experimental.pallas import tpu as pltpu

F32 = jnp.float32
MXU_DTYPE = jnp.bfloat16
ACT_DTYPE = jnp.bfloat16
XBC_DTYPE = jnp.bfloat16
WIRE_DTYPE = jnp.bfloat16
EPS = 1e-6
CHUNK = 128
D_STATE = 128
HEAD_DIM = 64
HEADS_PER_GROUP = 4
GROUP_CH = HEAD_DIM * HEADS_PER_GROUP
HALO = 16
N_DEV = 8
VMEM_LIMIT = 48 * 1024 * 1024

ADAM_LR = 0.001
ADAM_B1 = 0.9
ADAM_B2 = 0.999
ADAM_EPS = 1e-08
ADAM_WD = 0.01
ADAM_STEP = 10

WEIGHTS = ['norm_mix_g', 'w_in', 'ln_a_g', 'ln_a_b', 'w_s', 'b_s', 'norm_a_g', 'conv_ssm_w', 'conv_ssm_b', 'dt_bias',
           'a_log', 'd_skip', 'ssm_norm_g', 'w_out', 'norm_ffn_g', 'w_up', 'conv_ffn_w', 'conv_ffn_b', 'w_down',
           'norm_ple_g', 'w_ple_gate', 'w_ple', 'norm_final_g']
SHARDED = {'w_in': 'col', 'conv_ssm_w': 'col', 'w_out': 'row', 'w_up': 'col', 'conv_ffn_w': 'col', 'w_down': 'row',
           'w_ple_gate': 'row', 'w_ple': 'col'}
F32_ON_WIRE = ('conv_ssm_w', 'conv_ffn_w')
REPLICATED = [n for n in WEIGHTS if n not in SHARDED]

S = jax.ShapeDtypeStruct


def _pick(dim, cands):
    for c in cands:
        if c <= dim and dim % c == 0:
            return c
    return dim


def _cp(sem, vmem=VMEM_LIMIT):
    return pltpu.CompilerParams(dimension_semantics=sem, vmem_limit_bytes=vmem)


def _mx(v):
    return v.astype(MXU_DTYPE)


def _rms(v, g):
    return v * lax.rsqrt(jnp.mean(v * v, axis=-1, keepdims=True) + EPS) * g


MM_VMEM_BUDGET = 34 * 1024 * 1024


def _mm_tiles(m, n, k, out_bytes, has_res, tn_cands=(512, 256, 128), k_mult=1):
    tm = _pick(m, (1024, 512, 256, 128))
    tn = _pick(n, tn_cands)
    ks = k // k_mult
    for nk in range(1, ks // 128 + 1):
        if ks % nk or (ks // nk) % 128:
            continue
        tk = ks // nk
        need = 2 * 2 * (tm * tk + tk * tn) + tm * tn * (4 + 2 * out_bytes + (8 if has_res else 0))
        if need <= MM_VMEM_BUDGET:
            return tm, tn, tk
    return tm, tn, _pick(ks, (128,))


def _mm_body(dot, nk, has_res):
    def body(*refs):
        if has_res:
            a_ref, b_ref, r_ref, o_ref, acc_ref = refs
        else:
            a_ref, b_ref, o_ref, acc_ref = refs
            r_ref = None
        kk = pl.program_id(2)
        d = dot(a_ref[...], b_ref[...])

        def fin(acc):
            if r_ref is not None:
                acc = acc + r_ref[...]
            o_ref[...] = acc.astype(o_ref.dtype)

        if nk == 1:
            fin(d)
        else:
            @pl.when(kk == 0)
            def _():
                acc_ref[...] = d

            @pl.when(kk > 0)
            def _():
                acc_ref[...] += d

            @pl.when(kk == nk - 1)
            def _():
                fin(acc_ref[...])

    return body


def _mm_call(body, grid, a_spec, b_spec, tm, tn, m, n, out_dtype, name, args, res):
    in_specs = [a_spec, b_spec]
    if res is not None:
        in_specs.append(pl.BlockSpec((tm, tn), lambda i, j, kk: (i, j)))
        args = args + [res]
    return pl.pallas_call(
        body, grid=grid, in_specs=in_specs,
        out_specs=pl.BlockSpec((tm, tn), lambda i, j, kk: (i, j)), out_shape=S((m, n), out_dtype),
        scratch_shapes=[pltpu.VMEM((tm, tn), F32)], name=name,
        compiler_params=_cp(("parallel", "parallel", "arbitrary")))(*args)


def _mm_nn(a, b, *, out_dtype, name, res=None, b_split=1, wide=False):
    m, k = a.shape
    n = b.shape[1] if b_split == 1 else b.shape[2] * b_split
    tm, tn, tk = _mm_tiles(m, n // b_split, k, jnp.dtype(out_dtype).itemsize, res is not None,
                           tn_cands=(1024, 512, 256, 128) if wide else (512, 256, 128))
    nk = k // tk
    njs = (n // b_split) // tn
    body = _mm_body(lambda x, y: jnp.dot(x, y, preferred_element_type=F32), nk, res is not None)
    if b_split == 1:
        b_spec = pl.BlockSpec((tk, tn), lambda i, j, kk: (kk, j))
    else:
        b_spec = pl.BlockSpec((None, tk, tn), lambda i, j, kk: (j // njs, kk, j % njs))
    return _mm_call(body, (m // tm, n // tn, nk), pl.BlockSpec((tm, tk), lambda i, j, kk: (i, kk)), b_spec,
                    tm, tn, m, n, out_dtype, name, [a, b], res)


def _mm_nt(a, b, *, out_dtype, name, res=None, a_split=1):
    if a_split == 1:
        m, k = a.shape
    else:
        m, k = a.shape[1], a.shape[2] * a_split
    n = b.shape[0]
    tm, tn, tk = _mm_tiles(m, n, k, jnp.dtype(out_dtype).itemsize, res is not None, k_mult=a_split)
    nk = k // tk
    nks = nk // a_split
    body = _mm_body(lambda x, y: lax.dot_general(x, y, (((1,), (1,)), ((), ())), preferred_element_type=F32),
                    nk, res is not None)
    if a_split == 1:
        a_spec = pl.BlockSpec((tm, tk), lambda i, j, kk: (i, kk))
    else:
        a_spec = pl.BlockSpec((None, tm, tk), lambda i, j, kk: (kk // nks, i, kk % nks))
    return _mm_call(body, (m // tm, n // tn, nk), a_spec, pl.BlockSpec((tn, tk), lambda i, j, kk: (j, kk)),
                    tm, tn, m, n, out_dtype, name, [a, b], res)


def _rms_fwd(x, g, name):
    t, d = x.shape
    tr = _pick(t, (512, 256, 128))

    def body(x_ref, g_ref, o_ref):
        o_ref[...] = _rms(x_ref[...], g_ref[...]).astype(o_ref.dtype)

    return pl.pallas_call(
        body, grid=(t // tr,),
        in_specs=[pl.BlockSpec((tr, d), lambda i: (i, 0)), pl.BlockSpec((1, d), lambda i: (0, 0))],
        out_specs=pl.BlockSpec((tr, d), lambda i: (i, 0)), out_shape=S((t, d), ACT_DTYPE), name=name,
        compiler_params=_cp(("parallel",)))(x, g)


def _rms_bwd(xin, g, dn, dres, name):
    t, d = xin.shape
    tr = _pick(t, (256, 128))

    def body(x_ref, g_ref, dn_ref, dr_ref, dx_ref, dxm_ref, dg_ref):
        @pl.when(pl.program_id(0) == 0)
        def _():
            dg_ref[...] = jnp.zeros_like(dg_ref)

        _, vjp = jax.vjp(_rms, x_ref[...], g_ref[...])
        dx, dg = vjp(dn_ref[...].astype(F32))
        dx = dr_ref[...] + dx
        dx_ref[...] = dx
        dxm_ref[...] = dx.astype(dxm_ref.dtype)
        dg_ref[0:1, :] += dg

    row = pl.BlockSpec((tr, d), lambda i: (i, 0))
    return pl.pallas_call(
        body, grid=(t // tr,),
        in_specs=[row, pl.BlockSpec((1, d), lambda i: (0, 0)), row, row],
        out_specs=[row, row, pl.BlockSpec((8, d), lambda i: (0, 0))],
        out_shape=[S((t, d), F32), S((t, d), MXU_DTYPE), S((8, d), F32)], name=name,
        compiler_params=_cp(("arbitrary",)))(xin, g, dn, dres)


def _head(h2, q, pe, tgt, gf, name):
    t, d = h2.shape
    tr = _pick(t, (256, 128))

    def f(h2v, qv, pev, gfv, tv):
        h3 = h2v + jax.nn.sigmoid(qv) * pev
        y = _rms(h3, gfv)
        return 0.5 * jnp.sum(jnp.mean(jnp.square(y - tv), axis=-1))

    def body(h2_ref, q_ref, pe_ref, t_ref, g_ref, loss_ref, dh_ref, dq_ref, dpe_ref, dg_ref):
        @pl.when(pl.program_id(0) == 0)
        def _():
            loss_ref[...] = jnp.zeros_like(loss_ref)
            dg_ref[...] = jnp.zeros_like(dg_ref)

        tv = t_ref[...]
        loss, vjp = jax.vjp(lambda a, b, c, e: f(a, b, c, e, tv), h2_ref[...], q_ref[...].astype(F32),
                            pe_ref[...].astype(F32), g_ref[...])
        dh, dq, dpe, dg = vjp(jnp.ones((), F32))
        loss_ref[...] += jnp.full(loss_ref.shape, loss, F32)
        dh_ref[...] = dh
        dq_ref[...] = dq.astype(dq_ref.dtype)
        dpe_ref[...] = dpe.astype(dpe_ref.dtype)
        dg_ref[0:1, :] += dg

    row = pl.BlockSpec((tr, d), lambda i: (i, 0))
    return pl.pallas_call(
        body, grid=(t // tr,),
        in_specs=[row, row, row, row, pl.BlockSpec((1, d), lambda i: (0, 0))],
        out_specs=[pl.BlockSpec((8, 128), lambda i: (0, 0)), row, row, row, pl.BlockSpec((8, d), lambda i: (0, 0))],
        out_shape=[S((8, 128), F32), S((t, d), F32), S((t, d), MXU_DTYPE), S((t, d), MXU_DTYPE), S((8, d), F32)],
        name=name, compiler_params=_cp(("arbitrary",)))(h2, q, pe, tgt, gf)


def _gmlp_block(us, vs, lng, lnb, wss, bss, ng):
    n = us[0].shape[0]
    row = lax.broadcasted_iota(jnp.int32, (n, n), 0)
    col = lax.broadcasted_iota(jnp.int32, (n, n), 1)
    outs = []
    for u0, v0, lg, lb, ws, bs in zip(us, vs, lng, lnb, wss, bss):
        u = jax.nn.gelu(u0)
        v = jax.nn.gelu(v0)
        mu = jnp.mean(v, axis=-1, keepdims=True)
        var = jnp.mean(jnp.square(v - mu), axis=-1, keepdims=True)
        vn = (v - mu) * lax.rsqrt(var + EPS) * lg + lb
        w = jnp.where(row >= col, ws, 0.0)
        sg = jnp.dot(_mx(w), _mx(vn), preferred_element_type=F32) + bs
        outs.append(u * sg)
    return _rms(jnp.concatenate(outs, axis=1), ng)


def _gmlp_load(proj_ref, lng_ref, lnb_ref, ws_ref, bst_ref, d_a, ng):
    sl = lambda g: slice(CHUNK * g, CHUNK * (g + 1))
    us = [proj_ref[:, sl(g)].astype(F32) for g in range(ng)]
    vs = [proj_ref[:, d_a + CHUNK * g: d_a + CHUNK * (g + 1)].astype(F32) for g in range(ng)]
    lng = [lng_ref[:, sl(g)] for g in range(ng)]
    lnb = [lnb_ref[:, sl(g)] for g in range(ng)]
    wss = [ws_ref[g] for g in range(ng)]
    bss = [bst_ref[:, g:g + 1] for g in range(ng)]
    return us, vs, lng, lnb, wss, bss


def _gmlp_fwd(proj, ln_g, ln_b, w_s, bst, norm_g, d_mix, name):
    t = proj.shape[0]
    ng = w_s.shape[0]
    d_a = ng * CHUNK

    def body(proj_ref, lng_ref, lnb_ref, ws_ref, bst_ref, ng_ref, o_ref):
        args = _gmlp_load(proj_ref, lng_ref, lnb_ref, ws_ref, bst_ref, d_a, ng)
        o_ref[...] = _gmlp_block(*args, ng_ref[...]).astype(o_ref.dtype)

    vec = pl.BlockSpec((1, d_a), lambda c: (0, 0))
    return pl.pallas_call(
        body, grid=(t // CHUNK,),
        in_specs=[pl.BlockSpec((CHUNK, 2 * d_a), lambda c: (c, 0)), vec, vec,
                  pl.BlockSpec((ng, CHUNK, CHUNK), lambda c: (0, 0, 0)), pl.BlockSpec((CHUNK, 128), lambda c: (0, 0)), vec],
        out_specs=pl.BlockSpec((CHUNK, d_a), lambda c: (c, 0)), out_shape=S((t, d_mix), ACT_DTYPE),
        name=name, compiler_params=_cp(("parallel",)))(proj, ln_g, ln_b, w_s, bst, norm_g)


def _gmlp_bwd(proj, ln_g, ln_b, w_s, bst, norm_g, dcat, d_proj, name):
    t = proj.shape[0]
    ng = w_s.shape[0]
    d_a = ng * CHUNK

    def body(proj_ref, lng_ref, lnb_ref, ws_ref, bst_ref, ng_ref, dy_ref,
             dp_ref, dlng_ref, dlnb_ref, dws_ref, dbst_ref, dng_ref):
        @pl.when(pl.program_id(0) == 0)
        def _():
            for r in (dlng_ref, dlnb_ref, dws_ref, dbst_ref, dng_ref):
                r[...] = jnp.zeros_like(r)

        args = _gmlp_load(proj_ref, lng_ref, lnb_ref, ws_ref, bst_ref, d_a, ng)
        _, vjp = jax.vjp(_gmlp_block, *args, ng_ref[...])
        dus, dvs, dlng, dlnb, dwss, dbss, dng = vjp(dy_ref[...].astype(F32))
        lane = lax.broadcasted_iota(jnp.int32, (1, 128), 1)
        dbst = jnp.zeros((CHUNK, 128), F32)
        for g in range(ng):
            dp_ref[:, CHUNK * g:CHUNK * (g + 1)] = dus[g].astype(dp_ref.dtype)
            dp_ref[:, d_a + CHUNK * g:d_a + CHUNK * (g + 1)] = dvs[g].astype(dp_ref.dtype)
            dlng_ref[0:1, CHUNK * g:CHUNK * (g + 1)] += dlng[g]
            dlnb_ref[0:1, CHUNK * g:CHUNK * (g + 1)] += dlnb[g]
            dws_ref[g] += dwss[g]
            dbst = dbst + dbss[g] * (lane == g).astype(F32)
        dbst_ref[...] += dbst
        dng_ref[0:1, :] += dng

    vec = pl.BlockSpec((1, d_a), lambda c: (0, 0))
    acc = pl.BlockSpec((8, d_a), lambda c: (0, 0))
    wspec = pl.BlockSpec((ng, CHUNK, CHUNK), lambda c: (0, 0, 0))
    bspec = pl.BlockSpec((CHUNK, 128), lambda c: (0, 0))
    return pl.pallas_call(
        body, grid=(t // CHUNK,),
        in_specs=[pl.BlockSpec((CHUNK, 2 * d_a), lambda c: (c, 0)), vec, vec, wspec, bspec, vec,
                  pl.BlockSpec((CHUNK, d_a), lambda c: (c, 0))],
        out_specs=[pl.BlockSpec((CHUNK, 2 * d_a), lambda c: (c, 0)), acc, acc, wspec, bspec, acc],
        out_shape=[S((t, d_proj), ACT_DTYPE), S((8, d_a), F32), S((8, d_a), F32), S((ng, CHUNK, CHUNK), F32),
                   S((CHUNK, 128), F32), S((8, d_a), F32)],
        name=name, compiler_params=_cp(("arbitrary",)))(proj, ln_g, ln_b, w_s, bst, norm_g, dcat)


def _silu_grad(c):
    s = jax.nn.sigmoid(c)
    return s * (1.0 + c * (1.0 - s))


def _fill_prev_main(s_ref, prev_ref, main_ref, i, tt):
    s_ref[pl.ds(0, HALO), :] = jnp.where(i > 0, prev_ref[...].astype(F32), 0.0)
    s_ref[pl.ds(HALO, tt), :] = main_ref[...].astype(F32)


def _prev_spec(tt, tc, joff):
    return pl.BlockSpec((HALO, tc), lambda j, i: (jnp.maximum(i * (tt // HALO) - 1, 0), j + joff))


def _next_spec(tt, tc, joff, t):
    return pl.BlockSpec((HALO, tc), lambda j, i: (jnp.minimum((i + 1) * (tt // HALO), t // HALO - 1), j + joff))


CONV_RC = 32


def _conv_tiles(t, c):
    return _pick(t, (1024, 512, 256, 128)), _pick(c, (256, 128))


def _row_chunks(tt, fn, init=0):
    rc = min(CONV_RC, tt)
    return lax.fori_loop(0, tt // rc, lambda q, c: fn(pl.multiple_of(q * rc, rc), rc, c), init)


def _fold8(p):
    acc = p[0:8]
    for r in range(8, p.shape[0], 8):
        acc = acc + p[r:r + 8]
    return acc


def _taps_chunk(s_ref, w_ref, kw, r0, rc):
    xe = s_ref[pl.ds(HALO - 8 + r0, rc + 8), :]
    acc = w_ref[0:1, :] * xe[8 - (kw - 1):8 - (kw - 1) + rc]
    for k in range(1, kw):
        acc = acc + w_ref[k:k + 1, :] * xe[8 - (kw - 1) + k:8 - (kw - 1) + k + rc]
    return acc


def _conv_bwd_chunk(sd_ref, x, w_ref, kw, r0, rc, dws):
    de = sd_ref[pl.ds(r0, rc + 8), :]
    dx, out = None, list(dws)
    for j in range(kw):
        d = de[j:j + rc]
        k = kw - 1 - j
        term = w_ref[k:k + 1, :] * d
        dx = term if dx is None else dx + term
        out[k] = out[k] + _fold8(x * d)
    return dx, out


def _conv_ssm_fwd(proj, col0, w, b, name):
    t = proj.shape[0]
    kw, c = w.shape
    tt, tc = _conv_tiles(t, c)
    joff = col0 // tc
    assert col0 % tc == 0

    def body(x_ref, xp_ref, w_ref, b_ref, o_ref, c_ref, s_ref):
        _fill_prev_main(s_ref, xp_ref, x_ref, pl.program_id(1), tt)

        def chunk(r0, rc, carry):
            cpre = _taps_chunk(s_ref, w_ref, kw, r0, rc) + b_ref[...]
            o_ref[pl.ds(r0, rc), :] = jax.nn.silu(cpre).astype(o_ref.dtype)
            c_ref[pl.ds(r0, rc), :] = cpre.astype(c_ref.dtype)
            return carry

        _row_chunks(tt, chunk)

    out = pl.BlockSpec((tt, tc), lambda j, i: (i, j))
    return pl.pallas_call(
        body, grid=(c // tc, t // tt),
        in_specs=[pl.BlockSpec((tt, tc), lambda j, i: (i, j + joff)), _prev_spec(tt, tc, joff),
                  pl.BlockSpec((kw, tc), lambda j, i: (0, j)), pl.BlockSpec((1, tc), lambda j, i: (0, j))],
        out_specs=[out, out], out_shape=[S((t, c), XBC_DTYPE), S((t, c), ACT_DTYPE)],
        scratch_shapes=[pltpu.VMEM((HALO + tt, tc), F32)], name=name,
        compiler_params=_cp(("parallel", "arbitrary")))(proj, proj, w, b)


def _conv_ssm_bwd(proj, col0, w, cpre, wcol0, dact, dproj, out_col0, name):
    t = proj.shape[0]
    kw = w.shape[0]
    c = dact.shape[1]
    tt, tc = _conv_tiles(t, c)
    assert col0 % tc == 0 and wcol0 % tc == 0 and out_col0 % tc == 0
    joff, wj, oj = col0 // tc, wcol0 // tc, out_col0 // tc
    nt = t // tt

    def body(x_ref, w_ref, c_ref, cn_ref, d_ref, dn_ref, dp_in, dx_ref, dw_ref, db_ref, sd_ref):
        del dp_in
        i = pl.program_id(1)

        @pl.when(i == 0)
        def _():
            dw_ref[...] = jnp.zeros_like(dw_ref)
            db_ref[...] = jnp.zeros_like(db_ref)

        def stage(r0, rc, db):
            rows = pl.ds(r0, rc)
            d = d_ref[rows, :].astype(F32) * _silu_grad(c_ref[rows, :].astype(F32))
            sd_ref[rows, :] = d
            return db + _fold8(d)

        zero8 = jnp.zeros((8, tc), F32)
        db = _row_chunks(tt, stage, zero8)
        sd_ref[pl.ds(tt, HALO), :] = jnp.where(
            i < nt - 1, dn_ref[...].astype(F32) * _silu_grad(cn_ref[...].astype(F32)), 0.0)

        def chunk(r0, rc, dws):
            dx, dws = _conv_bwd_chunk(sd_ref, x_ref[pl.ds(r0, rc), :].astype(F32), w_ref, kw, r0, rc, dws)
            dx_ref[pl.ds(r0, rc), :] = dx.astype(dx_ref.dtype)
            return dws

        dws = _row_chunks(tt, chunk, [zero8] * kw)
        for k in range(kw):
            dw_ref[k:k + 1, :] += jnp.sum(dws[k], axis=0, keepdims=True)
        db_ref[0:1, :] += jnp.sum(db, axis=0, keepdims=True)

    acc = pl.BlockSpec((8, tc), lambda j, i: (0, j))
    return pl.pallas_call(
        body, grid=(c // tc, nt),
        in_specs=[pl.BlockSpec((tt, tc), lambda j, i: (i, j + joff)), pl.BlockSpec((kw, tc), lambda j, i: (0, j + wj)),
                  pl.BlockSpec((tt, tc), lambda j, i: (i, j + wj)), _next_spec(tt, tc, wj, t),
                  pl.BlockSpec((tt, tc), lambda j, i: (i, j)), _next_spec(tt, tc, 0, t),
                  pl.BlockSpec(memory_space=pl.ANY)],
        out_specs=[pl.BlockSpec((tt, tc), lambda j, i: (i, j + oj)), acc, acc],
        out_shape=[S(dproj.shape, dproj.dtype), S((8, c), F32), S((8, c), F32)],
        scratch_shapes=[pltpu.VMEM((tt + HALO, tc), F32)],
        input_output_aliases={6: 0}, name=name,
        compiler_params=_cp(("parallel", "arbitrary")))(proj, w, cpre, cpre, dact, dact, dproj)


def _conv_ffn_fwd(hid, w, b, name):
    t, f2 = hid.shape
    f = f2 // 2
    kw = w.shape[0]
    tt, tc = _conv_tiles(t, f)
    nj = f // tc

    def body(g_ref, gp_ref, u_ref, up_ref, wg_ref, wu_ref, bg_ref, bu_ref, o_ref, cv_ref, sg_ref, su_ref):
        i = pl.program_id(1)
        _fill_prev_main(sg_ref, gp_ref, g_ref, i, tt)
        _fill_prev_main(su_ref, up_ref, u_ref, i, tt)

        def chunk(r0, rc, carry):
            rows = pl.ds(r0, rc)
            gate = _taps_chunk(sg_ref, wg_ref, kw, r0, rc) + bg_ref[...]
            up = _taps_chunk(su_ref, wu_ref, kw, r0, rc) + bu_ref[...]
            o_ref[rows, :] = (jax.nn.silu(gate) * up).astype(o_ref.dtype)
            cv_ref[0, rows, :] = gate.astype(cv_ref.dtype)
            cv_ref[1, rows, :] = up.astype(cv_ref.dtype)
            return carry

        _row_chunks(tt, chunk)

    return pl.pallas_call(
        body, grid=(nj, t // tt),
        in_specs=[pl.BlockSpec((tt, tc), lambda j, i: (i, j)), _prev_spec(tt, tc, 0),
                  pl.BlockSpec((tt, tc), lambda j, i: (i, j + nj)), _prev_spec(tt, tc, nj),
                  pl.BlockSpec((kw, tc), lambda j, i: (0, j)), pl.BlockSpec((kw, tc), lambda j, i: (0, j + nj)),
                  pl.BlockSpec((1, tc), lambda j, i: (0, j)), pl.BlockSpec((1, tc), lambda j, i: (0, j + nj))],
        out_specs=[pl.BlockSpec((tt, tc), lambda j, i: (i, j)), pl.BlockSpec((2, tt, tc), lambda j, i: (0, i, j))],
        out_shape=[S((t, f), ACT_DTYPE), S((2, t, f), ACT_DTYPE)],
        scratch_shapes=[pltpu.VMEM((HALO + tt, tc), F32), pltpu.VMEM((HALO + tt, tc), F32)], name=name,
        compiler_params=_cp(("parallel", "arbitrary")))(hid, hid, hid, hid, w, w, b, b)


def _conv_ffn_bwd(hid, w, cv, dact, name):
    t, f2 = hid.shape
    f = f2 // 2
    kw = w.shape[0]
    tt, tc = _conv_tiles(t, f)
    nj = f // tc
    nt = t // tt

    def body(g_ref, u_ref, wg_ref, wu_ref, cv_ref, cvn_ref, d_ref, dn_ref, dh_ref, dw_ref, db_ref, dg_ref, du_ref):
        i = pl.program_id(1)

        @pl.when(i == 0)
        def _():
            dw_ref[...] = jnp.zeros_like(dw_ref)
            db_ref[...] = jnp.zeros_like(db_ref)

        def cotangents(gate, up, dact_v):
            sg = jax.nn.sigmoid(gate)
            return dact_v * up * (sg * (1.0 + gate * (1.0 - sg))), dact_v * (gate * sg)

        def stage(r0, rc, dbs):
            rows = pl.ds(r0, rc)
            dg, du = cotangents(cv_ref[0, rows, :].astype(F32), cv_ref[1, rows, :].astype(F32), d_ref[rows, :].astype(F32))
            dg_ref[rows, :] = dg
            du_ref[rows, :] = du
            return [dbs[0] + _fold8(dg), dbs[1] + _fold8(du)]

        zero8 = jnp.zeros((8, tc), F32)
        dbs = _row_chunks(tt, stage, [zero8, zero8])
        dgn, dun = cotangents(cvn_ref[0].astype(F32), cvn_ref[1].astype(F32), dn_ref[...].astype(F32))
        dg_ref[pl.ds(tt, HALO), :] = jnp.where(i < nt - 1, dgn, 0.0)
        du_ref[pl.ds(tt, HALO), :] = jnp.where(i < nt - 1, dun, 0.0)
        for s, (sd_ref, x_ref, w_ref) in enumerate(((dg_ref, g_ref, wg_ref), (du_ref, u_ref, wu_ref))):
            def chunk(r0, rc, dws, s=s, sd_ref=sd_ref, x_ref=x_ref, w_ref=w_ref):
                dx, dws = _conv_bwd_chunk(sd_ref, x_ref[pl.ds(r0, rc), :].astype(F32), w_ref, kw, r0, rc, dws)
                dh_ref[s, pl.ds(r0, rc), :] = dx.astype(dh_ref.dtype)
                return dws

            dws = _row_chunks(tt, chunk, [zero8] * kw)
            for k in range(kw):
                dw_ref[s, k:k + 1, :] += jnp.sum(dws[k], axis=0, keepdims=True)
            db_ref[s, 0:1, :] += jnp.sum(dbs[s], axis=0, keepdims=True)

    acc = pl.BlockSpec((2, 8, tc), lambda j, i: (0, 0, j))
    dsc = pltpu.VMEM((tt + HALO, tc), F32)
    nxt = lambda j, i: (0, jnp.minimum((i + 1) * (tt // HALO), t // HALO - 1), j)
    return pl.pallas_call(
        body, grid=(nj, nt),
        in_specs=[pl.BlockSpec((tt, tc), lambda j, i: (i, j)), pl.BlockSpec((tt, tc), lambda j, i: (i, j + nj)),
                  pl.BlockSpec((kw, tc), lambda j, i: (0, j)), pl.BlockSpec((kw, tc), lambda j, i: (0, j + nj)),
                  pl.BlockSpec((2, tt, tc), lambda j, i: (0, i, j)), pl.BlockSpec((2, HALO, tc), nxt),
                  pl.BlockSpec((tt, tc), lambda j, i: (i, j)), _next_spec(tt, tc, 0, t)],
        out_specs=[pl.BlockSpec((2, tt, tc), lambda j, i: (0, i, j)), acc, acc],
        out_shape=[S((2, t, f), MXU_DTYPE), S((2, 8, f), F32), S((2, 8, f), F32)],
        scratch_shapes=[dsc, dsc], name=name,
        compiler_params=_cp(("parallel", "arbitrary")))(hid, hid, w, w, cv, cv, dact, dact)


SSD_SUB = 2


def _ssd_chunk(xs, bm, cm, dtraw, z, hin, bias, alog, dskip, normg):
    n = bm.shape[0]
    row = lax.broadcasted_iota(jnp.int32, (n, n), 0)
    col = lax.broadcasted_iota(jnp.int32, (n, n), 1)
    causal = row >= col
    tri = causal.astype(F32)
    lane = lax.broadcasted_iota(jnp.int32, (1, 128), 1)
    sub = lax.broadcasted_iota(jnp.int32, (128, 1), 0)
    last = (lax.broadcasted_iota(jnp.int32, (n, 1), 0) == n - 1).astype(F32)
    dt = jax.nn.softplus(dtraw + bias)
    adt = dt * (-jnp.exp(alog))
    acs = jnp.dot(tri, adt, preferred_element_type=F32, precision=lax.Precision.HIGHEST)
    ch = lax.broadcasted_iota(jnp.int32, (128, GROUP_CH), 1)
    hd = lax.broadcasted_iota(jnp.int32, (128, GROUP_CH), 0) * HEAD_DIM
    expand = ((ch >= hd) & (ch < hd + HEAD_DIM)).astype(F32)
    acs_x = jnp.dot(acs, expand, preferred_element_type=F32, precision=lax.Precision.HIGHEST)
    alast_x = jnp.sum(acs_x * last, axis=0, keepdims=True)
    acs_t = acs.T
    scores = lax.dot_general(_mx(cm), _mx(bm), (((1,), (1,)), ((), ())), preferred_element_type=F32)
    yds, xts = [], []
    for r in range(HEADS_PER_GROUP):
        pick = (lane == r).astype(F32)
        acol = jnp.sum(acs * pick, axis=1, keepdims=True)
        arow = jnp.sum(acs_t * (sub == r).astype(F32), axis=0, keepdims=True)
        dtc = jnp.sum(dt * pick, axis=1, keepdims=True)
        lm = jnp.exp(jnp.where(causal, acol - arow, -1e30))
        xts.append(xs[r] * dtc)
        yds.append(jnp.dot(_mx(scores * lm), _mx(xts[r]), preferred_element_type=F32))
    xt = jnp.concatenate(xts, axis=1)
    yo = jnp.exp(acs_x) * jnp.dot(_mx(cm), _mx(hin), preferred_element_type=F32)
    st = lax.dot_general(_mx(bm), _mx(xt * jnp.exp(alast_x - acs_x)), (((0,), (0,)), ((), ())), preferred_element_type=F32)
    hout = jnp.exp(alast_x) * hin + st
    y = (jnp.concatenate(yds, axis=1) + yo + dskip * jnp.concatenate(xs, axis=1)) * jax.nn.silu(z)
    return _rms(y, normg), hout


def _ssd_block(datas, hin, consts):
    ys = []
    for data in datas:
        y, hin = _ssd_chunk(*data, hin, *consts)
        ys.append(y)
    return ys, hin


def _ssd_specs(d_ssm, ngrp, zcol0, rev, nb):
    cc = (lambda c: nb - 1 - c) if rev else (lambda c: c)
    rows = SSD_SUB * CHUNK
    xj, bj, cj, zj = 0, d_ssm // 128, d_ssm // 128 + ngrp, zcol0 // GROUP_CH
    const = lambda w: pl.BlockSpec((None, 8, w), lambda g, c: (g, 0, 0))
    return cc, [
        pl.BlockSpec((rows, GROUP_CH), lambda g, c: (cc(c), xj + g)),
        pl.BlockSpec((rows, 128), lambda g, c: (cc(c), bj + g)),
        pl.BlockSpec((rows, 128), lambda g, c: (cc(c), cj + g)),
        pl.BlockSpec((rows, 128), lambda g, c: (cc(c), g)),
        pl.BlockSpec((rows, GROUP_CH), lambda g, c: (cc(c), zj + g)),
        const(128), const(128), const(GROUP_CH), const(GROUP_CH)]


def _sub_rows(s):
    return slice(CHUNK * s, CHUNK * (s + 1))


def _ssd_load(x_ref, b_ref, c_ref, dt_ref, z_ref, bias_ref, alog_ref, dsk_ref, ng_ref):
    datas = []
    for s in range(SSD_SUB):
        rows = _sub_rows(s)
        xs = [x_ref[rows, HEAD_DIM * r:HEAD_DIM * (r + 1)].astype(F32) for r in range(HEADS_PER_GROUP)]
        datas.append((xs, b_ref[rows, :].astype(F32), c_ref[rows, :].astype(F32), dt_ref[rows, :],
                      z_ref[rows, :].astype(F32)))
    return datas, (bias_ref[0:1, :], alog_ref[0:1, :], dsk_ref[0:1, :], ng_ref[0:1, :])


def _ssd_fwd(xbc, dtp, proj, zcol0, bias_p, alog_p, dskip_x, normg_x, cat, name):
    t = xbc.shape[0]
    ngrp = bias_p.shape[0]
    d_ssm = ngrp * GROUP_CH
    rows = SSD_SUB * CHUNK
    nb = t // rows
    d_a = cat.shape[1] - d_ssm
    assert d_a % GROUP_CH == 0 and zcol0 % GROUP_CH == 0 and t % rows == 0
    _, specs = _ssd_specs(d_ssm, ngrp, zcol0, False, nb)

    def body(x_ref, b_ref, c_ref, dt_ref, z_ref, bias_ref, alog_ref, dsk_ref, ng_ref, cat_in, y_ref, hs_ref, h_ref):
        del cat_in

        @pl.when(pl.program_id(1) == 0)
        def _():
            h_ref[...] = jnp.zeros_like(h_ref)

        datas, consts = _ssd_load(x_ref, b_ref, c_ref, dt_ref, z_ref, bias_ref, alog_ref, dsk_ref, ng_ref)
        hs_ref[...] = h_ref[...]
        ys, hout = _ssd_block(datas, h_ref[...], consts)
        for s in range(SSD_SUB):
            y_ref[_sub_rows(s), :] = ys[s].astype(y_ref.dtype)
        h_ref[...] = hout

    return pl.pallas_call(
        body, grid=(ngrp, nb), in_specs=specs + [pl.BlockSpec(memory_space=pl.ANY)],
        out_specs=[pl.BlockSpec((rows, GROUP_CH), lambda g, c: (c, d_a // GROUP_CH + g)),
                   pl.BlockSpec((None, None, D_STATE, GROUP_CH), lambda g, c: (c, g, 0, 0))],
        out_shape=[S(cat.shape, cat.dtype), S((nb, ngrp, D_STATE, GROUP_CH), F32)],
        scratch_shapes=[pltpu.VMEM((D_STATE, GROUP_CH), F32)],
        input_output_aliases={9: 0}, name=name,
        compiler_params=_cp(("parallel", "arbitrary")))(xbc, xbc, xbc, dtp, proj, bias_p, alog_p, dskip_x, normg_x, cat)


def _ssd_bwd(xbc, dtp, proj, zcol0, bias_p, alog_p, dskip_x, normg_x, hs, dcat, dproj, name):
    t = xbc.shape[0]
    ngrp = bias_p.shape[0]
    d_ssm = ngrp * GROUP_CH
    rows = SSD_SUB * CHUNK
    nb = t // rows
    d_a = dcat.shape[1] - d_ssm
    cc, specs = _ssd_specs(d_ssm, ngrp, zcol0, True, nb)

    def body(x_ref, b_ref, c_ref, dt_ref, z_ref, bias_ref, alog_ref, dsk_ref, ng_ref, hs_ref, dy_ref, dp_in,
             dz_ref, dx_ref, db_ref, dc_ref, ddt_ref, dbias_ref, dalog_ref, ddsk_ref, dng_ref, dh_ref):
        del dp_in

        @pl.when(pl.program_id(1) == 0)
        def _():
            dh_ref[...] = jnp.zeros_like(dh_ref)
            for r in (dbias_ref, dalog_ref, ddsk_ref, dng_ref):
                r[...] = jnp.zeros_like(r)

        datas, consts = _ssd_load(x_ref, b_ref, c_ref, dt_ref, z_ref, bias_ref, alog_ref, dsk_ref, ng_ref)
        _, vjp = jax.vjp(_ssd_block, datas, hs_ref[...], consts)
        dys = [dy_ref[_sub_rows(s), :].astype(F32) for s in range(SSD_SUB)]
        ddatas, dhin, (dbias, dalog, ddsk, dng) = vjp((dys, dh_ref[...]))
        for s, (dxs, dbm, dcm, ddt, dz) in enumerate(ddatas):
            rws = _sub_rows(s)
            for r in range(HEADS_PER_GROUP):
                dx_ref[rws, HEAD_DIM * r:HEAD_DIM * (r + 1)] = dxs[r].astype(dx_ref.dtype)
            db_ref[rws, :] = dbm.astype(db_ref.dtype)
            dc_ref[rws, :] = dcm.astype(dc_ref.dtype)
            ddt_ref[rws, :] = ddt
            dz_ref[rws, :] = dz.astype(dz_ref.dtype)
        dh_ref[...] = dhin
        dbias_ref[0:1, :] += dbias
        dalog_ref[0:1, :] += dalog
        ddsk_ref[0:1, :] += ddsk
        dng_ref[0:1, :] += dng

    acc = lambda w: pl.BlockSpec((None, 8, w), lambda g, c: (g, 0, 0))
    blk = lambda w: pl.BlockSpec((rows, w), lambda g, c: (cc(c), g))
    return pl.pallas_call(
        body, grid=(ngrp, nb),
        in_specs=specs + [pl.BlockSpec((None, None, D_STATE, GROUP_CH), lambda g, c: (cc(c), g, 0, 0)),
                          pl.BlockSpec((rows, GROUP_CH), lambda g, c: (cc(c), d_a // GROUP_CH + g)),
                          pl.BlockSpec(memory_space=pl.ANY)],
        out_specs=[pl.BlockSpec((rows, GROUP_CH), lambda g, c: (cc(c), zcol0 // GROUP_CH + g)),
                   blk(GROUP_CH), blk(128), blk(128), blk(128), acc(128), acc(128), acc(GROUP_CH), acc(GROUP_CH)],
        out_shape=[S(dproj.shape, dproj.dtype), S((t, d_ssm), XBC_DTYPE), S((t, ngrp * 128), XBC_DTYPE),
                   S((t, ngrp * 128), XBC_DTYPE), S((t, ngrp * 128), F32), S((ngrp, 8, 128), F32),
                   S((ngrp, 8, 128), F32), S((ngrp, 8, GROUP_CH), F32), S((ngrp, 8, GROUP_CH), F32)],
        scratch_shapes=[pltpu.VMEM((D_STATE, GROUP_CH), F32)],
        input_output_aliases={11: 0}, name=name,
        compiler_params=_cp(("parallel", "arbitrary")))(xbc, xbc, xbc, dtp, proj, bias_p, alog_p, dskip_x, normg_x, hs, dcat, dproj)


def _adamw(parts, w, m, v, name):
    r, c = w.shape
    tr = _pick(r, (256, 128, 64, 32, 16, 8)) if c * 4 * 256 <= 4 * 1024 * 1024 else _pick(r, (64, 32, 16, 8))

    def body(p_ref, w_ref, m_ref, v_ref, g_ref, d_ref, nm_ref, nv_ref):
        g = p_ref[0].astype(F32)
        for k in range(1, N_DEV):
            g = g + p_ref[k].astype(F32)
        mm = ADAM_B1 * m_ref[...] + (1.0 - ADAM_B1) * g
        vv = ADAM_B2 * v_ref[...] + (1.0 - ADAM_B2) * jnp.square(g)
        m_hat = mm / (1.0 - ADAM_B1 ** ADAM_STEP)
        v_hat = vv / (1.0 - ADAM_B2 ** ADAM_STEP)
        g_ref[...] = g
        d_ref[...] = -ADAM_LR * (m_hat / (jnp.sqrt(v_hat) + ADAM_EPS) + ADAM_WD * w_ref[...])
        nm_ref[...] = mm
        nv_ref[...] = vv

    blk = pl.BlockSpec((tr, c), lambda i: (i, 0))
    return pl.pallas_call(
        body, grid=(r // tr,),
        in_specs=[pl.BlockSpec((N_DEV, tr, c), lambda i: (0, i, 0)), blk, blk, blk],
        out_specs=[blk, blk, blk, blk], out_shape=[S((r, c), F32)] * 4, name=name,
        compiler_params=_cp(("parallel",)))(parts, w, m, v)


def _mesh_pos():
    return lax.axis_index("x"), lax.axis_index("y"), lax.axis_index("c")


def _peer(d, x, y, c):
    return (1 - x if (d >> 2) & 1 else x, 1 - y if (d >> 1) & 1 else y, 1 - c if d & 1 else c)


def _xfer_start(items, name, after=None):
    n = len(items)
    kinds = [k for k, _ in items]
    srcs = [pltpu.with_memory_space_constraint(a, pltpu.HBM) for _, a in items]
    land_shapes = [((N_DEV,) + a.shape if k == 'gather' else a.shape, a.dtype) for k, a in items]
    lands = [pltpu.with_memory_space_constraint(lax.empty(s, dt), pltpu.HBM) for s, dt in land_shapes]
    extra = [] if after is None else [after]

    def body(*refs):
        src_refs, land_refs = refs[:n], refs[n:2 * n]
        outs = refs[2 * n + len(extra):]
        sems = outs[:2 * n]
        token = outs[4 * n]
        x, y, c = _mesh_pos()
        me = 4 * x + 2 * y + c
        for a in range(n):
            for d in range(1, N_DEV):
                px, py, pc = _peer(d, x, y, c)
                src = src_refs[a] if kinds[a] == 'gather' else src_refs[a].at[4 * px + 2 * py + pc]
                pltpu.make_async_remote_copy(
                    src_ref=src, dst_ref=land_refs[a].at[me], send_sem=sems[2 * a].at[d - 1],
                    recv_sem=sems[2 * a + 1].at[d - 1], device_id=(px, py, pc),
                    device_id_type=pl.DeviceIdType.MESH).start()
        token[...] = jnp.zeros_like(token)

    hbm = pl.BlockSpec(memory_space=pltpu.HBM)
    sem = pl.BlockSpec(memory_space=pltpu.SEMAPHORE)
    out_shape = ([pltpu.SemaphoreType.DMA((N_DEV - 1,))] * (2 * n)
                 + [pltpu.HBM(a.shape, a.dtype) for a in srcs] + [pltpu.HBM(s, dt) for s, dt in land_shapes]
                 + [S((8, 128), F32)])
    res = pl.pallas_call(
        body, name=name, out_shape=out_shape,
        in_specs=[hbm] * (2 * n) + [pl.BlockSpec(memory_space=pl.ANY)] * len(extra),
        out_specs=[sem] * (2 * n) + [hbm] * (2 * n) + [pl.BlockSpec(memory_space=pltpu.VMEM)],
        input_output_aliases={**{a: 2 * n + a for a in range(n)}, **{n + a: 3 * n + a for a in range(n)}},
        compiler_params=pltpu.CompilerParams(has_side_effects=pltpu.SideEffectType.DATAFLOW_SIDE_EFFECTING),
    )(*srcs, *lands, *extra)
    return (kinds, res[:2 * n], res[2 * n:3 * n], res[3 * n:4 * n]), res[4 * n]


def _xfer_wait(handle, after, name):
    kinds, sems, src_thru, land_thru = handle
    n = len(kinds)

    def body(*refs):
        land_refs = refs[n:2 * n]
        sem_refs = refs[2 * n:4 * n]
        x, y, c = _mesh_pos()
        me = 4 * x + 2 * y + c
        for a in range(n):
            for d in range(1, N_DEV):
                slab = land_refs[a].at[me]
                cp = pltpu.make_async_remote_copy(
                    src_ref=slab, dst_ref=slab, send_sem=sem_refs[2 * a].at[d - 1], recv_sem=sem_refs[2 * a + 1].at[d - 1],
                    device_id=_peer(d, x, y, c), device_id_type=pl.DeviceIdType.MESH)
                cp.wait_send()
                cp.wait_recv()

    hbm = pl.BlockSpec(memory_space=pltpu.HBM)
    sem = pl.BlockSpec(memory_space=pltpu.SEMAPHORE)
    res = pl.pallas_call(
        body, name=name,
        out_shape=[pltpu.HBM(a.shape, a.dtype) for a in src_thru] + [pltpu.HBM(a.shape, a.dtype) for a in land_thru],
        in_specs=[hbm] * (2 * n) + [sem] * (2 * n) + [pl.BlockSpec(memory_space=pl.ANY)],
        out_specs=[hbm] * (2 * n), input_output_aliases={a: a for a in range(2 * n)},
        compiler_params=pltpu.CompilerParams(has_side_effects=pltpu.SideEffectType.DATAFLOW_SIDE_EFFECTING),
    )(*src_thru, *land_thru, *sems, after)
    x, y, c = _mesh_pos()
    me = 4 * x + 2 * y + c
    out = []
    for a in range(n):
        src = res[a]
        own = src[None] if kinds[a] == 'gather' else lax.dynamic_index_in_dim(src, me, 0, keepdims=True)
        out.append(lax.dynamic_update_index_in_dim(res[n + a], own, me, 0))
    return out


def _stack_to_full(kind, st):
    if kind == 'row':
        return st.reshape(st.shape[0] * st.shape[1], st.shape[2])
    return jnp.transpose(st, (1, 0, 2)).reshape(st.shape[1], st.shape[0] * st.shape[2])


def _full_to_stack(kind, full):
    r, c = full.shape
    if kind == 'row':
        return full.reshape(N_DEV, r // N_DEV, c)
    return jnp.transpose(full.reshape(r, N_DEV, c // N_DEV), (1, 0, 2))


def _pack_small(named):
    layout = [(a.shape, a.size, -(-a.size // 1024) * 8) for a in named]
    total = sum(nr for _, _, nr in layout) * 128
    packed, off = None, 0
    for a, (_, n, nr) in zip(named, layout):
        part = jnp.pad(a.reshape(-1).astype(F32), (off, total - off - n))
        packed = part if packed is None else packed + part
        off += nr * 128
    return packed.reshape(total // 128, 128), layout


def _unpack_small(packed, layout):
    out, r0 = [], 0
    for shape, n, nr in layout:
        out.append(packed[r0:r0 + nr].reshape(-1)[:n].reshape(shape))
        r0 += nr
    return out


def _row0(acc):
    return acc[0]


def _local_step(x, p, tgt, sm, comm):
    t, d = x.shape
    h_n = sm['dt_bias'].shape[-1]
    ngrp = h_n // HEADS_PER_GROUP
    d_ssm = h_n * HEAD_DIM
    d_a = sm['ln_a_g'].shape[-1]
    d_mix = d_a + d_ssm
    d_xbc = sm['conv_ssm_b'].shape[-1]
    d_main = 2 * d_a + d_ssm + d_xbc
    assert d_xbc == d_ssm + 2 * ngrp * D_STATE and h_n <= 128
    zcol0, xcol0 = 2 * d_a, 2 * d_a + d_ssm
    vec = lambda v: v.reshape(1, -1)

    bst = jnp.pad(sm['b_s'].T, ((0, 0), (0, 128 - sm['b_s'].shape[0])))
    grp = lambda v, w: jnp.broadcast_to(jnp.pad(v.reshape(ngrp, 1, -1), ((0, 0), (0, 0), (0, w - v.size // ngrp))), (ngrp, 8, w))
    bias_p, alog_p = grp(sm['dt_bias'], 128), grp(sm['a_log'], 128)
    dskip_x = grp(jnp.repeat(sm['d_skip'], HEAD_DIM), GROUP_CH)
    normg_x = grp(sm['ssm_norm_g'], GROUP_CH)
    pad_dt = lambda v: jnp.pad(v[:, :h_n].reshape(t, ngrp, HEADS_PER_GROUP),
                               ((0, 0), (0, 0), (0, 128 - HEADS_PER_GROUP))).reshape(t, ngrp * 128)

    g_mix = vec(sm['norm_mix_g']) + comm.tok0
    a_n = _rms_fwd(x, g_mix, "rms_mix")
    wf = comm.weights('a', a_n)
    w_main = wf['w_in'][:, :d_main]
    w_dt = jnp.pad(wf['w_in'][:, d_main:], ((0, 0), (0, 128 - h_n)))
    proj = _mm_nn(a_n, w_main, out_dtype=ACT_DTYPE, name="mm_in")
    dtp = pad_dt(_mm_nn(a_n, w_dt, out_dtype=F32, name="mm_dt"))
    cat = _gmlp_fwd(proj, vec(sm['ln_a_g']), vec(sm['ln_a_b']), sm['w_s'], bst, vec(sm['norm_a_g']), d_mix, "gmlp_fwd")
    xbc, cpre = _conv_ssm_fwd(proj, xcol0, wf['conv_ssm_w'], vec(sm['conv_ssm_b']), "conv_ssm_fwd")
    cat, hs = _ssd_fwd(xbc, dtp, proj, zcol0, bias_p, alog_p, dskip_x, normg_x, cat, "ssd_fwd")
    wf.update(comm.weights('b', hs))
    h1 = _mm_nn(cat, wf['w_out'], out_dtype=F32, name="mm_out", res=x)
    f_n = _rms_fwd(h1, vec(sm['norm_ffn_g']), "rms_ffn")
    hid = _mm_nn(f_n, wf['w_up'], out_dtype=ACT_DTYPE, name="mm_up")
    act, cv = _conv_ffn_fwd(hid, wf['conv_ffn_w'], vec(sm['conv_ffn_b']), "conv_ffn_fwd")
    h2 = _mm_nn(act, wf['w_down'], out_dtype=F32, name="mm_down", res=h1)
    r_n = _rms_fwd(h2, vec(sm['norm_ple_g']), "rms_ple")
    q = _mm_nn(r_n, wf['w_ple_gate'], out_dtype=F32, name="mm_pg")
    p_m = p.astype(MXU_DTYPE)
    pe = _mm_nn(p_m, wf['w_ple'], out_dtype=F32, name="mm_ple")

    loss, dh3, dq, dpe, dgf = _head(h2, q, pe, tgt, vec(sm['norm_final_g']), "head")
    wgrad = lambda act_t, g, name, **kw: _mm_nn(act_t, g, out_dtype=WIRE_DTYPE, name=name, wide=True, **kw)
    gs = {}
    gs['norm_final_g'] = _row0(dgf)
    g_ple = wgrad(p_m.T, dpe, "wg_ple")
    g_pg = wgrad(r_n.T, dq, "wg_pg")
    dr = _mm_nt(dq, wf['w_ple_gate'], out_dtype=F32, name="dg_pg")
    dh2, dh2m, dg = _rms_bwd(h2, vec(sm['norm_ple_g']), dr, dh3, "rms_ple_bwd")
    gs['norm_ple_g'] = _row0(dg)
    g_down = wgrad(act.T, dh2m, "wg_down")
    tok = comm.send('1', {'w_ple': g_ple, 'w_ple_gate': g_pg, 'w_down': g_down})
    dact = _mm_nt(dh2m, wf['w_down'], out_dtype=ACT_DTYPE, name="dg_down")
    dhid, dcw, dcb = _conv_ffn_bwd(hid, wf['conv_ffn_w'] + tok, cv, dact, "conv_ffn_bwd")
    kf = wf['conv_ffn_w'].shape[0]
    g_cf = jnp.concatenate([dcw[0, :kf], dcw[1, :kf]], axis=1)
    gs['conv_ffn_b'] = jnp.concatenate([dcb[0, 0], dcb[1, 0]], axis=0)
    g_up = wgrad(f_n.T, dhid, "wg_up", b_split=2)
    df = _mm_nt(dhid, wf['w_up'], out_dtype=F32, name="dg_up", a_split=2)
    dh1, dh1m, dg = _rms_bwd(h1, vec(sm['norm_ffn_g']), df, dh2, "rms_ffn_bwd")
    gs['norm_ffn_g'] = _row0(dg)
    g_out = wgrad(cat.T, dh1m, "wg_out")
    tok = comm.send('2', {'conv_ffn_w': g_cf, 'w_up': g_up, 'w_out': g_out})
    dcat = _mm_nt(dh1m, wf['w_out'], out_dtype=ACT_DTYPE, name="dg_out")
    dproj, dlng, dlnb, dws, dbst, dng = _gmlp_bwd(proj, vec(sm['ln_a_g']) + tok, vec(sm['ln_a_b']), sm['w_s'], bst,
                                                  vec(sm['norm_a_g']), dcat, d_main, "gmlp_bwd")
    gs['ln_a_g'], gs['ln_a_b'], gs['w_s'], gs['norm_a_g'] = _row0(dlng), _row0(dlnb), dws, _row0(dng)
    gs['b_s'] = dbst[:, :sm['b_s'].shape[0]].T
    dproj, dxs, dbm, dcm, ddtp, dbias, dalog, ddsk, dsng = _ssd_bwd(
        xbc, dtp, proj, zcol0, bias_p, alog_p, dskip_x, normg_x, hs, dcat, dproj, "ssd_bwd")
    gs['dt_bias'] = dbias[:, 0, :HEADS_PER_GROUP].reshape(h_n)
    gs['a_log'] = dalog[:, 0, :HEADS_PER_GROUP].reshape(h_n)
    gs['d_skip'] = ddsk[:, 0, :].reshape(h_n, HEAD_DIM).sum(axis=-1)
    gs['ssm_norm_g'] = dsng[:, 0, :].reshape(d_ssm)
    dws_c, dbs_c = [], []
    off = 0
    for nm, dpart in (("x", dxs), ("b", dbm), ("c", dcm)):
        dproj, dw_c, db_c = _conv_ssm_bwd(proj, xcol0 + off, wf['conv_ssm_w'], cpre, off, dpart, dproj,
                                          xcol0 + off, "conv_ssm_bwd_" + nm)
        dws_c.append(dw_c[:wf['conv_ssm_w'].shape[0]])
        dbs_c.append(db_c[0])
        off += dpart.shape[1]
    g_cs = jnp.concatenate(dws_c, axis=1)
    gs['conv_ssm_b'] = jnp.concatenate(dbs_c, axis=0)
    ddt = jnp.pad(ddtp.reshape(t, ngrp, 128)[:, :, :HEADS_PER_GROUP].reshape(t, h_n), ((0, 0), (0, 128 - h_n))).astype(MXU_DTYPE)
    a_t = a_n.T
    g_in = jnp.concatenate([wgrad(a_t, dproj, "wg_in"), wgrad(a_t, ddt, "wg_dt")[:, :h_n]], axis=1)
    tok = comm.send('3', {'conv_ssm_w': g_cs, 'w_in': g_in}, [(n, gs[n]) for n in REPLICATED if n != 'norm_mix_g'])
    da = _mm_nt(ddt + tok.astype(ddt.dtype), w_dt, out_dtype=F32, name="dg_dt")
    da = _mm_nt(dproj, w_main, out_dtype=F32, name="dg_in", res=da)
    dx, _, dg = _rms_bwd(x, g_mix + tok, da, dh1, "rms_mix_bwd")
    comm.send('4', {}, [('norm_mix_g', _row0(dg)), ('loss', loss[0, 0:1])])
    return dx


def kernel(x, p, norm_mix_g, w_in, ln_a_g, ln_a_b, w_s, b_s, norm_a_g, conv_ssm_w, conv_ssm_b, dt_bias, a_log, d_skip, ssm_norm_g, w_out, norm_ffn_g, w_up, conv_ffn_w, conv_ffn_b, w_down, norm_ple_g, w_ple_gate, w_ple, norm_final_g, loss_target, m_norm_mix_g, m_w_in, m_ln_a_g, m_ln_a_b, m_w_s, m_b_s, m_norm_a_g, m_conv_ssm_w, m_conv_ssm_b, m_dt_bias, m_a_log, m_d_skip, m_ssm_norm_g, m_w_out, m_norm_ffn_g, m_w_up, m_conv_ffn_w, m_conv_ffn_b, m_w_down, m_norm_ple_g, m_w_ple_gate, m_w_ple, m_norm_final_g, v_norm_mix_g, v_w_in, v_ln_a_g, v_ln_a_b, v_w_s, v_b_s, v_norm_a_g, v_conv_ssm_w, v_conv_ssm_b, v_dt_bias, v_a_log, v_d_skip, v_ssm_norm_g, v_w_out, v_norm_ffn_g, v_w_up, v_conv_ffn_w, v_conv_ffn_b, v_w_down, v_norm_ple_g, v_w_ple_gate, v_w_ple, v_norm_final_g):
    given = dict(locals())
    wts = {n: given[n] for n in WEIGHTS}
    ms = {n: given["m_" + n] for n in WEIGHTS}
    vs = {n: given["v_" + n] for n in WEIGHTS}
    sm = {n: (wts[n][0] if wts[n].ndim > 1 else wts[n]) for n in REPLICATED}
    comm = _Comm({n: wts[n][0] for n in SHARDED})
    dx = _local_step(x[0], p[0, 0], loss_target[0], sm, comm)

    out, loss_out, after = {}, None, dx
    for tag, names, small_names, layout, handle in comm.sent:
        recv = _xfer_wait(handle, after, "grads_%s_wait" % tag)
        for n, parts in zip(names, recv):
            out[n] = _adamw(parts, wts[n][0], ms[n][0], vs[n][0], "adamw_" + n)
            after = out[n][1]
        if small_names:
            pick = lambda src, fill: _pack_small([src[n] if n in src else jnp.full((1,), fill, F32) for n in small_names])[0]
            res = _adamw(recv[-1], pick(wts, 0.0), pick(ms, 0.0), pick(vs, 1.0), "adamw_small_" + tag)
            res = [_unpack_small(o, layout) for o in res]
            after = res[1][0]
            for i, n in enumerate(small_names):
                if n == 'loss':
                    loss_out = res[0][i].reshape(())
                else:
                    out[n] = [res[k][i] for k in range(4)]
    return (loss_out, dx[None], *[out[n][k].reshape(wts[n].shape) for k in range(4) for n in WEIGHTS])


class _Comm:
    GATHER_GROUPS = {'a': ('w_in', 'conv_ssm_w'), 'b': ('w_out', 'w_up', 'conv_ffn_w', 'w_down', 'w_ple_gate', 'w_ple')}

    def __init__(self, blocks):
        self.handles, tok = {}, None
        for grp, names in self.GATHER_GROUPS.items():
            items = [('gather', blocks[n].astype(_wire(n))) for n in names]
            self.handles[grp], t = _xfer_start(items, "gather_%s_start" % grp, after=tok)
            tok = t
            self.tok0 = t[0, 0] if grp == 'a' else self.tok0 + t[0, 0]
        self.sent = []

    def weights(self, grp, after):
        stacks = _xfer_wait(self.handles[grp], after, "gather_%s_wait" % grp)
        return {n: _stack_to_full(SHARDED[n], st) for n, st in zip(self.GATHER_GROUPS[grp], stacks)}

    def send(self, tag, gw, small=None):
        items = [('scatter', _full_to_stack(SHARDED[n], g.astype(_wire(n)))) for n, g in gw.items()]
        layout, small_names = None, []
        if small:
            packed, layout = _pack_small([a for _, a in small])
            small_names = [n for n, _ in small]
            items.append(('gather', packed))
        handle, tok = _xfer_start(items, "grads_%s_start" % tag)
        self.sent.append((tag, list(gw), small_names, layout, handle))
        return tok[0, 0]


def _wire(name):
    return F32 if name in F32_ON_WIRE else WIRE_DTYPE
```

```python
import functools

import jax
import jax.numpy as jnp
from jax import lax
from jax.experimental import pallas as pl
from jax.experimental.pallas import tpu as pltpu

F32 = jnp.float32
MXU_DTYPE = jnp.bfloat16
ACT_DTYPE = jnp.bfloat16
XBC_DTYPE = jnp.bfloat16
WIRE_DTYPE = jnp.bfloat16
EPS = 1e-6
CHUNK = 128
D_STATE = 128
HEAD_DIM = 64
HEADS_PER_GROUP = 4
GROUP_CH = HEAD_DIM * HEADS_PER_GROUP
HALO = 16
N_DEV = 8
VMEM_LIMIT = 48 * 1024 * 1024

ADAM_LR = 0.001
ADAM_B1 = 0.9
ADAM_B2 = 0.999
ADAM_EPS = 1e-08
ADAM_WD = 0.01
ADAM_STEP = 10

WEIGHTS = ['norm_mix_g', 'w_in', 'ln_a_g', 'ln_a_b', 'w_s', 'b_s', 'norm_a_g', 'conv_ssm_w', 'conv_ssm_b', 'dt_bias',
           'a_log', 'd_skip', 'ssm_norm_g', 'w_out', 'norm_ffn_g', 'w_up', 'conv_ffn_w', 'conv_ffn_b', 'w_down',
           'norm_ple_g', 'w_ple_gate', 'w_ple', 'norm_final_g']
SHARDED = {'w_in': 'col', 'conv_ssm_w': 'col', 'w_out': 'row', 'w_up': 'col', 'conv_ffn_w': 'col', 'w_down': 'row',
           'w_ple_gate': 'row', 'w_ple': 'col'}
F32_ON_WIRE = ('conv_ssm_w', 'conv_ffn_w')
REPLICATED = [n for n in WEIGHTS if n not in SHARDED]

S = jax.ShapeDtypeStruct


def _pick(dim, cands):
    for c in cands:
        if c <= dim and dim % c == 0:
            return c
    return dim


def _cp(sem, vmem=VMEM_LIMIT):
    return pltpu.CompilerParams(dimension_semantics=sem, vmem_limit_bytes=vmem)


def _mx(v):
    return v.astype(MXU_DTYPE)


def _rms(v, g):
    return v * lax.rsqrt(jnp.mean(v * v, axis=-1, keepdims=True) + EPS) * g


MM_VMEM_BUDGET = 34 * 1024 * 1024


def _mm_tiles(m, n, k, out_bytes, has_res, tn_cands=(512, 256, 128), k_mult=1):
    tm = _pick(m, (1024, 512, 256, 128))
    tn = _pick(n, tn_cands)
    ks = k // k_mult
    for nk in range(1, ks // 128 + 1):
        if ks % nk or (ks // nk) % 128:
            continue
        tk = ks // nk
        need = 2 * 2 * (tm * tk + tk * tn) + tm * tn * (4 + 2 * out_bytes + (8 if has_res else 0))
        if need <= MM_VMEM_BUDGET:
            return tm, tn, tk
    return tm, tn, _pick(ks, (128,))


def _mm_body(dot, nk, has_res):
    def body(*refs):
        if has_res:
            a_ref, b_ref, r_ref, o_ref, acc_ref = refs
        else:
            a_ref, b_ref, o_ref, acc_ref = refs
            r_ref = None
        kk = pl.program_id(2)
        d = dot(a_ref[...], b_ref[...])

        def fin(acc):
            if r_ref is not None:
                acc = acc + r_ref[...]
            o_ref[...] = acc.astype(o_ref.dtype)

        if nk == 1:
            fin(d)
        else:
            @pl.when(kk == 0)
            def _():
                acc_ref[...] = d

            @pl.when(kk > 0)
            def _():
                acc_ref[...] += d

            @pl.when(kk == nk - 1)
            def _():
                fin(acc_ref[...])

    return body


def _mm_call(body, grid, a_spec, b_spec, tm, tn, m, n, out_dtype, name, args, res, out_slabs=1):
    in_specs = [a_spec, b_spec]
    if res is not None:
        in_specs.append(pl.BlockSpec((tm, tn), lambda i, j, kk: (i, j)))
        args = args + [res]
    if out_slabs == 1:
        out_spec, out_shape = pl.BlockSpec((tm, tn), lambda i, j, kk: (i, j)), S((m, n), out_dtype)
    else:
        out_spec, out_shape = pl.BlockSpec((None, tm, tn), lambda i, j, kk: (j, i, 0)), S((out_slabs, m, tn), out_dtype)
    return pl.pallas_call(
        body, grid=grid, in_specs=in_specs, out_specs=out_spec, out_shape=out_shape,
        scratch_shapes=[pltpu.VMEM((tm, tn), F32)], name=name,
        compiler_params=_cp(("parallel", "parallel", "arbitrary")))(*args)


def _mm_nn(a, b, *, out_dtype, name, res=None, b_split=1, wide=False, out_slabs=1):
    m, k = a.shape
    n = b.shape[1] if b_split == 1 else b.shape[2] * b_split
    tn_cands = (n // out_slabs,) if out_slabs > 1 else (1024, 512, 256, 128) if wide else (512, 256, 128)
    tm, tn, tk = _mm_tiles(m, n // b_split, k, jnp.dtype(out_dtype).itemsize, res is not None, tn_cands=tn_cands)
    assert out_slabs == 1 or (tn * out_slabs == n and tn % 128 == 0)
    nk = k // tk
    njs = (n // b_split) // tn
    body = _mm_body(lambda x, y: jnp.dot(x, y, preferred_element_type=F32), nk, res is not None)
    if b_split == 1:
        b_spec = pl.BlockSpec((tk, tn), lambda i, j, kk: (kk, j))
    else:
        b_spec = pl.BlockSpec((None, tk, tn), lambda i, j, kk: (j // njs, kk, j % njs))
    return _mm_call(body, (m // tm, n // tn, nk), pl.BlockSpec((tm, tk), lambda i, j, kk: (i, kk)), b_spec,
                    tm, tn, m, n, out_dtype, name, [a, b], res, out_slabs)


def _mm_nt(a, b, *, out_dtype, name, res=None, a_split=1):
    if a_split == 1:
        m, k = a.shape
    else:
        m, k = a.shape[1], a.shape[2] * a_split
    n = b.shape[0]
    tm, tn, tk = _mm_tiles(m, n, k, jnp.dtype(out_dtype).itemsize, res is not None, k_mult=a_split)
    nk = k // tk
    nks = nk // a_split
    body = _mm_body(lambda x, y: lax.dot_general(x, y, (((1,), (1,)), ((), ())), preferred_element_type=F32),
                    nk, res is not None)
    if a_split == 1:
        a_spec = pl.BlockSpec((tm, tk), lambda i, j, kk: (i, kk))
    else:
        a_spec = pl.BlockSpec((None, tm, tk), lambda i, j, kk: (kk // nks, i, kk % nks))
    return _mm_call(body, (m // tm, n // tn, nk), a_spec, pl.BlockSpec((tn, tk), lambda i, j, kk: (j, kk)),
                    tm, tn, m, n, out_dtype, name, [a, b], res)


def _rms_fwd(x, g, name):
    t, d = x.shape
    tr = _pick(t, (512, 256, 128))

    def body(x_ref, g_ref, o_ref, ot_ref):
        y = _rms(x_ref[...], g_ref[...]).astype(o_ref.dtype)
        o_ref[...] = y
        ot_ref[...] = y.T

    return pl.pallas_call(
        body, grid=(t // tr,),
        in_specs=[pl.BlockSpec((tr, d), lambda i: (i, 0)), pl.BlockSpec((1, d), lambda i: (0, 0))],
        out_specs=[pl.BlockSpec((tr, d), lambda i: (i, 0)), pl.BlockSpec((d, tr), lambda i: (0, i))],
        out_shape=[S((t, d), ACT_DTYPE), S((d, t), ACT_DTYPE)], name=name,
        compiler_params=_cp(("parallel",)))(x, g)


def _rms_bwd(xin, g, dn, dres, name):
    t, d = xin.shape
    tr = _pick(t, (256, 128))

    def body(x_ref, g_ref, dn_ref, dr_ref, dx_ref, dxm_ref, dg_ref):
        @pl.when(pl.program_id(0) == 0)
        def _():
            dg_ref[...] = jnp.zeros_like(dg_ref)

        _, vjp = jax.vjp(_rms, x_ref[...], g_ref[...])
        dx, dg = vjp(dn_ref[...].astype(F32))
        dx = dr_ref[...] + dx
        dx_ref[...] = dx
        dxm_ref[...] = dx.astype(dxm_ref.dtype)
        dg_ref[0:1, :] += dg

    row = pl.BlockSpec((tr, d), lambda i: (i, 0))
    return pl.pallas_call(
        body, grid=(t // tr,),
        in_specs=[row, pl.BlockSpec((1, d), lambda i: (0, 0)), row, row],
        out_specs=[row, row, pl.BlockSpec((8, d), lambda i: (0, 0))],
        out_shape=[S((t, d), F32), S((t, d), MXU_DTYPE), S((8, d), F32)], name=name,
        compiler_params=_cp(("arbitrary",)))(xin, g, dn, dres)


def _head(h2, q, pe, tgt, gf, name):
    t, d = h2.shape
    tr = _pick(t, (256, 128))

    def f(h2v, qv, pev, gfv, tv):
        h3 = h2v + jax.nn.sigmoid(qv) * pev
        y = _rms(h3, gfv)
        return 0.5 * jnp.sum(jnp.mean(jnp.square(y - tv), axis=-1))

    def body(h2_ref, q_ref, pe_ref, t_ref, g_ref, loss_ref, dh_ref, dq_ref, dpe_ref, dg_ref):
        @pl.when(pl.program_id(0) == 0)
        def _():
            loss_ref[...] = jnp.zeros_like(loss_ref)
            dg_ref[...] = jnp.zeros_like(dg_ref)

        tv = t_ref[...]
        loss, vjp = jax.vjp(lambda a, b, c, e: f(a, b, c, e, tv), h2_ref[...], q_ref[...].astype(F32),
                            pe_ref[...].astype(F32), g_ref[...])
        dh, dq, dpe, dg = vjp(jnp.ones((), F32))
        loss_ref[...] += jnp.full(loss_ref.shape, loss, F32)
        dh_ref[...] = dh
        dq_ref[...] = dq.astype(dq_ref.dtype)
        dpe_ref[...] = dpe.astype(dpe_ref.dtype)
        dg_ref[0:1, :] += dg

    row = pl.BlockSpec((tr, d), lambda i: (i, 0))
    return pl.pallas_call(
        body, grid=(t // tr,),
        in_specs=[row, row, row, row, pl.BlockSpec((1, d), lambda i: (0, 0))],
        out_specs=[pl.BlockSpec((8, 128), lambda i: (0, 0)), row, row, row, pl.BlockSpec((8, d), lambda i: (0, 0))],
        out_shape=[S((8, 128), F32), S((t, d), F32), S((t, d), MXU_DTYPE), S((t, d), MXU_DTYPE), S((8, d), F32)],
        name=name, compiler_params=_cp(("arbitrary",)))(h2, q, pe, tgt, gf)


def _gmlp_block(us, vs, lng, lnb, wss, bss, ng):
    n = us[0].shape[0]
    row = lax.broadcasted_iota(jnp.int32, (n, n), 0)
    col = lax.broadcasted_iota(jnp.int32, (n, n), 1)
    outs = []
    for u0, v0, lg, lb, ws, bs in zip(us, vs, lng, lnb, wss, bss):
        u = jax.nn.gelu(u0)
        v = jax.nn.gelu(v0)
        mu = jnp.mean(v, axis=-1, keepdims=True)
        var = jnp.mean(jnp.square(v - mu), axis=-1, keepdims=True)
        vn = (v - mu) * lax.rsqrt(var + EPS) * lg + lb
        w = jnp.where(row >= col, ws, 0.0)
        sg = jnp.dot(_mx(w), _mx(vn), preferred_element_type=F32) + bs
        outs.append(u * sg)
    return _rms(jnp.concatenate(outs, axis=1), ng)


def _gmlp_load(proj_ref, lng_ref, lnb_ref, ws_ref, bst_ref, d_a, ng):
    sl = lambda g: slice(CHUNK * g, CHUNK * (g + 1))
    us = [proj_ref[:, sl(g)].astype(F32) for g in range(ng)]
    vs = [proj_ref[:, d_a + CHUNK * g: d_a + CHUNK * (g + 1)].astype(F32) for g in range(ng)]
    lng = [lng_ref[:, sl(g)] for g in range(ng)]
    lnb = [lnb_ref[:, sl(g)] for g in range(ng)]
    wss = [ws_ref[g] for g in range(ng)]
    bss = [bst_ref[:, g:g + 1] for g in range(ng)]
    return us, vs, lng, lnb, wss, bss


def _gmlp_fwd(proj, ln_g, ln_b, w_s, bst, norm_g, d_mix, name):
    t = proj.shape[0]
    ng = w_s.shape[0]
    d_a = ng * CHUNK

    def body(proj_ref, lng_ref, lnb_ref, ws_ref, bst_ref, ng_ref, o_ref):
        args = _gmlp_load(proj_ref, lng_ref, lnb_ref, ws_ref, bst_ref, d_a, ng)
        o_ref[...] = _gmlp_block(*args, ng_ref[...]).astype(o_ref.dtype)

    vec = pl.BlockSpec((1, d_a), lambda c: (0, 0))
    return pl.pallas_call(
        body, grid=(t // CHUNK,),
        in_specs=[pl.BlockSpec((CHUNK, 2 * d_a), lambda c: (c, 0)), vec, vec,
                  pl.BlockSpec((ng, CHUNK, CHUNK), lambda c: (0, 0, 0)), pl.BlockSpec((CHUNK, 128), lambda c: (0, 0)), vec],
        out_specs=pl.BlockSpec((CHUNK, d_a), lambda c: (c, 0)), out_shape=S((t, d_mix), ACT_DTYPE),
        name=name, compiler_params=_cp(("parallel",)))(proj, ln_g, ln_b, w_s, bst, norm_g)


def _gmlp_bwd(proj, ln_g, ln_b, w_s, bst, norm_g, dcat, d_proj, name):
    t = proj.shape[0]
    ng = w_s.shape[0]
    d_a = ng * CHUNK

    def body(proj_ref, lng_ref, lnb_ref, ws_ref, bst_ref, ng_ref, dy_ref,
             dp_ref, dlng_ref, dlnb_ref, dws_ref, dbst_ref, dng_ref):
        @pl.when(pl.program_id(0) == 0)
        def _():
            for r in (dlng_ref, dlnb_ref, dws_ref, dbst_ref, dng_ref):
                r[...] = jnp.zeros_like(r)

        args = _gmlp_load(proj_ref, lng_ref, lnb_ref, ws_ref, bst_ref, d_a, ng)
        _, vjp = jax.vjp(_gmlp_block, *args, ng_ref[...])
        dus, dvs, dlng, dlnb, dwss, dbss, dng = vjp(dy_ref[...].astype(F32))
        lane = lax.broadcasted_iota(jnp.int32, (1, 128), 1)
        dbst = jnp.zeros((CHUNK, 128), F32)
        for g in range(ng):
            dp_ref[:, CHUNK * g:CHUNK * (g + 1)] = dus[g].astype(dp_ref.dtype)
            dp_ref[:, d_a + CHUNK * g:d_a + CHUNK * (g + 1)] = dvs[g].astype(dp_ref.dtype)
            dlng_ref[0:1, CHUNK * g:CHUNK * (g + 1)] += dlng[g]
            dlnb_ref[0:1, CHUNK * g:CHUNK * (g + 1)] += dlnb[g]
            dws_ref[g] += dwss[g]
            dbst = dbst + dbss[g] * (lane == g).astype(F32)
        dbst_ref[...] += dbst
        dng_ref[0:1, :] += dng

    vec = pl.BlockSpec((1, d_a), lambda c: (0, 0))
    acc = pl.BlockSpec((8, d_a), lambda c: (0, 0))
    wspec = pl.BlockSpec((ng, CHUNK, CHUNK), lambda c: (0, 0, 0))
    bspec = pl.BlockSpec((CHUNK, 128), lambda c: (0, 0))
    return pl.pallas_call(
        body, grid=(t // CHUNK,),
        in_specs=[pl.BlockSpec((CHUNK, 2 * d_a), lambda c: (c, 0)), vec, vec, wspec, bspec, vec,
                  pl.BlockSpec((CHUNK, d_a), lambda c: (c, 0))],
        out_specs=[pl.BlockSpec((CHUNK, 2 * d_a), lambda c: (c, 0)), acc, acc, wspec, bspec, acc],
        out_shape=[S((t, d_proj), ACT_DTYPE), S((8, d_a), F32), S((8, d_a), F32), S((ng, CHUNK, CHUNK), F32),
                   S((CHUNK, 128), F32), S((8, d_a), F32)],
        name=name, compiler_params=_cp(("arbitrary",)))(proj, ln_g, ln_b, w_s, bst, norm_g, dcat)


def _silu_grad(c):
    s = jax.nn.sigmoid(c)
    return s * (1.0 + c * (1.0 - s))


def _fill_prev_main(s_ref, prev_ref, main_ref, i, tt):
    s_ref[pl.ds(0, HALO), :] = jnp.where(i > 0, prev_ref[...].astype(F32), 0.0)
    s_ref[pl.ds(HALO, tt), :] = main_ref[...].astype(F32)


def _prev_spec(tt, tc, joff):
    return pl.BlockSpec((HALO, tc), lambda j, i: (jnp.maximum(i * (tt // HALO) - 1, 0), j + joff))


def _next_spec(tt, tc, joff, t):
    return pl.BlockSpec((HALO, tc), lambda j, i: (jnp.minimum((i + 1) * (tt // HALO), t // HALO - 1), j + joff))


CONV_RC = 32


def _conv_tiles(t, c):
    return _pick(t, (1024, 512, 256, 128)), _pick(c, (256, 128))


def _row_chunks(tt, fn, init=0):
    rc = min(CONV_RC, tt)
    return lax.fori_loop(0, tt // rc, lambda q, c: fn(pl.multiple_of(q * rc, rc), rc, c), init)


def _fold8(p):
    acc = p[0:8]
    for r in range(8, p.shape[0], 8):
        acc = acc + p[r:r + 8]
    return acc


def _taps_chunk(s_ref, w_ref, kw, r0, rc):
    xe = s_ref[pl.ds(HALO - 8 + r0, rc + 8), :]
    acc = w_ref[0:1, :] * xe[8 - (kw - 1):8 - (kw - 1) + rc]
    for k in range(1, kw):
        acc = acc + w_ref[k:k + 1, :] * xe[8 - (kw - 1) + k:8 - (kw - 1) + k + rc]
    return acc


def _conv_bwd_chunk(sd_ref, x, w_ref, kw, r0, rc, dws):
    de = sd_ref[pl.ds(r0, rc + 8), :]
    dx, out = None, list(dws)
    for j in range(kw):
        d = de[j:j + rc]
        k = kw - 1 - j
        term = w_ref[k:k + 1, :] * d
        dx = term if dx is None else dx + term
        out[k] = out[k] + _fold8(x * d)
    return dx, out


def _conv_ssm_fwd(proj, col0, w, b, name):
    t = proj.shape[0]
    kw, c = w.shape
    tt, tc = _conv_tiles(t, c)
    joff = col0 // tc
    assert col0 % tc == 0

    def body(x_ref, xp_ref, w_ref, b_ref, o_ref, c_ref, s_ref):
        _fill_prev_main(s_ref, xp_ref, x_ref, pl.program_id(1), tt)

        def chunk(r0, rc, carry):
            cpre = _taps_chunk(s_ref, w_ref, kw, r0, rc) + b_ref[...]
            o_ref[pl.ds(r0, rc), :] = jax.nn.silu(cpre).astype(o_ref.dtype)
            c_ref[pl.ds(r0, rc), :] = cpre.astype(c_ref.dtype)
            return carry

        _row_chunks(tt, chunk)

    out = pl.BlockSpec((tt, tc), lambda j, i: (i, j))
    return pl.pallas_call(
        body, grid=(c // tc, t // tt),
        in_specs=[pl.BlockSpec((tt, tc), lambda j, i: (i, j + joff)), _prev_spec(tt, tc, joff),
                  pl.BlockSpec((kw, tc), lambda j, i: (0, j)), pl.BlockSpec((1, tc), lambda j, i: (0, j))],
        out_specs=[out, out], out_shape=[S((t, c), XBC_DTYPE), S((t, c), ACT_DTYPE)],
        scratch_shapes=[pltpu.VMEM((HALO + tt, tc), F32)], name=name,
        compiler_params=_cp(("parallel", "arbitrary")))(proj, proj, w, b)


def _conv_ssm_bwd(proj, col0, w, cpre, wcol0, dact, dproj, out_col0, name):
    t = proj.shape[0]
    kw = w.shape[0]
    c = dact.shape[1]
    tt, tc = _conv_tiles(t, c)
    assert col0 % tc == 0 and wcol0 % tc == 0 and out_col0 % tc == 0
    joff, wj, oj = col0 // tc, wcol0 // tc, out_col0 // tc
    nt = t // tt

    def body(x_ref, w_ref, c_ref, cn_ref, d_ref, dn_ref, dp_in, dx_ref, dw_ref, db_ref, sd_ref):
        del dp_in
        i = pl.program_id(1)

        @pl.when(i == 0)
        def _():
            dw_ref[...] = jnp.zeros_like(dw_ref)
            db_ref[...] = jnp.zeros_like(db_ref)

        def stage(r0, rc, db):
            rows = pl.ds(r0, rc)
            d = d_ref[rows, :].astype(F32) * _silu_grad(c_ref[rows, :].astype(F32))
            sd_ref[rows, :] = d
            return db + _fold8(d)

        zero8 = jnp.zeros((8, tc), F32)
        db = _row_chunks(tt, stage, zero8)
        sd_ref[pl.ds(tt, HALO), :] = jnp.where(
            i < nt - 1, dn_ref[...].astype(F32) * _silu_grad(cn_ref[...].astype(F32)), 0.0)

        def chunk(r0, rc, dws):
            dx, dws = _conv_bwd_chunk(sd_ref, x_ref[pl.ds(r0, rc), :].astype(F32), w_ref, kw, r0, rc, dws)
            dx_ref[pl.ds(r0, rc), :] = dx.astype(dx_ref.dtype)
            return dws

        dws = _row_chunks(tt, chunk, [zero8] * kw)
        for k in range(kw):
            dw_ref[k:k + 1, :] += jnp.sum(dws[k], axis=0, keepdims=True)
        db_ref[0:1, :] += jnp.sum(db, axis=0, keepdims=True)

    acc = pl.BlockSpec((8, tc), lambda j, i: (0, j))
    return pl.pallas_call(
        body, grid=(c // tc, nt),
        in_specs=[pl.BlockSpec((tt, tc), lambda j, i: (i, j + joff)), pl.BlockSpec((kw, tc), lambda j, i: (0, j + wj)),
                  pl.BlockSpec((tt, tc), lambda j, i: (i, j + wj)), _next_spec(tt, tc, wj, t),
                  pl.BlockSpec((tt, tc), lambda j, i: (i, j)), _next_spec(tt, tc, 0, t),
                  pl.BlockSpec(memory_space=pl.ANY)],
        out_specs=[pl.BlockSpec((tt, tc), lambda j, i: (i, j + oj)), acc, acc],
        out_shape=[S(dproj.shape, dproj.dtype), S((8, c), F32), S((8, c), F32)],
        scratch_shapes=[pltpu.VMEM((tt + HALO, tc), F32)],
        input_output_aliases={6: 0}, name=name,
        compiler_params=_cp(("parallel", "arbitrary")))(proj, w, cpre, cpre, dact, dact, dproj)


def _conv_ffn_fwd(hid, w, b, name):
    t, f2 = hid.shape
    f = f2 // 2
    kw = w.shape[0]
    tt, tc = _conv_tiles(t, f)
    nj = f // tc

    def body(g_ref, gp_ref, u_ref, up_ref, wg_ref, wu_ref, bg_ref, bu_ref, o_ref, ot_ref, cv_ref, sg_ref, su_ref):
        i = pl.program_id(1)
        _fill_prev_main(sg_ref, gp_ref, g_ref, i, tt)
        _fill_prev_main(su_ref, up_ref, u_ref, i, tt)

        def chunk(r0, rc, carry):
            rows = pl.ds(r0, rc)
            gate = _taps_chunk(sg_ref, wg_ref, kw, r0, rc) + bg_ref[...]
            up = _taps_chunk(su_ref, wu_ref, kw, r0, rc) + bu_ref[...]
            o_ref[rows, :] = (jax.nn.silu(gate) * up).astype(o_ref.dtype)
            cv_ref[0, rows, :] = gate.astype(cv_ref.dtype)
            cv_ref[1, rows, :] = up.astype(cv_ref.dtype)
            return carry

        _row_chunks(tt, chunk)
        ot_ref[...] = o_ref[...].T

    return pl.pallas_call(
        body, grid=(nj, t // tt),
        in_specs=[pl.BlockSpec((tt, tc), lambda j, i: (i, j)), _prev_spec(tt, tc, 0),
                  pl.BlockSpec((tt, tc), lambda j, i: (i, j + nj)), _prev_spec(tt, tc, nj),
                  pl.BlockSpec((kw, tc), lambda j, i: (0, j)), pl.BlockSpec((kw, tc), lambda j, i: (0, j + nj)),
                  pl.BlockSpec((1, tc), lambda j, i: (0, j)), pl.BlockSpec((1, tc), lambda j, i: (0, j + nj))],
        out_specs=[pl.BlockSpec((tt, tc), lambda j, i: (i, j)), pl.BlockSpec((tc, tt), lambda j, i: (j, i)),
                   pl.BlockSpec((2, tt, tc), lambda j, i: (0, i, j))],
        out_shape=[S((t, f), ACT_DTYPE), S((f, t), ACT_DTYPE), S((2, t, f), ACT_DTYPE)],
        scratch_shapes=[pltpu.VMEM((HALO + tt, tc), F32), pltpu.VMEM((HALO + tt, tc), F32)], name=name,
        compiler_params=_cp(("parallel", "arbitrary")))(hid, hid, hid, hid, w, w, b, b)


def _conv_ffn_bwd(hid, w, cv, dact, name):
    t, f2 = hid.shape
    f = f2 // 2
    kw = w.shape[0]
    tt, tc = _conv_tiles(t, f)
    nj = f // tc
    nt = t // tt

    def body(g_ref, u_ref, wg_ref, wu_ref, cv_ref, cvn_ref, d_ref, dn_ref, dh_ref, dw_ref, db_ref, dg_ref, du_ref):
        i = pl.program_id(1)

        @pl.when(i == 0)
        def _():
            dw_ref[...] = jnp.zeros_like(dw_ref)
            db_ref[...] = jnp.zeros_like(db_ref)

        def cotangents(gate, up, dact_v):
            sg = jax.nn.sigmoid(gate)
            return dact_v * up * (sg * (1.0 + gate * (1.0 - sg))), dact_v * (gate * sg)

        def stage(r0, rc, dbs):
            rows = pl.ds(r0, rc)
            dg, du = cotangents(cv_ref[0, rows, :].astype(F32), cv_ref[1, rows, :].astype(F32), d_ref[rows, :].astype(F32))
            dg_ref[rows, :] = dg
            du_ref[rows, :] = du
            return [dbs[0] + _fold8(dg), dbs[1] + _fold8(du)]

        zero8 = jnp.zeros((8, tc), F32)
        dbs = _row_chunks(tt, stage, [zero8, zero8])
        dgn, dun = cotangents(cvn_ref[0].astype(F32), cvn_ref[1].astype(F32), dn_ref[...].astype(F32))
        dg_ref[pl.ds(tt, HALO), :] = jnp.where(i < nt - 1, dgn, 0.0)
        du_ref[pl.ds(tt, HALO), :] = jnp.where(i < nt - 1, dun, 0.0)
        for s, (sd_ref, x_ref, w_ref) in enumerate(((dg_ref, g_ref, wg_ref), (du_ref, u_ref, wu_ref))):
            def chunk(r0, rc, dws, s=s, sd_ref=sd_ref, x_ref=x_ref, w_ref=w_ref):
                dx, dws = _conv_bwd_chunk(sd_ref, x_ref[pl.ds(r0, rc), :].astype(F32), w_ref, kw, r0, rc, dws)
                dh_ref[s, pl.ds(r0, rc), :] = dx.astype(dh_ref.dtype)
                return dws

            dws = _row_chunks(tt, chunk, [zero8] * kw)
            for k in range(kw):
                dw_ref[s, k:k + 1, :] += jnp.sum(dws[k], axis=0, keepdims=True)
            db_ref[s, 0:1, :] += jnp.sum(dbs[s], axis=0, keepdims=True)

    acc = pl.BlockSpec((2, 8, tc), lambda j, i: (0, 0, j))
    dsc = pltpu.VMEM((tt + HALO, tc), F32)
    nxt = lambda j, i: (0, jnp.minimum((i + 1) * (tt // HALO), t // HALO - 1), j)
    return pl.pallas_call(
        body, grid=(nj, nt),
        in_specs=[pl.BlockSpec((tt, tc), lambda j, i: (i, j)), pl.BlockSpec((tt, tc), lambda j, i: (i, j + nj)),
                  pl.BlockSpec((kw, tc), lambda j, i: (0, j)), pl.BlockSpec((kw, tc), lambda j, i: (0, j + nj)),
                  pl.BlockSpec((2, tt, tc), lambda j, i: (0, i, j)), pl.BlockSpec((2, HALO, tc), nxt),
                  pl.BlockSpec((tt, tc), lambda j, i: (i, j)), _next_spec(tt, tc, 0, t)],
        out_specs=[pl.BlockSpec((2, tt, tc), lambda j, i: (0, i, j)), acc, acc],
        out_shape=[S((2, t, f), MXU_DTYPE), S((2, 8, f), F32), S((2, 8, f), F32)],
        scratch_shapes=[dsc, dsc], name=name,
        compiler_params=_cp(("parallel", "arbitrary")))(hid, hid, w, w, cv, cv, dact, dact)


SSD_SUB = 2


def _ssd_chunk(xs, bm, cm, dtraw, z, hin, bias, alog, dskip, normg):
    n = bm.shape[0]
    row = lax.broadcasted_iota(jnp.int32, (n, n), 0)
    col = lax.broadcasted_iota(jnp.int32, (n, n), 1)
    causal = row >= col
    tri = causal.astype(F32)
    lane = lax.broadcasted_iota(jnp.int32, (1, 128), 1)
    sub = lax.broadcasted_iota(jnp.int32, (128, 1), 0)
    last = (lax.broadcasted_iota(jnp.int32, (n, 1), 0) == n - 1).astype(F32)
    dt = jax.nn.softplus(dtraw + bias)
    adt = dt * (-jnp.exp(alog))
    acs = jnp.dot(tri, adt, preferred_element_type=F32, precision=lax.Precision.HIGHEST)
    ch = lax.broadcasted_iota(jnp.int32, (128, GROUP_CH), 1)
    hd = lax.broadcasted_iota(jnp.int32, (128, GROUP_CH), 0) * HEAD_DIM
    expand = ((ch >= hd) & (ch < hd + HEAD_DIM)).astype(F32)
    acs_x = jnp.dot(acs, expand, preferred_element_type=F32, precision=lax.Precision.HIGHEST)
    alast_x = jnp.sum(acs_x * last, axis=0, keepdims=True)
    acs_t = acs.T
    scores = lax.dot_general(_mx(cm), _mx(bm), (((1,), (1,)), ((), ())), preferred_element_type=F32)
    yds, xts = [], []
    for r in range(HEADS_PER_GROUP):
        pick = (lane == r).astype(F32)
        acol = jnp.sum(acs * pick, axis=1, keepdims=True)
        arow = jnp.sum(acs_t * (sub == r).astype(F32), axis=0, keepdims=True)
        dtc = jnp.sum(dt * pick, axis=1, keepdims=True)
        lm = jnp.exp(jnp.where(causal, acol - arow, -1e30))
        xts.append(xs[r] * dtc)
        yds.append(jnp.dot(_mx(scores * lm), _mx(xts[r]), preferred_element_type=F32))
    xt = jnp.concatenate(xts, axis=1)
    yo = jnp.exp(acs_x) * jnp.dot(_mx(cm), _mx(hin), preferred_element_type=F32)
    st = lax.dot_general(_mx(bm), _mx(xt * jnp.exp(alast_x - acs_x)), (((0,), (0,)), ((), ())), preferred_element_type=F32)
    hout = jnp.exp(alast_x) * hin + st
    y = (jnp.concatenate(yds, axis=1) + yo + dskip * jnp.concatenate(xs, axis=1)) * jax.nn.silu(z)
    return _rms(y, normg), hout


def _ssd_block(datas, hin, consts):
    ys = []
    for data in datas:
        y, hin = _ssd_chunk(*data, hin, *consts)
        ys.append(y)
    return ys, hin


def _ssd_specs(d_ssm, ngrp, zcol0, rev, nb):
    cc = (lambda c: nb - 1 - c) if rev else (lambda c: c)
    rows = SSD_SUB * CHUNK
    xj, bj, cj, zj = 0, d_ssm // 128, d_ssm // 128 + ngrp, zcol0 // GROUP_CH
    const = lambda w: pl.BlockSpec((None, 8, w), lambda g, c: (g, 0, 0))
    return cc, [
        pl.BlockSpec((rows, GROUP_CH), lambda g, c: (cc(c), xj + g)),
        pl.BlockSpec((rows, 128), lambda g, c: (cc(c), bj + g)),
        pl.BlockSpec((rows, 128), lambda g, c: (cc(c), cj + g)),
        pl.BlockSpec((rows, 128), lambda g, c: (cc(c), g)),
        pl.BlockSpec((rows, GROUP_CH), lambda g, c: (cc(c), zj + g)),
        const(128), const(128), const(GROUP_CH), const(GROUP_CH)]


def _sub_rows(s):
    return slice(CHUNK * s, CHUNK * (s + 1))


def _ssd_load(x_ref, b_ref, c_ref, dt_ref, z_ref, bias_ref, alog_ref, dsk_ref, ng_ref):
    datas = []
    for s in range(SSD_SUB):
        rows = _sub_rows(s)
        xs = [x_ref[rows, HEAD_DIM * r:HEAD_DIM * (r + 1)].astype(F32) for r in range(HEADS_PER_GROUP)]
        datas.append((xs, b_ref[rows, :].astype(F32), c_ref[rows, :].astype(F32), dt_ref[rows, :],
                      z_ref[rows, :].astype(F32)))
    return datas, (bias_ref[0:1, :], alog_ref[0:1, :], dsk_ref[0:1, :], ng_ref[0:1, :])


def _ssd_fwd(xbc, dtp, proj, zcol0, bias_p, alog_p, dskip_x, normg_x, cat, name):
    t = xbc.shape[0]
    ngrp = bias_p.shape[0]
    d_ssm = ngrp * GROUP_CH
    rows = SSD_SUB * CHUNK
    nb = t // rows
    d_a = cat.shape[1] - d_ssm
    assert d_a % GROUP_CH == 0 and zcol0 % GROUP_CH == 0 and t % rows == 0
    _, specs = _ssd_specs(d_ssm, ngrp, zcol0, False, nb)

    def body(x_ref, b_ref, c_ref, dt_ref, z_ref, bias_ref, alog_ref, dsk_ref, ng_ref, cat_in, y_ref, hs_ref, h_ref):
        del cat_in

        @pl.when(pl.program_id(1) == 0)
        def _():
            h_ref[...] = jnp.zeros_like(h_ref)

        datas, consts = _ssd_load(x_ref, b_ref, c_ref, dt_ref, z_ref, bias_ref, alog_ref, dsk_ref, ng_ref)
        hs_ref[...] = h_ref[...]
        ys, hout = _ssd_block(datas, h_ref[...], consts)
        for s in range(SSD_SUB):
            y_ref[_sub_rows(s), :] = ys[s].astype(y_ref.dtype)
        h_ref[...] = hout

    return pl.pallas_call(
        body, grid=(ngrp, nb), in_specs=specs + [pl.BlockSpec(memory_space=pl.ANY)],
        out_specs=[pl.BlockSpec((rows, GROUP_CH), lambda g, c: (c, d_a // GROUP_CH + g)),
                   pl.BlockSpec((None, None, D_STATE, GROUP_CH), lambda g, c: (c, g, 0, 0))],
        out_shape=[S(cat.shape, cat.dtype), S((nb, ngrp, D_STATE, GROUP_CH), F32)],
        scratch_shapes=[pltpu.VMEM((D_STATE, GROUP_CH), F32)],
        input_output_aliases={9: 0}, name=name,
        compiler_params=_cp(("parallel", "arbitrary")))(xbc, xbc, xbc, dtp, proj, bias_p, alog_p, dskip_x, normg_x, cat)


def _ssd_bwd(xbc, dtp, proj, zcol0, bias_p, alog_p, dskip_x, normg_x, hs, dcat, dproj, name):
    t = xbc.shape[0]
    ngrp = bias_p.shape[0]
    d_ssm = ngrp * GROUP_CH
    rows = SSD_SUB * CHUNK
    nb = t // rows
    d_a = dcat.shape[1] - d_ssm
    cc, specs = _ssd_specs(d_ssm, ngrp, zcol0, True, nb)

    def body(x_ref, b_ref, c_ref, dt_ref, z_ref, bias_ref, alog_ref, dsk_ref, ng_ref, hs_ref, dy_ref, dp_in,
             dz_ref, dx_ref, db_ref, dc_ref, ddt_ref, dbias_ref, dalog_ref, ddsk_ref, dng_ref, dh_ref):
        del dp_in

        @pl.when(pl.program_id(1) == 0)
        def _():
            dh_ref[...] = jnp.zeros_like(dh_ref)
            for r in (dbias_ref, dalog_ref, ddsk_ref, dng_ref):
                r[...] = jnp.zeros_like(r)

        datas, consts = _ssd_load(x_ref, b_ref, c_ref, dt_ref, z_ref, bias_ref, alog_ref, dsk_ref, ng_ref)
        _, vjp = jax.vjp(_ssd_block, datas, hs_ref[...], consts)
        dys = [dy_ref[_sub_rows(s), :].astype(F32) for s in range(SSD_SUB)]
        ddatas, dhin, (dbias, dalog, ddsk, dng) = vjp((dys, dh_ref[...]))
        for s, (dxs, dbm, dcm, ddt, dz) in enumerate(ddatas):
            rws = _sub_rows(s)
            for r in range(HEADS_PER_GROUP):
                dx_ref[rws, HEAD_DIM * r:HEAD_DIM * (r + 1)] = dxs[r].astype(dx_ref.dtype)
            db_ref[rws, :] = dbm.astype(db_ref.dtype)
            dc_ref[rws, :] = dcm.astype(dc_ref.dtype)
            ddt_ref[rws, :] = ddt
            dz_ref[rws, :] = dz.astype(dz_ref.dtype)
        dh_ref[...] = dhin
        dbias_ref[0:1, :] += dbias
        dalog_ref[0:1, :] += dalog
        ddsk_ref[0:1, :] += ddsk
        dng_ref[0:1, :] += dng

    acc = lambda w: pl.BlockSpec((None, 8, w), lambda g, c: (g, 0, 0))
    blk = lambda w: pl.BlockSpec((rows, w), lambda g, c: (cc(c), g))
    return pl.pallas_call(
        body, grid=(ngrp, nb),
        in_specs=specs + [pl.BlockSpec((None, None, D_STATE, GROUP_CH), lambda g, c: (cc(c), g, 0, 0)),
                          pl.BlockSpec((rows, GROUP_CH), lambda g, c: (cc(c), d_a // GROUP_CH + g)),
                          pl.BlockSpec(memory_space=pl.ANY)],
        out_specs=[pl.BlockSpec((rows, GROUP_CH), lambda g, c: (cc(c), zcol0 // GROUP_CH + g)),
                   blk(GROUP_CH), blk(128), blk(128), blk(128), acc(128), acc(128), acc(GROUP_CH), acc(GROUP_CH)],
        out_shape=[S(dproj.shape, dproj.dtype), S((t, d_ssm), XBC_DTYPE), S((t, ngrp * 128), XBC_DTYPE),
                   S((t, ngrp * 128), XBC_DTYPE), S((t, ngrp * 128), F32), S((ngrp, 8, 128), F32),
                   S((ngrp, 8, 128), F32), S((ngrp, 8, GROUP_CH), F32), S((ngrp, 8, GROUP_CH), F32)],
        scratch_shapes=[pltpu.VMEM((D_STATE, GROUP_CH), F32)],
        input_output_aliases={11: 0}, name=name,
        compiler_params=_cp(("parallel", "arbitrary")))(xbc, xbc, xbc, dtp, proj, bias_p, alog_p, dskip_x, normg_x, hs, dcat, dproj)


def _adamw(parts, w, m, v, name):
    r, c = w.shape
    tr = _pick(r, (256, 128, 64, 32, 16, 8)) if c * 4 * 256 <= 4 * 1024 * 1024 else _pick(r, (64, 32, 16, 8))

    def body(p_ref, w_ref, m_ref, v_ref, g_ref, d_ref, nm_ref, nv_ref):
        g = p_ref[0].astype(F32)
        for k in range(1, N_DEV):
            g = g + p_ref[k].astype(F32)
        mm = ADAM_B1 * m_ref[...] + (1.0 - ADAM_B1) * g
        vv = ADAM_B2 * v_ref[...] + (1.0 - ADAM_B2) * jnp.square(g)
        m_hat = mm / (1.0 - ADAM_B1 ** ADAM_STEP)
        v_hat = vv / (1.0 - ADAM_B2 ** ADAM_STEP)
        g_ref[...] = g
        d_ref[...] = -ADAM_LR * (m_hat / (jnp.sqrt(v_hat) + ADAM_EPS) + ADAM_WD * w_ref[...])
        nm_ref[...] = mm
        nv_ref[...] = vv

    blk = pl.BlockSpec((tr, c), lambda i: (i, 0))
    return pl.pallas_call(
        body, grid=(r // tr,),
        in_specs=[pl.BlockSpec((N_DEV, tr, c), lambda i: (0, i, 0)), blk, blk, blk],
        out_specs=[blk, blk, blk, blk], out_shape=[S((r, c), F32)] * 4, name=name,
        compiler_params=_cp(("parallel",)))(parts, w, m, v)


def _mesh_pos():
    return lax.axis_index("x"), lax.axis_index("y"), lax.axis_index("c")


def _peer(d, x, y, c):
    return (1 - x if (d >> 2) & 1 else x, 1 - y if (d >> 1) & 1 else y, 1 - c if d & 1 else c)


def _gather_two_level(blocks, name):
    n = len(blocks)

    def body(*refs):
        srcs, outs = refs[:n], refs[n:2 * n]
        send_sems, recv_sems, loc_sems = refs[2 * n:]
        x, y, c = _mesh_pos()
        lin = lambda px, py, pc: 4 * px + 2 * py + pc
        me, sibling = (x, y, c), (x, y, 1 - c)
        chips = [(1 - x, y), (x, 1 - y), (1 - x, 1 - y)]

        def copy(a, k, block, to, src=None):
            slab = outs[a].at[lin(*block)]
            return pltpu.make_async_remote_copy(
                src_ref=slab if src is None else src, dst_ref=slab, send_sem=send_sems.at[a, k],
                recv_sem=recv_sems.at[a, k], device_id=to, device_id_type=pl.DeviceIdType.MESH)

        mine = [pltpu.make_async_copy(srcs[a], outs[a].at[lin(*me)], loc_sems.at[a]) for a in range(n)]
        first = [copy(a, 0, me, sibling, src=srcs[a]) for a in range(n)]
        first += [copy(a, 1 + j, me, (*chip, c), src=srcs[a]) for j, chip in enumerate(chips) for a in range(n)]
        for cp in mine + first:
            cp.start()
        passed = []
        for j, chip in enumerate(chips):
            for a in range(n):
                copy(a, 1 + j, (*chip, c), me).wait_recv()
                passed.append(copy(a, 4 + j, (*chip, c), sibling))
                passed[-1].start()
        for a in range(n):
            copy(a, 0, sibling, me).wait_recv()
            for j, chip in enumerate(chips):
                copy(a, 4 + j, (*chip, 1 - c), me).wait_recv()
        for cp in first + passed:
            cp.wait_send()
        for cp in mine:
            cp.wait()

    hbm = pl.BlockSpec(memory_space=pl.ANY)
    return pl.pallas_call(
        body, in_specs=[hbm] * n, out_specs=[hbm] * n, out_shape=[S((N_DEV,) + b.shape, b.dtype) for b in blocks],
        scratch_shapes=[pltpu.SemaphoreType.DMA((n, N_DEV - 1)), pltpu.SemaphoreType.DMA((n, N_DEV - 1)),
                        pltpu.SemaphoreType.DMA((n,))],
        name=name, compiler_params=pltpu.CompilerParams(has_side_effects=True))(*blocks)


def _xfer_start(items, name, after=None):
    n = len(items)
    kinds = [k for k, _ in items]
    srcs = [pltpu.with_memory_space_constraint(a, pltpu.HBM) for _, a in items]
    land_shapes = [((N_DEV,) + a.shape if k == 'gather' else a.shape, a.dtype) for k, a in items]
    lands = [pltpu.with_memory_space_constraint(lax.empty(s, dt), pltpu.HBM) for s, dt in land_shapes]
    extra = [] if after is None else [after]

    def body(*refs):
        src_refs, land_refs = refs[:n], refs[n:2 * n]
        outs = refs[2 * n + len(extra):]
        sems = outs[:2 * n]
        token = outs[4 * n]
        x, y, c = _mesh_pos()
        me = 4 * x + 2 * y + c
        for a in range(n):
            for d in range(1, N_DEV):
                px, py, pc = _peer(d, x, y, c)
                src = src_refs[a] if kinds[a] == 'gather' else src_refs[a].at[4 * px + 2 * py + pc]
                pltpu.make_async_remote_copy(
                    src_ref=src, dst_ref=land_refs[a].at[me], send_sem=sems[2 * a].at[d - 1],
                    recv_sem=sems[2 * a + 1].at[d - 1], device_id=(px, py, pc),
                    device_id_type=pl.DeviceIdType.MESH).start()
        token[...] = jnp.zeros_like(token)

    hbm = pl.BlockSpec(memory_space=pltpu.HBM)
    sem = pl.BlockSpec(memory_space=pltpu.SEMAPHORE)
    out_shape = ([pltpu.SemaphoreType.DMA((N_DEV - 1,))] * (2 * n)
                 + [pltpu.HBM(a.shape, a.dtype) for a in srcs] + [pltpu.HBM(s, dt) for s, dt in land_shapes]
                 + [S((8, 128), F32)])
    res = pl.pallas_call(
        body, name=name, out_shape=out_shape,
        in_specs=[hbm] * (2 * n) + [pl.BlockSpec(memory_space=pl.ANY)] * len(extra),
        out_specs=[sem] * (2 * n) + [hbm] * (2 * n) + [pl.BlockSpec(memory_space=pltpu.VMEM)],
        input_output_aliases={**{a: 2 * n + a for a in range(n)}, **{n + a: 3 * n + a for a in range(n)}},
        compiler_params=pltpu.CompilerParams(has_side_effects=pltpu.SideEffectType.DATAFLOW_SIDE_EFFECTING),
    )(*srcs, *lands, *extra)
    return (kinds, res[:2 * n], res[2 * n:3 * n], res[3 * n:4 * n]), res[4 * n]


def _xfer_wait(handle, after, name):
    kinds, sems, src_thru, land_thru = handle
    n = len(kinds)

    def body(*refs):
        land_refs = refs[n:2 * n]
        sem_refs = refs[2 * n:4 * n]
        x, y, c = _mesh_pos()
        me = 4 * x + 2 * y + c
        for a in range(n):
            for d in range(1, N_DEV):
                slab = land_refs[a].at[me]
                cp = pltpu.make_async_remote_copy(
                    src_ref=slab, dst_ref=slab, send_sem=sem_refs[2 * a].at[d - 1], recv_sem=sem_refs[2 * a + 1].at[d - 1],
                    device_id=_peer(d, x, y, c), device_id_type=pl.DeviceIdType.MESH)
                cp.wait_send()
                cp.wait_recv()

    hbm = pl.BlockSpec(memory_space=pltpu.HBM)
    sem = pl.BlockSpec(memory_space=pltpu.SEMAPHORE)
    res = pl.pallas_call(
        body, name=name,
        out_shape=[pltpu.HBM(a.shape, a.dtype) for a in src_thru] + [pltpu.HBM(a.shape, a.dtype) for a in land_thru],
        in_specs=[hbm] * (2 * n) + [sem] * (2 * n) + [pl.BlockSpec(memory_space=pl.ANY)],
        out_specs=[hbm] * (2 * n), input_output_aliases={a: a for a in range(2 * n)},
        compiler_params=pltpu.CompilerParams(has_side_effects=pltpu.SideEffectType.DATAFLOW_SIDE_EFFECTING),
    )(*src_thru, *land_thru, *sems, after)
    x, y, c = _mesh_pos()
    me = 4 * x + 2 * y + c
    out = []
    for a in range(n):
        src = res[a]
        own = src[None] if kinds[a] == 'gather' else lax.dynamic_index_in_dim(src, me, 0, keepdims=True)
        out.append(lax.dynamic_update_index_in_dim(res[n + a], own, me, 0))
    return out


def _stack_to_full(kind, st):
    if kind == 'row':
        return st.reshape(st.shape[0] * st.shape[1], st.shape[2])
    return jnp.transpose(st, (1, 0, 2)).reshape(st.shape[1], st.shape[0] * st.shape[2])


def _full_to_stack(kind, full):
    r, c = full.shape
    if kind == 'row':
        return full.reshape(N_DEV, r // N_DEV, c)
    return jnp.transpose(full.reshape(r, N_DEV, c // N_DEV), (1, 0, 2))


def _pack_small(named):
    layout = [(a.shape, a.size, -(-a.size // 1024) * 8) for a in named]
    total = sum(nr for _, _, nr in layout) * 128
    packed, off = None, 0
    for a, (_, n, nr) in zip(named, layout):
        part = jnp.pad(a.reshape(-1).astype(F32), (off, total - off - n))
        packed = part if packed is None else packed + part
        off += nr * 128
    return packed.reshape(total // 128, 128), layout


def _unpack_small(packed, layout):
    out, r0 = [], 0
    for shape, n, nr in layout:
        out.append(packed[r0:r0 + nr].reshape(-1)[:n].reshape(shape))
        r0 += nr
    return out


def _row0(acc):
    return acc[0]


def _local_step(x, p, tgt, sm, comm):
    t, d = x.shape
    h_n = sm['dt_bias'].shape[-1]
    ngrp = h_n // HEADS_PER_GROUP
    d_ssm = h_n * HEAD_DIM
    d_a = sm['ln_a_g'].shape[-1]
    d_mix = d_a + d_ssm
    d_xbc = sm['conv_ssm_b'].shape[-1]
    d_main = 2 * d_a + d_ssm + d_xbc
    assert d_xbc == d_ssm + 2 * ngrp * D_STATE and h_n <= 128
    zcol0, xcol0 = 2 * d_a, 2 * d_a + d_ssm
    vec = lambda v: v.reshape(1, -1)

    bst = jnp.pad(sm['b_s'].T, ((0, 0), (0, 128 - sm['b_s'].shape[0])))
    grp = lambda v, w: jnp.broadcast_to(jnp.pad(v.reshape(ngrp, 1, -1), ((0, 0), (0, 0), (0, w - v.size // ngrp))), (ngrp, 8, w))
    bias_p, alog_p = grp(sm['dt_bias'], 128), grp(sm['a_log'], 128)
    dskip_x = grp(jnp.repeat(sm['d_skip'], HEAD_DIM), GROUP_CH)
    normg_x = grp(sm['ssm_norm_g'], GROUP_CH)
    pad_dt = lambda v: jnp.pad(v[:, :h_n].reshape(t, ngrp, HEADS_PER_GROUP),
                               ((0, 0), (0, 0), (0, 128 - HEADS_PER_GROUP))).reshape(t, ngrp * 128)

    g_mix = vec(sm['norm_mix_g']) + comm.tok0
    a_n, a_t = _rms_fwd(x, g_mix, "rms_mix")
    wf = comm.weights('a', a_n)
    w_main = wf['w_in'][:, :d_main]
    w_dt = jnp.pad(wf['w_in'][:, d_main:], ((0, 0), (0, 128 - h_n)))
    proj = _mm_nn(a_n, w_main, out_dtype=ACT_DTYPE, name="mm_in")
    dtp = pad_dt(_mm_nn(a_n, w_dt, out_dtype=F32, name="mm_dt"))
    cat = _gmlp_fwd(proj, vec(sm['ln_a_g']), vec(sm['ln_a_b']), sm['w_s'], bst, vec(sm['norm_a_g']), d_mix, "gmlp_fwd")
    xbc, cpre = _conv_ssm_fwd(proj, xcol0, wf['conv_ssm_w'], vec(sm['conv_ssm_b']), "conv_ssm_fwd")
    cat, hs = _ssd_fwd(xbc, dtp, proj, zcol0, bias_p, alog_p, dskip_x, normg_x, cat, "ssd_fwd")
    wf.update(comm.weights('b', hs))
    h1 = _mm_nn(cat, wf['w_out'], out_dtype=F32, name="mm_out", res=x)
    f_n, f_t = _rms_fwd(h1, vec(sm['norm_ffn_g']), "rms_ffn")
    hid = _mm_nn(f_n, wf['w_up'], out_dtype=ACT_DTYPE, name="mm_up")
    act, act_t, cv = _conv_ffn_fwd(hid, wf['conv_ffn_w'], vec(sm['conv_ffn_b']), "conv_ffn_fwd")
    h2 = _mm_nn(act, wf['w_down'], out_dtype=F32, name="mm_down", res=h1)
    r_n, r_t = _rms_fwd(h2, vec(sm['norm_ple_g']), "rms_ple")
    q = _mm_nn(r_n, wf['w_ple_gate'], out_dtype=F32, name="mm_pg")
    p_m = p.astype(MXU_DTYPE)
    pe = _mm_nn(p_m, wf['w_ple'], out_dtype=F32, name="mm_ple")

    loss, dh3, dq, dpe, dgf = _head(h2, q, pe, tgt, vec(sm['norm_final_g']), "head")
    wgrad = lambda act_t, g, name, **kw: _mm_nn(act_t, g, out_dtype=WIRE_DTYPE, name=name, wide=True, **kw)
    gs = {}
    gs['norm_final_g'] = _row0(dgf)
    g_ple = wgrad(p_m.T, dpe, "wg_ple")
    g_pg = wgrad(r_t, dq, "wg_pg")
    dr = _mm_nt(dq, wf['w_ple_gate'], out_dtype=F32, name="dg_pg")
    dh2, dh2m, dg = _rms_bwd(h2, vec(sm['norm_ple_g']), dr, dh3, "rms_ple_bwd")
    gs['norm_ple_g'] = _row0(dg)
    g_down = wgrad(act_t, dh2m, "wg_down")
    tok = comm.send('1', {'w_ple': g_ple, 'w_ple_gate': g_pg, 'w_down': g_down})
    dact = _mm_nt(dh2m, wf['w_down'], out_dtype=ACT_DTYPE, name="dg_down")
    dhid, dcw, dcb = _conv_ffn_bwd(hid, wf['conv_ffn_w'] + tok, cv, dact, "conv_ffn_bwd")
    kf = wf['conv_ffn_w'].shape[0]
    g_cf = jnp.concatenate([dcw[0, :kf], dcw[1, :kf]], axis=1)
    gs['conv_ffn_b'] = jnp.concatenate([dcb[0, 0], dcb[1, 0]], axis=0)
    g_up = wgrad(f_t, dhid, "wg_up", b_split=2, out_slabs=N_DEV)
    df = _mm_nt(dhid, wf['w_up'], out_dtype=F32, name="dg_up", a_split=2)
    dh1, dh1m, dg = _rms_bwd(h1, vec(sm['norm_ffn_g']), df, dh2, "rms_ffn_bwd")
    gs['norm_ffn_g'] = _row0(dg)
    g_out = wgrad(cat.T, dh1m, "wg_out")
    tok = comm.send('2', {'conv_ffn_w': g_cf, 'w_up': g_up, 'w_out': g_out}, stacked=('w_up',))
    dcat = _mm_nt(dh1m, wf['w_out'], out_dtype=ACT_DTYPE, name="dg_out")
    dproj, dlng, dlnb, dws, dbst, dng = _gmlp_bwd(proj, vec(sm['ln_a_g']) + tok, vec(sm['ln_a_b']), sm['w_s'], bst,
                                                  vec(sm['norm_a_g']), dcat, d_main, "gmlp_bwd")
    gs['ln_a_g'], gs['ln_a_b'], gs['w_s'], gs['norm_a_g'] = _row0(dlng), _row0(dlnb), dws, _row0(dng)
    gs['b_s'] = dbst[:, :sm['b_s'].shape[0]].T
    dproj, dxs, dbm, dcm, ddtp, dbias, dalog, ddsk, dsng = _ssd_bwd(
        xbc, dtp, proj, zcol0, bias_p, alog_p, dskip_x, normg_x, hs, dcat, dproj, "ssd_bwd")
    gs['dt_bias'] = dbias[:, 0, :HEADS_PER_GROUP].reshape(h_n)
    gs['a_log'] = dalog[:, 0, :HEADS_PER_GROUP].reshape(h_n)
    gs['d_skip'] = ddsk[:, 0, :].reshape(h_n, HEAD_DIM).sum(axis=-1)
    gs['ssm_norm_g'] = dsng[:, 0, :].reshape(d_ssm)
    dws_c, dbs_c = [], []
    off = 0
    for nm, dpart in (("x", dxs), ("b", dbm), ("c", dcm)):
        dproj, dw_c, db_c = _conv_ssm_bwd(proj, xcol0 + off, wf['conv_ssm_w'], cpre, off, dpart, dproj,
                                          xcol0 + off, "conv_ssm_bwd_" + nm)
        dws_c.append(dw_c[:wf['conv_ssm_w'].shape[0]])
        dbs_c.append(db_c[0])
        off += dpart.shape[1]
    g_cs = jnp.concatenate(dws_c, axis=1)
    gs['conv_ssm_b'] = jnp.concatenate(dbs_c, axis=0)
    ddt = jnp.pad(ddtp.reshape(t, ngrp, 128)[:, :, :HEADS_PER_GROUP].reshape(t, h_n), ((0, 0), (0, 128 - h_n))).astype(MXU_DTYPE)
    g_in = jnp.concatenate([wgrad(a_t, dproj, "wg_in"), wgrad(a_t, ddt, "wg_dt")[:, :h_n]], axis=1)
    tok = comm.send('3', {'conv_ssm_w': g_cs, 'w_in': g_in}, [(n, gs[n]) for n in REPLICATED if n != 'norm_mix_g'])
    da = _mm_nt(ddt + tok.astype(ddt.dtype), w_dt, out_dtype=F32, name="dg_dt")
    da = _mm_nt(dproj, w_main, out_dtype=F32, name="dg_in", res=da)
    dx, _, dg = _rms_bwd(x, g_mix + tok, da, dh1, "rms_mix_bwd")
    comm.send('4', {}, [('norm_mix_g', _row0(dg)), ('loss', loss[0, 0:1])])
    return dx


def kernel(x, p, norm_mix_g, w_in, ln_a_g, ln_a_b, w_s, b_s, norm_a_g, conv_ssm_w, conv_ssm_b, dt_bias, a_log, d_skip, ssm_norm_g, w_out, norm_ffn_g, w_up, conv_ffn_w, conv_ffn_b, w_down, norm_ple_g, w_ple_gate, w_ple, norm_final_g, loss_target, m_norm_mix_g, m_w_in, m_ln_a_g, m_ln_a_b, m_w_s, m_b_s, m_norm_a_g, m_conv_ssm_w, m_conv_ssm_b, m_dt_bias, m_a_log, m_d_skip, m_ssm_norm_g, m_w_out, m_norm_ffn_g, m_w_up, m_conv_ffn_w, m_conv_ffn_b, m_w_down, m_norm_ple_g, m_w_ple_gate, m_w_ple, m_norm_final_g, v_norm_mix_g, v_w_in, v_ln_a_g, v_ln_a_b, v_w_s, v_b_s, v_norm_a_g, v_conv_ssm_w, v_conv_ssm_b, v_dt_bias, v_a_log, v_d_skip, v_ssm_norm_g, v_w_out, v_norm_ffn_g, v_w_up, v_conv_ffn_w, v_conv_ffn_b, v_w_down, v_norm_ple_g, v_w_ple_gate, v_w_ple, v_norm_final_g):
    given = dict(locals())
    wts = {n: given[n] for n in WEIGHTS}
    ms = {n: given["m_" + n] for n in WEIGHTS}
    vs = {n: given["v_" + n] for n in WEIGHTS}
    sm = {n: (wts[n][0] if wts[n].ndim > 1 else wts[n]) for n in REPLICATED}
    comm = _Comm({n: wts[n][0] for n in SHARDED})
    dx = _local_step(x[0], p[0, 0], loss_target[0], sm, comm)

    out, loss_out, after = {}, None, dx
    for tag, names, small_names, layout, handle in comm.sent:
        recv = _xfer_wait(handle, after, "grads_%s_wait" % tag)
        for n, parts in zip(names, recv):
            out[n] = _adamw(parts, wts[n][0], ms[n][0], vs[n][0], "adamw_" + n)
            after = out[n][1]
        if small_names:
            pick = lambda src, fill: _pack_small([src[n] if n in src else jnp.full((1,), fill, F32) for n in small_names])[0]
            res = _adamw(recv[-1], pick(wts, 0.0), pick(ms, 0.0), pick(vs, 1.0), "adamw_small_" + tag)
            res = [_unpack_small(o, layout) for o in res]
            after = res[1][0]
            for i, n in enumerate(small_names):
                if n == 'loss':
                    loss_out = res[0][i].reshape(())
                else:
                    out[n] = [res[k][i] for k in range(4)]
    return (loss_out, dx[None], *[out[n][k].reshape(wts[n].shape) for k in range(4) for n in WEIGHTS])


class _Comm:
    GATHER_GROUPS = {'a': ('w_in', 'conv_ssm_w'), 'b': ('w_out', 'w_up', 'conv_ffn_w', 'w_down', 'w_ple_gate', 'w_ple')}

    def __init__(self, blocks):
        wired = lambda grp: [blocks[n].astype(_wire(n)) for n in self.GATHER_GROUPS[grp]]
        self.stacks_a = _gather_two_level(wired('a'), "gather_a")
        self.handle_b, tok = _xfer_start([('gather', b) for b in wired('b')], "gather_b_start", after=self.stacks_a[0])
        self.tok0 = tok[0, 0]
        self.sent = []

    def weights(self, grp, after):
        stacks = self.stacks_a if grp == 'a' else _xfer_wait(self.handle_b, after, "gather_b_wait")
        return {n: _stack_to_full(SHARDED[n], st) for n, st in zip(self.GATHER_GROUPS[grp], stacks)}

    def send(self, tag, gw, small=None, stacked=()):
        items = [('scatter', g if n in stacked else _full_to_stack(SHARDED[n], g.astype(_wire(n)))) for n, g in gw.items()]
        layout, small_names = None, []
        if small:
            packed, layout = _pack_small([a for _, a in small])
            small_names = [n for n, _ in small]
            items.append(('gather', packed))
        handle, tok = _xfer_start(items, "grads_%s_start" % tag)
        self.sent.append((tag, list(gw), small_names, layout, handle))
        return tok[0, 0]


def _wire(name):
    return F32 if name in F32_ON_WIRE else WIRE_DTYPE
```

```python
import functools

import jax
import jax.numpy as jnp
from jax import lax
from jax.experimental import pallas as pl
from jax.experimental.pallas import tpu as pltpu

F32 = jnp.float32
MXU_DTYPE = jnp.bfloat16
ACT_DTYPE = jnp.bfloat16
XBC_DTYPE = jnp.bfloat16
WIRE_DTYPE = jnp.bfloat16
EPS = 1e-6
CHUNK = 128
D_STATE = 128
HEAD_DIM = 64
HEADS_PER_GROUP = 4
GROUP_CH = HEAD_DIM * HEADS_PER_GROUP
HALO = 16
N_DEV = 8
VMEM_LIMIT = 48 * 1024 * 1024

ADAM_LR = 0.001
ADAM_B1 = 0.9
ADAM_B2 = 0.999
ADAM_EPS = 1e-08
ADAM_WD = 0.01
ADAM_STEP = 10

WEIGHTS = ['norm_mix_g', 'w_in', 'ln_a_g', 'ln_a_b', 'w_s', 'b_s', 'norm_a_g', 'conv_ssm_w', 'conv_ssm_b', 'dt_bias',
           'a_log', 'd_skip', 'ssm_norm_g', 'w_out', 'norm_ffn_g', 'w_up', 'conv_ffn_w', 'conv_ffn_b', 'w_down',
           'norm_ple_g', 'w_ple_gate', 'w_ple', 'norm_final_g']
SHARDED = {'w_in': 'col', 'conv_ssm_w': 'col', 'w_out': 'row', 'w_up': 'col', 'conv_ffn_w': 'col', 'w_down': 'row',
           'w_ple_gate': 'row', 'w_ple': 'col'}
F32_ON_WIRE = ('conv_ssm_w', 'conv_ffn_w')
REPLICATED = [n for n in WEIGHTS if n not in SHARDED]

S = jax.ShapeDtypeStruct


def _pick(dim, cands):
    for c in cands:
        if c <= dim and dim % c == 0:
            return c
    return dim


def _cp(sem, vmem=VMEM_LIMIT):
    return pltpu.CompilerParams(dimension_semantics=sem, vmem_limit_bytes=vmem)


def _mx(v):
    return v.astype(MXU_DTYPE)


def _rms(v, g):
    return v * lax.rsqrt(jnp.mean(v * v, axis=-1, keepdims=True) + EPS) * g


MM_VMEM_BUDGET = 34 * 1024 * 1024


def _mm_tiles(m, n, k, out_bytes, has_res, tn_cands=(512, 256, 128), k_mult=1):
    tn = _pick(n, tn_cands)
    ks = k // k_mult
    best = None
    for tm in [c for c in (1024, 512) if m % c == 0] or [_pick(m, (256, 128))]:
        for nk in range(1, ks // 128 + 1):
            if ks % nk or (ks // nk) % 128:
                continue
            tk = ks // nk
            need = 2 * 2 * (tm * tk + tk * tn) + tm * tn * (4 + 2 * out_bytes + (8 if has_res else 0))
            if need <= MM_VMEM_BUDGET:
                if best is None or (nk, -tm) < best[0]:
                    best = ((nk, -tm), (tm, tn, tk))
                break
    return best[1] if best else (_pick(m, (512, 256, 128)), tn, _pick(ks, (128,)))


def _mm_body(dot, nk, has_res):
    def body(*refs):
        if has_res:
            a_ref, b_ref, r_ref, o_ref, acc_ref = refs
        else:
            a_ref, b_ref, o_ref, acc_ref = refs
            r_ref = None
        kk = pl.program_id(2)
        d = dot(a_ref[...], b_ref[...])

        def fin(acc):
            if r_ref is not None:
                acc = acc + r_ref[...]
            o_ref[...] = acc.astype(o_ref.dtype)

        if nk == 1:
            fin(d)
        else:
            @pl.when(kk == 0)
            def _():
                acc_ref[...] = d

            @pl.when(kk > 0)
            def _():
                acc_ref[...] += d

            @pl.when(kk == nk - 1)
            def _():
                fin(acc_ref[...])

    return body


def _mm_call(body, grid, a_spec, b_spec, tm, tn, m, n, out_dtype, name, args, res, out_slabs=1):
    in_specs = [a_spec, b_spec]
    if res is not None:
        in_specs.append(pl.BlockSpec((tm, tn), lambda i, j, kk: (i, j)))
        args = args + [res]
    if out_slabs == 1:
        out_spec, out_shape = pl.BlockSpec((tm, tn), lambda i, j, kk: (i, j)), S((m, n), out_dtype)
    else:
        out_spec, out_shape = pl.BlockSpec((None, tm, tn), lambda i, j, kk: (j, i, 0)), S((out_slabs, m, tn), out_dtype)
    return pl.pallas_call(
        body, grid=grid, in_specs=in_specs, out_specs=out_spec, out_shape=out_shape,
        scratch_shapes=[pltpu.VMEM((tm, tn), F32)], name=name,
        compiler_params=_cp(("parallel", "parallel", "arbitrary")))(*args)


def _mm_nn(a, b, *, out_dtype, name, res=None, b_split=1, wide=False, out_slabs=1):
    m, k = a.shape
    n = b.shape[1] if b_split == 1 else b.shape[2] * b_split
    tn_cands = (n // out_slabs,) if out_slabs > 1 else (1024, 512, 256, 128) if wide else (512, 256, 128)
    tm, tn, tk = _mm_tiles(m, n // b_split, k, jnp.dtype(out_dtype).itemsize, res is not None, tn_cands=tn_cands)
    assert out_slabs == 1 or (tn * out_slabs == n and tn % 128 == 0)
    nk = k // tk
    njs = (n // b_split) // tn
    body = _mm_body(lambda x, y: jnp.dot(x, y, preferred_element_type=F32), nk, res is not None)
    if b_split == 1:
        b_spec = pl.BlockSpec((tk, tn), lambda i, j, kk: (kk, j))
    else:
        b_spec = pl.BlockSpec((None, tk, tn), lambda i, j, kk: (j // njs, kk, j % njs))
    return _mm_call(body, (m // tm, n // tn, nk), pl.BlockSpec((tm, tk), lambda i, j, kk: (i, kk)), b_spec,
                    tm, tn, m, n, out_dtype, name, [a, b], res, out_slabs)


def _mm_nt(a, b, *, out_dtype, name, res=None, a_split=1):
    if a_split == 1:
        m, k = a.shape
    else:
        m, k = a.shape[1], a.shape[2] * a_split
    n = b.shape[0]
    tm, tn, tk = _mm_tiles(m, n, k, jnp.dtype(out_dtype).itemsize, res is not None, k_mult=a_split)
    nk = k // tk
    nks = nk // a_split
    body = _mm_body(lambda x, y: lax.dot_general(x, y, (((1,), (1,)), ((), ())), preferred_element_type=F32),
                    nk, res is not None)
    if a_split == 1:
        a_spec = pl.BlockSpec((tm, tk), lambda i, j, kk: (i, kk))
    else:
        a_spec = pl.BlockSpec((None, tm, tk), lambda i, j, kk: (kk // nks, i, kk % nks))
    return _mm_call(body, (m // tm, n // tn, nk), a_spec, pl.BlockSpec((tn, tk), lambda i, j, kk: (j, kk)),
                    tm, tn, m, n, out_dtype, name, [a, b], res)


def _rms_fwd(x, g, name):
    t, d = x.shape
    tr = _pick(t, (512, 256, 128))

    def body(x_ref, g_ref, o_ref, ot_ref):
        y = _rms(x_ref[...], g_ref[...]).astype(o_ref.dtype)
        o_ref[...] = y
        ot_ref[...] = y.T

    return pl.pallas_call(
        body, grid=(t // tr,),
        in_specs=[pl.BlockSpec((tr, d), lambda i: (i, 0)), pl.BlockSpec((1, d), lambda i: (0, 0))],
        out_specs=[pl.BlockSpec((tr, d), lambda i: (i, 0)), pl.BlockSpec((d, tr), lambda i: (0, i))],
        out_shape=[S((t, d), ACT_DTYPE), S((d, t), ACT_DTYPE)], name=name,
        compiler_params=_cp(("parallel",)))(x, g)


def _rms_bwd(xin, g, dn, dres, name, mx_copy=True):
    t, d = xin.shape
    tr = _pick(t, (256, 128))

    def body(x_ref, g_ref, dn_ref, dr_ref, dx_ref, *rest):
        dg_ref = rest[-1]

        @pl.when(pl.program_id(0) == 0)
        def _():
            dg_ref[...] = jnp.zeros_like(dg_ref)

        _, vjp = jax.vjp(_rms, x_ref[...], g_ref[...])
        dx, dg = vjp(dn_ref[...].astype(F32))
        dx = dr_ref[...] + dx
        dx_ref[...] = dx
        if mx_copy:
            rest[0][...] = dx.astype(rest[0].dtype)
        dg_ref[0:1, :] += dg

    row = pl.BlockSpec((tr, d), lambda i: (i, 0))
    acc = pl.BlockSpec((8, d), lambda i: (0, 0))
    return pl.pallas_call(
        body, grid=(t // tr,),
        in_specs=[row, pl.BlockSpec((1, d), lambda i: (0, 0)), row, row],
        out_specs=[row, row, acc] if mx_copy else [row, acc],
        out_shape=[S((t, d), F32)] + ([S((t, d), MXU_DTYPE)] if mx_copy else []) + [S((8, d), F32)], name=name,
        compiler_params=_cp(("arbitrary",)))(xin, g, dn, dres)


def _head(h2, q, pe, tgt, gf, name):
    t, d = h2.shape
    tr = _pick(t, (256, 128))

    def f(h2v, qv, pev, gfv, tv):
        h3 = h2v + jax.nn.sigmoid(qv) * pev
        y = _rms(h3, gfv)
        return 0.5 * jnp.sum(jnp.mean(jnp.square(y - tv), axis=-1))

    def body(h2_ref, q_ref, pe_ref, t_ref, g_ref, loss_ref, dh_ref, dq_ref, dpe_ref, dg_ref):
        @pl.when(pl.program_id(0) == 0)
        def _():
            loss_ref[...] = jnp.zeros_like(loss_ref)
            dg_ref[...] = jnp.zeros_like(dg_ref)

        tv = t_ref[...]
        loss, vjp = jax.vjp(lambda a, b, c, e: f(a, b, c, e, tv), h2_ref[...], q_ref[...].astype(F32),
                            pe_ref[...].astype(F32), g_ref[...])
        dh, dq, dpe, dg = vjp(jnp.ones((), F32))
        loss_ref[...] += jnp.full(loss_ref.shape, loss, F32)
        dh_ref[...] = dh
        dq_ref[...] = dq.astype(dq_ref.dtype)
        dpe_ref[...] = dpe.astype(dpe_ref.dtype)
        dg_ref[0:1, :] += dg

    row = pl.BlockSpec((tr, d), lambda i: (i, 0))
    return pl.pallas_call(
        body, grid=(t // tr,),
        in_specs=[row, row, row, row, pl.BlockSpec((1, d), lambda i: (0, 0))],
        out_specs=[pl.BlockSpec((8, 128), lambda i: (0, 0)), row, row, row, pl.BlockSpec((8, d), lambda i: (0, 0))],
        out_shape=[S((8, 128), F32), S((t, d), F32), S((t, d), MXU_DTYPE), S((t, d), MXU_DTYPE), S((8, d), F32)],
        name=name, compiler_params=_cp(("arbitrary",)))(h2, q, pe, tgt, gf)


def _gmlp_block(us, vs, lng, lnb, wss, bss, ng):
    n = us[0].shape[0]
    row = lax.broadcasted_iota(jnp.int32, (n, n), 0)
    col = lax.broadcasted_iota(jnp.int32, (n, n), 1)
    outs = []
    for u0, v0, lg, lb, ws, bs in zip(us, vs, lng, lnb, wss, bss):
        u = jax.nn.gelu(u0)
        v = jax.nn.gelu(v0)
        mu = jnp.mean(v, axis=-1, keepdims=True)
        var = jnp.mean(jnp.square(v - mu), axis=-1, keepdims=True)
        vn = (v - mu) * lax.rsqrt(var + EPS) * lg + lb
        w = jnp.where(row >= col, ws, 0.0)
        sg = jnp.dot(_mx(w), _mx(vn), preferred_element_type=F32) + bs
        outs.append(u * sg)
    return _rms(jnp.concatenate(outs, axis=1), ng)


def _gmlp_load(proj_ref, lng_ref, lnb_ref, ws_ref, bst_ref, d_a, ng):
    sl = lambda g: slice(CHUNK * g, CHUNK * (g + 1))
    us = [proj_ref[:, sl(g)].astype(F32) for g in range(ng)]
    vs = [proj_ref[:, d_a + CHUNK * g: d_a + CHUNK * (g + 1)].astype(F32) for g in range(ng)]
    lng = [lng_ref[:, sl(g)] for g in range(ng)]
    lnb = [lnb_ref[:, sl(g)] for g in range(ng)]
    wss = [ws_ref[g] for g in range(ng)]
    bss = [bst_ref[:, g:g + 1] for g in range(ng)]
    return us, vs, lng, lnb, wss, bss


def _gmlp_fwd(proj, ln_g, ln_b, w_s, bst, norm_g, d_mix, name):
    t = proj.shape[0]
    ng = w_s.shape[0]
    d_a = ng * CHUNK

    def body(proj_ref, lng_ref, lnb_ref, ws_ref, bst_ref, ng_ref, o_ref):
        args = _gmlp_load(proj_ref, lng_ref, lnb_ref, ws_ref, bst_ref, d_a, ng)
        o_ref[...] = _gmlp_block(*args, ng_ref[...]).astype(o_ref.dtype)

    vec = pl.BlockSpec((1, d_a), lambda c: (0, 0))
    return pl.pallas_call(
        body, grid=(t // CHUNK,),
        in_specs=[pl.BlockSpec((CHUNK, 2 * d_a), lambda c: (c, 0)), vec, vec,
                  pl.BlockSpec((ng, CHUNK, CHUNK), lambda c: (0, 0, 0)), pl.BlockSpec((CHUNK, 128), lambda c: (0, 0)), vec],
        out_specs=pl.BlockSpec((CHUNK, d_a), lambda c: (c, 0)), out_shape=S((t, d_mix), ACT_DTYPE),
        name=name, compiler_params=_cp(("parallel",)))(proj, ln_g, ln_b, w_s, bst, norm_g)


def _gmlp_bwd(proj, ln_g, ln_b, w_s, bst, norm_g, dcat, d_proj, name):
    t = proj.shape[0]
    ng = w_s.shape[0]
    d_a = ng * CHUNK

    def body(proj_ref, lng_ref, lnb_ref, ws_ref, bst_ref, ng_ref, dy_ref,
             dp_ref, dlng_ref, dlnb_ref, dws_ref, dbst_ref, dng_ref):
        @pl.when(pl.program_id(0) == 0)
        def _():
            for r in (dlng_ref, dlnb_ref, dws_ref, dbst_ref, dng_ref):
                r[...] = jnp.zeros_like(r)

        args = _gmlp_load(proj_ref, lng_ref, lnb_ref, ws_ref, bst_ref, d_a, ng)
        _, vjp = jax.vjp(_gmlp_block, *args, ng_ref[...])
        dus, dvs, dlng, dlnb, dwss, dbss, dng = vjp(dy_ref[...].astype(F32))
        lane = lax.broadcasted_iota(jnp.int32, (1, 128), 1)
        dbst = jnp.zeros((CHUNK, 128), F32)
        for g in range(ng):
            dp_ref[:, CHUNK * g:CHUNK * (g + 1)] = dus[g].astype(dp_ref.dtype)
            dp_ref[:, d_a + CHUNK * g:d_a + CHUNK * (g + 1)] = dvs[g].astype(dp_ref.dtype)
            dlng_ref[0:1, CHUNK * g:CHUNK * (g + 1)] += dlng[g]
            dlnb_ref[0:1, CHUNK * g:CHUNK * (g + 1)] += dlnb[g]
            dws_ref[g] += dwss[g]
            dbst = dbst + dbss[g] * (lane == g).astype(F32)
        dbst_ref[...] += dbst
        dng_ref[0:1, :] += dng

    vec = pl.BlockSpec((1, d_a), lambda c: (0, 0))
    acc = pl.BlockSpec((8, d_a), lambda c: (0, 0))
    wspec = pl.BlockSpec((ng, CHUNK, CHUNK), lambda c: (0, 0, 0))
    bspec = pl.BlockSpec((CHUNK, 128), lambda c: (0, 0))
    return pl.pallas_call(
        body, grid=(t // CHUNK,),
        in_specs=[pl.BlockSpec((CHUNK, 2 * d_a), lambda c: (c, 0)), vec, vec, wspec, bspec, vec,
                  pl.BlockSpec((CHUNK, d_a), lambda c: (c, 0))],
        out_specs=[pl.BlockSpec((CHUNK, 2 * d_a), lambda c: (c, 0)), acc, acc, wspec, bspec, acc],
        out_shape=[S((t, d_proj), ACT_DTYPE), S((8, d_a), F32), S((8, d_a), F32), S((ng, CHUNK, CHUNK), F32),
                   S((CHUNK, 128), F32), S((8, d_a), F32)],
        name=name, compiler_params=_cp(("arbitrary",)))(proj, ln_g, ln_b, w_s, bst, norm_g, dcat)


def _silu_grad(c):
    s = jax.nn.sigmoid(c)
    return s * (1.0 + c * (1.0 - s))


def _fill_prev_main(s_ref, prev_ref, main_ref, i, tt):
    s_ref[pl.ds(0, HALO), :] = jnp.where(i > 0, prev_ref[...].astype(F32), 0.0)
    s_ref[pl.ds(HALO, tt), :] = main_ref[...].astype(F32)


def _prev_spec(tt, tc, joff):
    return pl.BlockSpec((HALO, tc), lambda j, i: (jnp.maximum(i * (tt // HALO) - 1, 0), j + joff))


def _next_spec(tt, tc, joff, t):
    return pl.BlockSpec((HALO, tc), lambda j, i: (jnp.minimum((i + 1) * (tt // HALO), t // HALO - 1), j + joff))


CONV_RC = 32


def _conv_tiles(t, c):
    return _pick(t, (1024, 512, 256, 128)), _pick(c, (256, 128))


def _row_chunks(tt, fn, init=0):
    rc = min(CONV_RC, tt)
    return lax.fori_loop(0, tt // rc, lambda q, c: fn(pl.multiple_of(q * rc, rc), rc, c), init)


def _fold8(p):
    acc = p[0:8]
    for r in range(8, p.shape[0], 8):
        acc = acc + p[r:r + 8]
    return acc


def _taps_chunk(s_ref, w_ref, kw, r0, rc):
    xe = s_ref[pl.ds(HALO - 8 + r0, rc + 8), :]
    acc = w_ref[0:1, :] * xe[8 - (kw - 1):8 - (kw - 1) + rc]
    for k in range(1, kw):
        acc = acc + w_ref[k:k + 1, :] * xe[8 - (kw - 1) + k:8 - (kw - 1) + k + rc]
    return acc


def _conv_bwd_chunk(sd_ref, x, w_ref, kw, r0, rc, dws):
    de = sd_ref[pl.ds(r0, rc + 8), :]
    dx, out = None, list(dws)
    for j in range(kw):
        d = de[j:j + rc]
        k = kw - 1 - j
        term = w_ref[k:k + 1, :] * d
        dx = term if dx is None else dx + term
        out[k] = out[k] + _fold8(x * d)
    return dx, out


def _conv_ssm_fwd(proj, col0, w, b, name):
    t = proj.shape[0]
    kw, c = w.shape
    tt, tc = _conv_tiles(t, c)
    joff = col0 // tc
    assert col0 % tc == 0

    def body(x_ref, xp_ref, w_ref, b_ref, o_ref, c_ref, s_ref):
        _fill_prev_main(s_ref, xp_ref, x_ref, pl.program_id(1), tt)

        def chunk(r0, rc, carry):
            cpre = _taps_chunk(s_ref, w_ref, kw, r0, rc) + b_ref[...]
            o_ref[pl.ds(r0, rc), :] = jax.nn.silu(cpre).astype(o_ref.dtype)
            c_ref[pl.ds(r0, rc), :] = cpre.astype(c_ref.dtype)
            return carry

        _row_chunks(tt, chunk)

    out = pl.BlockSpec((tt, tc), lambda j, i: (i, j))
    return pl.pallas_call(
        body, grid=(c // tc, t // tt),
        in_specs=[pl.BlockSpec((tt, tc), lambda j, i: (i, j + joff)), _prev_spec(tt, tc, joff),
                  pl.BlockSpec((kw, tc), lambda j, i: (0, j)), pl.BlockSpec((1, tc), lambda j, i: (0, j))],
        out_specs=[out, out], out_shape=[S((t, c), XBC_DTYPE), S((t, c), ACT_DTYPE)],
        scratch_shapes=[pltpu.VMEM((HALO + tt, tc), F32)], name=name,
        compiler_params=_cp(("parallel", "arbitrary")))(proj, proj, w, b)


def _conv_ssm_bwd(proj, col0, w, cpre, wcol0, dact, dproj, out_col0, name):
    t = proj.shape[0]
    kw = w.shape[0]
    c = dact.shape[1]
    tt, tc = _conv_tiles(t, c)
    assert col0 % tc == 0 and wcol0 % tc == 0 and out_col0 % tc == 0
    joff, wj, oj = col0 // tc, wcol0 // tc, out_col0 // tc
    nt = t // tt

    def body(x_ref, w_ref, c_ref, cn_ref, d_ref, dn_ref, dp_in, dx_ref, dw_ref, db_ref, sd_ref):
        del dp_in
        i = pl.program_id(1)

        @pl.when(i == 0)
        def _():
            dw_ref[...] = jnp.zeros_like(dw_ref)
            db_ref[...] = jnp.zeros_like(db_ref)

        def stage(r0, rc, db):
            rows = pl.ds(r0, rc)
            d = d_ref[rows, :].astype(F32) * _silu_grad(c_ref[rows, :].astype(F32))
            sd_ref[rows, :] = d
            return db + _fold8(d)

        zero8 = jnp.zeros((8, tc), F32)
        db = _row_chunks(tt, stage, zero8)
        sd_ref[pl.ds(tt, HALO), :] = jnp.where(
            i < nt - 1, dn_ref[...].astype(F32) * _silu_grad(cn_ref[...].astype(F32)), 0.0)

        def chunk(r0, rc, dws):
            dx, dws = _conv_bwd_chunk(sd_ref, x_ref[pl.ds(r0, rc), :].astype(F32), w_ref, kw, r0, rc, dws)
            dx_ref[pl.ds(r0, rc), :] = dx.astype(dx_ref.dtype)
            return dws

        dws = _row_chunks(tt, chunk, [zero8] * kw)
        for k in range(kw):
            dw_ref[k:k + 1, :] += jnp.sum(dws[k], axis=0, keepdims=True)
        db_ref[0:1, :] += jnp.sum(db, axis=0, keepdims=True)

    acc = pl.BlockSpec((8, tc), lambda j, i: (0, j))
    return pl.pallas_call(
        body, grid=(c // tc, nt),
        in_specs=[pl.BlockSpec((tt, tc), lambda j, i: (i, j + joff)), pl.BlockSpec((kw, tc), lambda j, i: (0, j + wj)),
                  pl.BlockSpec((tt, tc), lambda j, i: (i, j + wj)), _next_spec(tt, tc, wj, t),
                  pl.BlockSpec((tt, tc), lambda j, i: (i, j)), _next_spec(tt, tc, 0, t),
                  pl.BlockSpec(memory_space=pl.ANY)],
        out_specs=[pl.BlockSpec((tt, tc), lambda j, i: (i, j + oj)), acc, acc],
        out_shape=[S(dproj.shape, dproj.dtype), S((8, c), F32), S((8, c), F32)],
        scratch_shapes=[pltpu.VMEM((tt + HALO, tc), F32)],
        input_output_aliases={6: 0}, name=name,
        compiler_params=_cp(("parallel", "arbitrary")))(proj, w, cpre, cpre, dact, dact, dproj)


def _conv_ffn_fwd(hid, w, b, name):
    t, f2 = hid.shape
    f = f2 // 2
    kw = w.shape[0]
    tt, tc = _conv_tiles(t, f)
    nj = f // tc

    def body(g_ref, gp_ref, u_ref, up_ref, wg_ref, wu_ref, bg_ref, bu_ref, o_ref, ot_ref, cv_ref, sg_ref, su_ref):
        i = pl.program_id(1)
        _fill_prev_main(sg_ref, gp_ref, g_ref, i, tt)
        _fill_prev_main(su_ref, up_ref, u_ref, i, tt)

        def chunk(r0, rc, carry):
            rows = pl.ds(r0, rc)
            gate = _taps_chunk(sg_ref, wg_ref, kw, r0, rc) + bg_ref[...]
            up = _taps_chunk(su_ref, wu_ref, kw, r0, rc) + bu_ref[...]
            o_ref[rows, :] = (jax.nn.silu(gate) * up).astype(o_ref.dtype)
            cv_ref[0, rows, :] = gate.astype(cv_ref.dtype)
            cv_ref[1, rows, :] = up.astype(cv_ref.dtype)
            return carry

        _row_chunks(tt, chunk)
        ot_ref[...] = o_ref[...].T

    return pl.pallas_call(
        body, grid=(nj, t // tt),
        in_specs=[pl.BlockSpec((tt, tc), lambda j, i: (i, j)), _prev_spec(tt, tc, 0),
                  pl.BlockSpec((tt, tc), lambda j, i: (i, j + nj)), _prev_spec(tt, tc, nj),
                  pl.BlockSpec((kw, tc), lambda j, i: (0, j)), pl.BlockSpec((kw, tc), lambda j, i: (0, j + nj)),
                  pl.BlockSpec((1, tc), lambda j, i: (0, j)), pl.BlockSpec((1, tc), lambda j, i: (0, j + nj))],
        out_specs=[pl.BlockSpec((tt, tc), lambda j, i: (i, j)), pl.BlockSpec((tc, tt), lambda j, i: (j, i)),
                   pl.BlockSpec((2, tt, tc), lambda j, i: (0, i, j))],
        out_shape=[S((t, f), ACT_DTYPE), S((f, t), ACT_DTYPE), S((2, t, f), ACT_DTYPE)],
        scratch_shapes=[pltpu.VMEM((HALO + tt, tc), F32), pltpu.VMEM((HALO + tt, tc), F32)], name=name,
        compiler_params=_cp(("parallel", "arbitrary")))(hid, hid, hid, hid, w, w, b, b)


def _conv_ffn_bwd(hid, w, cv, dact, name):
    t, f2 = hid.shape
    f = f2 // 2
    kw = w.shape[0]
    tt, tc = _conv_tiles(t, f)
    nj = f // tc
    nt = t // tt

    def body(g_ref, u_ref, wg_ref, wu_ref, cv_ref, cvn_ref, d_ref, dn_ref, dh_ref, dw_ref, db_ref, dg_ref, du_ref):
        i = pl.program_id(1)

        @pl.when(i == 0)
        def _():
            dw_ref[...] = jnp.zeros_like(dw_ref)
            db_ref[...] = jnp.zeros_like(db_ref)

        def cotangents(gate, up, dact_v):
            sg = jax.nn.sigmoid(gate)
            return dact_v * up * (sg * (1.0 + gate * (1.0 - sg))), dact_v * (gate * sg)

        def stage(r0, rc, dbs):
            rows = pl.ds(r0, rc)
            dg, du = cotangents(cv_ref[0, rows, :].astype(F32), cv_ref[1, rows, :].astype(F32), d_ref[rows, :].astype(F32))
            dg_ref[rows, :] = dg
            du_ref[rows, :] = du
            return [dbs[0] + _fold8(dg), dbs[1] + _fold8(du)]

        zero8 = jnp.zeros((8, tc), F32)
        dbs = _row_chunks(tt, stage, [zero8, zero8])
        dgn, dun = cotangents(cvn_ref[0].astype(F32), cvn_ref[1].astype(F32), dn_ref[...].astype(F32))
        dg_ref[pl.ds(tt, HALO), :] = jnp.where(i < nt - 1, dgn, 0.0)
        du_ref[pl.ds(tt, HALO), :] = jnp.where(i < nt - 1, dun, 0.0)
        for s, (sd_ref, x_ref, w_ref) in enumerate(((dg_ref, g_ref, wg_ref), (du_ref, u_ref, wu_ref))):
            def chunk(r0, rc, dws, s=s, sd_ref=sd_ref, x_ref=x_ref, w_ref=w_ref):
                dx, dws = _conv_bwd_chunk(sd_ref, x_ref[pl.ds(r0, rc), :].astype(F32), w_ref, kw, r0, rc, dws)
                dh_ref[s, pl.ds(r0, rc), :] = dx.astype(dh_ref.dtype)
                return dws

            dws = _row_chunks(tt, chunk, [zero8] * kw)
            for k in range(kw):
                dw_ref[s, k:k + 1, :] += jnp.sum(dws[k], axis=0, keepdims=True)
            db_ref[s, 0:1, :] += jnp.sum(dbs[s], axis=0, keepdims=True)

    acc = pl.BlockSpec((2, 8, tc), lambda j, i: (0, 0, j))
    dsc = pltpu.VMEM((tt + HALO, tc), F32)
    nxt = lambda j, i: (0, jnp.minimum((i + 1) * (tt // HALO), t // HALO - 1), j)
    return pl.pallas_call(
        body, grid=(nj, nt),
        in_specs=[pl.BlockSpec((tt, tc), lambda j, i: (i, j)), pl.BlockSpec((tt, tc), lambda j, i: (i, j + nj)),
                  pl.BlockSpec((kw, tc), lambda j, i: (0, j)), pl.BlockSpec((kw, tc), lambda j, i: (0, j + nj)),
                  pl.BlockSpec((2, tt, tc), lambda j, i: (0, i, j)), pl.BlockSpec((2, HALO, tc), nxt),
                  pl.BlockSpec((tt, tc), lambda j, i: (i, j)), _next_spec(tt, tc, 0, t)],
        out_specs=[pl.BlockSpec((2, tt, tc), lambda j, i: (0, i, j)), acc, acc],
        out_shape=[S((2, t, f), MXU_DTYPE), S((2, 8, f), F32), S((2, 8, f), F32)],
        scratch_shapes=[dsc, dsc], name=name,
        compiler_params=_cp(("parallel", "arbitrary")))(hid, hid, w, w, cv, cv, dact, dact)


SSD_SUB = 2


def _ssd_chunk(xs, bm, cm, dtraw, z, hin, bias, alog, dskip, normg):
    n = bm.shape[0]
    row = lax.broadcasted_iota(jnp.int32, (n, n), 0)
    col = lax.broadcasted_iota(jnp.int32, (n, n), 1)
    causal = row >= col
    tri = causal.astype(F32)
    lane = lax.broadcasted_iota(jnp.int32, (1, 128), 1)
    sub = lax.broadcasted_iota(jnp.int32, (128, 1), 0)
    last = (lax.broadcasted_iota(jnp.int32, (n, 1), 0) == n - 1).astype(F32)
    dt = jax.nn.softplus(dtraw + bias)
    adt = dt * (-jnp.exp(alog))
    acs = jnp.dot(tri, adt, preferred_element_type=F32, precision=lax.Precision.HIGHEST)
    ch = lax.broadcasted_iota(jnp.int32, (128, GROUP_CH), 1)
    hd = lax.broadcasted_iota(jnp.int32, (128, GROUP_CH), 0) * HEAD_DIM
    expand = ((ch >= hd) & (ch < hd + HEAD_DIM)).astype(F32)
    acs_x = jnp.dot(acs, expand, preferred_element_type=F32, precision=lax.Precision.HIGHEST)
    alast_x = jnp.sum(acs_x * last, axis=0, keepdims=True)
    acs_t = acs.T
    scores = lax.dot_general(_mx(cm), _mx(bm), (((1,), (1,)), ((), ())), preferred_element_type=F32)
    yds, xts = [], []
    for r in range(HEADS_PER_GROUP):
        pick = (lane == r).astype(F32)
        acol = jnp.sum(acs * pick, axis=1, keepdims=True)
        arow = jnp.sum(acs_t * (sub == r).astype(F32), axis=0, keepdims=True)
        dtc = jnp.sum(dt * pick, axis=1, keepdims=True)
        lm = jnp.exp(jnp.where(causal, acol - arow, -1e30))
        xts.append(xs[r] * dtc)
        yds.append(jnp.dot(_mx(scores * lm), _mx(xts[r]), preferred_element_type=F32))
    xt = jnp.concatenate(xts, axis=1)
    yo = jnp.exp(acs_x) * jnp.dot(_mx(cm), _mx(hin), preferred_element_type=F32)
    st = lax.dot_general(_mx(bm), _mx(xt * jnp.exp(alast_x - acs_x)), (((0,), (0,)), ((), ())), preferred_element_type=F32)
    hout = jnp.exp(alast_x) * hin + st
    y = (jnp.concatenate(yds, axis=1) + yo + dskip * jnp.concatenate(xs, axis=1)) * jax.nn.silu(z)
    return _rms(y, normg), hout


def _ssd_block(datas, hin, consts):
    ys = []
    for data in datas:
        y, hin = _ssd_chunk(*data, hin, *consts)
        ys.append(y)
    return ys, hin


def _ssd_specs(d_ssm, ngrp, zcol0, rev, nb):
    cc = (lambda c: nb - 1 - c) if rev else (lambda c: c)
    rows = SSD_SUB * CHUNK
    xj, bj, cj, zj = 0, d_ssm // 128, d_ssm // 128 + ngrp, zcol0 // GROUP_CH
    const = lambda w: pl.BlockSpec((None, 8, w), lambda g, c: (g, 0, 0))
    return cc, [
        pl.BlockSpec((rows, GROUP_CH), lambda g, c: (cc(c), xj + g)),
        pl.BlockSpec((rows, 128), lambda g, c: (cc(c), bj + g)),
        pl.BlockSpec((rows, 128), lambda g, c: (cc(c), cj + g)),
        pl.BlockSpec((rows, 128), lambda g, c: (cc(c), g)),
        pl.BlockSpec((rows, GROUP_CH), lambda g, c: (cc(c), zj + g)),
        const(128), const(128), const(GROUP_CH), const(GROUP_CH)]


def _sub_rows(s):
    return slice(CHUNK * s, CHUNK * (s + 1))


def _ssd_load(x_ref, b_ref, c_ref, dt_ref, z_ref, bias_ref, alog_ref, dsk_ref, ng_ref):
    datas = []
    for s in range(SSD_SUB):
        rows = _sub_rows(s)
        xs = [x_ref[rows, HEAD_DIM * r:HEAD_DIM * (r + 1)].astype(F32) for r in range(HEADS_PER_GROUP)]
        datas.append((xs, b_ref[rows, :].astype(F32), c_ref[rows, :].astype(F32), dt_ref[rows, :],
                      z_ref[rows, :].astype(F32)))
    return datas, (bias_ref[0:1, :], alog_ref[0:1, :], dsk_ref[0:1, :], ng_ref[0:1, :])


def _ssd_fwd(xbc, dtp, proj, zcol0, bias_p, alog_p, dskip_x, normg_x, cat, name):
    t = xbc.shape[0]
    ngrp = bias_p.shape[0]
    d_ssm = ngrp * GROUP_CH
    rows = SSD_SUB * CHUNK
    nb = t // rows
    d_a = cat.shape[1] - d_ssm
    assert d_a % GROUP_CH == 0 and zcol0 % GROUP_CH == 0 and t % rows == 0
    _, specs = _ssd_specs(d_ssm, ngrp, zcol0, False, nb)

    def body(x_ref, b_ref, c_ref, dt_ref, z_ref, bias_ref, alog_ref, dsk_ref, ng_ref, cat_in, y_ref, hs_ref, h_ref):
        del cat_in

        @pl.when(pl.program_id(1) == 0)
        def _():
            h_ref[...] = jnp.zeros_like(h_ref)

        datas, consts = _ssd_load(x_ref, b_ref, c_ref, dt_ref, z_ref, bias_ref, alog_ref, dsk_ref, ng_ref)
        hs_ref[...] = h_ref[...]
        ys, hout = _ssd_block(datas, h_ref[...], consts)
        for s in range(SSD_SUB):
            y_ref[_sub_rows(s), :] = ys[s].astype(y_ref.dtype)
        h_ref[...] = hout

    return pl.pallas_call(
        body, grid=(ngrp, nb), in_specs=specs + [pl.BlockSpec(memory_space=pl.ANY)],
        out_specs=[pl.BlockSpec((rows, GROUP_CH), lambda g, c: (c, d_a // GROUP_CH + g)),
                   pl.BlockSpec((None, None, D_STATE, GROUP_CH), lambda g, c: (c, g, 0, 0))],
        out_shape=[S(cat.shape, cat.dtype), S((nb, ngrp, D_STATE, GROUP_CH), F32)],
        scratch_shapes=[pltpu.VMEM((D_STATE, GROUP_CH), F32)],
        input_output_aliases={9: 0}, name=name,
        compiler_params=_cp(("parallel", "arbitrary")))(xbc, xbc, xbc, dtp, proj, bias_p, alog_p, dskip_x, normg_x, cat)


def _ssd_bwd(xbc, dtp, proj, zcol0, bias_p, alog_p, dskip_x, normg_x, hs, dcat, dproj, name):
    t = xbc.shape[0]
    ngrp = bias_p.shape[0]
    d_ssm = ngrp * GROUP_CH
    rows = SSD_SUB * CHUNK
    nb = t // rows
    d_a = dcat.shape[1] - d_ssm
    cc, specs = _ssd_specs(d_ssm, ngrp, zcol0, True, nb)

    def body(x_ref, b_ref, c_ref, dt_ref, z_ref, bias_ref, alog_ref, dsk_ref, ng_ref, hs_ref, dy_ref, dp_in,
             dz_ref, dx_ref, db_ref, dc_ref, ddt_ref, dbias_ref, dalog_ref, ddsk_ref, dng_ref, dh_ref):
        del dp_in

        @pl.when(pl.program_id(1) == 0)
        def _():
            dh_ref[...] = jnp.zeros_like(dh_ref)
            for r in (dbias_ref, dalog_ref, ddsk_ref, dng_ref):
                r[...] = jnp.zeros_like(r)

        datas, consts = _ssd_load(x_ref, b_ref, c_ref, dt_ref, z_ref, bias_ref, alog_ref, dsk_ref, ng_ref)
        _, vjp = jax.vjp(_ssd_block, datas, hs_ref[...], consts)
        dys = [dy_ref[_sub_rows(s), :].astype(F32) for s in range(SSD_SUB)]
        ddatas, dhin, (dbias, dalog, ddsk, dng) = vjp((dys, dh_ref[...]))
        for s, (dxs, dbm, dcm, ddt, dz) in enumerate(ddatas):
            rws = _sub_rows(s)
            for r in range(HEADS_PER_GROUP):
                dx_ref[rws, HEAD_DIM * r:HEAD_DIM * (r + 1)] = dxs[r].astype(dx_ref.dtype)
            db_ref[rws, :] = dbm.astype(db_ref.dtype)
            dc_ref[rws, :] = dcm.astype(dc_ref.dtype)
            ddt_ref[rws, :] = ddt
            dz_ref[rws, :] = dz.astype(dz_ref.dtype)
        dh_ref[...] = dhin
        dbias_ref[0:1, :] += dbias
        dalog_ref[0:1, :] += dalog
        ddsk_ref[0:1, :] += ddsk
        dng_ref[0:1, :] += dng

    acc = lambda w: pl.BlockSpec((None, 8, w), lambda g, c: (g, 0, 0))
    blk = lambda w: pl.BlockSpec((rows, w), lambda g, c: (cc(c), g))
    return pl.pallas_call(
        body, grid=(ngrp, nb),
        in_specs=specs + [pl.BlockSpec((None, None, D_STATE, GROUP_CH), lambda g, c: (cc(c), g, 0, 0)),
                          pl.BlockSpec((rows, GROUP_CH), lambda g, c: (cc(c), d_a // GROUP_CH + g)),
                          pl.BlockSpec(memory_space=pl.ANY)],
        out_specs=[pl.BlockSpec((rows, GROUP_CH), lambda g, c: (cc(c), zcol0 // GROUP_CH + g)),
                   blk(GROUP_CH), blk(128), blk(128), blk(128), acc(128), acc(128), acc(GROUP_CH), acc(GROUP_CH)],
        out_shape=[S(dproj.shape, dproj.dtype), S((t, d_ssm), XBC_DTYPE), S((t, ngrp * 128), XBC_DTYPE),
                   S((t, ngrp * 128), XBC_DTYPE), S((t, ngrp * 128), F32), S((ngrp, 8, 128), F32),
                   S((ngrp, 8, 128), F32), S((ngrp, 8, GROUP_CH), F32), S((ngrp, 8, GROUP_CH), F32)],
        scratch_shapes=[pltpu.VMEM((D_STATE, GROUP_CH), F32)],
        input_output_aliases={11: 0}, name=name,
        compiler_params=_cp(("parallel", "arbitrary")))(xbc, xbc, xbc, dtp, proj, bias_p, alog_p, dskip_x, normg_x, hs, dcat, dproj)


def _adamw(parts, w, m, v, name):
    r, c = w.shape
    tr = _pick(r, (256, 128, 64, 32, 16, 8)) if c * 4 * 256 <= 4 * 1024 * 1024 else _pick(r, (64, 32, 16, 8))

    def body(p_ref, w_ref, m_ref, v_ref, g_ref, d_ref, nm_ref, nv_ref):
        g = p_ref[0].astype(F32)
        for k in range(1, N_DEV):
            g = g + p_ref[k].astype(F32)
        mm = ADAM_B1 * m_ref[...] + (1.0 - ADAM_B1) * g
        vv = ADAM_B2 * v_ref[...] + (1.0 - ADAM_B2) * jnp.square(g)
        m_hat = mm / (1.0 - ADAM_B1 ** ADAM_STEP)
        v_hat = vv / (1.0 - ADAM_B2 ** ADAM_STEP)
        g_ref[...] = g
        d_ref[...] = -ADAM_LR * (m_hat / (jnp.sqrt(v_hat) + ADAM_EPS) + ADAM_WD * w_ref[...])
        nm_ref[...] = mm
        nv_ref[...] = vv

    blk = pl.BlockSpec((tr, c), lambda i: (i, 0))
    return pl.pallas_call(
        body, grid=(r // tr,),
        in_specs=[pl.BlockSpec((N_DEV, tr, c), lambda i: (0, i, 0)), blk, blk, blk],
        out_specs=[blk, blk, blk, blk], out_shape=[S((r, c), F32)] * 4, name=name,
        compiler_params=_cp(("parallel",)))(parts, w, m, v)


def _mesh_pos():
    return lax.axis_index("x"), lax.axis_index("y"), lax.axis_index("c")


def _peer(d, x, y, c):
    return (1 - x if (d >> 2) & 1 else x, 1 - y if (d >> 1) & 1 else y, 1 - c if d & 1 else c)


def _gather_two_level(blocks, name):
    n = len(blocks)

    def body(*refs):
        srcs, outs = refs[:n], refs[n:2 * n]
        send_sems, recv_sems, loc_sems = refs[2 * n:]
        x, y, c = _mesh_pos()
        lin = lambda px, py, pc: 4 * px + 2 * py + pc
        me, sibling = (x, y, c), (x, y, 1 - c)
        chips = [(1 - x, y), (x, 1 - y), (1 - x, 1 - y)]

        def copy(a, k, block, to, src=None):
            slab = outs[a].at[lin(*block)]
            return pltpu.make_async_remote_copy(
                src_ref=slab if src is None else src, dst_ref=slab, send_sem=send_sems.at[a, k],
                recv_sem=recv_sems.at[a, k], device_id=to, device_id_type=pl.DeviceIdType.MESH)

        mine = [pltpu.make_async_copy(srcs[a], outs[a].at[lin(*me)], loc_sems.at[a]) for a in range(n)]
        first = [copy(a, 0, me, sibling, src=srcs[a]) for a in range(n)]
        first += [copy(a, 1 + j, me, (*chip, c), src=srcs[a]) for j, chip in enumerate(chips) for a in range(n)]
        for cp in mine + first:
            cp.start()
        passed = []
        for j, chip in enumerate(chips):
            for a in range(n):
                copy(a, 1 + j, (*chip, c), me).wait_recv()
                passed.append(copy(a, 4 + j, (*chip, c), sibling))
                passed[-1].start()
        for a in range(n):
            copy(a, 0, sibling, me).wait_recv()
            for j, chip in enumerate(chips):
                copy(a, 4 + j, (*chip, 1 - c), me).wait_recv()
        for cp in first + passed:
            cp.wait_send()
        for cp in mine:
            cp.wait()

    hbm = pl.BlockSpec(memory_space=pl.ANY)
    return pl.pallas_call(
        body, in_specs=[hbm] * n, out_specs=[hbm] * n, out_shape=[S((N_DEV,) + b.shape, b.dtype) for b in blocks],
        scratch_shapes=[pltpu.SemaphoreType.DMA((n, N_DEV - 1)), pltpu.SemaphoreType.DMA((n, N_DEV - 1)),
                        pltpu.SemaphoreType.DMA((n,))],
        name=name, compiler_params=pltpu.CompilerParams(has_side_effects=True))(*blocks)


def _xfer_start(items, name, after=None):
    n = len(items)
    kinds = [k for k, _ in items]
    srcs = [pltpu.with_memory_space_constraint(a, pltpu.HBM) for _, a in items]
    land_shapes = [((N_DEV,) + a.shape if k == 'gather' else a.shape, a.dtype) for k, a in items]
    lands = [pltpu.with_memory_space_constraint(lax.empty(s, dt), pltpu.HBM) for s, dt in land_shapes]
    extra = [] if after is None else [after]

    def body(*refs):
        src_refs, land_refs = refs[:n], refs[n:2 * n]
        outs = refs[2 * n + len(extra):]
        sems = outs[:2 * n]
        token = outs[4 * n]
        x, y, c = _mesh_pos()
        me = 4 * x + 2 * y + c
        for a in range(n):
            for d in range(1, N_DEV):
                px, py, pc = _peer(d, x, y, c)
                src = src_refs[a] if kinds[a] == 'gather' else src_refs[a].at[4 * px + 2 * py + pc]
                pltpu.make_async_remote_copy(
                    src_ref=src, dst_ref=land_refs[a].at[me], send_sem=sems[2 * a].at[d - 1],
                    recv_sem=sems[2 * a + 1].at[d - 1], device_id=(px, py, pc),
                    device_id_type=pl.DeviceIdType.MESH).start()
        token[...] = jnp.zeros_like(token)

    hbm = pl.BlockSpec(memory_space=pltpu.HBM)
    sem = pl.BlockSpec(memory_space=pltpu.SEMAPHORE)
    out_shape = ([pltpu.SemaphoreType.DMA((N_DEV - 1,))] * (2 * n)
                 + [pltpu.HBM(a.shape, a.dtype) for a in srcs] + [pltpu.HBM(s, dt) for s, dt in land_shapes]
                 + [S((8, 128), F32)])
    res = pl.pallas_call(
        body, name=name, out_shape=out_shape,
        in_specs=[hbm] * (2 * n) + [pl.BlockSpec(memory_space=pl.ANY)] * len(extra),
        out_specs=[sem] * (2 * n) + [hbm] * (2 * n) + [pl.BlockSpec(memory_space=pltpu.VMEM)],
        input_output_aliases={**{a: 2 * n + a for a in range(n)}, **{n + a: 3 * n + a for a in range(n)}},
        compiler_params=pltpu.CompilerParams(has_side_effects=pltpu.SideEffectType.DATAFLOW_SIDE_EFFECTING),
    )(*srcs, *lands, *extra)
    return (kinds, res[:2 * n], res[2 * n:3 * n], res[3 * n:4 * n]), res[4 * n]


def _xfer_wait(handle, after, name):
    kinds, sems, src_thru, land_thru = handle
    n = len(kinds)

    def body(*refs):
        land_refs = refs[n:2 * n]
        sem_refs = refs[2 * n:4 * n]
        x, y, c = _mesh_pos()
        me = 4 * x + 2 * y + c
        for a in range(n):
            for d in range(1, N_DEV):
                slab = land_refs[a].at[me]
                cp = pltpu.make_async_remote_copy(
                    src_ref=slab, dst_ref=slab, send_sem=sem_refs[2 * a].at[d - 1], recv_sem=sem_refs[2 * a + 1].at[d - 1],
                    device_id=_peer(d, x, y, c), device_id_type=pl.DeviceIdType.MESH)
                cp.wait_send()
                cp.wait_recv()

    hbm = pl.BlockSpec(memory_space=pltpu.HBM)
    sem = pl.BlockSpec(memory_space=pltpu.SEMAPHORE)
    res = pl.pallas_call(
        body, name=name,
        out_shape=[pltpu.HBM(a.shape, a.dtype) for a in src_thru] + [pltpu.HBM(a.shape, a.dtype) for a in land_thru],
        in_specs=[hbm] * (2 * n) + [sem] * (2 * n) + [pl.BlockSpec(memory_space=pl.ANY)],
        out_specs=[hbm] * (2 * n), input_output_aliases={a: a for a in range(2 * n)},
        compiler_params=pltpu.CompilerParams(has_side_effects=pltpu.SideEffectType.DATAFLOW_SIDE_EFFECTING),
    )(*src_thru, *land_thru, *sems, after)
    x, y, c = _mesh_pos()
    me = 4 * x + 2 * y + c
    out = []
    for a in range(n):
        src = res[a]
        own = src[None] if kinds[a] == 'gather' else lax.dynamic_index_in_dim(src, me, 0, keepdims=True)
        out.append(lax.dynamic_update_index_in_dim(res[n + a], own, me, 0))
    return out


def _stack_to_full(kind, st):
    if kind == 'row':
        return st.reshape(st.shape[0] * st.shape[1], st.shape[2])
    return jnp.concatenate([st[k] for k in range(st.shape[0])], axis=1)


def _full_to_stack(kind, full):
    r, c = full.shape
    if kind == 'row':
        return full.reshape(N_DEV, r // N_DEV, c)
    w = c // N_DEV
    return jnp.stack([full[:, k * w:(k + 1) * w] for k in range(N_DEV)], axis=0)


def _pack_small(named):
    layout = [(a.shape, a.size, -(-a.size // 1024) * 8) for a in named]
    total = sum(nr for _, _, nr in layout) * 128
    packed, off = None, 0
    for a, (_, n, nr) in zip(named, layout):
        part = jnp.pad(a.reshape(-1).astype(F32), (off, total - off - n))
        packed = part if packed is None else packed + part
        off += nr * 128
    return packed.reshape(total // 128, 128), layout


def _unpack_small(packed, layout):
    out, r0 = [], 0
    for shape, n, nr in layout:
        out.append(packed[r0:r0 + nr].reshape(-1)[:n].reshape(shape))
        r0 += nr
    return out


def _row0(acc):
    return acc[0]


def _local_step(x, p, tgt, sm, comm):
    t, d = x.shape
    h_n = sm['dt_bias'].shape[-1]
    ngrp = h_n // HEADS_PER_GROUP
    d_ssm = h_n * HEAD_DIM
    d_a = sm['ln_a_g'].shape[-1]
    d_mix = d_a + d_ssm
    d_xbc = sm['conv_ssm_b'].shape[-1]
    d_main = 2 * d_a + d_ssm + d_xbc
    assert d_xbc == d_ssm + 2 * ngrp * D_STATE and h_n <= 128
    zcol0, xcol0 = 2 * d_a, 2 * d_a + d_ssm
    vec = lambda v: v.reshape(1, -1)

    bst = jnp.pad(sm['b_s'].T, ((0, 0), (0, 128 - sm['b_s'].shape[0])))
    grp = lambda v, w: jnp.broadcast_to(jnp.pad(v.reshape(ngrp, 1, -1), ((0, 0), (0, 0), (0, w - v.size // ngrp))), (ngrp, 8, w))
    bias_p, alog_p = grp(sm['dt_bias'], 128), grp(sm['a_log'], 128)
    dskip_x = grp(jnp.repeat(sm['d_skip'], HEAD_DIM), GROUP_CH)
    normg_x = grp(sm['ssm_norm_g'], GROUP_CH)
    pad_dt = lambda v: jnp.pad(v[:, :h_n].reshape(t, ngrp, HEADS_PER_GROUP),
                               ((0, 0), (0, 0), (0, 128 - HEADS_PER_GROUP))).reshape(t, ngrp * 128)

    g_mix = vec(sm['norm_mix_g']) + comm.tok0
    a_n, a_t = _rms_fwd(x, g_mix, "rms_mix")
    wf = comm.weights('a', a_n)
    w_main = wf['w_in'][:, :d_main]
    w_dt = jnp.pad(wf['w_in'][:, d_main:], ((0, 0), (0, 128 - h_n)))
    proj = _mm_nn(a_n, w_main, out_dtype=ACT_DTYPE, name="mm_in")
    dtp = pad_dt(_mm_nn(a_n, w_dt, out_dtype=F32, name="mm_dt"))
    cat = _gmlp_fwd(proj, vec(sm['ln_a_g']), vec(sm['ln_a_b']), sm['w_s'], bst, vec(sm['norm_a_g']), d_mix, "gmlp_fwd")
    xbc, cpre = _conv_ssm_fwd(proj, xcol0, wf['conv_ssm_w'], vec(sm['conv_ssm_b']), "conv_ssm_fwd")
    cat, hs = _ssd_fwd(xbc, dtp, proj, zcol0, bias_p, alog_p, dskip_x, normg_x, cat, "ssd_fwd")
    wf.update(comm.weights('b', hs))
    h1 = _mm_nn(cat, wf['w_out'], out_dtype=F32, name="mm_out", res=x)
    f_n, f_t = _rms_fwd(h1, vec(sm['norm_ffn_g']), "rms_ffn")
    hid = _mm_nn(f_n, wf['w_up'], out_dtype=ACT_DTYPE, name="mm_up")
    act, act_t, cv = _conv_ffn_fwd(hid, wf['conv_ffn_w'], vec(sm['conv_ffn_b']), "conv_ffn_fwd")
    h2 = _mm_nn(act, wf['w_down'], out_dtype=F32, name="mm_down", res=h1)
    r_n, r_t = _rms_fwd(h2, vec(sm['norm_ple_g']), "rms_ple")
    q = _mm_nn(r_n, wf['w_ple_gate'], out_dtype=F32, name="mm_pg")
    p_m = p.astype(MXU_DTYPE)
    pe = _mm_nn(p_m, wf['w_ple'], out_dtype=F32, name="mm_ple")

    loss, dh3, dq, dpe, dgf = _head(h2, q, pe, tgt, vec(sm['norm_final_g']), "head")
    wgrad = lambda act_t, g, name, **kw: _mm_nn(act_t, g, out_dtype=WIRE_DTYPE, name=name, wide=True, **kw)
    gs = {}
    gs['norm_final_g'] = _row0(dgf)
    g_ple = wgrad(p_m.T, dpe, "wg_ple")
    g_pg = wgrad(r_t, dq, "wg_pg")
    dr = _mm_nt(dq, wf['w_ple_gate'], out_dtype=ACT_DTYPE, name="dg_pg")
    dh2, dh2m, dg = _rms_bwd(h2, vec(sm['norm_ple_g']), dr, dh3, "rms_ple_bwd")
    gs['norm_ple_g'] = _row0(dg)
    g_down = wgrad(act_t, dh2m, "wg_down")
    tok = comm.send('1', {'w_ple': g_ple, 'w_ple_gate': g_pg, 'w_down': g_down})
    dact = _mm_nt(dh2m, wf['w_down'], out_dtype=ACT_DTYPE, name="dg_down")
    dhid, dcw, dcb = _conv_ffn_bwd(hid, wf['conv_ffn_w'] + tok, cv, dact, "conv_ffn_bwd")
    kf = wf['conv_ffn_w'].shape[0]
    g_cf = jnp.concatenate([dcw[0, :kf], dcw[1, :kf]], axis=1)
    gs['conv_ffn_b'] = jnp.concatenate([dcb[0, 0], dcb[1, 0]], axis=0)
    g_up = wgrad(f_t, dhid, "wg_up", b_split=2, out_slabs=N_DEV)
    df = _mm_nt(dhid, wf['w_up'], out_dtype=ACT_DTYPE, name="dg_up", a_split=2)
    dh1, dh1m, dg = _rms_bwd(h1, vec(sm['norm_ffn_g']), df, dh2, "rms_ffn_bwd")
    gs['norm_ffn_g'] = _row0(dg)
    g_out = wgrad(cat.T, dh1m, "wg_out")
    tok = comm.send('2', {'conv_ffn_w': g_cf, 'w_up': g_up, 'w_out': g_out}, stacked=('w_up',))
    dcat = _mm_nt(dh1m, wf['w_out'], out_dtype=ACT_DTYPE, name="dg_out")
    dproj, dlng, dlnb, dws, dbst, dng = _gmlp_bwd(proj, vec(sm['ln_a_g']) + tok, vec(sm['ln_a_b']), sm['w_s'], bst,
                                                  vec(sm['norm_a_g']), dcat, d_main, "gmlp_bwd")
    gs['ln_a_g'], gs['ln_a_b'], gs['w_s'], gs['norm_a_g'] = _row0(dlng), _row0(dlnb), dws, _row0(dng)
    gs['b_s'] = dbst[:, :sm['b_s'].shape[0]].T
    dproj, dxs, dbm, dcm, ddtp, dbias, dalog, ddsk, dsng = _ssd_bwd(
        xbc, dtp, proj, zcol0, bias_p, alog_p, dskip_x, normg_x, hs, dcat, dproj, "ssd_bwd")
    gs['dt_bias'] = dbias[:, 0, :HEADS_PER_GROUP].reshape(h_n)
    gs['a_log'] = dalog[:, 0, :HEADS_PER_GROUP].reshape(h_n)
    gs['d_skip'] = ddsk[:, 0, :].reshape(h_n, HEAD_DIM).sum(axis=-1)
    gs['ssm_norm_g'] = dsng[:, 0, :].reshape(d_ssm)
    dws_c, dbs_c = [], []
    off = 0
    for nm, dpart in (("x", dxs), ("b", dbm), ("c", dcm)):
        dproj, dw_c, db_c = _conv_ssm_bwd(proj, xcol0 + off, wf['conv_ssm_w'], cpre, off, dpart, dproj,
                                          xcol0 + off, "conv_ssm_bwd_" + nm)
        dws_c.append(dw_c[:wf['conv_ssm_w'].shape[0]])
        dbs_c.append(db_c[0])
        off += dpart.shape[1]
    g_cs = jnp.concatenate(dws_c, axis=1)
    gs['conv_ssm_b'] = jnp.concatenate(dbs_c, axis=0)
    ddt = jnp.pad(ddtp.reshape(t, ngrp, 128)[:, :, :HEADS_PER_GROUP].reshape(t, h_n), ((0, 0), (0, 128 - h_n))).astype(MXU_DTYPE)
    g_in = jnp.concatenate([wgrad(a_t, dproj, "wg_in"), wgrad(a_t, ddt, "wg_dt")[:, :h_n]], axis=1)
    tok = comm.send('3', {'conv_ssm_w': g_cs, 'w_in': g_in}, [(n, gs[n]) for n in REPLICATED if n != 'norm_mix_g'])
    da = _mm_nt(ddt + tok.astype(ddt.dtype), w_dt, out_dtype=F32, name="dg_dt")
    da = _mm_nt(dproj, w_main, out_dtype=ACT_DTYPE, name="dg_in", res=da)
    dx, dg = _rms_bwd(x, g_mix + tok, da, dh1, "rms_mix_bwd", mx_copy=False)
    comm.send('4', {}, [('norm_mix_g', _row0(dg)), ('loss', loss[0, 0:1])])
    return dx


def kernel(x, p, norm_mix_g, w_in, ln_a_g, ln_a_b, w_s, b_s, norm_a_g, conv_ssm_w, conv_ssm_b, dt_bias, a_log, d_skip, ssm_norm_g, w_out, norm_ffn_g, w_up, conv_ffn_w, conv_ffn_b, w_down, norm_ple_g, w_ple_gate, w_ple, norm_final_g, loss_target, m_norm_mix_g, m_w_in, m_ln_a_g, m_ln_a_b, m_w_s, m_b_s, m_norm_a_g, m_conv_ssm_w, m_conv_ssm_b, m_dt_bias, m_a_log, m_d_skip, m_ssm_norm_g, m_w_out, m_norm_ffn_g, m_w_up, m_conv_ffn_w, m_conv_ffn_b, m_w_down, m_norm_ple_g, m_w_ple_gate, m_w_ple, m_norm_final_g, v_norm_mix_g, v_w_in, v_ln_a_g, v_ln_a_b, v_w_s, v_b_s, v_norm_a_g, v_conv_ssm_w, v_conv_ssm_b, v_dt_bias, v_a_log, v_d_skip, v_ssm_norm_g, v_w_out, v_norm_ffn_g, v_w_up, v_conv_ffn_w, v_conv_ffn_b, v_w_down, v_norm_ple_g, v_w_ple_gate, v_w_ple, v_norm_final_g):
    given = dict(locals())
    wts = {n: given[n] for n in WEIGHTS}
    ms = {n: given["m_" + n] for n in WEIGHTS}
    vs = {n: given["v_" + n] for n in WEIGHTS}
    sm = {n: (wts[n][0] if wts[n].ndim > 1 else wts[n]) for n in REPLICATED}
    comm = _Comm({n: wts[n][0] for n in SHARDED})
    dx = _local_step(x[0], p[0, 0], loss_target[0], sm, comm)

    out, loss_out, after = {}, None, dx
    for tag, names, small_names, layout, handle in comm.sent:
        recv = _xfer_wait(handle, after, "grads_%s_wait" % tag)
        for n, parts in zip(names, recv):
            out[n] = _adamw(parts, wts[n][0], ms[n][0], vs[n][0], "adamw_" + n)
            after = out[n][1]
        if small_names:
            pick = lambda src, fill: _pack_small([src[n] if n in src else jnp.full((1,), fill, F32) for n in small_names])[0]
            res = _adamw(recv[-1], pick(wts, 0.0), pick(ms, 0.0), pick(vs, 1.0), "adamw_small_" + tag)
            res = [_unpack_small(o, layout) for o in res]
            after = res[1][0]
            for i, n in enumerate(small_names):
                if n == 'loss':
                    loss_out = res[0][i].reshape(())
                else:
                    out[n] = [res[k][i] for k in range(4)]
    return (loss_out, dx[None], *[out[n][k].reshape(wts[n].shape) for k in range(4) for n in WEIGHTS])


class _Comm:
    GATHER_GROUPS = {'a': ('w_in', 'conv_ssm_w'), 'b': ('w_out', 'w_up', 'conv_ffn_w', 'w_down', 'w_ple_gate', 'w_ple')}

    def __init__(self, blocks):
        wired = lambda grp: [blocks[n].astype(_wire(n)) for n in self.GATHER_GROUPS[grp]]
        self.stacks_a = _gather_two_level(wired('a'), "gather_a")
        self.handle_b, tok = _xfer_start([('gather', b) for b in wired('b')], "gather_b_start", after=self.stacks_a[0])
        self.tok0 = tok[0, 0]
        self.sent = []

    def weights(self, grp, after):
        stacks = self.stacks_a if grp == 'a' else _xfer_wait(self.handle_b, after, "gather_b_wait")
        return {n: _stack_to_full(SHARDED[n], st) for n, st in zip(self.GATHER_GROUPS[grp], stacks)}

    def send(self, tag, gw, small=None, stacked=()):
        items = [('scatter', g if n in stacked else _full_to_stack(SHARDED[n], g.astype(_wire(n)))) for n, g in gw.items()]
        layout, small_names = None, []
        if small:
            packed, layout = _pack_small([a for _, a in small])
            small_names = [n for n, _ in small]
            items.append(('gather', packed))
        handle, tok = _xfer_start(items, "grads_%s_start" % tag)
        self.sent.append((tag, list(gw), small_names, layout, handle))
        return tok[0, 0]


def _wire(name):
    return F32 if name in F32_ON_WIRE else WIRE_DTYPE
```

```python
import functools

import jax
import jax.numpy as jnp
from jax import lax
from jax.experimental import pallas as pl
from jax.experimental.pallas import tpu as pltpu

F32 = jnp.float32
MXU_DTYPE = jnp.bfloat16
ACT_DTYPE = jnp.bfloat16
XBC_DTYPE = jnp.bfloat16
WIRE_DTYPE = jnp.bfloat16
EPS = 1e-6
CHUNK = 128
D_STATE = 128
HEAD_DIM = 64
HEADS_PER_GROUP = 4
GROUP_CH = HEAD_DIM * HEADS_PER_GROUP
HALO = 16
N_DEV = 8
VMEM_LIMIT = 48 * 1024 * 1024

ADAM_LR = 0.001
ADAM_B1 = 0.9
ADAM_B2 = 0.999
ADAM_EPS = 1e-08
ADAM_WD = 0.01
ADAM_STEP = 10

WEIGHTS = ['norm_mix_g', 'w_in', 'ln_a_g', 'ln_a_b', 'w_s', 'b_s', 'norm_a_g', 'conv_ssm_w', 'conv_ssm_b', 'dt_bias',
           'a_log', 'd_skip', 'ssm_norm_g', 'w_out', 'norm_ffn_g', 'w_up', 'conv_ffn_w', 'conv_ffn_b', 'w_down',
           'norm_ple_g', 'w_ple_gate', 'w_ple', 'norm_final_g']
SHARDED = {'w_in': 'col', 'conv_ssm_w': 'col', 'w_out': 'row', 'w_up': 'col', 'conv_ffn_w': 'col', 'w_down': 'row',
           'w_ple_gate': 'row', 'w_ple': 'col'}
F32_ON_WIRE = ('conv_ssm_w', 'conv_ffn_w')
REPLICATED = [n for n in WEIGHTS if n not in SHARDED]

S = jax.ShapeDtypeStruct


def _pick(dim, cands):
    for c in cands:
        if c <= dim and dim % c == 0:
            return c
    return dim


def _cp(sem, vmem=VMEM_LIMIT):
    return pltpu.CompilerParams(dimension_semantics=sem, vmem_limit_bytes=vmem)


def _mx(v):
    return v.astype(MXU_DTYPE)


def _rms(v, g):
    return v * lax.rsqrt(jnp.mean(v * v, axis=-1, keepdims=True) + EPS) * g


MM_VMEM_BUDGET = 34 * 1024 * 1024


def _mm_tiles(m, n, k, out_bytes, has_res, tn_cands=(512, 256, 128), k_mult=1, tm_cands=(1024, 512)):
    tn = _pick(n, tn_cands)
    ks = k // k_mult
    best = None
    for tm in [c for c in tm_cands if m % c == 0] or [_pick(m, (256, 128))]:
        for nk in range(1, ks // 128 + 1):
            if ks % nk or (ks // nk) % 128:
                continue
            tk = ks // nk
            need = 2 * 2 * (tm * tk + tk * tn) + tm * tn * (4 + 2 * out_bytes + (8 if has_res else 0))
            if need <= MM_VMEM_BUDGET:
                if best is None or (nk, -tm) < best[0]:
                    best = ((nk, -tm), (tm, tn, tk))
                break
    return best[1] if best else (_pick(m, (512, 256, 128)), tn, _pick(ks, (128,)))


def _mm_body(dot, nk, has_res):
    def body(*refs):
        if has_res:
            a_ref, b_ref, r_ref, o_ref, acc_ref = refs
        else:
            a_ref, b_ref, o_ref, acc_ref = refs
            r_ref = None
        kk = pl.program_id(2)
        d = dot(a_ref[...], b_ref[...])

        def fin(acc):
            if r_ref is not None:
                acc = acc + r_ref[...]
            o_ref[...] = acc.astype(o_ref.dtype)

        if nk == 1:
            fin(d)
        else:
            @pl.when(kk == 0)
            def _():
                acc_ref[...] = d

            @pl.when(kk > 0)
            def _():
                acc_ref[...] += d

            @pl.when(kk == nk - 1)
            def _():
                fin(acc_ref[...])

    return body


def _mm_call(body, grid, a_spec, b_spec, tm, tn, m, n, out_dtype, name, args, res, out_slabs=1):
    in_specs = [a_spec, b_spec]
    if res is not None:
        in_specs.append(pl.BlockSpec((tm, tn), lambda i, j, kk: (i, j)))
        args = args + [res]
    if out_slabs == 1:
        out_spec, out_shape = pl.BlockSpec((tm, tn), lambda i, j, kk: (i, j)), S((m, n), out_dtype)
    else:
        out_spec, out_shape = pl.BlockSpec((None, tm, tn), lambda i, j, kk: (j, i, 0)), S((out_slabs, m, tn), out_dtype)
    return pl.pallas_call(
        body, grid=grid, in_specs=in_specs, out_specs=out_spec, out_shape=out_shape,
        scratch_shapes=[pltpu.VMEM((tm, tn), F32)], name=name,
        compiler_params=_cp(("parallel", "parallel", "arbitrary")))(*args)


def _mm_nn(a, b, *, out_dtype, name, res=None, b_split=1, wide=False, out_slabs=1):
    m, k = a.shape
    n = b.shape[1] if b_split == 1 else b.shape[2] * b_split
    tn_cands = (n // out_slabs,) if out_slabs > 1 else (1024, 512, 256, 128) if wide else (512, 256, 128)
    tm, tn, tk = _mm_tiles(m, n // b_split, k, jnp.dtype(out_dtype).itemsize, res is not None, tn_cands=tn_cands)
    assert out_slabs == 1 or (tn * out_slabs == n and tn % 128 == 0)
    nk = k // tk
    njs = (n // b_split) // tn
    body = _mm_body(lambda x, y: jnp.dot(x, y, preferred_element_type=F32), nk, res is not None)
    if b_split == 1:
        b_spec = pl.BlockSpec((tk, tn), lambda i, j, kk: (kk, j))
    else:
        b_spec = pl.BlockSpec((None, tk, tn), lambda i, j, kk: (j // njs, kk, j % njs))
    return _mm_call(body, (m // tm, n // tn, nk), pl.BlockSpec((tm, tk), lambda i, j, kk: (i, kk)), b_spec,
                    tm, tn, m, n, out_dtype, name, [a, b], res, out_slabs)


def _mm_nt(a, b, *, out_dtype, name, res=None, a_split=1):
    if a_split == 1:
        m, k = a.shape
    else:
        m, k = a.shape[1], a.shape[2] * a_split
    n = b.shape[0]
    tm, tn, tk = _mm_tiles(m, n, k, jnp.dtype(out_dtype).itemsize, res is not None, k_mult=a_split, tm_cands=(1024,))
    nk = k // tk
    nks = nk // a_split
    body = _mm_body(lambda x, y: lax.dot_general(x, y, (((1,), (1,)), ((), ())), preferred_element_type=F32),
                    nk, res is not None)
    if a_split == 1:
        a_spec = pl.BlockSpec((tm, tk), lambda i, j, kk: (i, kk))
    else:
        a_spec = pl.BlockSpec((None, tm, tk), lambda i, j, kk: (kk // nks, i, kk % nks))
    return _mm_call(body, (m // tm, n // tn, nk), a_spec, pl.BlockSpec((tn, tk), lambda i, j, kk: (j, kk)),
                    tm, tn, m, n, out_dtype, name, [a, b], res)


def _rms_fwd(x, g, name):
    t, d = x.shape
    tr = _pick(t, (512, 256, 128))

    def body(x_ref, g_ref, o_ref, ot_ref):
        y = _rms(x_ref[...], g_ref[...]).astype(o_ref.dtype)
        o_ref[...] = y
        ot_ref[...] = y.T

    return pl.pallas_call(
        body, grid=(t // tr,),
        in_specs=[pl.BlockSpec((tr, d), lambda i: (i, 0)), pl.BlockSpec((1, d), lambda i: (0, 0))],
        out_specs=[pl.BlockSpec((tr, d), lambda i: (i, 0)), pl.BlockSpec((d, tr), lambda i: (0, i))],
        out_shape=[S((t, d), ACT_DTYPE), S((d, t), ACT_DTYPE)], name=name,
        compiler_params=_cp(("parallel",)))(x, g)


def _rms_bwd(xin, g, dn, dres, name, mx_copy=True):
    t, d = xin.shape
    tr = _pick(t, (256, 128))

    def body(x_ref, g_ref, dn_ref, dr_ref, dx_ref, *rest):
        dg_ref = rest[-1]

        @pl.when(pl.program_id(0) == 0)
        def _():
            dg_ref[...] = jnp.zeros_like(dg_ref)

        _, vjp = jax.vjp(_rms, x_ref[...], g_ref[...])
        dx, dg = vjp(dn_ref[...].astype(F32))
        dx = dr_ref[...] + dx
        dx_ref[...] = dx
        if mx_copy:
            rest[0][...] = dx.astype(rest[0].dtype)
        dg_ref[0:1, :] += dg

    row = pl.BlockSpec((tr, d), lambda i: (i, 0))
    acc = pl.BlockSpec((8, d), lambda i: (0, 0))
    return pl.pallas_call(
        body, grid=(t // tr,),
        in_specs=[row, pl.BlockSpec((1, d), lambda i: (0, 0)), row, row],
        out_specs=[row, row, acc] if mx_copy else [row, acc],
        out_shape=[S((t, d), F32)] + ([S((t, d), MXU_DTYPE)] if mx_copy else []) + [S((8, d), F32)], name=name,
        compiler_params=_cp(("arbitrary",)))(xin, g, dn, dres)


def _head(h2, q, pe, tgt, gf, name):
    t, d = h2.shape
    tr = _pick(t, (256, 128))

    def f(h2v, qv, pev, gfv, tv):
        h3 = h2v + jax.nn.sigmoid(qv) * pev
        y = _rms(h3, gfv)
        return 0.5 * jnp.sum(jnp.mean(jnp.square(y - tv), axis=-1))

    def body(h2_ref, q_ref, pe_ref, t_ref, g_ref, loss_ref, dh_ref, dq_ref, dpe_ref, dg_ref):
        @pl.when(pl.program_id(0) == 0)
        def _():
            loss_ref[...] = jnp.zeros_like(loss_ref)
            dg_ref[...] = jnp.zeros_like(dg_ref)

        tv = t_ref[...]
        loss, vjp = jax.vjp(lambda a, b, c, e: f(a, b, c, e, tv), h2_ref[...], q_ref[...].astype(F32),
                            pe_ref[...].astype(F32), g_ref[...])
        dh, dq, dpe, dg = vjp(jnp.ones((), F32))
        loss_ref[...] += jnp.full(loss_ref.shape, loss, F32)
        dh_ref[...] = dh
        dq_ref[...] = dq.astype(dq_ref.dtype)
        dpe_ref[...] = dpe.astype(dpe_ref.dtype)
        dg_ref[0:1, :] += dg

    row = pl.BlockSpec((tr, d), lambda i: (i, 0))
    return pl.pallas_call(
        body, grid=(t // tr,),
        in_specs=[row, row, row, row, pl.BlockSpec((1, d), lambda i: (0, 0))],
        out_specs=[pl.BlockSpec((8, 128), lambda i: (0, 0)), row, row, row, pl.BlockSpec((8, d), lambda i: (0, 0))],
        out_shape=[S((8, 128), F32), S((t, d), F32), S((t, d), MXU_DTYPE), S((t, d), MXU_DTYPE), S((8, d), F32)],
        name=name, compiler_params=_cp(("arbitrary",)))(h2, q, pe, tgt, gf)


def _gmlp_block(us, vs, lng, lnb, wss, bss, ng):
    n = us[0].shape[0]
    row = lax.broadcasted_iota(jnp.int32, (n, n), 0)
    col = lax.broadcasted_iota(jnp.int32, (n, n), 1)
    outs = []
    for u0, v0, lg, lb, ws, bs in zip(us, vs, lng, lnb, wss, bss):
        u = jax.nn.gelu(u0)
        v = jax.nn.gelu(v0)
        mu = jnp.mean(v, axis=-1, keepdims=True)
        var = jnp.mean(jnp.square(v - mu), axis=-1, keepdims=True)
        vn = (v - mu) * lax.rsqrt(var + EPS) * lg + lb
        w = jnp.where(row >= col, ws, 0.0)
        sg = jnp.dot(_mx(w), _mx(vn), preferred_element_type=F32) + bs
        outs.append(u * sg)
    return _rms(jnp.concatenate(outs, axis=1), ng)


def _gmlp_load(proj_ref, lng_ref, lnb_ref, ws_ref, bst_ref, d_a, ng):
    sl = lambda g: slice(CHUNK * g, CHUNK * (g + 1))
    us = [proj_ref[:, sl(g)].astype(F32) for g in range(ng)]
    vs = [proj_ref[:, d_a + CHUNK * g: d_a + CHUNK * (g + 1)].astype(F32) for g in range(ng)]
    lng = [lng_ref[:, sl(g)] for g in range(ng)]
    lnb = [lnb_ref[:, sl(g)] for g in range(ng)]
    wss = [ws_ref[g] for g in range(ng)]
    bss = [bst_ref[:, g:g + 1] for g in range(ng)]
    return us, vs, lng, lnb, wss, bss


def _gmlp_fwd(proj, ln_g, ln_b, w_s, bst, norm_g, d_mix, name):
    t = proj.shape[0]
    ng = w_s.shape[0]
    d_a = ng * CHUNK

    def body(proj_ref, lng_ref, lnb_ref, ws_ref, bst_ref, ng_ref, o_ref):
        args = _gmlp_load(proj_ref, lng_ref, lnb_ref, ws_ref, bst_ref, d_a, ng)
        o_ref[...] = _gmlp_block(*args, ng_ref[...]).astype(o_ref.dtype)

    vec = pl.BlockSpec((1, d_a), lambda c: (0, 0))
    return pl.pallas_call(
        body, grid=(t // CHUNK,),
        in_specs=[pl.BlockSpec((CHUNK, 2 * d_a), lambda c: (c, 0)), vec, vec,
                  pl.BlockSpec((ng, CHUNK, CHUNK), lambda c: (0, 0, 0)), pl.BlockSpec((CHUNK, 128), lambda c: (0, 0)), vec],
        out_specs=pl.BlockSpec((CHUNK, d_a), lambda c: (c, 0)), out_shape=S((t, d_mix), ACT_DTYPE),
        name=name, compiler_params=_cp(("parallel",)))(proj, ln_g, ln_b, w_s, bst, norm_g)


def _gmlp_bwd(proj, ln_g, ln_b, w_s, bst, norm_g, dcat, d_proj, name):
    t = proj.shape[0]
    ng = w_s.shape[0]
    d_a = ng * CHUNK

    def body(proj_ref, lng_ref, lnb_ref, ws_ref, bst_ref, ng_ref, dy_ref,
             dp_ref, dlng_ref, dlnb_ref, dws_ref, dbst_ref, dng_ref):
        @pl.when(pl.program_id(0) == 0)
        def _():
            for r in (dlng_ref, dlnb_ref, dws_ref, dbst_ref, dng_ref):
                r[...] = jnp.zeros_like(r)

        args = _gmlp_load(proj_ref, lng_ref, lnb_ref, ws_ref, bst_ref, d_a, ng)
        _, vjp = jax.vjp(_gmlp_block, *args, ng_ref[...])
        dus, dvs, dlng, dlnb, dwss, dbss, dng = vjp(dy_ref[...].astype(F32))
        lane = lax.broadcasted_iota(jnp.int32, (1, 128), 1)
        dbst = jnp.zeros((CHUNK, 128), F32)
        for g in range(ng):
            dp_ref[:, CHUNK * g:CHUNK * (g + 1)] = dus[g].astype(dp_ref.dtype)
            dp_ref[:, d_a + CHUNK * g:d_a + CHUNK * (g + 1)] = dvs[g].astype(dp_ref.dtype)
            dlng_ref[0:1, CHUNK * g:CHUNK * (g + 1)] += dlng[g]
            dlnb_ref[0:1, CHUNK * g:CHUNK * (g + 1)] += dlnb[g]
            dws_ref[g] += dwss[g]
            dbst = dbst + dbss[g] * (lane == g).astype(F32)
        dbst_ref[...] += dbst
        dng_ref[0:1, :] += dng

    vec = pl.BlockSpec((1, d_a), lambda c: (0, 0))
    acc = pl.BlockSpec((8, d_a), lambda c: (0, 0))
    wspec = pl.BlockSpec((ng, CHUNK, CHUNK), lambda c: (0, 0, 0))
    bspec = pl.BlockSpec((CHUNK, 128), lambda c: (0, 0))
    return pl.pallas_call(
        body, grid=(t // CHUNK,),
        in_specs=[pl.BlockSpec((CHUNK, 2 * d_a), lambda c: (c, 0)), vec, vec, wspec, bspec, vec,
                  pl.BlockSpec((CHUNK, d_a), lambda c: (c, 0))],
        out_specs=[pl.BlockSpec((CHUNK, 2 * d_a), lambda c: (c, 0)), acc, acc, wspec, bspec, acc],
        out_shape=[S((t, d_proj), ACT_DTYPE), S((8, d_a), F32), S((8, d_a), F32), S((ng, CHUNK, CHUNK), F32),
                   S((CHUNK, 128), F32), S((8, d_a), F32)],
        name=name, compiler_params=_cp(("arbitrary",)))(proj, ln_g, ln_b, w_s, bst, norm_g, dcat)


def _silu_grad(c):
    s = jax.nn.sigmoid(c)
    return s * (1.0 + c * (1.0 - s))


def _fill_prev_main(s_ref, prev_ref, main_ref, i, tt):
    s_ref[pl.ds(0, HALO), :] = jnp.where(i > 0, prev_ref[...].astype(F32), 0.0)
    s_ref[pl.ds(HALO, tt), :] = main_ref[...].astype(F32)


def _prev_spec(tt, tc, joff):
    return pl.BlockSpec((HALO, tc), lambda j, i: (jnp.maximum(i * (tt // HALO) - 1, 0), j + joff))


def _next_spec(tt, tc, joff, t):
    return pl.BlockSpec((HALO, tc), lambda j, i: (jnp.minimum((i + 1) * (tt // HALO), t // HALO - 1), j + joff))


CONV_RC = 32


def _conv_tiles(t, c):
    return _pick(t, (1024, 512, 256, 128)), _pick(c, (256, 128))


def _row_chunks(tt, fn, init=0):
    rc = min(CONV_RC, tt)
    return lax.fori_loop(0, tt // rc, lambda q, c: fn(pl.multiple_of(q * rc, rc), rc, c), init)


def _fold8(p):
    acc = p[0:8]
    for r in range(8, p.shape[0], 8):
        acc = acc + p[r:r + 8]
    return acc


def _taps_chunk(s_ref, w_ref, kw, r0, rc):
    xe = s_ref[pl.ds(HALO - 8 + r0, rc + 8), :]
    acc = w_ref[0:1, :] * xe[8 - (kw - 1):8 - (kw - 1) + rc]
    for k in range(1, kw):
        acc = acc + w_ref[k:k + 1, :] * xe[8 - (kw - 1) + k:8 - (kw - 1) + k + rc]
    return acc


def _conv_bwd_chunk(sd_ref, x, w_ref, kw, r0, rc, dws):
    de = sd_ref[pl.ds(r0, rc + 8), :]
    dx, out = None, list(dws)
    for j in range(kw):
        d = de[j:j + rc]
        k = kw - 1 - j
        term = w_ref[k:k + 1, :] * d
        dx = term if dx is None else dx + term
        out[k] = out[k] + _fold8(x * d)
    return dx, out


def _conv_ssm_fwd(proj, col0, w, b, name):
    t = proj.shape[0]
    kw, c = w.shape
    tt, tc = _conv_tiles(t, c)
    joff = col0 // tc
    assert col0 % tc == 0

    def body(x_ref, xp_ref, w_ref, b_ref, o_ref, c_ref, s_ref):
        _fill_prev_main(s_ref, xp_ref, x_ref, pl.program_id(1), tt)

        def chunk(r0, rc, carry):
            cpre = _taps_chunk(s_ref, w_ref, kw, r0, rc) + b_ref[...]
            o_ref[pl.ds(r0, rc), :] = jax.nn.silu(cpre).astype(o_ref.dtype)
            c_ref[pl.ds(r0, rc), :] = cpre.astype(c_ref.dtype)
            return carry

        _row_chunks(tt, chunk)

    out = pl.BlockSpec((tt, tc), lambda j, i: (i, j))
    return pl.pallas_call(
        body, grid=(c // tc, t // tt),
        in_specs=[pl.BlockSpec((tt, tc), lambda j, i: (i, j + joff)), _prev_spec(tt, tc, joff),
                  pl.BlockSpec((kw, tc), lambda j, i: (0, j)), pl.BlockSpec((1, tc), lambda j, i: (0, j))],
        out_specs=[out, out], out_shape=[S((t, c), XBC_DTYPE), S((t, c), ACT_DTYPE)],
        scratch_shapes=[pltpu.VMEM((HALO + tt, tc), F32)], name=name,
        compiler_params=_cp(("parallel", "arbitrary")))(proj, proj, w, b)


def _conv_ssm_bwd(proj, col0, w, cpre, wcol0, dact, dproj, out_col0, name):
    t = proj.shape[0]
    kw = w.shape[0]
    c = dact.shape[1]
    tt, tc = _conv_tiles(t, c)
    assert col0 % tc == 0 and wcol0 % tc == 0 and out_col0 % tc == 0
    joff, wj, oj = col0 // tc, wcol0 // tc, out_col0 // tc
    nt = t // tt

    def body(x_ref, w_ref, c_ref, cn_ref, d_ref, dn_ref, dp_in, dx_ref, dw_ref, db_ref, sd_ref):
        del dp_in
        i = pl.program_id(1)

        @pl.when(i == 0)
        def _():
            dw_ref[...] = jnp.zeros_like(dw_ref)
            db_ref[...] = jnp.zeros_like(db_ref)

        def stage(r0, rc, db):
            rows = pl.ds(r0, rc)
            d = d_ref[rows, :].astype(F32) * _silu_grad(c_ref[rows, :].astype(F32))
            sd_ref[rows, :] = d
            return db + _fold8(d)

        zero8 = jnp.zeros((8, tc), F32)
        db = _row_chunks(tt, stage, zero8)
        sd_ref[pl.ds(tt, HALO), :] = jnp.where(
            i < nt - 1, dn_ref[...].astype(F32) * _silu_grad(cn_ref[...].astype(F32)), 0.0)

        def chunk(r0, rc, dws):
            dx, dws = _conv_bwd_chunk(sd_ref, x_ref[pl.ds(r0, rc), :].astype(F32), w_ref, kw, r0, rc, dws)
            dx_ref[pl.ds(r0, rc), :] = dx.astype(dx_ref.dtype)
            return dws

        dws = _row_chunks(tt, chunk, [zero8] * kw)
        for k in range(kw):
            dw_ref[k:k + 1, :] += jnp.sum(dws[k], axis=0, keepdims=True)
        db_ref[0:1, :] += jnp.sum(db, axis=0, keepdims=True)

    acc = pl.BlockSpec((8, tc), lambda j, i: (0, j))
    return pl.pallas_call(
        body, grid=(c // tc, nt),
        in_specs=[pl.BlockSpec((tt, tc), lambda j, i: (i, j + joff)), pl.BlockSpec((kw, tc), lambda j, i: (0, j + wj)),
                  pl.BlockSpec((tt, tc), lambda j, i: (i, j + wj)), _next_spec(tt, tc, wj, t),
                  pl.BlockSpec((tt, tc), lambda j, i: (i, j)), _next_spec(tt, tc, 0, t),
                  pl.BlockSpec(memory_space=pl.ANY)],
        out_specs=[pl.BlockSpec((tt, tc), lambda j, i: (i, j + oj)), acc, acc],
        out_shape=[S(dproj.shape, dproj.dtype), S((8, c), F32), S((8, c), F32)],
        scratch_shapes=[pltpu.VMEM((tt + HALO, tc), F32)],
        input_output_aliases={6: 0}, name=name,
        compiler_params=_cp(("parallel", "arbitrary")))(proj, w, cpre, cpre, dact, dact, dproj)


def _conv_ffn_fwd(hid, w, b, name):
    t, f2 = hid.shape
    f = f2 // 2
    kw = w.shape[0]
    tt, tc = _conv_tiles(t, f)
    nj = f // tc

    def body(g_ref, gp_ref, u_ref, up_ref, wg_ref, wu_ref, bg_ref, bu_ref, o_ref, ot_ref, cv_ref, sg_ref, su_ref):
        i = pl.program_id(1)
        _fill_prev_main(sg_ref, gp_ref, g_ref, i, tt)
        _fill_prev_main(su_ref, up_ref, u_ref, i, tt)

        def chunk(r0, rc, carry):
            rows = pl.ds(r0, rc)
            gate = _taps_chunk(sg_ref, wg_ref, kw, r0, rc) + bg_ref[...]
            up = _taps_chunk(su_ref, wu_ref, kw, r0, rc) + bu_ref[...]
            o_ref[rows, :] = (jax.nn.silu(gate) * up).astype(o_ref.dtype)
            cv_ref[0, rows, :] = gate.astype(cv_ref.dtype)
            cv_ref[1, rows, :] = up.astype(cv_ref.dtype)
            return carry

        _row_chunks(tt, chunk)
        ot_ref[...] = o_ref[...].T

    return pl.pallas_call(
        body, grid=(nj, t // tt),
        in_specs=[pl.BlockSpec((tt, tc), lambda j, i: (i, j)), _prev_spec(tt, tc, 0),
                  pl.BlockSpec((tt, tc), lambda j, i: (i, j + nj)), _prev_spec(tt, tc, nj),
                  pl.BlockSpec((kw, tc), lambda j, i: (0, j)), pl.BlockSpec((kw, tc), lambda j, i: (0, j + nj)),
                  pl.BlockSpec((1, tc), lambda j, i: (0, j)), pl.BlockSpec((1, tc), lambda j, i: (0, j + nj))],
        out_specs=[pl.BlockSpec((tt, tc), lambda j, i: (i, j)), pl.BlockSpec((tc, tt), lambda j, i: (j, i)),
                   pl.BlockSpec((2, tt, tc), lambda j, i: (0, i, j))],
        out_shape=[S((t, f), ACT_DTYPE), S((f, t), ACT_DTYPE), S((2, t, f), ACT_DTYPE)],
        scratch_shapes=[pltpu.VMEM((HALO + tt, tc), F32), pltpu.VMEM((HALO + tt, tc), F32)], name=name,
        compiler_params=_cp(("parallel", "arbitrary")))(hid, hid, hid, hid, w, w, b, b)


def _conv_ffn_bwd(hid, w, cv, dact, name):
    t, f2 = hid.shape
    f = f2 // 2
    kw = w.shape[0]
    tt, tc = _conv_tiles(t, f)
    nj = f // tc
    nt = t // tt

    def body(g_ref, u_ref, wg_ref, wu_ref, cv_ref, cvn_ref, d_ref, dn_ref, dh_ref, dw_ref, db_ref, dg_ref, du_ref):
        i = pl.program_id(1)

        @pl.when(i == 0)
        def _():
            dw_ref[...] = jnp.zeros_like(dw_ref)
            db_ref[...] = jnp.zeros_like(db_ref)

        def cotangents(gate, up, dact_v):
            sg = jax.nn.sigmoid(gate)
            return dact_v * up * (sg * (1.0 + gate * (1.0 - sg))), dact_v * (gate * sg)

        def stage(r0, rc, dbs):
            rows = pl.ds(r0, rc)
            dg, du = cotangents(cv_ref[0, rows, :].astype(F32), cv_ref[1, rows, :].astype(F32), d_ref[rows, :].astype(F32))
            dg_ref[rows, :] = dg
            du_ref[rows, :] = du
            return [dbs[0] + _fold8(dg), dbs[1] + _fold8(du)]

        zero8 = jnp.zeros((8, tc), F32)
        dbs = _row_chunks(tt, stage, [zero8, zero8])
        dgn, dun = cotangents(cvn_ref[0].astype(F32), cvn_ref[1].astype(F32), dn_ref[...].astype(F32))
        dg_ref[pl.ds(tt, HALO), :] = jnp.where(i < nt - 1, dgn, 0.0)
        du_ref[pl.ds(tt, HALO), :] = jnp.where(i < nt - 1, dun, 0.0)
        for s, (sd_ref, x_ref, w_ref) in enumerate(((dg_ref, g_ref, wg_ref), (du_ref, u_ref, wu_ref))):
            def chunk(r0, rc, dws, s=s, sd_ref=sd_ref, x_ref=x_ref, w_ref=w_ref):
                dx, dws = _conv_bwd_chunk(sd_ref, x_ref[pl.ds(r0, rc), :].astype(F32), w_ref, kw, r0, rc, dws)
                dh_ref[s, pl.ds(r0, rc), :] = dx.astype(dh_ref.dtype)
                return dws

            dws = _row_chunks(tt, chunk, [zero8] * kw)
            for k in range(kw):
                dw_ref[s, k:k + 1, :] += jnp.sum(dws[k], axis=0, keepdims=True)
            db_ref[s, 0:1, :] += jnp.sum(dbs[s], axis=0, keepdims=True)

    acc = pl.BlockSpec((2, 8, tc), lambda j, i: (0, 0, j))
    dsc = pltpu.VMEM((tt + HALO, tc), F32)
    nxt = lambda j, i: (0, jnp.minimum((i + 1) * (tt // HALO), t // HALO - 1), j)
    return pl.pallas_call(
        body, grid=(nj, nt),
        in_specs=[pl.BlockSpec((tt, tc), lambda j, i: (i, j)), pl.BlockSpec((tt, tc), lambda j, i: (i, j + nj)),
                  pl.BlockSpec((kw, tc), lambda j, i: (0, j)), pl.BlockSpec((kw, tc), lambda j, i: (0, j + nj)),
                  pl.BlockSpec((2, tt, tc), lambda j, i: (0, i, j)), pl.BlockSpec((2, HALO, tc), nxt),
                  pl.BlockSpec((tt, tc), lambda j, i: (i, j)), _next_spec(tt, tc, 0, t)],
        out_specs=[pl.BlockSpec((2, tt, tc), lambda j, i: (0, i, j)), acc, acc],
        out_shape=[S((2, t, f), MXU_DTYPE), S((2, 8, f), F32), S((2, 8, f), F32)],
        scratch_shapes=[dsc, dsc], name=name,
        compiler_params=_cp(("parallel", "arbitrary")))(hid, hid, w, w, cv, cv, dact, dact)


SSD_SUB = 2


def _ssd_chunk(xs, bm, cm, dtraw, z, hin, bias, alog, dskip, normg):
    n = bm.shape[0]
    row = lax.broadcasted_iota(jnp.int32, (n, n), 0)
    col = lax.broadcasted_iota(jnp.int32, (n, n), 1)
    causal = row >= col
    lane = lax.broadcasted_iota(jnp.int32, (1, 128), 1)
    sub = lax.broadcasted_iota(jnp.int32, (128, 1), 0)
    last = (lax.broadcasted_iota(jnp.int32, (n, 1), 0) == n - 1).astype(F32)
    dt = jax.nn.softplus(dtraw + bias)
    adt = dt * (-jnp.exp(alog))
    tri = causal.astype(F32)
    acs = jnp.dot(tri, adt, preferred_element_type=F32, precision=lax.Precision.HIGHEST)
    ch = lax.broadcasted_iota(jnp.int32, (128, GROUP_CH), 1)
    hd = lax.broadcasted_iota(jnp.int32, (128, GROUP_CH), 0) * HEAD_DIM
    expand = ((ch >= hd) & (ch < hd + HEAD_DIM)).astype(F32)
    acs_x = jnp.dot(acs, expand, preferred_element_type=F32, precision=lax.Precision.HIGHEST)
    alast_x = jnp.sum(acs_x * last, axis=0, keepdims=True)
    acs_t = acs.T
    scores = lax.dot_general(_mx(cm), _mx(bm), (((1,), (1,)), ((), ())), preferred_element_type=F32)
    yds, xts = [], []
    for r in range(HEADS_PER_GROUP):
        pick = (lane == r).astype(F32)
        acol = jnp.sum(acs * pick, axis=1, keepdims=True)
        arow = jnp.sum(acs_t * (sub == r).astype(F32), axis=0, keepdims=True)
        dtc = jnp.sum(dt * pick, axis=1, keepdims=True)
        lm = jnp.exp(jnp.where(causal, acol - arow, -1e30))
        xts.append(xs[r] * dtc)
        yds.append(jnp.dot(_mx(scores * lm), _mx(xts[r]), preferred_element_type=F32))
    xt = jnp.concatenate(xts, axis=1)
    yo = jnp.exp(acs_x) * jnp.dot(_mx(cm), _mx(hin), preferred_element_type=F32)
    st = lax.dot_general(_mx(bm), _mx(xt * jnp.exp(alast_x - acs_x)), (((0,), (0,)), ((), ())), preferred_element_type=F32)
    hout = jnp.exp(alast_x) * hin + st
    y = (jnp.concatenate(yds, axis=1) + yo + dskip * jnp.concatenate(xs, axis=1)) * jax.nn.silu(z)
    return _rms(y, normg), hout


def _ssd_block(datas, hin, consts):
    ys = []
    for data in datas:
        y, hin = _ssd_chunk(*data, hin, *consts)
        ys.append(y)
    return ys, hin


def _ssd_specs(d_ssm, ngrp, zcol0, rev, nb):
    cc = (lambda c: nb - 1 - c) if rev else (lambda c: c)
    rows = SSD_SUB * CHUNK
    xj, bj, cj, zj = 0, d_ssm // 128, d_ssm // 128 + ngrp, zcol0 // GROUP_CH
    const = lambda w: pl.BlockSpec((None, 8, w), lambda g, c: (g, 0, 0))
    return cc, [
        pl.BlockSpec((rows, GROUP_CH), lambda g, c: (cc(c), xj + g)),
        pl.BlockSpec((rows, 128), lambda g, c: (cc(c), bj + g)),
        pl.BlockSpec((rows, 128), lambda g, c: (cc(c), cj + g)),
        pl.BlockSpec((rows, 128), lambda g, c: (cc(c), g)),
        pl.BlockSpec((rows, GROUP_CH), lambda g, c: (cc(c), zj + g)),
        const(128), const(128), const(GROUP_CH), const(GROUP_CH)]


def _sub_rows(s):
    return slice(CHUNK * s, CHUNK * (s + 1))


def _ssd_load(x_ref, b_ref, c_ref, dt_ref, z_ref, bias_ref, alog_ref, dsk_ref, ng_ref):
    datas = []
    for s in range(SSD_SUB):
        rows = _sub_rows(s)
        xs = [x_ref[rows, HEAD_DIM * r:HEAD_DIM * (r + 1)].astype(F32) for r in range(HEADS_PER_GROUP)]
        datas.append((xs, b_ref[rows, :].astype(F32), c_ref[rows, :].astype(F32), dt_ref[rows, :],
                      z_ref[rows, :].astype(F32)))
    return datas, (bias_ref[0:1, :], alog_ref[0:1, :], dsk_ref[0:1, :], ng_ref[0:1, :])


def _ssd_fwd(xbc, dtp, proj, zcol0, bias_p, alog_p, dskip_x, normg_x, cat, name):
    t = xbc.shape[0]
    ngrp = bias_p.shape[0]
    d_ssm = ngrp * GROUP_CH
    rows = SSD_SUB * CHUNK
    nb = t // rows
    d_a = cat.shape[1] - d_ssm
    assert d_a % GROUP_CH == 0 and zcol0 % GROUP_CH == 0 and t % rows == 0
    _, specs = _ssd_specs(d_ssm, ngrp, zcol0, False, nb)

    def body(x_ref, b_ref, c_ref, dt_ref, z_ref, bias_ref, alog_ref, dsk_ref, ng_ref, cat_in, y_ref, hs_ref, h_ref):
        del cat_in

        @pl.when(pl.program_id(1) == 0)
        def _():
            h_ref[...] = jnp.zeros_like(h_ref)

        datas, consts = _ssd_load(x_ref, b_ref, c_ref, dt_ref, z_ref, bias_ref, alog_ref, dsk_ref, ng_ref)
        hs_ref[...] = h_ref[...]
        ys, hout = _ssd_block(datas, h_ref[...], consts)
        for s in range(SSD_SUB):
            y_ref[_sub_rows(s), :] = ys[s].astype(y_ref.dtype)
        h_ref[...] = hout

    return pl.pallas_call(
        body, grid=(ngrp, nb), in_specs=specs + [pl.BlockSpec(memory_space=pl.ANY)],
        out_specs=[pl.BlockSpec((rows, GROUP_CH), lambda g, c: (c, d_a // GROUP_CH + g)),
                   pl.BlockSpec((None, None, D_STATE, GROUP_CH), lambda g, c: (c, g, 0, 0))],
        out_shape=[S(cat.shape, cat.dtype), S((nb, ngrp, D_STATE, GROUP_CH), F32)],
        scratch_shapes=[pltpu.VMEM((D_STATE, GROUP_CH), F32)],
        input_output_aliases={9: 0}, name=name,
        compiler_params=_cp(("parallel", "arbitrary")))(xbc, xbc, xbc, dtp, proj, bias_p, alog_p, dskip_x, normg_x, cat)


def _ssd_bwd(xbc, dtp, proj, zcol0, bias_p, alog_p, dskip_x, normg_x, hs, dcat, dproj, name):
    t = xbc.shape[0]
    ngrp = bias_p.shape[0]
    d_ssm = ngrp * GROUP_CH
    rows = SSD_SUB * CHUNK
    nb = t // rows
    d_a = dcat.shape[1] - d_ssm
    cc, specs = _ssd_specs(d_ssm, ngrp, zcol0, True, nb)

    def body(x_ref, b_ref, c_ref, dt_ref, z_ref, bias_ref, alog_ref, dsk_ref, ng_ref, hs_ref, dy_ref, dp_in,
             dz_ref, dx_ref, db_ref, dc_ref, ddt_ref, dbias_ref, dalog_ref, ddsk_ref, dng_ref, dh_ref):
        del dp_in

        @pl.when(pl.program_id(1) == 0)
        def _():
            dh_ref[...] = jnp.zeros_like(dh_ref)
            for r in (dbias_ref, dalog_ref, ddsk_ref, dng_ref):
                r[...] = jnp.zeros_like(r)

        datas, consts = _ssd_load(x_ref, b_ref, c_ref, dt_ref, z_ref, bias_ref, alog_ref, dsk_ref, ng_ref)
        _, vjp = jax.vjp(_ssd_block, datas, hs_ref[...], consts)
        dys = [dy_ref[_sub_rows(s), :].astype(F32) for s in range(SSD_SUB)]
        ddatas, dhin, (dbias, dalog, ddsk, dng) = vjp((dys, dh_ref[...]))
        for s, (dxs, dbm, dcm, ddt, dz) in enumerate(ddatas):
            rws = _sub_rows(s)
            for r in range(HEADS_PER_GROUP):
                dx_ref[rws, HEAD_DIM * r:HEAD_DIM * (r + 1)] = dxs[r].astype(dx_ref.dtype)
            db_ref[rws, :] = dbm.astype(db_ref.dtype)
            dc_ref[rws, :] = dcm.astype(dc_ref.dtype)
            ddt_ref[rws, :] = ddt
            dz_ref[rws, :] = dz.astype(dz_ref.dtype)
        dh_ref[...] = dhin
        dbias_ref[0:1, :] += dbias
        dalog_ref[0:1, :] += dalog
        ddsk_ref[0:1, :] += ddsk
        dng_ref[0:1, :] += dng

    acc = lambda w: pl.BlockSpec((None, 8, w), lambda g, c: (g, 0, 0))
    blk = lambda w: pl.BlockSpec((rows, w), lambda g, c: (cc(c), g))
    return pl.pallas_call(
        body, grid=(ngrp, nb),
        in_specs=specs + [pl.BlockSpec((None, None, D_STATE, GROUP_CH), lambda g, c: (cc(c), g, 0, 0)),
                          pl.BlockSpec((rows, GROUP_CH), lambda g, c: (cc(c), d_a // GROUP_CH + g)),
                          pl.BlockSpec(memory_space=pl.ANY)],
        out_specs=[pl.BlockSpec((rows, GROUP_CH), lambda g, c: (cc(c), zcol0 // GROUP_CH + g)),
                   blk(GROUP_CH), blk(128), blk(128), blk(128), acc(128), acc(128), acc(GROUP_CH), acc(GROUP_CH)],
        out_shape=[S(dproj.shape, dproj.dtype), S((t, d_ssm), XBC_DTYPE), S((t, ngrp * 128), XBC_DTYPE),
                   S((t, ngrp * 128), XBC_DTYPE), S((t, ngrp * 128), F32), S((ngrp, 8, 128), F32),
                   S((ngrp, 8, 128), F32), S((ngrp, 8, GROUP_CH), F32), S((ngrp, 8, GROUP_CH), F32)],
        scratch_shapes=[pltpu.VMEM((D_STATE, GROUP_CH), F32)],
        input_output_aliases={11: 0}, name=name,
        compiler_params=_cp(("parallel", "arbitrary")))(xbc, xbc, xbc, dtp, proj, bias_p, alog_p, dskip_x, normg_x, hs, dcat, dproj)


def _adamw(parts, w, m, v, name):
    r, c = w.shape
    tr = _pick(r, (256, 128, 64, 32, 16, 8)) if c * 4 * 256 <= 4 * 1024 * 1024 else _pick(r, (64, 32, 16, 8))

    def body(p_ref, w_ref, m_ref, v_ref, g_ref, d_ref, nm_ref, nv_ref):
        g = p_ref[0].astype(F32)
        for k in range(1, N_DEV):
            g = g + p_ref[k].astype(F32)
        mm = ADAM_B1 * m_ref[...] + (1.0 - ADAM_B1) * g
        vv = ADAM_B2 * v_ref[...] + (1.0 - ADAM_B2) * jnp.square(g)
        m_hat = mm / (1.0 - ADAM_B1 ** ADAM_STEP)
        v_hat = vv / (1.0 - ADAM_B2 ** ADAM_STEP)
        g_ref[...] = g
        d_ref[...] = -ADAM_LR * (m_hat / (jnp.sqrt(v_hat) + ADAM_EPS) + ADAM_WD * w_ref[...])
        nm_ref[...] = mm
        nv_ref[...] = vv

    blk = pl.BlockSpec((tr, c), lambda i: (i, 0))
    return pl.pallas_call(
        body, grid=(r // tr,),
        in_specs=[pl.BlockSpec((N_DEV, tr, c), lambda i: (0, i, 0)), blk, blk, blk],
        out_specs=[blk, blk, blk, blk], out_shape=[S((r, c), F32)] * 4, name=name,
        compiler_params=_cp(("parallel",)))(parts, w, m, v)


def _mesh_pos():
    return lax.axis_index("x"), lax.axis_index("y"), lax.axis_index("c")


def _peer(d, x, y, c):
    return (1 - x if (d >> 2) & 1 else x, 1 - y if (d >> 1) & 1 else y, 1 - c if d & 1 else c)


def _gather_two_level(blocks, name):
    n = len(blocks)

    def body(*refs):
        srcs, outs = refs[:n], refs[n:2 * n]
        send_sems, recv_sems, loc_sems = refs[2 * n:]
        x, y, c = _mesh_pos()
        lin = lambda px, py, pc: 4 * px + 2 * py + pc
        me, sibling = (x, y, c), (x, y, 1 - c)
        chips = [(1 - x, y), (x, 1 - y), (1 - x, 1 - y)]

        def copy(a, k, block, to, src=None):
            slab = outs[a].at[lin(*block)]
            return pltpu.make_async_remote_copy(
                src_ref=slab if src is None else src, dst_ref=slab, send_sem=send_sems.at[a, k],
                recv_sem=recv_sems.at[a, k], device_id=to, device_id_type=pl.DeviceIdType.MESH)

        mine = [pltpu.make_async_copy(srcs[a], outs[a].at[lin(*me)], loc_sems.at[a]) for a in range(n)]
        first = [copy(a, 0, me, sibling, src=srcs[a]) for a in range(n)]
        first += [copy(a, 1 + j, me, (*chip, c), src=srcs[a]) for j, chip in enumerate(chips) for a in range(n)]
        for cp in mine + first:
            cp.start()
        passed = []
        for j, chip in enumerate(chips):
            for a in range(n):
                copy(a, 1 + j, (*chip, c), me).wait_recv()
                passed.append(copy(a, 4 + j, (*chip, c), sibling))
                passed[-1].start()
        for a in range(n):
            copy(a, 0, sibling, me).wait_recv()
            for j, chip in enumerate(chips):
                copy(a, 4 + j, (*chip, 1 - c), me).wait_recv()
        for cp in first + passed:
            cp.wait_send()
        for cp in mine:
            cp.wait()

    hbm = pl.BlockSpec(memory_space=pl.ANY)
    return pl.pallas_call(
        body, in_specs=[hbm] * n, out_specs=[hbm] * n, out_shape=[S((N_DEV,) + b.shape, b.dtype) for b in blocks],
        scratch_shapes=[pltpu.SemaphoreType.DMA((n, N_DEV - 1)), pltpu.SemaphoreType.DMA((n, N_DEV - 1)),
                        pltpu.SemaphoreType.DMA((n,))],
        name=name, compiler_params=pltpu.CompilerParams(has_side_effects=True))(*blocks)


def _xfer_start(items, name, after=None):
    n = len(items)
    kinds = [k for k, _ in items]
    srcs = [pltpu.with_memory_space_constraint(a, pltpu.HBM) for _, a in items]
    land_shapes = [((N_DEV,) + a.shape if k == 'gather' else a.shape, a.dtype) for k, a in items]
    lands = [pltpu.with_memory_space_constraint(lax.empty(s, dt), pltpu.HBM) for s, dt in land_shapes]
    extra = [] if after is None else [after]

    def body(*refs):
        src_refs, land_refs = refs[:n], refs[n:2 * n]
        outs = refs[2 * n + len(extra):]
        sems = outs[:2 * n]
        token = outs[4 * n]
        x, y, c = _mesh_pos()
        me = 4 * x + 2 * y + c
        for a in range(n):
            for d in range(1, N_DEV):
                px, py, pc = _peer(d, x, y, c)
                src = src_refs[a] if kinds[a] == 'gather' else src_refs[a].at[4 * px + 2 * py + pc]
                pltpu.make_async_remote_copy(
                    src_ref=src, dst_ref=land_refs[a].at[me], send_sem=sems[2 * a].at[d - 1],
                    recv_sem=sems[2 * a + 1].at[d - 1], device_id=(px, py, pc),
                    device_id_type=pl.DeviceIdType.MESH).start()
        token[...] = jnp.zeros_like(token)

    hbm = pl.BlockSpec(memory_space=pltpu.HBM)
    sem = pl.BlockSpec(memory_space=pltpu.SEMAPHORE)
    out_shape = ([pltpu.SemaphoreType.DMA((N_DEV - 1,))] * (2 * n)
                 + [pltpu.HBM(a.shape, a.dtype) for a in srcs] + [pltpu.HBM(s, dt) for s, dt in land_shapes]
                 + [S((8, 128), F32)])
    res = pl.pallas_call(
        body, name=name, out_shape=out_shape,
        in_specs=[hbm] * (2 * n) + [pl.BlockSpec(memory_space=pl.ANY)] * len(extra),
        out_specs=[sem] * (2 * n) + [hbm] * (2 * n) + [pl.BlockSpec(memory_space=pltpu.VMEM)],
        input_output_aliases={**{a: 2 * n + a for a in range(n)}, **{n + a: 3 * n + a for a in range(n)}},
        compiler_params=pltpu.CompilerParams(has_side_effects=pltpu.SideEffectType.DATAFLOW_SIDE_EFFECTING),
    )(*srcs, *lands, *extra)
    return (kinds, res[:2 * n], res[2 * n:3 * n], res[3 * n:4 * n]), res[4 * n]


def _xfer_wait(handle, after, name):
    kinds, sems, src_thru, land_thru = handle
    n = len(kinds)

    def body(*refs):
        land_refs = refs[n:2 * n]
        sem_refs = refs[2 * n:4 * n]
        x, y, c = _mesh_pos()
        me = 4 * x + 2 * y + c
        for a in range(n):
            for d in range(1, N_DEV):
                slab = land_refs[a].at[me]
                cp = pltpu.make_async_remote_copy(
                    src_ref=slab, dst_ref=slab, send_sem=sem_refs[2 * a].at[d - 1], recv_sem=sem_refs[2 * a + 1].at[d - 1],
                    device_id=_peer(d, x, y, c), device_id_type=pl.DeviceIdType.MESH)
                cp.wait_send()
                cp.wait_recv()

    hbm = pl.BlockSpec(memory_space=pltpu.HBM)
    sem = pl.BlockSpec(memory_space=pltpu.SEMAPHORE)
    res = pl.pallas_call(
        body, name=name,
        out_shape=[pltpu.HBM(a.shape, a.dtype) for a in src_thru] + [pltpu.HBM(a.shape, a.dtype) for a in land_thru],
        in_specs=[hbm] * (2 * n) + [sem] * (2 * n) + [pl.BlockSpec(memory_space=pl.ANY)],
        out_specs=[hbm] * (2 * n), input_output_aliases={a: a for a in range(2 * n)},
        compiler_params=pltpu.CompilerParams(has_side_effects=pltpu.SideEffectType.DATAFLOW_SIDE_EFFECTING),
    )(*src_thru, *land_thru, *sems, after)
    x, y, c = _mesh_pos()
    me = 4 * x + 2 * y + c
    out = []
    for a in range(n):
        src = res[a]
        own = src[None] if kinds[a] == 'gather' else lax.dynamic_index_in_dim(src, me, 0, keepdims=True)
        out.append(lax.dynamic_update_index_in_dim(res[n + a], own, me, 0))
    return out


def _stack_to_full(kind, st):
    if kind == 'row':
        return st.reshape(st.shape[0] * st.shape[1], st.shape[2])
    return jnp.concatenate([st[k] for k in range(st.shape[0])], axis=1)


def _full_to_stack(kind, full):
    r, c = full.shape
    if kind == 'row':
        return full.reshape(N_DEV, r // N_DEV, c)
    w = c // N_DEV
    return jnp.stack([full[:, k * w:(k + 1) * w] for k in range(N_DEV)], axis=0)


SMALL_ROWS = 256


def _pack_small(named):
    layout = [(a.shape, a.size, -(-a.size // 1024) * 8) for a in named]
    total = -(-sum(nr for _, _, nr in layout) // SMALL_ROWS) * SMALL_ROWS * 128
    packed, off = None, 0
    for a, (_, n, nr) in zip(named, layout):
        part = jnp.pad(a.reshape(-1).astype(F32), (off, total - off - n))
        packed = part if packed is None else packed + part
        off += nr * 128
    return packed.reshape(total // 128, 128), layout


def _unpack_small(packed, layout):
    out, r0 = [], 0
    for shape, n, nr in layout:
        out.append(packed[r0:r0 + nr].reshape(-1)[:n].reshape(shape))
        r0 += nr
    return out


def _row0(acc):
    return acc[0]


def _local_step(x, p, tgt, sm, comm):
    t, d = x.shape
    h_n = sm['dt_bias'].shape[-1]
    ngrp = h_n // HEADS_PER_GROUP
    d_ssm = h_n * HEAD_DIM
    d_a = sm['ln_a_g'].shape[-1]
    d_mix = d_a + d_ssm
    d_xbc = sm['conv_ssm_b'].shape[-1]
    d_main = 2 * d_a + d_ssm + d_xbc
    assert d_xbc == d_ssm + 2 * ngrp * D_STATE and h_n <= 128
    zcol0, xcol0 = 2 * d_a, 2 * d_a + d_ssm
    vec = lambda v: v.reshape(1, -1)

    bst = jnp.pad(sm['b_s'].T, ((0, 0), (0, 128 - sm['b_s'].shape[0])))
    grp = lambda v, w: jnp.broadcast_to(jnp.pad(v.reshape(ngrp, 1, -1), ((0, 0), (0, 0), (0, w - v.size // ngrp))), (ngrp, 8, w))
    bias_p, alog_p = grp(sm['dt_bias'], 128), grp(sm['a_log'], 128)
    dskip_x = grp(jnp.repeat(sm['d_skip'], HEAD_DIM), GROUP_CH)
    normg_x = grp(sm['ssm_norm_g'], GROUP_CH)
    pad_dt = lambda v: jnp.pad(v[:, :h_n].reshape(t, ngrp, HEADS_PER_GROUP),
                               ((0, 0), (0, 0), (0, 128 - HEADS_PER_GROUP))).reshape(t, ngrp * 128)

    g_mix = vec(sm['norm_mix_g']) + comm.tok0
    a_n, a_t = _rms_fwd(x, g_mix, "rms_mix")
    wf = comm.weights('a', a_n)
    slabs = [wf['w_in'][k] for k in range(N_DEV)]
    w_main = jnp.concatenate(slabs[:-1] + [slabs[-1][:, :slabs[-1].shape[1] - h_n]], axis=1)
    w_dt = jnp.pad(slabs[-1][:, slabs[-1].shape[1] - h_n:], ((0, 0), (0, 128 - h_n)))
    proj = _mm_nn(a_n, w_main, out_dtype=ACT_DTYPE, name="mm_in")
    dtp = pad_dt(_mm_nn(a_n, w_dt, out_dtype=F32, name="mm_dt"))
    cat = _gmlp_fwd(proj, vec(sm['ln_a_g']), vec(sm['ln_a_b']), sm['w_s'], bst, vec(sm['norm_a_g']), d_mix, "gmlp_fwd")
    xbc, cpre = _conv_ssm_fwd(proj, xcol0, wf['conv_ssm_w'], vec(sm['conv_ssm_b']), "conv_ssm_fwd")
    cat, hs = _ssd_fwd(xbc, dtp, proj, zcol0, bias_p, alog_p, dskip_x, normg_x, cat, "ssd_fwd")
    wf.update(comm.weights('b', hs))
    h1 = _mm_nn(cat, wf['w_out'], out_dtype=F32, name="mm_out", res=x)
    f_n, f_t = _rms_fwd(h1, vec(sm['norm_ffn_g']), "rms_ffn")
    hid = _mm_nn(f_n, wf['w_up'], out_dtype=ACT_DTYPE, name="mm_up")
    act, act_t, cv = _conv_ffn_fwd(hid, wf['conv_ffn_w'], vec(sm['conv_ffn_b']), "conv_ffn_fwd")
    h2 = _mm_nn(act, wf['w_down'], out_dtype=F32, name="mm_down", res=h1)
    r_n, r_t = _rms_fwd(h2, vec(sm['norm_ple_g']), "rms_ple")
    q = _mm_nn(r_n, wf['w_ple_gate'], out_dtype=ACT_DTYPE, name="mm_pg")
    p_m = p.astype(MXU_DTYPE)
    pe = _mm_nn(p_m, wf['w_ple'], out_dtype=ACT_DTYPE, name="mm_ple")

    loss, dh3, dq, dpe, dgf = _head(h2, q, pe, tgt, vec(sm['norm_final_g']), "head")
    wgrad = lambda act_t, g, name, **kw: _mm_nn(act_t, g, out_dtype=WIRE_DTYPE, name=name, wide=True, **kw)
    gs = {}
    gs['norm_final_g'] = _row0(dgf)
    g_ple = wgrad(p_m.T, dpe, "wg_ple")
    g_pg = wgrad(r_t, dq, "wg_pg")
    dr = _mm_nt(dq, wf['w_ple_gate'], out_dtype=ACT_DTYPE, name="dg_pg")
    dh2, dh2m, dg = _rms_bwd(h2, vec(sm['norm_ple_g']), dr, dh3, "rms_ple_bwd")
    gs['norm_ple_g'] = _row0(dg)
    g_down = wgrad(act_t, dh2m, "wg_down")
    tok = comm.send('1', {'w_ple': g_ple, 'w_ple_gate': g_pg, 'w_down': g_down})
    dact = _mm_nt(dh2m, wf['w_down'], out_dtype=ACT_DTYPE, name="dg_down")
    dhid, dcw, dcb = _conv_ffn_bwd(hid, wf['conv_ffn_w'] + tok, cv, dact, "conv_ffn_bwd")
    kf = wf['conv_ffn_w'].shape[0]
    g_cf = jnp.concatenate([dcw[0, :kf], dcw[1, :kf]], axis=1)
    gs['conv_ffn_b'] = jnp.concatenate([dcb[0, 0], dcb[1, 0]], axis=0)
    g_up = wgrad(f_t, dhid, "wg_up", b_split=2, out_slabs=N_DEV)
    df = _mm_nt(dhid, wf['w_up'], out_dtype=ACT_DTYPE, name="dg_up", a_split=2)
    dh1, dh1m, dg = _rms_bwd(h1, vec(sm['norm_ffn_g']), df, dh2, "rms_ffn_bwd")
    gs['norm_ffn_g'] = _row0(dg)
    g_out = wgrad(cat.T, dh1m, "wg_out")
    tok = comm.send('2', {'conv_ffn_w': g_cf, 'w_up': g_up, 'w_out': g_out}, stacked=('w_up',))
    dcat = _mm_nt(dh1m, wf['w_out'], out_dtype=ACT_DTYPE, name="dg_out")
    dproj, dlng, dlnb, dws, dbst, dng = _gmlp_bwd(proj, vec(sm['ln_a_g']) + tok, vec(sm['ln_a_b']), sm['w_s'], bst,
                                                  vec(sm['norm_a_g']), dcat, d_main, "gmlp_bwd")
    gs['ln_a_g'], gs['ln_a_b'], gs['w_s'], gs['norm_a_g'] = _row0(dlng), _row0(dlnb), dws, _row0(dng)
    gs['b_s'] = dbst[:, :sm['b_s'].shape[0]].T
    dproj, dxs, dbm, dcm, ddtp, dbias, dalog, ddsk, dsng = _ssd_bwd(
        xbc, dtp, proj, zcol0, bias_p, alog_p, dskip_x, normg_x, hs, dcat, dproj, "ssd_bwd")
    gs['dt_bias'] = dbias[:, 0, :HEADS_PER_GROUP].reshape(h_n)
    gs['a_log'] = dalog[:, 0, :HEADS_PER_GROUP].reshape(h_n)
    gs['d_skip'] = ddsk[:, 0, :].reshape(h_n, HEAD_DIM).sum(axis=-1)
    gs['ssm_norm_g'] = dsng[:, 0, :].reshape(d_ssm)
    dws_c, dbs_c = [], []
    off = 0
    for nm, dpart in (("x", dxs), ("b", dbm), ("c", dcm)):
        dproj, dw_c, db_c = _conv_ssm_bwd(proj, xcol0 + off, wf['conv_ssm_w'], cpre, off, dpart, dproj,
                                          xcol0 + off, "conv_ssm_bwd_" + nm)
        dws_c.append(dw_c[:wf['conv_ssm_w'].shape[0]])
        dbs_c.append(db_c[0])
        off += dpart.shape[1]
    g_cs = jnp.concatenate(dws_c, axis=1)
    gs['conv_ssm_b'] = jnp.concatenate(dbs_c, axis=0)
    ddt = jnp.pad(ddtp.reshape(t, ngrp, 128)[:, :, :HEADS_PER_GROUP].reshape(t, h_n), ((0, 0), (0, 128 - h_n))).astype(MXU_DTYPE)
    g_in = jnp.concatenate([wgrad(a_t, dproj, "wg_in"), wgrad(a_t, ddt, "wg_dt")[:, :h_n]], axis=1)
    tok = comm.send('3', {'conv_ssm_w': g_cs, 'w_in': g_in}, [(n, gs[n]) for n in REPLICATED if n != 'norm_mix_g'])
    da = _mm_nt(ddt + tok.astype(ddt.dtype), w_dt, out_dtype=F32, name="dg_dt")
    da = _mm_nt(dproj, w_main, out_dtype=ACT_DTYPE, name="dg_in", res=da)
    dx, dg = _rms_bwd(x, g_mix + tok, da, dh1, "rms_mix_bwd", mx_copy=False)
    comm.send('4', {}, [('norm_mix_g', _row0(dg)), ('loss', loss[0, 0:1])])
    return dx


def kernel(x, p, norm_mix_g, w_in, ln_a_g, ln_a_b, w_s, b_s, norm_a_g, conv_ssm_w, conv_ssm_b, dt_bias, a_log, d_skip, ssm_norm_g, w_out, norm_ffn_g, w_up, conv_ffn_w, conv_ffn_b, w_down, norm_ple_g, w_ple_gate, w_ple, norm_final_g, loss_target, m_norm_mix_g, m_w_in, m_ln_a_g, m_ln_a_b, m_w_s, m_b_s, m_norm_a_g, m_conv_ssm_w, m_conv_ssm_b, m_dt_bias, m_a_log, m_d_skip, m_ssm_norm_g, m_w_out, m_norm_ffn_g, m_w_up, m_conv_ffn_w, m_conv_ffn_b, m_w_down, m_norm_ple_g, m_w_ple_gate, m_w_ple, m_norm_final_g, v_norm_mix_g, v_w_in, v_ln_a_g, v_ln_a_b, v_w_s, v_b_s, v_norm_a_g, v_conv_ssm_w, v_conv_ssm_b, v_dt_bias, v_a_log, v_d_skip, v_ssm_norm_g, v_w_out, v_norm_ffn_g, v_w_up, v_conv_ffn_w, v_conv_ffn_b, v_w_down, v_norm_ple_g, v_w_ple_gate, v_w_ple, v_norm_final_g):
    given = dict(locals())
    wts = {n: given[n] for n in WEIGHTS}
    ms = {n: given["m_" + n] for n in WEIGHTS}
    vs = {n: given["v_" + n] for n in WEIGHTS}
    sm = {n: (wts[n][0] if wts[n].ndim > 1 else wts[n]) for n in REPLICATED}
    comm = _Comm({n: wts[n][0] for n in SHARDED})
    dx = _local_step(x[0], p[0, 0], loss_target[0], sm, comm)

    out, loss_out, after = {}, None, dx
    for tag, names, small_names, layout, handle in comm.sent:
        recv = _xfer_wait(handle, after, "grads_%s_wait" % tag)
        for n, parts in zip(names, recv):
            out[n] = _adamw(parts, wts[n][0], ms[n][0], vs[n][0], "adamw_" + n)
            after = out[n][1]
        if small_names:
            pick = lambda src, fill: _pack_small([src[n] if n in src else jnp.full((1,), fill, F32) for n in small_names])[0]
            res = _adamw(recv[-1], pick(wts, 0.0), pick(ms, 0.0), pick(vs, 1.0), "adamw_small_" + tag)
            res = [_unpack_small(o, layout) for o in res]
            after = res[1][0]
            for i, n in enumerate(small_names):
                if n == 'loss':
                    loss_out = res[0][i].reshape(())
                else:
                    out[n] = [res[k][i] for k in range(4)]
    return (loss_out, dx[None], *[out[n][k].reshape(wts[n].shape) for k in range(4) for n in WEIGHTS])


class _Comm:
    GATHER_GROUPS = {'a': ('w_in', 'conv_ssm_w'), 'b': ('w_out', 'w_up', 'conv_ffn_w', 'w_down', 'w_ple_gate', 'w_ple')}

    def __init__(self, blocks):
        wired = lambda grp: [blocks[n].astype(_wire(n)) for n in self.GATHER_GROUPS[grp]]
        self.stacks_a = _gather_two_level(wired('a'), "gather_a")
        self.handle_b, tok = _xfer_start([('gather', b) for b in wired('b')], "gather_b_start", after=self.stacks_a[0])
        self.tok0 = tok[0, 0]
        self.sent = []

    def weights(self, grp, after):
        stacks = self.stacks_a if grp == 'a' else _xfer_wait(self.handle_b, after, "gather_b_wait")
        return {n: st if n == 'w_in' else _stack_to_full(SHARDED[n], st) for n, st in zip(self.GATHER_GROUPS[grp], stacks)}

    def send(self, tag, gw, small=None, stacked=()):
        items = [('scatter', g if n in stacked else _full_to_stack(SHARDED[n], g.astype(_wire(n)))) for n, g in gw.items()]
        layout, small_names = None, []
        if small:
            packed, layout = _pack_small([a for _, a in small])
            small_names = [n for n, _ in small]
            items.append(('gather', packed))
        handle, tok = _xfer_start(items, "grads_%s_start" % tag)
        self.sent.append((tag, list(gw), small_names, layout, handle))
        return tok[0, 0]


def _wire(name):
    return F32 if name in F32_ON_WIRE else WIRE_DTYPE
```

```python
import functools

import jax
import jax.numpy as jnp
from jax import lax
from jax.experimental import pallas as pl
from jax.experimental.pallas import tpu as pltpu

F32 = jnp.float32
MXU_DTYPE = jnp.bfloat16
ACT_DTYPE = jnp.bfloat16
XBC_DTYPE = jnp.bfloat16
WIRE_DTYPE = jnp.bfloat16
EPS = 1e-6
CHUNK = 128
D_STATE = 128
HEAD_DIM = 64
HEADS_PER_GROUP = 4
GROUP_CH = HEAD_DIM * HEADS_PER_GROUP
HALO = 16
N_DEV = 8
VMEM_LIMIT = 48 * 1024 * 1024

ADAM_LR = 0.001
ADAM_B1 = 0.9
ADAM_B2 = 0.999
ADAM_EPS = 1e-08
ADAM_WD = 0.01
ADAM_STEP = 10

WEIGHTS = ['norm_mix_g', 'w_in', 'ln_a_g', 'ln_a_b', 'w_s', 'b_s', 'norm_a_g', 'conv_ssm_w', 'conv_ssm_b', 'dt_bias',
           'a_log', 'd_skip', 'ssm_norm_g', 'w_out', 'norm_ffn_g', 'w_up', 'conv_ffn_w', 'conv_ffn_b', 'w_down',
           'norm_ple_g', 'w_ple_gate', 'w_ple', 'norm_final_g']
SHARDED = {'w_in': 'col', 'conv_ssm_w': 'col', 'w_out': 'row', 'w_up': 'col', 'conv_ffn_w': 'col', 'w_down': 'row',
           'w_ple_gate': 'row', 'w_ple': 'col'}
F32_ON_WIRE = ('conv_ssm_w', 'conv_ffn_w')
REPLICATED = [n for n in WEIGHTS if n not in SHARDED]

S = jax.ShapeDtypeStruct


def _pick(dim, cands):
    for c in cands:
        if c <= dim and dim % c == 0:
            return c
    return dim


def _cp(sem, vmem=VMEM_LIMIT):
    return pltpu.CompilerParams(dimension_semantics=sem, vmem_limit_bytes=vmem)


def _mx(v):
    return v.astype(MXU_DTYPE)


def _rms(v, g):
    return v * lax.rsqrt(jnp.mean(v * v, axis=-1, keepdims=True) + EPS) * g


MM_VMEM_BUDGET = 34 * 1024 * 1024


def _mm_tiles(m, n, k, out_bytes, has_res, tn_cands=(512, 256, 128), k_mult=1, tm_cands=(1024, 512)):
    tn = _pick(n, tn_cands)
    ks = k // k_mult
    best = None
    for tm in [c for c in tm_cands if m % c == 0] or [_pick(m, (256, 128))]:
        for nk in range(1, ks // 128 + 1):
            if ks % nk or (ks // nk) % 128:
                continue
            tk = ks // nk
            need = 2 * 2 * (tm * tk + tk * tn) + tm * tn * (4 + 2 * out_bytes + (8 if has_res else 0))
            if need <= MM_VMEM_BUDGET:
                if best is None or (nk, -tm) < best[0]:
                    best = ((nk, -tm), (tm, tn, tk))
                break
    return best[1] if best else (_pick(m, (512, 256, 128)), tn, _pick(ks, (128,)))


def _mm_body(dot, nk, has_res):
    def body(*refs):
        if has_res:
            a_ref, b_ref, r_ref, o_ref, acc_ref = refs
        else:
            a_ref, b_ref, o_ref, acc_ref = refs
            r_ref = None
        kk = pl.program_id(2)
        d = dot(a_ref[...], b_ref[...])

        def fin(acc):
            if r_ref is not None:
                acc = acc + r_ref[...]
            o_ref[...] = acc.astype(o_ref.dtype)

        if nk == 1:
            fin(d)
        else:
            @pl.when(kk == 0)
            def _():
                acc_ref[...] = d

            @pl.when(kk > 0)
            def _():
                acc_ref[...] += d

            @pl.when(kk == nk - 1)
            def _():
                fin(acc_ref[...])

    return body


def _mm_call(body, grid, a_spec, b_spec, tm, tn, m, n, out_dtype, name, args, res, out_slabs=1):
    in_specs = [a_spec, b_spec]
    if res is not None:
        in_specs.append(pl.BlockSpec((tm, tn), lambda i, j, kk: (i, j)))
        args = args + [res]
    if out_slabs == 1:
        out_spec, out_shape = pl.BlockSpec((tm, tn), lambda i, j, kk: (i, j)), S((m, n), out_dtype)
    else:
        out_spec, out_shape = pl.BlockSpec((None, tm, tn), lambda i, j, kk: (j, i, 0)), S((out_slabs, m, tn), out_dtype)
    return pl.pallas_call(
        body, grid=grid, in_specs=in_specs, out_specs=out_spec, out_shape=out_shape,
        scratch_shapes=[pltpu.VMEM((tm, tn), F32)], name=name,
        compiler_params=_cp(("parallel", "parallel", "arbitrary")))(*args)


def _mm_nn(a, b, *, out_dtype, name, res=None, b_split=1, wide=False, out_slabs=1):
    m, k = a.shape
    n = b.shape[1] if b_split == 1 else b.shape[2] * b_split
    tn_cands = (n // out_slabs,) if out_slabs > 1 else (1024, 512, 256, 128) if wide else (512, 256, 128)
    tm, tn, tk = _mm_tiles(m, n // b_split, k, jnp.dtype(out_dtype).itemsize, res is not None, tn_cands=tn_cands)
    assert out_slabs == 1 or (tn * out_slabs == n and tn % 128 == 0)
    nk = k // tk
    njs = (n // b_split) // tn
    body = _mm_body(lambda x, y: jnp.dot(x, y, preferred_element_type=F32), nk, res is not None)
    if b_split == 1:
        b_spec = pl.BlockSpec((tk, tn), lambda i, j, kk: (kk, j))
    else:
        b_spec = pl.BlockSpec((None, tk, tn), lambda i, j, kk: (j // njs, kk, j % njs))
    return _mm_call(body, (m // tm, n // tn, nk), pl.BlockSpec((tm, tk), lambda i, j, kk: (i, kk)), b_spec,
                    tm, tn, m, n, out_dtype, name, [a, b], res, out_slabs)


def _mm_nt(a, b, *, out_dtype, name, res=None, a_split=1):
    if a_split == 1:
        m, k = a.shape
    else:
        m, k = a.shape[1], a.shape[2] * a_split
    n = b.shape[0]
    tm, tn, tk = _mm_tiles(m, n, k, jnp.dtype(out_dtype).itemsize, res is not None, k_mult=a_split, tm_cands=(1024,))
    nk = k // tk
    nks = nk // a_split
    body = _mm_body(lambda x, y: lax.dot_general(x, y, (((1,), (1,)), ((), ())), preferred_element_type=F32),
                    nk, res is not None)
    if a_split == 1:
        a_spec = pl.BlockSpec((tm, tk), lambda i, j, kk: (i, kk))
    else:
        a_spec = pl.BlockSpec((None, tm, tk), lambda i, j, kk: (kk // nks, i, kk % nks))
    return _mm_call(body, (m // tm, n // tn, nk), a_spec, pl.BlockSpec((tn, tk), lambda i, j, kk: (j, kk)),
                    tm, tn, m, n, out_dtype, name, [a, b], res)


def _rms_fwd(x, g, name):
    t, d = x.shape
    tr = _pick(t, (512, 256, 128))

    def body(x_ref, g_ref, o_ref, ot_ref):
        y = _rms(x_ref[...], g_ref[...]).astype(o_ref.dtype)
        o_ref[...] = y
        ot_ref[...] = y.T

    return pl.pallas_call(
        body, grid=(t // tr,),
        in_specs=[pl.BlockSpec((tr, d), lambda i: (i, 0)), pl.BlockSpec((1, d), lambda i: (0, 0))],
        out_specs=[pl.BlockSpec((tr, d), lambda i: (i, 0)), pl.BlockSpec((d, tr), lambda i: (0, i))],
        out_shape=[S((t, d), ACT_DTYPE), S((d, t), ACT_DTYPE)], name=name,
        compiler_params=_cp(("parallel",)))(x, g)


def _rms_bwd(xin, g, dn, dres, name, mx_copy=True):
    t, d = xin.shape
    tr = _pick(t, (256, 128))

    def body(x_ref, g_ref, dn_ref, dr_ref, dx_ref, *rest):
        dg_ref = rest[-1]

        @pl.when(pl.program_id(0) == 0)
        def _():
            dg_ref[...] = jnp.zeros_like(dg_ref)

        _, vjp = jax.vjp(_rms, x_ref[...], g_ref[...])
        dx, dg = vjp(dn_ref[...].astype(F32))
        dx = dr_ref[...] + dx
        dx_ref[...] = dx
        if mx_copy:
            rest[0][...] = dx.astype(rest[0].dtype)
        dg_ref[0:1, :] += dg

    row = pl.BlockSpec((tr, d), lambda i: (i, 0))
    acc = pl.BlockSpec((8, d), lambda i: (0, 0))
    return pl.pallas_call(
        body, grid=(t // tr,),
        in_specs=[row, pl.BlockSpec((1, d), lambda i: (0, 0)), row, row],
        out_specs=[row, row, acc] if mx_copy else [row, acc],
        out_shape=[S((t, d), F32)] + ([S((t, d), MXU_DTYPE)] if mx_copy else []) + [S((8, d), F32)], name=name,
        compiler_params=_cp(("arbitrary",)))(xin, g, dn, dres)


def _head(h2, q, pe, tgt, gf, name):
    t, d = h2.shape
    tr = _pick(t, (256, 128))

    def f(h2v, qv, pev, gfv, tv):
        h3 = h2v + jax.nn.sigmoid(qv) * pev
        y = _rms(h3, gfv)
        return 0.5 * jnp.sum(jnp.mean(jnp.square(y - tv), axis=-1))

    def body(h2_ref, q_ref, pe_ref, t_ref, g_ref, loss_ref, dh_ref, dq_ref, dpe_ref, dg_ref):
        @pl.when(pl.program_id(0) == 0)
        def _():
            loss_ref[...] = jnp.zeros_like(loss_ref)
            dg_ref[...] = jnp.zeros_like(dg_ref)

        tv = t_ref[...]
        loss, vjp = jax.vjp(lambda a, b, c, e: f(a, b, c, e, tv), h2_ref[...], q_ref[...].astype(F32),
                            pe_ref[...].astype(F32), g_ref[...])
        dh, dq, dpe, dg = vjp(jnp.ones((), F32))
        loss_ref[...] += jnp.full(loss_ref.shape, loss, F32)
        dh_ref[...] = dh
        dq_ref[...] = dq.astype(dq_ref.dtype)
        dpe_ref[...] = dpe.astype(dpe_ref.dtype)
        dg_ref[0:1, :] += dg

    row = pl.BlockSpec((tr, d), lambda i: (i, 0))
    return pl.pallas_call(
        body, grid=(t // tr,),
        in_specs=[row, row, row, row, pl.BlockSpec((1, d), lambda i: (0, 0))],
        out_specs=[pl.BlockSpec((8, 128), lambda i: (0, 0)), row, row, row, pl.BlockSpec((8, d), lambda i: (0, 0))],
        out_shape=[S((8, 128), F32), S((t, d), F32), S((t, d), MXU_DTYPE), S((t, d), MXU_DTYPE), S((8, d), F32)],
        name=name, compiler_params=_cp(("arbitrary",)))(h2, q, pe, tgt, gf)


def _gmlp_block(us, vs, lng, lnb, wss, bss, ng):
    n = us[0].shape[0]
    row = lax.broadcasted_iota(jnp.int32, (n, n), 0)
    col = lax.broadcasted_iota(jnp.int32, (n, n), 1)
    outs = []
    for u0, v0, lg, lb, ws, bs in zip(us, vs, lng, lnb, wss, bss):
        u = jax.nn.gelu(u0)
        v = jax.nn.gelu(v0)
        mu = jnp.mean(v, axis=-1, keepdims=True)
        var = jnp.mean(jnp.square(v - mu), axis=-1, keepdims=True)
        vn = (v - mu) * lax.rsqrt(var + EPS) * lg + lb
        w = jnp.where(row >= col, ws, 0.0)
        sg = jnp.dot(_mx(w), _mx(vn), preferred_element_type=F32) + bs
        outs.append(u * sg)
    return _rms(jnp.concatenate(outs, axis=1), ng)


def _gmlp_load(proj_ref, lng_ref, lnb_ref, ws_ref, bst_ref, d_a, ng):
    sl = lambda g: slice(CHUNK * g, CHUNK * (g + 1))
    us = [proj_ref[:, sl(g)].astype(F32) for g in range(ng)]
    vs = [proj_ref[:, d_a + CHUNK * g: d_a + CHUNK * (g + 1)].astype(F32) for g in range(ng)]
    lng = [lng_ref[:, sl(g)] for g in range(ng)]
    lnb = [lnb_ref[:, sl(g)] for g in range(ng)]
    wss = [ws_ref[g] for g in range(ng)]
    bss = [bst_ref[:, g:g + 1] for g in range(ng)]
    return us, vs, lng, lnb, wss, bss


def _gmlp_fwd(proj, ln_g, ln_b, w_s, bst, norm_g, d_mix, name):
    t = proj.shape[0]
    ng = w_s.shape[0]
    d_a = ng * CHUNK

    def body(proj_ref, lng_ref, lnb_ref, ws_ref, bst_ref, ng_ref, o_ref, ot_ref):
        args = _gmlp_load(proj_ref, lng_ref, lnb_ref, ws_ref, bst_ref, d_a, ng)
        y = _gmlp_block(*args, ng_ref[...]).astype(o_ref.dtype)
        o_ref[...] = y
        ot_ref[...] = y.T

    vec = pl.BlockSpec((1, d_a), lambda c: (0, 0))
    return pl.pallas_call(
        body, grid=(t // CHUNK,),
        in_specs=[pl.BlockSpec((CHUNK, 2 * d_a), lambda c: (c, 0)), vec, vec,
                  pl.BlockSpec((ng, CHUNK, CHUNK), lambda c: (0, 0, 0)), pl.BlockSpec((CHUNK, 128), lambda c: (0, 0)), vec],
        out_specs=[pl.BlockSpec((CHUNK, d_a), lambda c: (c, 0)), pl.BlockSpec((d_a, CHUNK), lambda c: (0, c))],
        out_shape=[S((t, d_mix), ACT_DTYPE), S((d_mix, t), ACT_DTYPE)],
        name=name, compiler_params=_cp(("parallel",)))(proj, ln_g, ln_b, w_s, bst, norm_g)


def _gmlp_bwd(proj, ln_g, ln_b, w_s, bst, norm_g, dcat, d_proj, name):
    t = proj.shape[0]
    ng = w_s.shape[0]
    d_a = ng * CHUNK

    def body(proj_ref, lng_ref, lnb_ref, ws_ref, bst_ref, ng_ref, dy_ref,
             dp_ref, dlng_ref, dlnb_ref, dws_ref, dbst_ref, dng_ref):
        @pl.when(pl.program_id(0) == 0)
        def _():
            for r in (dlng_ref, dlnb_ref, dws_ref, dbst_ref, dng_ref):
                r[...] = jnp.zeros_like(r)

        args = _gmlp_load(proj_ref, lng_ref, lnb_ref, ws_ref, bst_ref, d_a, ng)
        _, vjp = jax.vjp(_gmlp_block, *args, ng_ref[...])
        dus, dvs, dlng, dlnb, dwss, dbss, dng = vjp(dy_ref[...].astype(F32))
        lane = lax.broadcasted_iota(jnp.int32, (1, 128), 1)
        dbst = jnp.zeros((CHUNK, 128), F32)
        for g in range(ng):
            dp_ref[:, CHUNK * g:CHUNK * (g + 1)] = dus[g].astype(dp_ref.dtype)
            dp_ref[:, d_a + CHUNK * g:d_a + CHUNK * (g + 1)] = dvs[g].astype(dp_ref.dtype)
            dlng_ref[0:1, CHUNK * g:CHUNK * (g + 1)] += dlng[g]
            dlnb_ref[0:1, CHUNK * g:CHUNK * (g + 1)] += dlnb[g]
            dws_ref[g] += dwss[g]
            dbst = dbst + dbss[g] * (lane == g).astype(F32)
        dbst_ref[...] += dbst
        dng_ref[0:1, :] += dng

    vec = pl.BlockSpec((1, d_a), lambda c: (0, 0))
    acc = pl.BlockSpec((8, d_a), lambda c: (0, 0))
    wspec = pl.BlockSpec((ng, CHUNK, CHUNK), lambda c: (0, 0, 0))
    bspec = pl.BlockSpec((CHUNK, 128), lambda c: (0, 0))
    return pl.pallas_call(
        body, grid=(t // CHUNK,),
        in_specs=[pl.BlockSpec((CHUNK, 2 * d_a), lambda c: (c, 0)), vec, vec, wspec, bspec, vec,
                  pl.BlockSpec((CHUNK, d_a), lambda c: (c, 0))],
        out_specs=[pl.BlockSpec((CHUNK, 2 * d_a), lambda c: (c, 0)), acc, acc, wspec, bspec, acc],
        out_shape=[S((t, d_proj), ACT_DTYPE), S((8, d_a), F32), S((8, d_a), F32), S((ng, CHUNK, CHUNK), F32),
                   S((CHUNK, 128), F32), S((8, d_a), F32)],
        name=name, compiler_params=_cp(("arbitrary",)))(proj, ln_g, ln_b, w_s, bst, norm_g, dcat)


def _silu_grad(c):
    s = jax.nn.sigmoid(c)
    return s * (1.0 + c * (1.0 - s))


def _fill_prev_main(s_ref, prev_ref, main_ref, i, tt):
    s_ref[pl.ds(0, HALO), :] = jnp.where(i > 0, prev_ref[...].astype(F32), 0.0)
    s_ref[pl.ds(HALO, tt), :] = main_ref[...].astype(F32)


def _prev_spec(tt, tc, joff):
    return pl.BlockSpec((HALO, tc), lambda j, i: (jnp.maximum(i * (tt // HALO) - 1, 0), j + joff))


def _next_spec(tt, tc, joff, t):
    return pl.BlockSpec((HALO, tc), lambda j, i: (jnp.minimum((i + 1) * (tt // HALO), t // HALO - 1), j + joff))


CONV_RC = 32


def _conv_tiles(t, c):
    return _pick(t, (1024, 512, 256, 128)), _pick(c, (256, 128))


def _row_chunks(tt, fn, init=0):
    rc = min(CONV_RC, tt)
    return lax.fori_loop(0, tt // rc, lambda q, c: fn(pl.multiple_of(q * rc, rc), rc, c), init)


def _fold8(p):
    acc = p[0:8]
    for r in range(8, p.shape[0], 8):
        acc = acc + p[r:r + 8]
    return acc


def _taps_chunk(s_ref, w_ref, kw, r0, rc):
    xe = s_ref[pl.ds(HALO - 8 + r0, rc + 8), :]
    acc = w_ref[0:1, :] * xe[8 - (kw - 1):8 - (kw - 1) + rc]
    for k in range(1, kw):
        acc = acc + w_ref[k:k + 1, :] * xe[8 - (kw - 1) + k:8 - (kw - 1) + k + rc]
    return acc


def _conv_bwd_chunk(sd_ref, x, w_ref, kw, r0, rc, dws):
    de = sd_ref[pl.ds(r0, rc + 8), :]
    dx, out = None, list(dws)
    for j in range(kw):
        d = de[j:j + rc]
        k = kw - 1 - j
        term = w_ref[k:k + 1, :] * d
        dx = term if dx is None else dx + term
        out[k] = out[k] + _fold8(x * d)
    return dx, out


def _conv_ssm_fwd(proj, col0, w, b, name):
    t = proj.shape[0]
    kw, c = w.shape
    tt, tc = _conv_tiles(t, c)
    joff = col0 // tc
    assert col0 % tc == 0

    def body(x_ref, xp_ref, w_ref, b_ref, o_ref, c_ref, s_ref):
        _fill_prev_main(s_ref, xp_ref, x_ref, pl.program_id(1), tt)

        def chunk(r0, rc, carry):
            cpre = _taps_chunk(s_ref, w_ref, kw, r0, rc) + b_ref[...]
            o_ref[pl.ds(r0, rc), :] = jax.nn.silu(cpre).astype(o_ref.dtype)
            c_ref[pl.ds(r0, rc), :] = cpre.astype(c_ref.dtype)
            return carry

        _row_chunks(tt, chunk)

    out = pl.BlockSpec((tt, tc), lambda j, i: (i, j))
    return pl.pallas_call(
        body, grid=(c // tc, t // tt),
        in_specs=[pl.BlockSpec((tt, tc), lambda j, i: (i, j + joff)), _prev_spec(tt, tc, joff),
                  pl.BlockSpec((kw, tc), lambda j, i: (0, j)), pl.BlockSpec((1, tc), lambda j, i: (0, j))],
        out_specs=[out, out], out_shape=[S((t, c), XBC_DTYPE), S((t, c), ACT_DTYPE)],
        scratch_shapes=[pltpu.VMEM((HALO + tt, tc), F32)], name=name,
        compiler_params=_cp(("parallel", "arbitrary")))(proj, proj, w, b)


def _conv_ssm_bwd(proj, col0, w, cpre, wcol0, dact, dproj, out_col0, name):
    t = proj.shape[0]
    kw = w.shape[0]
    c = dact.shape[1]
    tt, tc = _conv_tiles(t, c)
    assert col0 % tc == 0 and wcol0 % tc == 0 and out_col0 % tc == 0
    joff, wj, oj = col0 // tc, wcol0 // tc, out_col0 // tc
    nt = t // tt

    def body(x_ref, w_ref, c_ref, cn_ref, d_ref, dn_ref, dp_in, dx_ref, dw_ref, db_ref, sd_ref):
        del dp_in
        i = pl.program_id(1)

        @pl.when(i == 0)
        def _():
            dw_ref[...] = jnp.zeros_like(dw_ref)
            db_ref[...] = jnp.zeros_like(db_ref)

        def stage(r0, rc, db):
            rows = pl.ds(r0, rc)
            d = d_ref[rows, :].astype(F32) * _silu_grad(c_ref[rows, :].astype(F32))
            sd_ref[rows, :] = d
            return db + _fold8(d)

        zero8 = jnp.zeros((8, tc), F32)
        db = _row_chunks(tt, stage, zero8)
        sd_ref[pl.ds(tt, HALO), :] = jnp.where(
            i < nt - 1, dn_ref[...].astype(F32) * _silu_grad(cn_ref[...].astype(F32)), 0.0)

        def chunk(r0, rc, dws):
            dx, dws = _conv_bwd_chunk(sd_ref, x_ref[pl.ds(r0, rc), :].astype(F32), w_ref, kw, r0, rc, dws)
            dx_ref[pl.ds(r0, rc), :] = dx.astype(dx_ref.dtype)
            return dws

        dws = _row_chunks(tt, chunk, [zero8] * kw)
        for k in range(kw):
            dw_ref[k:k + 1, :] += jnp.sum(dws[k], axis=0, keepdims=True)
        db_ref[0:1, :] += jnp.sum(db, axis=0, keepdims=True)

    acc = pl.BlockSpec((8, tc), lambda j, i: (0, j))
    return pl.pallas_call(
        body, grid=(c // tc, nt),
        in_specs=[pl.BlockSpec((tt, tc), lambda j, i: (i, j + joff)), pl.BlockSpec((kw, tc), lambda j, i: (0, j + wj)),
                  pl.BlockSpec((tt, tc), lambda j, i: (i, j + wj)), _next_spec(tt, tc, wj, t),
                  pl.BlockSpec((tt, tc), lambda j, i: (i, j)), _next_spec(tt, tc, 0, t),
                  pl.BlockSpec(memory_space=pl.ANY)],
        out_specs=[pl.BlockSpec((tt, tc), lambda j, i: (i, j + oj)), acc, acc],
        out_shape=[S(dproj.shape, dproj.dtype), S((8, c), F32), S((8, c), F32)],
        scratch_shapes=[pltpu.VMEM((tt + HALO, tc), F32)],
        input_output_aliases={6: 0}, name=name,
        compiler_params=_cp(("parallel", "arbitrary")))(proj, w, cpre, cpre, dact, dact, dproj)


def _conv_ffn_fwd(hid, w, b, name):
    t, f2 = hid.shape
    f = f2 // 2
    kw = w.shape[0]
    tt, tc = _conv_tiles(t, f)
    nj = f // tc

    def body(g_ref, gp_ref, u_ref, up_ref, wg_ref, wu_ref, bg_ref, bu_ref, o_ref, ot_ref, cv_ref, sg_ref, su_ref):
        i = pl.program_id(1)
        _fill_prev_main(sg_ref, gp_ref, g_ref, i, tt)
        _fill_prev_main(su_ref, up_ref, u_ref, i, tt)

        def chunk(r0, rc, carry):
            rows = pl.ds(r0, rc)
            gate = _taps_chunk(sg_ref, wg_ref, kw, r0, rc) + bg_ref[...]
            up = _taps_chunk(su_ref, wu_ref, kw, r0, rc) + bu_ref[...]
            o_ref[rows, :] = (jax.nn.silu(gate) * up).astype(o_ref.dtype)
            cv_ref[0, rows, :] = gate.astype(cv_ref.dtype)
            cv_ref[1, rows, :] = up.astype(cv_ref.dtype)
            return carry

        _row_chunks(tt, chunk)
        ot_ref[...] = o_ref[...].T

    return pl.pallas_call(
        body, grid=(nj, t // tt),
        in_specs=[pl.BlockSpec((tt, tc), lambda j, i: (i, j)), _prev_spec(tt, tc, 0),
                  pl.BlockSpec((tt, tc), lambda j, i: (i, j + nj)), _prev_spec(tt, tc, nj),
                  pl.BlockSpec((kw, tc), lambda j, i: (0, j)), pl.BlockSpec((kw, tc), lambda j, i: (0, j + nj)),
                  pl.BlockSpec((1, tc), lambda j, i: (0, j)), pl.BlockSpec((1, tc), lambda j, i: (0, j + nj))],
        out_specs=[pl.BlockSpec((tt, tc), lambda j, i: (i, j)), pl.BlockSpec((tc, tt), lambda j, i: (j, i)),
                   pl.BlockSpec((2, tt, tc), lambda j, i: (0, i, j))],
        out_shape=[S((t, f), ACT_DTYPE), S((f, t), ACT_DTYPE), S((2, t, f), ACT_DTYPE)],
        scratch_shapes=[pltpu.VMEM((HALO + tt, tc), F32), pltpu.VMEM((HALO + tt, tc), F32)], name=name,
        compiler_params=_cp(("parallel", "arbitrary")))(hid, hid, hid, hid, w, w, b, b)


def _conv_ffn_bwd(hid, w, cv, dact, name):
    t, f2 = hid.shape
    f = f2 // 2
    kw = w.shape[0]
    tt, tc = _conv_tiles(t, f)
    nj = f // tc
    nt = t // tt

    def body(g_ref, u_ref, wg_ref, wu_ref, cv_ref, cvn_ref, d_ref, dn_ref, dh_ref, dw_ref, db_ref, dg_ref, du_ref):
        i = pl.program_id(1)

        @pl.when(i == 0)
        def _():
            dw_ref[...] = jnp.zeros_like(dw_ref)
            db_ref[...] = jnp.zeros_like(db_ref)

        def cotangents(gate, up, dact_v):
            sg = jax.nn.sigmoid(gate)
            return dact_v * up * (sg * (1.0 + gate * (1.0 - sg))), dact_v * (gate * sg)

        def stage(r0, rc, dbs):
            rows = pl.ds(r0, rc)
            dg, du = cotangents(cv_ref[0, rows, :].astype(F32), cv_ref[1, rows, :].astype(F32), d_ref[rows, :].astype(F32))
            dg_ref[rows, :] = dg
            du_ref[rows, :] = du
            return [dbs[0] + _fold8(dg), dbs[1] + _fold8(du)]

        zero8 = jnp.zeros((8, tc), F32)
        dbs = _row_chunks(tt, stage, [zero8, zero8])
        dgn, dun = cotangents(cvn_ref[0].astype(F32), cvn_ref[1].astype(F32), dn_ref[...].astype(F32))
        dg_ref[pl.ds(tt, HALO), :] = jnp.where(i < nt - 1, dgn, 0.0)
        du_ref[pl.ds(tt, HALO), :] = jnp.where(i < nt - 1, dun, 0.0)
        for s, (sd_ref, x_ref, w_ref) in enumerate(((dg_ref, g_ref, wg_ref), (du_ref, u_ref, wu_ref))):
            def chunk(r0, rc, dws, s=s, sd_ref=sd_ref, x_ref=x_ref, w_ref=w_ref):
                dx, dws = _conv_bwd_chunk(sd_ref, x_ref[pl.ds(r0, rc), :].astype(F32), w_ref, kw, r0, rc, dws)
                dh_ref[s, pl.ds(r0, rc), :] = dx.astype(dh_ref.dtype)
                return dws

            dws = _row_chunks(tt, chunk, [zero8] * kw)
            for k in range(kw):
                dw_ref[s, k:k + 1, :] += jnp.sum(dws[k], axis=0, keepdims=True)
            db_ref[s, 0:1, :] += jnp.sum(dbs[s], axis=0, keepdims=True)

    acc = pl.BlockSpec((2, 8, tc), lambda j, i: (0, 0, j))
    dsc = pltpu.VMEM((tt + HALO, tc), F32)
    nxt = lambda j, i: (0, jnp.minimum((i + 1) * (tt // HALO), t // HALO - 1), j)
    return pl.pallas_call(
        body, grid=(nj, nt),
        in_specs=[pl.BlockSpec((tt, tc), lambda j, i: (i, j)), pl.BlockSpec((tt, tc), lambda j, i: (i, j + nj)),
                  pl.BlockSpec((kw, tc), lambda j, i: (0, j)), pl.BlockSpec((kw, tc), lambda j, i: (0, j + nj)),
                  pl.BlockSpec((2, tt, tc), lambda j, i: (0, i, j)), pl.BlockSpec((2, HALO, tc), nxt),
                  pl.BlockSpec((tt, tc), lambda j, i: (i, j)), _next_spec(tt, tc, 0, t)],
        out_specs=[pl.BlockSpec((2, tt, tc), lambda j, i: (0, i, j)), acc, acc],
        out_shape=[S((2, t, f), MXU_DTYPE), S((2, 8, f), F32), S((2, 8, f), F32)],
        scratch_shapes=[dsc, dsc], name=name,
        compiler_params=_cp(("parallel", "arbitrary")))(hid, hid, w, w, cv, cv, dact, dact)


SSD_SUB = 2


def _ssd_chunk(xs, bm, cm, dtraw, hin, bias, alog, dskip):
    n = bm.shape[0]
    row = lax.broadcasted_iota(jnp.int32, (n, n), 0)
    col = lax.broadcasted_iota(jnp.int32, (n, n), 1)
    causal = row >= col
    lane = lax.broadcasted_iota(jnp.int32, (1, 128), 1)
    sub = lax.broadcasted_iota(jnp.int32, (128, 1), 0)
    last = (lax.broadcasted_iota(jnp.int32, (n, 1), 0) == n - 1).astype(F32)
    dt = jax.nn.softplus(dtraw + bias)
    adt = dt * (-jnp.exp(alog))
    tri = causal.astype(F32)
    acs = jnp.dot(tri, adt, preferred_element_type=F32, precision=lax.Precision.HIGHEST)
    ch = lax.broadcasted_iota(jnp.int32, (128, GROUP_CH), 1)
    hd = lax.broadcasted_iota(jnp.int32, (128, GROUP_CH), 0) * HEAD_DIM
    expand = ((ch >= hd) & (ch < hd + HEAD_DIM)).astype(F32)
    acs_x = jnp.dot(acs, expand, preferred_element_type=F32, precision=lax.Precision.HIGHEST)
    alast_x = jnp.sum(acs_x * last, axis=0, keepdims=True)
    acs_t = acs.T
    scores = lax.dot_general(_mx(cm), _mx(bm), (((1,), (1,)), ((), ())), preferred_element_type=F32)
    yds, xts = [], []
    for r in range(HEADS_PER_GROUP):
        pick = (lane == r).astype(F32)
        acol = jnp.sum(acs * pick, axis=1, keepdims=True)
        arow = jnp.sum(acs_t * (sub == r).astype(F32), axis=0, keepdims=True)
        dtc = jnp.sum(dt * pick, axis=1, keepdims=True)
        lm = jnp.exp(jnp.where(causal, acol - arow, -1e30))
        xts.append(xs[r] * dtc)
        yds.append(jnp.dot(_mx(scores * lm), _mx(xts[r]), preferred_element_type=F32))
    xt = jnp.concatenate(xts, axis=1)
    yo = jnp.exp(acs_x) * jnp.dot(_mx(cm), _mx(hin), preferred_element_type=F32)
    st = lax.dot_general(_mx(bm), _mx(xt * jnp.exp(alast_x - acs_x)), (((0,), (0,)), ((), ())), preferred_element_type=F32)
    hout = jnp.exp(alast_x) * hin + st
    return jnp.concatenate(yds, axis=1) + yo + dskip * jnp.concatenate(xs, axis=1), hout


def _ssd_post(y, z, normg):
    return _rms(y * jax.nn.silu(z), normg)


def _ssd_block(datas, hin, consts):
    ys = []
    for data in datas:
        y, hin = _ssd_chunk(*data, hin, *consts)
        ys.append(y)
    return ys, hin


def _ssd_specs(d_ssm, ngrp, zcol0, rev, nb):
    cc = (lambda c: nb - 1 - c) if rev else (lambda c: c)
    rows = SSD_SUB * CHUNK
    xj, bj, cj, zj = 0, d_ssm // 128, d_ssm // 128 + ngrp, zcol0 // GROUP_CH
    const = lambda w: pl.BlockSpec((None, 8, w), lambda g, c: (g, 0, 0))
    return cc, [
        pl.BlockSpec((rows, GROUP_CH), lambda g, c: (cc(c), xj + g)),
        pl.BlockSpec((rows, 128), lambda g, c: (cc(c), bj + g)),
        pl.BlockSpec((rows, 128), lambda g, c: (cc(c), cj + g)),
        pl.BlockSpec((rows, 128), lambda g, c: (cc(c), g)),
        pl.BlockSpec((rows, GROUP_CH), lambda g, c: (cc(c), zj + g)),
        const(128), const(128), const(GROUP_CH), const(GROUP_CH)]


def _sub_rows(s):
    return slice(CHUNK * s, CHUNK * (s + 1))


def _ssd_load(x_ref, b_ref, c_ref, dt_ref, z_ref, bias_ref, alog_ref, dsk_ref, ng_ref):
    datas, zs = [], []
    for s in range(SSD_SUB):
        rows = _sub_rows(s)
        xs = [x_ref[rows, HEAD_DIM * r:HEAD_DIM * (r + 1)].astype(F32) for r in range(HEADS_PER_GROUP)]
        datas.append((xs, b_ref[rows, :].astype(F32), c_ref[rows, :].astype(F32), dt_ref[rows, :]))
        zs.append(z_ref[rows, :].astype(F32))
    return datas, zs, (bias_ref[0:1, :], alog_ref[0:1, :], dsk_ref[0:1, :]), ng_ref[0:1, :]


def _ssd_fwd(xbc, dtp, proj, zcol0, bias_p, alog_p, dskip_x, normg_x, cat, cat_t, name):
    t = xbc.shape[0]
    ngrp = bias_p.shape[0]
    d_ssm = ngrp * GROUP_CH
    rows = SSD_SUB * CHUNK
    nb = t // rows
    d_a = cat.shape[1] - d_ssm
    assert d_a % GROUP_CH == 0 and zcol0 % GROUP_CH == 0 and t % rows == 0
    _, specs = _ssd_specs(d_ssm, ngrp, zcol0, False, nb)

    def body(x_ref, b_ref, c_ref, dt_ref, z_ref, bias_ref, alog_ref, dsk_ref, ng_ref, cat_in, catt_in,
             yn_ref, ynt_ref, y_ref, hs_ref, h_ref):
        del cat_in, catt_in

        @pl.when(pl.program_id(1) == 0)
        def _():
            h_ref[...] = jnp.zeros_like(h_ref)

        datas, zs, consts, normg = _ssd_load(x_ref, b_ref, c_ref, dt_ref, z_ref, bias_ref, alog_ref, dsk_ref, ng_ref)
        hs_ref[...] = h_ref[...]
        ys, hout = _ssd_block(datas, h_ref[...], consts)
        for s in range(SSD_SUB):
            y_ref[_sub_rows(s), :] = ys[s].astype(y_ref.dtype)
            yn = _ssd_post(ys[s], zs[s], normg).astype(yn_ref.dtype)
            yn_ref[_sub_rows(s), :] = yn
            ynt_ref[:, _sub_rows(s)] = yn.T
        h_ref[...] = hout

    hbm = pl.BlockSpec(memory_space=pl.ANY)
    return pl.pallas_call(
        body, grid=(ngrp, nb), in_specs=specs + [hbm, hbm],
        out_specs=[pl.BlockSpec((rows, GROUP_CH), lambda g, c: (c, d_a // GROUP_CH + g)),
                   pl.BlockSpec((GROUP_CH, rows), lambda g, c: (d_a // GROUP_CH + g, c)),
                   pl.BlockSpec((rows, GROUP_CH), lambda g, c: (c, g)),
                   pl.BlockSpec((None, None, D_STATE, GROUP_CH), lambda g, c: (c, g, 0, 0))],
        out_shape=[S(cat.shape, cat.dtype), S(cat_t.shape, cat_t.dtype), S((t, d_ssm), ACT_DTYPE),
                   S((nb, ngrp, D_STATE, GROUP_CH), F32)],
        scratch_shapes=[pltpu.VMEM((D_STATE, GROUP_CH), F32)],
        input_output_aliases={9: 0, 10: 1}, name=name,
        compiler_params=_cp(("parallel", "arbitrary")))(xbc, xbc, xbc, dtp, proj, bias_p, alog_p, dskip_x, normg_x, cat, cat_t)


def _ssd_bwd(xbc, dtp, proj, zcol0, bias_p, alog_p, dskip_x, normg_x, hs, ypre, dcat, dproj, name):
    t = xbc.shape[0]
    ngrp = bias_p.shape[0]
    d_ssm = ngrp * GROUP_CH
    rows = SSD_SUB * CHUNK
    nb = t // rows
    d_a = dcat.shape[1] - d_ssm
    cc, specs = _ssd_specs(d_ssm, ngrp, zcol0, True, nb)

    def body(x_ref, b_ref, c_ref, dt_ref, z_ref, bias_ref, alog_ref, dsk_ref, ng_ref, hs_ref, yp_ref, dy_ref, dp_in,
             dz_ref, dx_ref, db_ref, dc_ref, ddt_ref, dbias_ref, dalog_ref, ddsk_ref, dng_ref, dh_ref):
        del dp_in

        @pl.when(pl.program_id(1) == 0)
        def _():
            dh_ref[...] = jnp.zeros_like(dh_ref)
            for r in (dbias_ref, dalog_ref, ddsk_ref, dng_ref):
                r[...] = jnp.zeros_like(r)

        datas, zs, consts, normg = _ssd_load(x_ref, b_ref, c_ref, dt_ref, z_ref, bias_ref, alog_ref, dsk_ref, ng_ref)
        dys, dng = [], jnp.zeros_like(normg)
        for s in range(SSD_SUB):
            rws = _sub_rows(s)
            _, vjp_post = jax.vjp(_ssd_post, yp_ref[rws, :].astype(F32), zs[s], normg)
            dy, dz, dg = vjp_post(dy_ref[rws, :].astype(F32))
            dys.append(dy)
            dng = dng + dg
            dz_ref[rws, :] = dz.astype(dz_ref.dtype)
        _, vjp = jax.vjp(_ssd_block, datas, hs_ref[...], consts)
        ddatas, dhin, (dbias, dalog, ddsk) = vjp((dys, dh_ref[...]))
        for s, (dxs, dbm, dcm, ddt) in enumerate(ddatas):
            rws = _sub_rows(s)
            for r in range(HEADS_PER_GROUP):
                dx_ref[rws, HEAD_DIM * r:HEAD_DIM * (r + 1)] = dxs[r].astype(dx_ref.dtype)
            db_ref[rws, :] = dbm.astype(db_ref.dtype)
            dc_ref[rws, :] = dcm.astype(dc_ref.dtype)
            ddt_ref[rws, :] = ddt
        dh_ref[...] = dhin
        dbias_ref[0:1, :] += dbias
        dalog_ref[0:1, :] += dalog
        ddsk_ref[0:1, :] += ddsk
        dng_ref[0:1, :] += dng

    acc = lambda w: pl.BlockSpec((None, 8, w), lambda g, c: (g, 0, 0))
    blk = lambda w: pl.BlockSpec((rows, w), lambda g, c: (cc(c), g))
    return pl.pallas_call(
        body, grid=(ngrp, nb),
        in_specs=specs + [pl.BlockSpec((None, None, D_STATE, GROUP_CH), lambda g, c: (cc(c), g, 0, 0)),
                          blk(GROUP_CH),
                          pl.BlockSpec((rows, GROUP_CH), lambda g, c: (cc(c), d_a // GROUP_CH + g)),
                          pl.BlockSpec(memory_space=pl.ANY)],
        out_specs=[pl.BlockSpec((rows, GROUP_CH), lambda g, c: (cc(c), zcol0 // GROUP_CH + g)),
                   blk(GROUP_CH), blk(128), blk(128), blk(128), acc(128), acc(128), acc(GROUP_CH), acc(GROUP_CH)],
        out_shape=[S(dproj.shape, dproj.dtype), S((t, d_ssm), XBC_DTYPE), S((t, ngrp * 128), XBC_DTYPE),
                   S((t, ngrp * 128), XBC_DTYPE), S((t, ngrp * 128), F32), S((ngrp, 8, 128), F32),
                   S((ngrp, 8, 128), F32), S((ngrp, 8, GROUP_CH), F32), S((ngrp, 8, GROUP_CH), F32)],
        scratch_shapes=[pltpu.VMEM((D_STATE, GROUP_CH), F32)],
        input_output_aliases={12: 0}, name=name,
        compiler_params=_cp(("parallel", "arbitrary")))(xbc, xbc, xbc, dtp, proj, bias_p, alog_p, dskip_x, normg_x, hs, ypre,
                                                        dcat, dproj)


def _adamw(parts, w, m, v, name):
    r, c = w.shape
    tr = _pick(r, (256, 128, 64, 32, 16, 8)) if c * 4 * 256 <= 4 * 1024 * 1024 else _pick(r, (64, 32, 16, 8))

    def body(p_ref, w_ref, m_ref, v_ref, g_ref, d_ref, nm_ref, nv_ref):
        g = p_ref[0].astype(F32)
        for k in range(1, N_DEV):
            g = g + p_ref[k].astype(F32)
        mm = ADAM_B1 * m_ref[...] + (1.0 - ADAM_B1) * g
        vv = ADAM_B2 * v_ref[...] + (1.0 - ADAM_B2) * jnp.square(g)
        m_hat = mm / (1.0 - ADAM_B1 ** ADAM_STEP)
        v_hat = vv / (1.0 - ADAM_B2 ** ADAM_STEP)
        g_ref[...] = g
        d_ref[...] = -ADAM_LR * (m_hat / (jnp.sqrt(v_hat) + ADAM_EPS) + ADAM_WD * w_ref[...])
        nm_ref[...] = mm
        nv_ref[...] = vv

    blk = pl.BlockSpec((tr, c), lambda i: (i, 0))
    return pl.pallas_call(
        body, grid=(r // tr,),
        in_specs=[pl.BlockSpec((N_DEV, tr, c), lambda i: (0, i, 0)), blk, blk, blk],
        out_specs=[blk, blk, blk, blk], out_shape=[S((r, c), F32)] * 4, name=name,
        compiler_params=_cp(("parallel",)))(parts, w, m, v)


def _mesh_pos():
    return lax.axis_index("x"), lax.axis_index("y"), lax.axis_index("c")


def _peer(d, x, y, c):
    return (1 - x if (d >> 2) & 1 else x, 1 - y if (d >> 1) & 1 else y, 1 - c if d & 1 else c)


def _gather_two_level(blocks, name):
    n = len(blocks)

    def body(*refs):
        srcs, outs = refs[:n], refs[n:2 * n]
        send_sems, recv_sems, loc_sems = refs[2 * n:]
        x, y, c = _mesh_pos()
        lin = lambda px, py, pc: 4 * px + 2 * py + pc
        me, sibling = (x, y, c), (x, y, 1 - c)
        chips = [(1 - x, y), (x, 1 - y), (1 - x, 1 - y)]

        def copy(a, k, block, to, src=None):
            slab = outs[a].at[lin(*block)]
            return pltpu.make_async_remote_copy(
                src_ref=slab if src is None else src, dst_ref=slab, send_sem=send_sems.at[a, k],
                recv_sem=recv_sems.at[a, k], device_id=to, device_id_type=pl.DeviceIdType.MESH)

        mine = [pltpu.make_async_copy(srcs[a], outs[a].at[lin(*me)], loc_sems.at[a]) for a in range(n)]
        first = [copy(a, 0, me, sibling, src=srcs[a]) for a in range(n)]
        first += [copy(a, 1 + j, me, (*chip, c), src=srcs[a]) for j, chip in enumerate(chips) for a in range(n)]
        for cp in mine + first:
            cp.start()
        passed = []
        for j, chip in enumerate(chips):
            for a in range(n):
                copy(a, 1 + j, (*chip, c), me).wait_recv()
                passed.append(copy(a, 4 + j, (*chip, c), sibling))
                passed[-1].start()
        for a in range(n):
            copy(a, 0, sibling, me).wait_recv()
            for j, chip in enumerate(chips):
                copy(a, 4 + j, (*chip, 1 - c), me).wait_recv()
        for cp in first + passed:
            cp.wait_send()
        for cp in mine:
            cp.wait()

    hbm = pl.BlockSpec(memory_space=pl.ANY)
    return pl.pallas_call(
        body, in_specs=[hbm] * n, out_specs=[hbm] * n, out_shape=[S((N_DEV,) + b.shape, b.dtype) for b in blocks],
        scratch_shapes=[pltpu.SemaphoreType.DMA((n, N_DEV - 1)), pltpu.SemaphoreType.DMA((n, N_DEV - 1)),
                        pltpu.SemaphoreType.DMA((n,))],
        name=name, compiler_params=pltpu.CompilerParams(has_side_effects=True))(*blocks)


def _xfer_start(items, name, after=None):
    n = len(items)
    kinds = [k for k, _ in items]
    srcs = [pltpu.with_memory_space_constraint(a, pltpu.HBM) for _, a in items]
    land_shapes = [((N_DEV,) + a.shape if k == 'gather' else a.shape, a.dtype) for k, a in items]
    lands = [pltpu.with_memory_space_constraint(lax.empty(s, dt), pltpu.HBM) for s, dt in land_shapes]
    extra = [] if after is None else [after]

    def body(*refs):
        src_refs, land_refs = refs[:n], refs[n:2 * n]
        outs = refs[2 * n + len(extra):]
        sems = outs[:2 * n]
        token = outs[4 * n]
        x, y, c = _mesh_pos()
        me = 4 * x + 2 * y + c
        for a in range(n):
            for d in range(1, N_DEV):
                px, py, pc = _peer(d, x, y, c)
                src = src_refs[a] if kinds[a] == 'gather' else src_refs[a].at[4 * px + 2 * py + pc]
                pltpu.make_async_remote_copy(
                    src_ref=src, dst_ref=land_refs[a].at[me], send_sem=sems[2 * a].at[d - 1],
                    recv_sem=sems[2 * a + 1].at[d - 1], device_id=(px, py, pc),
                    device_id_type=pl.DeviceIdType.MESH).start()
        token[...] = jnp.zeros_like(token)

    hbm = pl.BlockSpec(memory_space=pltpu.HBM)
    sem = pl.BlockSpec(memory_space=pltpu.SEMAPHORE)
    out_shape = ([pltpu.SemaphoreType.DMA((N_DEV - 1,))] * (2 * n)
                 + [pltpu.HBM(a.shape, a.dtype) for a in srcs] + [pltpu.HBM(s, dt) for s, dt in land_shapes]
                 + [S((8, 128), F32)])
    res = pl.pallas_call(
        body, name=name, out_shape=out_shape,
        in_specs=[hbm] * (2 * n) + [pl.BlockSpec(memory_space=pl.ANY)] * len(extra),
        out_specs=[sem] * (2 * n) + [hbm] * (2 * n) + [pl.BlockSpec(memory_space=pltpu.VMEM)],
        input_output_aliases={**{a: 2 * n + a for a in range(n)}, **{n + a: 3 * n + a for a in range(n)}},
        compiler_params=pltpu.CompilerParams(has_side_effects=pltpu.SideEffectType.DATAFLOW_SIDE_EFFECTING),
    )(*srcs, *lands, *extra)
    return (kinds, res[:2 * n], res[2 * n:3 * n], res[3 * n:4 * n]), res[4 * n]


def _xfer_wait(handle, after, name):
    kinds, sems, src_thru, land_thru = handle
    n = len(kinds)

    def body(*refs):
        land_refs = refs[n:2 * n]
        sem_refs = refs[2 * n:4 * n]
        x, y, c = _mesh_pos()
        me = 4 * x + 2 * y + c
        for a in range(n):
            for d in range(1, N_DEV):
                slab = land_refs[a].at[me]
                cp = pltpu.make_async_remote_copy(
                    src_ref=slab, dst_ref=slab, send_sem=sem_refs[2 * a].at[d - 1], recv_sem=sem_refs[2 * a + 1].at[d - 1],
                    device_id=_peer(d, x, y, c), device_id_type=pl.DeviceIdType.MESH)
                cp.wait_send()
                cp.wait_recv()

    hbm = pl.BlockSpec(memory_space=pltpu.HBM)
    sem = pl.BlockSpec(memory_space=pltpu.SEMAPHORE)
    res = pl.pallas_call(
        body, name=name,
        out_shape=[pltpu.HBM(a.shape, a.dtype) for a in src_thru] + [pltpu.HBM(a.shape, a.dtype) for a in land_thru],
        in_specs=[hbm] * (2 * n) + [sem] * (2 * n) + [pl.BlockSpec(memory_space=pl.ANY)],
        out_specs=[hbm] * (2 * n), input_output_aliases={a: a for a in range(2 * n)},
        compiler_params=pltpu.CompilerParams(has_side_effects=pltpu.SideEffectType.DATAFLOW_SIDE_EFFECTING),
    )(*src_thru, *land_thru, *sems, after)
    x, y, c = _mesh_pos()
    me = 4 * x + 2 * y + c
    out = []
    for a in range(n):
        src = res[a]
        own = src[None] if kinds[a] == 'gather' else lax.dynamic_index_in_dim(src, me, 0, keepdims=True)
        out.append(lax.dynamic_update_index_in_dim(res[n + a], own, me, 0))
    return out


def _stack_to_full(kind, st):
    if kind == 'row':
        return st.reshape(st.shape[0] * st.shape[1], st.shape[2])
    return jnp.concatenate([st[k] for k in range(st.shape[0])], axis=1)


def _full_to_stack(kind, full):
    r, c = full.shape
    if kind == 'row':
        return full.reshape(N_DEV, r // N_DEV, c)
    w = c // N_DEV
    return jnp.stack([full[:, k * w:(k + 1) * w] for k in range(N_DEV)], axis=0)


SMALL_ROWS = 256


def _pack_small(named):
    layout = [(a.shape, a.size, -(-a.size // 1024) * 8) for a in named]
    total = -(-sum(nr for _, _, nr in layout) // SMALL_ROWS) * SMALL_ROWS * 128
    packed, off = None, 0
    for a, (_, n, nr) in zip(named, layout):
        part = jnp.pad(a.reshape(-1).astype(F32), (off, total - off - n))
        packed = part if packed is None else packed + part
        off += nr * 128
    return packed.reshape(total // 128, 128), layout


def _unpack_small(packed, layout):
    out, r0 = [], 0
    for shape, n, nr in layout:
        out.append(packed[r0:r0 + nr].reshape(-1)[:n].reshape(shape))
        r0 += nr
    return out


def _row0(acc):
    return acc[0]


def _local_step(x, p, tgt, sm, comm):
    t, d = x.shape
    h_n = sm['dt_bias'].shape[-1]
    ngrp = h_n // HEADS_PER_GROUP
    d_ssm = h_n * HEAD_DIM
    d_a = sm['ln_a_g'].shape[-1]
    d_mix = d_a + d_ssm
    d_xbc = sm['conv_ssm_b'].shape[-1]
    d_main = 2 * d_a + d_ssm + d_xbc
    assert d_xbc == d_ssm + 2 * ngrp * D_STATE and h_n <= 128
    zcol0, xcol0 = 2 * d_a, 2 * d_a + d_ssm
    vec = lambda v: v.reshape(1, -1)

    bst = jnp.pad(sm['b_s'].T, ((0, 0), (0, 128 - sm['b_s'].shape[0])))
    grp = lambda v, w: jnp.broadcast_to(jnp.pad(v.reshape(ngrp, 1, -1), ((0, 0), (0, 0), (0, w - v.size // ngrp))), (ngrp, 8, w))
    bias_p, alog_p = grp(sm['dt_bias'], 128), grp(sm['a_log'], 128)
    dskip_x = grp(jnp.repeat(sm['d_skip'], HEAD_DIM), GROUP_CH)
    normg_x = grp(sm['ssm_norm_g'], GROUP_CH)
    pad_dt = lambda v: jnp.pad(v[:, :h_n].reshape(t, ngrp, HEADS_PER_GROUP),
                               ((0, 0), (0, 0), (0, 128 - HEADS_PER_GROUP))).reshape(t, ngrp * 128)

    g_mix = vec(sm['norm_mix_g']) + comm.tok0
    a_n, a_t = _rms_fwd(x, g_mix, "rms_mix")
    wf = comm.weights('a', a_n)
    slabs = [wf['w_in'][k] for k in range(N_DEV)]
    w_main = jnp.concatenate(slabs[:-1] + [slabs[-1][:, :slabs[-1].shape[1] - h_n]], axis=1)
    w_dt = jnp.pad(slabs[-1][:, slabs[-1].shape[1] - h_n:], ((0, 0), (0, 128 - h_n)))
    proj = _mm_nn(a_n, w_main, out_dtype=ACT_DTYPE, name="mm_in")
    dtp = pad_dt(_mm_nn(a_n, w_dt, out_dtype=F32, name="mm_dt"))
    cat, cat_t = _gmlp_fwd(proj, vec(sm['ln_a_g']), vec(sm['ln_a_b']), sm['w_s'], bst, vec(sm['norm_a_g']), d_mix, "gmlp_fwd")
    xbc, cpre = _conv_ssm_fwd(proj, xcol0, wf['conv_ssm_w'], vec(sm['conv_ssm_b']), "conv_ssm_fwd")
    cat, cat_t, ypre, hs = _ssd_fwd(xbc, dtp, proj, zcol0, bias_p, alog_p, dskip_x, normg_x, cat, cat_t, "ssd_fwd")
    wf.update(comm.weights('b', hs))
    h1 = _mm_nn(cat, wf['w_out'], out_dtype=F32, name="mm_out", res=x)
    f_n, f_t = _rms_fwd(h1, vec(sm['norm_ffn_g']), "rms_ffn")
    hid = _mm_nn(f_n, wf['w_up'], out_dtype=ACT_DTYPE, name="mm_up")
    act, act_t, cv = _conv_ffn_fwd(hid, wf['conv_ffn_w'], vec(sm['conv_ffn_b']), "conv_ffn_fwd")
    h2 = _mm_nn(act, wf['w_down'], out_dtype=F32, name="mm_down", res=h1)
    r_n, r_t = _rms_fwd(h2, vec(sm['norm_ple_g']), "rms_ple")
    q = _mm_nn(r_n, wf['w_ple_gate'], out_dtype=ACT_DTYPE, name="mm_pg")
    p_m = p.astype(MXU_DTYPE)
    pe = _mm_nn(p_m, wf['w_ple'], out_dtype=ACT_DTYPE, name="mm_ple")

    loss, dh3, dq, dpe, dgf = _head(h2, q, pe, tgt, vec(sm['norm_final_g']), "head")
    wgrad = lambda act_t, g, name, **kw: _mm_nn(act_t, g, out_dtype=WIRE_DTYPE, name=name, wide=True, **kw)
    gs = {}
    gs['norm_final_g'] = _row0(dgf)
    g_ple = wgrad(p_m.T, dpe, "wg_ple")
    g_pg = wgrad(r_t, dq, "wg_pg")
    dr = _mm_nt(dq, wf['w_ple_gate'], out_dtype=ACT_DTYPE, name="dg_pg")
    dh2, dh2m, dg = _rms_bwd(h2, vec(sm['norm_ple_g']), dr, dh3, "rms_ple_bwd")
    gs['norm_ple_g'] = _row0(dg)
    g_down = wgrad(act_t, dh2m, "wg_down")
    tok = comm.send('1', {'w_ple': g_ple, 'w_ple_gate': g_pg, 'w_down': g_down})
    dact = _mm_nt(dh2m, wf['w_down'], out_dtype=ACT_DTYPE, name="dg_down")
    dhid, dcw, dcb = _conv_ffn_bwd(hid, wf['conv_ffn_w'] + tok, cv, dact, "conv_ffn_bwd")
    kf = wf['conv_ffn_w'].shape[0]
    g_cf = jnp.concatenate([dcw[0, :kf], dcw[1, :kf]], axis=1)
    gs['conv_ffn_b'] = jnp.concatenate([dcb[0, 0], dcb[1, 0]], axis=0)
    g_up = wgrad(f_t, dhid, "wg_up", b_split=2, out_slabs=N_DEV)
    df = _mm_nt(dhid, wf['w_up'], out_dtype=ACT_DTYPE, name="dg_up", a_split=2)
    dh1, dh1m, dg = _rms_bwd(h1, vec(sm['norm_ffn_g']), df, dh2, "rms_ffn_bwd")
    gs['norm_ffn_g'] = _row0(dg)
    g_out = wgrad(cat_t, dh1m, "wg_out")
    tok = comm.send('2', {'conv_ffn_w': g_cf, 'w_up': g_up, 'w_out': g_out}, stacked=('w_up',))
    dcat = _mm_nt(dh1m, wf['w_out'], out_dtype=ACT_DTYPE, name="dg_out")
    dproj, dlng, dlnb, dws, dbst, dng = _gmlp_bwd(proj, vec(sm['ln_a_g']) + tok, vec(sm['ln_a_b']), sm['w_s'], bst,
                                                  vec(sm['norm_a_g']), dcat, d_main, "gmlp_bwd")
    gs['ln_a_g'], gs['ln_a_b'], gs['w_s'], gs['norm_a_g'] = _row0(dlng), _row0(dlnb), dws, _row0(dng)
    gs['b_s'] = dbst[:, :sm['b_s'].shape[0]].T
    dproj, dxs, dbm, dcm, ddtp, dbias, dalog, ddsk, dsng = _ssd_bwd(
        xbc, dtp, proj, zcol0, bias_p, alog_p, dskip_x, normg_x, hs, ypre, dcat, dproj, "ssd_bwd")
    gs['dt_bias'] = dbias[:, 0, :HEADS_PER_GROUP].reshape(h_n)
    gs['a_log'] = dalog[:, 0, :HEADS_PER_GROUP].reshape(h_n)
    gs['d_skip'] = ddsk[:, 0, :].reshape(h_n, HEAD_DIM).sum(axis=-1)
    gs['ssm_norm_g'] = dsng[:, 0, :].reshape(d_ssm)
    dws_c, dbs_c = [], []
    off = 0
    for nm, dpart in (("x", dxs), ("b", dbm), ("c", dcm)):
        dproj, dw_c, db_c = _conv_ssm_bwd(proj, xcol0 + off, wf['conv_ssm_w'], cpre, off, dpart, dproj,
                                          xcol0 + off, "conv_ssm_bwd_" + nm)
        dws_c.append(dw_c[:wf['conv_ssm_w'].shape[0]])
        dbs_c.append(db_c[0])
        off += dpart.shape[1]
    g_cs = jnp.concatenate(dws_c, axis=1)
    gs['conv_ssm_b'] = jnp.concatenate(dbs_c, axis=0)
    ddt = jnp.pad(ddtp.reshape(t, ngrp, 128)[:, :, :HEADS_PER_GROUP].reshape(t, h_n), ((0, 0), (0, 128 - h_n))).astype(MXU_DTYPE)
    g_in = jnp.concatenate([wgrad(a_t, dproj, "wg_in"), wgrad(a_t, ddt, "wg_dt")[:, :h_n]], axis=1)
    tok = comm.send('3', {'conv_ssm_w': g_cs, 'w_in': g_in}, [(n, gs[n]) for n in REPLICATED if n != 'norm_mix_g'])
    da = _mm_nt(ddt + tok.astype(ddt.dtype), w_dt, out_dtype=F32, name="dg_dt")
    da = _mm_nt(dproj, w_main, out_dtype=ACT_DTYPE, name="dg_in", res=da)
    dx, dg = _rms_bwd(x, g_mix + tok, da, dh1, "rms_mix_bwd", mx_copy=False)
    comm.send('4', {}, [('norm_mix_g', _row0(dg)), ('loss', loss[0, 0:1])])
    return dx


def kernel(x, p, norm_mix_g, w_in, ln_a_g, ln_a_b, w_s, b_s, norm_a_g, conv_ssm_w, conv_ssm_b, dt_bias, a_log, d_skip, ssm_norm_g, w_out, norm_ffn_g, w_up, conv_ffn_w, conv_ffn_b, w_down, norm_ple_g, w_ple_gate, w_ple, norm_final_g, loss_target, m_norm_mix_g, m_w_in, m_ln_a_g, m_ln_a_b, m_w_s, m_b_s, m_norm_a_g, m_conv_ssm_w, m_conv_ssm_b, m_dt_bias, m_a_log, m_d_skip, m_ssm_norm_g, m_w_out, m_norm_ffn_g, m_w_up, m_conv_ffn_w, m_conv_ffn_b, m_w_down, m_norm_ple_g, m_w_ple_gate, m_w_ple, m_norm_final_g, v_norm_mix_g, v_w_in, v_ln_a_g, v_ln_a_b, v_w_s, v_b_s, v_norm_a_g, v_conv_ssm_w, v_conv_ssm_b, v_dt_bias, v_a_log, v_d_skip, v_ssm_norm_g, v_w_out, v_norm_ffn_g, v_w_up, v_conv_ffn_w, v_conv_ffn_b, v_w_down, v_norm_ple_g, v_w_ple_gate, v_w_ple, v_norm_final_g):
    given = dict(locals())
    wts = {n: given[n] for n in WEIGHTS}
    ms = {n: given["m_" + n] for n in WEIGHTS}
    vs = {n: given["v_" + n] for n in WEIGHTS}
    sm = {n: (wts[n][0] if wts[n].ndim > 1 else wts[n]) for n in REPLICATED}
    comm = _Comm({n: wts[n][0] for n in SHARDED})
    dx = _local_step(x[0], p[0, 0], loss_target[0], sm, comm)

    out, loss_out, after = {}, None, dx
    for tag, names, small_names, layout, handle in comm.sent:
        recv = _xfer_wait(handle, after, "grads_%s_wait" % tag)
        for n, parts in zip(names, recv):
            out[n] = _adamw(parts, wts[n][0], ms[n][0], vs[n][0], "adamw_" + n)
            after = out[n][1]
        if small_names:
            pick = lambda src, fill: _pack_small([src[n] if n in src else jnp.full((1,), fill, F32) for n in small_names])[0]
            res = _adamw(recv[-1], pick(wts, 0.0), pick(ms, 0.0), pick(vs, 1.0), "adamw_small_" + tag)
            res = [_unpack_small(o, layout) for o in res]
            after = res[1][0]
            for i, n in enumerate(small_names):
                if n == 'loss':
                    loss_out = res[0][i].reshape(())
                else:
                    out[n] = [res[k][i] for k in range(4)]
    return (loss_out, dx[None], *[out[n][k].reshape(wts[n].shape) for k in range(4) for n in WEIGHTS])


class _Comm:
    GATHER_GROUPS = {'a': ('w_in', 'conv_ssm_w'), 'b': ('w_out', 'w_up', 'conv_ffn_w', 'w_down', 'w_ple_gate', 'w_ple')}

    def __init__(self, blocks):
        wired = lambda grp: [blocks[n].astype(_wire(n)) for n in self.GATHER_GROUPS[grp]]
        self.stacks_a = _gather_two_level(wired('a'), "gather_a")
        self.handle_b, tok = _xfer_start([('gather', b) for b in wired('b')], "gather_b_start", after=self.stacks_a[0])
        self.tok0 = tok[0, 0]
        self.sent = []

    def weights(self, grp, after):
        stacks = self.stacks_a if grp == 'a' else _xfer_wait(self.handle_b, after, "gather_b_wait")
        return {n: st if n == 'w_in' else _stack_to_full(SHARDED[n], st) for n, st in zip(self.GATHER_GROUPS[grp], stacks)}

    def send(self, tag, gw, small=None, stacked=()):
        items = [('scatter', g if n in stacked else _full_to_stack(SHARDED[n], g.astype(_wire(n)))) for n, g in gw.items()]
        layout, small_names = None, []
        if small:
            packed, layout = _pack_small([a for _, a in small])
            small_names = [n for n, _ in small]
            items.append(('gather', packed))
        handle, tok = _xfer_start(items, "grads_%s_start" % tag)
        self.sent.append((tag, list(gw), small_names, layout, handle))
        return tok[0, 0]


def _wire(name):
    return F32 if name in F32_ON_WIRE else WIRE_DTYPE
```

```python
import functools

import jax
import jax.numpy as jnp
from jax import lax
from jax.experimental import pallas as pl
from jax.experimental.pallas import tpu as pltpu

F32 = jnp.float32
MXU_DTYPE = jnp.bfloat16
ACT_DTYPE = jnp.bfloat16
XBC_DTYPE = jnp.bfloat16
WIRE_DTYPE = jnp.bfloat16
EPS = 1e-6
CHUNK = 128
D_STATE = 128
HEAD_DIM = 64
HEADS_PER_GROUP = 4
GROUP_CH = HEAD_DIM * HEADS_PER_GROUP
HALO = 16
N_DEV = 8
VMEM_LIMIT = 56 * 1024 * 1024

ADAM_LR = 0.001
ADAM_B1 = 0.9
ADAM_B2 = 0.999
ADAM_EPS = 1e-08
ADAM_WD = 0.01
ADAM_STEP = 10

WEIGHTS = ['norm_mix_g', 'w_in', 'ln_a_g', 'ln_a_b', 'w_s', 'b_s', 'norm_a_g', 'conv_ssm_w', 'conv_ssm_b', 'dt_bias',
           'a_log', 'd_skip', 'ssm_norm_g', 'w_out', 'norm_ffn_g', 'w_up', 'conv_ffn_w', 'conv_ffn_b', 'w_down',
           'norm_ple_g', 'w_ple_gate', 'w_ple', 'norm_final_g']
SHARDED = {'w_in': 'col', 'conv_ssm_w': 'col', 'w_out': 'row', 'w_up': 'col', 'conv_ffn_w': 'col', 'w_down': 'row',
           'w_ple_gate': 'row', 'w_ple': 'col'}
F32_ON_WIRE = ('conv_ssm_w', 'conv_ffn_w')
REPLICATED = [n for n in WEIGHTS if n not in SHARDED]

S = jax.ShapeDtypeStruct


def _pick(dim, cands):
    for c in cands:
        if c <= dim and dim % c == 0:
            return c
    return dim


def _cp(sem, vmem=VMEM_LIMIT):
    return pltpu.CompilerParams(dimension_semantics=sem, vmem_limit_bytes=vmem)


def _mx(v):
    return v.astype(MXU_DTYPE)


def _rms(v, g):
    return v * lax.rsqrt(jnp.mean(v * v, axis=-1, keepdims=True) + EPS) * g


MM_VMEM_BUDGET = 42 * 1024 * 1024


def _mm_tiles(m, n, k, out_bytes, has_res, tn_cands=(512, 256, 128), k_mult=1, tm_cands=(1024, 512)):
    tn = _pick(n, tn_cands)
    ks = k // k_mult
    best = None
    for tm in [c for c in tm_cands if m % c == 0] or [_pick(m, (256, 128))]:
        for nk in range(1, ks // 128 + 1):
            if ks % nk or (ks // nk) % 128:
                continue
            tk = ks // nk
            need = 2 * 2 * (tm * tk + tk * tn) + tm * tn * (4 + 2 * out_bytes + (8 if has_res else 0))
            if need <= MM_VMEM_BUDGET:
                if best is None or (nk, -tm) < best[0]:
                    best = ((nk, -tm), (tm, tn, tk))
                break
    return best[1] if best else (_pick(m, (512, 256, 128)), tn, _pick(ks, (128,)))


def _mm_body(dot, nk, has_res):
    def body(*refs):
        if has_res:
            a_ref, b_ref, r_ref, o_ref, acc_ref = refs
        else:
            a_ref, b_ref, o_ref, acc_ref = refs
            r_ref = None
        kk = pl.program_id(2)
        d = dot(a_ref[...], b_ref[...])

        def fin(acc):
            if r_ref is not None:
                acc = acc + r_ref[...]
            o_ref[...] = acc.astype(o_ref.dtype)

        if nk == 1:
            fin(d)
        else:
            @pl.when(kk == 0)
            def _():
                acc_ref[...] = d

            @pl.when(kk > 0)
            def _():
                acc_ref[...] += d

            @pl.when(kk == nk - 1)
            def _():
                fin(acc_ref[...])

    return body


def _mm_call(body, grid, a_spec, b_spec, tm, tn, m, n, out_dtype, name, args, res, out_slabs=1):
    in_specs = [a_spec, b_spec]
    if res is not None:
        in_specs.append(pl.BlockSpec((tm, tn), lambda i, j, kk: (i, j)))
        args = args + [res]
    if out_slabs == 1:
        out_spec, out_shape = pl.BlockSpec((tm, tn), lambda i, j, kk: (i, j)), S((m, n), out_dtype)
    else:
        out_spec, out_shape = pl.BlockSpec((None, tm, tn), lambda i, j, kk: (j, i, 0)), S((out_slabs, m, tn), out_dtype)
    return pl.pallas_call(
        body, grid=grid, in_specs=in_specs, out_specs=out_spec, out_shape=out_shape,
        scratch_shapes=[pltpu.VMEM((tm, tn), F32)], name=name,
        compiler_params=_cp(("parallel", "parallel", "arbitrary")))(*args)


def _mm_nn(a, b, *, out_dtype, name, res=None, b_split=1, wide=False, out_slabs=1):
    m, k = a.shape
    n = b.shape[1] if b_split == 1 else b.shape[2] * b_split
    tn_cands = (n // out_slabs,) if out_slabs > 1 else (1024, 512, 256, 128) if wide else (512, 256, 128)
    tm, tn, tk = _mm_tiles(m, n // b_split, k, jnp.dtype(out_dtype).itemsize, res is not None, tn_cands=tn_cands)
    assert out_slabs == 1 or (tn * out_slabs == n and tn % 128 == 0)
    nk = k // tk
    njs = (n // b_split) // tn
    body = _mm_body(lambda x, y: jnp.dot(x, y, preferred_element_type=F32), nk, res is not None)
    if b_split == 1:
        b_spec = pl.BlockSpec((tk, tn), lambda i, j, kk: (kk, j))
    else:
        b_spec = pl.BlockSpec((None, tk, tn), lambda i, j, kk: (j // njs, kk, j % njs))
    return _mm_call(body, (m // tm, n // tn, nk), pl.BlockSpec((tm, tk), lambda i, j, kk: (i, kk)), b_spec,
                    tm, tn, m, n, out_dtype, name, [a, b], res, out_slabs)


def _mm_nt(a, b, *, out_dtype, name, res=None, a_split=1):
    if a_split == 1:
        m, k = a.shape
    else:
        m, k = a.shape[1], a.shape[2] * a_split
    n = b.shape[0]
    tm, tn, tk = _mm_tiles(m, n, k, jnp.dtype(out_dtype).itemsize, res is not None, k_mult=a_split, tm_cands=(1024,))
    nk = k // tk
    nks = nk // a_split
    body = _mm_body(lambda x, y: lax.dot_general(x, y, (((1,), (1,)), ((), ())), preferred_element_type=F32),
                    nk, res is not None)
    if a_split == 1:
        a_spec = pl.BlockSpec((tm, tk), lambda i, j, kk: (i, kk))
    else:
        a_spec = pl.BlockSpec((None, tm, tk), lambda i, j, kk: (kk // nks, i, kk % nks))
    return _mm_call(body, (m // tm, n // tn, nk), a_spec, pl.BlockSpec((tn, tk), lambda i, j, kk: (j, kk)),
                    tm, tn, m, n, out_dtype, name, [a, b], res)


def _rms_fwd(x, g, name):
    t, d = x.shape
    tr = _pick(t, (512, 256, 128))

    def body(x_ref, g_ref, o_ref, ot_ref):
        y = _rms(x_ref[...], g_ref[...]).astype(o_ref.dtype)
        o_ref[...] = y
        ot_ref[...] = y.T

    return pl.pallas_call(
        body, grid=(t // tr,),
        in_specs=[pl.BlockSpec((tr, d), lambda i: (i, 0)), pl.BlockSpec((1, d), lambda i: (0, 0))],
        out_specs=[pl.BlockSpec((tr, d), lambda i: (i, 0)), pl.BlockSpec((d, tr), lambda i: (0, i))],
        out_shape=[S((t, d), ACT_DTYPE), S((d, t), ACT_DTYPE)], name=name,
        compiler_params=_cp(("parallel",)))(x, g)


def _rms_bwd(xin, g, dn, dres, name, mx_copy=True):
    t, d = xin.shape
    tr = _pick(t, (256, 128))

    def body(x_ref, g_ref, dn_ref, dr_ref, dx_ref, *rest):
        dg_ref = rest[-1]

        @pl.when(pl.program_id(0) == 0)
        def _():
            dg_ref[...] = jnp.zeros_like(dg_ref)

        _, vjp = jax.vjp(_rms, x_ref[...], g_ref[...])
        dx, dg = vjp(dn_ref[...].astype(F32))
        dx = dr_ref[...] + dx
        dx_ref[...] = dx
        if mx_copy:
            rest[0][...] = dx.astype(rest[0].dtype)
        dg_ref[0:1, :] += dg

    row = pl.BlockSpec((tr, d), lambda i: (i, 0))
    acc = pl.BlockSpec((8, d), lambda i: (0, 0))
    return pl.pallas_call(
        body, grid=(t // tr,),
        in_specs=[row, pl.BlockSpec((1, d), lambda i: (0, 0)), row, row],
        out_specs=[row, row, acc] if mx_copy else [row, acc],
        out_shape=[S((t, d), F32)] + ([S((t, d), MXU_DTYPE)] if mx_copy else []) + [S((8, d), F32)], name=name,
        compiler_params=_cp(("arbitrary",)))(xin, g, dn, dres)


def _head(h2, q, pe, tgt, gf, name):
    t, d = h2.shape
    tr = _pick(t, (256, 128))

    def f(h2v, qv, pev, gfv, tv):
        h3 = h2v + jax.nn.sigmoid(qv) * pev
        y = _rms(h3, gfv)
        return 0.5 * jnp.sum(jnp.mean(jnp.square(y - tv), axis=-1))

    def body(h2_ref, q_ref, pe_ref, t_ref, g_ref, loss_ref, dh_ref, dq_ref, dpe_ref, dg_ref):
        @pl.when(pl.program_id(0) == 0)
        def _():
            loss_ref[...] = jnp.zeros_like(loss_ref)
            dg_ref[...] = jnp.zeros_like(dg_ref)

        tv = t_ref[...]
        loss, vjp = jax.vjp(lambda a, b, c, e: f(a, b, c, e, tv), h2_ref[...], q_ref[...].astype(F32),
                            pe_ref[...].astype(F32), g_ref[...])
        dh, dq, dpe, dg = vjp(jnp.ones((), F32))
        loss_ref[...] += jnp.full(loss_ref.shape, loss, F32)
        dh_ref[...] = dh
        dq_ref[...] = dq.astype(dq_ref.dtype)
        dpe_ref[...] = dpe.astype(dpe_ref.dtype)
        dg_ref[0:1, :] += dg

    row = pl.BlockSpec((tr, d), lambda i: (i, 0))
    return pl.pallas_call(
        body, grid=(t // tr,),
        in_specs=[row, row, row, row, pl.BlockSpec((1, d), lambda i: (0, 0))],
        out_specs=[pl.BlockSpec((8, 128), lambda i: (0, 0)), row, row, row, pl.BlockSpec((8, d), lambda i: (0, 0))],
        out_shape=[S((8, 128), F32), S((t, d), F32), S((t, d), MXU_DTYPE), S((t, d), MXU_DTYPE), S((8, d), F32)],
        name=name, compiler_params=_cp(("arbitrary",)))(h2, q, pe, tgt, gf)


def _gmlp_block(us, vs, lng, lnb, wss, bss, ng):
    n = us[0].shape[0]
    row = lax.broadcasted_iota(jnp.int32, (n, n), 0)
    col = lax.broadcasted_iota(jnp.int32, (n, n), 1)
    outs = []
    for u0, v0, lg, lb, ws, bs in zip(us, vs, lng, lnb, wss, bss):
        u = jax.nn.gelu(u0)
        v = jax.nn.gelu(v0)
        mu = jnp.mean(v, axis=-1, keepdims=True)
        var = jnp.mean(jnp.square(v - mu), axis=-1, keepdims=True)
        vn = (v - mu) * lax.rsqrt(var + EPS) * lg + lb
        w = jnp.where(row >= col, ws, 0.0)
        sg = jnp.dot(_mx(w), _mx(vn), preferred_element_type=F32) + bs
        outs.append(u * sg)
    return _rms(jnp.concatenate(outs, axis=1), ng)


def _gmlp_load(proj_ref, lng_ref, lnb_ref, ws_ref, bst_ref, d_a, ng):
    sl = lambda g: slice(CHUNK * g, CHUNK * (g + 1))
    us = [proj_ref[:, sl(g)].astype(F32) for g in range(ng)]
    vs = [proj_ref[:, d_a + CHUNK * g: d_a + CHUNK * (g + 1)].astype(F32) for g in range(ng)]
    lng = [lng_ref[:, sl(g)] for g in range(ng)]
    lnb = [lnb_ref[:, sl(g)] for g in range(ng)]
    wss = [ws_ref[g] for g in range(ng)]
    bss = [bst_ref[:, g:g + 1] for g in range(ng)]
    return us, vs, lng, lnb, wss, bss


def _gmlp_fwd(proj, ln_g, ln_b, w_s, bst, norm_g, d_mix, name):
    t = proj.shape[0]
    ng = w_s.shape[0]
    d_a = ng * CHUNK

    def body(proj_ref, lng_ref, lnb_ref, ws_ref, bst_ref, ng_ref, o_ref, ot_ref):
        args = _gmlp_load(proj_ref, lng_ref, lnb_ref, ws_ref, bst_ref, d_a, ng)
        y = _gmlp_block(*args, ng_ref[...]).astype(o_ref.dtype)
        o_ref[...] = y
        ot_ref[...] = y.T

    vec = pl.BlockSpec((1, d_a), lambda c: (0, 0))
    return pl.pallas_call(
        body, grid=(t // CHUNK,),
        in_specs=[pl.BlockSpec((CHUNK, 2 * d_a), lambda c: (c, 0)), vec, vec,
                  pl.BlockSpec((ng, CHUNK, CHUNK), lambda c: (0, 0, 0)), pl.BlockSpec((CHUNK, 128), lambda c: (0, 0)), vec],
        out_specs=[pl.BlockSpec((CHUNK, d_a), lambda c: (c, 0)), pl.BlockSpec((d_a, CHUNK), lambda c: (0, c))],
        out_shape=[S((t, d_mix), ACT_DTYPE), S((d_mix, t), ACT_DTYPE)],
        name=name, compiler_params=_cp(("parallel",)))(proj, ln_g, ln_b, w_s, bst, norm_g)


def _gmlp_bwd(proj, ln_g, ln_b, w_s, bst, norm_g, dcat, d_proj, name):
    t = proj.shape[0]
    ng = w_s.shape[0]
    d_a = ng * CHUNK

    def body(proj_ref, lng_ref, lnb_ref, ws_ref, bst_ref, ng_ref, dy_ref,
             dp_ref, dlng_ref, dlnb_ref, dws_ref, dbst_ref, dng_ref):
        @pl.when(pl.program_id(0) == 0)
        def _():
            for r in (dlng_ref, dlnb_ref, dws_ref, dbst_ref, dng_ref):
                r[...] = jnp.zeros_like(r)

        args = _gmlp_load(proj_ref, lng_ref, lnb_ref, ws_ref, bst_ref, d_a, ng)
        _, vjp = jax.vjp(_gmlp_block, *args, ng_ref[...])
        dus, dvs, dlng, dlnb, dwss, dbss, dng = vjp(dy_ref[...].astype(F32))
        lane = lax.broadcasted_iota(jnp.int32, (1, 128), 1)
        dbst = jnp.zeros((CHUNK, 128), F32)
        for g in range(ng):
            dp_ref[:, CHUNK * g:CHUNK * (g + 1)] = dus[g].astype(dp_ref.dtype)
            dp_ref[:, d_a + CHUNK * g:d_a + CHUNK * (g + 1)] = dvs[g].astype(dp_ref.dtype)
            dlng_ref[0:1, CHUNK * g:CHUNK * (g + 1)] += dlng[g]
            dlnb_ref[0:1, CHUNK * g:CHUNK * (g + 1)] += dlnb[g]
            dws_ref[g] += dwss[g]
            dbst = dbst + dbss[g] * (lane == g).astype(F32)
        dbst_ref[...] += dbst
        dng_ref[0:1, :] += dng

    vec = pl.BlockSpec((1, d_a), lambda c: (0, 0))
    acc = pl.BlockSpec((8, d_a), lambda c: (0, 0))
    wspec = pl.BlockSpec((ng, CHUNK, CHUNK), lambda c: (0, 0, 0))
    bspec = pl.BlockSpec((CHUNK, 128), lambda c: (0, 0))
    return pl.pallas_call(
        body, grid=(t // CHUNK,),
        in_specs=[pl.BlockSpec((CHUNK, 2 * d_a), lambda c: (c, 0)), vec, vec, wspec, bspec, vec,
                  pl.BlockSpec((CHUNK, d_a), lambda c: (c, 0))],
        out_specs=[pl.BlockSpec((CHUNK, 2 * d_a), lambda c: (c, 0)), acc, acc, wspec, bspec, acc],
        out_shape=[S((t, d_proj), ACT_DTYPE), S((8, d_a), F32), S((8, d_a), F32), S((ng, CHUNK, CHUNK), F32),
                   S((CHUNK, 128), F32), S((8, d_a), F32)],
        name=name, compiler_params=_cp(("arbitrary",)))(proj, ln_g, ln_b, w_s, bst, norm_g, dcat)


def _silu_grad(c):
    s = jax.nn.sigmoid(c)
    return s * (1.0 + c * (1.0 - s))


def _fill_prev_main(s_ref, prev_ref, main_ref, i, tt):
    s_ref[pl.ds(0, HALO), :] = jnp.where(i > 0, prev_ref[...].astype(F32), 0.0)
    s_ref[pl.ds(HALO, tt), :] = main_ref[...].astype(F32)


def _prev_spec(tt, tc, joff):
    return pl.BlockSpec((HALO, tc), lambda j, i: (jnp.maximum(i * (tt // HALO) - 1, 0), j + joff))


def _next_spec(tt, tc, joff, t):
    return pl.BlockSpec((HALO, tc), lambda j, i: (jnp.minimum((i + 1) * (tt // HALO), t // HALO - 1), j + joff))


CONV_RC = 32


def _conv_tiles(t, c):
    return _pick(t, (1024, 512, 256, 128)), _pick(c, (256, 128))


def _row_chunks(tt, fn, init=0):
    rc = min(CONV_RC, tt)
    return lax.fori_loop(0, tt // rc, lambda q, c: fn(pl.multiple_of(q * rc, rc), rc, c), init)


def _fold8(p):
    acc = p[0:8]
    for r in range(8, p.shape[0], 8):
        acc = acc + p[r:r + 8]
    return acc


def _taps_chunk(s_ref, w_ref, kw, r0, rc):
    xe = s_ref[pl.ds(HALO - 8 + r0, rc + 8), :]
    acc = w_ref[0:1, :] * xe[8 - (kw - 1):8 - (kw - 1) + rc]
    for k in range(1, kw):
        acc = acc + w_ref[k:k + 1, :] * xe[8 - (kw - 1) + k:8 - (kw - 1) + k + rc]
    return acc


def _conv_bwd_chunk(sd_ref, x, w_ref, kw, r0, rc, dws):
    de = sd_ref[pl.ds(r0, rc + 8), :]
    dx, out = None, list(dws)
    for j in range(kw):
        d = de[j:j + rc]
        k = kw - 1 - j
        term = w_ref[k:k + 1, :] * d
        dx = term if dx is None else dx + term
        out[k] = out[k] + _fold8(x * d)
    return dx, out


def _conv_ssm_fwd(proj, col0, w, b, name):
    t = proj.shape[0]
    kw, c = w.shape
    tt, tc = _conv_tiles(t, c)
    joff = col0 // tc
    assert col0 % tc == 0

    def body(x_ref, xp_ref, w_ref, b_ref, o_ref, c_ref, s_ref):
        _fill_prev_main(s_ref, xp_ref, x_ref, pl.program_id(1), tt)

        def chunk(r0, rc, carry):
            cpre = _taps_chunk(s_ref, w_ref, kw, r0, rc) + b_ref[...]
            o_ref[pl.ds(r0, rc), :] = jax.nn.silu(cpre).astype(o_ref.dtype)
            c_ref[pl.ds(r0, rc), :] = cpre.astype(c_ref.dtype)
            return carry

        _row_chunks(tt, chunk)

    out = pl.BlockSpec((tt, tc), lambda j, i: (i, j))
    return pl.pallas_call(
        body, grid=(c // tc, t // tt),
        in_specs=[pl.BlockSpec((tt, tc), lambda j, i: (i, j + joff)), _prev_spec(tt, tc, joff),
                  pl.BlockSpec((kw, tc), lambda j, i: (0, j)), pl.BlockSpec((1, tc), lambda j, i: (0, j))],
        out_specs=[out, out], out_shape=[S((t, c), XBC_DTYPE), S((t, c), ACT_DTYPE)],
        scratch_shapes=[pltpu.VMEM((HALO + tt, tc), F32)], name=name,
        compiler_params=_cp(("parallel", "arbitrary")))(proj, proj, w, b)


def _conv_ssm_bwd(proj, col0, w, cpre, wcol0, dact, dproj, out_col0, name):
    t = proj.shape[0]
    kw = w.shape[0]
    c = dact.shape[1]
    tt, tc = _conv_tiles(t, c)
    assert col0 % tc == 0 and wcol0 % tc == 0 and out_col0 % tc == 0
    joff, wj, oj = col0 // tc, wcol0 // tc, out_col0 // tc
    nt = t // tt

    def body(x_ref, w_ref, c_ref, cn_ref, d_ref, dn_ref, dp_in, dx_ref, dw_ref, db_ref, sd_ref):
        del dp_in
        i = pl.program_id(1)

        @pl.when(i == 0)
        def _():
            dw_ref[...] = jnp.zeros_like(dw_ref)
            db_ref[...] = jnp.zeros_like(db_ref)

        def stage(r0, rc, db):
            rows = pl.ds(r0, rc)
            d = d_ref[rows, :].astype(F32) * _silu_grad(c_ref[rows, :].astype(F32))
            sd_ref[rows, :] = d
            return db + _fold8(d)

        zero8 = jnp.zeros((8, tc), F32)
        db = _row_chunks(tt, stage, zero8)
        sd_ref[pl.ds(tt, HALO), :] = jnp.where(
            i < nt - 1, dn_ref[...].astype(F32) * _silu_grad(cn_ref[...].astype(F32)), 0.0)

        def chunk(r0, rc, dws):
            dx, dws = _conv_bwd_chunk(sd_ref, x_ref[pl.ds(r0, rc), :].astype(F32), w_ref, kw, r0, rc, dws)
            dx_ref[pl.ds(r0, rc), :] = dx.astype(dx_ref.dtype)
            return dws

        dws = _row_chunks(tt, chunk, [zero8] * kw)
        for k in range(kw):
            dw_ref[k:k + 1, :] += jnp.sum(dws[k], axis=0, keepdims=True)
        db_ref[0:1, :] += jnp.sum(db, axis=0, keepdims=True)

    acc = pl.BlockSpec((8, tc), lambda j, i: (0, j))
    return pl.pallas_call(
        body, grid=(c // tc, nt),
        in_specs=[pl.BlockSpec((tt, tc), lambda j, i: (i, j + joff)), pl.BlockSpec((kw, tc), lambda j, i: (0, j + wj)),
                  pl.BlockSpec((tt, tc), lambda j, i: (i, j + wj)), _next_spec(tt, tc, wj, t),
                  pl.BlockSpec((tt, tc), lambda j, i: (i, j)), _next_spec(tt, tc, 0, t),
                  pl.BlockSpec(memory_space=pl.ANY)],
        out_specs=[pl.BlockSpec((tt, tc), lambda j, i: (i, j + oj)), acc, acc],
        out_shape=[S(dproj.shape, dproj.dtype), S((8, c), F32), S((8, c), F32)],
        scratch_shapes=[pltpu.VMEM((tt + HALO, tc), F32)],
        input_output_aliases={6: 0}, name=name,
        compiler_params=_cp(("parallel", "arbitrary")))(proj, w, cpre, cpre, dact, dact, dproj)


def _conv_ffn_fwd(hid, w, b, name):
    t, f2 = hid.shape
    f = f2 // 2
    kw = w.shape[0]
    tt, tc = _conv_tiles(t, f)
    nj = f // tc

    def body(g_ref, gp_ref, u_ref, up_ref, wg_ref, wu_ref, bg_ref, bu_ref, o_ref, ot_ref, cv_ref, sg_ref, su_ref):
        i = pl.program_id(1)
        _fill_prev_main(sg_ref, gp_ref, g_ref, i, tt)
        _fill_prev_main(su_ref, up_ref, u_ref, i, tt)

        def chunk(r0, rc, carry):
            rows = pl.ds(r0, rc)
            gate = _taps_chunk(sg_ref, wg_ref, kw, r0, rc) + bg_ref[...]
            up = _taps_chunk(su_ref, wu_ref, kw, r0, rc) + bu_ref[...]
            o_ref[rows, :] = (jax.nn.silu(gate) * up).astype(o_ref.dtype)
            cv_ref[0, rows, :] = gate.astype(cv_ref.dtype)
            cv_ref[1, rows, :] = up.astype(cv_ref.dtype)
            return carry

        _row_chunks(tt, chunk)
        ot_ref[...] = o_ref[...].T

    return pl.pallas_call(
        body, grid=(nj, t // tt),
        in_specs=[pl.BlockSpec((tt, tc), lambda j, i: (i, j)), _prev_spec(tt, tc, 0),
                  pl.BlockSpec((tt, tc), lambda j, i: (i, j + nj)), _prev_spec(tt, tc, nj),
                  pl.BlockSpec((kw, tc), lambda j, i: (0, j)), pl.BlockSpec((kw, tc), lambda j, i: (0, j + nj)),
                  pl.BlockSpec((1, tc), lambda j, i: (0, j)), pl.BlockSpec((1, tc), lambda j, i: (0, j + nj))],
        out_specs=[pl.BlockSpec((tt, tc), lambda j, i: (i, j)), pl.BlockSpec((tc, tt), lambda j, i: (j, i)),
                   pl.BlockSpec((2, tt, tc), lambda j, i: (0, i, j))],
        out_shape=[S((t, f), ACT_DTYPE), S((f, t), ACT_DTYPE), S((2, t, f), ACT_DTYPE)],
        scratch_shapes=[pltpu.VMEM((HALO + tt, tc), F32), pltpu.VMEM((HALO + tt, tc), F32)], name=name,
        compiler_params=_cp(("parallel", "arbitrary")))(hid, hid, hid, hid, w, w, b, b)


def _conv_ffn_bwd(hid, w, cv, dact, name):
    t, f2 = hid.shape
    f = f2 // 2
    kw = w.shape[0]
    tt, tc = _conv_tiles(t, f)
    nj = f // tc
    nt = t // tt

    def body(g_ref, u_ref, wg_ref, wu_ref, cv_ref, cvn_ref, d_ref, dn_ref, dh_ref, dw_ref, db_ref, dg_ref, du_ref):
        i = pl.program_id(1)

        @pl.when(i == 0)
        def _():
            dw_ref[...] = jnp.zeros_like(dw_ref)
            db_ref[...] = jnp.zeros_like(db_ref)

        def cotangents(gate, up, dact_v):
            sg = jax.nn.sigmoid(gate)
            return dact_v * up * (sg * (1.0 + gate * (1.0 - sg))), dact_v * (gate * sg)

        def stage(r0, rc, dbs):
            rows = pl.ds(r0, rc)
            dg, du = cotangents(cv_ref[0, rows, :].astype(F32), cv_ref[1, rows, :].astype(F32), d_ref[rows, :].astype(F32))
            dg_ref[rows, :] = dg
            du_ref[rows, :] = du
            return [dbs[0] + _fold8(dg), dbs[1] + _fold8(du)]

        zero8 = jnp.zeros((8, tc), F32)
        dbs = _row_chunks(tt, stage, [zero8, zero8])
        dgn, dun = cotangents(cvn_ref[0].astype(F32), cvn_ref[1].astype(F32), dn_ref[...].astype(F32))
        dg_ref[pl.ds(tt, HALO), :] = jnp.where(i < nt - 1, dgn, 0.0)
        du_ref[pl.ds(tt, HALO), :] = jnp.where(i < nt - 1, dun, 0.0)
        for s, (sd_ref, x_ref, w_ref) in enumerate(((dg_ref, g_ref, wg_ref), (du_ref, u_ref, wu_ref))):
            def chunk(r0, rc, dws, s=s, sd_ref=sd_ref, x_ref=x_ref, w_ref=w_ref):
                dx, dws = _conv_bwd_chunk(sd_ref, x_ref[pl.ds(r0, rc), :].astype(F32), w_ref, kw, r0, rc, dws)
                dh_ref[s, pl.ds(r0, rc), :] = dx.astype(dh_ref.dtype)
                return dws

            dws = _row_chunks(tt, chunk, [zero8] * kw)
            for k in range(kw):
                dw_ref[s, k:k + 1, :] += jnp.sum(dws[k], axis=0, keepdims=True)
            db_ref[s, 0:1, :] += jnp.sum(dbs[s], axis=0, keepdims=True)

    acc = pl.BlockSpec((2, 8, tc), lambda j, i: (0, 0, j))
    dsc = pltpu.VMEM((tt + HALO, tc), F32)
    nxt = lambda j, i: (0, jnp.minimum((i + 1) * (tt // HALO), t // HALO - 1), j)
    return pl.pallas_call(
        body, grid=(nj, nt),
        in_specs=[pl.BlockSpec((tt, tc), lambda j, i: (i, j)), pl.BlockSpec((tt, tc), lambda j, i: (i, j + nj)),
                  pl.BlockSpec((kw, tc), lambda j, i: (0, j)), pl.BlockSpec((kw, tc), lambda j, i: (0, j + nj)),
                  pl.BlockSpec((2, tt, tc), lambda j, i: (0, i, j)), pl.BlockSpec((2, HALO, tc), nxt),
                  pl.BlockSpec((tt, tc), lambda j, i: (i, j)), _next_spec(tt, tc, 0, t)],
        out_specs=[pl.BlockSpec((2, tt, tc), lambda j, i: (0, i, j)), acc, acc],
        out_shape=[S((2, t, f), MXU_DTYPE), S((2, 8, f), F32), S((2, 8, f), F32)],
        scratch_shapes=[dsc, dsc], name=name,
        compiler_params=_cp(("parallel", "arbitrary")))(hid, hid, w, w, cv, cv, dact, dact)


SSD_SUB = 2


def _ssd_chunk(xs, bm, cm, dtraw, hin, bias, alog, dskip):
    n = bm.shape[0]
    row = lax.broadcasted_iota(jnp.int32, (n, n), 0)
    col = lax.broadcasted_iota(jnp.int32, (n, n), 1)
    causal = row >= col
    lane = lax.broadcasted_iota(jnp.int32, (1, 128), 1)
    sub = lax.broadcasted_iota(jnp.int32, (128, 1), 0)
    last = (lax.broadcasted_iota(jnp.int32, (n, 1), 0) == n - 1).astype(F32)
    dt = jax.nn.softplus(dtraw + bias)
    adt = dt * (-jnp.exp(alog))
    tri = causal.astype(F32)
    acs = jnp.dot(tri, adt, preferred_element_type=F32, precision=lax.Precision.HIGHEST)
    ch = lax.broadcasted_iota(jnp.int32, (128, GROUP_CH), 1)
    hd = lax.broadcasted_iota(jnp.int32, (128, GROUP_CH), 0) * HEAD_DIM
    expand = ((ch >= hd) & (ch < hd + HEAD_DIM)).astype(F32)
    acs_x = jnp.dot(acs, expand, preferred_element_type=F32, precision=lax.Precision.HIGHEST)
    alast_x = jnp.sum(acs_x * last, axis=0, keepdims=True)
    acs_t = acs.T
    scores = lax.dot_general(_mx(cm), _mx(bm), (((1,), (1,)), ((), ())), preferred_element_type=F32)
    yds, xts = [], []
    for r in range(HEADS_PER_GROUP):
        pick = (lane == r).astype(F32)
        acol = jnp.sum(acs * pick, axis=1, keepdims=True)
        arow = jnp.sum(acs_t * (sub == r).astype(F32), axis=0, keepdims=True)
        dtc = jnp.sum(dt * pick, axis=1, keepdims=True)
        lm = jnp.exp(jnp.where(causal, acol - arow, -1e30))
        xts.append(xs[r] * dtc)
        yds.append(jnp.dot(_mx(scores * lm), _mx(xts[r]), preferred_element_type=F32))
    xt = jnp.concatenate(xts, axis=1)
    yo = jnp.exp(acs_x) * jnp.dot(_mx(cm), _mx(hin), preferred_element_type=F32)
    st = lax.dot_general(_mx(bm), _mx(xt * jnp.exp(alast_x - acs_x)), (((0,), (0,)), ((), ())), preferred_element_type=F32)
    hout = jnp.exp(alast_x) * hin + st
    return jnp.concatenate(yds, axis=1) + yo + dskip * jnp.concatenate(xs, axis=1), hout


def _ssd_post(y, z, normg):
    return _rms(y * jax.nn.silu(z), normg)


def _ssd_block(datas, hin, consts):
    ys = []
    for data in datas:
        y, hin = _ssd_chunk(*data, hin, *consts)
        ys.append(y)
    return ys, hin


def _ssd_specs(d_ssm, ngrp, zcol0, rev, nb):
    cc = (lambda c: nb - 1 - c) if rev else (lambda c: c)
    rows = SSD_SUB * CHUNK
    xj, bj, cj, zj = 0, d_ssm // 128, d_ssm // 128 + ngrp, zcol0 // GROUP_CH
    const = lambda w: pl.BlockSpec((None, 8, w), lambda g, c: (g, 0, 0))
    return cc, [
        pl.BlockSpec((rows, GROUP_CH), lambda g, c: (cc(c), xj + g)),
        pl.BlockSpec((rows, 128), lambda g, c: (cc(c), bj + g)),
        pl.BlockSpec((rows, 128), lambda g, c: (cc(c), cj + g)),
        pl.BlockSpec((rows, 128), lambda g, c: (cc(c), g)),
        pl.BlockSpec((rows, GROUP_CH), lambda g, c: (cc(c), zj + g)),
        const(128), const(128), const(GROUP_CH), const(GROUP_CH)]


def _sub_rows(s):
    return slice(CHUNK * s, CHUNK * (s + 1))


def _ssd_load(x_ref, b_ref, c_ref, dt_ref, z_ref, bias_ref, alog_ref, dsk_ref, ng_ref):
    datas, zs = [], []
    for s in range(SSD_SUB):
        rows = _sub_rows(s)
        xs = [x_ref[rows, HEAD_DIM * r:HEAD_DIM * (r + 1)].astype(F32) for r in range(HEADS_PER_GROUP)]
        datas.append((xs, b_ref[rows, :].astype(F32), c_ref[rows, :].astype(F32), dt_ref[rows, :]))
        zs.append(z_ref[rows, :].astype(F32))
    return datas, zs, (bias_ref[0:1, :], alog_ref[0:1, :], dsk_ref[0:1, :]), ng_ref[0:1, :]


def _ssd_fwd(xbc, dtp, proj, zcol0, bias_p, alog_p, dskip_x, normg_x, cat, cat_t, name):
    t = xbc.shape[0]
    ngrp = bias_p.shape[0]
    d_ssm = ngrp * GROUP_CH
    rows = SSD_SUB * CHUNK
    nb = t // rows
    d_a = cat.shape[1] - d_ssm
    assert d_a % GROUP_CH == 0 and zcol0 % GROUP_CH == 0 and t % rows == 0
    _, specs = _ssd_specs(d_ssm, ngrp, zcol0, False, nb)

    def body(x_ref, b_ref, c_ref, dt_ref, z_ref, bias_ref, alog_ref, dsk_ref, ng_ref, cat_in, catt_in,
             yn_ref, ynt_ref, y_ref, hs_ref, h_ref):
        del cat_in, catt_in

        @pl.when(pl.program_id(1) == 0)
        def _():
            h_ref[...] = jnp.zeros_like(h_ref)

        datas, zs, consts, normg = _ssd_load(x_ref, b_ref, c_ref, dt_ref, z_ref, bias_ref, alog_ref, dsk_ref, ng_ref)
        hs_ref[...] = h_ref[...]
        ys, hout = _ssd_block(datas, h_ref[...], consts)
        for s in range(SSD_SUB):
            y_ref[_sub_rows(s), :] = ys[s].astype(y_ref.dtype)
            yn = _ssd_post(ys[s], zs[s], normg).astype(yn_ref.dtype)
            yn_ref[_sub_rows(s), :] = yn
            ynt_ref[:, _sub_rows(s)] = yn.T
        h_ref[...] = hout

    hbm = pl.BlockSpec(memory_space=pl.ANY)
    return pl.pallas_call(
        body, grid=(ngrp, nb), in_specs=specs + [hbm, hbm],
        out_specs=[pl.BlockSpec((rows, GROUP_CH), lambda g, c: (c, d_a // GROUP_CH + g)),
                   pl.BlockSpec((GROUP_CH, rows), lambda g, c: (d_a // GROUP_CH + g, c)),
                   pl.BlockSpec((rows, GROUP_CH), lambda g, c: (c, g)),
                   pl.BlockSpec((None, None, D_STATE, GROUP_CH), lambda g, c: (c, g, 0, 0))],
        out_shape=[S(cat.shape, cat.dtype), S(cat_t.shape, cat_t.dtype), S((t, d_ssm), ACT_DTYPE),
                   S((nb, ngrp, D_STATE, GROUP_CH), F32)],
        scratch_shapes=[pltpu.VMEM((D_STATE, GROUP_CH), F32)],
        input_output_aliases={9: 0, 10: 1}, name=name,
        compiler_params=_cp(("parallel", "arbitrary")))(xbc, xbc, xbc, dtp, proj, bias_p, alog_p, dskip_x, normg_x, cat, cat_t)


def _ssd_bwd(xbc, dtp, proj, zcol0, bias_p, alog_p, dskip_x, normg_x, hs, ypre, dcat, dproj, name):
    t = xbc.shape[0]
    ngrp = bias_p.shape[0]
    d_ssm = ngrp * GROUP_CH
    rows = SSD_SUB * CHUNK
    nb = t // rows
    d_a = dcat.shape[1] - d_ssm
    cc, specs = _ssd_specs(d_ssm, ngrp, zcol0, True, nb)

    def body(x_ref, b_ref, c_ref, dt_ref, z_ref, bias_ref, alog_ref, dsk_ref, ng_ref, hs_ref, yp_ref, dy_ref, dp_in,
             dz_ref, dx_ref, db_ref, dc_ref, ddt_ref, dbias_ref, dalog_ref, ddsk_ref, dng_ref, dh_ref):
        del dp_in

        @pl.when(pl.program_id(1) == 0)
        def _():
            dh_ref[...] = jnp.zeros_like(dh_ref)
            for r in (dbias_ref, dalog_ref, ddsk_ref, dng_ref):
                r[...] = jnp.zeros_like(r)

        datas, zs, consts, normg = _ssd_load(x_ref, b_ref, c_ref, dt_ref, z_ref, bias_ref, alog_ref, dsk_ref, ng_ref)
        dys, dng = [], jnp.zeros_like(normg)
        for s in range(SSD_SUB):
            rws = _sub_rows(s)
            _, vjp_post = jax.vjp(_ssd_post, yp_ref[rws, :].astype(F32), zs[s], normg)
            dy, dz, dg = vjp_post(dy_ref[rws, :].astype(F32))
            dys.append(dy)
            dng = dng + dg
            dz_ref[rws, :] = dz.astype(dz_ref.dtype)
        _, vjp = jax.vjp(_ssd_block, datas, hs_ref[...], consts)
        ddatas, dhin, (dbias, dalog, ddsk) = vjp((dys, dh_ref[...]))
        for s, (dxs, dbm, dcm, ddt) in enumerate(ddatas):
            rws = _sub_rows(s)
            for r in range(HEADS_PER_GROUP):
                dx_ref[rws, HEAD_DIM * r:HEAD_DIM * (r + 1)] = dxs[r].astype(dx_ref.dtype)
            db_ref[rws, :] = dbm.astype(db_ref.dtype)
            dc_ref[rws, :] = dcm.astype(dc_ref.dtype)
            ddt_ref[rws, :] = ddt
        dh_ref[...] = dhin
        dbias_ref[0:1, :] += dbias
        dalog_ref[0:1, :] += dalog
        ddsk_ref[0:1, :] += ddsk
        dng_ref[0:1, :] += dng

    acc = lambda w: pl.BlockSpec((None, 8, w), lambda g, c: (g, 0, 0))
    blk = lambda w: pl.BlockSpec((rows, w), lambda g, c: (cc(c), g))
    return pl.pallas_call(
        body, grid=(ngrp, nb),
        in_specs=specs + [pl.BlockSpec((None, None, D_STATE, GROUP_CH), lambda g, c: (cc(c), g, 0, 0)),
                          blk(GROUP_CH),
                          pl.BlockSpec((rows, GROUP_CH), lambda g, c: (cc(c), d_a // GROUP_CH + g)),
                          pl.BlockSpec(memory_space=pl.ANY)],
        out_specs=[pl.BlockSpec((rows, GROUP_CH), lambda g, c: (cc(c), zcol0 // GROUP_CH + g)),
                   blk(GROUP_CH), blk(128), blk(128), blk(128), acc(128), acc(128), acc(GROUP_CH), acc(GROUP_CH)],
        out_shape=[S(dproj.shape, dproj.dtype), S((t, d_ssm), XBC_DTYPE), S((t, ngrp * 128), XBC_DTYPE),
                   S((t, ngrp * 128), XBC_DTYPE), S((t, ngrp * 128), F32), S((ngrp, 8, 128), F32),
                   S((ngrp, 8, 128), F32), S((ngrp, 8, GROUP_CH), F32), S((ngrp, 8, GROUP_CH), F32)],
        scratch_shapes=[pltpu.VMEM((D_STATE, GROUP_CH), F32)],
        input_output_aliases={12: 0}, name=name,
        compiler_params=_cp(("parallel", "arbitrary")))(xbc, xbc, xbc, dtp, proj, bias_p, alog_p, dskip_x, normg_x, hs, ypre,
                                                        dcat, dproj)


def _adamw(parts, w, m, v, name):
    r, c = w.shape
    tr = _pick(r, (256, 128, 64, 32, 16, 8)) if c * 4 * 256 <= 4 * 1024 * 1024 else _pick(r, (64, 32, 16, 8))

    def body(p_ref, w_ref, m_ref, v_ref, g_ref, d_ref, nm_ref, nv_ref):
        g = p_ref[0].astype(F32)
        for k in range(1, N_DEV):
            g = g + p_ref[k].astype(F32)
        mm = ADAM_B1 * m_ref[...] + (1.0 - ADAM_B1) * g
        vv = ADAM_B2 * v_ref[...] + (1.0 - ADAM_B2) * jnp.square(g)
        m_hat = mm / (1.0 - ADAM_B1 ** ADAM_STEP)
        v_hat = vv / (1.0 - ADAM_B2 ** ADAM_STEP)
        g_ref[...] = g
        d_ref[...] = -ADAM_LR * (m_hat / (jnp.sqrt(v_hat) + ADAM_EPS) + ADAM_WD * w_ref[...])
        nm_ref[...] = mm
        nv_ref[...] = vv

    blk = pl.BlockSpec((tr, c), lambda i: (i, 0))
    return pl.pallas_call(
        body, grid=(r // tr,),
        in_specs=[pl.BlockSpec((N_DEV, tr, c), lambda i: (0, i, 0)), blk, blk, blk],
        out_specs=[blk, blk, blk, blk], out_shape=[S((r, c), F32)] * 4, name=name,
        compiler_params=_cp(("parallel",)))(parts, w, m, v)


def _mesh_pos():
    return lax.axis_index("x"), lax.axis_index("y"), lax.axis_index("c")


def _peer(d, x, y, c):
    return (1 - x if (d >> 2) & 1 else x, 1 - y if (d >> 1) & 1 else y, 1 - c if d & 1 else c)


def _gather_two_level(blocks, name):
    n = len(blocks)

    def body(*refs):
        srcs, outs = refs[:n], refs[n:2 * n]
        send_sems, recv_sems, loc_sems = refs[2 * n:]
        x, y, c = _mesh_pos()
        lin = lambda px, py, pc: 4 * px + 2 * py + pc
        me, sibling = (x, y, c), (x, y, 1 - c)
        chips = [(1 - x, y), (x, 1 - y), (1 - x, 1 - y)]

        def copy(a, k, block, to, src=None):
            slab = outs[a].at[lin(*block)]
            return pltpu.make_async_remote_copy(
                src_ref=slab if src is None else src, dst_ref=slab, send_sem=send_sems.at[a, k],
                recv_sem=recv_sems.at[a, k], device_id=to, device_id_type=pl.DeviceIdType.MESH)

        mine = [pltpu.make_async_copy(srcs[a], outs[a].at[lin(*me)], loc_sems.at[a]) for a in range(n)]
        first = [copy(a, 0, me, sibling, src=srcs[a]) for a in range(n)]
        first += [copy(a, 1 + j, me, (*chip, c), src=srcs[a]) for j, chip in enumerate(chips) for a in range(n)]
        for cp in mine + first:
            cp.start()
        passed = []
        for j, chip in enumerate(chips):
            for a in range(n):
                copy(a, 1 + j, (*chip, c), me).wait_recv()
                passed.append(copy(a, 4 + j, (*chip, c), sibling))
                passed[-1].start()
        for a in range(n):
            copy(a, 0, sibling, me).wait_recv()
            for j, chip in enumerate(chips):
                copy(a, 4 + j, (*chip, 1 - c), me).wait_recv()
        for cp in first + passed:
            cp.wait_send()
        for cp in mine:
            cp.wait()

    hbm = pl.BlockSpec(memory_space=pl.ANY)
    return pl.pallas_call(
        body, in_specs=[hbm] * n, out_specs=[hbm] * n, out_shape=[S((N_DEV,) + b.shape, b.dtype) for b in blocks],
        scratch_shapes=[pltpu.SemaphoreType.DMA((n, N_DEV - 1)), pltpu.SemaphoreType.DMA((n, N_DEV - 1)),
                        pltpu.SemaphoreType.DMA((n,))],
        name=name, compiler_params=pltpu.CompilerParams(has_side_effects=True))(*blocks)


def _xfer_start(items, name, after=None):
    n = len(items)
    kinds = [k for k, _ in items]
    srcs = [pltpu.with_memory_space_constraint(a, pltpu.HBM) for _, a in items]
    land_shapes = [((N_DEV,) + a.shape if k == 'gather' else a.shape, a.dtype) for k, a in items]
    lands = [pltpu.with_memory_space_constraint(lax.empty(s, dt), pltpu.HBM) for s, dt in land_shapes]
    extra = [] if after is None else [after]

    def body(*refs):
        src_refs, land_refs = refs[:n], refs[n:2 * n]
        outs = refs[2 * n + len(extra):]
        sems = outs[:2 * n]
        token = outs[4 * n]
        x, y, c = _mesh_pos()
        me = 4 * x + 2 * y + c
        for a in range(n):
            for d in range(1, N_DEV):
                px, py, pc = _peer(d, x, y, c)
                src = src_refs[a] if kinds[a] == 'gather' else src_refs[a].at[4 * px + 2 * py + pc]
                pltpu.make_async_remote_copy(
                    src_ref=src, dst_ref=land_refs[a].at[me], send_sem=sems[2 * a].at[d - 1],
                    recv_sem=sems[2 * a + 1].at[d - 1], device_id=(px, py, pc),
                    device_id_type=pl.DeviceIdType.MESH).start()
        token[...] = jnp.zeros_like(token)

    hbm = pl.BlockSpec(memory_space=pltpu.HBM)
    sem = pl.BlockSpec(memory_space=pltpu.SEMAPHORE)
    out_shape = ([pltpu.SemaphoreType.DMA((N_DEV - 1,))] * (2 * n)
                 + [pltpu.HBM(a.shape, a.dtype) for a in srcs] + [pltpu.HBM(s, dt) for s, dt in land_shapes]
                 + [S((8, 128), F32)])
    res = pl.pallas_call(
        body, name=name, out_shape=out_shape,
        in_specs=[hbm] * (2 * n) + [pl.BlockSpec(memory_space=pl.ANY)] * len(extra),
        out_specs=[sem] * (2 * n) + [hbm] * (2 * n) + [pl.BlockSpec(memory_space=pltpu.VMEM)],
        input_output_aliases={**{a: 2 * n + a for a in range(n)}, **{n + a: 3 * n + a for a in range(n)}},
        compiler_params=pltpu.CompilerParams(has_side_effects=pltpu.SideEffectType.DATAFLOW_SIDE_EFFECTING),
    )(*srcs, *lands, *extra)
    return (kinds, res[:2 * n], res[2 * n:3 * n], res[3 * n:4 * n]), res[4 * n]


def _xfer_wait(handle, after, name):
    kinds, sems, src_thru, land_thru = handle
    n = len(kinds)

    def body(*refs):
        land_refs = refs[n:2 * n]
        sem_refs = refs[2 * n:4 * n]
        x, y, c = _mesh_pos()
        me = 4 * x + 2 * y + c
        for a in range(n):
            for d in range(1, N_DEV):
                slab = land_refs[a].at[me]
                cp = pltpu.make_async_remote_copy(
                    src_ref=slab, dst_ref=slab, send_sem=sem_refs[2 * a].at[d - 1], recv_sem=sem_refs[2 * a + 1].at[d - 1],
                    device_id=_peer(d, x, y, c), device_id_type=pl.DeviceIdType.MESH)
                cp.wait_send()
                cp.wait_recv()

    hbm = pl.BlockSpec(memory_space=pltpu.HBM)
    sem = pl.BlockSpec(memory_space=pltpu.SEMAPHORE)
    res = pl.pallas_call(
        body, name=name,
        out_shape=[pltpu.HBM(a.shape, a.dtype) for a in src_thru] + [pltpu.HBM(a.shape, a.dtype) for a in land_thru],
        in_specs=[hbm] * (2 * n) + [sem] * (2 * n) + [pl.BlockSpec(memory_space=pl.ANY)],
        out_specs=[hbm] * (2 * n), input_output_aliases={a: a for a in range(2 * n)},
        compiler_params=pltpu.CompilerParams(has_side_effects=pltpu.SideEffectType.DATAFLOW_SIDE_EFFECTING),
    )(*src_thru, *land_thru, *sems, after)
    x, y, c = _mesh_pos()
    me = 4 * x + 2 * y + c
    out = []
    for a in range(n):
        src = res[a]
        own = src[None] if kinds[a] == 'gather' else lax.dynamic_index_in_dim(src, me, 0, keepdims=True)
        out.append(lax.dynamic_update_index_in_dim(res[n + a], own, me, 0))
    return out


def _stack_to_full(kind, st):
    if kind == 'row':
        return st.reshape(st.shape[0] * st.shape[1], st.shape[2])
    return jnp.concatenate([st[k] for k in range(st.shape[0])], axis=1)


def _full_to_stack(kind, full):
    r, c = full.shape
    if kind == 'row':
        return full.reshape(N_DEV, r // N_DEV, c)
    w = c // N_DEV
    return jnp.stack([full[:, k * w:(k + 1) * w] for k in range(N_DEV)], axis=0)


SMALL_ROWS = 256


def _pack_small(named):
    layout = [(a.shape, a.size, -(-a.size // 1024) * 8) for a in named]
    total = -(-sum(nr for _, _, nr in layout) // SMALL_ROWS) * SMALL_ROWS * 128
    packed, off = None, 0
    for a, (_, n, nr) in zip(named, layout):
        part = jnp.pad(a.reshape(-1).astype(F32), (off, total - off - n))
        packed = part if packed is None else packed + part
        off += nr * 128
    return packed.reshape(total // 128, 128), layout


def _unpack_small(packed, layout):
    out, r0 = [], 0
    for shape, n, nr in layout:
        out.append(packed[r0:r0 + nr].reshape(-1)[:n].reshape(shape))
        r0 += nr
    return out


def _row0(acc):
    return acc[0]


def _local_step(x, p, tgt, sm, comm):
    t, d = x.shape
    h_n = sm['dt_bias'].shape[-1]
    ngrp = h_n // HEADS_PER_GROUP
    d_ssm = h_n * HEAD_DIM
    d_a = sm['ln_a_g'].shape[-1]
    d_mix = d_a + d_ssm
    d_xbc = sm['conv_ssm_b'].shape[-1]
    d_main = 2 * d_a + d_ssm + d_xbc
    assert d_xbc == d_ssm + 2 * ngrp * D_STATE and h_n <= 128
    zcol0, xcol0 = 2 * d_a, 2 * d_a + d_ssm
    vec = lambda v: v.reshape(1, -1)

    bst = jnp.pad(sm['b_s'].T, ((0, 0), (0, 128 - sm['b_s'].shape[0])))
    grp = lambda v, w: jnp.broadcast_to(jnp.pad(v.reshape(ngrp, 1, -1), ((0, 0), (0, 0), (0, w - v.size // ngrp))), (ngrp, 8, w))
    bias_p, alog_p = grp(sm['dt_bias'], 128), grp(sm['a_log'], 128)
    dskip_x = grp(jnp.repeat(sm['d_skip'], HEAD_DIM), GROUP_CH)
    normg_x = grp(sm['ssm_norm_g'], GROUP_CH)
    pad_dt = lambda v: jnp.pad(v[:, :h_n].reshape(t, ngrp, HEADS_PER_GROUP),
                               ((0, 0), (0, 0), (0, 128 - HEADS_PER_GROUP))).reshape(t, ngrp * 128)

    g_mix = vec(sm['norm_mix_g']) + comm.tok0
    a_n, a_t = _rms_fwd(x, g_mix, "rms_mix")
    wf = comm.weights('a', a_n)
    slabs = [wf['w_in'][k] for k in range(N_DEV)]
    w_main = jnp.concatenate(slabs[:-1] + [slabs[-1][:, :slabs[-1].shape[1] - h_n]], axis=1)
    w_dt = jnp.pad(slabs[-1][:, slabs[-1].shape[1] - h_n:], ((0, 0), (0, 128 - h_n)))
    proj = _mm_nn(a_n, w_main, out_dtype=ACT_DTYPE, name="mm_in")
    dtp = pad_dt(_mm_nn(a_n, w_dt, out_dtype=F32, name="mm_dt"))
    cat, cat_t = _gmlp_fwd(proj, vec(sm['ln_a_g']), vec(sm['ln_a_b']), sm['w_s'], bst, vec(sm['norm_a_g']), d_mix, "gmlp_fwd")
    xbc, cpre = _conv_ssm_fwd(proj, xcol0, wf['conv_ssm_w'], vec(sm['conv_ssm_b']), "conv_ssm_fwd")
    cat, cat_t, ypre, hs = _ssd_fwd(xbc, dtp, proj, zcol0, bias_p, alog_p, dskip_x, normg_x, cat, cat_t, "ssd_fwd")
    wf.update(comm.weights('b', hs))
    h1 = _mm_nn(cat, wf['w_out'], out_dtype=F32, name="mm_out", res=x)
    f_n, f_t = _rms_fwd(h1, vec(sm['norm_ffn_g']), "rms_ffn")
    hid = _mm_nn(f_n, wf['w_up'], out_dtype=ACT_DTYPE, name="mm_up")
    act, act_t, cv = _conv_ffn_fwd(hid, wf['conv_ffn_w'], vec(sm['conv_ffn_b']), "conv_ffn_fwd")
    h2 = _mm_nn(act, wf['w_down'], out_dtype=F32, name="mm_down", res=h1)
    r_n, r_t = _rms_fwd(h2, vec(sm['norm_ple_g']), "rms_ple")
    q = _mm_nn(r_n, wf['w_ple_gate'], out_dtype=ACT_DTYPE, name="mm_pg")
    p_m = p.astype(MXU_DTYPE)
    pe = _mm_nn(p_m, wf['w_ple'], out_dtype=ACT_DTYPE, name="mm_ple")

    loss, dh3, dq, dpe, dgf = _head(h2, q, pe, tgt, vec(sm['norm_final_g']), "head")
    wgrad = lambda act_t, g, name, **kw: _mm_nn(act_t, g, out_dtype=WIRE_DTYPE, name=name, wide=True, **kw)
    gs = {}
    gs['norm_final_g'] = _row0(dgf)
    g_ple = wgrad(p_m.T, dpe, "wg_ple")
    g_pg = wgrad(r_t, dq, "wg_pg")
    dr = _mm_nt(dq, wf['w_ple_gate'], out_dtype=ACT_DTYPE, name="dg_pg")
    dh2, dh2m, dg = _rms_bwd(h2, vec(sm['norm_ple_g']), dr, dh3, "rms_ple_bwd")
    gs['norm_ple_g'] = _row0(dg)
    g_down = wgrad(act_t, dh2m, "wg_down")
    tok = comm.send('1', {'w_ple': g_ple, 'w_ple_gate': g_pg, 'w_down': g_down})
    dact = _mm_nt(dh2m, wf['w_down'], out_dtype=ACT_DTYPE, name="dg_down")
    dhid, dcw, dcb = _conv_ffn_bwd(hid, wf['conv_ffn_w'] + tok, cv, dact, "conv_ffn_bwd")
    kf = wf['conv_ffn_w'].shape[0]
    g_cf = jnp.concatenate([dcw[0, :kf], dcw[1, :kf]], axis=1)
    gs['conv_ffn_b'] = jnp.concatenate([dcb[0, 0], dcb[1, 0]], axis=0)
    g_up = wgrad(f_t, dhid, "wg_up", b_split=2, out_slabs=N_DEV)
    df = _mm_nt(dhid, wf['w_up'], out_dtype=ACT_DTYPE, name="dg_up", a_split=2)
    dh1, dh1m, dg = _rms_bwd(h1, vec(sm['norm_ffn_g']), df, dh2, "rms_ffn_bwd")
    gs['norm_ffn_g'] = _row0(dg)
    g_out = wgrad(cat_t, dh1m, "wg_out")
    tok = comm.send('2', {'conv_ffn_w': g_cf, 'w_up': g_up, 'w_out': g_out}, stacked=('w_up',))
    dcat = _mm_nt(dh1m, wf['w_out'], out_dtype=ACT_DTYPE, name="dg_out")
    dproj, dlng, dlnb, dws, dbst, dng = _gmlp_bwd(proj, vec(sm['ln_a_g']) + tok, vec(sm['ln_a_b']), sm['w_s'], bst,
                                                  vec(sm['norm_a_g']), dcat, d_main, "gmlp_bwd")
    gs['ln_a_g'], gs['ln_a_b'], gs['w_s'], gs['norm_a_g'] = _row0(dlng), _row0(dlnb), dws, _row0(dng)
    gs['b_s'] = dbst[:, :sm['b_s'].shape[0]].T
    dproj, dxs, dbm, dcm, ddtp, dbias, dalog, ddsk, dsng = _ssd_bwd(
        xbc, dtp, proj, zcol0, bias_p, alog_p, dskip_x, normg_x, hs, ypre, dcat, dproj, "ssd_bwd")
    gs['dt_bias'] = dbias[:, 0, :HEADS_PER_GROUP].reshape(h_n)
    gs['a_log'] = dalog[:, 0, :HEADS_PER_GROUP].reshape(h_n)
    gs['d_skip'] = ddsk[:, 0, :].reshape(h_n, HEAD_DIM).sum(axis=-1)
    gs['ssm_norm_g'] = dsng[:, 0, :].reshape(d_ssm)
    dws_c, dbs_c = [], []
    off = 0
    for nm, dpart in (("x", dxs), ("b", dbm), ("c", dcm)):
        dproj, dw_c, db_c = _conv_ssm_bwd(proj, xcol0 + off, wf['conv_ssm_w'], cpre, off, dpart, dproj,
                                          xcol0 + off, "conv_ssm_bwd_" + nm)
        dws_c.append(dw_c[:wf['conv_ssm_w'].shape[0]])
        dbs_c.append(db_c[0])
        off += dpart.shape[1]
    g_cs = jnp.concatenate(dws_c, axis=1)
    gs['conv_ssm_b'] = jnp.concatenate(dbs_c, axis=0)
    ddt = jnp.pad(ddtp.reshape(t, ngrp, 128)[:, :, :HEADS_PER_GROUP].reshape(t, h_n), ((0, 0), (0, 128 - h_n))).astype(MXU_DTYPE)
    g_in = jnp.concatenate([wgrad(a_t, dproj, "wg_in"), wgrad(a_t, ddt, "wg_dt")[:, :h_n]], axis=1)
    tok = comm.send('3', {'conv_ssm_w': g_cs, 'w_in': g_in}, [(n, gs[n]) for n in REPLICATED if n != 'norm_mix_g'])
    da = _mm_nt(ddt + tok.astype(ddt.dtype), w_dt, out_dtype=F32, name="dg_dt")
    da = _mm_nt(dproj, w_main, out_dtype=ACT_DTYPE, name="dg_in", res=da)
    dx, dg = _rms_bwd(x, g_mix + tok, da, dh1, "rms_mix_bwd", mx_copy=False)
    comm.send('4', {}, [('norm_mix_g', _row0(dg)), ('loss', loss[0, 0:1])])
    return dx


def kernel(x, p, norm_mix_g, w_in, ln_a_g, ln_a_b, w_s, b_s, norm_a_g, conv_ssm_w, conv_ssm_b, dt_bias, a_log, d_skip, ssm_norm_g, w_out, norm_ffn_g, w_up, conv_ffn_w, conv_ffn_b, w_down, norm_ple_g, w_ple_gate, w_ple, norm_final_g, loss_target, m_norm_mix_g, m_w_in, m_ln_a_g, m_ln_a_b, m_w_s, m_b_s, m_norm_a_g, m_conv_ssm_w, m_conv_ssm_b, m_dt_bias, m_a_log, m_d_skip, m_ssm_norm_g, m_w_out, m_norm_ffn_g, m_w_up, m_conv_ffn_w, m_conv_ffn_b, m_w_down, m_norm_ple_g, m_w_ple_gate, m_w_ple, m_norm_final_g, v_norm_mix_g, v_w_in, v_ln_a_g, v_ln_a_b, v_w_s, v_b_s, v_norm_a_g, v_conv_ssm_w, v_conv_ssm_b, v_dt_bias, v_a_log, v_d_skip, v_ssm_norm_g, v_w_out, v_norm_ffn_g, v_w_up, v_conv_ffn_w, v_conv_ffn_b, v_w_down, v_norm_ple_g, v_w_ple_gate, v_w_ple, v_norm_final_g):
    given = dict(locals())
    wts = {n: given[n] for n in WEIGHTS}
    ms = {n: given["m_" + n] for n in WEIGHTS}
    vs = {n: given["v_" + n] for n in WEIGHTS}
    sm = {n: (wts[n][0] if wts[n].ndim > 1 else wts[n]) for n in REPLICATED}
    comm = _Comm({n: wts[n][0] for n in SHARDED})
    dx = _local_step(x[0], p[0, 0], loss_target[0], sm, comm)

    out, loss_out, after = {}, None, dx
    for tag, names, small_names, layout, handle in comm.sent:
        recv = _xfer_wait(handle, after, "grads_%s_wait" % tag)
        for n, parts in zip(names, recv):
            out[n] = _adamw(parts, wts[n][0], ms[n][0], vs[n][0], "adamw_" + n)
            after = out[n][1]
        if small_names:
            pick = lambda src, fill: _pack_small([src[n] if n in src else jnp.full((1,), fill, F32) for n in small_names])[0]
            res = _adamw(recv[-1], pick(wts, 0.0), pick(ms, 0.0), pick(vs, 1.0), "adamw_small_" + tag)
            res = [_unpack_small(o, layout) for o in res]
            after = res[1][0]
            for i, n in enumerate(small_names):
                if n == 'loss':
                    loss_out = res[0][i].reshape(())
                else:
                    out[n] = [res[k][i] for k in range(4)]
    return (loss_out, dx[None], *[out[n][k].reshape(wts[n].shape) for k in range(4) for n in WEIGHTS])


class _Comm:
    GATHER_GROUPS = {'a': ('w_in', 'conv_ssm_w'), 'b': ('w_out', 'w_up', 'conv_ffn_w', 'w_down', 'w_ple_gate', 'w_ple')}

    def __init__(self, blocks):
        wired = lambda grp: [blocks[n].astype(_wire(n)) for n in self.GATHER_GROUPS[grp]]
        self.stacks_a = _gather_two_level(wired('a'), "gather_a")
        self.handle_b, tok = _xfer_start([('gather', b) for b in wired('b')], "gather_b_start", after=self.stacks_a[0])
        self.tok0 = tok[0, 0]
        self.sent = []

    def weights(self, grp, after):
        stacks = self.stacks_a if grp == 'a' else _xfer_wait(self.handle_b, after, "gather_b_wait")
        return {n: st if n == 'w_in' else _stack_to_full(SHARDED[n], st) for n, st in zip(self.GATHER_GROUPS[grp], stacks)}

    def send(self, tag, gw, small=None, stacked=()):
        items = [('scatter', g if n in stacked else _full_to_stack(SHARDED[n], g.astype(_wire(n)))) for n, g in gw.items()]
        layout, small_names = None, []
        if small:
            packed, layout = _pack_small([a for _, a in small])
            small_names = [n for n, _ in small]
            items.append(('gather', packed))
        handle, tok = _xfer_start(items, "grads_%s_start" % tag)
        self.sent.append((tag, list(gw), small_names, layout, handle))
        return tok[0, 0]


def _wire(name):
    return F32 if name in F32_ON_WIRE else WIRE_DTYPE
```

```python
import functools

import jax
import jax.numpy as jnp
from jax import lax
from jax.experimental import pallas as pl
from jax.experimental.pallas import tpu as pltpu

F32 = jnp.float32
MXU_DTYPE = jnp.bfloat16
ACT_DTYPE = jnp.bfloat16
XBC_DTYPE = jnp.bfloat16
WIRE_DTYPE = jnp.bfloat16
EPS = 1e-6
CHUNK = 128
D_STATE = 128
HEAD_DIM = 64
HEADS_PER_GROUP = 4
GROUP_CH = HEAD_DIM * HEADS_PER_GROUP
HALO = 16
N_DEV = 8
VMEM_LIMIT = 56 * 1024 * 1024

ADAM_LR = 0.001
ADAM_B1 = 0.9
ADAM_B2 = 0.999
ADAM_EPS = 1e-08
ADAM_WD = 0.01
ADAM_STEP = 10

WEIGHTS = ['norm_mix_g', 'w_in', 'ln_a_g', 'ln_a_b', 'w_s', 'b_s', 'norm_a_g', 'conv_ssm_w', 'conv_ssm_b', 'dt_bias',
           'a_log', 'd_skip', 'ssm_norm_g', 'w_out', 'norm_ffn_g', 'w_up', 'conv_ffn_w', 'conv_ffn_b', 'w_down',
           'norm_ple_g', 'w_ple_gate', 'w_ple', 'norm_final_g']
SHARDED = {'w_in': 'col', 'conv_ssm_w': 'col', 'w_out': 'row', 'w_up': 'col', 'conv_ffn_w': 'col', 'w_down': 'row',
           'w_ple_gate': 'row', 'w_ple': 'col'}
F32_ON_WIRE = ('conv_ssm_w', 'conv_ffn_w')
REPLICATED = [n for n in WEIGHTS if n not in SHARDED]

S = jax.ShapeDtypeStruct


def _pick(dim, cands):
    for c in cands:
        if c <= dim and dim % c == 0:
            return c
    return dim


def _cp(sem, vmem=VMEM_LIMIT):
    return pltpu.CompilerParams(dimension_semantics=sem, vmem_limit_bytes=vmem)


def _mx(v):
    return v.astype(MXU_DTYPE)


def _rms(v, g):
    return v * lax.rsqrt(jnp.mean(v * v, axis=-1, keepdims=True) + EPS) * g


MM_VMEM_BUDGET = 42 * 1024 * 1024


def _mm_tiles(m, n, k, out_bytes, has_res, tn_cands=(512, 256, 128), k_mult=1, tm_cands=(1024, 512), tn_alts=2):
    tns = [c for c in tn_cands if c <= n and n % c == 0][:tn_alts] or [n]
    ks = k // k_mult
    best = None
    for tn in tns:
        for tm in [c for c in tm_cands if m % c == 0] or [_pick(m, (256, 128))]:
            for nk in range(1, ks // 128 + 1):
                if ks % nk or (ks // nk) % 128:
                    continue
                tk = ks // nk
                need = 2 * 2 * (tm * tk + tk * tn) + tm * tn * (4 + 2 * out_bytes + (8 if has_res else 0))
                if need <= MM_VMEM_BUDGET:
                    if best is None or (nk, -tm, -tn) < best[0]:
                        best = ((nk, -tm, -tn), (tm, tn, tk))
                    break
    return best[1] if best else (_pick(m, (512, 256, 128)), tns[0], _pick(ks, (128,)))


def _mm_body(dot, nk, has_res):
    def body(*refs):
        if has_res:
            a_ref, b_ref, r_ref, o_ref, acc_ref = refs
        else:
            a_ref, b_ref, o_ref, acc_ref = refs
            r_ref = None
        kk = pl.program_id(2)
        d = dot(a_ref[...], b_ref[...])

        def fin(acc):
            if r_ref is not None:
                acc = acc + r_ref[...]
            o_ref[...] = acc.astype(o_ref.dtype)

        if nk == 1:
            fin(d)
        else:
            @pl.when(kk == 0)
            def _():
                acc_ref[...] = d

            if nk > 2:
                @pl.when((kk > 0) & (kk < nk - 1))
                def _():
                    acc_ref[...] += d

            @pl.when(kk == nk - 1)
            def _():
                fin(acc_ref[...] + d)

    return body


def _mm_call(body, grid, a_spec, b_spec, tm, tn, m, n, out_dtype, name, args, res, out_slabs=1):
    in_specs = [a_spec, b_spec]
    if res is not None:
        in_specs.append(pl.BlockSpec((tm, tn), lambda i, j, kk: (i, j)))
        args = args + [res]
    if out_slabs == 1:
        out_spec, out_shape = pl.BlockSpec((tm, tn), lambda i, j, kk: (i, j)), S((m, n), out_dtype)
    else:
        out_spec, out_shape = pl.BlockSpec((None, tm, tn), lambda i, j, kk: (j, i, 0)), S((out_slabs, m, tn), out_dtype)
    return pl.pallas_call(
        body, grid=grid, in_specs=in_specs, out_specs=out_spec, out_shape=out_shape,
        scratch_shapes=[pltpu.VMEM((tm, tn), F32)], name=name,
        compiler_params=_cp(("parallel", "parallel", "arbitrary")))(*args)


def _mm_nn(a, b, *, out_dtype, name, res=None, b_split=1, wide=False, out_slabs=1):
    m, k = a.shape
    n = b.shape[1] if b_split == 1 else b.shape[2] * b_split
    tn_cands = (n // out_slabs,) if out_slabs > 1 else (1024, 512, 256, 128) if wide else (512, 256, 128)
    tm, tn, tk = _mm_tiles(m, n // b_split, k, jnp.dtype(out_dtype).itemsize, res is not None, tn_cands=tn_cands,
                           tn_alts=1 if wide or out_slabs > 1 else 2)
    assert out_slabs == 1 or (tn * out_slabs == n and tn % 128 == 0)
    nk = k // tk
    njs = (n // b_split) // tn
    body = _mm_body(lambda x, y: jnp.dot(x, y, preferred_element_type=F32), nk, res is not None)
    if b_split == 1:
        b_spec = pl.BlockSpec((tk, tn), lambda i, j, kk: (kk, j))
    else:
        b_spec = pl.BlockSpec((None, tk, tn), lambda i, j, kk: (j // njs, kk, j % njs))
    return _mm_call(body, (m // tm, n // tn, nk), pl.BlockSpec((tm, tk), lambda i, j, kk: (i, kk)), b_spec,
                    tm, tn, m, n, out_dtype, name, [a, b], res, out_slabs)


def _mm_nt(a, b, *, out_dtype, name, res=None, a_split=1):
    if a_split == 1:
        m, k = a.shape
    else:
        m, k = a.shape[1], a.shape[2] * a_split
    n = b.shape[0]
    tm, tn, tk = _mm_tiles(m, n, k, jnp.dtype(out_dtype).itemsize, res is not None, k_mult=a_split, tm_cands=(1024,))
    nk = k // tk
    nks = nk // a_split
    body = _mm_body(lambda x, y: lax.dot_general(x, y, (((1,), (1,)), ((), ())), preferred_element_type=F32),
                    nk, res is not None)
    if a_split == 1:
        a_spec = pl.BlockSpec((tm, tk), lambda i, j, kk: (i, kk))
    else:
        a_spec = pl.BlockSpec((None, tm, tk), lambda i, j, kk: (kk // nks, i, kk % nks))
    return _mm_call(body, (m // tm, n // tn, nk), a_spec, pl.BlockSpec((tn, tk), lambda i, j, kk: (j, kk)),
                    tm, tn, m, n, out_dtype, name, [a, b], res)


def _rms_fwd(x, g, name):
    t, d = x.shape
    tr = _pick(t, (512, 256, 128))

    def body(x_ref, g_ref, o_ref, ot_ref):
        y = _rms(x_ref[...], g_ref[...]).astype(o_ref.dtype)
        o_ref[...] = y
        ot_ref[...] = y.T

    return pl.pallas_call(
        body, grid=(t // tr,),
        in_specs=[pl.BlockSpec((tr, d), lambda i: (i, 0)), pl.BlockSpec((1, d), lambda i: (0, 0))],
        out_specs=[pl.BlockSpec((tr, d), lambda i: (i, 0)), pl.BlockSpec((d, tr), lambda i: (0, i))],
        out_shape=[S((t, d), ACT_DTYPE), S((d, t), ACT_DTYPE)], name=name,
        compiler_params=_cp(("parallel",)))(x, g)


def _rms_bwd(xin, g, dn, dres, name, mx_copy=True):
    t, d = xin.shape
    tr = _pick(t, (256, 128))

    def body(x_ref, g_ref, dn_ref, dr_ref, dx_ref, *rest):
        dg_ref = rest[-1]

        @pl.when(pl.program_id(0) == 0)
        def _():
            dg_ref[...] = jnp.zeros_like(dg_ref)

        _, vjp = jax.vjp(_rms, x_ref[...], g_ref[...])
        dx, dg = vjp(dn_ref[...].astype(F32))
        dx = dr_ref[...] + dx
        dx_ref[...] = dx
        if mx_copy:
            rest[0][...] = dx.astype(rest[0].dtype)
        dg_ref[0:1, :] += dg

    row = pl.BlockSpec((tr, d), lambda i: (i, 0))
    acc = pl.BlockSpec((8, d), lambda i: (0, 0))
    return pl.pallas_call(
        body, grid=(t // tr,),
        in_specs=[row, pl.BlockSpec((1, d), lambda i: (0, 0)), row, row],
        out_specs=[row, row, acc] if mx_copy else [row, acc],
        out_shape=[S((t, d), F32)] + ([S((t, d), MXU_DTYPE)] if mx_copy else []) + [S((8, d), F32)], name=name,
        compiler_params=_cp(("arbitrary",)))(xin, g, dn, dres)


def _head(h2, q, pe, tgt, gf, name):
    t, d = h2.shape
    tr = _pick(t, (256, 128))

    def f(h2v, qv, pev, gfv, tv):
        h3 = h2v + jax.nn.sigmoid(qv) * pev
        y = _rms(h3, gfv)
        return 0.5 * jnp.sum(jnp.mean(jnp.square(y - tv), axis=-1))

    def body(h2_ref, q_ref, pe_ref, t_ref, g_ref, loss_ref, dh_ref, dq_ref, dpe_ref, dg_ref):
        @pl.when(pl.program_id(0) == 0)
        def _():
            loss_ref[...] = jnp.zeros_like(loss_ref)
            dg_ref[...] = jnp.zeros_like(dg_ref)

        tv = t_ref[...]
        loss, vjp = jax.vjp(lambda a, b, c, e: f(a, b, c, e, tv), h2_ref[...], q_ref[...].astype(F32),
                            pe_ref[...].astype(F32), g_ref[...])
        dh, dq, dpe, dg = vjp(jnp.ones((), F32))
        loss_ref[...] += jnp.full(loss_ref.shape, loss, F32)
        dh_ref[...] = dh
        dq_ref[...] = dq.astype(dq_ref.dtype)
        dpe_ref[...] = dpe.astype(dpe_ref.dtype)
        dg_ref[0:1, :] += dg

    row = pl.BlockSpec((tr, d), lambda i: (i, 0))
    return pl.pallas_call(
        body, grid=(t // tr,),
        in_specs=[row, row, row, row, pl.BlockSpec((1, d), lambda i: (0, 0))],
        out_specs=[pl.BlockSpec((8, 128), lambda i: (0, 0)), row, row, row, pl.BlockSpec((8, d), lambda i: (0, 0))],
        out_shape=[S((8, 128), F32), S((t, d), F32), S((t, d), MXU_DTYPE), S((t, d), MXU_DTYPE), S((8, d), F32)],
        name=name, compiler_params=_cp(("arbitrary",)))(h2, q, pe, tgt, gf)


def _gmlp_block(us, vs, lng, lnb, wss, bss, ng):
    n = us[0].shape[0]
    row = lax.broadcasted_iota(jnp.int32, (n, n), 0)
    col = lax.broadcasted_iota(jnp.int32, (n, n), 1)
    outs = []
    for u0, v0, lg, lb, ws, bs in zip(us, vs, lng, lnb, wss, bss):
        u = jax.nn.gelu(u0)
        v = jax.nn.gelu(v0)
        mu = jnp.mean(v, axis=-1, keepdims=True)
        var = jnp.mean(jnp.square(v - mu), axis=-1, keepdims=True)
        vn = (v - mu) * lax.rsqrt(var + EPS) * lg + lb
        w = jnp.where(row >= col, ws, 0.0)
        sg = jnp.dot(_mx(w), _mx(vn), preferred_element_type=F32) + bs
        outs.append(u * sg)
    return _rms(jnp.concatenate(outs, axis=1), ng)


def _gmlp_load(proj_ref, lng_ref, lnb_ref, ws_ref, bst_ref, d_a, ng):
    sl = lambda g: slice(CHUNK * g, CHUNK * (g + 1))
    us = [proj_ref[:, sl(g)].astype(F32) for g in range(ng)]
    vs = [proj_ref[:, d_a + CHUNK * g: d_a + CHUNK * (g + 1)].astype(F32) for g in range(ng)]
    lng = [lng_ref[:, sl(g)] for g in range(ng)]
    lnb = [lnb_ref[:, sl(g)] for g in range(ng)]
    wss = [ws_ref[g] for g in range(ng)]
    bss = [bst_ref[:, g:g + 1] for g in range(ng)]
    return us, vs, lng, lnb, wss, bss


def _gmlp_fwd(proj, ln_g, ln_b, w_s, bst, norm_g, d_mix, name):
    t = proj.shape[0]
    ng = w_s.shape[0]
    d_a = ng * CHUNK

    def body(proj_ref, lng_ref, lnb_ref, ws_ref, bst_ref, ng_ref, o_ref, ot_ref):
        args = _gmlp_load(proj_ref, lng_ref, lnb_ref, ws_ref, bst_ref, d_a, ng)
        y = _gmlp_block(*args, ng_ref[...]).astype(o_ref.dtype)
        o_ref[...] = y
        ot_ref[...] = y.T

    vec = pl.BlockSpec((1, d_a), lambda c: (0, 0))
    return pl.pallas_call(
        body, grid=(t // CHUNK,),
        in_specs=[pl.BlockSpec((CHUNK, 2 * d_a), lambda c: (c, 0)), vec, vec,
                  pl.BlockSpec((ng, CHUNK, CHUNK), lambda c: (0, 0, 0)), pl.BlockSpec((CHUNK, 128), lambda c: (0, 0)), vec],
        out_specs=[pl.BlockSpec((CHUNK, d_a), lambda c: (c, 0)), pl.BlockSpec((d_a, CHUNK), lambda c: (0, c))],
        out_shape=[S((t, d_mix), ACT_DTYPE), S((d_mix, t), ACT_DTYPE)],
        name=name, compiler_params=_cp(("parallel",)))(proj, ln_g, ln_b, w_s, bst, norm_g)


def _gmlp_bwd(proj, ln_g, ln_b, w_s, bst, norm_g, dcat, d_proj, name):
    t = proj.shape[0]
    ng = w_s.shape[0]
    d_a = ng * CHUNK

    def body(proj_ref, lng_ref, lnb_ref, ws_ref, bst_ref, ng_ref, dy_ref,
             dp_ref, dlng_ref, dlnb_ref, dws_ref, dbst_ref, dng_ref):
        @pl.when(pl.program_id(0) == 0)
        def _():
            for r in (dlng_ref, dlnb_ref, dws_ref, dbst_ref, dng_ref):
                r[...] = jnp.zeros_like(r)

        args = _gmlp_load(proj_ref, lng_ref, lnb_ref, ws_ref, bst_ref, d_a, ng)
        _, vjp = jax.vjp(_gmlp_block, *args, ng_ref[...])
        dus, dvs, dlng, dlnb, dwss, dbss, dng = vjp(dy_ref[...].astype(F32))
        lane = lax.broadcasted_iota(jnp.int32, (1, 128), 1)
        dbst = jnp.zeros((CHUNK, 128), F32)
        for g in range(ng):
            dp_ref[:, CHUNK * g:CHUNK * (g + 1)] = dus[g].astype(dp_ref.dtype)
            dp_ref[:, d_a + CHUNK * g:d_a + CHUNK * (g + 1)] = dvs[g].astype(dp_ref.dtype)
            dlng_ref[0:1, CHUNK * g:CHUNK * (g + 1)] += dlng[g]
            dlnb_ref[0:1, CHUNK * g:CHUNK * (g + 1)] += dlnb[g]
            dws_ref[g] += dwss[g]
            dbst = dbst + dbss[g] * (lane == g).astype(F32)
        dbst_ref[...] += dbst
        dng_ref[0:1, :] += dng

    vec = pl.BlockSpec((1, d_a), lambda c: (0, 0))
    acc = pl.BlockSpec((8, d_a), lambda c: (0, 0))
    wspec = pl.BlockSpec((ng, CHUNK, CHUNK), lambda c: (0, 0, 0))
    bspec = pl.BlockSpec((CHUNK, 128), lambda c: (0, 0))
    return pl.pallas_call(
        body, grid=(t // CHUNK,),
        in_specs=[pl.BlockSpec((CHUNK, 2 * d_a), lambda c: (c, 0)), vec, vec, wspec, bspec, vec,
                  pl.BlockSpec((CHUNK, d_a), lambda c: (c, 0))],
        out_specs=[pl.BlockSpec((CHUNK, 2 * d_a), lambda c: (c, 0)), acc, acc, wspec, bspec, acc],
        out_shape=[S((t, d_proj), ACT_DTYPE), S((8, d_a), F32), S((8, d_a), F32), S((ng, CHUNK, CHUNK), F32),
                   S((CHUNK, 128), F32), S((8, d_a), F32)],
        name=name, compiler_params=_cp(("arbitrary",)))(proj, ln_g, ln_b, w_s, bst, norm_g, dcat)


def _silu_grad(c):
    s = jax.nn.sigmoid(c)
    return s * (1.0 + c * (1.0 - s))


def _fill_prev_main(s_ref, prev_ref, main_ref, i, tt):
    s_ref[pl.ds(0, HALO), :] = jnp.where(i > 0, prev_ref[...].astype(F32), 0.0)
    s_ref[pl.ds(HALO, tt), :] = main_ref[...].astype(F32)


def _prev_spec(tt, tc, joff):
    return pl.BlockSpec((HALO, tc), lambda j, i: (jnp.maximum(i * (tt // HALO) - 1, 0), j + joff))


def _next_spec(tt, tc, joff, t):
    return pl.BlockSpec((HALO, tc), lambda j, i: (jnp.minimum((i + 1) * (tt // HALO), t // HALO - 1), j + joff))


CONV_RC = 32


def _conv_tiles(t, c):
    return _pick(t, (1024, 512, 256, 128)), _pick(c, (256, 128))


def _row_chunks(tt, fn, init=0):
    rc = min(CONV_RC, tt)
    return lax.fori_loop(0, tt // rc, lambda q, c: fn(pl.multiple_of(q * rc, rc), rc, c), init)


def _fold8(p):
    acc = p[0:8]
    for r in range(8, p.shape[0], 8):
        acc = acc + p[r:r + 8]
    return acc


def _taps_chunk(s_ref, w_ref, kw, r0, rc):
    xe = s_ref[pl.ds(HALO - 8 + r0, rc + 8), :]
    acc = w_ref[0:1, :] * xe[8 - (kw - 1):8 - (kw - 1) + rc]
    for k in range(1, kw):
        acc = acc + w_ref[k:k + 1, :] * xe[8 - (kw - 1) + k:8 - (kw - 1) + k + rc]
    return acc


def _conv_bwd_chunk(sd_ref, x, w_ref, kw, r0, rc, dws):
    de = sd_ref[pl.ds(r0, rc + 8), :]
    dx, out = None, list(dws)
    for j in range(kw):
        d = de[j:j + rc]
        k = kw - 1 - j
        term = w_ref[k:k + 1, :] * d
        dx = term if dx is None else dx + term
        out[k] = out[k] + _fold8(x * d)
    return dx, out


def _conv_ssm_fwd(proj, col0, w, b, name):
    t = proj.shape[0]
    kw, c = w.shape
    tt, tc = _conv_tiles(t, c)
    joff = col0 // tc
    assert col0 % tc == 0

    def body(x_ref, xp_ref, w_ref, b_ref, o_ref, c_ref, s_ref):
        _fill_prev_main(s_ref, xp_ref, x_ref, pl.program_id(1), tt)

        def chunk(r0, rc, carry):
            cpre = _taps_chunk(s_ref, w_ref, kw, r0, rc) + b_ref[...]
            o_ref[pl.ds(r0, rc), :] = jax.nn.silu(cpre).astype(o_ref.dtype)
            c_ref[pl.ds(r0, rc), :] = cpre.astype(c_ref.dtype)
            return carry

        _row_chunks(tt, chunk)

    out = pl.BlockSpec((tt, tc), lambda j, i: (i, j))
    return pl.pallas_call(
        body, grid=(c // tc, t // tt),
        in_specs=[pl.BlockSpec((tt, tc), lambda j, i: (i, j + joff)), _prev_spec(tt, tc, joff),
                  pl.BlockSpec((kw, tc), lambda j, i: (0, j)), pl.BlockSpec((1, tc), lambda j, i: (0, j))],
        out_specs=[out, out], out_shape=[S((t, c), XBC_DTYPE), S((t, c), ACT_DTYPE)],
        scratch_shapes=[pltpu.VMEM((HALO + tt, tc), F32)], name=name,
        compiler_params=_cp(("parallel", "arbitrary")))(proj, proj, w, b)


def _conv_ssm_bwd(proj, col0, w, cpre, wcol0, dact, dproj, out_col0, name):
    t = proj.shape[0]
    kw = w.shape[0]
    c = dact.shape[1]
    tt, tc = _conv_tiles(t, c)
    assert col0 % tc == 0 and wcol0 % tc == 0 and out_col0 % tc == 0
    joff, wj, oj = col0 // tc, wcol0 // tc, out_col0 // tc
    nt = t // tt

    def body(x_ref, w_ref, c_ref, cn_ref, d_ref, dn_ref, dp_in, dx_ref, dw_ref, db_ref, sd_ref):
        del dp_in
        i = pl.program_id(1)

        @pl.when(i == 0)
        def _():
            dw_ref[...] = jnp.zeros_like(dw_ref)
            db_ref[...] = jnp.zeros_like(db_ref)

        def stage(r0, rc, db):
            rows = pl.ds(r0, rc)
            d = d_ref[rows, :].astype(F32) * _silu_grad(c_ref[rows, :].astype(F32))
            sd_ref[rows, :] = d
            return db + _fold8(d)

        zero8 = jnp.zeros((8, tc), F32)
        db = _row_chunks(tt, stage, zero8)
        sd_ref[pl.ds(tt, HALO), :] = jnp.where(
            i < nt - 1, dn_ref[...].astype(F32) * _silu_grad(cn_ref[...].astype(F32)), 0.0)

        def chunk(r0, rc, dws):
            dx, dws = _conv_bwd_chunk(sd_ref, x_ref[pl.ds(r0, rc), :].astype(F32), w_ref, kw, r0, rc, dws)
            dx_ref[pl.ds(r0, rc), :] = dx.astype(dx_ref.dtype)
            return dws

        dws = _row_chunks(tt, chunk, [zero8] * kw)
        for k in range(kw):
            dw_ref[k:k + 1, :] += jnp.sum(dws[k], axis=0, keepdims=True)
        db_ref[0:1, :] += jnp.sum(db, axis=0, keepdims=True)

    acc = pl.BlockSpec((8, tc), lambda j, i: (0, j))
    return pl.pallas_call(
        body, grid=(c // tc, nt),
        in_specs=[pl.BlockSpec((tt, tc), lambda j, i: (i, j + joff)), pl.BlockSpec((kw, tc), lambda j, i: (0, j + wj)),
                  pl.BlockSpec((tt, tc), lambda j, i: (i, j + wj)), _next_spec(tt, tc, wj, t),
                  pl.BlockSpec((tt, tc), lambda j, i: (i, j)), _next_spec(tt, tc, 0, t),
                  pl.BlockSpec(memory_space=pl.ANY)],
        out_specs=[pl.BlockSpec((tt, tc), lambda j, i: (i, j + oj)), acc, acc],
        out_shape=[S(dproj.shape, dproj.dtype), S((8, c), F32), S((8, c), F32)],
        scratch_shapes=[pltpu.VMEM((tt + HALO, tc), F32)],
        input_output_aliases={6: 0}, name=name,
        compiler_params=_cp(("parallel", "arbitrary")))(proj, w, cpre, cpre, dact, dact, dproj)


def _conv_ffn_fwd(hid, w, b, name):
    t, f2 = hid.shape
    f = f2 // 2
    kw = w.shape[0]
    tt, tc = _conv_tiles(t, f)
    nj = f // tc

    def body(g_ref, gp_ref, u_ref, up_ref, wg_ref, wu_ref, bg_ref, bu_ref, o_ref, ot_ref, cv_ref, sg_ref, su_ref):
        i = pl.program_id(1)
        _fill_prev_main(sg_ref, gp_ref, g_ref, i, tt)
        _fill_prev_main(su_ref, up_ref, u_ref, i, tt)

        def chunk(r0, rc, carry):
            rows = pl.ds(r0, rc)
            gate = _taps_chunk(sg_ref, wg_ref, kw, r0, rc) + bg_ref[...]
            up = _taps_chunk(su_ref, wu_ref, kw, r0, rc) + bu_ref[...]
            o_ref[rows, :] = (jax.nn.silu(gate) * up).astype(o_ref.dtype)
            cv_ref[0, rows, :] = gate.astype(cv_ref.dtype)
            cv_ref[1, rows, :] = up.astype(cv_ref.dtype)
            return carry

        _row_chunks(tt, chunk)
        ot_ref[...] = o_ref[...].T

    return pl.pallas_call(
        body, grid=(nj, t // tt),
        in_specs=[pl.BlockSpec((tt, tc), lambda j, i: (i, j)), _prev_spec(tt, tc, 0),
                  pl.BlockSpec((tt, tc), lambda j, i: (i, j + nj)), _prev_spec(tt, tc, nj),
                  pl.BlockSpec((kw, tc), lambda j, i: (0, j)), pl.BlockSpec((kw, tc), lambda j, i: (0, j + nj)),
                  pl.BlockSpec((1, tc), lambda j, i: (0, j)), pl.BlockSpec((1, tc), lambda j, i: (0, j + nj))],
        out_specs=[pl.BlockSpec((tt, tc), lambda j, i: (i, j)), pl.BlockSpec((tc, tt), lambda j, i: (j, i)),
                   pl.BlockSpec((2, tt, tc), lambda j, i: (0, i, j))],
        out_shape=[S((t, f), ACT_DTYPE), S((f, t), ACT_DTYPE), S((2, t, f), ACT_DTYPE)],
        scratch_shapes=[pltpu.VMEM((HALO + tt, tc), F32), pltpu.VMEM((HALO + tt, tc), F32)], name=name,
        compiler_params=_cp(("parallel", "arbitrary")))(hid, hid, hid, hid, w, w, b, b)


def _conv_ffn_bwd(hid, w, cv, dact, name):
    t, f2 = hid.shape
    f = f2 // 2
    kw = w.shape[0]
    tt, tc = _conv_tiles(t, f)
    nj = f // tc
    nt = t // tt

    def body(g_ref, u_ref, wg_ref, wu_ref, cv_ref, cvn_ref, d_ref, dn_ref, dh_ref, dw_ref, db_ref, dg_ref, du_ref):
        i = pl.program_id(1)

        @pl.when(i == 0)
        def _():
            dw_ref[...] = jnp.zeros_like(dw_ref)
            db_ref[...] = jnp.zeros_like(db_ref)

        def cotangents(gate, up, dact_v):
            sg = jax.nn.sigmoid(gate)
            return dact_v * up * (sg * (1.0 + gate * (1.0 - sg))), dact_v * (gate * sg)

        def stage(r0, rc, dbs):
            rows = pl.ds(r0, rc)
            dg, du = cotangents(cv_ref[0, rows, :].astype(F32), cv_ref[1, rows, :].astype(F32), d_ref[rows, :].astype(F32))
            dg_ref[rows, :] = dg
            du_ref[rows, :] = du
            return [dbs[0] + _fold8(dg), dbs[1] + _fold8(du)]

        zero8 = jnp.zeros((8, tc), F32)
        dbs = _row_chunks(tt, stage, [zero8, zero8])
        dgn, dun = cotangents(cvn_ref[0].astype(F32), cvn_ref[1].astype(F32), dn_ref[...].astype(F32))
        dg_ref[pl.ds(tt, HALO), :] = jnp.where(i < nt - 1, dgn, 0.0)
        du_ref[pl.ds(tt, HALO), :] = jnp.where(i < nt - 1, dun, 0.0)
        for s, (sd_ref, x_ref, w_ref) in enumerate(((dg_ref, g_ref, wg_ref), (du_ref, u_ref, wu_ref))):
            def chunk(r0, rc, dws, s=s, sd_ref=sd_ref, x_ref=x_ref, w_ref=w_ref):
                dx, dws = _conv_bwd_chunk(sd_ref, x_ref[pl.ds(r0, rc), :].astype(F32), w_ref, kw, r0, rc, dws)
                dh_ref[s, pl.ds(r0, rc), :] = dx.astype(dh_ref.dtype)
                return dws

            dws = _row_chunks(tt, chunk, [zero8] * kw)
            for k in range(kw):
                dw_ref[s, k:k + 1, :] += jnp.sum(dws[k], axis=0, keepdims=True)
            db_ref[s, 0:1, :] += jnp.sum(dbs[s], axis=0, keepdims=True)

    acc = pl.BlockSpec((2, 8, tc), lambda j, i: (0, 0, j))
    dsc = pltpu.VMEM((tt + HALO, tc), F32)
    nxt = lambda j, i: (0, jnp.minimum((i + 1) * (tt // HALO), t // HALO - 1), j)
    return pl.pallas_call(
        body, grid=(nj, nt),
        in_specs=[pl.BlockSpec((tt, tc), lambda j, i: (i, j)), pl.BlockSpec((tt, tc), lambda j, i: (i, j + nj)),
                  pl.BlockSpec((kw, tc), lambda j, i: (0, j)), pl.BlockSpec((kw, tc), lambda j, i: (0, j + nj)),
                  pl.BlockSpec((2, tt, tc), lambda j, i: (0, i, j)), pl.BlockSpec((2, HALO, tc), nxt),
                  pl.BlockSpec((tt, tc), lambda j, i: (i, j)), _next_spec(tt, tc, 0, t)],
        out_specs=[pl.BlockSpec((2, tt, tc), lambda j, i: (0, i, j)), acc, acc],
        out_shape=[S((2, t, f), MXU_DTYPE), S((2, 8, f), F32), S((2, 8, f), F32)],
        scratch_shapes=[dsc, dsc], name=name,
        compiler_params=_cp(("parallel", "arbitrary")))(hid, hid, w, w, cv, cv, dact, dact)


SSD_SUB = 2


def _ssd_chunk(xs, bm, cm, dtraw, hin, bias, alog, dskip):
    n = bm.shape[0]
    row = lax.broadcasted_iota(jnp.int32, (n, n), 0)
    col = lax.broadcasted_iota(jnp.int32, (n, n), 1)
    causal = row >= col
    lane = lax.broadcasted_iota(jnp.int32, (1, 128), 1)
    sub = lax.broadcasted_iota(jnp.int32, (128, 1), 0)
    last = (lax.broadcasted_iota(jnp.int32, (n, 1), 0) == n - 1).astype(F32)
    dt = jax.nn.softplus(dtraw + bias)
    adt = dt * (-jnp.exp(alog))
    tri = causal.astype(F32)
    acs = jnp.dot(tri, adt, preferred_element_type=F32, precision=lax.Precision.HIGHEST)
    ch = lax.broadcasted_iota(jnp.int32, (128, GROUP_CH), 1)
    hd = lax.broadcasted_iota(jnp.int32, (128, GROUP_CH), 0) * HEAD_DIM
    expand = ((ch >= hd) & (ch < hd + HEAD_DIM)).astype(F32)
    acs_x = jnp.dot(acs, expand, preferred_element_type=F32, precision=lax.Precision.HIGHEST)
    alast_x = jnp.sum(acs_x * last, axis=0, keepdims=True)
    acs_t = acs.T
    scores = lax.dot_general(_mx(cm), _mx(bm), (((1,), (1,)), ((), ())), preferred_element_type=F32)
    yds, xts = [], []
    for r in range(HEADS_PER_GROUP):
        pick = (lane == r).astype(F32)
        acol = jnp.sum(acs * pick, axis=1, keepdims=True)
        arow = jnp.sum(acs_t * (sub == r).astype(F32), axis=0, keepdims=True)
        dtc = jnp.sum(dt * pick, axis=1, keepdims=True)
        lm = jnp.exp(jnp.where(causal, acol - arow, -1e30))
        xts.append(xs[r] * dtc)
        yds.append(jnp.dot(_mx(scores * lm), _mx(xts[r]), preferred_element_type=F32))
    xt = jnp.concatenate(xts, axis=1)
    yo = jnp.exp(acs_x) * jnp.dot(_mx(cm), _mx(hin), preferred_element_type=F32)
    st = lax.dot_general(_mx(bm), _mx(xt * jnp.exp(alast_x - acs_x)), (((0,), (0,)), ((), ())), preferred_element_type=F32)
    hout = jnp.exp(alast_x) * hin + st
    return jnp.concatenate(yds, axis=1) + yo + dskip * jnp.concatenate(xs, axis=1), hout


def _ssd_post(y, z, normg):
    return _rms(y * jax.nn.silu(z), normg)


def _ssd_block(datas, hin, consts):
    ys = []
    for data in datas:
        y, hin = _ssd_chunk(*data, hin, *consts)
        ys.append(y)
    return ys, hin


def _ssd_specs(d_ssm, ngrp, zcol0, rev, nb):
    cc = (lambda c: nb - 1 - c) if rev else (lambda c: c)
    rows = SSD_SUB * CHUNK
    xj, bj, cj, zj = 0, d_ssm // 128, d_ssm // 128 + ngrp, zcol0 // GROUP_CH
    const = lambda w: pl.BlockSpec((None, 8, w), lambda g, c: (g, 0, 0))
    return cc, [
        pl.BlockSpec((rows, GROUP_CH), lambda g, c: (cc(c), xj + g)),
        pl.BlockSpec((rows, 128), lambda g, c: (cc(c), bj + g)),
        pl.BlockSpec((rows, 128), lambda g, c: (cc(c), cj + g)),
        pl.BlockSpec((rows, 128), lambda g, c: (cc(c), g)),
        pl.BlockSpec((rows, GROUP_CH), lambda g, c: (cc(c), zj + g)),
        const(128), const(128), const(GROUP_CH), const(GROUP_CH)]


def _sub_rows(s):
    return slice(CHUNK * s, CHUNK * (s + 1))


def _ssd_load(x_ref, b_ref, c_ref, dt_ref, z_ref, bias_ref, alog_ref, dsk_ref, ng_ref):
    datas, zs = [], []
    for s in range(SSD_SUB):
        rows = _sub_rows(s)
        xs = [x_ref[rows, HEAD_DIM * r:HEAD_DIM * (r + 1)].astype(F32) for r in range(HEADS_PER_GROUP)]
        datas.append((xs, b_ref[rows, :].astype(F32), c_ref[rows, :].astype(F32), dt_ref[rows, :]))
        zs.append(z_ref[rows, :].astype(F32))
    return datas, zs, (bias_ref[0:1, :], alog_ref[0:1, :], dsk_ref[0:1, :]), ng_ref[0:1, :]


def _ssd_fwd(xbc, dtp, proj, zcol0, bias_p, alog_p, dskip_x, normg_x, cat, cat_t, name):
    t = xbc.shape[0]
    ngrp = bias_p.shape[0]
    d_ssm = ngrp * GROUP_CH
    rows = SSD_SUB * CHUNK
    nb = t // rows
    d_a = cat.shape[1] - d_ssm
    assert d_a % GROUP_CH == 0 and zcol0 % GROUP_CH == 0 and t % rows == 0
    _, specs = _ssd_specs(d_ssm, ngrp, zcol0, False, nb)

    def body(x_ref, b_ref, c_ref, dt_ref, z_ref, bias_ref, alog_ref, dsk_ref, ng_ref, cat_in, catt_in,
             yn_ref, ynt_ref, y_ref, hs_ref, h_ref):
        del cat_in, catt_in

        @pl.when(pl.program_id(1) == 0)
        def _():
            h_ref[...] = jnp.zeros_like(h_ref)

        datas, zs, consts, normg = _ssd_load(x_ref, b_ref, c_ref, dt_ref, z_ref, bias_ref, alog_ref, dsk_ref, ng_ref)
        hs_ref[...] = h_ref[...]
        ys, hout = _ssd_block(datas, h_ref[...], consts)
        for s in range(SSD_SUB):
            y_ref[_sub_rows(s), :] = ys[s].astype(y_ref.dtype)
            yn = _ssd_post(ys[s], zs[s], normg).astype(yn_ref.dtype)
            yn_ref[_sub_rows(s), :] = yn
            ynt_ref[:, _sub_rows(s)] = yn.T
        h_ref[...] = hout

    hbm = pl.BlockSpec(memory_space=pl.ANY)
    return pl.pallas_call(
        body, grid=(ngrp, nb), in_specs=specs + [hbm, hbm],
        out_specs=[pl.BlockSpec((rows, GROUP_CH), lambda g, c: (c, d_a // GROUP_CH + g)),
                   pl.BlockSpec((GROUP_CH, rows), lambda g, c: (d_a // GROUP_CH + g, c)),
                   pl.BlockSpec((rows, GROUP_CH), lambda g, c: (c, g)),
                   pl.BlockSpec((None, None, D_STATE, GROUP_CH), lambda g, c: (c, g, 0, 0))],
        out_shape=[S(cat.shape, cat.dtype), S(cat_t.shape, cat_t.dtype), S((t, d_ssm), ACT_DTYPE),
                   S((nb, ngrp, D_STATE, GROUP_CH), F32)],
        scratch_shapes=[pltpu.VMEM((D_STATE, GROUP_CH), F32)],
        input_output_aliases={9: 0, 10: 1}, name=name,
        compiler_params=_cp(("parallel", "arbitrary")))(xbc, xbc, xbc, dtp, proj, bias_p, alog_p, dskip_x, normg_x, cat, cat_t)


def _ssd_bwd(xbc, dtp, proj, zcol0, bias_p, alog_p, dskip_x, normg_x, hs, ypre, dcat, dproj, name):
    t = xbc.shape[0]
    ngrp = bias_p.shape[0]
    d_ssm = ngrp * GROUP_CH
    rows = SSD_SUB * CHUNK
    nb = t // rows
    d_a = dcat.shape[1] - d_ssm
    cc, specs = _ssd_specs(d_ssm, ngrp, zcol0, True, nb)

    def body(x_ref, b_ref, c_ref, dt_ref, z_ref, bias_ref, alog_ref, dsk_ref, ng_ref, hs_ref, yp_ref, dy_ref, dp_in,
             dz_ref, dx_ref, db_ref, dc_ref, ddt_ref, dbias_ref, dalog_ref, ddsk_ref, dng_ref, dh_ref):
        del dp_in

        @pl.when(pl.program_id(1) == 0)
        def _():
            dh_ref[...] = jnp.zeros_like(dh_ref)
            for r in (dbias_ref, dalog_ref, ddsk_ref, dng_ref):
                r[...] = jnp.zeros_like(r)

        datas, zs, consts, normg = _ssd_load(x_ref, b_ref, c_ref, dt_ref, z_ref, bias_ref, alog_ref, dsk_ref, ng_ref)
        dys, dng = [], jnp.zeros_like(normg)
        for s in range(SSD_SUB):
            rws = _sub_rows(s)
            _, vjp_post = jax.vjp(_ssd_post, yp_ref[rws, :].astype(F32), zs[s], normg)
            dy, dz, dg = vjp_post(dy_ref[rws, :].astype(F32))
            dys.append(dy)
            dng = dng + dg
            dz_ref[rws, :] = dz.astype(dz_ref.dtype)
        _, vjp = jax.vjp(_ssd_block, datas, hs_ref[...], consts)
        ddatas, dhin, (dbias, dalog, ddsk) = vjp((dys, dh_ref[...]))
        for s, (dxs, dbm, dcm, ddt) in enumerate(ddatas):
            rws = _sub_rows(s)
            for r in range(HEADS_PER_GROUP):
                dx_ref[rws, HEAD_DIM * r:HEAD_DIM * (r + 1)] = dxs[r].astype(dx_ref.dtype)
            db_ref[rws, :] = dbm.astype(db_ref.dtype)
            dc_ref[rws, :] = dcm.astype(dc_ref.dtype)
            ddt_ref[rws, :] = ddt
        dh_ref[...] = dhin
        dbias_ref[0:1, :] += dbias
        dalog_ref[0:1, :] += dalog
        ddsk_ref[0:1, :] += ddsk
        dng_ref[0:1, :] += dng

    acc = lambda w: pl.BlockSpec((None, 8, w), lambda g, c: (g, 0, 0))
    blk = lambda w: pl.BlockSpec((rows, w), lambda g, c: (cc(c), g))
    return pl.pallas_call(
        body, grid=(ngrp, nb),
        in_specs=specs + [pl.BlockSpec((None, None, D_STATE, GROUP_CH), lambda g, c: (cc(c), g, 0, 0)),
                          blk(GROUP_CH),
                          pl.BlockSpec((rows, GROUP_CH), lambda g, c: (cc(c), d_a // GROUP_CH + g)),
                          pl.BlockSpec(memory_space=pl.ANY)],
        out_specs=[pl.BlockSpec((rows, GROUP_CH), lambda g, c: (cc(c), zcol0 // GROUP_CH + g)),
                   blk(GROUP_CH), blk(128), blk(128), blk(128), acc(128), acc(128), acc(GROUP_CH), acc(GROUP_CH)],
        out_shape=[S(dproj.shape, dproj.dtype), S((t, d_ssm), XBC_DTYPE), S((t, ngrp * 128), XBC_DTYPE),
                   S((t, ngrp * 128), XBC_DTYPE), S((t, ngrp * 128), F32), S((ngrp, 8, 128), F32),
                   S((ngrp, 8, 128), F32), S((ngrp, 8, GROUP_CH), F32), S((ngrp, 8, GROUP_CH), F32)],
        scratch_shapes=[pltpu.VMEM((D_STATE, GROUP_CH), F32)],
        input_output_aliases={12: 0}, name=name,
        compiler_params=_cp(("parallel", "arbitrary")))(xbc, xbc, xbc, dtp, proj, bias_p, alog_p, dskip_x, normg_x, hs, ypre,
                                                        dcat, dproj)


def _adamw(parts, w, m, v, name):
    r, c = w.shape
    tr = _pick(r, (256, 128, 64, 32, 16, 8)) if c * 4 * 256 <= 4 * 1024 * 1024 else _pick(r, (64, 32, 16, 8))

    def body(p_ref, w_ref, m_ref, v_ref, g_ref, d_ref, nm_ref, nv_ref):
        g = p_ref[0].astype(F32)
        for k in range(1, N_DEV):
            g = g + p_ref[k].astype(F32)
        mm = ADAM_B1 * m_ref[...] + (1.0 - ADAM_B1) * g
        vv = ADAM_B2 * v_ref[...] + (1.0 - ADAM_B2) * jnp.square(g)
        m_hat = mm / (1.0 - ADAM_B1 ** ADAM_STEP)
        v_hat = vv / (1.0 - ADAM_B2 ** ADAM_STEP)
        g_ref[...] = g
        d_ref[...] = -ADAM_LR * (m_hat / (jnp.sqrt(v_hat) + ADAM_EPS) + ADAM_WD * w_ref[...])
        nm_ref[...] = mm
        nv_ref[...] = vv

    blk = pl.BlockSpec((tr, c), lambda i: (i, 0))
    return pl.pallas_call(
        body, grid=(r // tr,),
        in_specs=[pl.BlockSpec((N_DEV, tr, c), lambda i: (0, i, 0)), blk, blk, blk],
        out_specs=[blk, blk, blk, blk], out_shape=[S((r, c), F32)] * 4, name=name,
        compiler_params=_cp(("parallel",)))(parts, w, m, v)


def _mesh_pos():
    return lax.axis_index("x"), lax.axis_index("y"), lax.axis_index("c")


def _peer(d, x, y, c):
    return (1 - x if (d >> 2) & 1 else x, 1 - y if (d >> 1) & 1 else y, 1 - c if d & 1 else c)


def _gather_two_level(blocks, name):
    n = len(blocks)

    def body(*refs):
        srcs, outs = refs[:n], refs[n:2 * n]
        send_sems, recv_sems, loc_sems = refs[2 * n:]
        x, y, c = _mesh_pos()
        lin = lambda px, py, pc: 4 * px + 2 * py + pc
        me, sibling = (x, y, c), (x, y, 1 - c)
        chips = [(1 - x, y), (x, 1 - y), (1 - x, 1 - y)]

        def copy(a, k, block, to, src=None):
            slab = outs[a].at[lin(*block)]
            return pltpu.make_async_remote_copy(
                src_ref=slab if src is None else src, dst_ref=slab, send_sem=send_sems.at[a, k],
                recv_sem=recv_sems.at[a, k], device_id=to, device_id_type=pl.DeviceIdType.MESH)

        mine = [pltpu.make_async_copy(srcs[a], outs[a].at[lin(*me)], loc_sems.at[a]) for a in range(n)]
        first = [copy(a, 0, me, sibling, src=srcs[a]) for a in range(n)]
        first += [copy(a, 1 + j, me, (*chip, c), src=srcs[a]) for j, chip in enumerate(chips) for a in range(n)]
        for cp in mine + first:
            cp.start()
        passed = []
        for j, chip in enumerate(chips):
            for a in range(n):
                copy(a, 1 + j, (*chip, c), me).wait_recv()
                passed.append(copy(a, 4 + j, (*chip, c), sibling))
                passed[-1].start()
        for a in range(n):
            copy(a, 0, sibling, me).wait_recv()
            for j, chip in enumerate(chips):
                copy(a, 4 + j, (*chip, 1 - c), me).wait_recv()
        for cp in first + passed:
            cp.wait_send()
        for cp in mine:
            cp.wait()

    hbm = pl.BlockSpec(memory_space=pl.ANY)
    return pl.pallas_call(
        body, in_specs=[hbm] * n, out_specs=[hbm] * n, out_shape=[S((N_DEV,) + b.shape, b.dtype) for b in blocks],
        scratch_shapes=[pltpu.SemaphoreType.DMA((n, N_DEV - 1)), pltpu.SemaphoreType.DMA((n, N_DEV - 1)),
                        pltpu.SemaphoreType.DMA((n,))],
        name=name, compiler_params=pltpu.CompilerParams(has_side_effects=True))(*blocks)


def _xfer_start(items, name, after=None):
    n = len(items)
    kinds = [k for k, _ in items]
    srcs = [pltpu.with_memory_space_constraint(a, pltpu.HBM) for _, a in items]
    land_shapes = [((N_DEV,) + a.shape if k == 'gather' else a.shape, a.dtype) for k, a in items]
    lands = [pltpu.with_memory_space_constraint(lax.empty(s, dt), pltpu.HBM) for s, dt in land_shapes]
    extra = [] if after is None else [after]

    def body(*refs):
        src_refs, land_refs = refs[:n], refs[n:2 * n]
        outs = refs[2 * n + len(extra):]
        sems = outs[:2 * n]
        token = outs[4 * n]
        x, y, c = _mesh_pos()
        me = 4 * x + 2 * y + c
        for a in range(n):
            for d in range(1, N_DEV):
                px, py, pc = _peer(d, x, y, c)
                src = src_refs[a] if kinds[a] == 'gather' else src_refs[a].at[4 * px + 2 * py + pc]
                pltpu.make_async_remote_copy(
                    src_ref=src, dst_ref=land_refs[a].at[me], send_sem=sems[2 * a].at[d - 1],
                    recv_sem=sems[2 * a + 1].at[d - 1], device_id=(px, py, pc),
                    device_id_type=pl.DeviceIdType.MESH).start()
        token[...] = jnp.zeros_like(token)

    hbm = pl.BlockSpec(memory_space=pltpu.HBM)
    sem = pl.BlockSpec(memory_space=pltpu.SEMAPHORE)
    out_shape = ([pltpu.SemaphoreType.DMA((N_DEV - 1,))] * (2 * n)
                 + [pltpu.HBM(a.shape, a.dtype) for a in srcs] + [pltpu.HBM(s, dt) for s, dt in land_shapes]
                 + [S((8, 128), F32)])
    res = pl.pallas_call(
        body, name=name, out_shape=out_shape,
        in_specs=[hbm] * (2 * n) + [pl.BlockSpec(memory_space=pl.ANY)] * len(extra),
        out_specs=[sem] * (2 * n) + [hbm] * (2 * n) + [pl.BlockSpec(memory_space=pltpu.VMEM)],
        input_output_aliases={**{a: 2 * n + a for a in range(n)}, **{n + a: 3 * n + a for a in range(n)}},
        compiler_params=pltpu.CompilerParams(has_side_effects=pltpu.SideEffectType.DATAFLOW_SIDE_EFFECTING),
    )(*srcs, *lands, *extra)
    return (kinds, res[:2 * n], res[2 * n:3 * n], res[3 * n:4 * n]), res[4 * n]


def _xfer_wait(handle, after, name):
    kinds, sems, src_thru, land_thru = handle
    n = len(kinds)

    def body(*refs):
        land_refs = refs[n:2 * n]
        sem_refs = refs[2 * n:4 * n]
        x, y, c = _mesh_pos()
        me = 4 * x + 2 * y + c
        for a in range(n):
            for d in range(1, N_DEV):
                slab = land_refs[a].at[me]
                cp = pltpu.make_async_remote_copy(
                    src_ref=slab, dst_ref=slab, send_sem=sem_refs[2 * a].at[d - 1], recv_sem=sem_refs[2 * a + 1].at[d - 1],
                    device_id=_peer(d, x, y, c), device_id_type=pl.DeviceIdType.MESH)
                cp.wait_send()
                cp.wait_recv()

    hbm = pl.BlockSpec(memory_space=pltpu.HBM)
    sem = pl.BlockSpec(memory_space=pltpu.SEMAPHORE)
    res = pl.pallas_call(
        body, name=name,
        out_shape=[pltpu.HBM(a.shape, a.dtype) for a in src_thru] + [pltpu.HBM(a.shape, a.dtype) for a in land_thru],
        in_specs=[hbm] * (2 * n) + [sem] * (2 * n) + [pl.BlockSpec(memory_space=pl.ANY)],
        out_specs=[hbm] * (2 * n), input_output_aliases={a: a for a in range(2 * n)},
        compiler_params=pltpu.CompilerParams(has_side_effects=pltpu.SideEffectType.DATAFLOW_SIDE_EFFECTING),
    )(*src_thru, *land_thru, *sems, after)
    x, y, c = _mesh_pos()
    me = 4 * x + 2 * y + c
    out = []
    for a in range(n):
        src = res[a]
        own = src[None] if kinds[a] == 'gather' else lax.dynamic_index_in_dim(src, me, 0, keepdims=True)
        out.append(lax.dynamic_update_index_in_dim(res[n + a], own, me, 0))
    return out


def _stack_to_full(kind, st):
    if kind == 'row':
        return st.reshape(st.shape[0] * st.shape[1], st.shape[2])
    return jnp.concatenate([st[k] for k in range(st.shape[0])], axis=1)


def _full_to_stack(kind, full):
    r, c = full.shape
    if kind == 'row':
        return full.reshape(N_DEV, r // N_DEV, c)
    w = c // N_DEV
    return jnp.stack([full[:, k * w:(k + 1) * w] for k in range(N_DEV)], axis=0)


SMALL_ROWS = 256


def _pack_small(named):
    layout = [(a.shape, a.size, -(-a.size // 1024) * 8) for a in named]
    total = -(-sum(nr for _, _, nr in layout) // SMALL_ROWS) * SMALL_ROWS * 128
    packed, off = None, 0
    for a, (_, n, nr) in zip(named, layout):
        part = jnp.pad(a.reshape(-1).astype(F32), (off, total - off - n))
        packed = part if packed is None else packed + part
        off += nr * 128
    return packed.reshape(total // 128, 128), layout


def _unpack_small(packed, layout):
    out, r0 = [], 0
    for shape, n, nr in layout:
        out.append(packed[r0:r0 + nr].reshape(-1)[:n].reshape(shape))
        r0 += nr
    return out


def _row0(acc):
    return acc[0]


def _local_step(x, p, tgt, sm, comm):
    t, d = x.shape
    h_n = sm['dt_bias'].shape[-1]
    ngrp = h_n // HEADS_PER_GROUP
    d_ssm = h_n * HEAD_DIM
    d_a = sm['ln_a_g'].shape[-1]
    d_mix = d_a + d_ssm
    d_xbc = sm['conv_ssm_b'].shape[-1]
    d_main = 2 * d_a + d_ssm + d_xbc
    assert d_xbc == d_ssm + 2 * ngrp * D_STATE and h_n <= 128
    zcol0, xcol0 = 2 * d_a, 2 * d_a + d_ssm
    vec = lambda v: v.reshape(1, -1)

    bst = jnp.pad(sm['b_s'].T, ((0, 0), (0, 128 - sm['b_s'].shape[0])))
    grp = lambda v, w: jnp.broadcast_to(jnp.pad(v.reshape(ngrp, 1, -1), ((0, 0), (0, 0), (0, w - v.size // ngrp))), (ngrp, 8, w))
    bias_p, alog_p = grp(sm['dt_bias'], 128), grp(sm['a_log'], 128)
    dskip_x = grp(jnp.repeat(sm['d_skip'], HEAD_DIM), GROUP_CH)
    normg_x = grp(sm['ssm_norm_g'], GROUP_CH)
    pad_dt = lambda v: jnp.pad(v[:, :h_n].reshape(t, ngrp, HEADS_PER_GROUP),
                               ((0, 0), (0, 0), (0, 128 - HEADS_PER_GROUP))).reshape(t, ngrp * 128)

    g_mix = vec(sm['norm_mix_g']) + comm.tok0
    a_n, a_t = _rms_fwd(x, g_mix, "rms_mix")
    wf = comm.weights('a', a_n)
    slabs = [wf['w_in'][k] for k in range(N_DEV)]
    w_main = jnp.concatenate(slabs[:-1] + [slabs[-1][:, :slabs[-1].shape[1] - h_n]], axis=1)
    w_dt = jnp.pad(slabs[-1][:, slabs[-1].shape[1] - h_n:], ((0, 0), (0, 128 - h_n)))
    proj = _mm_nn(a_n, w_main, out_dtype=ACT_DTYPE, name="mm_in")
    dtp = pad_dt(_mm_nn(a_n, w_dt, out_dtype=F32, name="mm_dt"))
    cat, cat_t = _gmlp_fwd(proj, vec(sm['ln_a_g']), vec(sm['ln_a_b']), sm['w_s'], bst, vec(sm['norm_a_g']), d_mix, "gmlp_fwd")
    xbc, cpre = _conv_ssm_fwd(proj, xcol0, wf['conv_ssm_w'], vec(sm['conv_ssm_b']), "conv_ssm_fwd")
    cat, cat_t, ypre, hs = _ssd_fwd(xbc, dtp, proj, zcol0, bias_p, alog_p, dskip_x, normg_x, cat, cat_t, "ssd_fwd")
    wf.update(comm.weights('b', hs))
    h1 = _mm_nn(cat, wf['w_out'], out_dtype=F32, name="mm_out", res=x)
    f_n, f_t = _rms_fwd(h1, vec(sm['norm_ffn_g']), "rms_ffn")
    hid = _mm_nn(f_n, wf['w_up'], out_dtype=ACT_DTYPE, name="mm_up")
    act, act_t, cv = _conv_ffn_fwd(hid, wf['conv_ffn_w'], vec(sm['conv_ffn_b']), "conv_ffn_fwd")
    h2 = _mm_nn(act, wf['w_down'], out_dtype=F32, name="mm_down", res=h1)
    r_n, r_t = _rms_fwd(h2, vec(sm['norm_ple_g']), "rms_ple")
    q = _mm_nn(r_n, wf['w_ple_gate'], out_dtype=ACT_DTYPE, name="mm_pg")
    p_m = p.astype(MXU_DTYPE)
    pe = _mm_nn(p_m, wf['w_ple'], out_dtype=ACT_DTYPE, name="mm_ple")

    loss, dh3, dq, dpe, dgf = _head(h2, q, pe, tgt, vec(sm['norm_final_g']), "head")
    wgrad = lambda act_t, g, name, **kw: _mm_nn(act_t, g, out_dtype=WIRE_DTYPE, name=name, wide=True, **kw)
    gs = {}
    gs['norm_final_g'] = _row0(dgf)
    g_ple = wgrad(p_m.T, dpe, "wg_ple")
    g_pg = wgrad(r_t, dq, "wg_pg")
    dr = _mm_nt(dq, wf['w_ple_gate'], out_dtype=ACT_DTYPE, name="dg_pg")
    dh2, dh2m, dg = _rms_bwd(h2, vec(sm['norm_ple_g']), dr, dh3, "rms_ple_bwd")
    gs['norm_ple_g'] = _row0(dg)
    g_down = wgrad(act_t, dh2m, "wg_down")
    tok = comm.send('1', {'w_ple': g_ple, 'w_ple_gate': g_pg, 'w_down': g_down})
    dact = _mm_nt(dh2m, wf['w_down'], out_dtype=ACT_DTYPE, name="dg_down")
    dhid, dcw, dcb = _conv_ffn_bwd(hid, wf['conv_ffn_w'] + tok, cv, dact, "conv_ffn_bwd")
    kf = wf['conv_ffn_w'].shape[0]
    g_cf = jnp.concatenate([dcw[0, :kf], dcw[1, :kf]], axis=1)
    gs['conv_ffn_b'] = jnp.concatenate([dcb[0, 0], dcb[1, 0]], axis=0)
    g_up = wgrad(f_t, dhid, "wg_up", b_split=2, out_slabs=N_DEV)
    df = _mm_nt(dhid, wf['w_up'], out_dtype=ACT_DTYPE, name="dg_up", a_split=2)
    dh1, dh1m, dg = _rms_bwd(h1, vec(sm['norm_ffn_g']), df, dh2, "rms_ffn_bwd")
    gs['norm_ffn_g'] = _row0(dg)
    g_out = wgrad(cat_t, dh1m, "wg_out")
    tok = comm.send('2', {'conv_ffn_w': g_cf, 'w_up': g_up, 'w_out': g_out}, stacked=('w_up',))
    dcat = _mm_nt(dh1m, wf['w_out'], out_dtype=ACT_DTYPE, name="dg_out")
    dproj, dlng, dlnb, dws, dbst, dng = _gmlp_bwd(proj, vec(sm['ln_a_g']) + tok, vec(sm['ln_a_b']), sm['w_s'], bst,
                                                  vec(sm['norm_a_g']), dcat, d_main, "gmlp_bwd")
    gs['ln_a_g'], gs['ln_a_b'], gs['w_s'], gs['norm_a_g'] = _row0(dlng), _row0(dlnb), dws, _row0(dng)
    gs['b_s'] = dbst[:, :sm['b_s'].shape[0]].T
    dproj, dxs, dbm, dcm, ddtp, dbias, dalog, ddsk, dsng = _ssd_bwd(
        xbc, dtp, proj, zcol0, bias_p, alog_p, dskip_x, normg_x, hs, ypre, dcat, dproj, "ssd_bwd")
    gs['dt_bias'] = dbias[:, 0, :HEADS_PER_GROUP].reshape(h_n)
    gs['a_log'] = dalog[:, 0, :HEADS_PER_GROUP].reshape(h_n)
    gs['d_skip'] = ddsk[:, 0, :].reshape(h_n, HEAD_DIM).sum(axis=-1)
    gs['ssm_norm_g'] = dsng[:, 0, :].reshape(d_ssm)
    dws_c, dbs_c = [], []
    off = 0
    for nm, dpart in (("x", dxs), ("b", dbm), ("c", dcm)):
        dproj, dw_c, db_c = _conv_ssm_bwd(proj, xcol0 + off, wf['conv_ssm_w'], cpre, off, dpart, dproj,
                                          xcol0 + off, "conv_ssm_bwd_" + nm)
        dws_c.append(dw_c[:wf['conv_ssm_w'].shape[0]])
        dbs_c.append(db_c[0])
        off += dpart.shape[1]
    g_cs = jnp.concatenate(dws_c, axis=1)
    gs['conv_ssm_b'] = jnp.concatenate(dbs_c, axis=0)
    ddt = jnp.pad(ddtp.reshape(t, ngrp, 128)[:, :, :HEADS_PER_GROUP].reshape(t, h_n), ((0, 0), (0, 128 - h_n))).astype(MXU_DTYPE)
    g_in = jnp.concatenate([wgrad(a_t, dproj, "wg_in"), wgrad(a_t, ddt, "wg_dt")[:, :h_n]], axis=1)
    tok = comm.send('3', {'conv_ssm_w': g_cs, 'w_in': g_in}, [(n, gs[n]) for n in REPLICATED if n != 'norm_mix_g'])
    da = _mm_nt(ddt + tok.astype(ddt.dtype), w_dt, out_dtype=F32, name="dg_dt")
    da = _mm_nt(dproj, w_main, out_dtype=ACT_DTYPE, name="dg_in", res=da)
    dx, dg = _rms_bwd(x, g_mix + tok, da, dh1, "rms_mix_bwd", mx_copy=False)
    comm.send('4', {}, [('norm_mix_g', _row0(dg)), ('loss', loss[0, 0:1])])
    return dx


def kernel(x, p, norm_mix_g, w_in, ln_a_g, ln_a_b, w_s, b_s, norm_a_g, conv_ssm_w, conv_ssm_b, dt_bias, a_log, d_skip, ssm_norm_g, w_out, norm_ffn_g, w_up, conv_ffn_w, conv_ffn_b, w_down, norm_ple_g, w_ple_gate, w_ple, norm_final_g, loss_target, m_norm_mix_g, m_w_in, m_ln_a_g, m_ln_a_b, m_w_s, m_b_s, m_norm_a_g, m_conv_ssm_w, m_conv_ssm_b, m_dt_bias, m_a_log, m_d_skip, m_ssm_norm_g, m_w_out, m_norm_ffn_g, m_w_up, m_conv_ffn_w, m_conv_ffn_b, m_w_down, m_norm_ple_g, m_w_ple_gate, m_w_ple, m_norm_final_g, v_norm_mix_g, v_w_in, v_ln_a_g, v_ln_a_b, v_w_s, v_b_s, v_norm_a_g, v_conv_ssm_w, v_conv_ssm_b, v_dt_bias, v_a_log, v_d_skip, v_ssm_norm_g, v_w_out, v_norm_ffn_g, v_w_up, v_conv_ffn_w, v_conv_ffn_b, v_w_down, v_norm_ple_g, v_w_ple_gate, v_w_ple, v_norm_final_g):
    given = dict(locals())
    wts = {n: given[n] for n in WEIGHTS}
    ms = {n: given["m_" + n] for n in WEIGHTS}
    vs = {n: given["v_" + n] for n in WEIGHTS}
    sm = {n: (wts[n][0] if wts[n].ndim > 1 else wts[n]) for n in REPLICATED}
    comm = _Comm({n: wts[n][0] for n in SHARDED})
    dx = _local_step(x[0], p[0, 0], loss_target[0], sm, comm)

    out, loss_out, after = {}, None, dx
    for tag, names, small_names, layout, handle in comm.sent:
        recv = _xfer_wait(handle, after, "grads_%s_wait" % tag)
        for n, parts in zip(names, recv):
            out[n] = _adamw(parts, wts[n][0], ms[n][0], vs[n][0], "adamw_" + n)
            after = out[n][1]
        if small_names:
            pick = lambda src, fill: _pack_small([src[n] if n in src else jnp.full((1,), fill, F32) for n in small_names])[0]
            res = _adamw(recv[-1], pick(wts, 0.0), pick(ms, 0.0), pick(vs, 1.0), "adamw_small_" + tag)
            res = [_unpack_small(o, layout) for o in res]
            after = res[1][0]
            for i, n in enumerate(small_names):
                if n == 'loss':
                    loss_out = res[0][i].reshape(())
                else:
                    out[n] = [res[k][i] for k in range(4)]
    return (loss_out, dx[None], *[out[n][k].reshape(wts[n].shape) for k in range(4) for n in WEIGHTS])


class _Comm:
    GATHER_GROUPS = {'a': ('w_in', 'conv_ssm_w'), 'b': ('w_out', 'w_up', 'conv_ffn_w', 'w_down', 'w_ple_gate', 'w_ple')}

    def __init__(self, blocks):
        wired = lambda grp: [blocks[n].astype(_wire(n)) for n in self.GATHER_GROUPS[grp]]
        self.stacks_a = _gather_two_level(wired('a'), "gather_a")
        self.handle_b, tok = _xfer_start([('gather', b) for b in wired('b')], "gather_b_start", after=self.stacks_a[0])
        self.tok0 = tok[0, 0]
        self.sent = []

    def weights(self, grp, after):
        stacks = self.stacks_a if grp == 'a' else _xfer_wait(self.handle_b, after, "gather_b_wait")
        return {n: st if n == 'w_in' else _stack_to_full(SHARDED[n], st) for n, st in zip(self.GATHER_GROUPS[grp], stacks)}

    def send(self, tag, gw, small=None, stacked=()):
        items = [('scatter', g if n in stacked else _full_to_stack(SHARDED[n], g.astype(_wire(n)))) for n, g in gw.items()]
        layout, small_names = None, []
        if small:
            packed, layout = _pack_small([a for _, a in small])
            small_names = [n for n, _ in small]
            items.append(('gather', packed))
        handle, tok = _xfer_start(items, "grads_%s_start" % tag)
        self.sent.append((tag, list(gw), small_names, layout, handle))
        return tok[0, 0]


def _wire(name):
    return F32 if name in F32_ON_WIRE else WIRE_DTYPE
```

```python
import functools

import jax
import jax.numpy as jnp
from jax import lax
from jax.experimental import pallas as pl
from jax.experimental.pallas import tpu as pltpu

F32 = jnp.float32
MXU_DTYPE = jnp.bfloat16
ACT_DTYPE = jnp.bfloat16
XBC_DTYPE = jnp.bfloat16
WIRE_DTYPE = jnp.bfloat16
EPS = 1e-6
CHUNK = 128
D_STATE = 128
HEAD_DIM = 64
HEADS_PER_GROUP = 4
GROUP_CH = HEAD_DIM * HEADS_PER_GROUP
HALO = 16
N_DEV = 8
VMEM_LIMIT = 56 * 1024 * 1024

ADAM_LR = 0.001
ADAM_B1 = 0.9
ADAM_B2 = 0.999
ADAM_EPS = 1e-08
ADAM_WD = 0.01
ADAM_STEP = 10

WEIGHTS = ['norm_mix_g', 'w_in', 'ln_a_g', 'ln_a_b', 'w_s', 'b_s', 'norm_a_g', 'conv_ssm_w', 'conv_ssm_b', 'dt_bias',
           'a_log', 'd_skip', 'ssm_norm_g', 'w_out', 'norm_ffn_g', 'w_up', 'conv_ffn_w', 'conv_ffn_b', 'w_down',
           'norm_ple_g', 'w_ple_gate', 'w_ple', 'norm_final_g']
SHARDED = {'w_in': 'col', 'conv_ssm_w': 'col', 'w_out': 'row', 'w_up': 'col', 'conv_ffn_w': 'col', 'w_down': 'row',
           'w_ple_gate': 'row', 'w_ple': 'col'}
F32_ON_WIRE = ('conv_ssm_w', 'conv_ffn_w')
REPLICATED = [n for n in WEIGHTS if n not in SHARDED]

S = jax.ShapeDtypeStruct


def _pick(dim, cands):
    for c in cands:
        if c <= dim and dim % c == 0:
            return c
    return dim


def _cp(sem, vmem=VMEM_LIMIT):
    return pltpu.CompilerParams(dimension_semantics=sem, vmem_limit_bytes=vmem)


def _mx(v):
    return v.astype(MXU_DTYPE)


def _rms(v, g):
    return v * lax.rsqrt(jnp.mean(v * v, axis=-1, keepdims=True) + EPS) * g


MM_VMEM_BUDGET = 42 * 1024 * 1024


def _mm_tiles(m, n, k, out_bytes, has_res, tn_cands=(512, 256, 128), k_mult=1, tm_cands=(1024, 512), tn_alts=2):
    tns = [c for c in tn_cands if c <= n and n % c == 0][:tn_alts] or [n]
    ks = k // k_mult
    best = None
    for tn in tns:
        for tm in [c for c in tm_cands if m % c == 0] or [_pick(m, (256, 128))]:
            for nk in range(1, ks // 128 + 1):
                if ks % nk or (ks // nk) % 128:
                    continue
                tk = ks // nk
                need = 2 * 2 * (tm * tk + tk * tn) + tm * tn * (4 + 2 * out_bytes + (8 if has_res else 0))
                if need <= MM_VMEM_BUDGET:
                    if best is None or (nk, -tm, -tn) < best[0]:
                        best = ((nk, -tm, -tn), (tm, tn, tk))
                    break
    return best[1] if best else (_pick(m, (512, 256, 128)), tns[0], _pick(ks, (128,)))


def _mm_body(dot, nk, has_res):
    def body(*refs):
        if has_res:
            a_ref, b_ref, r_ref, o_ref, acc_ref = refs
        else:
            a_ref, b_ref, o_ref, acc_ref = refs
            r_ref = None
        kk = pl.program_id(2)
        d = dot(a_ref[...], b_ref[...])

        def fin(acc):
            if r_ref is not None:
                acc = acc + r_ref[...]
            o_ref[...] = acc.astype(o_ref.dtype)

        if nk == 1:
            fin(d)
        else:
            @pl.when(kk == 0)
            def _():
                acc_ref[...] = d

            if nk > 2:
                @pl.when((kk > 0) & (kk < nk - 1))
                def _():
                    acc_ref[...] += d

            @pl.when(kk == nk - 1)
            def _():
                fin(acc_ref[...] + d)

    return body


def _mm_call(body, grid, a_spec, b_spec, tm, tn, m, n, out_dtype, name, args, res, out_slabs=1):
    in_specs = [a_spec, b_spec]
    if res is not None:
        in_specs.append(pl.BlockSpec((tm, tn), lambda i, j, kk: (i, j)))
        args = args + [res]
    if out_slabs == 1:
        out_spec, out_shape = pl.BlockSpec((tm, tn), lambda i, j, kk: (i, j)), S((m, n), out_dtype)
    else:
        out_spec, out_shape = pl.BlockSpec((None, tm, tn), lambda i, j, kk: (j, i, 0)), S((out_slabs, m, tn), out_dtype)
    return pl.pallas_call(
        body, grid=grid, in_specs=in_specs, out_specs=out_spec, out_shape=out_shape,
        scratch_shapes=[pltpu.VMEM((tm, tn), F32)], name=name,
        compiler_params=_cp(("parallel", "parallel", "arbitrary")))(*args)


def _mm_nn(a, b, *, out_dtype, name, res=None, b_split=1, wide=False, out_slabs=1):
    m, k = a.shape
    n = b.shape[1] if b_split == 1 else b.shape[2] * b_split
    tn_cands = (n // out_slabs,) if out_slabs > 1 else (1024, 512, 256, 128) if wide else (512, 256, 128)
    tm, tn, tk = _mm_tiles(m, n // b_split, k, jnp.dtype(out_dtype).itemsize, res is not None, tn_cands=tn_cands,
                           tn_alts=1 if wide or out_slabs > 1 else 2,
                           tm_cands=(1024, 512) if wide or out_slabs > 1 else (2048, 1024, 512))
    assert out_slabs == 1 or (tn * out_slabs == n and tn % 128 == 0)
    nk = k // tk
    njs = (n // b_split) // tn
    body = _mm_body(lambda x, y: jnp.dot(x, y, preferred_element_type=F32), nk, res is not None)
    if b_split == 1:
        b_spec = pl.BlockSpec((tk, tn), lambda i, j, kk: (kk, j))
    else:
        b_spec = pl.BlockSpec((None, tk, tn), lambda i, j, kk: (j // njs, kk, j % njs))
    return _mm_call(body, (m // tm, n // tn, nk), pl.BlockSpec((tm, tk), lambda i, j, kk: (i, kk)), b_spec,
                    tm, tn, m, n, out_dtype, name, [a, b], res, out_slabs)


def _mm_nt(a, b, *, out_dtype, name, res=None, a_split=1):
    if a_split == 1:
        m, k = a.shape
    else:
        m, k = a.shape[1], a.shape[2] * a_split
    n = b.shape[0]
    tm, tn, tk = _mm_tiles(m, n, k, jnp.dtype(out_dtype).itemsize, res is not None, k_mult=a_split,
                           tm_cands=(2048, 1024))
    nk = k // tk
    nks = nk // a_split
    body = _mm_body(lambda x, y: lax.dot_general(x, y, (((1,), (1,)), ((), ())), preferred_element_type=F32),
                    nk, res is not None)
    if a_split == 1:
        a_spec = pl.BlockSpec((tm, tk), lambda i, j, kk: (i, kk))
    else:
        a_spec = pl.BlockSpec((None, tm, tk), lambda i, j, kk: (kk // nks, i, kk % nks))
    return _mm_call(body, (m // tm, n // tn, nk), a_spec, pl.BlockSpec((tn, tk), lambda i, j, kk: (j, kk)),
                    tm, tn, m, n, out_dtype, name, [a, b], res)


def _rms_fwd(x, g, name):
    t, d = x.shape
    tr = _pick(t, (512, 256, 128))

    def body(x_ref, g_ref, o_ref, ot_ref):
        y = _rms(x_ref[...], g_ref[...]).astype(o_ref.dtype)
        o_ref[...] = y
        ot_ref[...] = y.T

    return pl.pallas_call(
        body, grid=(t // tr,),
        in_specs=[pl.BlockSpec((tr, d), lambda i: (i, 0)), pl.BlockSpec((1, d), lambda i: (0, 0))],
        out_specs=[pl.BlockSpec((tr, d), lambda i: (i, 0)), pl.BlockSpec((d, tr), lambda i: (0, i))],
        out_shape=[S((t, d), ACT_DTYPE), S((d, t), ACT_DTYPE)], name=name,
        compiler_params=_cp(("parallel",)))(x, g)


def _rms_bwd(xin, g, dn, dres, name, mx_copy=True):
    t, d = xin.shape
    tr = _pick(t, (256, 128))

    def body(x_ref, g_ref, dn_ref, dr_ref, dx_ref, *rest):
        dg_ref = rest[-1]

        @pl.when(pl.program_id(0) == 0)
        def _():
            dg_ref[...] = jnp.zeros_like(dg_ref)

        _, vjp = jax.vjp(_rms, x_ref[...], g_ref[...])
        dx, dg = vjp(dn_ref[...].astype(F32))
        dx = dr_ref[...] + dx
        dx_ref[...] = dx
        if mx_copy:
            rest[0][...] = dx.astype(rest[0].dtype)
        dg_ref[0:1, :] += dg

    row = pl.BlockSpec((tr, d), lambda i: (i, 0))
    acc = pl.BlockSpec((8, d), lambda i: (0, 0))
    return pl.pallas_call(
        body, grid=(t // tr,),
        in_specs=[row, pl.BlockSpec((1, d), lambda i: (0, 0)), row, row],
        out_specs=[row, row, acc] if mx_copy else [row, acc],
        out_shape=[S((t, d), F32)] + ([S((t, d), MXU_DTYPE)] if mx_copy else []) + [S((8, d), F32)], name=name,
        compiler_params=_cp(("arbitrary",)))(xin, g, dn, dres)


def _head(h2, q, pe, tgt, gf, name):
    t, d = h2.shape
    tr = _pick(t, (256, 128))

    def f(h2v, qv, pev, gfv, tv):
        h3 = h2v + jax.nn.sigmoid(qv) * pev
        y = _rms(h3, gfv)
        return 0.5 * jnp.sum(jnp.mean(jnp.square(y - tv), axis=-1))

    def body(h2_ref, q_ref, pe_ref, t_ref, g_ref, loss_ref, dh_ref, dq_ref, dpe_ref, dg_ref):
        @pl.when(pl.program_id(0) == 0)
        def _():
            loss_ref[...] = jnp.zeros_like(loss_ref)
            dg_ref[...] = jnp.zeros_like(dg_ref)

        tv = t_ref[...]
        loss, vjp = jax.vjp(lambda a, b, c, e: f(a, b, c, e, tv), h2_ref[...], q_ref[...].astype(F32),
                            pe_ref[...].astype(F32), g_ref[...])
        dh, dq, dpe, dg = vjp(jnp.ones((), F32))
        loss_ref[...] += jnp.full(loss_ref.shape, loss, F32)
        dh_ref[...] = dh
        dq_ref[...] = dq.astype(dq_ref.dtype)
        dpe_ref[...] = dpe.astype(dpe_ref.dtype)
        dg_ref[0:1, :] += dg

    row = pl.BlockSpec((tr, d), lambda i: (i, 0))
    return pl.pallas_call(
        body, grid=(t // tr,),
        in_specs=[row, row, row, row, pl.BlockSpec((1, d), lambda i: (0, 0))],
        out_specs=[pl.BlockSpec((8, 128), lambda i: (0, 0)), row, row, row, pl.BlockSpec((8, d), lambda i: (0, 0))],
        out_shape=[S((8, 128), F32), S((t, d), F32), S((t, d), MXU_DTYPE), S((t, d), MXU_DTYPE), S((8, d), F32)],
        name=name, compiler_params=_cp(("arbitrary",)))(h2, q, pe, tgt, gf)


def _gmlp_block(us, vs, lng, lnb, wss, bss, ng):
    n = us[0].shape[0]
    row = lax.broadcasted_iota(jnp.int32, (n, n), 0)
    col = lax.broadcasted_iota(jnp.int32, (n, n), 1)
    outs = []
    for u0, v0, lg, lb, ws, bs in zip(us, vs, lng, lnb, wss, bss):
        u = jax.nn.gelu(u0)
        v = jax.nn.gelu(v0)
        mu = jnp.mean(v, axis=-1, keepdims=True)
        var = jnp.mean(jnp.square(v - mu), axis=-1, keepdims=True)
        vn = (v - mu) * lax.rsqrt(var + EPS) * lg + lb
        w = jnp.where(row >= col, ws, 0.0)
        sg = jnp.dot(_mx(w), _mx(vn), preferred_element_type=F32) + bs
        outs.append(u * sg)
    return _rms(jnp.concatenate(outs, axis=1), ng)


def _gmlp_load(proj_ref, lng_ref, lnb_ref, ws_ref, bst_ref, d_a, ng):
    sl = lambda g: slice(CHUNK * g, CHUNK * (g + 1))
    us = [proj_ref[:, sl(g)].astype(F32) for g in range(ng)]
    vs = [proj_ref[:, d_a + CHUNK * g: d_a + CHUNK * (g + 1)].astype(F32) for g in range(ng)]
    lng = [lng_ref[:, sl(g)] for g in range(ng)]
    lnb = [lnb_ref[:, sl(g)] for g in range(ng)]
    wss = [ws_ref[g] for g in range(ng)]
    bss = [bst_ref[:, g:g + 1] for g in range(ng)]
    return us, vs, lng, lnb, wss, bss


def _gmlp_fwd(proj, ln_g, ln_b, w_s, bst, norm_g, d_mix, name):
    t = proj.shape[0]
    ng = w_s.shape[0]
    d_a = ng * CHUNK

    def body(proj_ref, lng_ref, lnb_ref, ws_ref, bst_ref, ng_ref, o_ref, ot_ref):
        args = _gmlp_load(proj_ref, lng_ref, lnb_ref, ws_ref, bst_ref, d_a, ng)
        y = _gmlp_block(*args, ng_ref[...]).astype(o_ref.dtype)
        o_ref[...] = y
        ot_ref[...] = y.T

    vec = pl.BlockSpec((1, d_a), lambda c: (0, 0))
    return pl.pallas_call(
        body, grid=(t // CHUNK,),
        in_specs=[pl.BlockSpec((CHUNK, 2 * d_a), lambda c: (c, 0)), vec, vec,
                  pl.BlockSpec((ng, CHUNK, CHUNK), lambda c: (0, 0, 0)), pl.BlockSpec((CHUNK, 128), lambda c: (0, 0)), vec],
        out_specs=[pl.BlockSpec((CHUNK, d_a), lambda c: (c, 0)), pl.BlockSpec((d_a, CHUNK), lambda c: (0, c))],
        out_shape=[S((t, d_mix), ACT_DTYPE), S((d_mix, t), ACT_DTYPE)],
        name=name, compiler_params=_cp(("parallel",)))(proj, ln_g, ln_b, w_s, bst, norm_g)


def _gmlp_bwd(proj, ln_g, ln_b, w_s, bst, norm_g, dcat, d_proj, name):
    t = proj.shape[0]
    ng = w_s.shape[0]
    d_a = ng * CHUNK

    def body(proj_ref, lng_ref, lnb_ref, ws_ref, bst_ref, ng_ref, dy_ref,
             dp_ref, dlng_ref, dlnb_ref, dws_ref, dbst_ref, dng_ref):
        @pl.when(pl.program_id(0) == 0)
        def _():
            for r in (dlng_ref, dlnb_ref, dws_ref, dbst_ref, dng_ref):
                r[...] = jnp.zeros_like(r)

        args = _gmlp_load(proj_ref, lng_ref, lnb_ref, ws_ref, bst_ref, d_a, ng)
        _, vjp = jax.vjp(_gmlp_block, *args, ng_ref[...])
        dus, dvs, dlng, dlnb, dwss, dbss, dng = vjp(dy_ref[...].astype(F32))
        lane = lax.broadcasted_iota(jnp.int32, (1, 128), 1)
        dbst = jnp.zeros((CHUNK, 128), F32)
        for g in range(ng):
            dp_ref[:, CHUNK * g:CHUNK * (g + 1)] = dus[g].astype(dp_ref.dtype)
            dp_ref[:, d_a + CHUNK * g:d_a + CHUNK * (g + 1)] = dvs[g].astype(dp_ref.dtype)
            dlng_ref[0:1, CHUNK * g:CHUNK * (g + 1)] += dlng[g]
            dlnb_ref[0:1, CHUNK * g:CHUNK * (g + 1)] += dlnb[g]
            dws_ref[g] += dwss[g]
            dbst = dbst + dbss[g] * (lane == g).astype(F32)
        dbst_ref[...] += dbst
        dng_ref[0:1, :] += dng

    vec = pl.BlockSpec((1, d_a), lambda c: (0, 0))
    acc = pl.BlockSpec((8, d_a), lambda c: (0, 0))
    wspec = pl.BlockSpec((ng, CHUNK, CHUNK), lambda c: (0, 0, 0))
    bspec = pl.BlockSpec((CHUNK, 128), lambda c: (0, 0))
    return pl.pallas_call(
        body, grid=(t // CHUNK,),
        in_specs=[pl.BlockSpec((CHUNK, 2 * d_a), lambda c: (c, 0)), vec, vec, wspec, bspec, vec,
                  pl.BlockSpec((CHUNK, d_a), lambda c: (c, 0))],
        out_specs=[pl.BlockSpec((CHUNK, 2 * d_a), lambda c: (c, 0)), acc, acc, wspec, bspec, acc],
        out_shape=[S((t, d_proj), ACT_DTYPE), S((8, d_a), F32), S((8, d_a), F32), S((ng, CHUNK, CHUNK), F32),
                   S((CHUNK, 128), F32), S((8, d_a), F32)],
        name=name, compiler_params=_cp(("arbitrary",)))(proj, ln_g, ln_b, w_s, bst, norm_g, dcat)


def _silu_grad(c):
    s = jax.nn.sigmoid(c)
    return s * (1.0 + c * (1.0 - s))


def _fill_prev_main(s_ref, prev_ref, main_ref, i, tt):
    s_ref[pl.ds(0, HALO), :] = jnp.where(i > 0, prev_ref[...].astype(F32), 0.0)
    s_ref[pl.ds(HALO, tt), :] = main_ref[...].astype(F32)


def _prev_spec(tt, tc, joff):
    return pl.BlockSpec((HALO, tc), lambda j, i: (jnp.maximum(i * (tt // HALO) - 1, 0), j + joff))


def _next_spec(tt, tc, joff, t):
    return pl.BlockSpec((HALO, tc), lambda j, i: (jnp.minimum((i + 1) * (tt // HALO), t // HALO - 1), j + joff))


CONV_RC = 32


def _conv_tiles(t, c):
    return _pick(t, (1024, 512, 256, 128)), _pick(c, (256, 128))


def _row_chunks(tt, fn, init=0):
    rc = min(CONV_RC, tt)
    return lax.fori_loop(0, tt // rc, lambda q, c: fn(pl.multiple_of(q * rc, rc), rc, c), init)


def _fold8(p):
    acc = p[0:8]
    for r in range(8, p.shape[0], 8):
        acc = acc + p[r:r + 8]
    return acc


def _taps_chunk(s_ref, w_ref, kw, r0, rc):
    xe = s_ref[pl.ds(HALO - 8 + r0, rc + 8), :]
    acc = w_ref[0:1, :] * xe[8 - (kw - 1):8 - (kw - 1) + rc]
    for k in range(1, kw):
        acc = acc + w_ref[k:k + 1, :] * xe[8 - (kw - 1) + k:8 - (kw - 1) + k + rc]
    return acc


def _conv_bwd_chunk(sd_ref, x, w_ref, kw, r0, rc, dws):
    de = sd_ref[pl.ds(r0, rc + 8), :]
    dx, out = None, list(dws)
    for j in range(kw):
        d = de[j:j + rc]
        k = kw - 1 - j
        term = w_ref[k:k + 1, :] * d
        dx = term if dx is None else dx + term
        out[k] = out[k] + _fold8(x * d)
    return dx, out


def _conv_ssm_fwd(proj, col0, w, b, name):
    t = proj.shape[0]
    kw, c = w.shape
    tt, tc = _conv_tiles(t, c)
    joff = col0 // tc
    assert col0 % tc == 0

    def body(x_ref, xp_ref, w_ref, b_ref, o_ref, c_ref, s_ref):
        _fill_prev_main(s_ref, xp_ref, x_ref, pl.program_id(1), tt)

        def chunk(r0, rc, carry):
            cpre = _taps_chunk(s_ref, w_ref, kw, r0, rc) + b_ref[...]
            o_ref[pl.ds(r0, rc), :] = jax.nn.silu(cpre).astype(o_ref.dtype)
            c_ref[pl.ds(r0, rc), :] = cpre.astype(c_ref.dtype)
            return carry

        _row_chunks(tt, chunk)

    out = pl.BlockSpec((tt, tc), lambda j, i: (i, j))
    return pl.pallas_call(
        body, grid=(c // tc, t // tt),
        in_specs=[pl.BlockSpec((tt, tc), lambda j, i: (i, j + joff)), _prev_spec(tt, tc, joff),
                  pl.BlockSpec((kw, tc), lambda j, i: (0, j)), pl.BlockSpec((1, tc), lambda j, i: (0, j))],
        out_specs=[out, out], out_shape=[S((t, c), XBC_DTYPE), S((t, c), ACT_DTYPE)],
        scratch_shapes=[pltpu.VMEM((HALO + tt, tc), F32)], name=name,
        compiler_params=_cp(("parallel", "arbitrary")))(proj, proj, w, b)


def _conv_ssm_bwd(proj, col0, w, cpre, wcol0, dact, dproj, out_col0, name):
    t = proj.shape[0]
    kw = w.shape[0]
    c = dact.shape[1]
    tt, tc = _conv_tiles(t, c)
    assert col0 % tc == 0 and wcol0 % tc == 0 and out_col0 % tc == 0
    joff, wj, oj = col0 // tc, wcol0 // tc, out_col0 // tc
    nt = t // tt

    def body(x_ref, w_ref, c_ref, cn_ref, d_ref, dn_ref, dp_in, dx_ref, dw_ref, db_ref, sd_ref):
        del dp_in
        i = pl.program_id(1)

        @pl.when(i == 0)
        def _():
            dw_ref[...] = jnp.zeros_like(dw_ref)
            db_ref[...] = jnp.zeros_like(db_ref)

        def stage(r0, rc, db):
            rows = pl.ds(r0, rc)
            d = d_ref[rows, :].astype(F32) * _silu_grad(c_ref[rows, :].astype(F32))
            sd_ref[rows, :] = d
            return db + _fold8(d)

        zero8 = jnp.zeros((8, tc), F32)
        db = _row_chunks(tt, stage, zero8)
        sd_ref[pl.ds(tt, HALO), :] = jnp.where(
            i < nt - 1, dn_ref[...].astype(F32) * _silu_grad(cn_ref[...].astype(F32)), 0.0)

        def chunk(r0, rc, dws):
            dx, dws = _conv_bwd_chunk(sd_ref, x_ref[pl.ds(r0, rc), :].astype(F32), w_ref, kw, r0, rc, dws)
            dx_ref[pl.ds(r0, rc), :] = dx.astype(dx_ref.dtype)
            return dws

        dws = _row_chunks(tt, chunk, [zero8] * kw)
        for k in range(kw):
            dw_ref[k:k + 1, :] += jnp.sum(dws[k], axis=0, keepdims=True)
        db_ref[0:1, :] += jnp.sum(db, axis=0, keepdims=True)

    acc = pl.BlockSpec((8, tc), lambda j, i: (0, j))
    return pl.pallas_call(
        body, grid=(c // tc, nt),
        in_specs=[pl.BlockSpec((tt, tc), lambda j, i: (i, j + joff)), pl.BlockSpec((kw, tc), lambda j, i: (0, j + wj)),
                  pl.BlockSpec((tt, tc), lambda j, i: (i, j + wj)), _next_spec(tt, tc, wj, t),
                  pl.BlockSpec((tt, tc), lambda j, i: (i, j)), _next_spec(tt, tc, 0, t),
                  pl.BlockSpec(memory_space=pl.ANY)],
        out_specs=[pl.BlockSpec((tt, tc), lambda j, i: (i, j + oj)), acc, acc],
        out_shape=[S(dproj.shape, dproj.dtype), S((8, c), F32), S((8, c), F32)],
        scratch_shapes=[pltpu.VMEM((tt + HALO, tc), F32)],
        input_output_aliases={6: 0}, name=name,
        compiler_params=_cp(("parallel", "arbitrary")))(proj, w, cpre, cpre, dact, dact, dproj)


def _conv_ffn_fwd(hid, w, b, name):
    t, f2 = hid.shape
    f = f2 // 2
    kw = w.shape[0]
    tt, tc = _conv_tiles(t, f)
    nj = f // tc

    def body(g_ref, gp_ref, u_ref, up_ref, wg_ref, wu_ref, bg_ref, bu_ref, o_ref, ot_ref, cv_ref, sg_ref, su_ref):
        i = pl.program_id(1)
        _fill_prev_main(sg_ref, gp_ref, g_ref, i, tt)
        _fill_prev_main(su_ref, up_ref, u_ref, i, tt)

        def chunk(r0, rc, carry):
            rows = pl.ds(r0, rc)
            gate = _taps_chunk(sg_ref, wg_ref, kw, r0, rc) + bg_ref[...]
            up = _taps_chunk(su_ref, wu_ref, kw, r0, rc) + bu_ref[...]
            o_ref[rows, :] = (jax.nn.silu(gate) * up).astype(o_ref.dtype)
            cv_ref[0, rows, :] = gate.astype(cv_ref.dtype)
            cv_ref[1, rows, :] = up.astype(cv_ref.dtype)
            return carry

        _row_chunks(tt, chunk)
        ot_ref[...] = o_ref[...].T

    return pl.pallas_call(
        body, grid=(nj, t // tt),
        in_specs=[pl.BlockSpec((tt, tc), lambda j, i: (i, j)), _prev_spec(tt, tc, 0),
                  pl.BlockSpec((tt, tc), lambda j, i: (i, j + nj)), _prev_spec(tt, tc, nj),
                  pl.BlockSpec((kw, tc), lambda j, i: (0, j)), pl.BlockSpec((kw, tc), lambda j, i: (0, j + nj)),
                  pl.BlockSpec((1, tc), lambda j, i: (0, j)), pl.BlockSpec((1, tc), lambda j, i: (0, j + nj))],
        out_specs=[pl.BlockSpec((tt, tc), lambda j, i: (i, j)), pl.BlockSpec((tc, tt), lambda j, i: (j, i)),
                   pl.BlockSpec((2, tt, tc), lambda j, i: (0, i, j))],
        out_shape=[S((t, f), ACT_DTYPE), S((f, t), ACT_DTYPE), S((2, t, f), ACT_DTYPE)],
        scratch_shapes=[pltpu.VMEM((HALO + tt, tc), F32), pltpu.VMEM((HALO + tt, tc), F32)], name=name,
        compiler_params=_cp(("parallel", "arbitrary")))(hid, hid, hid, hid, w, w, b, b)


def _conv_ffn_bwd(hid, w, cv, dact, name):
    t, f2 = hid.shape
    f = f2 // 2
    kw = w.shape[0]
    tt, tc = _conv_tiles(t, f)
    nj = f // tc
    nt = t // tt

    def body(g_ref, u_ref, wg_ref, wu_ref, cv_ref, cvn_ref, d_ref, dn_ref, dh_ref, dw_ref, db_ref, dg_ref, du_ref):
        i = pl.program_id(1)

        @pl.when(i == 0)
        def _():
            dw_ref[...] = jnp.zeros_like(dw_ref)
            db_ref[...] = jnp.zeros_like(db_ref)

        def cotangents(gate, up, dact_v):
            sg = jax.nn.sigmoid(gate)
            return dact_v * up * (sg * (1.0 + gate * (1.0 - sg))), dact_v * (gate * sg)

        def stage(r0, rc, dbs):
            rows = pl.ds(r0, rc)
            dg, du = cotangents(cv_ref[0, rows, :].astype(F32), cv_ref[1, rows, :].astype(F32), d_ref[rows, :].astype(F32))
            dg_ref[rows, :] = dg
            du_ref[rows, :] = du
            return [dbs[0] + _fold8(dg), dbs[1] + _fold8(du)]

        zero8 = jnp.zeros((8, tc), F32)
        dbs = _row_chunks(tt, stage, [zero8, zero8])
        dgn, dun = cotangents(cvn_ref[0].astype(F32), cvn_ref[1].astype(F32), dn_ref[...].astype(F32))
        dg_ref[pl.ds(tt, HALO), :] = jnp.where(i < nt - 1, dgn, 0.0)
        du_ref[pl.ds(tt, HALO), :] = jnp.where(i < nt - 1, dun, 0.0)
        for s, (sd_ref, x_ref, w_ref) in enumerate(((dg_ref, g_ref, wg_ref), (du_ref, u_ref, wu_ref))):
            def chunk(r0, rc, dws, s=s, sd_ref=sd_ref, x_ref=x_ref, w_ref=w_ref):
                dx, dws = _conv_bwd_chunk(sd_ref, x_ref[pl.ds(r0, rc), :].astype(F32), w_ref, kw, r0, rc, dws)
                dh_ref[s, pl.ds(r0, rc), :] = dx.astype(dh_ref.dtype)
                return dws

            dws = _row_chunks(tt, chunk, [zero8] * kw)
            for k in range(kw):
                dw_ref[s, k:k + 1, :] += jnp.sum(dws[k], axis=0, keepdims=True)
            db_ref[s, 0:1, :] += jnp.sum(dbs[s], axis=0, keepdims=True)

    acc = pl.BlockSpec((2, 8, tc), lambda j, i: (0, 0, j))
    dsc = pltpu.VMEM((tt + HALO, tc), F32)
    nxt = lambda j, i: (0, jnp.minimum((i + 1) * (tt // HALO), t // HALO - 1), j)
    return pl.pallas_call(
        body, grid=(nj, nt),
        in_specs=[pl.BlockSpec((tt, tc), lambda j, i: (i, j)), pl.BlockSpec((tt, tc), lambda j, i: (i, j + nj)),
                  pl.BlockSpec((kw, tc), lambda j, i: (0, j)), pl.BlockSpec((kw, tc), lambda j, i: (0, j + nj)),
                  pl.BlockSpec((2, tt, tc), lambda j, i: (0, i, j)), pl.BlockSpec((2, HALO, tc), nxt),
                  pl.BlockSpec((tt, tc), lambda j, i: (i, j)), _next_spec(tt, tc, 0, t)],
        out_specs=[pl.BlockSpec((2, tt, tc), lambda j, i: (0, i, j)), acc, acc],
        out_shape=[S((2, t, f), MXU_DTYPE), S((2, 8, f), F32), S((2, 8, f), F32)],
        scratch_shapes=[dsc, dsc], name=name,
        compiler_params=_cp(("parallel", "arbitrary")))(hid, hid, w, w, cv, cv, dact, dact)


SSD_SUB = 2


def _ssd_chunk(xs, bm, cm, dtraw, hin, bias, alog, dskip):
    n = bm.shape[0]
    row = lax.broadcasted_iota(jnp.int32, (n, n), 0)
    col = lax.broadcasted_iota(jnp.int32, (n, n), 1)
    causal = row >= col
    lane = lax.broadcasted_iota(jnp.int32, (1, 128), 1)
    sub = lax.broadcasted_iota(jnp.int32, (128, 1), 0)
    last = (lax.broadcasted_iota(jnp.int32, (n, 1), 0) == n - 1).astype(F32)
    dt = jax.nn.softplus(dtraw + bias)
    adt = dt * (-jnp.exp(alog))
    tri = causal.astype(F32)
    acs = jnp.dot(tri, adt, preferred_element_type=F32, precision=lax.Precision.HIGHEST)
    ch = lax.broadcasted_iota(jnp.int32, (128, GROUP_CH), 1)
    hd = lax.broadcasted_iota(jnp.int32, (128, GROUP_CH), 0) * HEAD_DIM
    expand = ((ch >= hd) & (ch < hd + HEAD_DIM)).astype(F32)
    acs_x = jnp.dot(acs, expand, preferred_element_type=F32, precision=lax.Precision.HIGHEST)
    alast_x = jnp.sum(acs_x * last, axis=0, keepdims=True)
    acs_t = acs.T
    scores = lax.dot_general(_mx(cm), _mx(bm), (((1,), (1,)), ((), ())), preferred_element_type=F32)
    yds, xts = [], []
    for r in range(HEADS_PER_GROUP):
        pick = (lane == r).astype(F32)
        acol = jnp.sum(acs * pick, axis=1, keepdims=True)
        arow = jnp.sum(acs_t * (sub == r).astype(F32), axis=0, keepdims=True)
        dtc = jnp.sum(dt * pick, axis=1, keepdims=True)
        lm = jnp.exp(jnp.where(causal, acol - arow, -1e30))
        xts.append(xs[r] * dtc)
        yds.append(jnp.dot(_mx(scores * lm), _mx(xts[r]), preferred_element_type=F32))
    xt = jnp.concatenate(xts, axis=1)
    yo = jnp.exp(acs_x) * jnp.dot(_mx(cm), _mx(hin), preferred_element_type=F32)
    st = lax.dot_general(_mx(bm), _mx(xt * jnp.exp(alast_x - acs_x)), (((0,), (0,)), ((), ())), preferred_element_type=F32)
    hout = jnp.exp(alast_x) * hin + st
    return jnp.concatenate(yds, axis=1) + yo + dskip * jnp.concatenate(xs, axis=1), hout


def _ssd_post(y, z, normg):
    return _rms(y * jax.nn.silu(z), normg)


def _ssd_block(datas, hin, consts):
    ys = []
    for data in datas:
        y, hin = _ssd_chunk(*data, hin, *consts)
        ys.append(y)
    return ys, hin


def _ssd_specs(d_ssm, ngrp, zcol0, rev, nb):
    cc = (lambda c: nb - 1 - c) if rev else (lambda c: c)
    rows = SSD_SUB * CHUNK
    xj, bj, cj, zj = 0, d_ssm // 128, d_ssm // 128 + ngrp, zcol0 // GROUP_CH
    const = lambda w: pl.BlockSpec((None, 8, w), lambda g, c: (g, 0, 0))
    return cc, [
        pl.BlockSpec((rows, GROUP_CH), lambda g, c: (cc(c), xj + g)),
        pl.BlockSpec((rows, 128), lambda g, c: (cc(c), bj + g)),
        pl.BlockSpec((rows, 128), lambda g, c: (cc(c), cj + g)),
        pl.BlockSpec((rows, 128), lambda g, c: (cc(c), g)),
        pl.BlockSpec((rows, GROUP_CH), lambda g, c: (cc(c), zj + g)),
        const(128), const(128), const(GROUP_CH), const(GROUP_CH)]


def _sub_rows(s):
    return slice(CHUNK * s, CHUNK * (s + 1))


def _ssd_load(x_ref, b_ref, c_ref, dt_ref, z_ref, bias_ref, alog_ref, dsk_ref, ng_ref):
    datas, zs = [], []
    for s in range(SSD_SUB):
        rows = _sub_rows(s)
        xs = [x_ref[rows, HEAD_DIM * r:HEAD_DIM * (r + 1)].astype(F32) for r in range(HEADS_PER_GROUP)]
        datas.append((xs, b_ref[rows, :].astype(F32), c_ref[rows, :].astype(F32), dt_ref[rows, :]))
        zs.append(z_ref[rows, :].astype(F32))
    return datas, zs, (bias_ref[0:1, :], alog_ref[0:1, :], dsk_ref[0:1, :]), ng_ref[0:1, :]


def _ssd_fwd(xbc, dtp, proj, zcol0, bias_p, alog_p, dskip_x, normg_x, cat, cat_t, name):
    t = xbc.shape[0]
    ngrp = bias_p.shape[0]
    d_ssm = ngrp * GROUP_CH
    rows = SSD_SUB * CHUNK
    nb = t // rows
    d_a = cat.shape[1] - d_ssm
    assert d_a % GROUP_CH == 0 and zcol0 % GROUP_CH == 0 and t % rows == 0
    _, specs = _ssd_specs(d_ssm, ngrp, zcol0, False, nb)

    def body(x_ref, b_ref, c_ref, dt_ref, z_ref, bias_ref, alog_ref, dsk_ref, ng_ref, cat_in, catt_in,
             yn_ref, ynt_ref, y_ref, hs_ref, h_ref):
        del cat_in, catt_in

        @pl.when(pl.program_id(1) == 0)
        def _():
            h_ref[...] = jnp.zeros_like(h_ref)

        datas, zs, consts, normg = _ssd_load(x_ref, b_ref, c_ref, dt_ref, z_ref, bias_ref, alog_ref, dsk_ref, ng_ref)
        hs_ref[...] = h_ref[...]
        ys, hout = _ssd_block(datas, h_ref[...], consts)
        for s in range(SSD_SUB):
            y_ref[_sub_rows(s), :] = ys[s].astype(y_ref.dtype)
            yn = _ssd_post(ys[s], zs[s], normg).astype(yn_ref.dtype)
            yn_ref[_sub_rows(s), :] = yn
            ynt_ref[:, _sub_rows(s)] = yn.T
        h_ref[...] = hout

    hbm = pl.BlockSpec(memory_space=pl.ANY)
    return pl.pallas_call(
        body, grid=(ngrp, nb), in_specs=specs + [hbm, hbm],
        out_specs=[pl.BlockSpec((rows, GROUP_CH), lambda g, c: (c, d_a // GROUP_CH + g)),
                   pl.BlockSpec((GROUP_CH, rows), lambda g, c: (d_a // GROUP_CH + g, c)),
                   pl.BlockSpec((rows, GROUP_CH), lambda g, c: (c, g)),
                   pl.BlockSpec((None, None, D_STATE, GROUP_CH), lambda g, c: (c, g, 0, 0))],
        out_shape=[S(cat.shape, cat.dtype), S(cat_t.shape, cat_t.dtype), S((t, d_ssm), ACT_DTYPE),
                   S((nb, ngrp, D_STATE, GROUP_CH), F32)],
        scratch_shapes=[pltpu.VMEM((D_STATE, GROUP_CH), F32)],
        input_output_aliases={9: 0, 10: 1}, name=name,
        compiler_params=_cp(("parallel", "arbitrary")))(xbc, xbc, xbc, dtp, proj, bias_p, alog_p, dskip_x, normg_x, cat, cat_t)


def _ssd_bwd(xbc, dtp, proj, zcol0, bias_p, alog_p, dskip_x, normg_x, hs, ypre, dcat, dproj, name):
    t = xbc.shape[0]
    ngrp = bias_p.shape[0]
    d_ssm = ngrp * GROUP_CH
    rows = SSD_SUB * CHUNK
    nb = t // rows
    d_a = dcat.shape[1] - d_ssm
    cc, specs = _ssd_specs(d_ssm, ngrp, zcol0, True, nb)

    def body(x_ref, b_ref, c_ref, dt_ref, z_ref, bias_ref, alog_ref, dsk_ref, ng_ref, hs_ref, yp_ref, dy_ref, dp_in,
             dz_ref, dx_ref, db_ref, dc_ref, ddt_ref, dbias_ref, dalog_ref, ddsk_ref, dng_ref, dh_ref):
        del dp_in

        @pl.when(pl.program_id(1) == 0)
        def _():
            dh_ref[...] = jnp.zeros_like(dh_ref)
            for r in (dbias_ref, dalog_ref, ddsk_ref, dng_ref):
                r[...] = jnp.zeros_like(r)

        datas, zs, consts, normg = _ssd_load(x_ref, b_ref, c_ref, dt_ref, z_ref, bias_ref, alog_ref, dsk_ref, ng_ref)
        dys, dng = [], jnp.zeros_like(normg)
        for s in range(SSD_SUB):
            rws = _sub_rows(s)
            _, vjp_post = jax.vjp(_ssd_post, yp_ref[rws, :].astype(F32), zs[s], normg)
            dy, dz, dg = vjp_post(dy_ref[rws, :].astype(F32))
            dys.append(dy)
            dng = dng + dg
            dz_ref[rws, :] = dz.astype(dz_ref.dtype)
        _, vjp = jax.vjp(_ssd_block, datas, hs_ref[...], consts)
        ddatas, dhin, (dbias, dalog, ddsk) = vjp((dys, dh_ref[...]))
        for s, (dxs, dbm, dcm, ddt) in enumerate(ddatas):
            rws = _sub_rows(s)
            for r in range(HEADS_PER_GROUP):
                dx_ref[rws, HEAD_DIM * r:HEAD_DIM * (r + 1)] = dxs[r].astype(dx_ref.dtype)
            db_ref[rws, :] = dbm.astype(db_ref.dtype)
            dc_ref[rws, :] = dcm.astype(dc_ref.dtype)
            ddt_ref[rws, :] = ddt
        dh_ref[...] = dhin
        dbias_ref[0:1, :] += dbias
        dalog_ref[0:1, :] += dalog
        ddsk_ref[0:1, :] += ddsk
        dng_ref[0:1, :] += dng

    acc = lambda w: pl.BlockSpec((None, 8, w), lambda g, c: (g, 0, 0))
    blk = lambda w: pl.BlockSpec((rows, w), lambda g, c: (cc(c), g))
    return pl.pallas_call(
        body, grid=(ngrp, nb),
        in_specs=specs + [pl.BlockSpec((None, None, D_STATE, GROUP_CH), lambda g, c: (cc(c), g, 0, 0)),
                          blk(GROUP_CH),
                          pl.BlockSpec((rows, GROUP_CH), lambda g, c: (cc(c), d_a // GROUP_CH + g)),
                          pl.BlockSpec(memory_space=pl.ANY)],
        out_specs=[pl.BlockSpec((rows, GROUP_CH), lambda g, c: (cc(c), zcol0 // GROUP_CH + g)),
                   blk(GROUP_CH), blk(128), blk(128), blk(128), acc(128), acc(128), acc(GROUP_CH), acc(GROUP_CH)],
        out_shape=[S(dproj.shape, dproj.dtype), S((t, d_ssm), XBC_DTYPE), S((t, ngrp * 128), XBC_DTYPE),
                   S((t, ngrp * 128), XBC_DTYPE), S((t, ngrp * 128), F32), S((ngrp, 8, 128), F32),
                   S((ngrp, 8, 128), F32), S((ngrp, 8, GROUP_CH), F32), S((ngrp, 8, GROUP_CH), F32)],
        scratch_shapes=[pltpu.VMEM((D_STATE, GROUP_CH), F32)],
        input_output_aliases={12: 0}, name=name,
        compiler_params=_cp(("parallel", "arbitrary")))(xbc, xbc, xbc, dtp, proj, bias_p, alog_p, dskip_x, normg_x, hs, ypre,
                                                        dcat, dproj)


def _adamw(parts, w, m, v, name):
    r, c = w.shape
    tr = _pick(r, (256, 128, 64, 32, 16, 8)) if c * 4 * 256 <= 4 * 1024 * 1024 else _pick(r, (64, 32, 16, 8))

    def body(p_ref, w_ref, m_ref, v_ref, g_ref, d_ref, nm_ref, nv_ref):
        g = p_ref[0].astype(F32)
        for k in range(1, N_DEV):
            g = g + p_ref[k].astype(F32)
        mm = ADAM_B1 * m_ref[...] + (1.0 - ADAM_B1) * g
        vv = ADAM_B2 * v_ref[...] + (1.0 - ADAM_B2) * jnp.square(g)
        m_hat = mm / (1.0 - ADAM_B1 ** ADAM_STEP)
        v_hat = vv / (1.0 - ADAM_B2 ** ADAM_STEP)
        g_ref[...] = g
        d_ref[...] = -ADAM_LR * (m_hat / (jnp.sqrt(v_hat) + ADAM_EPS) + ADAM_WD * w_ref[...])
        nm_ref[...] = mm
        nv_ref[...] = vv

    blk = pl.BlockSpec((tr, c), lambda i: (i, 0))
    return pl.pallas_call(
        body, grid=(r // tr,),
        in_specs=[pl.BlockSpec((N_DEV, tr, c), lambda i: (0, i, 0)), blk, blk, blk],
        out_specs=[blk, blk, blk, blk], out_shape=[S((r, c), F32)] * 4, name=name,
        compiler_params=_cp(("parallel",)))(parts, w, m, v)


def _mesh_pos():
    return lax.axis_index("x"), lax.axis_index("y"), lax.axis_index("c")


def _peer(d, x, y, c):
    return (1 - x if (d >> 2) & 1 else x, 1 - y if (d >> 1) & 1 else y, 1 - c if d & 1 else c)


def _gather_two_level(blocks, name):
    n = len(blocks)

    def body(*refs):
        srcs, outs = refs[:n], refs[n:2 * n]
        send_sems, recv_sems, loc_sems = refs[2 * n:]
        x, y, c = _mesh_pos()
        lin = lambda px, py, pc: 4 * px + 2 * py + pc
        me, sibling = (x, y, c), (x, y, 1 - c)
        chips = [(1 - x, y), (x, 1 - y), (1 - x, 1 - y)]

        def copy(a, k, block, to, src=None):
            slab = outs[a].at[lin(*block)]
            return pltpu.make_async_remote_copy(
                src_ref=slab if src is None else src, dst_ref=slab, send_sem=send_sems.at[a, k],
                recv_sem=recv_sems.at[a, k], device_id=to, device_id_type=pl.DeviceIdType.MESH)

        mine = [pltpu.make_async_copy(srcs[a], outs[a].at[lin(*me)], loc_sems.at[a]) for a in range(n)]
        first = [copy(a, 0, me, sibling, src=srcs[a]) for a in range(n)]
        first += [copy(a, 1 + j, me, (*chip, c), src=srcs[a]) for j, chip in enumerate(chips) for a in range(n)]
        for cp in mine + first:
            cp.start()
        passed = []
        for j, chip in enumerate(chips):
            for a in range(n):
                copy(a, 1 + j, (*chip, c), me).wait_recv()
                passed.append(copy(a, 4 + j, (*chip, c), sibling))
                passed[-1].start()
        for a in range(n):
            copy(a, 0, sibling, me).wait_recv()
            for j, chip in enumerate(chips):
                copy(a, 4 + j, (*chip, 1 - c), me).wait_recv()
        for cp in first + passed:
            cp.wait_send()
        for cp in mine:
            cp.wait()

    hbm = pl.BlockSpec(memory_space=pl.ANY)
    return pl.pallas_call(
        body, in_specs=[hbm] * n, out_specs=[hbm] * n, out_shape=[S((N_DEV,) + b.shape, b.dtype) for b in blocks],
        scratch_shapes=[pltpu.SemaphoreType.DMA((n, N_DEV - 1)), pltpu.SemaphoreType.DMA((n, N_DEV - 1)),
                        pltpu.SemaphoreType.DMA((n,))],
        name=name, compiler_params=pltpu.CompilerParams(has_side_effects=True))(*blocks)


def _xfer_start(items, name, after=None):
    n = len(items)
    kinds = [k for k, _ in items]
    srcs = [pltpu.with_memory_space_constraint(a, pltpu.HBM) for _, a in items]
    land_shapes = [((N_DEV,) + a.shape if k == 'gather' else a.shape, a.dtype) for k, a in items]
    lands = [pltpu.with_memory_space_constraint(lax.empty(s, dt), pltpu.HBM) for s, dt in land_shapes]
    extra = [] if after is None else [after]

    def body(*refs):
        src_refs, land_refs = refs[:n], refs[n:2 * n]
        outs = refs[2 * n + len(extra):]
        sems = outs[:2 * n]
        token = outs[4 * n]
        x, y, c = _mesh_pos()
        me = 4 * x + 2 * y + c
        for a in range(n):
            for d in range(1, N_DEV):
                px, py, pc = _peer(d, x, y, c)
                src = src_refs[a] if kinds[a] == 'gather' else src_refs[a].at[4 * px + 2 * py + pc]
                pltpu.make_async_remote_copy(
                    src_ref=src, dst_ref=land_refs[a].at[me], send_sem=sems[2 * a].at[d - 1],
                    recv_sem=sems[2 * a + 1].at[d - 1], device_id=(px, py, pc),
                    device_id_type=pl.DeviceIdType.MESH).start()
        token[...] = jnp.zeros_like(token)

    hbm = pl.BlockSpec(memory_space=pltpu.HBM)
    sem = pl.BlockSpec(memory_space=pltpu.SEMAPHORE)
    out_shape = ([pltpu.SemaphoreType.DMA((N_DEV - 1,))] * (2 * n)
                 + [pltpu.HBM(a.shape, a.dtype) for a in srcs] + [pltpu.HBM(s, dt) for s, dt in land_shapes]
                 + [S((8, 128), F32)])
    res = pl.pallas_call(
        body, name=name, out_shape=out_shape,
        in_specs=[hbm] * (2 * n) + [pl.BlockSpec(memory_space=pl.ANY)] * len(extra),
        out_specs=[sem] * (2 * n) + [hbm] * (2 * n) + [pl.BlockSpec(memory_space=pltpu.VMEM)],
        input_output_aliases={**{a: 2 * n + a for a in range(n)}, **{n + a: 3 * n + a for a in range(n)}},
        compiler_params=pltpu.CompilerParams(has_side_effects=pltpu.SideEffectType.DATAFLOW_SIDE_EFFECTING),
    )(*srcs, *lands, *extra)
    return (kinds, res[:2 * n], res[2 * n:3 * n], res[3 * n:4 * n]), res[4 * n]


def _xfer_wait(handle, after, name):
    kinds, sems, src_thru, land_thru = handle
    n = len(kinds)

    def body(*refs):
        land_refs = refs[n:2 * n]
        sem_refs = refs[2 * n:4 * n]
        x, y, c = _mesh_pos()
        me = 4 * x + 2 * y + c
        for a in range(n):
            for d in range(1, N_DEV):
                slab = land_refs[a].at[me]
                cp = pltpu.make_async_remote_copy(
                    src_ref=slab, dst_ref=slab, send_sem=sem_refs[2 * a].at[d - 1], recv_sem=sem_refs[2 * a + 1].at[d - 1],
                    device_id=_peer(d, x, y, c), device_id_type=pl.DeviceIdType.MESH)
                cp.wait_send()
                cp.wait_recv()

    hbm = pl.BlockSpec(memory_space=pltpu.HBM)
    sem = pl.BlockSpec(memory_space=pltpu.SEMAPHORE)
    res = pl.pallas_call(
        body, name=name,
        out_shape=[pltpu.HBM(a.shape, a.dtype) for a in src_thru] + [pltpu.HBM(a.shape, a.dtype) for a in land_thru],
        in_specs=[hbm] * (2 * n) + [sem] * (2 * n) + [pl.BlockSpec(memory_space=pl.ANY)],
        out_specs=[hbm] * (2 * n), input_output_aliases={a: a for a in range(2 * n)},
        compiler_params=pltpu.CompilerParams(has_side_effects=pltpu.SideEffectType.DATAFLOW_SIDE_EFFECTING),
    )(*src_thru, *land_thru, *sems, after)
    x, y, c = _mesh_pos()
    me = 4 * x + 2 * y + c
    out = []
    for a in range(n):
        src = res[a]
        own = src[None] if kinds[a] == 'gather' else lax.dynamic_index_in_dim(src, me, 0, keepdims=True)
        out.append(lax.dynamic_update_index_in_dim(res[n + a], own, me, 0))
    return out


def _stack_to_full(kind, st):
    if kind == 'row':
        return st.reshape(st.shape[0] * st.shape[1], st.shape[2])
    return jnp.concatenate([st[k] for k in range(st.shape[0])], axis=1)


def _full_to_stack(kind, full):
    r, c = full.shape
    if kind == 'row':
        return full.reshape(N_DEV, r // N_DEV, c)
    w = c // N_DEV
    return jnp.stack([full[:, k * w:(k + 1) * w] for k in range(N_DEV)], axis=0)


SMALL_ROWS = 256


def _pack_small(named):
    layout = [(a.shape, a.size, -(-a.size // 1024) * 8) for a in named]
    total = -(-sum(nr for _, _, nr in layout) // SMALL_ROWS) * SMALL_ROWS * 128
    packed, off = None, 0
    for a, (_, n, nr) in zip(named, layout):
        part = jnp.pad(a.reshape(-1).astype(F32), (off, total - off - n))
        packed = part if packed is None else packed + part
        off += nr * 128
    return packed.reshape(total // 128, 128), layout


def _unpack_small(packed, layout):
    out, r0 = [], 0
    for shape, n, nr in layout:
        out.append(packed[r0:r0 + nr].reshape(-1)[:n].reshape(shape))
        r0 += nr
    return out


def _row0(acc):
    return acc[0]


def _local_step(x, p, tgt, sm, comm):
    t, d = x.shape
    h_n = sm['dt_bias'].shape[-1]
    ngrp = h_n // HEADS_PER_GROUP
    d_ssm = h_n * HEAD_DIM
    d_a = sm['ln_a_g'].shape[-1]
    d_mix = d_a + d_ssm
    d_xbc = sm['conv_ssm_b'].shape[-1]
    d_main = 2 * d_a + d_ssm + d_xbc
    assert d_xbc == d_ssm + 2 * ngrp * D_STATE and h_n <= 128
    zcol0, xcol0 = 2 * d_a, 2 * d_a + d_ssm
    vec = lambda v: v.reshape(1, -1)

    bst = jnp.pad(sm['b_s'].T, ((0, 0), (0, 128 - sm['b_s'].shape[0])))
    grp = lambda v, w: jnp.broadcast_to(jnp.pad(v.reshape(ngrp, 1, -1), ((0, 0), (0, 0), (0, w - v.size // ngrp))), (ngrp, 8, w))
    bias_p, alog_p = grp(sm['dt_bias'], 128), grp(sm['a_log'], 128)
    dskip_x = grp(jnp.repeat(sm['d_skip'], HEAD_DIM), GROUP_CH)
    normg_x = grp(sm['ssm_norm_g'], GROUP_CH)
    pad_dt = lambda v: jnp.pad(v[:, :h_n].reshape(t, ngrp, HEADS_PER_GROUP),
                               ((0, 0), (0, 0), (0, 128 - HEADS_PER_GROUP))).reshape(t, ngrp * 128)

    g_mix = vec(sm['norm_mix_g']) + comm.tok0
    a_n, a_t = _rms_fwd(x, g_mix, "rms_mix")
    wf = comm.weights('a', a_n)
    slabs = [wf['w_in'][k] for k in range(N_DEV)]
    w_main = jnp.concatenate(slabs[:-1] + [slabs[-1][:, :slabs[-1].shape[1] - h_n]], axis=1)
    w_dt = jnp.pad(slabs[-1][:, slabs[-1].shape[1] - h_n:], ((0, 0), (0, 128 - h_n)))
    proj = _mm_nn(a_n, w_main, out_dtype=ACT_DTYPE, name="mm_in")
    dtp = pad_dt(_mm_nn(a_n, w_dt, out_dtype=F32, name="mm_dt"))
    cat, cat_t = _gmlp_fwd(proj, vec(sm['ln_a_g']), vec(sm['ln_a_b']), sm['w_s'], bst, vec(sm['norm_a_g']), d_mix, "gmlp_fwd")
    xbc, cpre = _conv_ssm_fwd(proj, xcol0, wf['conv_ssm_w'], vec(sm['conv_ssm_b']), "conv_ssm_fwd")
    cat, cat_t, ypre, hs = _ssd_fwd(xbc, dtp, proj, zcol0, bias_p, alog_p, dskip_x, normg_x, cat, cat_t, "ssd_fwd")
    wf.update(comm.weights('b', hs))
    h1 = _mm_nn(cat, wf['w_out'], out_dtype=F32, name="mm_out", res=x)
    f_n, f_t = _rms_fwd(h1, vec(sm['norm_ffn_g']), "rms_ffn")
    hid = _mm_nn(f_n, wf['w_up'], out_dtype=ACT_DTYPE, name="mm_up")
    act, act_t, cv = _conv_ffn_fwd(hid, wf['conv_ffn_w'], vec(sm['conv_ffn_b']), "conv_ffn_fwd")
    h2 = _mm_nn(act, wf['w_down'], out_dtype=F32, name="mm_down", res=h1)
    r_n, r_t = _rms_fwd(h2, vec(sm['norm_ple_g']), "rms_ple")
    q = _mm_nn(r_n, wf['w_ple_gate'], out_dtype=ACT_DTYPE, name="mm_pg")
    p_m = p.astype(MXU_DTYPE)
    pe = _mm_nn(p_m, wf['w_ple'], out_dtype=ACT_DTYPE, name="mm_ple")

    loss, dh3, dq, dpe, dgf = _head(h2, q, pe, tgt, vec(sm['norm_final_g']), "head")
    wgrad = lambda act_t, g, name, **kw: _mm_nn(act_t, g, out_dtype=WIRE_DTYPE, name=name, wide=True, **kw)
    gs = {}
    gs['norm_final_g'] = _row0(dgf)
    g_ple = wgrad(p_m.T, dpe, "wg_ple")
    g_pg = wgrad(r_t, dq, "wg_pg")
    dr = _mm_nt(dq, wf['w_ple_gate'], out_dtype=ACT_DTYPE, name="dg_pg")
    dh2, dh2m, dg = _rms_bwd(h2, vec(sm['norm_ple_g']), dr, dh3, "rms_ple_bwd")
    gs['norm_ple_g'] = _row0(dg)
    g_down = wgrad(act_t, dh2m, "wg_down")
    tok = comm.send('1', {'w_ple': g_ple, 'w_ple_gate': g_pg, 'w_down': g_down})
    dact = _mm_nt(dh2m, wf['w_down'], out_dtype=ACT_DTYPE, name="dg_down")
    dhid, dcw, dcb = _conv_ffn_bwd(hid, wf['conv_ffn_w'] + tok, cv, dact, "conv_ffn_bwd")
    kf = wf['conv_ffn_w'].shape[0]
    g_cf = jnp.concatenate([dcw[0, :kf], dcw[1, :kf]], axis=1)
    gs['conv_ffn_b'] = jnp.concatenate([dcb[0, 0], dcb[1, 0]], axis=0)
    g_up = wgrad(f_t, dhid, "wg_up", b_split=2, out_slabs=N_DEV)
    df = _mm_nt(dhid, wf['w_up'], out_dtype=ACT_DTYPE, name="dg_up", a_split=2)
    dh1, dh1m, dg = _rms_bwd(h1, vec(sm['norm_ffn_g']), df, dh2, "rms_ffn_bwd")
    gs['norm_ffn_g'] = _row0(dg)
    g_out = wgrad(cat_t, dh1m, "wg_out")
    tok = comm.send('2', {'conv_ffn_w': g_cf, 'w_up': g_up, 'w_out': g_out}, stacked=('w_up',))
    dcat = _mm_nt(dh1m, wf['w_out'], out_dtype=ACT_DTYPE, name="dg_out")
    dproj, dlng, dlnb, dws, dbst, dng = _gmlp_bwd(proj, vec(sm['ln_a_g']) + tok, vec(sm['ln_a_b']), sm['w_s'], bst,
                                                  vec(sm['norm_a_g']), dcat, d_main, "gmlp_bwd")
    gs['ln_a_g'], gs['ln_a_b'], gs['w_s'], gs['norm_a_g'] = _row0(dlng), _row0(dlnb), dws, _row0(dng)
    gs['b_s'] = dbst[:, :sm['b_s'].shape[0]].T
    dproj, dxs, dbm, dcm, ddtp, dbias, dalog, ddsk, dsng = _ssd_bwd(
        xbc, dtp, proj, zcol0, bias_p, alog_p, dskip_x, normg_x, hs, ypre, dcat, dproj, "ssd_bwd")
    gs['dt_bias'] = dbias[:, 0, :HEADS_PER_GROUP].reshape(h_n)
    gs['a_log'] = dalog[:, 0, :HEADS_PER_GROUP].reshape(h_n)
    gs['d_skip'] = ddsk[:, 0, :].reshape(h_n, HEAD_DIM).sum(axis=-1)
    gs['ssm_norm_g'] = dsng[:, 0, :].reshape(d_ssm)
    dws_c, dbs_c = [], []
    off = 0
    for nm, dpart in (("x", dxs), ("b", dbm), ("c", dcm)):
        dproj, dw_c, db_c = _conv_ssm_bwd(proj, xcol0 + off, wf['conv_ssm_w'], cpre, off, dpart, dproj,
                                          xcol0 + off, "conv_ssm_bwd_" + nm)
        dws_c.append(dw_c[:wf['conv_ssm_w'].shape[0]])
        dbs_c.append(db_c[0])
        off += dpart.shape[1]
    g_cs = jnp.concatenate(dws_c, axis=1)
    gs['conv_ssm_b'] = jnp.concatenate(dbs_c, axis=0)
    ddt = jnp.pad(ddtp.reshape(t, ngrp, 128)[:, :, :HEADS_PER_GROUP].reshape(t, h_n), ((0, 0), (0, 128 - h_n))).astype(MXU_DTYPE)
    g_in = jnp.concatenate([wgrad(a_t, dproj, "wg_in"), wgrad(a_t, ddt, "wg_dt")[:, :h_n]], axis=1)
    tok = comm.send('3', {'conv_ssm_w': g_cs, 'w_in': g_in}, [(n, gs[n]) for n in REPLICATED if n != 'norm_mix_g'])
    da = _mm_nt(ddt + tok.astype(ddt.dtype), w_dt, out_dtype=F32, name="dg_dt")
    da = _mm_nt(dproj, w_main, out_dtype=ACT_DTYPE, name="dg_in", res=da)
    dx, dg = _rms_bwd(x, g_mix + tok, da, dh1, "rms_mix_bwd", mx_copy=False)
    comm.send('4', {}, [('norm_mix_g', _row0(dg)), ('loss', loss[0, 0:1])])
    return dx


def kernel(x, p, norm_mix_g, w_in, ln_a_g, ln_a_b, w_s, b_s, norm_a_g, conv_ssm_w, conv_ssm_b, dt_bias, a_log, d_skip, ssm_norm_g, w_out, norm_ffn_g, w_up, conv_ffn_w, conv_ffn_b, w_down, norm_ple_g, w_ple_gate, w_ple, norm_final_g, loss_target, m_norm_mix_g, m_w_in, m_ln_a_g, m_ln_a_b, m_w_s, m_b_s, m_norm_a_g, m_conv_ssm_w, m_conv_ssm_b, m_dt_bias, m_a_log, m_d_skip, m_ssm_norm_g, m_w_out, m_norm_ffn_g, m_w_up, m_conv_ffn_w, m_conv_ffn_b, m_w_down, m_norm_ple_g, m_w_ple_gate, m_w_ple, m_norm_final_g, v_norm_mix_g, v_w_in, v_ln_a_g, v_ln_a_b, v_w_s, v_b_s, v_norm_a_g, v_conv_ssm_w, v_conv_ssm_b, v_dt_bias, v_a_log, v_d_skip, v_ssm_norm_g, v_w_out, v_norm_ffn_g, v_w_up, v_conv_ffn_w, v_conv_ffn_b, v_w_down, v_norm_ple_g, v_w_ple_gate, v_w_ple, v_norm_final_g):
    given = dict(locals())
    wts = {n: given[n] for n in WEIGHTS}
    ms = {n: given["m_" + n] for n in WEIGHTS}
    vs = {n: given["v_" + n] for n in WEIGHTS}
    sm = {n: (wts[n][0] if wts[n].ndim > 1 else wts[n]) for n in REPLICATED}
    comm = _Comm({n: wts[n][0] for n in SHARDED})
    dx = _local_step(x[0], p[0, 0], loss_target[0], sm, comm)

    out, loss_out, after = {}, None, dx
    for tag, names, small_names, layout, handle in comm.sent:
        recv = _xfer_wait(handle, after, "grads_%s_wait" % tag)
        for n, parts in zip(names, recv):
            out[n] = _adamw(parts, wts[n][0], ms[n][0], vs[n][0], "adamw_" + n)
            after = out[n][1]
        if small_names:
            pick = lambda src, fill: _pack_small([src[n] if n in src else jnp.full((1,), fill, F32) for n in small_names])[0]
            res = _adamw(recv[-1], pick(wts, 0.0), pick(ms, 0.0), pick(vs, 1.0), "adamw_small_" + tag)
            res = [_unpack_small(o, layout) for o in res]
            after = res[1][0]
            for i, n in enumerate(small_names):
                if n == 'loss':
                    loss_out = res[0][i].reshape(())
                else:
                    out[n] = [res[k][i] for k in range(4)]
    return (loss_out, dx[None], *[out[n][k].reshape(wts[n].shape) for k in range(4) for n in WEIGHTS])


class _Comm:
    GATHER_GROUPS = {'a': ('w_in', 'conv_ssm_w'), 'b': ('w_out', 'w_up', 'conv_ffn_w', 'w_down', 'w_ple_gate', 'w_ple')}

    def __init__(self, blocks):
        wired = lambda grp: [blocks[n].astype(_wire(n)) for n in self.GATHER_GROUPS[grp]]
        self.stacks_a = _gather_two_level(wired('a'), "gather_a")
        self.handle_b, tok = _xfer_start([('gather', b) for b in wired('b')], "gather_b_start", after=self.stacks_a[0])
        self.tok0 = tok[0, 0]
        self.sent = []

    def weights(self, grp, after):
        stacks = self.stacks_a if grp == 'a' else _xfer_wait(self.handle_b, after, "gather_b_wait")
        return {n: st if n == 'w_in' else _stack_to_full(SHARDED[n], st) for n, st in zip(self.GATHER_GROUPS[grp], stacks)}

    def send(self, tag, gw, small=None, stacked=()):
        items = [('scatter', g if n in stacked else _full_to_stack(SHARDED[n], g.astype(_wire(n)))) for n, g in gw.items()]
        layout, small_names = None, []
        if small:
            packed, layout = _pack_small([a for _, a in small])
            small_names = [n for n, _ in small]
            items.append(('gather', packed))
        handle, tok = _xfer_start(items, "grads_%s_start" % tag)
        self.sent.append((tag, list(gw), small_names, layout, handle))
        return tok[0, 0]


def _wire(name):
    return F32 if name in F32_ON_WIRE else WIRE_DTYPE
```

```python
import jax
import jax.numpy as jnp
from jax import lax
from jax.experimental import pallas as pl
from jax.experimental.pallas import tpu as pltpu

F32 = jnp.float32
MXU_DTYPE = jnp.bfloat16
ACT_DTYPE = jnp.bfloat16
XBC_DTYPE = jnp.bfloat16
WIRE_DTYPE = jnp.bfloat16
EPS = 1e-6
CHUNK = 128
D_STATE = 128
HEAD_DIM = 64
HEADS_PER_GROUP = 4
GROUP_CH = HEAD_DIM * HEADS_PER_GROUP
HALO = 16
N_DEV = 8
VMEM_LIMIT = 56 * 1024 * 1024

ADAM_LR = 0.001
ADAM_B1 = 0.9
ADAM_B2 = 0.999
ADAM_EPS = 1e-08
ADAM_WD = 0.01
ADAM_STEP = 10

WEIGHTS = ['norm_mix_g', 'w_in', 'ln_a_g', 'ln_a_b', 'w_s', 'b_s', 'norm_a_g', 'conv_ssm_w', 'conv_ssm_b', 'dt_bias',
           'a_log', 'd_skip', 'ssm_norm_g', 'w_out', 'norm_ffn_g', 'w_up', 'conv_ffn_w', 'conv_ffn_b', 'w_down',
           'norm_ple_g', 'w_ple_gate', 'w_ple', 'norm_final_g']
SHARDED = {'w_in': 'col', 'conv_ssm_w': 'col', 'w_out': 'row', 'w_up': 'col', 'conv_ffn_w': 'col', 'w_down': 'row',
           'w_ple_gate': 'row', 'w_ple': 'col'}
F32_ON_WIRE = ('conv_ssm_w', 'conv_ffn_w')
REPLICATED = [n for n in WEIGHTS if n not in SHARDED]

S = jax.ShapeDtypeStruct


def _pick(dim, cands):
    for c in cands:
        if c <= dim and dim % c == 0:
            return c
    return dim


def _cp(sem, vmem=VMEM_LIMIT):
    return pltpu.CompilerParams(dimension_semantics=sem, vmem_limit_bytes=vmem)


def _mx(v):
    return v.astype(MXU_DTYPE)


def _rms(v, g):
    return v * lax.rsqrt(jnp.mean(v * v, axis=-1, keepdims=True) + EPS) * g


MM_VMEM_BUDGET = 42 * 1024 * 1024


def _mm_tiles(m, n, k, out_bytes, has_res, tn_cands=(512, 256, 128), k_mult=1, tm_cands=(1024, 512), tn_alts=2):
    tns = [c for c in tn_cands if c <= n and n % c == 0][:tn_alts] or [n]
    ks = k // k_mult
    best = None
    for tn in tns:
        for tm in [c for c in tm_cands if m % c == 0] or [_pick(m, (256, 128))]:
            for nk in range(1, ks // 128 + 1):
                if ks % nk or (ks // nk) % 128:
                    continue
                tk = ks // nk
                need = 2 * 2 * (tm * tk + tk * tn) + tm * tn * (4 + 2 * out_bytes + (8 if has_res else 0))
                if need <= MM_VMEM_BUDGET:
                    if best is None or (nk, -tm, -tn) < best[0]:
                        best = ((nk, -tm, -tn), (tm, tn, tk))
                    break
    return best[1] if best else (_pick(m, (512, 256, 128)), tns[0], _pick(ks, (128,)))


def _mm_body(dot, nk, has_res):
    def body(*refs):
        if has_res:
            a_ref, b_ref, r_ref, o_ref, acc_ref = refs
        else:
            a_ref, b_ref, o_ref, acc_ref = refs
            r_ref = None
        kk = pl.program_id(2)
        d = dot(a_ref[...], b_ref[...])

        def fin(acc):
            if r_ref is not None:
                acc = acc + r_ref[...]
            o_ref[...] = acc.astype(o_ref.dtype)

        if nk == 1:
            fin(d)
        else:
            @pl.when(kk == 0)
            def _():
                acc_ref[...] = d

            if nk > 2:
                @pl.when((kk > 0) & (kk < nk - 1))
                def _():
                    acc_ref[...] += d

            @pl.when(kk == nk - 1)
            def _():
                fin(acc_ref[...] + d)

    return body


def _mm_call(body, grid, a_spec, b_spec, tm, tn, m, n, out_dtype, name, args, res, out_slabs=1):
    in_specs = [a_spec, b_spec]
    if res is not None:
        in_specs.append(pl.BlockSpec((tm, tn), lambda i, j, kk: (i, j)))
        args = args + [res]
    if out_slabs == 1:
        out_spec, out_shape = pl.BlockSpec((tm, tn), lambda i, j, kk: (i, j)), S((m, n), out_dtype)
    else:
        out_spec, out_shape = pl.BlockSpec((None, tm, tn), lambda i, j, kk: (j, i, 0)), S((out_slabs, m, tn), out_dtype)
    return pl.pallas_call(
        body, grid=grid, in_specs=in_specs, out_specs=out_spec, out_shape=out_shape,
        scratch_shapes=[pltpu.VMEM((tm, tn), F32)], name=name,
        compiler_params=_cp(("parallel", "parallel", "arbitrary")))(*args)


def _mm_nn(a, b, *, out_dtype, name, res=None, b_split=1, wide=False, out_slabs=1):
    m, k = a.shape
    n = b.shape[1] if b_split == 1 else b.shape[2] * b_split
    tn_cands = (n // out_slabs,) if out_slabs > 1 else (1024, 512, 256, 128) if wide else (512, 256, 128)
    tm, tn, tk = _mm_tiles(m, n // b_split, k, jnp.dtype(out_dtype).itemsize, res is not None, tn_cands=tn_cands,
                           tn_alts=1 if wide or out_slabs > 1 else 2,
                           tm_cands=(1024, 512) if wide or out_slabs > 1 else (2048, 1024, 512))
    assert out_slabs == 1 or (tn * out_slabs == n and tn % 128 == 0)
    nk = k // tk
    njs = (n // b_split) // tn
    body = _mm_body(lambda x, y: jnp.dot(x, y, preferred_element_type=F32), nk, res is not None)
    if b_split == 1:
        b_spec = pl.BlockSpec((tk, tn), lambda i, j, kk: (kk, j))
    else:
        b_spec = pl.BlockSpec((None, tk, tn), lambda i, j, kk: (j // njs, kk, j % njs))
    return _mm_call(body, (m // tm, n // tn, nk), pl.BlockSpec((tm, tk), lambda i, j, kk: (i, kk)), b_spec,
                    tm, tn, m, n, out_dtype, name, [a, b], res, out_slabs)


def _mm_nt(a, b, *, out_dtype, name, res=None, a_split=1):
    if a_split == 1:
        m, k = a.shape
    else:
        m, k = a.shape[1], a.shape[2] * a_split
    n = b.shape[0]
    tm, tn, tk = _mm_tiles(m, n, k, jnp.dtype(out_dtype).itemsize, res is not None, k_mult=a_split,
                           tm_cands=(2048, 1024))
    nk = k // tk
    nks = nk // a_split
    body = _mm_body(lambda x, y: lax.dot_general(x, y, (((1,), (1,)), ((), ())), preferred_element_type=F32),
                    nk, res is not None)
    if a_split == 1:
        a_spec = pl.BlockSpec((tm, tk), lambda i, j, kk: (i, kk))
    else:
        a_spec = pl.BlockSpec((None, tm, tk), lambda i, j, kk: (kk // nks, i, kk % nks))
    return _mm_call(body, (m // tm, n // tn, nk), a_spec, pl.BlockSpec((tn, tk), lambda i, j, kk: (j, kk)),
                    tm, tn, m, n, out_dtype, name, [a, b], res)


def _rms_fwd(x, g, name):
    t, d = x.shape
    tr = _pick(t, (512, 256, 128))

    def body(x_ref, g_ref, o_ref, ot_ref):
        y = _rms(x_ref[...], g_ref[...]).astype(o_ref.dtype)
        o_ref[...] = y
        ot_ref[...] = y.T

    return pl.pallas_call(
        body, grid=(t // tr,),
        in_specs=[pl.BlockSpec((tr, d), lambda i: (i, 0)), pl.BlockSpec((1, d), lambda i: (0, 0))],
        out_specs=[pl.BlockSpec((tr, d), lambda i: (i, 0)), pl.BlockSpec((d, tr), lambda i: (0, i))],
        out_shape=[S((t, d), ACT_DTYPE), S((d, t), ACT_DTYPE)], name=name,
        compiler_params=_cp(("parallel",)))(x, g)


def _rms_bwd(xin, g, dn, dres, name, mx_copy=True):
    t, d = xin.shape
    tr = _pick(t, (256, 128))

    def body(x_ref, g_ref, dn_ref, dr_ref, dx_ref, *rest):
        dg_ref = rest[-1]

        @pl.when(pl.program_id(0) == 0)
        def _():
            dg_ref[...] = jnp.zeros_like(dg_ref)

        _, vjp = jax.vjp(_rms, x_ref[...], g_ref[...])
        dx, dg = vjp(dn_ref[...].astype(F32))
        dx = dr_ref[...] + dx
        dx_ref[...] = dx
        if mx_copy:
            rest[0][...] = dx.astype(rest[0].dtype)
        dg_ref[0:1, :] += dg

    row = pl.BlockSpec((tr, d), lambda i: (i, 0))
    acc = pl.BlockSpec((8, d), lambda i: (0, 0))
    return pl.pallas_call(
        body, grid=(t // tr,),
        in_specs=[row, pl.BlockSpec((1, d), lambda i: (0, 0)), row, row],
        out_specs=[row, row, acc] if mx_copy else [row, acc],
        out_shape=[S((t, d), F32)] + ([S((t, d), MXU_DTYPE)] if mx_copy else []) + [S((8, d), F32)], name=name,
        compiler_params=_cp(("arbitrary",)))(xin, g, dn, dres)


def _head(h2, q, pe, tgt, gf, name):
    t, d = h2.shape
    tr = _pick(t, (256, 128))

    def f(h2v, qv, pev, gfv, tv):
        h3 = h2v + jax.nn.sigmoid(qv) * pev
        y = _rms(h3, gfv)
        return 0.5 * jnp.sum(jnp.mean(jnp.square(y - tv), axis=-1))

    def body(h2_ref, q_ref, pe_ref, t_ref, g_ref, loss_ref, dh_ref, dq_ref, dpe_ref, dg_ref):
        @pl.when(pl.program_id(0) == 0)
        def _():
            loss_ref[...] = jnp.zeros_like(loss_ref)
            dg_ref[...] = jnp.zeros_like(dg_ref)

        tv = t_ref[...]
        loss, vjp = jax.vjp(lambda a, b, c, e: f(a, b, c, e, tv), h2_ref[...], q_ref[...].astype(F32),
                            pe_ref[...].astype(F32), g_ref[...])
        dh, dq, dpe, dg = vjp(jnp.ones((), F32))
        loss_ref[...] += jnp.full(loss_ref.shape, loss, F32)
        dh_ref[...] = dh
        dq_ref[...] = dq.astype(dq_ref.dtype)
        dpe_ref[...] = dpe.astype(dpe_ref.dtype)
        dg_ref[0:1, :] += dg

    row = pl.BlockSpec((tr, d), lambda i: (i, 0))
    return pl.pallas_call(
        body, grid=(t // tr,),
        in_specs=[row, row, row, row, pl.BlockSpec((1, d), lambda i: (0, 0))],
        out_specs=[pl.BlockSpec((8, 128), lambda i: (0, 0)), row, row, row, pl.BlockSpec((8, d), lambda i: (0, 0))],
        out_shape=[S((8, 128), F32), S((t, d), F32), S((t, d), MXU_DTYPE), S((t, d), MXU_DTYPE), S((8, d), F32)],
        name=name, compiler_params=_cp(("arbitrary",)))(h2, q, pe, tgt, gf)


def _gmlp_block(us, vs, lng, lnb, wss, bss, ng):
    n = us[0].shape[0]
    row = lax.broadcasted_iota(jnp.int32, (n, n), 0)
    col = lax.broadcasted_iota(jnp.int32, (n, n), 1)
    outs = []
    for u0, v0, lg, lb, ws, bs in zip(us, vs, lng, lnb, wss, bss):
        u = jax.nn.gelu(u0)
        v = jax.nn.gelu(v0)
        mu = jnp.mean(v, axis=-1, keepdims=True)
        var = jnp.mean(jnp.square(v - mu), axis=-1, keepdims=True)
        vn = (v - mu) * lax.rsqrt(var + EPS) * lg + lb
        w = jnp.where(row >= col, ws, 0.0)
        sg = jnp.dot(_mx(w), _mx(vn), preferred_element_type=F32) + bs
        outs.append(u * sg)
    return _rms(jnp.concatenate(outs, axis=1), ng)


def _gmlp_load(proj_ref, lng_ref, lnb_ref, ws_ref, bst_ref, d_a, ng):
    sl = lambda g: slice(CHUNK * g, CHUNK * (g + 1))
    us = [proj_ref[:, sl(g)].astype(F32) for g in range(ng)]
    vs = [proj_ref[:, d_a + CHUNK * g: d_a + CHUNK * (g + 1)].astype(F32) for g in range(ng)]
    lng = [lng_ref[:, sl(g)] for g in range(ng)]
    lnb = [lnb_ref[:, sl(g)] for g in range(ng)]
    wss = [ws_ref[g] for g in range(ng)]
    bss = [bst_ref[:, g:g + 1] for g in range(ng)]
    return us, vs, lng, lnb, wss, bss


def _gmlp_fwd(proj, ln_g, ln_b, w_s, bst, norm_g, d_mix, name):
    t = proj.shape[0]
    ng = w_s.shape[0]
    d_a = ng * CHUNK

    def body(proj_ref, lng_ref, lnb_ref, ws_ref, bst_ref, ng_ref, o_ref, ot_ref):
        args = _gmlp_load(proj_ref, lng_ref, lnb_ref, ws_ref, bst_ref, d_a, ng)
        y = _gmlp_block(*args, ng_ref[...]).astype(o_ref.dtype)
        o_ref[...] = y
        ot_ref[...] = y.T

    vec = pl.BlockSpec((1, d_a), lambda c: (0, 0))
    return pl.pallas_call(
        body, grid=(t // CHUNK,),
        in_specs=[pl.BlockSpec((CHUNK, 2 * d_a), lambda c: (c, 0)), vec, vec,
                  pl.BlockSpec((ng, CHUNK, CHUNK), lambda c: (0, 0, 0)), pl.BlockSpec((CHUNK, 128), lambda c: (0, 0)), vec],
        out_specs=[pl.BlockSpec((CHUNK, d_a), lambda c: (c, 0)), pl.BlockSpec((d_a, CHUNK), lambda c: (0, c))],
        out_shape=[S((t, d_mix), ACT_DTYPE), S((d_mix, t), ACT_DTYPE)],
        name=name, compiler_params=_cp(("parallel",)))(proj, ln_g, ln_b, w_s, bst, norm_g)


def _gmlp_bwd(proj, ln_g, ln_b, w_s, bst, norm_g, dcat, d_proj, name):
    t = proj.shape[0]
    ng = w_s.shape[0]
    d_a = ng * CHUNK

    def body(proj_ref, lng_ref, lnb_ref, ws_ref, bst_ref, ng_ref, dy_ref,
             dp_ref, dlng_ref, dlnb_ref, dws_ref, dbst_ref, dng_ref):
        @pl.when(pl.program_id(0) == 0)
        def _():
            for r in (dlng_ref, dlnb_ref, dws_ref, dbst_ref, dng_ref):
                r[...] = jnp.zeros_like(r)

        args = _gmlp_load(proj_ref, lng_ref, lnb_ref, ws_ref, bst_ref, d_a, ng)
        _, vjp = jax.vjp(_gmlp_block, *args, ng_ref[...])
        dus, dvs, dlng, dlnb, dwss, dbss, dng = vjp(dy_ref[...].astype(F32))
        lane = lax.broadcasted_iota(jnp.int32, (1, 128), 1)
        dbst = jnp.zeros((CHUNK, 128), F32)
        for g in range(ng):
            dp_ref[:, CHUNK * g:CHUNK * (g + 1)] = dus[g].astype(dp_ref.dtype)
            dp_ref[:, d_a + CHUNK * g:d_a + CHUNK * (g + 1)] = dvs[g].astype(dp_ref.dtype)
            dlng_ref[0:1, CHUNK * g:CHUNK * (g + 1)] += dlng[g]
            dlnb_ref[0:1, CHUNK * g:CHUNK * (g + 1)] += dlnb[g]
            dws_ref[g] += dwss[g]
            dbst = dbst + dbss[g] * (lane == g).astype(F32)
        dbst_ref[...] += dbst
        dng_ref[0:1, :] += dng

    vec = pl.BlockSpec((1, d_a), lambda c: (0, 0))
    acc = pl.BlockSpec((8, d_a), lambda c: (0, 0))
    wspec = pl.BlockSpec((ng, CHUNK, CHUNK), lambda c: (0, 0, 0))
    bspec = pl.BlockSpec((CHUNK, 128), lambda c: (0, 0))
    return pl.pallas_call(
        body, grid=(t // CHUNK,),
        in_specs=[pl.BlockSpec((CHUNK, 2 * d_a), lambda c: (c, 0)), vec, vec, wspec, bspec, vec,
                  pl.BlockSpec((CHUNK, d_a), lambda c: (c, 0))],
        out_specs=[pl.BlockSpec((CHUNK, 2 * d_a), lambda c: (c, 0)), acc, acc, wspec, bspec, acc],
        out_shape=[S((t, d_proj), ACT_DTYPE), S((8, d_a), F32), S((8, d_a), F32), S((ng, CHUNK, CHUNK), F32),
                   S((CHUNK, 128), F32), S((8, d_a), F32)],
        name=name, compiler_params=_cp(("arbitrary",)))(proj, ln_g, ln_b, w_s, bst, norm_g, dcat)


def _silu_grad(c):
    s = jax.nn.sigmoid(c)
    return s * (1.0 + c * (1.0 - s))


def _fill_prev_main(s_ref, prev_ref, main_ref, i, tt):
    s_ref[pl.ds(0, HALO), :] = jnp.where(i > 0, prev_ref[...].astype(F32), 0.0)
    s_ref[pl.ds(HALO, tt), :] = main_ref[...].astype(F32)


def _prev_spec(tt, tc, joff):
    return pl.BlockSpec((HALO, tc), lambda j, i: (jnp.maximum(i * (tt // HALO) - 1, 0), j + joff))


def _next_spec(tt, tc, joff, t):
    return pl.BlockSpec((HALO, tc), lambda j, i: (jnp.minimum((i + 1) * (tt // HALO), t // HALO - 1), j + joff))


CONV_RC = 32


def _conv_tiles(t, c):
    return _pick(t, (1024, 512, 256, 128)), _pick(c, (256, 128))


def _row_chunks(tt, fn, init=0):
    rc = min(CONV_RC, tt)
    return lax.fori_loop(0, tt // rc, lambda q, c: fn(pl.multiple_of(q * rc, rc), rc, c), init)


def _fold8(p):
    acc = p[0:8]
    for r in range(8, p.shape[0], 8):
        acc = acc + p[r:r + 8]
    return acc


def _taps_chunk(s_ref, w_ref, kw, r0, rc):
    xe = s_ref[pl.ds(HALO - 8 + r0, rc + 8), :]
    acc = w_ref[0:1, :] * xe[8 - (kw - 1):8 - (kw - 1) + rc]
    for k in range(1, kw):
        acc = acc + w_ref[k:k + 1, :] * xe[8 - (kw - 1) + k:8 - (kw - 1) + k + rc]
    return acc


def _conv_bwd_chunk(sd_ref, x, w_ref, kw, r0, rc, dws):
    de = sd_ref[pl.ds(r0, rc + 8), :]
    dx, out = None, list(dws)
    for j in range(kw):
        d = de[j:j + rc]
        k = kw - 1 - j
        term = w_ref[k:k + 1, :] * d
        dx = term if dx is None else dx + term
        out[k] = out[k] + _fold8(x * d)
    return dx, out


def _conv_ssm_fwd(proj, col0, w, b, name):
    t = proj.shape[0]
    kw, c = w.shape
    tt, tc = _conv_tiles(t, c)
    joff = col0 // tc
    assert col0 % tc == 0

    def body(x_ref, xp_ref, w_ref, b_ref, o_ref, c_ref, s_ref):
        _fill_prev_main(s_ref, xp_ref, x_ref, pl.program_id(1), tt)

        def chunk(r0, rc, carry):
            cpre = _taps_chunk(s_ref, w_ref, kw, r0, rc) + b_ref[...]
            o_ref[pl.ds(r0, rc), :] = jax.nn.silu(cpre).astype(o_ref.dtype)
            c_ref[pl.ds(r0, rc), :] = cpre.astype(c_ref.dtype)
            return carry

        _row_chunks(tt, chunk)

    out = pl.BlockSpec((tt, tc), lambda j, i: (i, j))
    return pl.pallas_call(
        body, grid=(c // tc, t // tt),
        in_specs=[pl.BlockSpec((tt, tc), lambda j, i: (i, j + joff)), _prev_spec(tt, tc, joff),
                  pl.BlockSpec((kw, tc), lambda j, i: (0, j)), pl.BlockSpec((1, tc), lambda j, i: (0, j))],
        out_specs=[out, out], out_shape=[S((t, c), XBC_DTYPE), S((t, c), ACT_DTYPE)],
        scratch_shapes=[pltpu.VMEM((HALO + tt, tc), F32)], name=name,
        compiler_params=_cp(("parallel", "arbitrary")))(proj, proj, w, b)


def _conv_ssm_bwd(proj, col0, w, cpre, wcol0, dact, dproj, out_col0, name):
    t = proj.shape[0]
    kw = w.shape[0]
    c = dact.shape[1]
    tt, tc = _conv_tiles(t, c)
    assert col0 % tc == 0 and wcol0 % tc == 0 and out_col0 % tc == 0
    joff, wj, oj = col0 // tc, wcol0 // tc, out_col0 // tc
    nt = t // tt

    def body(x_ref, w_ref, c_ref, cn_ref, d_ref, dn_ref, dp_in, dx_ref, dw_ref, db_ref, sd_ref):
        del dp_in
        i = pl.program_id(1)

        @pl.when(i == 0)
        def _():
            dw_ref[...] = jnp.zeros_like(dw_ref)
            db_ref[...] = jnp.zeros_like(db_ref)

        def stage(r0, rc, db):
            rows = pl.ds(r0, rc)
            d = d_ref[rows, :].astype(F32) * _silu_grad(c_ref[rows, :].astype(F32))
            sd_ref[rows, :] = d
            return db + _fold8(d)

        zero8 = jnp.zeros((8, tc), F32)
        db = _row_chunks(tt, stage, zero8)
        sd_ref[pl.ds(tt, HALO), :] = jnp.where(
            i < nt - 1, dn_ref[...].astype(F32) * _silu_grad(cn_ref[...].astype(F32)), 0.0)

        def chunk(r0, rc, dws):
            dx, dws = _conv_bwd_chunk(sd_ref, x_ref[pl.ds(r0, rc), :].astype(F32), w_ref, kw, r0, rc, dws)
            dx_ref[pl.ds(r0, rc), :] = dx.astype(dx_ref.dtype)
            return dws

        dws = _row_chunks(tt, chunk, [zero8] * kw)
        for k in range(kw):
            dw_ref[k:k + 1, :] += jnp.sum(dws[k], axis=0, keepdims=True)
        db_ref[0:1, :] += jnp.sum(db, axis=0, keepdims=True)

    acc = pl.BlockSpec((8, tc), lambda j, i: (0, j))
    return pl.pallas_call(
        body, grid=(c // tc, nt),
        in_specs=[pl.BlockSpec((tt, tc), lambda j, i: (i, j + joff)), pl.BlockSpec((kw, tc), lambda j, i: (0, j + wj)),
                  pl.BlockSpec((tt, tc), lambda j, i: (i, j + wj)), _next_spec(tt, tc, wj, t),
                  pl.BlockSpec((tt, tc), lambda j, i: (i, j)), _next_spec(tt, tc, 0, t),
                  pl.BlockSpec(memory_space=pl.ANY)],
        out_specs=[pl.BlockSpec((tt, tc), lambda j, i: (i, j + oj)), acc, acc],
        out_shape=[S(dproj.shape, dproj.dtype), S((8, c), F32), S((8, c), F32)],
        scratch_shapes=[pltpu.VMEM((tt + HALO, tc), F32)],
        input_output_aliases={6: 0}, name=name,
        compiler_params=_cp(("parallel", "arbitrary")))(proj, w, cpre, cpre, dact, dact, dproj)


def _conv_ffn_fwd(hid, w, b, name):
    t, f2 = hid.shape
    f = f2 // 2
    kw = w.shape[0]
    tt, tc = _conv_tiles(t, f)
    nj = f // tc

    def body(g_ref, gp_ref, u_ref, up_ref, wg_ref, wu_ref, bg_ref, bu_ref, o_ref, ot_ref, cv_ref, sg_ref, su_ref):
        i = pl.program_id(1)
        _fill_prev_main(sg_ref, gp_ref, g_ref, i, tt)
        _fill_prev_main(su_ref, up_ref, u_ref, i, tt)

        def chunk(r0, rc, carry):
            rows = pl.ds(r0, rc)
            gate = _taps_chunk(sg_ref, wg_ref, kw, r0, rc) + bg_ref[...]
            up = _taps_chunk(su_ref, wu_ref, kw, r0, rc) + bu_ref[...]
            o_ref[rows, :] = (jax.nn.silu(gate) * up).astype(o_ref.dtype)
            cv_ref[0, rows, :] = gate.astype(cv_ref.dtype)
            cv_ref[1, rows, :] = up.astype(cv_ref.dtype)
            return carry

        _row_chunks(tt, chunk)
        ot_ref[...] = o_ref[...].T

    return pl.pallas_call(
        body, grid=(nj, t // tt),
        in_specs=[pl.BlockSpec((tt, tc), lambda j, i: (i, j)), _prev_spec(tt, tc, 0),
                  pl.BlockSpec((tt, tc), lambda j, i: (i, j + nj)), _prev_spec(tt, tc, nj),
                  pl.BlockSpec((kw, tc), lambda j, i: (0, j)), pl.BlockSpec((kw, tc), lambda j, i: (0, j + nj)),
                  pl.BlockSpec((1, tc), lambda j, i: (0, j)), pl.BlockSpec((1, tc), lambda j, i: (0, j + nj))],
        out_specs=[pl.BlockSpec((tt, tc), lambda j, i: (i, j)), pl.BlockSpec((tc, tt), lambda j, i: (j, i)),
                   pl.BlockSpec((2, tt, tc), lambda j, i: (0, i, j))],
        out_shape=[S((t, f), ACT_DTYPE), S((f, t), ACT_DTYPE), S((2, t, f), ACT_DTYPE)],
        scratch_shapes=[pltpu.VMEM((HALO + tt, tc), F32), pltpu.VMEM((HALO + tt, tc), F32)], name=name,
        compiler_params=_cp(("parallel", "arbitrary")))(hid, hid, hid, hid, w, w, b, b)


def _conv_ffn_bwd(hid, w, cv, dact, name):
    t, f2 = hid.shape
    f = f2 // 2
    kw = w.shape[0]
    tt, tc = _conv_tiles(t, f)
    nj = f // tc
    nt = t // tt

    def body(g_ref, u_ref, wg_ref, wu_ref, cv_ref, cvn_ref, d_ref, dn_ref, dh_ref, dw_ref, db_ref, dg_ref, du_ref):
        i = pl.program_id(1)

        @pl.when(i == 0)
        def _():
            dw_ref[...] = jnp.zeros_like(dw_ref)
            db_ref[...] = jnp.zeros_like(db_ref)

        def cotangents(gate, up, dact_v):
            sg = jax.nn.sigmoid(gate)
            return dact_v * up * (sg * (1.0 + gate * (1.0 - sg))), dact_v * (gate * sg)

        def stage(r0, rc, dbs):
            rows = pl.ds(r0, rc)
            dg, du = cotangents(cv_ref[0, rows, :].astype(F32), cv_ref[1, rows, :].astype(F32), d_ref[rows, :].astype(F32))
            dg_ref[rows, :] = dg
            du_ref[rows, :] = du
            return [dbs[0] + _fold8(dg), dbs[1] + _fold8(du)]

        zero8 = jnp.zeros((8, tc), F32)
        dbs = _row_chunks(tt, stage, [zero8, zero8])
        dgn, dun = cotangents(cvn_ref[0].astype(F32), cvn_ref[1].astype(F32), dn_ref[...].astype(F32))
        dg_ref[pl.ds(tt, HALO), :] = jnp.where(i < nt - 1, dgn, 0.0)
        du_ref[pl.ds(tt, HALO), :] = jnp.where(i < nt - 1, dun, 0.0)
        for s, (sd_ref, x_ref, w_ref) in enumerate(((dg_ref, g_ref, wg_ref), (du_ref, u_ref, wu_ref))):
            def chunk(r0, rc, dws, s=s, sd_ref=sd_ref, x_ref=x_ref, w_ref=w_ref):
                dx, dws = _conv_bwd_chunk(sd_ref, x_ref[pl.ds(r0, rc), :].astype(F32), w_ref, kw, r0, rc, dws)
                dh_ref[s, pl.ds(r0, rc), :] = dx.astype(dh_ref.dtype)
                return dws

            dws = _row_chunks(tt, chunk, [zero8] * kw)
            for k in range(kw):
                dw_ref[s, k:k + 1, :] += jnp.sum(dws[k], axis=0, keepdims=True)
            db_ref[s, 0:1, :] += jnp.sum(dbs[s], axis=0, keepdims=True)

    acc = pl.BlockSpec((2, 8, tc), lambda j, i: (0, 0, j))
    dsc = pltpu.VMEM((tt + HALO, tc), F32)
    nxt = lambda j, i: (0, jnp.minimum((i + 1) * (tt // HALO), t // HALO - 1), j)
    return pl.pallas_call(
        body, grid=(nj, nt),
        in_specs=[pl.BlockSpec((tt, tc), lambda j, i: (i, j)), pl.BlockSpec((tt, tc), lambda j, i: (i, j + nj)),
                  pl.BlockSpec((kw, tc), lambda j, i: (0, j)), pl.BlockSpec((kw, tc), lambda j, i: (0, j + nj)),
                  pl.BlockSpec((2, tt, tc), lambda j, i: (0, i, j)), pl.BlockSpec((2, HALO, tc), nxt),
                  pl.BlockSpec((tt, tc), lambda j, i: (i, j)), _next_spec(tt, tc, 0, t)],
        out_specs=[pl.BlockSpec((2, tt, tc), lambda j, i: (0, i, j)), acc, acc],
        out_shape=[S((2, t, f), MXU_DTYPE), S((2, 8, f), F32), S((2, 8, f), F32)],
        scratch_shapes=[dsc, dsc], name=name,
        compiler_params=_cp(("parallel", "arbitrary")))(hid, hid, w, w, cv, cv, dact, dact)


SSD_SUB = 2


def _ssd_chunk(xs, bm, cm, dtraw, hin, bias, alog, dskip):
    n = bm.shape[0]
    row = lax.broadcasted_iota(jnp.int32, (n, n), 0)
    col = lax.broadcasted_iota(jnp.int32, (n, n), 1)
    causal = row >= col
    lane = lax.broadcasted_iota(jnp.int32, (1, 128), 1)
    sub = lax.broadcasted_iota(jnp.int32, (128, 1), 0)
    last = (lax.broadcasted_iota(jnp.int32, (n, 1), 0) == n - 1).astype(F32)
    dt = jax.nn.softplus(dtraw + bias)
    adt = dt * (-jnp.exp(alog))
    tri = causal.astype(F32)
    acs = jnp.dot(tri, adt, preferred_element_type=F32, precision=lax.Precision.HIGHEST)
    ch = lax.broadcasted_iota(jnp.int32, (128, GROUP_CH), 1)
    hd = lax.broadcasted_iota(jnp.int32, (128, GROUP_CH), 0) * HEAD_DIM
    expand = ((ch >= hd) & (ch < hd + HEAD_DIM)).astype(F32)
    acs_x = jnp.dot(acs, expand, preferred_element_type=F32, precision=lax.Precision.HIGHEST)
    alast_x = jnp.sum(acs_x * last, axis=0, keepdims=True)
    acs_t = acs.T
    scores = lax.dot_general(_mx(cm), _mx(bm), (((1,), (1,)), ((), ())), preferred_element_type=F32)
    yds, xts = [], []
    for r in range(HEADS_PER_GROUP):
        pick = (lane == r).astype(F32)
        acol = jnp.sum(acs * pick, axis=1, keepdims=True)
        arow = jnp.sum(acs_t * (sub == r).astype(F32), axis=0, keepdims=True)
        dtc = jnp.sum(dt * pick, axis=1, keepdims=True)
        lm = jnp.exp(jnp.where(causal, acol - arow, -1e30))
        xts.append(xs[r] * dtc)
        yds.append(jnp.dot(_mx(scores * lm), _mx(xts[r]), preferred_element_type=F32))
    xt = jnp.concatenate(xts, axis=1)
    yo = jnp.exp(acs_x) * jnp.dot(_mx(cm), _mx(hin), preferred_element_type=F32)
    st = lax.dot_general(_mx(bm), _mx(xt * jnp.exp(alast_x - acs_x)), (((0,), (0,)), ((), ())), preferred_element_type=F32)
    hout = jnp.exp(alast_x) * hin + st
    return jnp.concatenate(yds, axis=1) + yo + dskip * jnp.concatenate(xs, axis=1), hout


def _ssd_post(y, z, normg):
    return _rms(y * jax.nn.silu(z), normg)


def _ssd_block(datas, hin, consts):
    ys = []
    for data in datas:
        y, hin = _ssd_chunk(*data, hin, *consts)
        ys.append(y)
    return ys, hin


def _ssd_specs(d_ssm, ngrp, zcol0, rev, nb):
    cc = (lambda c: nb - 1 - c) if rev else (lambda c: c)
    rows = SSD_SUB * CHUNK
    xj, bj, cj, zj = 0, d_ssm // 128, d_ssm // 128 + ngrp, zcol0 // GROUP_CH
    const = lambda w: pl.BlockSpec((None, 8, w), lambda g, c: (g, 0, 0))
    return cc, [
        pl.BlockSpec((rows, GROUP_CH), lambda g, c: (cc(c), xj + g)),
        pl.BlockSpec((rows, 128), lambda g, c: (cc(c), bj + g)),
        pl.BlockSpec((rows, 128), lambda g, c: (cc(c), cj + g)),
        pl.BlockSpec((rows, 128), lambda g, c: (cc(c), g)),
        pl.BlockSpec((rows, GROUP_CH), lambda g, c: (cc(c), zj + g)),
        const(128), const(128), const(GROUP_CH), const(GROUP_CH)]


def _sub_rows(s):
    return slice(CHUNK * s, CHUNK * (s + 1))


def _ssd_load(x_ref, b_ref, c_ref, dt_ref, z_ref, bias_ref, alog_ref, dsk_ref, ng_ref):
    datas, zs = [], []
    for s in range(SSD_SUB):
        rows = _sub_rows(s)
        xs = [x_ref[rows, HEAD_DIM * r:HEAD_DIM * (r + 1)].astype(F32) for r in range(HEADS_PER_GROUP)]
        datas.append((xs, b_ref[rows, :].astype(F32), c_ref[rows, :].astype(F32), dt_ref[rows, :]))
        zs.append(z_ref[rows, :].astype(F32))
    return datas, zs, (bias_ref[0:1, :], alog_ref[0:1, :], dsk_ref[0:1, :]), ng_ref[0:1, :]


def _ssd_fwd(xbc, dtp, proj, zcol0, bias_p, alog_p, dskip_x, normg_x, cat, cat_t, name):
    t = xbc.shape[0]
    ngrp = bias_p.shape[0]
    d_ssm = ngrp * GROUP_CH
    rows = SSD_SUB * CHUNK
    nb = t // rows
    d_a = cat.shape[1] - d_ssm
    assert d_a % GROUP_CH == 0 and zcol0 % GROUP_CH == 0 and t % rows == 0
    _, specs = _ssd_specs(d_ssm, ngrp, zcol0, False, nb)

    def body(x_ref, b_ref, c_ref, dt_ref, z_ref, bias_ref, alog_ref, dsk_ref, ng_ref, cat_in, catt_in,
             yn_ref, ynt_ref, y_ref, hs_ref, h_ref):
        del cat_in, catt_in

        @pl.when(pl.program_id(1) == 0)
        def _():
            h_ref[...] = jnp.zeros_like(h_ref)

        datas, zs, consts, normg = _ssd_load(x_ref, b_ref, c_ref, dt_ref, z_ref, bias_ref, alog_ref, dsk_ref, ng_ref)
        hs_ref[...] = h_ref[...]
        ys, hout = _ssd_block(datas, h_ref[...], consts)
        for s in range(SSD_SUB):
            y_ref[_sub_rows(s), :] = ys[s].astype(y_ref.dtype)
            yn = _ssd_post(ys[s], zs[s], normg).astype(yn_ref.dtype)
            yn_ref[_sub_rows(s), :] = yn
            ynt_ref[:, _sub_rows(s)] = yn.T
        h_ref[...] = hout

    hbm = pl.BlockSpec(memory_space=pl.ANY)
    return pl.pallas_call(
        body, grid=(ngrp, nb), in_specs=specs + [hbm, hbm],
        out_specs=[pl.BlockSpec((rows, GROUP_CH), lambda g, c: (c, d_a // GROUP_CH + g)),
                   pl.BlockSpec((GROUP_CH, rows), lambda g, c: (d_a // GROUP_CH + g, c)),
                   pl.BlockSpec((rows, GROUP_CH), lambda g, c: (c, g)),
                   pl.BlockSpec((None, None, D_STATE, GROUP_CH), lambda g, c: (c, g, 0, 0))],
        out_shape=[S(cat.shape, cat.dtype), S(cat_t.shape, cat_t.dtype), S((t, d_ssm), ACT_DTYPE),
                   S((nb, ngrp, D_STATE, GROUP_CH), F32)],
        scratch_shapes=[pltpu.VMEM((D_STATE, GROUP_CH), F32)],
        input_output_aliases={9: 0, 10: 1}, name=name,
        compiler_params=_cp(("parallel", "arbitrary")))(xbc, xbc, xbc, dtp, proj, bias_p, alog_p, dskip_x, normg_x, cat, cat_t)


def _ssd_bwd(xbc, dtp, proj, zcol0, bias_p, alog_p, dskip_x, normg_x, hs, ypre, dcat, dproj, name):
    t = xbc.shape[0]
    ngrp = bias_p.shape[0]
    d_ssm = ngrp * GROUP_CH
    rows = SSD_SUB * CHUNK
    nb = t // rows
    d_a = dcat.shape[1] - d_ssm
    cc, specs = _ssd_specs(d_ssm, ngrp, zcol0, True, nb)

    def body(x_ref, b_ref, c_ref, dt_ref, z_ref, bias_ref, alog_ref, dsk_ref, ng_ref, hs_ref, yp_ref, dy_ref, dp_in,
             dz_ref, dx_ref, db_ref, dc_ref, ddt_ref, dbias_ref, dalog_ref, ddsk_ref, dng_ref, dh_ref):
        del dp_in

        @pl.when(pl.program_id(1) == 0)
        def _():
            dh_ref[...] = jnp.zeros_like(dh_ref)
            for r in (dbias_ref, dalog_ref, ddsk_ref, dng_ref):
                r[...] = jnp.zeros_like(r)

        datas, zs, consts, normg = _ssd_load(x_ref, b_ref, c_ref, dt_ref, z_ref, bias_ref, alog_ref, dsk_ref, ng_ref)
        dys, dng = [], jnp.zeros_like(normg)
        for s in range(SSD_SUB):
            rws = _sub_rows(s)
            _, vjp_post = jax.vjp(_ssd_post, yp_ref[rws, :].astype(F32), zs[s], normg)
            dy, dz, dg = vjp_post(dy_ref[rws, :].astype(F32))
            dys.append(dy)
            dng = dng + dg
            dz_ref[rws, :] = dz.astype(dz_ref.dtype)
        _, vjp = jax.vjp(_ssd_block, datas, hs_ref[...], consts)
        ddatas, dhin, (dbias, dalog, ddsk) = vjp((dys, dh_ref[...]))
        for s, (dxs, dbm, dcm, ddt) in enumerate(ddatas):
            rws = _sub_rows(s)
            for r in range(HEADS_PER_GROUP):
                dx_ref[rws, HEAD_DIM * r:HEAD_DIM * (r + 1)] = dxs[r].astype(dx_ref.dtype)
            db_ref[rws, :] = dbm.astype(db_ref.dtype)
            dc_ref[rws, :] = dcm.astype(dc_ref.dtype)
            ddt_ref[rws, :] = ddt
        dh_ref[...] = dhin
        dbias_ref[0:1, :] += dbias
        dalog_ref[0:1, :] += dalog
        ddsk_ref[0:1, :] += ddsk
        dng_ref[0:1, :] += dng

    acc = lambda w: pl.BlockSpec((None, 8, w), lambda g, c: (g, 0, 0))
    blk = lambda w: pl.BlockSpec((rows, w), lambda g, c: (cc(c), g))
    return pl.pallas_call(
        body, grid=(ngrp, nb),
        in_specs=specs + [pl.BlockSpec((None, None, D_STATE, GROUP_CH), lambda g, c: (cc(c), g, 0, 0)),
                          blk(GROUP_CH),
                          pl.BlockSpec((rows, GROUP_CH), lambda g, c: (cc(c), d_a // GROUP_CH + g)),
                          pl.BlockSpec(memory_space=pl.ANY)],
        out_specs=[pl.BlockSpec((rows, GROUP_CH), lambda g, c: (cc(c), zcol0 // GROUP_CH + g)),
                   blk(GROUP_CH), blk(128), blk(128), blk(128), acc(128), acc(128), acc(GROUP_CH), acc(GROUP_CH)],
        out_shape=[S(dproj.shape, dproj.dtype), S((t, d_ssm), XBC_DTYPE), S((t, ngrp * 128), XBC_DTYPE),
                   S((t, ngrp * 128), XBC_DTYPE), S((t, ngrp * 128), F32), S((ngrp, 8, 128), F32),
                   S((ngrp, 8, 128), F32), S((ngrp, 8, GROUP_CH), F32), S((ngrp, 8, GROUP_CH), F32)],
        scratch_shapes=[pltpu.VMEM((D_STATE, GROUP_CH), F32)],
        input_output_aliases={12: 0}, name=name,
        compiler_params=_cp(("parallel", "arbitrary")))(xbc, xbc, xbc, dtp, proj, bias_p, alog_p, dskip_x, normg_x, hs, ypre,
                                                        dcat, dproj)


def _adamw(parts, w, m, v, after, name):
    r, c = w.shape
    tr = _pick(r, (256, 128, 64, 32, 16, 8)) if c * 4 * 256 <= 4 * 1024 * 1024 else _pick(r, (64, 32, 16, 8))

    def body(p_ref, w_ref, m_ref, v_ref, after_ref, g_ref, d_ref, nm_ref, nv_ref):
        del after_ref
        g = p_ref[0].astype(F32)
        for k in range(1, N_DEV):
            g = g + p_ref[k].astype(F32)
        mm = ADAM_B1 * m_ref[...] + (1.0 - ADAM_B1) * g
        vv = ADAM_B2 * v_ref[...] + (1.0 - ADAM_B2) * jnp.square(g)
        m_hat = mm / (1.0 - ADAM_B1 ** ADAM_STEP)
        v_hat = vv / (1.0 - ADAM_B2 ** ADAM_STEP)
        g_ref[...] = g
        d_ref[...] = -ADAM_LR * (m_hat / (jnp.sqrt(v_hat) + ADAM_EPS) + ADAM_WD * w_ref[...])
        nm_ref[...] = mm
        nv_ref[...] = vv

    blk = pl.BlockSpec((tr, c), lambda i: (i, 0))
    return pl.pallas_call(
        body, grid=(r // tr,),
        in_specs=[pl.BlockSpec((N_DEV, tr, c), lambda i: (0, i, 0)), blk, blk, blk, pl.BlockSpec(memory_space=pl.ANY)],
        out_specs=[blk, blk, blk, blk], out_shape=[S((r, c), F32)] * 4, name=name,
        compiler_params=_cp(("parallel",)))(parts, w, m, v, after)


def _mesh_pos():
    return lax.axis_index("x"), lax.axis_index("y"), lax.axis_index("c")


def _peer(d, x, y, c):
    return (1 - x if (d >> 2) & 1 else x, 1 - y if (d >> 1) & 1 else y, 1 - c if d & 1 else c)


def _gather_two_level(blocks, name):
    n = len(blocks)

    def body(*refs):
        srcs, outs = refs[:n], refs[n:2 * n]
        send_sems, recv_sems, loc_sems = refs[2 * n:]
        x, y, c = _mesh_pos()
        lin = lambda px, py, pc: 4 * px + 2 * py + pc
        me, sibling = (x, y, c), (x, y, 1 - c)
        chips = [(1 - x, y), (x, 1 - y), (1 - x, 1 - y)]

        def copy(a, k, block, to, src=None):
            slab = outs[a].at[lin(*block)]
            return pltpu.make_async_remote_copy(
                src_ref=slab if src is None else src, dst_ref=slab, send_sem=send_sems.at[a, k],
                recv_sem=recv_sems.at[a, k], device_id=to, device_id_type=pl.DeviceIdType.MESH)

        mine = [pltpu.make_async_copy(srcs[a], outs[a].at[lin(*me)], loc_sems.at[a]) for a in range(n)]
        first = [copy(a, 0, me, sibling, src=srcs[a]) for a in range(n)]
        first += [copy(a, 1 + j, me, (*chip, c), src=srcs[a]) for j, chip in enumerate(chips) for a in range(n)]
        for cp in mine + first:
            cp.start()
        passed = []
        for j, chip in enumerate(chips):
            for a in range(n):
                copy(a, 1 + j, (*chip, c), me).wait_recv()
                passed.append(copy(a, 4 + j, (*chip, c), sibling))
                passed[-1].start()
        for a in range(n):
            copy(a, 0, sibling, me).wait_recv()
            for j, chip in enumerate(chips):
                copy(a, 4 + j, (*chip, 1 - c), me).wait_recv()
        for cp in first + passed:
            cp.wait_send()
        for cp in mine:
            cp.wait()

    hbm = pl.BlockSpec(memory_space=pl.ANY)
    return pl.pallas_call(
        body, in_specs=[hbm] * n, out_specs=[hbm] * n, out_shape=[S((N_DEV,) + b.shape, b.dtype) for b in blocks],
        scratch_shapes=[pltpu.SemaphoreType.DMA((n, N_DEV - 1)), pltpu.SemaphoreType.DMA((n, N_DEV - 1)),
                        pltpu.SemaphoreType.DMA((n,))],
        name=name, compiler_params=pltpu.CompilerParams(has_side_effects=True))(*blocks)


def _xfer_start(items, name, after=None):
    n = len(items)
    kinds = [k for k, _ in items]
    srcs = [pltpu.with_memory_space_constraint(a, pltpu.HBM) for _, a in items]
    land_shapes = [((N_DEV,) + a.shape if k == 'gather' else a.shape, a.dtype) for k, a in items]
    lands = [pltpu.with_memory_space_constraint(lax.empty(s, dt), pltpu.HBM) for s, dt in land_shapes]
    extra = [] if after is None else [after]

    def body(*refs):
        src_refs, land_refs = refs[:n], refs[n:2 * n]
        outs = refs[2 * n + len(extra):]
        sems = outs[:2 * n]
        token = outs[4 * n]
        x, y, c = _mesh_pos()
        me = 4 * x + 2 * y + c
        for a in range(n):
            for d in range(1, N_DEV):
                px, py, pc = _peer(d, x, y, c)
                src = src_refs[a] if kinds[a] == 'gather' else src_refs[a].at[4 * px + 2 * py + pc]
                pltpu.make_async_remote_copy(
                    src_ref=src, dst_ref=land_refs[a].at[me], send_sem=sems[2 * a].at[d - 1],
                    recv_sem=sems[2 * a + 1].at[d - 1], device_id=(px, py, pc),
                    device_id_type=pl.DeviceIdType.MESH).start()
        token[...] = jnp.zeros_like(token)

    hbm = pl.BlockSpec(memory_space=pltpu.HBM)
    sem = pl.BlockSpec(memory_space=pltpu.SEMAPHORE)
    out_shape = ([pltpu.SemaphoreType.DMA((N_DEV - 1,))] * (2 * n)
                 + [pltpu.HBM(a.shape, a.dtype) for a in srcs] + [pltpu.HBM(s, dt) for s, dt in land_shapes]
                 + [S((8, 128), F32)])
    res = pl.pallas_call(
        body, name=name, out_shape=out_shape,
        in_specs=[hbm] * (2 * n) + [pl.BlockSpec(memory_space=pl.ANY)] * len(extra),
        out_specs=[sem] * (2 * n) + [hbm] * (2 * n) + [pl.BlockSpec(memory_space=pltpu.VMEM)],
        input_output_aliases={**{a: 2 * n + a for a in range(n)}, **{n + a: 3 * n + a for a in range(n)}},
        compiler_params=pltpu.CompilerParams(has_side_effects=pltpu.SideEffectType.DATAFLOW_SIDE_EFFECTING),
    )(*srcs, *lands, *extra)
    return (kinds, res[:2 * n], res[2 * n:3 * n], res[3 * n:4 * n]), res[4 * n]


def _xfer_wait(handle, after, name):
    kinds, sems, src_thru, land_thru = handle
    n = len(kinds)

    def body(*refs):
        land_refs = refs[n:2 * n]
        sem_refs = refs[2 * n:4 * n]
        x, y, c = _mesh_pos()
        me = 4 * x + 2 * y + c
        for a in range(n):
            for d in range(1, N_DEV):
                slab = land_refs[a].at[me]
                cp = pltpu.make_async_remote_copy(
                    src_ref=slab, dst_ref=slab, send_sem=sem_refs[2 * a].at[d - 1], recv_sem=sem_refs[2 * a + 1].at[d - 1],
                    device_id=_peer(d, x, y, c), device_id_type=pl.DeviceIdType.MESH)
                cp.wait_send()
                cp.wait_recv()

    hbm = pl.BlockSpec(memory_space=pltpu.HBM)
    sem = pl.BlockSpec(memory_space=pltpu.SEMAPHORE)
    res = pl.pallas_call(
        body, name=name,
        out_shape=[pltpu.HBM(a.shape, a.dtype) for a in src_thru] + [pltpu.HBM(a.shape, a.dtype) for a in land_thru],
        in_specs=[hbm] * (2 * n) + [sem] * (2 * n) + [pl.BlockSpec(memory_space=pl.ANY)],
        out_specs=[hbm] * (2 * n), input_output_aliases={a: a for a in range(2 * n)},
        compiler_params=pltpu.CompilerParams(has_side_effects=pltpu.SideEffectType.DATAFLOW_SIDE_EFFECTING),
    )(*src_thru, *land_thru, *sems, after)
    x, y, c = _mesh_pos()
    me = 4 * x + 2 * y + c
    out = []
    for a in range(n):
        src = res[a]
        own = src[None] if kinds[a] == 'gather' else lax.dynamic_index_in_dim(src, me, 0, keepdims=True)
        out.append(lax.dynamic_update_index_in_dim(res[n + a], own, me, 0))
    return out


def _stack_to_full(kind, st):
    if kind == 'row':
        return st.reshape(st.shape[0] * st.shape[1], st.shape[2])
    return jnp.concatenate([st[k] for k in range(st.shape[0])], axis=1)


def _full_to_stack(kind, full):
    r, c = full.shape
    if kind == 'row':
        return full.reshape(N_DEV, r // N_DEV, c)
    w = c // N_DEV
    return jnp.stack([full[:, k * w:(k + 1) * w] for k in range(N_DEV)], axis=0)


SMALL_ROWS = 256


def _pack_small(named):
    layout = [(a.shape, a.size, -(-a.size // 1024) * 8) for a in named]
    total = -(-sum(nr for _, _, nr in layout) // SMALL_ROWS) * SMALL_ROWS * 128
    packed, off = None, 0
    for a, (_, n, nr) in zip(named, layout):
        part = jnp.pad(a.reshape(-1).astype(F32), (off, total - off - n))
        packed = part if packed is None else packed + part
        off += nr * 128
    return packed.reshape(total // 128, 128), layout


def _unpack_small(packed, layout):
    out, r0 = [], 0
    for shape, n, nr in layout:
        out.append(packed[r0:r0 + nr].reshape(-1)[:n].reshape(shape))
        r0 += nr
    return out


def _row0(acc):
    return acc[0]


def _local_step(x, p, tgt, sm, comm):
    t, d = x.shape
    h_n = sm['dt_bias'].shape[-1]
    ngrp = h_n // HEADS_PER_GROUP
    d_ssm = h_n * HEAD_DIM
    d_a = sm['ln_a_g'].shape[-1]
    d_mix = d_a + d_ssm
    d_xbc = sm['conv_ssm_b'].shape[-1]
    d_main = 2 * d_a + d_ssm + d_xbc
    assert d_xbc == d_ssm + 2 * ngrp * D_STATE and h_n <= 128
    zcol0, xcol0 = 2 * d_a, 2 * d_a + d_ssm
    vec = lambda v: v.reshape(1, -1)

    bst = jnp.pad(sm['b_s'].T, ((0, 0), (0, 128 - sm['b_s'].shape[0])))
    grp = lambda v, w: jnp.broadcast_to(jnp.pad(v.reshape(ngrp, 1, -1), ((0, 0), (0, 0), (0, w - v.size // ngrp))), (ngrp, 8, w))
    bias_p, alog_p = grp(sm['dt_bias'], 128), grp(sm['a_log'], 128)
    dskip_x = grp(jnp.repeat(sm['d_skip'], HEAD_DIM), GROUP_CH)
    normg_x = grp(sm['ssm_norm_g'], GROUP_CH)
    pad_dt = lambda v: jnp.pad(v[:, :h_n].reshape(t, ngrp, HEADS_PER_GROUP),
                               ((0, 0), (0, 0), (0, 128 - HEADS_PER_GROUP))).reshape(t, ngrp * 128)

    g_mix = vec(sm['norm_mix_g']) + comm.tok0
    a_n, a_t = _rms_fwd(x, g_mix, "rms_mix")
    wf = comm.weights('a', a_n)
    slabs = [wf['w_in'][k] for k in range(N_DEV)]
    w_main = jnp.concatenate(slabs[:-1] + [slabs[-1][:, :slabs[-1].shape[1] - h_n]], axis=1)
    w_dt = jnp.pad(slabs[-1][:, slabs[-1].shape[1] - h_n:], ((0, 0), (0, 128 - h_n)))
    proj = _mm_nn(a_n, w_main, out_dtype=ACT_DTYPE, name="mm_in")
    dtp = pad_dt(_mm_nn(a_n, w_dt, out_dtype=F32, name="mm_dt"))
    cat, cat_t = _gmlp_fwd(proj, vec(sm['ln_a_g']), vec(sm['ln_a_b']), sm['w_s'], bst, vec(sm['norm_a_g']), d_mix, "gmlp_fwd")
    xbc, cpre = _conv_ssm_fwd(proj, xcol0, wf['conv_ssm_w'], vec(sm['conv_ssm_b']), "conv_ssm_fwd")
    cat, cat_t, ypre, hs = _ssd_fwd(xbc, dtp, proj, zcol0, bias_p, alog_p, dskip_x, normg_x, cat, cat_t, "ssd_fwd")
    wf.update(comm.weights('b', hs))
    h1 = _mm_nn(cat, wf['w_out'], out_dtype=F32, name="mm_out", res=x)
    f_n, f_t = _rms_fwd(h1, vec(sm['norm_ffn_g']), "rms_ffn")
    hid = _mm_nn(f_n, wf['w_up'], out_dtype=ACT_DTYPE, name="mm_up")
    act, act_t, cv = _conv_ffn_fwd(hid, wf['conv_ffn_w'], vec(sm['conv_ffn_b']), "conv_ffn_fwd")
    h2 = _mm_nn(act, wf['w_down'], out_dtype=F32, name="mm_down", res=h1)
    r_n, r_t = _rms_fwd(h2, vec(sm['norm_ple_g']), "rms_ple")
    q = _mm_nn(r_n, wf['w_ple_gate'], out_dtype=ACT_DTYPE, name="mm_pg")
    p_m = p.astype(MXU_DTYPE)
    pe = _mm_nn(p_m, wf['w_ple'], out_dtype=ACT_DTYPE, name="mm_ple")

    loss, dh3, dq, dpe, dgf = _head(h2, q, pe, tgt, vec(sm['norm_final_g']), "head")
    wgrad = lambda act_t, g, name, **kw: _mm_nn(act_t, g, out_dtype=WIRE_DTYPE, name=name, wide=True, **kw)
    gs = {}
    gs['norm_final_g'] = _row0(dgf)
    g_ple = wgrad(p_m.T, dpe, "wg_ple")
    g_pg = wgrad(r_t, dq, "wg_pg")
    dr = _mm_nt(dq, wf['w_ple_gate'], out_dtype=ACT_DTYPE, name="dg_pg")
    dh2, dh2m, dg = _rms_bwd(h2, vec(sm['norm_ple_g']), dr, dh3, "rms_ple_bwd")
    gs['norm_ple_g'] = _row0(dg)
    g_down = wgrad(act_t, dh2m, "wg_down")
    tok = comm.send('1', {'w_ple': g_ple, 'w_ple_gate': g_pg, 'w_down': g_down})
    dact = _mm_nt(dh2m, wf['w_down'], out_dtype=ACT_DTYPE, name="dg_down")
    dhid, dcw, dcb = _conv_ffn_bwd(hid, wf['conv_ffn_w'] + tok, cv, dact, "conv_ffn_bwd")
    kf = wf['conv_ffn_w'].shape[0]
    g_cf = jnp.concatenate([dcw[0, :kf], dcw[1, :kf]], axis=1)
    gs['conv_ffn_b'] = jnp.concatenate([dcb[0, 0], dcb[1, 0]], axis=0)
    g_up = wgrad(f_t, dhid, "wg_up", b_split=2, out_slabs=N_DEV)
    df = _mm_nt(dhid, wf['w_up'], out_dtype=ACT_DTYPE, name="dg_up", a_split=2)
    dh1, dh1m, dg = _rms_bwd(h1, vec(sm['norm_ffn_g']), df, dh2, "rms_ffn_bwd")
    gs['norm_ffn_g'] = _row0(dg)
    g_out = wgrad(cat_t, dh1m, "wg_out")
    tok = comm.send('2', {'conv_ffn_w': g_cf, 'w_up': g_up, 'w_out': g_out}, stacked=('w_up',))
    dcat = _mm_nt(dh1m, wf['w_out'], out_dtype=ACT_DTYPE, name="dg_out")
    dproj, dlng, dlnb, dws, dbst, dng = _gmlp_bwd(proj, vec(sm['ln_a_g']) + tok, vec(sm['ln_a_b']), sm['w_s'], bst,
                                                  vec(sm['norm_a_g']), dcat, d_main, "gmlp_bwd")
    gs['ln_a_g'], gs['ln_a_b'], gs['w_s'], gs['norm_a_g'] = _row0(dlng), _row0(dlnb), dws, _row0(dng)
    gs['b_s'] = dbst[:, :sm['b_s'].shape[0]].T
    dproj, dxs, dbm, dcm, ddtp, dbias, dalog, ddsk, dsng = _ssd_bwd(
        xbc, dtp, proj, zcol0, bias_p, alog_p, dskip_x, normg_x, hs, ypre, dcat, dproj, "ssd_bwd")
    gs['dt_bias'] = dbias[:, 0, :HEADS_PER_GROUP].reshape(h_n)
    gs['a_log'] = dalog[:, 0, :HEADS_PER_GROUP].reshape(h_n)
    gs['d_skip'] = ddsk[:, 0, :].reshape(h_n, HEAD_DIM).sum(axis=-1)
    gs['ssm_norm_g'] = dsng[:, 0, :].reshape(d_ssm)
    dws_c, dbs_c = [], []
    off = 0
    for nm, dpart in (("x", dxs), ("b", dbm), ("c", dcm)):
        dproj, dw_c, db_c = _conv_ssm_bwd(proj, xcol0 + off, wf['conv_ssm_w'], cpre, off, dpart, dproj,
                                          xcol0 + off, "conv_ssm_bwd_" + nm)
        dws_c.append(dw_c[:wf['conv_ssm_w'].shape[0]])
        dbs_c.append(db_c[0])
        off += dpart.shape[1]
    g_cs = jnp.concatenate(dws_c, axis=1)
    gs['conv_ssm_b'] = jnp.concatenate(dbs_c, axis=0)
    ddt = jnp.pad(ddtp.reshape(t, ngrp, 128)[:, :, :HEADS_PER_GROUP].reshape(t, h_n), ((0, 0), (0, 128 - h_n))).astype(MXU_DTYPE)
    g_in = jnp.concatenate([wgrad(a_t, dproj, "wg_in"), wgrad(a_t, ddt, "wg_dt")[:, :h_n]], axis=1)
    tok = comm.send('3', {'conv_ssm_w': g_cs, 'w_in': g_in}, [(n, gs[n]) for n in REPLICATED if n != 'norm_mix_g'])
    da = _mm_nt(ddt + tok.astype(ddt.dtype), w_dt, out_dtype=F32, name="dg_dt")
    da = _mm_nt(dproj, w_main, out_dtype=ACT_DTYPE, name="dg_in", res=da)
    dx, dg = _rms_bwd(x, g_mix + tok, da, dh1, "rms_mix_bwd", mx_copy=False)
    return dx, comm.send('4', {}, [('norm_mix_g', _row0(dg)), ('loss', loss[0, 0:1])])


def kernel(x, p, norm_mix_g, w_in, ln_a_g, ln_a_b, w_s, b_s, norm_a_g, conv_ssm_w, conv_ssm_b, dt_bias, a_log, d_skip, ssm_norm_g, w_out, norm_ffn_g, w_up, conv_ffn_w, conv_ffn_b, w_down, norm_ple_g, w_ple_gate, w_ple, norm_final_g, loss_target, m_norm_mix_g, m_w_in, m_ln_a_g, m_ln_a_b, m_w_s, m_b_s, m_norm_a_g, m_conv_ssm_w, m_conv_ssm_b, m_dt_bias, m_a_log, m_d_skip, m_ssm_norm_g, m_w_out, m_norm_ffn_g, m_w_up, m_conv_ffn_w, m_conv_ffn_b, m_w_down, m_norm_ple_g, m_w_ple_gate, m_w_ple, m_norm_final_g, v_norm_mix_g, v_w_in, v_ln_a_g, v_ln_a_b, v_w_s, v_b_s, v_norm_a_g, v_conv_ssm_w, v_conv_ssm_b, v_dt_bias, v_a_log, v_d_skip, v_ssm_norm_g, v_w_out, v_norm_ffn_g, v_w_up, v_conv_ffn_w, v_conv_ffn_b, v_w_down, v_norm_ple_g, v_w_ple_gate, v_w_ple, v_norm_final_g):
    given = dict(locals())
    wts = {n: given[n] for n in WEIGHTS}
    ms = {n: given["m_" + n] for n in WEIGHTS}
    vs = {n: given["v_" + n] for n in WEIGHTS}
    sm = {n: (wts[n][0] if wts[n].ndim > 1 else wts[n]) for n in REPLICATED}
    comm = _Comm({n: wts[n][0] for n in SHARDED})
    dx, _ = _local_step(x[0], p[0, 0], loss_target[0], sm, comm)

    out, loss_out, after = {}, None, comm.last_token
    for tag, names, small_names, layout, handle in comm.sent:
        recv = _xfer_wait(handle, after, "grads_%s_wait" % tag)
        for n, parts in zip(names, recv):
            out[n] = _adamw(parts, wts[n][0], ms[n][0], vs[n][0], after, "adamw_" + n)
            after = out[n][1]
        if small_names:
            pick = lambda src, fill: _pack_small([src[n] if n in src else jnp.full((1,), fill, F32) for n in small_names])[0]
            res = _adamw(recv[-1], pick(wts, 0.0), pick(ms, 0.0), pick(vs, 1.0), after, "adamw_small_" + tag)
            after = res[1]
            res = [_unpack_small(o, layout) for o in res]
            for i, n in enumerate(small_names):
                if n == 'loss':
                    loss_out = res[0][i].reshape(())
                else:
                    out[n] = [res[k][i] for k in range(4)]
    return (loss_out, dx[None], *[out[n][k].reshape(wts[n].shape) for k in range(4) for n in WEIGHTS])


class _Comm:
    GATHER_GROUPS = {'a': ('w_in', 'conv_ssm_w'), 'b': ('w_out', 'w_up', 'conv_ffn_w', 'w_down', 'w_ple_gate', 'w_ple')}

    def __init__(self, blocks):
        wired = lambda grp: [blocks[n].astype(_wire(n)) for n in self.GATHER_GROUPS[grp]]
        self.stacks_a = _gather_two_level(wired('a'), "gather_a")
        self.handle_b, tok = _xfer_start([('gather', b) for b in wired('b')], "gather_b_start", after=self.stacks_a[0])
        self.tok0 = tok[0, 0]
        self.sent = []

    def weights(self, grp, after):
        stacks = self.stacks_a if grp == 'a' else _xfer_wait(self.handle_b, after, "gather_b_wait")
        return {n: st if n == 'w_in' else _stack_to_full(SHARDED[n], st) for n, st in zip(self.GATHER_GROUPS[grp], stacks)}

    def send(self, tag, gw, small=None, stacked=()):
        items = [('scatter', g if n in stacked else _full_to_stack(SHARDED[n], g.astype(_wire(n)))) for n, g in gw.items()]
        layout, small_names = None, []
        if small:
            packed, layout = _pack_small([a for _, a in small])
            small_names = [n for n, _ in small]
            items.append(('gather', packed))
        handle, self.last_token = _xfer_start(items, "grads_%s_start" % tag)
        self.sent.append((tag, list(gw), small_names, layout, handle))
        return self.last_token[0, 0]


def _wire(name):
    return F32 if name in F32_ON_WIRE else WIRE_DTYPE
```

```python
import jax
import jax.numpy as jnp
from jax import lax
from jax.experimental import pallas as pl
from jax.experimental.pallas import tpu as pltpu

F32 = jnp.float32
MXU_DTYPE = jnp.bfloat16
ACT_DTYPE = jnp.bfloat16
XBC_DTYPE = jnp.bfloat16
WIRE_DTYPE = jnp.bfloat16
DRES_DTYPE = jnp.bfloat16
EPS = 1e-6
CHUNK = 128
D_STATE = 128
HEAD_DIM = 64
HEADS_PER_GROUP = 4
GROUP_CH = HEAD_DIM * HEADS_PER_GROUP
HALO = 16
N_DEV = 8
VMEM_LIMIT = 56 * 1024 * 1024

ADAM_LR = 0.001
ADAM_B1 = 0.9
ADAM_B2 = 0.999
ADAM_EPS = 1e-08
ADAM_WD = 0.01
ADAM_STEP = 10

WEIGHTS = ['norm_mix_g', 'w_in', 'ln_a_g', 'ln_a_b', 'w_s', 'b_s', 'norm_a_g', 'conv_ssm_w', 'conv_ssm_b', 'dt_bias',
           'a_log', 'd_skip', 'ssm_norm_g', 'w_out', 'norm_ffn_g', 'w_up', 'conv_ffn_w', 'conv_ffn_b', 'w_down',
           'norm_ple_g', 'w_ple_gate', 'w_ple', 'norm_final_g']
SHARDED = {'w_in': 'col', 'conv_ssm_w': 'col', 'w_out': 'row', 'w_up': 'col', 'conv_ffn_w': 'col', 'w_down': 'row',
           'w_ple_gate': 'row', 'w_ple': 'col'}
F32_ON_WIRE = ('conv_ssm_w', 'conv_ffn_w')
REPLICATED = [n for n in WEIGHTS if n not in SHARDED]

S = jax.ShapeDtypeStruct


def _pick(dim, cands):
    for c in cands:
        if c <= dim and dim % c == 0:
            return c
    return dim


def _cp(sem, vmem=VMEM_LIMIT):
    return pltpu.CompilerParams(dimension_semantics=sem, vmem_limit_bytes=vmem)


def _mx(v):
    return v.astype(MXU_DTYPE)


def _rms(v, g):
    return v * lax.rsqrt(jnp.mean(v * v, axis=-1, keepdims=True) + EPS) * g


MM_VMEM_BUDGET = 42 * 1024 * 1024


def _mm_tiles(m, n, k, out_bytes, has_res, tn_cands=(512, 256, 128), k_mult=1, tm_cands=(1024, 512), tn_alts=2):
    tns = [c for c in tn_cands if c <= n and n % c == 0][:tn_alts] or [n]
    ks = k // k_mult
    best = None
    for tn in tns:
        for tm in [c for c in tm_cands if m % c == 0] or [_pick(m, (256, 128))]:
            for nk in range(1, ks // 128 + 1):
                if ks % nk or (ks // nk) % 128:
                    continue
                tk = ks // nk
                need = 2 * 2 * (tm * tk + tk * tn) + tm * tn * (4 + 2 * out_bytes + (8 if has_res else 0))
                if need <= MM_VMEM_BUDGET:
                    if best is None or (nk, -tm, -tn) < best[0]:
                        best = ((nk, -tm, -tn), (tm, tn, tk))
                    break
    return best[1] if best else (_pick(m, (512, 256, 128)), tns[0], _pick(ks, (128,)))


def _mm_body(dot, nk, has_res):
    def body(*refs):
        if has_res:
            a_ref, b_ref, r_ref, o_ref, acc_ref = refs
        else:
            a_ref, b_ref, o_ref, acc_ref = refs
            r_ref = None
        kk = pl.program_id(2)
        d = dot(a_ref[...], b_ref[...])

        def fin(acc):
            if r_ref is not None:
                acc = acc + r_ref[...]
            o_ref[...] = acc.astype(o_ref.dtype)

        if nk == 1:
            fin(d)
        else:
            @pl.when(kk == 0)
            def _():
                acc_ref[...] = d

            if nk > 2:
                @pl.when((kk > 0) & (kk < nk - 1))
                def _():
                    acc_ref[...] += d

            @pl.when(kk == nk - 1)
            def _():
                fin(acc_ref[...] + d)

    return body


def _mm_call(body, grid, a_spec, b_spec, tm, tn, m, n, out_dtype, name, args, res, out_slabs=1):
    in_specs = [a_spec, b_spec]
    if res is not None:
        in_specs.append(pl.BlockSpec((tm, tn), lambda i, j, kk: (i, j)))
        args = args + [res]
    if out_slabs == 1:
        out_spec, out_shape = pl.BlockSpec((tm, tn), lambda i, j, kk: (i, j)), S((m, n), out_dtype)
    else:
        out_spec, out_shape = pl.BlockSpec((None, tm, tn), lambda i, j, kk: (j, i, 0)), S((out_slabs, m, tn), out_dtype)
    return pl.pallas_call(
        body, grid=grid, in_specs=in_specs, out_specs=out_spec, out_shape=out_shape,
        scratch_shapes=[pltpu.VMEM((tm, tn), F32)], name=name,
        compiler_params=_cp(("parallel", "parallel", "arbitrary")))(*args)


def _mm_nn(a, b, *, out_dtype, name, res=None, b_split=1, wide=False, out_slabs=1):
    m, k = a.shape
    n = b.shape[1] if b_split == 1 else b.shape[2] * b_split
    tn_cands = (n // out_slabs,) if out_slabs > 1 else (1024, 512, 256, 128) if wide else (512, 256, 128)
    tm, tn, tk = _mm_tiles(m, n // b_split, k, jnp.dtype(out_dtype).itemsize, res is not None, tn_cands=tn_cands,
                           tn_alts=1 if wide or out_slabs > 1 else 2,
                           tm_cands=(1024, 512) if wide or out_slabs > 1 else (2048, 1024, 512))
    assert out_slabs == 1 or (tn * out_slabs == n and tn % 128 == 0)
    nk = k // tk
    njs = (n // b_split) // tn
    body = _mm_body(lambda x, y: jnp.dot(x, y, preferred_element_type=F32), nk, res is not None)
    if b_split == 1:
        b_spec = pl.BlockSpec((tk, tn), lambda i, j, kk: (kk, j))
    else:
        b_spec = pl.BlockSpec((None, tk, tn), lambda i, j, kk: (j // njs, kk, j % njs))
    return _mm_call(body, (m // tm, n // tn, nk), pl.BlockSpec((tm, tk), lambda i, j, kk: (i, kk)), b_spec,
                    tm, tn, m, n, out_dtype, name, [a, b], res, out_slabs)


def _mm_nt(a, b, *, out_dtype, name, res=None, a_split=1):
    if a_split == 1:
        m, k = a.shape
    else:
        m, k = a.shape[1], a.shape[2] * a_split
    n = b.shape[0]
    tm, tn, tk = _mm_tiles(m, n, k, jnp.dtype(out_dtype).itemsize, res is not None, k_mult=a_split,
                           tm_cands=(2048, 1024))
    nk = k // tk
    nks = nk // a_split
    body = _mm_body(lambda x, y: lax.dot_general(x, y, (((1,), (1,)), ((), ())), preferred_element_type=F32),
                    nk, res is not None)
    if a_split == 1:
        a_spec = pl.BlockSpec((tm, tk), lambda i, j, kk: (i, kk))
    else:
        a_spec = pl.BlockSpec((None, tm, tk), lambda i, j, kk: (kk // nks, i, kk % nks))
    return _mm_call(body, (m // tm, n // tn, nk), a_spec, pl.BlockSpec((tn, tk), lambda i, j, kk: (j, kk)),
                    tm, tn, m, n, out_dtype, name, [a, b], res)


def _rms_fwd(x, g, name):
    t, d = x.shape
    tr = _pick(t, (512, 256, 128))

    def body(x_ref, g_ref, o_ref, ot_ref):
        y = _rms(x_ref[...], g_ref[...]).astype(o_ref.dtype)
        o_ref[...] = y
        ot_ref[...] = y.T

    return pl.pallas_call(
        body, grid=(t // tr,),
        in_specs=[pl.BlockSpec((tr, d), lambda i: (i, 0)), pl.BlockSpec((1, d), lambda i: (0, 0))],
        out_specs=[pl.BlockSpec((tr, d), lambda i: (i, 0)), pl.BlockSpec((d, tr), lambda i: (0, i))],
        out_shape=[S((t, d), ACT_DTYPE), S((d, t), ACT_DTYPE)], name=name,
        compiler_params=_cp(("parallel",)))(x, g)


def _rms_bwd(xin, g, dn, dres, name, out_dtype):
    t, d = xin.shape
    tr = _pick(t, (256, 128))

    def body(x_ref, g_ref, dn_ref, dr_ref, dx_ref, dg_ref):
        @pl.when(pl.program_id(0) == 0)
        def _():
            dg_ref[...] = jnp.zeros_like(dg_ref)

        _, vjp = jax.vjp(_rms, x_ref[...], g_ref[...])
        dx, dg = vjp(dn_ref[...].astype(F32))
        dx_ref[...] = (dr_ref[...].astype(F32) + dx).astype(dx_ref.dtype)
        dg_ref[0:1, :] += dg

    row = pl.BlockSpec((tr, d), lambda i: (i, 0))
    return pl.pallas_call(
        body, grid=(t // tr,),
        in_specs=[row, pl.BlockSpec((1, d), lambda i: (0, 0)), row, row],
        out_specs=[row, pl.BlockSpec((8, d), lambda i: (0, 0))],
        out_shape=[S((t, d), out_dtype), S((8, d), F32)], name=name,
        compiler_params=_cp(("arbitrary",)))(xin, g, dn, dres)


def _head(h2, q, pe, tgt, gf, name):
    t, d = h2.shape
    tr = _pick(t, (256, 128))

    def f(h2v, qv, pev, gfv, tv):
        h3 = h2v + jax.nn.sigmoid(qv) * pev
        y = _rms(h3, gfv)
        return 0.5 * jnp.sum(jnp.mean(jnp.square(y - tv), axis=-1))

    def body(h2_ref, q_ref, pe_ref, t_ref, g_ref, loss_ref, dh_ref, dq_ref, dpe_ref, dg_ref):
        @pl.when(pl.program_id(0) == 0)
        def _():
            loss_ref[...] = jnp.zeros_like(loss_ref)
            dg_ref[...] = jnp.zeros_like(dg_ref)

        tv = t_ref[...]
        loss, vjp = jax.vjp(lambda a, b, c, e: f(a, b, c, e, tv), h2_ref[...], q_ref[...].astype(F32),
                            pe_ref[...].astype(F32), g_ref[...])
        dh, dq, dpe, dg = vjp(jnp.ones((), F32))
        loss_ref[...] += jnp.full(loss_ref.shape, loss, F32)
        dh_ref[...] = dh.astype(dh_ref.dtype)
        dq_ref[...] = dq.astype(dq_ref.dtype)
        dpe_ref[...] = dpe.astype(dpe_ref.dtype)
        dg_ref[0:1, :] += dg

    row = pl.BlockSpec((tr, d), lambda i: (i, 0))
    return pl.pallas_call(
        body, grid=(t // tr,),
        in_specs=[row, row, row, row, pl.BlockSpec((1, d), lambda i: (0, 0))],
        out_specs=[pl.BlockSpec((8, 128), lambda i: (0, 0)), row, row, row, pl.BlockSpec((8, d), lambda i: (0, 0))],
        out_shape=[S((8, 128), F32), S((t, d), DRES_DTYPE), S((t, d), MXU_DTYPE), S((t, d), MXU_DTYPE), S((8, d), F32)],
        name=name, compiler_params=_cp(("arbitrary",)))(h2, q, pe, tgt, gf)


def _gmlp_block(us, vs, lng, lnb, wss, bss, ng):
    n = us[0].shape[0]
    row = lax.broadcasted_iota(jnp.int32, (n, n), 0)
    col = lax.broadcasted_iota(jnp.int32, (n, n), 1)
    outs = []
    for u0, v0, lg, lb, ws, bs in zip(us, vs, lng, lnb, wss, bss):
        u = jax.nn.gelu(u0)
        v = jax.nn.gelu(v0)
        mu = jnp.mean(v, axis=-1, keepdims=True)
        var = jnp.mean(jnp.square(v - mu), axis=-1, keepdims=True)
        vn = (v - mu) * lax.rsqrt(var + EPS) * lg + lb
        w = jnp.where(row >= col, ws, 0.0)
        sg = jnp.dot(_mx(w), _mx(vn), preferred_element_type=F32) + bs
        outs.append(u * sg)
    return _rms(jnp.concatenate(outs, axis=1), ng)


def _gmlp_load(proj_ref, lng_ref, lnb_ref, ws_ref, bst_ref, d_a, ng):
    sl = lambda g: slice(CHUNK * g, CHUNK * (g + 1))
    us = [proj_ref[:, sl(g)].astype(F32) for g in range(ng)]
    vs = [proj_ref[:, d_a + CHUNK * g: d_a + CHUNK * (g + 1)].astype(F32) for g in range(ng)]
    lng = [lng_ref[:, sl(g)] for g in range(ng)]
    lnb = [lnb_ref[:, sl(g)] for g in range(ng)]
    wss = [ws_ref[g] for g in range(ng)]
    bss = [bst_ref[:, g:g + 1] for g in range(ng)]
    return us, vs, lng, lnb, wss, bss


def _gmlp_fwd(proj, ln_g, ln_b, w_s, bst, norm_g, d_mix, name):
    t = proj.shape[0]
    ng = w_s.shape[0]
    d_a = ng * CHUNK

    def body(proj_ref, lng_ref, lnb_ref, ws_ref, bst_ref, ng_ref, o_ref, ot_ref):
        args = _gmlp_load(proj_ref, lng_ref, lnb_ref, ws_ref, bst_ref, d_a, ng)
        y = _gmlp_block(*args, ng_ref[...]).astype(o_ref.dtype)
        o_ref[...] = y
        ot_ref[...] = y.T

    vec = pl.BlockSpec((1, d_a), lambda c: (0, 0))
    return pl.pallas_call(
        body, grid=(t // CHUNK,),
        in_specs=[pl.BlockSpec((CHUNK, 2 * d_a), lambda c: (c, 0)), vec, vec,
                  pl.BlockSpec((ng, CHUNK, CHUNK), lambda c: (0, 0, 0)), pl.BlockSpec((CHUNK, 128), lambda c: (0, 0)), vec],
        out_specs=[pl.BlockSpec((CHUNK, d_a), lambda c: (c, 0)), pl.BlockSpec((d_a, CHUNK), lambda c: (0, c))],
        out_shape=[S((t, d_mix), ACT_DTYPE), S((d_mix, t), ACT_DTYPE)],
        name=name, compiler_params=_cp(("parallel",)))(proj, ln_g, ln_b, w_s, bst, norm_g)


def _gmlp_bwd(proj, ln_g, ln_b, w_s, bst, norm_g, dcat, d_proj, name):
    t = proj.shape[0]
    ng = w_s.shape[0]
    d_a = ng * CHUNK

    def body(proj_ref, lng_ref, lnb_ref, ws_ref, bst_ref, ng_ref, dy_ref,
             dp_ref, dlng_ref, dlnb_ref, dws_ref, dbst_ref, dng_ref):
        @pl.when(pl.program_id(0) == 0)
        def _():
            for r in (dlng_ref, dlnb_ref, dws_ref, dbst_ref, dng_ref):
                r[...] = jnp.zeros_like(r)

        args = _gmlp_load(proj_ref, lng_ref, lnb_ref, ws_ref, bst_ref, d_a, ng)
        _, vjp = jax.vjp(_gmlp_block, *args, ng_ref[...])
        dus, dvs, dlng, dlnb, dwss, dbss, dng = vjp(dy_ref[...].astype(F32))
        lane = lax.broadcasted_iota(jnp.int32, (1, 128), 1)
        dbst = jnp.zeros((CHUNK, 128), F32)
        for g in range(ng):
            dp_ref[:, CHUNK * g:CHUNK * (g + 1)] = dus[g].astype(dp_ref.dtype)
            dp_ref[:, d_a + CHUNK * g:d_a + CHUNK * (g + 1)] = dvs[g].astype(dp_ref.dtype)
            dlng_ref[0:1, CHUNK * g:CHUNK * (g + 1)] += dlng[g]
            dlnb_ref[0:1, CHUNK * g:CHUNK * (g + 1)] += dlnb[g]
            dws_ref[g] += dwss[g]
            dbst = dbst + dbss[g] * (lane == g).astype(F32)
        dbst_ref[...] += dbst
        dng_ref[0:1, :] += dng

    vec = pl.BlockSpec((1, d_a), lambda c: (0, 0))
    acc = pl.BlockSpec((8, d_a), lambda c: (0, 0))
    wspec = pl.BlockSpec((ng, CHUNK, CHUNK), lambda c: (0, 0, 0))
    bspec = pl.BlockSpec((CHUNK, 128), lambda c: (0, 0))
    return pl.pallas_call(
        body, grid=(t // CHUNK,),
        in_specs=[pl.BlockSpec((CHUNK, 2 * d_a), lambda c: (c, 0)), vec, vec, wspec, bspec, vec,
                  pl.BlockSpec((CHUNK, d_a), lambda c: (c, 0))],
        out_specs=[pl.BlockSpec((CHUNK, 2 * d_a), lambda c: (c, 0)), acc, acc, wspec, bspec, acc],
        out_shape=[S((t, d_proj), ACT_DTYPE), S((8, d_a), F32), S((8, d_a), F32), S((ng, CHUNK, CHUNK), F32),
                   S((CHUNK, 128), F32), S((8, d_a), F32)],
        name=name, compiler_params=_cp(("arbitrary",)))(proj, ln_g, ln_b, w_s, bst, norm_g, dcat)


def _silu_grad(c):
    s = jax.nn.sigmoid(c)
    return s * (1.0 + c * (1.0 - s))


def _fill_prev_main(s_ref, prev_ref, main_ref, i, tt):
    s_ref[pl.ds(0, HALO), :] = jnp.where(i > 0, prev_ref[...].astype(F32), 0.0)
    s_ref[pl.ds(HALO, tt), :] = main_ref[...].astype(F32)


def _prev_spec(tt, tc, joff):
    return pl.BlockSpec((HALO, tc), lambda j, i: (jnp.maximum(i * (tt // HALO) - 1, 0), j + joff))


def _next_spec(tt, tc, joff, t):
    return pl.BlockSpec((HALO, tc), lambda j, i: (jnp.minimum((i + 1) * (tt // HALO), t // HALO - 1), j + joff))


CONV_RC = 32


def _conv_tiles(t, c):
    return _pick(t, (1024, 512, 256, 128)), _pick(c, (256, 128))


def _row_chunks(tt, fn, init=0):
    rc = min(CONV_RC, tt)
    return lax.fori_loop(0, tt // rc, lambda q, c: fn(pl.multiple_of(q * rc, rc), rc, c), init)


def _fold8(p):
    acc = p[0:8]
    for r in range(8, p.shape[0], 8):
        acc = acc + p[r:r + 8]
    return acc


def _taps_chunk(s_ref, w_ref, kw, r0, rc):
    xe = s_ref[pl.ds(HALO - 8 + r0, rc + 8), :]
    acc = w_ref[0:1, :] * xe[8 - (kw - 1):8 - (kw - 1) + rc]
    for k in range(1, kw):
        acc = acc + w_ref[k:k + 1, :] * xe[8 - (kw - 1) + k:8 - (kw - 1) + k + rc]
    return acc


def _conv_bwd_chunk(sd_ref, x, w_ref, kw, r0, rc, dws):
    de = sd_ref[pl.ds(r0, rc + 8), :]
    dx, out = None, list(dws)
    for j in range(kw):
        d = de[j:j + rc]
        k = kw - 1 - j
        term = w_ref[k:k + 1, :] * d
        dx = term if dx is None else dx + term
        out[k] = out[k] + _fold8(x * d)
    return dx, out


def _conv_ssm_fwd(proj, col0, w, b, name):
    t = proj.shape[0]
    kw, c = w.shape
    tt, tc = _conv_tiles(t, c)
    joff = col0 // tc
    assert col0 % tc == 0

    def body(x_ref, xp_ref, w_ref, b_ref, o_ref, c_ref, s_ref):
        _fill_prev_main(s_ref, xp_ref, x_ref, pl.program_id(1), tt)

        def chunk(r0, rc, carry):
            cpre = _taps_chunk(s_ref, w_ref, kw, r0, rc) + b_ref[...]
            o_ref[pl.ds(r0, rc), :] = jax.nn.silu(cpre).astype(o_ref.dtype)
            c_ref[pl.ds(r0, rc), :] = cpre.astype(c_ref.dtype)
            return carry

        _row_chunks(tt, chunk)

    out = pl.BlockSpec((tt, tc), lambda j, i: (i, j))
    return pl.pallas_call(
        body, grid=(c // tc, t // tt),
        in_specs=[pl.BlockSpec((tt, tc), lambda j, i: (i, j + joff)), _prev_spec(tt, tc, joff),
                  pl.BlockSpec((kw, tc), lambda j, i: (0, j)), pl.BlockSpec((1, tc), lambda j, i: (0, j))],
        out_specs=[out, out], out_shape=[S((t, c), XBC_DTYPE), S((t, c), ACT_DTYPE)],
        scratch_shapes=[pltpu.VMEM((HALO + tt, tc), F32)], name=name,
        compiler_params=_cp(("parallel", "arbitrary")))(proj, proj, w, b)


def _conv_ssm_bwd(proj, col0, w, cpre, wcol0, dact, dproj, out_col0, name):
    t = proj.shape[0]
    kw = w.shape[0]
    c = dact.shape[1]
    tt, tc = _conv_tiles(t, c)
    assert col0 % tc == 0 and wcol0 % tc == 0 and out_col0 % tc == 0
    joff, wj, oj = col0 // tc, wcol0 // tc, out_col0 // tc
    nt = t // tt

    def body(x_ref, w_ref, c_ref, cn_ref, d_ref, dn_ref, dp_in, dx_ref, dw_ref, db_ref, sd_ref):
        del dp_in
        i = pl.program_id(1)

        @pl.when(i == 0)
        def _():
            dw_ref[...] = jnp.zeros_like(dw_ref)
            db_ref[...] = jnp.zeros_like(db_ref)

        def stage(r0, rc, db):
            rows = pl.ds(r0, rc)
            d = d_ref[rows, :].astype(F32) * _silu_grad(c_ref[rows, :].astype(F32))
            sd_ref[rows, :] = d
            return db + _fold8(d)

        zero8 = jnp.zeros((8, tc), F32)
        db = _row_chunks(tt, stage, zero8)
        sd_ref[pl.ds(tt, HALO), :] = jnp.where(
            i < nt - 1, dn_ref[...].astype(F32) * _silu_grad(cn_ref[...].astype(F32)), 0.0)

        def chunk(r0, rc, dws):
            dx, dws = _conv_bwd_chunk(sd_ref, x_ref[pl.ds(r0, rc), :].astype(F32), w_ref, kw, r0, rc, dws)
            dx_ref[pl.ds(r0, rc), :] = dx.astype(dx_ref.dtype)
            return dws

        dws = _row_chunks(tt, chunk, [zero8] * kw)
        for k in range(kw):
            dw_ref[k:k + 1, :] += jnp.sum(dws[k], axis=0, keepdims=True)
        db_ref[0:1, :] += jnp.sum(db, axis=0, keepdims=True)

    acc = pl.BlockSpec((8, tc), lambda j, i: (0, j))
    return pl.pallas_call(
        body, grid=(c // tc, nt),
        in_specs=[pl.BlockSpec((tt, tc), lambda j, i: (i, j + joff)), pl.BlockSpec((kw, tc), lambda j, i: (0, j + wj)),
                  pl.BlockSpec((tt, tc), lambda j, i: (i, j + wj)), _next_spec(tt, tc, wj, t),
                  pl.BlockSpec((tt, tc), lambda j, i: (i, j)), _next_spec(tt, tc, 0, t),
                  pl.BlockSpec(memory_space=pl.ANY)],
        out_specs=[pl.BlockSpec((tt, tc), lambda j, i: (i, j + oj)), acc, acc],
        out_shape=[S(dproj.shape, dproj.dtype), S((8, c), F32), S((8, c), F32)],
        scratch_shapes=[pltpu.VMEM((tt + HALO, tc), F32)],
        input_output_aliases={6: 0}, name=name,
        compiler_params=_cp(("parallel", "arbitrary")))(proj, w, cpre, cpre, dact, dact, dproj)


def _conv_ffn_fwd(hid, w, b, name):
    t, f2 = hid.shape
    f = f2 // 2
    kw = w.shape[0]
    tt, tc = _conv_tiles(t, f)
    nj = f // tc

    def body(g_ref, gp_ref, u_ref, up_ref, wg_ref, wu_ref, bg_ref, bu_ref, o_ref, ot_ref, cv_ref, sg_ref, su_ref):
        i = pl.program_id(1)
        _fill_prev_main(sg_ref, gp_ref, g_ref, i, tt)
        _fill_prev_main(su_ref, up_ref, u_ref, i, tt)

        def chunk(r0, rc, carry):
            rows = pl.ds(r0, rc)
            gate = _taps_chunk(sg_ref, wg_ref, kw, r0, rc) + bg_ref[...]
            up = _taps_chunk(su_ref, wu_ref, kw, r0, rc) + bu_ref[...]
            o_ref[rows, :] = (jax.nn.silu(gate) * up).astype(o_ref.dtype)
            cv_ref[0, rows, :] = gate.astype(cv_ref.dtype)
            cv_ref[1, rows, :] = up.astype(cv_ref.dtype)
            return carry

        _row_chunks(tt, chunk)
        ot_ref[...] = o_ref[...].T

    return pl.pallas_call(
        body, grid=(nj, t // tt),
        in_specs=[pl.BlockSpec((tt, tc), lambda j, i: (i, j)), _prev_spec(tt, tc, 0),
                  pl.BlockSpec((tt, tc), lambda j, i: (i, j + nj)), _prev_spec(tt, tc, nj),
                  pl.BlockSpec((kw, tc), lambda j, i: (0, j)), pl.BlockSpec((kw, tc), lambda j, i: (0, j + nj)),
                  pl.BlockSpec((1, tc), lambda j, i: (0, j)), pl.BlockSpec((1, tc), lambda j, i: (0, j + nj))],
        out_specs=[pl.BlockSpec((tt, tc), lambda j, i: (i, j)), pl.BlockSpec((tc, tt), lambda j, i: (j, i)),
                   pl.BlockSpec((2, tt, tc), lambda j, i: (0, i, j))],
        out_shape=[S((t, f), ACT_DTYPE), S((f, t), ACT_DTYPE), S((2, t, f), ACT_DTYPE)],
        scratch_shapes=[pltpu.VMEM((HALO + tt, tc), F32), pltpu.VMEM((HALO + tt, tc), F32)], name=name,
        compiler_params=_cp(("parallel", "arbitrary")))(hid, hid, hid, hid, w, w, b, b)


def _conv_ffn_bwd(hid, w, cv, dact, name):
    t, f2 = hid.shape
    f = f2 // 2
    kw = w.shape[0]
    tt, tc = _conv_tiles(t, f)
    nj = f // tc
    nt = t // tt

    def body(g_ref, u_ref, wg_ref, wu_ref, cv_ref, cvn_ref, d_ref, dn_ref, dh_ref, dw_ref, db_ref, dg_ref, du_ref):
        i = pl.program_id(1)

        @pl.when(i == 0)
        def _():
            dw_ref[...] = jnp.zeros_like(dw_ref)
            db_ref[...] = jnp.zeros_like(db_ref)

        def cotangents(gate, up, dact_v):
            sg = jax.nn.sigmoid(gate)
            return dact_v * up * (sg * (1.0 + gate * (1.0 - sg))), dact_v * (gate * sg)

        def stage(r0, rc, dbs):
            rows = pl.ds(r0, rc)
            dg, du = cotangents(cv_ref[0, rows, :].astype(F32), cv_ref[1, rows, :].astype(F32), d_ref[rows, :].astype(F32))
            dg_ref[rows, :] = dg
            du_ref[rows, :] = du
            return [dbs[0] + _fold8(dg), dbs[1] + _fold8(du)]

        zero8 = jnp.zeros((8, tc), F32)
        dbs = _row_chunks(tt, stage, [zero8, zero8])
        dgn, dun = cotangents(cvn_ref[0].astype(F32), cvn_ref[1].astype(F32), dn_ref[...].astype(F32))
        dg_ref[pl.ds(tt, HALO), :] = jnp.where(i < nt - 1, dgn, 0.0)
        du_ref[pl.ds(tt, HALO), :] = jnp.where(i < nt - 1, dun, 0.0)
        for s, (sd_ref, x_ref, w_ref) in enumerate(((dg_ref, g_ref, wg_ref), (du_ref, u_ref, wu_ref))):
            def chunk(r0, rc, dws, s=s, sd_ref=sd_ref, x_ref=x_ref, w_ref=w_ref):
                dx, dws = _conv_bwd_chunk(sd_ref, x_ref[pl.ds(r0, rc), :].astype(F32), w_ref, kw, r0, rc, dws)
                dh_ref[s, pl.ds(r0, rc), :] = dx.astype(dh_ref.dtype)
                return dws

            dws = _row_chunks(tt, chunk, [zero8] * kw)
            for k in range(kw):
                dw_ref[s, k:k + 1, :] += jnp.sum(dws[k], axis=0, keepdims=True)
            db_ref[s, 0:1, :] += jnp.sum(dbs[s], axis=0, keepdims=True)

    acc = pl.BlockSpec((2, 8, tc), lambda j, i: (0, 0, j))
    dsc = pltpu.VMEM((tt + HALO, tc), F32)
    nxt = lambda j, i: (0, jnp.minimum((i + 1) * (tt // HALO), t // HALO - 1), j)
    return pl.pallas_call(
        body, grid=(nj, nt),
        in_specs=[pl.BlockSpec((tt, tc), lambda j, i: (i, j)), pl.BlockSpec((tt, tc), lambda j, i: (i, j + nj)),
                  pl.BlockSpec((kw, tc), lambda j, i: (0, j)), pl.BlockSpec((kw, tc), lambda j, i: (0, j + nj)),
                  pl.BlockSpec((2, tt, tc), lambda j, i: (0, i, j)), pl.BlockSpec((2, HALO, tc), nxt),
                  pl.BlockSpec((tt, tc), lambda j, i: (i, j)), _next_spec(tt, tc, 0, t)],
        out_specs=[pl.BlockSpec((2, tt, tc), lambda j, i: (0, i, j)), acc, acc],
        out_shape=[S((2, t, f), MXU_DTYPE), S((2, 8, f), F32), S((2, 8, f), F32)],
        scratch_shapes=[dsc, dsc], name=name,
        compiler_params=_cp(("parallel", "arbitrary")))(hid, hid, w, w, cv, cv, dact, dact)


SSD_SUB = 2


def _ssd_chunk(xs, bm, cm, dtraw, hin, bias, alog, dskip):
    n = bm.shape[0]
    row = lax.broadcasted_iota(jnp.int32, (n, n), 0)
    col = lax.broadcasted_iota(jnp.int32, (n, n), 1)
    causal = row >= col
    lane = lax.broadcasted_iota(jnp.int32, (1, 128), 1)
    sub = lax.broadcasted_iota(jnp.int32, (128, 1), 0)
    last = (lax.broadcasted_iota(jnp.int32, (n, 1), 0) == n - 1).astype(F32)
    dt = jax.nn.softplus(dtraw + bias)
    adt = dt * (-jnp.exp(alog))
    tri = causal.astype(F32)
    acs = jnp.dot(tri, adt, preferred_element_type=F32, precision=lax.Precision.HIGHEST)
    ch = lax.broadcasted_iota(jnp.int32, (128, GROUP_CH), 1)
    hd = lax.broadcasted_iota(jnp.int32, (128, GROUP_CH), 0) * HEAD_DIM
    expand = ((ch >= hd) & (ch < hd + HEAD_DIM)).astype(F32)
    acs_x = jnp.dot(acs, expand, preferred_element_type=F32, precision=lax.Precision.HIGHEST)
    alast_x = jnp.sum(acs_x * last, axis=0, keepdims=True)
    acs_t = acs.T
    scores = lax.dot_general(_mx(cm), _mx(bm), (((1,), (1,)), ((), ())), preferred_element_type=F32)
    yds, xts = [], []
    for r in range(HEADS_PER_GROUP):
        pick = (lane == r).astype(F32)
        acol = jnp.sum(acs * pick, axis=1, keepdims=True)
        arow = jnp.sum(acs_t * (sub == r).astype(F32), axis=0, keepdims=True)
        dtc = jnp.sum(dt * pick, axis=1, keepdims=True)
        lm = jnp.exp(jnp.where(causal, acol - arow, -1e30))
        xts.append(xs[r] * dtc)
        yds.append(jnp.dot(_mx(scores * lm), _mx(xts[r]), preferred_element_type=F32))
    xt = jnp.concatenate(xts, axis=1)
    yo = jnp.exp(acs_x) * jnp.dot(_mx(cm), _mx(hin), preferred_element_type=F32)
    st = lax.dot_general(_mx(bm), _mx(xt * jnp.exp(alast_x - acs_x)), (((0,), (0,)), ((), ())), preferred_element_type=F32)
    hout = jnp.exp(alast_x) * hin + st
    return jnp.concatenate(yds, axis=1) + yo + dskip * jnp.concatenate(xs, axis=1), hout


def _ssd_post(y, z, normg):
    return _rms(y * jax.nn.silu(z), normg)


def _ssd_block(datas, hin, consts):
    ys = []
    for data in datas:
        y, hin = _ssd_chunk(*data, hin, *consts)
        ys.append(y)
    return ys, hin


def _ssd_specs(d_ssm, ngrp, zcol0, rev, nb):
    cc = (lambda c: nb - 1 - c) if rev else (lambda c: c)
    rows = SSD_SUB * CHUNK
    xj, bj, cj, zj = 0, d_ssm // 128, d_ssm // 128 + ngrp, zcol0 // GROUP_CH
    const = lambda w: pl.BlockSpec((None, 8, w), lambda g, c: (g, 0, 0))
    return cc, [
        pl.BlockSpec((rows, GROUP_CH), lambda g, c: (cc(c), xj + g)),
        pl.BlockSpec((rows, 128), lambda g, c: (cc(c), bj + g)),
        pl.BlockSpec((rows, 128), lambda g, c: (cc(c), cj + g)),
        pl.BlockSpec((rows, 128), lambda g, c: (cc(c), g)),
        pl.BlockSpec((rows, GROUP_CH), lambda g, c: (cc(c), zj + g)),
        const(128), const(128), const(GROUP_CH), const(GROUP_CH)]


def _sub_rows(s):
    return slice(CHUNK * s, CHUNK * (s + 1))


def _ssd_load(x_ref, b_ref, c_ref, dt_ref, z_ref, bias_ref, alog_ref, dsk_ref, ng_ref):
    datas, zs = [], []
    for s in range(SSD_SUB):
        rows = _sub_rows(s)
        xs = [x_ref[rows, HEAD_DIM * r:HEAD_DIM * (r + 1)].astype(F32) for r in range(HEADS_PER_GROUP)]
        datas.append((xs, b_ref[rows, :].astype(F32), c_ref[rows, :].astype(F32), dt_ref[rows, :]))
        zs.append(z_ref[rows, :].astype(F32))
    return datas, zs, (bias_ref[0:1, :], alog_ref[0:1, :], dsk_ref[0:1, :]), ng_ref[0:1, :]


def _ssd_fwd(xbc, dtp, proj, zcol0, bias_p, alog_p, dskip_x, normg_x, cat, cat_t, name):
    t = xbc.shape[0]
    ngrp = bias_p.shape[0]
    d_ssm = ngrp * GROUP_CH
    rows = SSD_SUB * CHUNK
    nb = t // rows
    d_a = cat.shape[1] - d_ssm
    assert d_a % GROUP_CH == 0 and zcol0 % GROUP_CH == 0 and t % rows == 0
    _, specs = _ssd_specs(d_ssm, ngrp, zcol0, False, nb)

    def body(x_ref, b_ref, c_ref, dt_ref, z_ref, bias_ref, alog_ref, dsk_ref, ng_ref, cat_in, catt_in,
             yn_ref, ynt_ref, y_ref, hs_ref, h_ref):
        del cat_in, catt_in

        @pl.when(pl.program_id(1) == 0)
        def _():
            h_ref[...] = jnp.zeros_like(h_ref)

        datas, zs, consts, normg = _ssd_load(x_ref, b_ref, c_ref, dt_ref, z_ref, bias_ref, alog_ref, dsk_ref, ng_ref)
        hs_ref[...] = h_ref[...]
        ys, hout = _ssd_block(datas, h_ref[...], consts)
        for s in range(SSD_SUB):
            y_ref[_sub_rows(s), :] = ys[s].astype(y_ref.dtype)
            yn = _ssd_post(ys[s], zs[s], normg).astype(yn_ref.dtype)
            yn_ref[_sub_rows(s), :] = yn
            ynt_ref[:, _sub_rows(s)] = yn.T
        h_ref[...] = hout

    hbm = pl.BlockSpec(memory_space=pl.ANY)
    return pl.pallas_call(
        body, grid=(ngrp, nb), in_specs=specs + [hbm, hbm],
        out_specs=[pl.BlockSpec((rows, GROUP_CH), lambda g, c: (c, d_a // GROUP_CH + g)),
                   pl.BlockSpec((GROUP_CH, rows), lambda g, c: (d_a // GROUP_CH + g, c)),
                   pl.BlockSpec((rows, GROUP_CH), lambda g, c: (c, g)),
                   pl.BlockSpec((None, None, D_STATE, GROUP_CH), lambda g, c: (c, g, 0, 0))],
        out_shape=[S(cat.shape, cat.dtype), S(cat_t.shape, cat_t.dtype), S((t, d_ssm), ACT_DTYPE),
                   S((nb, ngrp, D_STATE, GROUP_CH), F32)],
        scratch_shapes=[pltpu.VMEM((D_STATE, GROUP_CH), F32)],
        input_output_aliases={9: 0, 10: 1}, name=name,
        compiler_params=_cp(("parallel", "arbitrary")))(xbc, xbc, xbc, dtp, proj, bias_p, alog_p, dskip_x, normg_x, cat, cat_t)


def _ssd_bwd(xbc, dtp, proj, zcol0, bias_p, alog_p, dskip_x, normg_x, hs, ypre, dcat, dproj, name):
    t = xbc.shape[0]
    ngrp = bias_p.shape[0]
    d_ssm = ngrp * GROUP_CH
    rows = SSD_SUB * CHUNK
    nb = t // rows
    d_a = dcat.shape[1] - d_ssm
    cc, specs = _ssd_specs(d_ssm, ngrp, zcol0, True, nb)

    def body(x_ref, b_ref, c_ref, dt_ref, z_ref, bias_ref, alog_ref, dsk_ref, ng_ref, hs_ref, yp_ref, dy_ref, dp_in,
             dz_ref, dx_ref, db_ref, dc_ref, ddt_ref, dbias_ref, dalog_ref, ddsk_ref, dng_ref, dh_ref):
        del dp_in

        @pl.when(pl.program_id(1) == 0)
        def _():
            dh_ref[...] = jnp.zeros_like(dh_ref)
            for r in (dbias_ref, dalog_ref, ddsk_ref, dng_ref):
                r[...] = jnp.zeros_like(r)

        datas, zs, consts, normg = _ssd_load(x_ref, b_ref, c_ref, dt_ref, z_ref, bias_ref, alog_ref, dsk_ref, ng_ref)
        dys, dng = [], jnp.zeros_like(normg)
        for s in range(SSD_SUB):
            rws = _sub_rows(s)
            _, vjp_post = jax.vjp(_ssd_post, yp_ref[rws, :].astype(F32), zs[s], normg)
            dy, dz, dg = vjp_post(dy_ref[rws, :].astype(F32))
            dys.append(dy)
            dng = dng + dg
            dz_ref[rws, :] = dz.astype(dz_ref.dtype)
        _, vjp = jax.vjp(_ssd_block, datas, hs_ref[...], consts)
        ddatas, dhin, (dbias, dalog, ddsk) = vjp((dys, dh_ref[...]))
        for s, (dxs, dbm, dcm, ddt) in enumerate(ddatas):
            rws = _sub_rows(s)
            for r in range(HEADS_PER_GROUP):
                dx_ref[rws, HEAD_DIM * r:HEAD_DIM * (r + 1)] = dxs[r].astype(dx_ref.dtype)
            db_ref[rws, :] = dbm.astype(db_ref.dtype)
            dc_ref[rws, :] = dcm.astype(dc_ref.dtype)
            ddt_ref[rws, :] = ddt
        dh_ref[...] = dhin
        dbias_ref[0:1, :] += dbias
        dalog_ref[0:1, :] += dalog
        ddsk_ref[0:1, :] += ddsk
        dng_ref[0:1, :] += dng

    acc = lambda w: pl.BlockSpec((None, 8, w), lambda g, c: (g, 0, 0))
    blk = lambda w: pl.BlockSpec((rows, w), lambda g, c: (cc(c), g))
    return pl.pallas_call(
        body, grid=(ngrp, nb),
        in_specs=specs + [pl.BlockSpec((None, None, D_STATE, GROUP_CH), lambda g, c: (cc(c), g, 0, 0)),
                          blk(GROUP_CH),
                          pl.BlockSpec((rows, GROUP_CH), lambda g, c: (cc(c), d_a // GROUP_CH + g)),
                          pl.BlockSpec(memory_space=pl.ANY)],
        out_specs=[pl.BlockSpec((rows, GROUP_CH), lambda g, c: (cc(c), zcol0 // GROUP_CH + g)),
                   blk(GROUP_CH), blk(128), blk(128), blk(128), acc(128), acc(128), acc(GROUP_CH), acc(GROUP_CH)],
        out_shape=[S(dproj.shape, dproj.dtype), S((t, d_ssm), XBC_DTYPE), S((t, ngrp * 128), XBC_DTYPE),
                   S((t, ngrp * 128), XBC_DTYPE), S((t, ngrp * 128), F32), S((ngrp, 8, 128), F32),
                   S((ngrp, 8, 128), F32), S((ngrp, 8, GROUP_CH), F32), S((ngrp, 8, GROUP_CH), F32)],
        scratch_shapes=[pltpu.VMEM((D_STATE, GROUP_CH), F32)],
        input_output_aliases={12: 0}, name=name,
        compiler_params=_cp(("parallel", "arbitrary")))(xbc, xbc, xbc, dtp, proj, bias_p, alog_p, dskip_x, normg_x, hs, ypre,
                                                        dcat, dproj)


def _adamw(parts, w, m, v, after, name):
    r, c = w.shape
    tr = _pick(r, (256, 128, 64, 32, 16, 8)) if c * 4 * 256 <= 4 * 1024 * 1024 else _pick(r, (64, 32, 16, 8))

    def body(p_ref, w_ref, m_ref, v_ref, after_ref, g_ref, d_ref, nm_ref, nv_ref):
        del after_ref
        g = p_ref[0].astype(F32)
        for k in range(1, N_DEV):
            g = g + p_ref[k].astype(F32)
        mm = ADAM_B1 * m_ref[...] + (1.0 - ADAM_B1) * g
        vv = ADAM_B2 * v_ref[...] + (1.0 - ADAM_B2) * jnp.square(g)
        m_hat = mm / (1.0 - ADAM_B1 ** ADAM_STEP)
        v_hat = vv / (1.0 - ADAM_B2 ** ADAM_STEP)
        g_ref[...] = g
        d_ref[...] = -ADAM_LR * (m_hat / (jnp.sqrt(v_hat) + ADAM_EPS) + ADAM_WD * w_ref[...])
        nm_ref[...] = mm
        nv_ref[...] = vv

    blk = pl.BlockSpec((tr, c), lambda i: (i, 0))
    return pl.pallas_call(
        body, grid=(r // tr,),
        in_specs=[pl.BlockSpec((N_DEV, tr, c), lambda i: (0, i, 0)), blk, blk, blk, pl.BlockSpec(memory_space=pl.ANY)],
        out_specs=[blk, blk, blk, blk], out_shape=[S((r, c), F32)] * 4, name=name,
        compiler_params=_cp(("parallel",)))(parts, w, m, v, after)


def _mesh_pos():
    return lax.axis_index("x"), lax.axis_index("y"), lax.axis_index("c")


def _peer(d, x, y, c):
    return (1 - x if (d >> 2) & 1 else x, 1 - y if (d >> 1) & 1 else y, 1 - c if d & 1 else c)


def _gather_two_level(blocks, name):
    n = len(blocks)

    def body(*refs):
        srcs, outs = refs[:n], refs[n:2 * n]
        send_sems, recv_sems, loc_sems = refs[2 * n:]
        x, y, c = _mesh_pos()
        lin = lambda px, py, pc: 4 * px + 2 * py + pc
        me, sibling = (x, y, c), (x, y, 1 - c)
        chips = [(1 - x, y), (x, 1 - y), (1 - x, 1 - y)]

        def copy(a, k, block, to, src=None):
            slab = outs[a].at[lin(*block)]
            return pltpu.make_async_remote_copy(
                src_ref=slab if src is None else src, dst_ref=slab, send_sem=send_sems.at[a, k],
                recv_sem=recv_sems.at[a, k], device_id=to, device_id_type=pl.DeviceIdType.MESH)

        mine = [pltpu.make_async_copy(srcs[a], outs[a].at[lin(*me)], loc_sems.at[a]) for a in range(n)]
        first = [copy(a, 0, me, sibling, src=srcs[a]) for a in range(n)]
        first += [copy(a, 1 + j, me, (*chip, c), src=srcs[a]) for j, chip in enumerate(chips) for a in range(n)]
        for cp in mine + first:
            cp.start()
        passed = []
        for j, chip in enumerate(chips):
            for a in range(n):
                copy(a, 1 + j, (*chip, c), me).wait_recv()
                passed.append(copy(a, 4 + j, (*chip, c), sibling))
                passed[-1].start()
        for a in range(n):
            copy(a, 0, sibling, me).wait_recv()
            for j, chip in enumerate(chips):
                copy(a, 4 + j, (*chip, 1 - c), me).wait_recv()
        for cp in first + passed:
            cp.wait_send()
        for cp in mine:
            cp.wait()

    hbm = pl.BlockSpec(memory_space=pl.ANY)
    return pl.pallas_call(
        body, in_specs=[hbm] * n, out_specs=[hbm] * n, out_shape=[S((N_DEV,) + b.shape, b.dtype) for b in blocks],
        scratch_shapes=[pltpu.SemaphoreType.DMA((n, N_DEV - 1)), pltpu.SemaphoreType.DMA((n, N_DEV - 1)),
                        pltpu.SemaphoreType.DMA((n,))],
        name=name, compiler_params=pltpu.CompilerParams(has_side_effects=True))(*blocks)


def _xfer_start(items, name, after=None):
    n = len(items)
    kinds = [k for k, _ in items]
    srcs = [pltpu.with_memory_space_constraint(a, pltpu.HBM) for _, a in items]
    land_shapes = [((N_DEV,) + a.shape if k == 'gather' else a.shape, a.dtype) for k, a in items]
    lands = [pltpu.with_memory_space_constraint(lax.empty(s, dt), pltpu.HBM) for s, dt in land_shapes]
    extra = [] if after is None else [after]

    def body(*refs):
        src_refs, land_refs = refs[:n], refs[n:2 * n]
        outs = refs[2 * n + len(extra):]
        sems = outs[:2 * n]
        token = outs[4 * n]
        x, y, c = _mesh_pos()
        me = 4 * x + 2 * y + c
        for a in range(n):
            for d in range(1, N_DEV):
                px, py, pc = _peer(d, x, y, c)
                src = src_refs[a] if kinds[a] == 'gather' else src_refs[a].at[4 * px + 2 * py + pc]
                pltpu.make_async_remote_copy(
                    src_ref=src, dst_ref=land_refs[a].at[me], send_sem=sems[2 * a].at[d - 1],
                    recv_sem=sems[2 * a + 1].at[d - 1], device_id=(px, py, pc),
                    device_id_type=pl.DeviceIdType.MESH).start()
        token[...] = jnp.zeros_like(token)

    hbm = pl.BlockSpec(memory_space=pltpu.HBM)
    sem = pl.BlockSpec(memory_space=pltpu.SEMAPHORE)
    out_shape = ([pltpu.SemaphoreType.DMA((N_DEV - 1,))] * (2 * n)
                 + [pltpu.HBM(a.shape, a.dtype) for a in srcs] + [pltpu.HBM(s, dt) for s, dt in land_shapes]
                 + [S((8, 128), F32)])
    res = pl.pallas_call(
        body, name=name, out_shape=out_shape,
        in_specs=[hbm] * (2 * n) + [pl.BlockSpec(memory_space=pl.ANY)] * len(extra),
        out_specs=[sem] * (2 * n) + [hbm] * (2 * n) + [pl.BlockSpec(memory_space=pltpu.VMEM)],
        input_output_aliases={**{a: 2 * n + a for a in range(n)}, **{n + a: 3 * n + a for a in range(n)}},
        compiler_params=pltpu.CompilerParams(has_side_effects=pltpu.SideEffectType.DATAFLOW_SIDE_EFFECTING),
    )(*srcs, *lands, *extra)
    return (kinds, res[:2 * n], res[2 * n:3 * n], res[3 * n:4 * n]), res[4 * n]


def _xfer_wait(handle, after, name):
    kinds, sems, src_thru, land_thru = handle
    n = len(kinds)

    def body(*refs):
        land_refs = refs[n:2 * n]
        sem_refs = refs[2 * n:4 * n]
        x, y, c = _mesh_pos()
        me = 4 * x + 2 * y + c
        for a in range(n):
            for d in range(1, N_DEV):
                slab = land_refs[a].at[me]
                cp = pltpu.make_async_remote_copy(
                    src_ref=slab, dst_ref=slab, send_sem=sem_refs[2 * a].at[d - 1], recv_sem=sem_refs[2 * a + 1].at[d - 1],
                    device_id=_peer(d, x, y, c), device_id_type=pl.DeviceIdType.MESH)
                cp.wait_send()
                cp.wait_recv()

    hbm = pl.BlockSpec(memory_space=pltpu.HBM)
    sem = pl.BlockSpec(memory_space=pltpu.SEMAPHORE)
    res = pl.pallas_call(
        body, name=name,
        out_shape=[pltpu.HBM(a.shape, a.dtype) for a in src_thru] + [pltpu.HBM(a.shape, a.dtype) for a in land_thru],
        in_specs=[hbm] * (2 * n) + [sem] * (2 * n) + [pl.BlockSpec(memory_space=pl.ANY)],
        out_specs=[hbm] * (2 * n), input_output_aliases={a: a for a in range(2 * n)},
        compiler_params=pltpu.CompilerParams(has_side_effects=pltpu.SideEffectType.DATAFLOW_SIDE_EFFECTING),
    )(*src_thru, *land_thru, *sems, after)
    x, y, c = _mesh_pos()
    me = 4 * x + 2 * y + c
    out = []
    for a in range(n):
        src = res[a]
        own = src[None] if kinds[a] == 'gather' else lax.dynamic_index_in_dim(src, me, 0, keepdims=True)
        out.append(lax.dynamic_update_index_in_dim(res[n + a], own, me, 0))
    return out


def _stack_to_full(kind, st):
    if kind == 'row':
        return st.reshape(st.shape[0] * st.shape[1], st.shape[2])
    return jnp.concatenate([st[k] for k in range(st.shape[0])], axis=1)


def _full_to_stack(kind, full):
    r, c = full.shape
    if kind == 'row':
        return full.reshape(N_DEV, r // N_DEV, c)
    w = c // N_DEV
    return jnp.stack([full[:, k * w:(k + 1) * w] for k in range(N_DEV)], axis=0)


SMALL_ROWS = 256


def _pack_small(named):
    layout = [(a.shape, a.size, -(-a.size // 1024) * 8) for a in named]
    total = -(-sum(nr for _, _, nr in layout) // SMALL_ROWS) * SMALL_ROWS * 128
    packed, off = None, 0
    for a, (_, n, nr) in zip(named, layout):
        part = jnp.pad(a.reshape(-1).astype(F32), (off, total - off - n))
        packed = part if packed is None else packed + part
        off += nr * 128
    return packed.reshape(total // 128, 128), layout


def _unpack_small(packed, layout):
    out, r0 = [], 0
    for shape, n, nr in layout:
        out.append(packed[r0:r0 + nr].reshape(-1)[:n].reshape(shape))
        r0 += nr
    return out


def _row0(acc):
    return acc[0]


def _local_step(x, p, tgt, sm, comm):
    t, d = x.shape
    h_n = sm['dt_bias'].shape[-1]
    ngrp = h_n // HEADS_PER_GROUP
    d_ssm = h_n * HEAD_DIM
    d_a = sm['ln_a_g'].shape[-1]
    d_mix = d_a + d_ssm
    d_xbc = sm['conv_ssm_b'].shape[-1]
    d_main = 2 * d_a + d_ssm + d_xbc
    assert d_xbc == d_ssm + 2 * ngrp * D_STATE and h_n <= 128
    zcol0, xcol0 = 2 * d_a, 2 * d_a + d_ssm
    vec = lambda v: v.reshape(1, -1)

    bst = jnp.pad(sm['b_s'].T, ((0, 0), (0, 128 - sm['b_s'].shape[0])))
    grp = lambda v, w: jnp.broadcast_to(jnp.pad(v.reshape(ngrp, 1, -1), ((0, 0), (0, 0), (0, w - v.size // ngrp))), (ngrp, 8, w))
    bias_p, alog_p = grp(sm['dt_bias'], 128), grp(sm['a_log'], 128)
    dskip_x = grp(jnp.repeat(sm['d_skip'], HEAD_DIM), GROUP_CH)
    normg_x = grp(sm['ssm_norm_g'], GROUP_CH)
    pad_dt = lambda v: jnp.pad(v[:, :h_n].reshape(t, ngrp, HEADS_PER_GROUP),
                               ((0, 0), (0, 0), (0, 128 - HEADS_PER_GROUP))).reshape(t, ngrp * 128)

    g_mix = vec(sm['norm_mix_g']) + comm.tok0
    a_n, a_t = _rms_fwd(x, g_mix, "rms_mix")
    wf = comm.weights('a', a_n)
    slabs = [wf['w_in'][k] for k in range(N_DEV)]
    w_main = jnp.concatenate(slabs[:-1] + [slabs[-1][:, :slabs[-1].shape[1] - h_n]], axis=1)
    w_dt = jnp.pad(slabs[-1][:, slabs[-1].shape[1] - h_n:], ((0, 0), (0, 128 - h_n)))
    proj = _mm_nn(a_n, w_main, out_dtype=ACT_DTYPE, name="mm_in")
    dtp = pad_dt(_mm_nn(a_n, w_dt, out_dtype=F32, name="mm_dt"))
    cat, cat_t = _gmlp_fwd(proj, vec(sm['ln_a_g']), vec(sm['ln_a_b']), sm['w_s'], bst, vec(sm['norm_a_g']), d_mix, "gmlp_fwd")
    xbc, cpre = _conv_ssm_fwd(proj, xcol0, wf['conv_ssm_w'], vec(sm['conv_ssm_b']), "conv_ssm_fwd")
    cat, cat_t, ypre, hs = _ssd_fwd(xbc, dtp, proj, zcol0, bias_p, alog_p, dskip_x, normg_x, cat, cat_t, "ssd_fwd")
    wf.update(comm.weights('b', hs))
    h1 = _mm_nn(cat, wf['w_out'], out_dtype=F32, name="mm_out", res=x)
    f_n, f_t = _rms_fwd(h1, vec(sm['norm_ffn_g']), "rms_ffn")
    hid = _mm_nn(f_n, wf['w_up'], out_dtype=ACT_DTYPE, name="mm_up")
    act, act_t, cv = _conv_ffn_fwd(hid, wf['conv_ffn_w'], vec(sm['conv_ffn_b']), "conv_ffn_fwd")
    h2 = _mm_nn(act, wf['w_down'], out_dtype=F32, name="mm_down", res=h1)
    r_n, r_t = _rms_fwd(h2, vec(sm['norm_ple_g']), "rms_ple")
    q = _mm_nn(r_n, wf['w_ple_gate'], out_dtype=ACT_DTYPE, name="mm_pg")
    p_m = p.astype(MXU_DTYPE)
    pe = _mm_nn(p_m, wf['w_ple'], out_dtype=ACT_DTYPE, name="mm_ple")

    loss, dh3, dq, dpe, dgf = _head(h2, q, pe, tgt, vec(sm['norm_final_g']), "head")
    wgrad = lambda act_t, g, name, **kw: _mm_nn(act_t, g, out_dtype=WIRE_DTYPE, name=name, wide=True, **kw)
    gs = {}
    gs['norm_final_g'] = _row0(dgf)
    g_ple = wgrad(p_m.T, dpe, "wg_ple")
    g_pg = wgrad(r_t, dq, "wg_pg")
    dr = _mm_nt(dq, wf['w_ple_gate'], out_dtype=ACT_DTYPE, name="dg_pg")
    dh2m, dg = _rms_bwd(h2, vec(sm['norm_ple_g']), dr, dh3, "rms_ple_bwd", DRES_DTYPE)
    gs['norm_ple_g'] = _row0(dg)
    g_down = wgrad(act_t, dh2m, "wg_down")
    tok = comm.send('1', {'w_ple': g_ple, 'w_ple_gate': g_pg, 'w_down': g_down})
    dact = _mm_nt(dh2m, wf['w_down'], out_dtype=ACT_DTYPE, name="dg_down")
    dhid, dcw, dcb = _conv_ffn_bwd(hid, wf['conv_ffn_w'] + tok, cv, dact, "conv_ffn_bwd")
    kf = wf['conv_ffn_w'].shape[0]
    g_cf = jnp.concatenate([dcw[0, :kf], dcw[1, :kf]], axis=1)
    gs['conv_ffn_b'] = jnp.concatenate([dcb[0, 0], dcb[1, 0]], axis=0)
    g_up = wgrad(f_t, dhid, "wg_up", b_split=2, out_slabs=N_DEV)
    df = _mm_nt(dhid, wf['w_up'], out_dtype=ACT_DTYPE, name="dg_up", a_split=2)
    dh1m, dg = _rms_bwd(h1, vec(sm['norm_ffn_g']), df, dh2m, "rms_ffn_bwd", DRES_DTYPE)
    gs['norm_ffn_g'] = _row0(dg)
    g_out = wgrad(cat_t, dh1m, "wg_out")
    tok = comm.send('2', {'conv_ffn_w': g_cf, 'w_up': g_up, 'w_out': g_out}, stacked=('w_up',))
    dcat = _mm_nt(dh1m, wf['w_out'], out_dtype=ACT_DTYPE, name="dg_out")
    dproj, dlng, dlnb, dws, dbst, dng = _gmlp_bwd(proj, vec(sm['ln_a_g']) + tok, vec(sm['ln_a_b']), sm['w_s'], bst,
                                                  vec(sm['norm_a_g']), dcat, d_main, "gmlp_bwd")
    gs['ln_a_g'], gs['ln_a_b'], gs['w_s'], gs['norm_a_g'] = _row0(dlng), _row0(dlnb), dws, _row0(dng)
    gs['b_s'] = dbst[:, :sm['b_s'].shape[0]].T
    dproj, dxs, dbm, dcm, ddtp, dbias, dalog, ddsk, dsng = _ssd_bwd(
        xbc, dtp, proj, zcol0, bias_p, alog_p, dskip_x, normg_x, hs, ypre, dcat, dproj, "ssd_bwd")
    gs['dt_bias'] = dbias[:, 0, :HEADS_PER_GROUP].reshape(h_n)
    gs['a_log'] = dalog[:, 0, :HEADS_PER_GROUP].reshape(h_n)
    gs['d_skip'] = ddsk[:, 0, :].reshape(h_n, HEAD_DIM).sum(axis=-1)
    gs['ssm_norm_g'] = dsng[:, 0, :].reshape(d_ssm)
    dws_c, dbs_c = [], []
    off = 0
    for nm, dpart in (("x", dxs), ("b", dbm), ("c", dcm)):
        dproj, dw_c, db_c = _conv_ssm_bwd(proj, xcol0 + off, wf['conv_ssm_w'], cpre, off, dpart, dproj,
                                          xcol0 + off, "conv_ssm_bwd_" + nm)
        dws_c.append(dw_c[:wf['conv_ssm_w'].shape[0]])
        dbs_c.append(db_c[0])
        off += dpart.shape[1]
    g_cs = jnp.concatenate(dws_c, axis=1)
    gs['conv_ssm_b'] = jnp.concatenate(dbs_c, axis=0)
    ddt = jnp.pad(ddtp.reshape(t, ngrp, 128)[:, :, :HEADS_PER_GROUP].reshape(t, h_n), ((0, 0), (0, 128 - h_n))).astype(MXU_DTYPE)
    g_in = jnp.concatenate([wgrad(a_t, dproj, "wg_in"), wgrad(a_t, ddt, "wg_dt")[:, :h_n]], axis=1)
    tok = comm.send('3', {'conv_ssm_w': g_cs, 'w_in': g_in}, [(n, gs[n]) for n in REPLICATED if n != 'norm_mix_g'])
    da = _mm_nt(ddt + tok.astype(ddt.dtype), w_dt, out_dtype=F32, name="dg_dt")
    da = _mm_nt(dproj, w_main, out_dtype=ACT_DTYPE, name="dg_in", res=da)
    dx, dg = _rms_bwd(x, g_mix + tok, da, dh1m, "rms_mix_bwd", F32)
    return dx, comm.send('4', {}, [('norm_mix_g', _row0(dg)), ('loss', loss[0, 0:1])])


def kernel(x, p, norm_mix_g, w_in, ln_a_g, ln_a_b, w_s, b_s, norm_a_g, conv_ssm_w, conv_ssm_b, dt_bias, a_log, d_skip, ssm_norm_g, w_out, norm_ffn_g, w_up, conv_ffn_w, conv_ffn_b, w_down, norm_ple_g, w_ple_gate, w_ple, norm_final_g, loss_target, m_norm_mix_g, m_w_in, m_ln_a_g, m_ln_a_b, m_w_s, m_b_s, m_norm_a_g, m_conv_ssm_w, m_conv_ssm_b, m_dt_bias, m_a_log, m_d_skip, m_ssm_norm_g, m_w_out, m_norm_ffn_g, m_w_up, m_conv_ffn_w, m_conv_ffn_b, m_w_down, m_norm_ple_g, m_w_ple_gate, m_w_ple, m_norm_final_g, v_norm_mix_g, v_w_in, v_ln_a_g, v_ln_a_b, v_w_s, v_b_s, v_norm_a_g, v_conv_ssm_w, v_conv_ssm_b, v_dt_bias, v_a_log, v_d_skip, v_ssm_norm_g, v_w_out, v_norm_ffn_g, v_w_up, v_conv_ffn_w, v_conv_ffn_b, v_w_down, v_norm_ple_g, v_w_ple_gate, v_w_ple, v_norm_final_g):
    given = dict(locals())
    wts = {n: given[n] for n in WEIGHTS}
    ms = {n: given["m_" + n] for n in WEIGHTS}
    vs = {n: given["v_" + n] for n in WEIGHTS}
    sm = {n: (wts[n][0] if wts[n].ndim > 1 else wts[n]) for n in REPLICATED}
    comm = _Comm({n: wts[n][0] for n in SHARDED})
    dx, _ = _local_step(x[0], p[0, 0], loss_target[0], sm, comm)

    out, loss_out, after = {}, None, comm.last_token
    for tag, names, small_names, layout, handle in comm.sent:
        recv = _xfer_wait(handle, after, "grads_%s_wait" % tag)
        for n, parts in zip(names, recv):
            out[n] = _adamw(parts, wts[n][0], ms[n][0], vs[n][0], after, "adamw_" + n)
            after = out[n][1]
        if small_names:
            pick = lambda src, fill: _pack_small([src[n] if n in src else jnp.full((1,), fill, F32) for n in small_names])[0]
            res = _adamw(recv[-1], pick(wts, 0.0), pick(ms, 0.0), pick(vs, 1.0), after, "adamw_small_" + tag)
            after = res[1]
            res = [_unpack_small(o, layout) for o in res]
            for i, n in enumerate(small_names):
                if n == 'loss':
                    loss_out = res[0][i].reshape(())
                else:
                    out[n] = [res[k][i] for k in range(4)]
    return (loss_out, dx[None], *[out[n][k].reshape(wts[n].shape) for k in range(4) for n in WEIGHTS])


class _Comm:
    GATHER_GROUPS = {'a': ('w_in', 'conv_ssm_w'), 'b': ('w_out', 'w_up', 'conv_ffn_w', 'w_down', 'w_ple_gate', 'w_ple')}

    def __init__(self, blocks):
        wired = lambda grp: [blocks[n].astype(_wire(n)) for n in self.GATHER_GROUPS[grp]]
        self.stacks_a = _gather_two_level(wired('a'), "gather_a")
        self.handle_b, tok = _xfer_start([('gather', b) for b in wired('b')], "gather_b_start", after=self.stacks_a[0])
        self.tok0 = tok[0, 0]
        self.sent = []

    def weights(self, grp, after):
        stacks = self.stacks_a if grp == 'a' else _xfer_wait(self.handle_b, after, "gather_b_wait")
        return {n: st if n == 'w_in' else _stack_to_full(SHARDED[n], st) for n, st in zip(self.GATHER_GROUPS[grp], stacks)}

    def send(self, tag, gw, small=None, stacked=()):
        items = [('scatter', g if n in stacked else _full_to_stack(SHARDED[n], g.astype(_wire(n)))) for n, g in gw.items()]
        layout, small_names = None, []
        if small:
            packed, layout = _pack_small([a for _, a in small])
            small_names = [n for n, _ in small]
            items.append(('gather', packed))
        handle, self.last_token = _xfer_start(items, "grads_%s_start" % tag)
        self.sent.append((tag, list(gw), small_names, layout, handle))
        return self.last_token[0, 0]


def _wire(name):
    return F32 if name in F32_ON_WIRE else WIRE_DTYPE
```

```python
import jax
import jax.numpy as jnp
from jax import lax
from jax.experimental import pallas as pl
from jax.experimental.pallas import tpu as pltpu

F32 = jnp.float32
MXU_DTYPE = jnp.bfloat16
ACT_DTYPE = jnp.bfloat16
XBC_DTYPE = jnp.bfloat16
WIRE_DTYPE = jnp.bfloat16
DRES_DTYPE = jnp.bfloat16
EPS = 1e-6
CHUNK = 128
D_STATE = 128
HEAD_DIM = 64
HEADS_PER_GROUP = 4
GROUP_CH = HEAD_DIM * HEADS_PER_GROUP
HALO = 16
N_DEV = 8
VMEM_LIMIT = 56 * 1024 * 1024

ADAM_LR = 0.001
ADAM_B1 = 0.9
ADAM_B2 = 0.999
ADAM_EPS = 1e-08
ADAM_WD = 0.01
ADAM_STEP = 10

WEIGHTS = ['norm_mix_g', 'w_in', 'ln_a_g', 'ln_a_b', 'w_s', 'b_s', 'norm_a_g', 'conv_ssm_w', 'conv_ssm_b', 'dt_bias',
           'a_log', 'd_skip', 'ssm_norm_g', 'w_out', 'norm_ffn_g', 'w_up', 'conv_ffn_w', 'conv_ffn_b', 'w_down',
           'norm_ple_g', 'w_ple_gate', 'w_ple', 'norm_final_g']
SHARDED = {'w_in': 'col', 'conv_ssm_w': 'col', 'w_out': 'row', 'w_up': 'col', 'conv_ffn_w': 'col', 'w_down': 'row',
           'w_ple_gate': 'row', 'w_ple': 'col'}
F32_ON_WIRE = ('conv_ssm_w', 'conv_ffn_w')
REPLICATED = [n for n in WEIGHTS if n not in SHARDED]

S = jax.ShapeDtypeStruct


def _pick(dim, cands):
    for c in cands:
        if c <= dim and dim % c == 0:
            return c
    return dim


def _cp(sem, vmem=VMEM_LIMIT):
    return pltpu.CompilerParams(dimension_semantics=sem, vmem_limit_bytes=vmem)


def _mx(v):
    return v.astype(MXU_DTYPE)


def _rms(v, g):
    return v * lax.rsqrt(jnp.mean(v * v, axis=-1, keepdims=True) + EPS) * g


MM_VMEM_BUDGET = 42 * 1024 * 1024


def _mm_tiles(m, n, k, out_bytes, has_res, tn_cands=(512, 256, 128), k_mult=1, tm_cands=(1024, 512), tn_alts=2):
    tns = [c for c in tn_cands if c <= n and n % c == 0][:tn_alts] or [n]
    ks = k // k_mult
    best = None
    for tn in tns:
        for tm in [c for c in tm_cands if m % c == 0] or [_pick(m, (256, 128))]:
            for nk in range(1, ks // 128 + 1):
                if ks % nk or (ks // nk) % 128:
                    continue
                tk = ks // nk
                need = 2 * 2 * (tm * tk + tk * tn) + tm * tn * (4 + 2 * out_bytes + (8 if has_res else 0))
                if need <= MM_VMEM_BUDGET:
                    if best is None or (nk, -tm, -tn) < best[0]:
                        best = ((nk, -tm, -tn), (tm, tn, tk))
                    break
    return best[1] if best else (_pick(m, (512, 256, 128)), tns[0], _pick(ks, (128,)))


def _mm_body(dot, nk, has_res):
    def body(*refs):
        if has_res:
            a_ref, b_ref, r_ref, o_ref, acc_ref = refs
        else:
            a_ref, b_ref, o_ref, acc_ref = refs
            r_ref = None
        kk = pl.program_id(2)
        d = dot(a_ref[...], b_ref[...])

        def fin(acc):
            if r_ref is not None:
                acc = acc + r_ref[...]
            o_ref[...] = acc.astype(o_ref.dtype)

        if nk == 1:
            fin(d)
        else:
            @pl.when(kk == 0)
            def _():
                acc_ref[...] = d

            if nk > 2:
                @pl.when((kk > 0) & (kk < nk - 1))
                def _():
                    acc_ref[...] += d

            @pl.when(kk == nk - 1)
            def _():
                fin(acc_ref[...] + d)

    return body


def _mm_call(body, grid, a_spec, b_spec, tm, tn, m, n, out_dtype, name, args, res, out_slabs=1):
    in_specs = [a_spec, b_spec]
    if res is not None:
        in_specs.append(pl.BlockSpec((tm, tn), lambda i, j, kk: (i, j)))
        args = args + [res]
    if out_slabs == 1:
        out_spec, out_shape = pl.BlockSpec((tm, tn), lambda i, j, kk: (i, j)), S((m, n), out_dtype)
    else:
        out_spec, out_shape = pl.BlockSpec((None, tm, tn), lambda i, j, kk: (j, i, 0)), S((out_slabs, m, tn), out_dtype)
    return pl.pallas_call(
        body, grid=grid, in_specs=in_specs, out_specs=out_spec, out_shape=out_shape,
        scratch_shapes=[pltpu.VMEM((tm, tn), F32)], name=name,
        compiler_params=_cp(("parallel", "parallel", "arbitrary")))(*args)


def _mm_nn(a, b, *, out_dtype, name, res=None, b_split=1, wide=False, out_slabs=1):
    m, k = a.shape
    n = b.shape[1] if b_split == 1 else b.shape[2] * b_split
    tn_cands = (n // out_slabs,) if out_slabs > 1 else (1024, 512, 256, 128) if wide else (512, 256, 128)
    tm, tn, tk = _mm_tiles(m, n // b_split, k, jnp.dtype(out_dtype).itemsize, res is not None, tn_cands=tn_cands,
                           tn_alts=1 if wide or out_slabs > 1 else 2,
                           tm_cands=(1024, 512) if wide or out_slabs > 1 else (2048, 1024, 512))
    assert out_slabs == 1 or (tn * out_slabs == n and tn % 128 == 0)
    nk = k // tk
    njs = (n // b_split) // tn
    body = _mm_body(lambda x, y: jnp.dot(x, y, preferred_element_type=F32), nk, res is not None)
    if b_split == 1:
        b_spec = pl.BlockSpec((tk, tn), lambda i, j, kk: (kk, j))
    else:
        b_spec = pl.BlockSpec((None, tk, tn), lambda i, j, kk: (j // njs, kk, j % njs))
    return _mm_call(body, (m // tm, n // tn, nk), pl.BlockSpec((tm, tk), lambda i, j, kk: (i, kk)), b_spec,
                    tm, tn, m, n, out_dtype, name, [a, b], res, out_slabs)


def _mm_nt(a, b, *, out_dtype, name, res=None, a_split=1):
    if a_split == 1:
        m, k = a.shape
    else:
        m, k = a.shape[1], a.shape[2] * a_split
    n = b.shape[0]
    tm, tn, tk = _mm_tiles(m, n, k, jnp.dtype(out_dtype).itemsize, res is not None, k_mult=a_split,
                           tm_cands=(2048, 1024))
    nk = k // tk
    nks = nk // a_split
    body = _mm_body(lambda x, y: lax.dot_general(x, y, (((1,), (1,)), ((), ())), preferred_element_type=F32),
                    nk, res is not None)
    if a_split == 1:
        a_spec = pl.BlockSpec((tm, tk), lambda i, j, kk: (i, kk))
    else:
        a_spec = pl.BlockSpec((None, tm, tk), lambda i, j, kk: (kk // nks, i, kk % nks))
    return _mm_call(body, (m // tm, n // tn, nk), a_spec, pl.BlockSpec((tn, tk), lambda i, j, kk: (j, kk)),
                    tm, tn, m, n, out_dtype, name, [a, b], res)


def _rms_fwd(x, g, name):
    t, d = x.shape
    tr = _pick(t, (512, 256, 128))

    def body(x_ref, g_ref, o_ref, ot_ref):
        y = _rms(x_ref[...], g_ref[...]).astype(o_ref.dtype)
        o_ref[...] = y
        ot_ref[...] = y.T

    return pl.pallas_call(
        body, grid=(t // tr,),
        in_specs=[pl.BlockSpec((tr, d), lambda i: (i, 0)), pl.BlockSpec((1, d), lambda i: (0, 0))],
        out_specs=[pl.BlockSpec((tr, d), lambda i: (i, 0)), pl.BlockSpec((d, tr), lambda i: (0, i))],
        out_shape=[S((t, d), ACT_DTYPE), S((d, t), ACT_DTYPE)], name=name,
        compiler_params=_cp(("parallel",)))(x, g)


def _rms_bwd(xin, g, dn, dres, name, out_dtype):
    t, d = xin.shape
    tr = _pick(t, (256, 128))

    def body(x_ref, g_ref, dn_ref, dr_ref, dx_ref, dg_ref):
        @pl.when(pl.program_id(0) == 0)
        def _():
            dg_ref[...] = jnp.zeros_like(dg_ref)

        _, vjp = jax.vjp(_rms, x_ref[...], g_ref[...])
        dx, dg = vjp(dn_ref[...].astype(F32))
        dx_ref[...] = (dr_ref[...].astype(F32) + dx).astype(dx_ref.dtype)
        dg_ref[0:1, :] += dg

    row = pl.BlockSpec((tr, d), lambda i: (i, 0))
    return pl.pallas_call(
        body, grid=(t // tr,),
        in_specs=[row, pl.BlockSpec((1, d), lambda i: (0, 0)), row, row],
        out_specs=[row, pl.BlockSpec((8, d), lambda i: (0, 0))],
        out_shape=[S((t, d), out_dtype), S((8, d), F32)], name=name,
        compiler_params=_cp(("arbitrary",)))(xin, g, dn, dres)


def _head(h2, q, pe, tgt, gf, name):
    t, d = h2.shape
    tr = _pick(t, (256, 128))

    def f(h2v, qv, pev, gfv, tv):
        h3 = h2v + jax.nn.sigmoid(qv) * pev
        y = _rms(h3, gfv)
        return 0.5 * jnp.sum(jnp.mean(jnp.square(y - tv), axis=-1))

    def body(h2_ref, q_ref, pe_ref, t_ref, g_ref, loss_ref, dh_ref, dq_ref, dpe_ref, dg_ref):
        @pl.when(pl.program_id(0) == 0)
        def _():
            loss_ref[...] = jnp.zeros_like(loss_ref)
            dg_ref[...] = jnp.zeros_like(dg_ref)

        tv = t_ref[...]
        loss, vjp = jax.vjp(lambda a, b, c, e: f(a, b, c, e, tv), h2_ref[...], q_ref[...].astype(F32),
                            pe_ref[...].astype(F32), g_ref[...])
        dh, dq, dpe, dg = vjp(jnp.ones((), F32))
        loss_ref[...] += jnp.full(loss_ref.shape, loss, F32)
        dh_ref[...] = dh.astype(dh_ref.dtype)
        dq_ref[...] = dq.astype(dq_ref.dtype)
        dpe_ref[...] = dpe.astype(dpe_ref.dtype)
        dg_ref[0:1, :] += dg

    row = pl.BlockSpec((tr, d), lambda i: (i, 0))
    return pl.pallas_call(
        body, grid=(t // tr,),
        in_specs=[row, row, row, row, pl.BlockSpec((1, d), lambda i: (0, 0))],
        out_specs=[pl.BlockSpec((8, 128), lambda i: (0, 0)), row, row, row, pl.BlockSpec((8, d), lambda i: (0, 0))],
        out_shape=[S((8, 128), F32), S((t, d), DRES_DTYPE), S((t, d), MXU_DTYPE), S((t, d), MXU_DTYPE), S((8, d), F32)],
        name=name, compiler_params=_cp(("arbitrary",)))(h2, q, pe, tgt, gf)


def _gmlp_block(us, vs, lng, lnb, wss, bss, ng):
    n = us[0].shape[0]
    row = lax.broadcasted_iota(jnp.int32, (n, n), 0)
    col = lax.broadcasted_iota(jnp.int32, (n, n), 1)
    outs = []
    for u0, v0, lg, lb, ws, bs in zip(us, vs, lng, lnb, wss, bss):
        u = jax.nn.gelu(u0)
        v = jax.nn.gelu(v0)
        mu = jnp.mean(v, axis=-1, keepdims=True)
        var = jnp.mean(jnp.square(v - mu), axis=-1, keepdims=True)
        vn = (v - mu) * lax.rsqrt(var + EPS) * lg + lb
        w = jnp.where(row >= col, ws, 0.0)
        sg = jnp.dot(_mx(w), _mx(vn), preferred_element_type=F32) + bs
        outs.append(u * sg)
    return _rms(jnp.concatenate(outs, axis=1), ng)


def _gmlp_load(proj_ref, lng_ref, lnb_ref, ws_ref, bst_ref, d_a, ng):
    sl = lambda g: slice(CHUNK * g, CHUNK * (g + 1))
    us = [proj_ref[:, sl(g)].astype(F32) for g in range(ng)]
    vs = [proj_ref[:, d_a + CHUNK * g: d_a + CHUNK * (g + 1)].astype(F32) for g in range(ng)]
    lng = [lng_ref[:, sl(g)] for g in range(ng)]
    lnb = [lnb_ref[:, sl(g)] for g in range(ng)]
    wss = [ws_ref[g] for g in range(ng)]
    bss = [bst_ref[:, g:g + 1] for g in range(ng)]
    return us, vs, lng, lnb, wss, bss


def _gmlp_fwd(proj, ln_g, ln_b, w_s, bst, norm_g, d_mix, name):
    t = proj.shape[0]
    ng = w_s.shape[0]
    d_a = ng * CHUNK

    def body(proj_ref, lng_ref, lnb_ref, ws_ref, bst_ref, ng_ref, o_ref, ot_ref):
        args = _gmlp_load(proj_ref, lng_ref, lnb_ref, ws_ref, bst_ref, d_a, ng)
        y = _gmlp_block(*args, ng_ref[...]).astype(o_ref.dtype)
        o_ref[...] = y
        ot_ref[...] = y.T

    vec = pl.BlockSpec((1, d_a), lambda c: (0, 0))
    return pl.pallas_call(
        body, grid=(t // CHUNK,),
        in_specs=[pl.BlockSpec((CHUNK, 2 * d_a), lambda c: (c, 0)), vec, vec,
                  pl.BlockSpec((ng, CHUNK, CHUNK), lambda c: (0, 0, 0)), pl.BlockSpec((CHUNK, 128), lambda c: (0, 0)), vec],
        out_specs=[pl.BlockSpec((CHUNK, d_a), lambda c: (c, 0)), pl.BlockSpec((d_a, CHUNK), lambda c: (0, c))],
        out_shape=[S((t, d_mix), ACT_DTYPE), S((d_mix, t), ACT_DTYPE)],
        name=name, compiler_params=_cp(("parallel",)))(proj, ln_g, ln_b, w_s, bst, norm_g)


def _gmlp_bwd(proj, ln_g, ln_b, w_s, bst, norm_g, dcat, d_proj, name):
    t = proj.shape[0]
    ng = w_s.shape[0]
    d_a = ng * CHUNK

    def body(proj_ref, lng_ref, lnb_ref, ws_ref, bst_ref, ng_ref, dy_ref,
             dp_ref, dlng_ref, dlnb_ref, dws_ref, dbst_ref, dng_ref):
        @pl.when(pl.program_id(0) == 0)
        def _():
            for r in (dlng_ref, dlnb_ref, dws_ref, dbst_ref, dng_ref):
                r[...] = jnp.zeros_like(r)

        args = _gmlp_load(proj_ref, lng_ref, lnb_ref, ws_ref, bst_ref, d_a, ng)
        _, vjp = jax.vjp(_gmlp_block, *args, ng_ref[...])
        dus, dvs, dlng, dlnb, dwss, dbss, dng = vjp(dy_ref[...].astype(F32))
        lane = lax.broadcasted_iota(jnp.int32, (1, 128), 1)
        dbst = jnp.zeros((CHUNK, 128), F32)
        for g in range(ng):
            dp_ref[:, CHUNK * g:CHUNK * (g + 1)] = dus[g].astype(dp_ref.dtype)
            dp_ref[:, d_a + CHUNK * g:d_a + CHUNK * (g + 1)] = dvs[g].astype(dp_ref.dtype)
            dlng_ref[0:1, CHUNK * g:CHUNK * (g + 1)] += dlng[g]
            dlnb_ref[0:1, CHUNK * g:CHUNK * (g + 1)] += dlnb[g]
            dws_ref[g] += dwss[g]
            dbst = dbst + dbss[g] * (lane == g).astype(F32)
        dbst_ref[...] += dbst
        dng_ref[0:1, :] += dng

    vec = pl.BlockSpec((1, d_a), lambda c: (0, 0))
    acc = pl.BlockSpec((8, d_a), lambda c: (0, 0))
    wspec = pl.BlockSpec((ng, CHUNK, CHUNK), lambda c: (0, 0, 0))
    bspec = pl.BlockSpec((CHUNK, 128), lambda c: (0, 0))
    return pl.pallas_call(
        body, grid=(t // CHUNK,),
        in_specs=[pl.BlockSpec((CHUNK, 2 * d_a), lambda c: (c, 0)), vec, vec, wspec, bspec, vec,
                  pl.BlockSpec((CHUNK, d_a), lambda c: (c, 0))],
        out_specs=[pl.BlockSpec((CHUNK, 2 * d_a), lambda c: (c, 0)), acc, acc, wspec, bspec, acc],
        out_shape=[S((t, d_proj), ACT_DTYPE), S((8, d_a), F32), S((8, d_a), F32), S((ng, CHUNK, CHUNK), F32),
                   S((CHUNK, 128), F32), S((8, d_a), F32)],
        name=name, compiler_params=_cp(("arbitrary",)))(proj, ln_g, ln_b, w_s, bst, norm_g, dcat)


def _silu_grad(c):
    s = jax.nn.sigmoid(c)
    return s * (1.0 + c * (1.0 - s))


def _fill_prev_main(s_ref, prev_ref, main_ref, i, tt):
    s_ref[pl.ds(0, HALO), :] = jnp.where(i > 0, prev_ref[...].astype(F32), 0.0)
    s_ref[pl.ds(HALO, tt), :] = main_ref[...].astype(F32)


def _prev_spec(tt, tc, joff):
    return pl.BlockSpec((HALO, tc), lambda j, i: (jnp.maximum(i * (tt // HALO) - 1, 0), j + joff))


def _next_spec(tt, tc, joff, t):
    return pl.BlockSpec((HALO, tc), lambda j, i: (jnp.minimum((i + 1) * (tt // HALO), t // HALO - 1), j + joff))


CONV_RC = 32


def _conv_tiles(t, c):
    return _pick(t, (1024, 512, 256, 128)), _pick(c, (256, 128))


def _row_chunks(tt, fn, init=0):
    rc = min(CONV_RC, tt)
    return lax.fori_loop(0, tt // rc, lambda q, c: fn(pl.multiple_of(q * rc, rc), rc, c), init)


def _fold8(p):
    acc = p[0:8]
    for r in range(8, p.shape[0], 8):
        acc = acc + p[r:r + 8]
    return acc


def _taps_chunk(s_ref, w_ref, kw, r0, rc):
    xe = s_ref[pl.ds(HALO - 8 + r0, rc + 8), :]
    acc = w_ref[0:1, :] * xe[8 - (kw - 1):8 - (kw - 1) + rc]
    for k in range(1, kw):
        acc = acc + w_ref[k:k + 1, :] * xe[8 - (kw - 1) + k:8 - (kw - 1) + k + rc]
    return acc


def _conv_bwd_chunk(sd_ref, x, w_ref, kw, r0, rc, dws):
    de = sd_ref[pl.ds(r0, rc + 8), :]
    dx, out = None, list(dws)
    for j in range(kw):
        d = de[j:j + rc]
        k = kw - 1 - j
        term = w_ref[k:k + 1, :] * d
        dx = term if dx is None else dx + term
        out[k] = out[k] + _fold8(x * d)
    return dx, out


def _conv_ssm_fwd(proj, col0, w, b, name):
    t = proj.shape[0]
    kw, c = w.shape
    tt, tc = _conv_tiles(t, c)
    joff = col0 // tc
    assert col0 % tc == 0

    def body(x_ref, xp_ref, w_ref, b_ref, o_ref, c_ref, s_ref):
        _fill_prev_main(s_ref, xp_ref, x_ref, pl.program_id(1), tt)

        def chunk(r0, rc, carry):
            cpre = _taps_chunk(s_ref, w_ref, kw, r0, rc) + b_ref[...]
            o_ref[pl.ds(r0, rc), :] = jax.nn.silu(cpre).astype(o_ref.dtype)
            c_ref[pl.ds(r0, rc), :] = cpre.astype(c_ref.dtype)
            return carry

        _row_chunks(tt, chunk)

    out = pl.BlockSpec((tt, tc), lambda j, i: (i, j))
    return pl.pallas_call(
        body, grid=(c // tc, t // tt),
        in_specs=[pl.BlockSpec((tt, tc), lambda j, i: (i, j + joff)), _prev_spec(tt, tc, joff),
                  pl.BlockSpec((kw, tc), lambda j, i: (0, j)), pl.BlockSpec((1, tc), lambda j, i: (0, j))],
        out_specs=[out, out], out_shape=[S((t, c), XBC_DTYPE), S((t, c), ACT_DTYPE)],
        scratch_shapes=[pltpu.VMEM((HALO + tt, tc), F32)], name=name,
        compiler_params=_cp(("parallel", "arbitrary")))(proj, proj, w, b)


def _conv_ssm_bwd(proj, col0, w, cpre, wcol0, dact, dproj, out_col0, name):
    t = proj.shape[0]
    kw = w.shape[0]
    c = dact.shape[1]
    tt, tc = _conv_tiles(t, c)
    assert col0 % tc == 0 and wcol0 % tc == 0 and out_col0 % tc == 0
    joff, wj, oj = col0 // tc, wcol0 // tc, out_col0 // tc
    nt = t // tt

    def body(x_ref, w_ref, c_ref, cn_ref, d_ref, dn_ref, dp_in, dx_ref, dw_ref, db_ref, sd_ref):
        del dp_in
        i = pl.program_id(1)

        @pl.when(i == 0)
        def _():
            dw_ref[...] = jnp.zeros_like(dw_ref)
            db_ref[...] = jnp.zeros_like(db_ref)

        def stage(r0, rc, db):
            rows = pl.ds(r0, rc)
            d = d_ref[rows, :].astype(F32) * _silu_grad(c_ref[rows, :].astype(F32))
            sd_ref[rows, :] = d
            return db + _fold8(d)

        zero8 = jnp.zeros((8, tc), F32)
        db = _row_chunks(tt, stage, zero8)
        sd_ref[pl.ds(tt, HALO), :] = jnp.where(
            i < nt - 1, dn_ref[...].astype(F32) * _silu_grad(cn_ref[...].astype(F32)), 0.0)

        def chunk(r0, rc, dws):
            dx, dws = _conv_bwd_chunk(sd_ref, x_ref[pl.ds(r0, rc), :].astype(F32), w_ref, kw, r0, rc, dws)
            dx_ref[pl.ds(r0, rc), :] = dx.astype(dx_ref.dtype)
            return dws

        dws = _row_chunks(tt, chunk, [zero8] * kw)
        for k in range(kw):
            dw_ref[k:k + 1, :] += jnp.sum(dws[k], axis=0, keepdims=True)
        db_ref[0:1, :] += jnp.sum(db, axis=0, keepdims=True)

    acc = pl.BlockSpec((8, tc), lambda j, i: (0, j))
    return pl.pallas_call(
        body, grid=(c // tc, nt),
        in_specs=[pl.BlockSpec((tt, tc), lambda j, i: (i, j + joff)), pl.BlockSpec((kw, tc), lambda j, i: (0, j + wj)),
                  pl.BlockSpec((tt, tc), lambda j, i: (i, j + wj)), _next_spec(tt, tc, wj, t),
                  pl.BlockSpec((tt, tc), lambda j, i: (i, j)), _next_spec(tt, tc, 0, t),
                  pl.BlockSpec(memory_space=pl.ANY)],
        out_specs=[pl.BlockSpec((tt, tc), lambda j, i: (i, j + oj)), acc, acc],
        out_shape=[S(dproj.shape, dproj.dtype), S((8, c), F32), S((8, c), F32)],
        scratch_shapes=[pltpu.VMEM((tt + HALO, tc), F32)],
        input_output_aliases={6: 0}, name=name,
        compiler_params=_cp(("parallel", "arbitrary")))(proj, w, cpre, cpre, dact, dact, dproj)


def _conv_ffn_fwd(hid, w, b, name):
    t, f2 = hid.shape
    f = f2 // 2
    kw = w.shape[0]
    tt, tc = _conv_tiles(t, f)
    nj = f // tc

    def body(g_ref, gp_ref, u_ref, up_ref, wg_ref, wu_ref, bg_ref, bu_ref, o_ref, ot_ref, cv_ref, sg_ref, su_ref):
        i = pl.program_id(1)
        _fill_prev_main(sg_ref, gp_ref, g_ref, i, tt)
        _fill_prev_main(su_ref, up_ref, u_ref, i, tt)

        def chunk(r0, rc, carry):
            rows = pl.ds(r0, rc)
            gate = _taps_chunk(sg_ref, wg_ref, kw, r0, rc) + bg_ref[...]
            up = _taps_chunk(su_ref, wu_ref, kw, r0, rc) + bu_ref[...]
            o_ref[rows, :] = (jax.nn.silu(gate) * up).astype(o_ref.dtype)
            cv_ref[0, rows, :] = gate.astype(cv_ref.dtype)
            cv_ref[1, rows, :] = up.astype(cv_ref.dtype)
            return carry

        _row_chunks(tt, chunk)
        ot_ref[...] = o_ref[...].T

    return pl.pallas_call(
        body, grid=(nj, t // tt),
        in_specs=[pl.BlockSpec((tt, tc), lambda j, i: (i, j)), _prev_spec(tt, tc, 0),
                  pl.BlockSpec((tt, tc), lambda j, i: (i, j + nj)), _prev_spec(tt, tc, nj),
                  pl.BlockSpec((kw, tc), lambda j, i: (0, j)), pl.BlockSpec((kw, tc), lambda j, i: (0, j + nj)),
                  pl.BlockSpec((1, tc), lambda j, i: (0, j)), pl.BlockSpec((1, tc), lambda j, i: (0, j + nj))],
        out_specs=[pl.BlockSpec((tt, tc), lambda j, i: (i, j)), pl.BlockSpec((tc, tt), lambda j, i: (j, i)),
                   pl.BlockSpec((2, tt, tc), lambda j, i: (0, i, j))],
        out_shape=[S((t, f), ACT_DTYPE), S((f, t), ACT_DTYPE), S((2, t, f), ACT_DTYPE)],
        scratch_shapes=[pltpu.VMEM((HALO + tt, tc), F32), pltpu.VMEM((HALO + tt, tc), F32)], name=name,
        compiler_params=_cp(("parallel", "arbitrary")))(hid, hid, hid, hid, w, w, b, b)


def _conv_ffn_bwd(hid, w, cv, dact, name):
    t, f2 = hid.shape
    f = f2 // 2
    kw = w.shape[0]
    tt, tc = _conv_tiles(t, f)
    nj = f // tc
    nt = t // tt

    def body(g_ref, u_ref, wg_ref, wu_ref, cv_ref, cvn_ref, d_ref, dn_ref, dh_ref, dw_ref, db_ref, dg_ref, du_ref):
        i = pl.program_id(1)

        @pl.when(i == 0)
        def _():
            dw_ref[...] = jnp.zeros_like(dw_ref)
            db_ref[...] = jnp.zeros_like(db_ref)

        def cotangents(gate, up, dact_v):
            sg = jax.nn.sigmoid(gate)
            return dact_v * up * (sg * (1.0 + gate * (1.0 - sg))), dact_v * (gate * sg)

        def stage(r0, rc, dbs):
            rows = pl.ds(r0, rc)
            dg, du = cotangents(cv_ref[0, rows, :].astype(F32), cv_ref[1, rows, :].astype(F32), d_ref[rows, :].astype(F32))
            dg_ref[rows, :] = dg
            du_ref[rows, :] = du
            return [dbs[0] + _fold8(dg), dbs[1] + _fold8(du)]

        zero8 = jnp.zeros((8, tc), F32)
        dbs = _row_chunks(tt, stage, [zero8, zero8])
        dgn, dun = cotangents(cvn_ref[0].astype(F32), cvn_ref[1].astype(F32), dn_ref[...].astype(F32))
        dg_ref[pl.ds(tt, HALO), :] = jnp.where(i < nt - 1, dgn, 0.0)
        du_ref[pl.ds(tt, HALO), :] = jnp.where(i < nt - 1, dun, 0.0)
        for s, (sd_ref, x_ref, w_ref) in enumerate(((dg_ref, g_ref, wg_ref), (du_ref, u_ref, wu_ref))):
            def chunk(r0, rc, dws, s=s, sd_ref=sd_ref, x_ref=x_ref, w_ref=w_ref):
                dx, dws = _conv_bwd_chunk(sd_ref, x_ref[pl.ds(r0, rc), :].astype(F32), w_ref, kw, r0, rc, dws)
                dh_ref[s, pl.ds(r0, rc), :] = dx.astype(dh_ref.dtype)
                return dws

            dws = _row_chunks(tt, chunk, [zero8] * kw)
            for k in range(kw):
                dw_ref[s, k:k + 1, :] += jnp.sum(dws[k], axis=0, keepdims=True)
            db_ref[s, 0:1, :] += jnp.sum(dbs[s], axis=0, keepdims=True)

    acc = pl.BlockSpec((2, 8, tc), lambda j, i: (0, 0, j))
    dsc = pltpu.VMEM((tt + HALO, tc), F32)
    nxt = lambda j, i: (0, jnp.minimum((i + 1) * (tt // HALO), t // HALO - 1), j)
    return pl.pallas_call(
        body, grid=(nj, nt),
        in_specs=[pl.BlockSpec((tt, tc), lambda j, i: (i, j)), pl.BlockSpec((tt, tc), lambda j, i: (i, j + nj)),
                  pl.BlockSpec((kw, tc), lambda j, i: (0, j)), pl.BlockSpec((kw, tc), lambda j, i: (0, j + nj)),
                  pl.BlockSpec((2, tt, tc), lambda j, i: (0, i, j)), pl.BlockSpec((2, HALO, tc), nxt),
                  pl.BlockSpec((tt, tc), lambda j, i: (i, j)), _next_spec(tt, tc, 0, t)],
        out_specs=[pl.BlockSpec((2, tt, tc), lambda j, i: (0, i, j)), acc, acc],
        out_shape=[S((2, t, f), MXU_DTYPE), S((2, 8, f), F32), S((2, 8, f), F32)],
        scratch_shapes=[dsc, dsc], name=name,
        compiler_params=_cp(("parallel", "arbitrary")))(hid, hid, w, w, cv, cv, dact, dact)


SSD_SUB = 2
SSD_FWD_BLOCKS = 2


def _ssd_chunk(xs, bm, cm, dtraw, hin, bias, alog, dskip):
    n = bm.shape[0]
    row = lax.broadcasted_iota(jnp.int32, (n, n), 0)
    col = lax.broadcasted_iota(jnp.int32, (n, n), 1)
    causal = row >= col
    lane = lax.broadcasted_iota(jnp.int32, (1, 128), 1)
    sub = lax.broadcasted_iota(jnp.int32, (128, 1), 0)
    last = (lax.broadcasted_iota(jnp.int32, (n, 1), 0) == n - 1).astype(F32)
    dt = jax.nn.softplus(dtraw + bias)
    adt = dt * (-jnp.exp(alog))
    tri = causal.astype(F32)
    acs = jnp.dot(tri, adt, preferred_element_type=F32, precision=lax.Precision.HIGHEST)
    ch = lax.broadcasted_iota(jnp.int32, (128, GROUP_CH), 1)
    hd = lax.broadcasted_iota(jnp.int32, (128, GROUP_CH), 0) * HEAD_DIM
    expand = ((ch >= hd) & (ch < hd + HEAD_DIM)).astype(F32)
    acs_x = jnp.dot(acs, expand, preferred_element_type=F32, precision=lax.Precision.HIGHEST)
    alast_x = jnp.sum(acs_x * last, axis=0, keepdims=True)
    acs_t = acs.T
    scores = lax.dot_general(_mx(cm), _mx(bm), (((1,), (1,)), ((), ())), preferred_element_type=F32)
    yds, xts = [], []
    for r in range(HEADS_PER_GROUP):
        pick = (lane == r).astype(F32)
        acol = jnp.sum(acs * pick, axis=1, keepdims=True)
        arow = jnp.sum(acs_t * (sub == r).astype(F32), axis=0, keepdims=True)
        dtc = jnp.sum(dt * pick, axis=1, keepdims=True)
        lm = jnp.exp(jnp.where(causal, acol - arow, -1e30))
        xts.append(xs[r] * dtc)
        yds.append(jnp.dot(_mx(scores * lm), _mx(xts[r]), preferred_element_type=F32))
    xt = jnp.concatenate(xts, axis=1)
    yo = jnp.exp(acs_x) * jnp.dot(_mx(cm), _mx(hin), preferred_element_type=F32)
    st = lax.dot_general(_mx(bm), _mx(xt * jnp.exp(alast_x - acs_x)), (((0,), (0,)), ((), ())), preferred_element_type=F32)
    hout = jnp.exp(alast_x) * hin + st
    return jnp.concatenate(yds, axis=1) + yo + dskip * jnp.concatenate(xs, axis=1), hout


def _ssd_post(y, z, normg):
    return _rms(y * jax.nn.silu(z), normg)


def _ssd_block(datas, hin, consts):
    ys = []
    for data in datas:
        y, hin = _ssd_chunk(*data, hin, *consts)
        ys.append(y)
    return ys, hin


def _ssd_specs(d_ssm, ngrp, zcol0, rev, nb, sub=SSD_SUB):
    cc = (lambda c: nb - 1 - c) if rev else (lambda c: c)
    rows = sub * CHUNK
    xj, bj, cj, zj = 0, d_ssm // 128, d_ssm // 128 + ngrp, zcol0 // GROUP_CH
    const = lambda w: pl.BlockSpec((None, 8, w), lambda g, c: (g, 0, 0))
    return cc, [
        pl.BlockSpec((rows, GROUP_CH), lambda g, c: (cc(c), xj + g)),
        pl.BlockSpec((rows, 128), lambda g, c: (cc(c), bj + g)),
        pl.BlockSpec((rows, 128), lambda g, c: (cc(c), cj + g)),
        pl.BlockSpec((rows, 128), lambda g, c: (cc(c), g)),
        pl.BlockSpec((rows, GROUP_CH), lambda g, c: (cc(c), zj + g)),
        const(128), const(128), const(GROUP_CH), const(GROUP_CH)]


def _sub_rows(s):
    return slice(CHUNK * s, CHUNK * (s + 1))


def _ssd_load(x_ref, b_ref, c_ref, dt_ref, z_ref, bias_ref, alog_ref, dsk_ref, ng_ref, sub=SSD_SUB):
    datas, zs = [], []
    for s in range(sub):
        rows = _sub_rows(s)
        xs = [x_ref[rows, HEAD_DIM * r:HEAD_DIM * (r + 1)].astype(F32) for r in range(HEADS_PER_GROUP)]
        datas.append((xs, b_ref[rows, :].astype(F32), c_ref[rows, :].astype(F32), dt_ref[rows, :]))
        zs.append(z_ref[rows, :].astype(F32))
    return datas, zs, (bias_ref[0:1, :], alog_ref[0:1, :], dsk_ref[0:1, :]), ng_ref[0:1, :]


def _ssd_fwd(xbc, dtp, proj, zcol0, bias_p, alog_p, dskip_x, normg_x, cat, cat_t, name):
    t = xbc.shape[0]
    ngrp = bias_p.shape[0]
    d_ssm = ngrp * GROUP_CH
    fb = SSD_FWD_BLOCKS if t % (SSD_FWD_BLOCKS * SSD_SUB * CHUNK) == 0 else 1
    sub = fb * SSD_SUB
    rows = sub * CHUNK
    nb = t // rows
    d_a = cat.shape[1] - d_ssm
    assert d_a % GROUP_CH == 0 and zcol0 % GROUP_CH == 0 and t % rows == 0
    _, specs = _ssd_specs(d_ssm, ngrp, zcol0, False, nb, sub)

    def body(x_ref, b_ref, c_ref, dt_ref, z_ref, bias_ref, alog_ref, dsk_ref, ng_ref, cat_in, catt_in,
             yn_ref, ynt_ref, y_ref, hs_ref, h_ref):
        del cat_in, catt_in

        @pl.when(pl.program_id(1) == 0)
        def _():
            h_ref[...] = jnp.zeros_like(h_ref)

        datas, zs, consts, normg = _ssd_load(x_ref, b_ref, c_ref, dt_ref, z_ref, bias_ref, alog_ref, dsk_ref, ng_ref, sub)
        h, ys = h_ref[...], []
        for k in range(fb):
            hs_ref[k] = h
            ys_k, h = _ssd_block(datas[k * SSD_SUB:(k + 1) * SSD_SUB], h, consts)
            ys += ys_k
        for s in range(sub):
            y_ref[_sub_rows(s), :] = ys[s].astype(y_ref.dtype)
            yn = _ssd_post(ys[s], zs[s], normg).astype(yn_ref.dtype)
            yn_ref[_sub_rows(s), :] = yn
            ynt_ref[:, _sub_rows(s)] = yn.T
        h_ref[...] = h

    hbm = pl.BlockSpec(memory_space=pl.ANY)
    return pl.pallas_call(
        body, grid=(ngrp, nb), in_specs=specs + [hbm, hbm],
        out_specs=[pl.BlockSpec((rows, GROUP_CH), lambda g, c: (c, d_a // GROUP_CH + g)),
                   pl.BlockSpec((GROUP_CH, rows), lambda g, c: (d_a // GROUP_CH + g, c)),
                   pl.BlockSpec((rows, GROUP_CH), lambda g, c: (c, g)),
                   pl.BlockSpec((fb, None, D_STATE, GROUP_CH), lambda g, c: (c, g, 0, 0))],
        out_shape=[S(cat.shape, cat.dtype), S(cat_t.shape, cat_t.dtype), S((t, d_ssm), ACT_DTYPE),
                   S((nb * fb, ngrp, D_STATE, GROUP_CH), F32)],
        scratch_shapes=[pltpu.VMEM((D_STATE, GROUP_CH), F32)],
        input_output_aliases={9: 0, 10: 1}, name=name,
        compiler_params=_cp(("parallel", "arbitrary")))(xbc, xbc, xbc, dtp, proj, bias_p, alog_p, dskip_x, normg_x, cat, cat_t)


def _ssd_bwd(xbc, dtp, proj, zcol0, bias_p, alog_p, dskip_x, normg_x, hs, ypre, dcat, dproj, name):
    t = xbc.shape[0]
    ngrp = bias_p.shape[0]
    d_ssm = ngrp * GROUP_CH
    rows = SSD_SUB * CHUNK
    nb = t // rows
    d_a = dcat.shape[1] - d_ssm
    cc, specs = _ssd_specs(d_ssm, ngrp, zcol0, True, nb)

    def body(x_ref, b_ref, c_ref, dt_ref, z_ref, bias_ref, alog_ref, dsk_ref, ng_ref, hs_ref, yp_ref, dy_ref, dp_in,
             dz_ref, dx_ref, db_ref, dc_ref, ddt_ref, dbias_ref, dalog_ref, ddsk_ref, dng_ref, dh_ref):
        del dp_in

        @pl.when(pl.program_id(1) == 0)
        def _():
            dh_ref[...] = jnp.zeros_like(dh_ref)
            for r in (dbias_ref, dalog_ref, ddsk_ref, dng_ref):
                r[...] = jnp.zeros_like(r)

        datas, zs, consts, normg = _ssd_load(x_ref, b_ref, c_ref, dt_ref, z_ref, bias_ref, alog_ref, dsk_ref, ng_ref)
        dys, dng = [], jnp.zeros_like(normg)
        for s in range(SSD_SUB):
            rws = _sub_rows(s)
            _, vjp_post = jax.vjp(_ssd_post, yp_ref[rws, :].astype(F32), zs[s], normg)
            dy, dz, dg = vjp_post(dy_ref[rws, :].astype(F32))
            dys.append(dy)
            dng = dng + dg
            dz_ref[rws, :] = dz.astype(dz_ref.dtype)
        _, vjp = jax.vjp(_ssd_block, datas, hs_ref[...], consts)
        ddatas, dhin, (dbias, dalog, ddsk) = vjp((dys, dh_ref[...]))
        for s, (dxs, dbm, dcm, ddt) in enumerate(ddatas):
            rws = _sub_rows(s)
            for r in range(HEADS_PER_GROUP):
                dx_ref[rws, HEAD_DIM * r:HEAD_DIM * (r + 1)] = dxs[r].astype(dx_ref.dtype)
            db_ref[rws, :] = dbm.astype(db_ref.dtype)
            dc_ref[rws, :] = dcm.astype(dc_ref.dtype)
            ddt_ref[rws, :] = ddt
        dh_ref[...] = dhin
        dbias_ref[0:1, :] += dbias
        dalog_ref[0:1, :] += dalog
        ddsk_ref[0:1, :] += ddsk
        dng_ref[0:1, :] += dng

    acc = lambda w: pl.BlockSpec((None, 8, w), lambda g, c: (g, 0, 0))
    blk = lambda w: pl.BlockSpec((rows, w), lambda g, c: (cc(c), g))
    return pl.pallas_call(
        body, grid=(ngrp, nb),
        in_specs=specs + [pl.BlockSpec((None, None, D_STATE, GROUP_CH), lambda g, c: (cc(c), g, 0, 0)),
                          blk(GROUP_CH),
                          pl.BlockSpec((rows, GROUP_CH), lambda g, c: (cc(c), d_a // GROUP_CH + g)),
                          pl.BlockSpec(memory_space=pl.ANY)],
        out_specs=[pl.BlockSpec((rows, GROUP_CH), lambda g, c: (cc(c), zcol0 // GROUP_CH + g)),
                   blk(GROUP_CH), blk(128), blk(128), blk(128), acc(128), acc(128), acc(GROUP_CH), acc(GROUP_CH)],
        out_shape=[S(dproj.shape, dproj.dtype), S((t, d_ssm), XBC_DTYPE), S((t, ngrp * 128), XBC_DTYPE),
                   S((t, ngrp * 128), XBC_DTYPE), S((t, ngrp * 128), F32), S((ngrp, 8, 128), F32),
                   S((ngrp, 8, 128), F32), S((ngrp, 8, GROUP_CH), F32), S((ngrp, 8, GROUP_CH), F32)],
        scratch_shapes=[pltpu.VMEM((D_STATE, GROUP_CH), F32)],
        input_output_aliases={12: 0}, name=name,
        compiler_params=_cp(("parallel", "arbitrary")))(xbc, xbc, xbc, dtp, proj, bias_p, alog_p, dskip_x, normg_x, hs, ypre,
                                                        dcat, dproj)


def _adamw(parts, w, m, v, after, name):
    r, c = w.shape
    tr = _pick(r, (256, 128, 64, 32, 16, 8)) if c * 4 * 256 <= 4 * 1024 * 1024 else _pick(r, (64, 32, 16, 8))

    def body(p_ref, w_ref, m_ref, v_ref, after_ref, g_ref, d_ref, nm_ref, nv_ref):
        del after_ref
        g = p_ref[0].astype(F32)
        for k in range(1, N_DEV):
            g = g + p_ref[k].astype(F32)
        mm = ADAM_B1 * m_ref[...] + (1.0 - ADAM_B1) * g
        vv = ADAM_B2 * v_ref[...] + (1.0 - ADAM_B2) * jnp.square(g)
        m_hat = mm / (1.0 - ADAM_B1 ** ADAM_STEP)
        v_hat = vv / (1.0 - ADAM_B2 ** ADAM_STEP)
        g_ref[...] = g
        d_ref[...] = -ADAM_LR * (m_hat / (jnp.sqrt(v_hat) + ADAM_EPS) + ADAM_WD * w_ref[...])
        nm_ref[...] = mm
        nv_ref[...] = vv

    blk = pl.BlockSpec((tr, c), lambda i: (i, 0))
    return pl.pallas_call(
        body, grid=(r // tr,),
        in_specs=[pl.BlockSpec((N_DEV, tr, c), lambda i: (0, i, 0)), blk, blk, blk, pl.BlockSpec(memory_space=pl.ANY)],
        out_specs=[blk, blk, blk, blk], out_shape=[S((r, c), F32)] * 4, name=name,
        compiler_params=_cp(("parallel",)))(parts, w, m, v, after)


def _mesh_pos():
    return lax.axis_index("x"), lax.axis_index("y"), lax.axis_index("c")


def _peer(d, x, y, c):
    return (1 - x if (d >> 2) & 1 else x, 1 - y if (d >> 1) & 1 else y, 1 - c if d & 1 else c)


def _gather_two_level(blocks, name):
    n = len(blocks)

    def body(*refs):
        srcs, outs = refs[:n], refs[n:2 * n]
        send_sems, recv_sems, loc_sems = refs[2 * n:]
        x, y, c = _mesh_pos()
        lin = lambda px, py, pc: 4 * px + 2 * py + pc
        me, sibling = (x, y, c), (x, y, 1 - c)
        chips = [(1 - x, y), (x, 1 - y), (1 - x, 1 - y)]

        def copy(a, k, block, to, src=None):
            slab = outs[a].at[lin(*block)]
            return pltpu.make_async_remote_copy(
                src_ref=slab if src is None else src, dst_ref=slab, send_sem=send_sems.at[a, k],
                recv_sem=recv_sems.at[a, k], device_id=to, device_id_type=pl.DeviceIdType.MESH)

        mine = [pltpu.make_async_copy(srcs[a], outs[a].at[lin(*me)], loc_sems.at[a]) for a in range(n)]
        first = [copy(a, 0, me, sibling, src=srcs[a]) for a in range(n)]
        first += [copy(a, 1 + j, me, (*chip, c), src=srcs[a]) for j, chip in enumerate(chips) for a in range(n)]
        for cp in mine + first:
            cp.start()
        passed = []
        for j, chip in enumerate(chips):
            for a in range(n):
                copy(a, 1 + j, (*chip, c), me).wait_recv()
                passed.append(copy(a, 4 + j, (*chip, c), sibling))
                passed[-1].start()
        for a in range(n):
            copy(a, 0, sibling, me).wait_recv()
            for j, chip in enumerate(chips):
                copy(a, 4 + j, (*chip, 1 - c), me).wait_recv()
        for cp in first + passed:
            cp.wait_send()
        for cp in mine:
            cp.wait()

    hbm = pl.BlockSpec(memory_space=pl.ANY)
    return pl.pallas_call(
        body, in_specs=[hbm] * n, out_specs=[hbm] * n, out_shape=[S((N_DEV,) + b.shape, b.dtype) for b in blocks],
        scratch_shapes=[pltpu.SemaphoreType.DMA((n, N_DEV - 1)), pltpu.SemaphoreType.DMA((n, N_DEV - 1)),
                        pltpu.SemaphoreType.DMA((n,))],
        name=name, compiler_params=pltpu.CompilerParams(has_side_effects=True))(*blocks)


def _xfer_start(items, name, after=None):
    n = len(items)
    kinds = [k for k, _ in items]
    srcs = [pltpu.with_memory_space_constraint(a, pltpu.HBM) for _, a in items]
    land_shapes = [((N_DEV,) + a.shape if k == 'gather' else a.shape, a.dtype) for k, a in items]
    lands = [pltpu.with_memory_space_constraint(lax.empty(s, dt), pltpu.HBM) for s, dt in land_shapes]
    extra = [] if after is None else [after]

    def body(*refs):
        src_refs, land_refs = refs[:n], refs[n:2 * n]
        outs = refs[2 * n + len(extra):]
        sems = outs[:2 * n]
        token = outs[4 * n]
        x, y, c = _mesh_pos()
        me = 4 * x + 2 * y + c
        for a in range(n):
            for d in range(1, N_DEV):
                px, py, pc = _peer(d, x, y, c)
                src = src_refs[a] if kinds[a] == 'gather' else src_refs[a].at[4 * px + 2 * py + pc]
                pltpu.make_async_remote_copy(
                    src_ref=src, dst_ref=land_refs[a].at[me], send_sem=sems[2 * a].at[d - 1],
                    recv_sem=sems[2 * a + 1].at[d - 1], device_id=(px, py, pc),
                    device_id_type=pl.DeviceIdType.MESH).start()
        token[...] = jnp.zeros_like(token)

    hbm = pl.BlockSpec(memory_space=pltpu.HBM)
    sem = pl.BlockSpec(memory_space=pltpu.SEMAPHORE)
    out_shape = ([pltpu.SemaphoreType.DMA((N_DEV - 1,))] * (2 * n)
                 + [pltpu.HBM(a.shape, a.dtype) for a in srcs] + [pltpu.HBM(s, dt) for s, dt in land_shapes]
                 + [S((8, 128), F32)])
    res = pl.pallas_call(
        body, name=name, out_shape=out_shape,
        in_specs=[hbm] * (2 * n) + [pl.BlockSpec(memory_space=pl.ANY)] * len(extra),
        out_specs=[sem] * (2 * n) + [hbm] * (2 * n) + [pl.BlockSpec(memory_space=pltpu.VMEM)],
        input_output_aliases={**{a: 2 * n + a for a in range(n)}, **{n + a: 3 * n + a for a in range(n)}},
        compiler_params=pltpu.CompilerParams(has_side_effects=pltpu.SideEffectType.DATAFLOW_SIDE_EFFECTING),
    )(*srcs, *lands, *extra)
    return (kinds, res[:2 * n], res[2 * n:3 * n], res[3 * n:4 * n]), res[4 * n]


def _xfer_wait(handle, after, name):
    kinds, sems, src_thru, land_thru = handle
    n = len(kinds)

    def body(*refs):
        land_refs = refs[n:2 * n]
        sem_refs = refs[2 * n:4 * n]
        x, y, c = _mesh_pos()
        me = 4 * x + 2 * y + c
        for a in range(n):
            for d in range(1, N_DEV):
                slab = land_refs[a].at[me]
                cp = pltpu.make_async_remote_copy(
                    src_ref=slab, dst_ref=slab, send_sem=sem_refs[2 * a].at[d - 1], recv_sem=sem_refs[2 * a + 1].at[d - 1],
                    device_id=_peer(d, x, y, c), device_id_type=pl.DeviceIdType.MESH)
                cp.wait_send()
                cp.wait_recv()

    hbm = pl.BlockSpec(memory_space=pltpu.HBM)
    sem = pl.BlockSpec(memory_space=pltpu.SEMAPHORE)
    res = pl.pallas_call(
        body, name=name,
        out_shape=[pltpu.HBM(a.shape, a.dtype) for a in src_thru] + [pltpu.HBM(a.shape, a.dtype) for a in land_thru],
        in_specs=[hbm] * (2 * n) + [sem] * (2 * n) + [pl.BlockSpec(memory_space=pl.ANY)],
        out_specs=[hbm] * (2 * n), input_output_aliases={a: a for a in range(2 * n)},
        compiler_params=pltpu.CompilerParams(has_side_effects=pltpu.SideEffectType.DATAFLOW_SIDE_EFFECTING),
    )(*src_thru, *land_thru, *sems, after)
    x, y, c = _mesh_pos()
    me = 4 * x + 2 * y + c
    out = []
    for a in range(n):
        src = res[a]
        own = src[None] if kinds[a] == 'gather' else lax.dynamic_index_in_dim(src, me, 0, keepdims=True)
        out.append(lax.dynamic_update_index_in_dim(res[n + a], own, me, 0))
    return out


def _stack_to_full(kind, st):
    if kind == 'row':
        return st.reshape(st.shape[0] * st.shape[1], st.shape[2])
    return jnp.concatenate([st[k] for k in range(st.shape[0])], axis=1)


def _full_to_stack(kind, full):
    r, c = full.shape
    if kind == 'row':
        return full.reshape(N_DEV, r // N_DEV, c)
    w = c // N_DEV
    return jnp.stack([full[:, k * w:(k + 1) * w] for k in range(N_DEV)], axis=0)


SMALL_ROWS = 256


def _pack_small(named):
    layout = [(a.shape, a.size, -(-a.size // 1024) * 8) for a in named]
    total = -(-sum(nr for _, _, nr in layout) // SMALL_ROWS) * SMALL_ROWS * 128
    packed, off = None, 0
    for a, (_, n, nr) in zip(named, layout):
        part = jnp.pad(a.reshape(-1).astype(F32), (off, total - off - n))
        packed = part if packed is None else packed + part
        off += nr * 128
    return packed.reshape(total // 128, 128), layout


def _unpack_small(packed, layout):
    out, r0 = [], 0
    for shape, n, nr in layout:
        out.append(packed[r0:r0 + nr].reshape(-1)[:n].reshape(shape))
        r0 += nr
    return out


def _row0(acc):
    return acc[0]


def _local_step(x, p, tgt, sm, comm):
    t, d = x.shape
    h_n = sm['dt_bias'].shape[-1]
    ngrp = h_n // HEADS_PER_GROUP
    d_ssm = h_n * HEAD_DIM
    d_a = sm['ln_a_g'].shape[-1]
    d_mix = d_a + d_ssm
    d_xbc = sm['conv_ssm_b'].shape[-1]
    d_main = 2 * d_a + d_ssm + d_xbc
    assert d_xbc == d_ssm + 2 * ngrp * D_STATE and h_n <= 128
    zcol0, xcol0 = 2 * d_a, 2 * d_a + d_ssm
    vec = lambda v: v.reshape(1, -1)

    bst = jnp.pad(sm['b_s'].T, ((0, 0), (0, 128 - sm['b_s'].shape[0])))
    grp = lambda v, w: jnp.broadcast_to(jnp.pad(v.reshape(ngrp, 1, -1), ((0, 0), (0, 0), (0, w - v.size // ngrp))), (ngrp, 8, w))
    bias_p, alog_p = grp(sm['dt_bias'], 128), grp(sm['a_log'], 128)
    dskip_x = grp(jnp.repeat(sm['d_skip'], HEAD_DIM), GROUP_CH)
    normg_x = grp(sm['ssm_norm_g'], GROUP_CH)
    pad_dt = lambda v: jnp.pad(v[:, :h_n].reshape(t, ngrp, HEADS_PER_GROUP),
                               ((0, 0), (0, 0), (0, 128 - HEADS_PER_GROUP))).reshape(t, ngrp * 128)

    g_mix = vec(sm['norm_mix_g']) + comm.tok0
    a_n, a_t = _rms_fwd(x, g_mix, "rms_mix")
    wf = comm.weights('a', a_n)
    slabs = [wf['w_in'][k] for k in range(N_DEV)]
    w_main = jnp.concatenate(slabs[:-1] + [slabs[-1][:, :slabs[-1].shape[1] - h_n]], axis=1)
    w_dt = jnp.pad(slabs[-1][:, slabs[-1].shape[1] - h_n:], ((0, 0), (0, 128 - h_n)))
    proj = _mm_nn(a_n, w_main, out_dtype=ACT_DTYPE, name="mm_in")
    dtp = pad_dt(_mm_nn(a_n, w_dt, out_dtype=F32, name="mm_dt"))
    cat, cat_t = _gmlp_fwd(proj, vec(sm['ln_a_g']), vec(sm['ln_a_b']), sm['w_s'], bst, vec(sm['norm_a_g']), d_mix, "gmlp_fwd")
    xbc, cpre = _conv_ssm_fwd(proj, xcol0, wf['conv_ssm_w'], vec(sm['conv_ssm_b']), "conv_ssm_fwd")
    cat, cat_t, ypre, hs = _ssd_fwd(xbc, dtp, proj, zcol0, bias_p, alog_p, dskip_x, normg_x, cat, cat_t, "ssd_fwd")
    wf.update(comm.weights('b', hs))
    h1 = _mm_nn(cat, wf['w_out'], out_dtype=F32, name="mm_out", res=x)
    f_n, f_t = _rms_fwd(h1, vec(sm['norm_ffn_g']), "rms_ffn")
    hid = _mm_nn(f_n, wf['w_up'], out_dtype=ACT_DTYPE, name="mm_up")
    act, act_t, cv = _conv_ffn_fwd(hid, wf['conv_ffn_w'], vec(sm['conv_ffn_b']), "conv_ffn_fwd")
    h2 = _mm_nn(act, wf['w_down'], out_dtype=F32, name="mm_down", res=h1)
    r_n, r_t = _rms_fwd(h2, vec(sm['norm_ple_g']), "rms_ple")
    q = _mm_nn(r_n, wf['w_ple_gate'], out_dtype=ACT_DTYPE, name="mm_pg")
    p_m = p.astype(MXU_DTYPE)
    pe = _mm_nn(p_m, wf['w_ple'], out_dtype=ACT_DTYPE, name="mm_ple")

    loss, dh3, dq, dpe, dgf = _head(h2, q, pe, tgt, vec(sm['norm_final_g']), "head")
    wgrad = lambda act_t, g, name, **kw: _mm_nn(act_t, g, out_dtype=WIRE_DTYPE, name=name, wide=True, **kw)
    gs = {}
    gs['norm_final_g'] = _row0(dgf)
    g_ple = wgrad(p_m.T, dpe, "wg_ple")
    g_pg = wgrad(r_t, dq, "wg_pg")
    dr = _mm_nt(dq, wf['w_ple_gate'], out_dtype=ACT_DTYPE, name="dg_pg")
    dh2m, dg = _rms_bwd(h2, vec(sm['norm_ple_g']), dr, dh3, "rms_ple_bwd", DRES_DTYPE)
    gs['norm_ple_g'] = _row0(dg)
    g_down = wgrad(act_t, dh2m, "wg_down")
    tok = comm.send('1', {'w_ple': g_ple, 'w_ple_gate': g_pg, 'w_down': g_down})
    dact = _mm_nt(dh2m, wf['w_down'], out_dtype=ACT_DTYPE, name="dg_down")
    dhid, dcw, dcb = _conv_ffn_bwd(hid, wf['conv_ffn_w'] + tok, cv, dact, "conv_ffn_bwd")
    kf = wf['conv_ffn_w'].shape[0]
    g_cf = jnp.concatenate([dcw[0, :kf], dcw[1, :kf]], axis=1)
    gs['conv_ffn_b'] = jnp.concatenate([dcb[0, 0], dcb[1, 0]], axis=0)
    g_up = wgrad(f_t, dhid, "wg_up", b_split=2, out_slabs=N_DEV)
    df = _mm_nt(dhid, wf['w_up'], out_dtype=ACT_DTYPE, name="dg_up", a_split=2)
    dh1m, dg = _rms_bwd(h1, vec(sm['norm_ffn_g']), df, dh2m, "rms_ffn_bwd", DRES_DTYPE)
    gs['norm_ffn_g'] = _row0(dg)
    g_out = wgrad(cat_t, dh1m, "wg_out")
    tok = comm.send('2', {'conv_ffn_w': g_cf, 'w_up': g_up, 'w_out': g_out}, stacked=('w_up',))
    dcat = _mm_nt(dh1m, wf['w_out'], out_dtype=ACT_DTYPE, name="dg_out")
    dproj, dlng, dlnb, dws, dbst, dng = _gmlp_bwd(proj, vec(sm['ln_a_g']) + tok, vec(sm['ln_a_b']), sm['w_s'], bst,
                                                  vec(sm['norm_a_g']), dcat, d_main, "gmlp_bwd")
    gs['ln_a_g'], gs['ln_a_b'], gs['w_s'], gs['norm_a_g'] = _row0(dlng), _row0(dlnb), dws, _row0(dng)
    gs['b_s'] = dbst[:, :sm['b_s'].shape[0]].T
    dproj, dxs, dbm, dcm, ddtp, dbias, dalog, ddsk, dsng = _ssd_bwd(
        xbc, dtp, proj, zcol0, bias_p, alog_p, dskip_x, normg_x, hs, ypre, dcat, dproj, "ssd_bwd")
    gs['dt_bias'] = dbias[:, 0, :HEADS_PER_GROUP].reshape(h_n)
    gs['a_log'] = dalog[:, 0, :HEADS_PER_GROUP].reshape(h_n)
    gs['d_skip'] = ddsk[:, 0, :].reshape(h_n, HEAD_DIM).sum(axis=-1)
    gs['ssm_norm_g'] = dsng[:, 0, :].reshape(d_ssm)
    dws_c, dbs_c = [], []
    off = 0
    for nm, dpart in (("x", dxs), ("b", dbm), ("c", dcm)):
        dproj, dw_c, db_c = _conv_ssm_bwd(proj, xcol0 + off, wf['conv_ssm_w'], cpre, off, dpart, dproj,
                                          xcol0 + off, "conv_ssm_bwd_" + nm)
        dws_c.append(dw_c[:wf['conv_ssm_w'].shape[0]])
        dbs_c.append(db_c[0])
        off += dpart.shape[1]
    g_cs = jnp.concatenate(dws_c, axis=1)
    gs['conv_ssm_b'] = jnp.concatenate(dbs_c, axis=0)
    ddt = jnp.pad(ddtp.reshape(t, ngrp, 128)[:, :, :HEADS_PER_GROUP].reshape(t, h_n), ((0, 0), (0, 128 - h_n))).astype(MXU_DTYPE)
    g_in = jnp.concatenate([wgrad(a_t, dproj, "wg_in"), wgrad(a_t, ddt, "wg_dt")[:, :h_n]], axis=1)
    tok = comm.send('3', {'conv_ssm_w': g_cs, 'w_in': g_in}, [(n, gs[n]) for n in REPLICATED if n != 'norm_mix_g'])
    da = _mm_nt(ddt + tok.astype(ddt.dtype), w_dt, out_dtype=F32, name="dg_dt")
    da = _mm_nt(dproj, w_main, out_dtype=ACT_DTYPE, name="dg_in", res=da)
    dx, dg = _rms_bwd(x, g_mix + tok, da, dh1m, "rms_mix_bwd", F32)
    return dx, comm.send('4', {}, [('norm_mix_g', _row0(dg)), ('loss', loss[0, 0:1])])


def kernel(x, p, norm_mix_g, w_in, ln_a_g, ln_a_b, w_s, b_s, norm_a_g, conv_ssm_w, conv_ssm_b, dt_bias, a_log, d_skip, ssm_norm_g, w_out, norm_ffn_g, w_up, conv_ffn_w, conv_ffn_b, w_down, norm_ple_g, w_ple_gate, w_ple, norm_final_g, loss_target, m_norm_mix_g, m_w_in, m_ln_a_g, m_ln_a_b, m_w_s, m_b_s, m_norm_a_g, m_conv_ssm_w, m_conv_ssm_b, m_dt_bias, m_a_log, m_d_skip, m_ssm_norm_g, m_w_out, m_norm_ffn_g, m_w_up, m_conv_ffn_w, m_conv_ffn_b, m_w_down, m_norm_ple_g, m_w_ple_gate, m_w_ple, m_norm_final_g, v_norm_mix_g, v_w_in, v_ln_a_g, v_ln_a_b, v_w_s, v_b_s, v_norm_a_g, v_conv_ssm_w, v_conv_ssm_b, v_dt_bias, v_a_log, v_d_skip, v_ssm_norm_g, v_w_out, v_norm_ffn_g, v_w_up, v_conv_ffn_w, v_conv_ffn_b, v_w_down, v_norm_ple_g, v_w_ple_gate, v_w_ple, v_norm_final_g):
    given = dict(locals())
    wts = {n: given[n] for n in WEIGHTS}
    ms = {n: given["m_" + n] for n in WEIGHTS}
    vs = {n: given["v_" + n] for n in WEIGHTS}
    sm = {n: (wts[n][0] if wts[n].ndim > 1 else wts[n]) for n in REPLICATED}
    comm = _Comm({n: wts[n][0] for n in SHARDED})
    dx, _ = _local_step(x[0], p[0, 0], loss_target[0], sm, comm)

    out, loss_out, after = {}, None, comm.last_token
    for tag, names, small_names, layout, handle in comm.sent:
        recv = _xfer_wait(handle, after, "grads_%s_wait" % tag)
        for n, parts in zip(names, recv):
            out[n] = _adamw(parts, wts[n][0], ms[n][0], vs[n][0], after, "adamw_" + n)
            after = out[n][1]
        if small_names:
            pick = lambda src, fill: _pack_small([src[n] if n in src else jnp.full((1,), fill, F32) for n in small_names])[0]
            res = _adamw(recv[-1], pick(wts, 0.0), pick(ms, 0.0), pick(vs, 1.0), after, "adamw_small_" + tag)
            after = res[1]
            res = [_unpack_small(o, layout) for o in res]
            for i, n in enumerate(small_names):
                if n == 'loss':
                    loss_out = res[0][i].reshape(())
                else:
                    out[n] = [res[k][i] for k in range(4)]
    return (loss_out, dx[None], *[out[n][k].reshape(wts[n].shape) for k in range(4) for n in WEIGHTS])


class _Comm:
    GATHER_GROUPS = {'a': ('w_in', 'conv_ssm_w'), 'b': ('w_out', 'w_up', 'conv_ffn_w', 'w_down', 'w_ple_gate', 'w_ple')}

    def __init__(self, blocks):
        wired = lambda grp: [blocks[n].astype(_wire(n)) for n in self.GATHER_GROUPS[grp]]
        self.stacks_a = _gather_two_level(wired('a'), "gather_a")
        self.handle_b, tok = _xfer_start([('gather', b) for b in wired('b')], "gather_b_start", after=self.stacks_a[0])
        self.tok0 = tok[0, 0]
        self.sent = []

    def weights(self, grp, after):
        stacks = self.stacks_a if grp == 'a' else _xfer_wait(self.handle_b, after, "gather_b_wait")
        return {n: st if n == 'w_in' else _stack_to_full(SHARDED[n], st) for n, st in zip(self.GATHER_GROUPS[grp], stacks)}

    def send(self, tag, gw, small=None, stacked=()):
        items = [('scatter', g if n in stacked else _full_to_stack(SHARDED[n], g.astype(_wire(n)))) for n, g in gw.items()]
        layout, small_names = None, []
        if small:
            packed, layout = _pack_small([a for _, a in small])
            small_names = [n for n, _ in small]
            items.append(('gather', packed))
        handle, self.last_token = _xfer_start(items, "grads_%s_start" % tag)
        self.sent.append((tag, list(gw), small_names, layout, handle))
        return self.last_token[0, 0]


def _wire(name):
    return F32 if name in F32_ON_WIRE else WIRE_DTYPE
```

```python
import jax
import jax.numpy as jnp
from jax import lax
from jax.experimental import pallas as pl
from jax.experimental.pallas import tpu as pltpu

F32 = jnp.float32
MXU_DTYPE = jnp.bfloat16
ACT_DTYPE = jnp.bfloat16
XBC_DTYPE = jnp.bfloat16
WIRE_DTYPE = jnp.bfloat16
DRES_DTYPE = jnp.bfloat16
EPS = 1e-6
CHUNK = 128
D_STATE = 128
HEAD_DIM = 64
HEADS_PER_GROUP = 4
GROUP_CH = HEAD_DIM * HEADS_PER_GROUP
HALO = 16
N_DEV = 8
VMEM_LIMIT = 56 * 1024 * 1024

ADAM_LR = 0.001
ADAM_B1 = 0.9
ADAM_B2 = 0.999
ADAM_EPS = 1e-08
ADAM_WD = 0.01
ADAM_STEP = 10

WEIGHTS = ['norm_mix_g', 'w_in', 'ln_a_g', 'ln_a_b', 'w_s', 'b_s', 'norm_a_g', 'conv_ssm_w', 'conv_ssm_b', 'dt_bias',
           'a_log', 'd_skip', 'ssm_norm_g', 'w_out', 'norm_ffn_g', 'w_up', 'conv_ffn_w', 'conv_ffn_b', 'w_down',
           'norm_ple_g', 'w_ple_gate', 'w_ple', 'norm_final_g']
SHARDED = {'w_in': 'col', 'conv_ssm_w': 'col', 'w_out': 'row', 'w_up': 'col', 'conv_ffn_w': 'col', 'w_down': 'row',
           'w_ple_gate': 'row', 'w_ple': 'col'}
F32_ON_WIRE = ('conv_ssm_w', 'conv_ffn_w')
REPLICATED = [n for n in WEIGHTS if n not in SHARDED]

S = jax.ShapeDtypeStruct


def _pick(dim, cands):
    for c in cands:
        if c <= dim and dim % c == 0:
            return c
    return dim


def _cp(sem, vmem=VMEM_LIMIT):
    return pltpu.CompilerParams(dimension_semantics=sem, vmem_limit_bytes=vmem)


def _mx(v):
    return v.astype(MXU_DTYPE)


def _rms(v, g):
    return v * lax.rsqrt(jnp.mean(v * v, axis=-1, keepdims=True) + EPS) * g


MM_VMEM_BUDGET = 42 * 1024 * 1024


def _mm_tiles(m, n, k, out_bytes, has_res, tn_cands=(512, 256, 128), k_mult=1, tm_cands=(1024, 512), tn_alts=2):
    tns = [c for c in tn_cands if c <= n and n % c == 0][:tn_alts] or [n]
    ks = k // k_mult
    best = None
    for tn in tns:
        for tm in [c for c in tm_cands if m % c == 0] or [_pick(m, (256, 128))]:
            for nk in range(1, ks // 128 + 1):
                if ks % nk or (ks // nk) % 128:
                    continue
                tk = ks // nk
                need = 2 * 2 * (tm * tk + tk * tn) + tm * tn * (4 + 2 * out_bytes + (8 if has_res else 0))
                if need <= MM_VMEM_BUDGET:
                    if best is None or (nk, -tm, -tn) < best[0]:
                        best = ((nk, -tm, -tn), (tm, tn, tk))
                    break
    return best[1] if best else (_pick(m, (512, 256, 128)), tns[0], _pick(ks, (128,)))


def _mm_body(dot, nk, has_res):
    def body(*refs):
        if has_res:
            a_ref, b_ref, r_ref, o_ref, acc_ref = refs
        else:
            a_ref, b_ref, o_ref, acc_ref = refs
            r_ref = None
        kk = pl.program_id(2)
        d = dot(a_ref[...], b_ref[...])

        def fin(acc):
            if r_ref is not None:
                acc = acc + r_ref[...]
            o_ref[...] = acc.astype(o_ref.dtype)

        if nk == 1:
            fin(d)
        else:
            @pl.when(kk == 0)
            def _():
                acc_ref[...] = d

            if nk > 2:
                @pl.when((kk > 0) & (kk < nk - 1))
                def _():
                    acc_ref[...] += d

            @pl.when(kk == nk - 1)
            def _():
                fin(acc_ref[...] + d)

    return body


def _mm_call(body, grid, a_spec, b_spec, tm, tn, m, n, out_dtype, name, args, res, out_slabs=1):
    in_specs = [a_spec, b_spec]
    if res is not None:
        in_specs.append(pl.BlockSpec((tm, tn), lambda i, j, kk: (i, j)))
        args = args + [res]
    if out_slabs == 1:
        out_spec, out_shape = pl.BlockSpec((tm, tn), lambda i, j, kk: (i, j)), S((m, n), out_dtype)
    else:
        out_spec, out_shape = pl.BlockSpec((None, tm, tn), lambda i, j, kk: (j, i, 0)), S((out_slabs, m, tn), out_dtype)
    return pl.pallas_call(
        body, grid=grid, in_specs=in_specs, out_specs=out_spec, out_shape=out_shape,
        scratch_shapes=[pltpu.VMEM((tm, tn), F32)], name=name,
        compiler_params=_cp(("parallel", "parallel", "arbitrary")))(*args)


def _mm_nn(a, b, *, out_dtype, name, res=None, b_split=1, wide=False, out_slabs=1):
    m, k = a.shape
    n = b.shape[1] if b_split == 1 else b.shape[2] * b_split
    tn_cands = (n // out_slabs,) if out_slabs > 1 else (1024, 512, 256, 128) if wide else (512, 256, 128)
    tm, tn, tk = _mm_tiles(m, n // b_split, k, jnp.dtype(out_dtype).itemsize, res is not None, tn_cands=tn_cands,
                           tn_alts=1 if wide or out_slabs > 1 else 2,
                           tm_cands=(1024, 512) if wide or out_slabs > 1 else (2048, 1024, 512))
    assert out_slabs == 1 or (tn * out_slabs == n and tn % 128 == 0)
    nk = k // tk
    njs = (n // b_split) // tn
    body = _mm_body(lambda x, y: jnp.dot(x, y, preferred_element_type=F32), nk, res is not None)
    if b_split == 1:
        b_spec = pl.BlockSpec((tk, tn), lambda i, j, kk: (kk, j))
    else:
        b_spec = pl.BlockSpec((None, tk, tn), lambda i, j, kk: (j // njs, kk, j % njs))
    return _mm_call(body, (m // tm, n // tn, nk), pl.BlockSpec((tm, tk), lambda i, j, kk: (i, kk)), b_spec,
                    tm, tn, m, n, out_dtype, name, [a, b], res, out_slabs)


def _mm_nt(a, b, *, out_dtype, name, res=None, a_split=1):
    if a_split == 1:
        m, k = a.shape
    else:
        m, k = a.shape[1], a.shape[2] * a_split
    n = b.shape[0]
    tm, tn, tk = _mm_tiles(m, n, k, jnp.dtype(out_dtype).itemsize, res is not None, k_mult=a_split,
                           tm_cands=(2048, 1024))
    nk = k // tk
    nks = nk // a_split
    body = _mm_body(lambda x, y: lax.dot_general(x, y, (((1,), (1,)), ((), ())), preferred_element_type=F32),
                    nk, res is not None)
    if a_split == 1:
        a_spec = pl.BlockSpec((tm, tk), lambda i, j, kk: (i, kk))
    else:
        a_spec = pl.BlockSpec((None, tm, tk), lambda i, j, kk: (kk // nks, i, kk % nks))
    return _mm_call(body, (m // tm, n // tn, nk), a_spec, pl.BlockSpec((tn, tk), lambda i, j, kk: (j, kk)),
                    tm, tn, m, n, out_dtype, name, [a, b], res)


def _rms_fwd(x, g, name):
    t, d = x.shape
    tr = _pick(t, (512, 256, 128))

    def body(x_ref, g_ref, o_ref, ot_ref):
        y = _rms(x_ref[...], g_ref[...]).astype(o_ref.dtype)
        o_ref[...] = y
        ot_ref[...] = y.T

    return pl.pallas_call(
        body, grid=(t // tr,),
        in_specs=[pl.BlockSpec((tr, d), lambda i: (i, 0)), pl.BlockSpec((1, d), lambda i: (0, 0))],
        out_specs=[pl.BlockSpec((tr, d), lambda i: (i, 0)), pl.BlockSpec((d, tr), lambda i: (0, i))],
        out_shape=[S((t, d), ACT_DTYPE), S((d, t), ACT_DTYPE)], name=name,
        compiler_params=_cp(("parallel",)))(x, g)


def _rms_bwd(xin, g, dn, dres, name, out_dtype):
    t, d = xin.shape
    tr = _pick(t, (256, 128))

    def body(x_ref, g_ref, dn_ref, dr_ref, dx_ref, dg_ref):
        @pl.when(pl.program_id(0) == 0)
        def _():
            dg_ref[...] = jnp.zeros_like(dg_ref)

        _, vjp = jax.vjp(_rms, x_ref[...], g_ref[...])
        dx, dg = vjp(dn_ref[...].astype(F32))
        dx_ref[...] = (dr_ref[...].astype(F32) + dx).astype(dx_ref.dtype)
        dg_ref[0:1, :] += dg

    row = pl.BlockSpec((tr, d), lambda i: (i, 0))
    return pl.pallas_call(
        body, grid=(t // tr,),
        in_specs=[row, pl.BlockSpec((1, d), lambda i: (0, 0)), row, row],
        out_specs=[row, pl.BlockSpec((8, d), lambda i: (0, 0))],
        out_shape=[S((t, d), out_dtype), S((8, d), F32)], name=name,
        compiler_params=_cp(("arbitrary",)))(xin, g, dn, dres)


def _head(h2, q, pe, tgt, gf, name):
    t, d = h2.shape
    tr = _pick(t, (256, 128))

    def f(h2v, qv, pev, gfv, tv):
        h3 = h2v + jax.nn.sigmoid(qv) * pev
        y = _rms(h3, gfv)
        return 0.5 * jnp.sum(jnp.mean(jnp.square(y - tv), axis=-1))

    def body(h2_ref, q_ref, pe_ref, t_ref, g_ref, loss_ref, dh_ref, dq_ref, dpe_ref, dg_ref):
        @pl.when(pl.program_id(0) == 0)
        def _():
            loss_ref[...] = jnp.zeros_like(loss_ref)
            dg_ref[...] = jnp.zeros_like(dg_ref)

        tv = t_ref[...]
        loss, vjp = jax.vjp(lambda a, b, c, e: f(a, b, c, e, tv), h2_ref[...], q_ref[...].astype(F32),
                            pe_ref[...].astype(F32), g_ref[...])
        dh, dq, dpe, dg = vjp(jnp.ones((), F32))
        loss_ref[...] += jnp.full(loss_ref.shape, loss, F32)
        dh_ref[...] = dh.astype(dh_ref.dtype)
        dq_ref[...] = dq.astype(dq_ref.dtype)
        dpe_ref[...] = dpe.astype(dpe_ref.dtype)
        dg_ref[0:1, :] += dg

    row = pl.BlockSpec((tr, d), lambda i: (i, 0))
    return pl.pallas_call(
        body, grid=(t // tr,),
        in_specs=[row, row, row, row, pl.BlockSpec((1, d), lambda i: (0, 0))],
        out_specs=[pl.BlockSpec((8, 128), lambda i: (0, 0)), row, row, row, pl.BlockSpec((8, d), lambda i: (0, 0))],
        out_shape=[S((8, 128), F32), S((t, d), DRES_DTYPE), S((t, d), MXU_DTYPE), S((t, d), MXU_DTYPE), S((8, d), F32)],
        name=name, compiler_params=_cp(("arbitrary",)))(h2, q, pe, tgt, gf)


def _gmlp_block(us, vs, lng, lnb, wss, bss, ng):
    n = us[0].shape[0]
    row = lax.broadcasted_iota(jnp.int32, (n, n), 0)
    col = lax.broadcasted_iota(jnp.int32, (n, n), 1)
    outs = []
    for u0, v0, lg, lb, ws, bs in zip(us, vs, lng, lnb, wss, bss):
        u = jax.nn.gelu(u0)
        v = jax.nn.gelu(v0)
        mu = jnp.mean(v, axis=-1, keepdims=True)
        var = jnp.mean(jnp.square(v - mu), axis=-1, keepdims=True)
        vn = (v - mu) * lax.rsqrt(var + EPS) * lg + lb
        w = jnp.where(row >= col, ws, 0.0)
        sg = jnp.dot(_mx(w), _mx(vn), preferred_element_type=F32) + bs
        outs.append(u * sg)
    return _rms(jnp.concatenate(outs, axis=1), ng)


def _gmlp_load(proj_ref, lng_ref, lnb_ref, ws_ref, bst_ref, d_a, ng):
    sl = lambda g: slice(CHUNK * g, CHUNK * (g + 1))
    us = [proj_ref[:, sl(g)].astype(F32) for g in range(ng)]
    vs = [proj_ref[:, d_a + CHUNK * g: d_a + CHUNK * (g + 1)].astype(F32) for g in range(ng)]
    lng = [lng_ref[:, sl(g)] for g in range(ng)]
    lnb = [lnb_ref[:, sl(g)] for g in range(ng)]
    wss = [ws_ref[g] for g in range(ng)]
    bss = [bst_ref[:, g:g + 1] for g in range(ng)]
    return us, vs, lng, lnb, wss, bss


def _gmlp_fwd(proj, ln_g, ln_b, w_s, bst, norm_g, d_mix, name):
    t = proj.shape[0]
    ng = w_s.shape[0]
    d_a = ng * CHUNK

    def body(proj_ref, lng_ref, lnb_ref, ws_ref, bst_ref, ng_ref, o_ref, ot_ref):
        args = _gmlp_load(proj_ref, lng_ref, lnb_ref, ws_ref, bst_ref, d_a, ng)
        y = _gmlp_block(*args, ng_ref[...]).astype(o_ref.dtype)
        o_ref[...] = y
        ot_ref[...] = y.T

    vec = pl.BlockSpec((1, d_a), lambda c: (0, 0))
    return pl.pallas_call(
        body, grid=(t // CHUNK,),
        in_specs=[pl.BlockSpec((CHUNK, 2 * d_a), lambda c: (c, 0)), vec, vec,
                  pl.BlockSpec((ng, CHUNK, CHUNK), lambda c: (0, 0, 0)), pl.BlockSpec((CHUNK, 128), lambda c: (0, 0)), vec],
        out_specs=[pl.BlockSpec((CHUNK, d_a), lambda c: (c, 0)), pl.BlockSpec((d_a, CHUNK), lambda c: (0, c))],
        out_shape=[S((t, d_mix), ACT_DTYPE), S((d_mix, t), ACT_DTYPE)],
        name=name, compiler_params=_cp(("parallel",)))(proj, ln_g, ln_b, w_s, bst, norm_g)


def _gmlp_bwd(proj, ln_g, ln_b, w_s, bst, norm_g, dcat, d_proj, name):
    t = proj.shape[0]
    ng = w_s.shape[0]
    d_a = ng * CHUNK

    def body(proj_ref, lng_ref, lnb_ref, ws_ref, bst_ref, ng_ref, dy_ref,
             dp_ref, dlng_ref, dlnb_ref, dws_ref, dbst_ref, dng_ref):
        @pl.when(pl.program_id(0) == 0)
        def _():
            for r in (dlng_ref, dlnb_ref, dws_ref, dbst_ref, dng_ref):
                r[...] = jnp.zeros_like(r)

        args = _gmlp_load(proj_ref, lng_ref, lnb_ref, ws_ref, bst_ref, d_a, ng)
        _, vjp = jax.vjp(_gmlp_block, *args, ng_ref[...])
        dus, dvs, dlng, dlnb, dwss, dbss, dng = vjp(dy_ref[...].astype(F32))
        lane = lax.broadcasted_iota(jnp.int32, (1, 128), 1)
        dbst = jnp.zeros((CHUNK, 128), F32)
        for g in range(ng):
            dp_ref[:, CHUNK * g:CHUNK * (g + 1)] = dus[g].astype(dp_ref.dtype)
            dp_ref[:, d_a + CHUNK * g:d_a + CHUNK * (g + 1)] = dvs[g].astype(dp_ref.dtype)
            dlng_ref[0:1, CHUNK * g:CHUNK * (g + 1)] += dlng[g]
            dlnb_ref[0:1, CHUNK * g:CHUNK * (g + 1)] += dlnb[g]
            dws_ref[g] += dwss[g]
            dbst = dbst + dbss[g] * (lane == g).astype(F32)
        dbst_ref[...] += dbst
        dng_ref[0:1, :] += dng

    vec = pl.BlockSpec((1, d_a), lambda c: (0, 0))
    acc = pl.BlockSpec((8, d_a), lambda c: (0, 0))
    wspec = pl.BlockSpec((ng, CHUNK, CHUNK), lambda c: (0, 0, 0))
    bspec = pl.BlockSpec((CHUNK, 128), lambda c: (0, 0))
    return pl.pallas_call(
        body, grid=(t // CHUNK,),
        in_specs=[pl.BlockSpec((CHUNK, 2 * d_a), lambda c: (c, 0)), vec, vec, wspec, bspec, vec,
                  pl.BlockSpec((CHUNK, d_a), lambda c: (c, 0))],
        out_specs=[pl.BlockSpec((CHUNK, 2 * d_a), lambda c: (c, 0)), acc, acc, wspec, bspec, acc],
        out_shape=[S((t, d_proj), ACT_DTYPE), S((8, d_a), F32), S((8, d_a), F32), S((ng, CHUNK, CHUNK), F32),
                   S((CHUNK, 128), F32), S((8, d_a), F32)],
        name=name, compiler_params=_cp(("arbitrary",)))(proj, ln_g, ln_b, w_s, bst, norm_g, dcat)


def _silu_grad(c):
    s = jax.nn.sigmoid(c)
    return s * (1.0 + c * (1.0 - s))


def _fill_prev_main(s_ref, prev_ref, main_ref, i, tt):
    s_ref[pl.ds(0, HALO), :] = jnp.where(i > 0, prev_ref[...].astype(F32), 0.0)
    s_ref[pl.ds(HALO, tt), :] = main_ref[...].astype(F32)


def _prev_spec(tt, tc, joff):
    return pl.BlockSpec((HALO, tc), lambda j, i: (jnp.maximum(i * (tt // HALO) - 1, 0), j + joff))


def _next_spec(tt, tc, joff, t):
    return pl.BlockSpec((HALO, tc), lambda j, i: (jnp.minimum((i + 1) * (tt // HALO), t // HALO - 1), j + joff))


CONV_RC = 32


def _conv_tiles(t, c):
    return _pick(t, (2048, 1024, 512, 256, 128)), _pick(c, (256, 128))


def _row_chunks(tt, fn, init=0):
    rc = min(CONV_RC, tt)
    return lax.fori_loop(0, tt // rc, lambda q, c: fn(pl.multiple_of(q * rc, rc), rc, c), init)


def _fold8(p):
    acc = p[0:8]
    for r in range(8, p.shape[0], 8):
        acc = acc + p[r:r + 8]
    return acc


def _taps_chunk(s_ref, w_ref, kw, r0, rc):
    xe = s_ref[pl.ds(HALO - 8 + r0, rc + 8), :]
    acc = w_ref[0:1, :] * xe[8 - (kw - 1):8 - (kw - 1) + rc]
    for k in range(1, kw):
        acc = acc + w_ref[k:k + 1, :] * xe[8 - (kw - 1) + k:8 - (kw - 1) + k + rc]
    return acc


def _conv_bwd_chunk(sd_ref, x, w_ref, kw, r0, rc, dws):
    de = sd_ref[pl.ds(r0, rc + 8), :]
    dx, out = None, list(dws)
    for j in range(kw):
        d = de[j:j + rc]
        k = kw - 1 - j
        term = w_ref[k:k + 1, :] * d
        dx = term if dx is None else dx + term
        out[k] = out[k] + _fold8(x * d)
    return dx, out


def _conv_ssm_fwd(proj, col0, w, b, name):
    t = proj.shape[0]
    kw, c = w.shape
    tt, tc = _conv_tiles(t, c)
    joff = col0 // tc
    assert col0 % tc == 0

    def body(x_ref, xp_ref, w_ref, b_ref, o_ref, c_ref, s_ref):
        _fill_prev_main(s_ref, xp_ref, x_ref, pl.program_id(1), tt)

        def chunk(r0, rc, carry):
            cpre = _taps_chunk(s_ref, w_ref, kw, r0, rc) + b_ref[...]
            o_ref[pl.ds(r0, rc), :] = jax.nn.silu(cpre).astype(o_ref.dtype)
            c_ref[pl.ds(r0, rc), :] = cpre.astype(c_ref.dtype)
            return carry

        _row_chunks(tt, chunk)

    out = pl.BlockSpec((tt, tc), lambda j, i: (i, j))
    return pl.pallas_call(
        body, grid=(c // tc, t // tt),
        in_specs=[pl.BlockSpec((tt, tc), lambda j, i: (i, j + joff)), _prev_spec(tt, tc, joff),
                  pl.BlockSpec((kw, tc), lambda j, i: (0, j)), pl.BlockSpec((1, tc), lambda j, i: (0, j))],
        out_specs=[out, out], out_shape=[S((t, c), XBC_DTYPE), S((t, c), ACT_DTYPE)],
        scratch_shapes=[pltpu.VMEM((HALO + tt, tc), F32)], name=name,
        compiler_params=_cp(("parallel", "arbitrary")))(proj, proj, w, b)


def _conv_ssm_bwd(proj, col0, w, cpre, wcol0, dact, dproj, out_col0, name):
    t = proj.shape[0]
    kw = w.shape[0]
    c = dact.shape[1]
    tt, tc = _conv_tiles(t, c)
    assert col0 % tc == 0 and wcol0 % tc == 0 and out_col0 % tc == 0
    joff, wj, oj = col0 // tc, wcol0 // tc, out_col0 // tc
    nt = t // tt

    def body(x_ref, w_ref, c_ref, cn_ref, d_ref, dn_ref, dp_in, dx_ref, dw_ref, db_ref, sd_ref):
        del dp_in
        i = pl.program_id(1)

        @pl.when(i == 0)
        def _():
            dw_ref[...] = jnp.zeros_like(dw_ref)
            db_ref[...] = jnp.zeros_like(db_ref)

        def stage(r0, rc, db):
            rows = pl.ds(r0, rc)
            d = d_ref[rows, :].astype(F32) * _silu_grad(c_ref[rows, :].astype(F32))
            sd_ref[rows, :] = d
            return db + _fold8(d)

        zero8 = jnp.zeros((8, tc), F32)
        db = _row_chunks(tt, stage, zero8)
        sd_ref[pl.ds(tt, HALO), :] = jnp.where(
            i < nt - 1, dn_ref[...].astype(F32) * _silu_grad(cn_ref[...].astype(F32)), 0.0)

        def chunk(r0, rc, dws):
            dx, dws = _conv_bwd_chunk(sd_ref, x_ref[pl.ds(r0, rc), :].astype(F32), w_ref, kw, r0, rc, dws)
            dx_ref[pl.ds(r0, rc), :] = dx.astype(dx_ref.dtype)
            return dws

        dws = _row_chunks(tt, chunk, [zero8] * kw)
        for k in range(kw):
            dw_ref[k:k + 1, :] += jnp.sum(dws[k], axis=0, keepdims=True)
        db_ref[0:1, :] += jnp.sum(db, axis=0, keepdims=True)

    acc = pl.BlockSpec((8, tc), lambda j, i: (0, j))
    return pl.pallas_call(
        body, grid=(c // tc, nt),
        in_specs=[pl.BlockSpec((tt, tc), lambda j, i: (i, j + joff)), pl.BlockSpec((kw, tc), lambda j, i: (0, j + wj)),
                  pl.BlockSpec((tt, tc), lambda j, i: (i, j + wj)), _next_spec(tt, tc, wj, t),
                  pl.BlockSpec((tt, tc), lambda j, i: (i, j)), _next_spec(tt, tc, 0, t),
                  pl.BlockSpec(memory_space=pl.ANY)],
        out_specs=[pl.BlockSpec((tt, tc), lambda j, i: (i, j + oj)), acc, acc],
        out_shape=[S(dproj.shape, dproj.dtype), S((8, c), F32), S((8, c), F32)],
        scratch_shapes=[pltpu.VMEM((tt + HALO, tc), F32)],
        input_output_aliases={6: 0}, name=name,
        compiler_params=_cp(("parallel", "arbitrary")))(proj, w, cpre, cpre, dact, dact, dproj)


def _conv_ffn_fwd(hid, w, b, name):
    t, f2 = hid.shape
    f = f2 // 2
    kw = w.shape[0]
    tt, tc = _conv_tiles(t, f)
    nj = f // tc

    def body(g_ref, gp_ref, u_ref, up_ref, wg_ref, wu_ref, bg_ref, bu_ref, o_ref, ot_ref, cv_ref, sg_ref, su_ref):
        i = pl.program_id(1)
        _fill_prev_main(sg_ref, gp_ref, g_ref, i, tt)
        _fill_prev_main(su_ref, up_ref, u_ref, i, tt)

        def chunk(r0, rc, carry):
            rows = pl.ds(r0, rc)
            gate = _taps_chunk(sg_ref, wg_ref, kw, r0, rc) + bg_ref[...]
            up = _taps_chunk(su_ref, wu_ref, kw, r0, rc) + bu_ref[...]
            o_ref[rows, :] = (jax.nn.silu(gate) * up).astype(o_ref.dtype)
            cv_ref[0, rows, :] = gate.astype(cv_ref.dtype)
            cv_ref[1, rows, :] = up.astype(cv_ref.dtype)
            return carry

        _row_chunks(tt, chunk)
        ot_ref[...] = o_ref[...].T

    return pl.pallas_call(
        body, grid=(nj, t // tt),
        in_specs=[pl.BlockSpec((tt, tc), lambda j, i: (i, j)), _prev_spec(tt, tc, 0),
                  pl.BlockSpec((tt, tc), lambda j, i: (i, j + nj)), _prev_spec(tt, tc, nj),
                  pl.BlockSpec((kw, tc), lambda j, i: (0, j)), pl.BlockSpec((kw, tc), lambda j, i: (0, j + nj)),
                  pl.BlockSpec((1, tc), lambda j, i: (0, j)), pl.BlockSpec((1, tc), lambda j, i: (0, j + nj))],
        out_specs=[pl.BlockSpec((tt, tc), lambda j, i: (i, j)), pl.BlockSpec((tc, tt), lambda j, i: (j, i)),
                   pl.BlockSpec((2, tt, tc), lambda j, i: (0, i, j))],
        out_shape=[S((t, f), ACT_DTYPE), S((f, t), ACT_DTYPE), S((2, t, f), ACT_DTYPE)],
        scratch_shapes=[pltpu.VMEM((HALO + tt, tc), F32), pltpu.VMEM((HALO + tt, tc), F32)], name=name,
        compiler_params=_cp(("parallel", "arbitrary")))(hid, hid, hid, hid, w, w, b, b)


def _conv_ffn_bwd(hid, w, cv, dact, name):
    t, f2 = hid.shape
    f = f2 // 2
    kw = w.shape[0]
    tt, tc = _conv_tiles(t, f)
    nj = f // tc
    nt = t // tt

    def body(g_ref, u_ref, wg_ref, wu_ref, cv_ref, cvn_ref, d_ref, dn_ref, dh_ref, dw_ref, db_ref, dg_ref, du_ref):
        i = pl.program_id(1)

        @pl.when(i == 0)
        def _():
            dw_ref[...] = jnp.zeros_like(dw_ref)
            db_ref[...] = jnp.zeros_like(db_ref)

        def cotangents(gate, up, dact_v):
            sg = jax.nn.sigmoid(gate)
            return dact_v * up * (sg * (1.0 + gate * (1.0 - sg))), dact_v * (gate * sg)

        def stage(r0, rc, dbs):
            rows = pl.ds(r0, rc)
            dg, du = cotangents(cv_ref[0, rows, :].astype(F32), cv_ref[1, rows, :].astype(F32), d_ref[rows, :].astype(F32))
            dg_ref[rows, :] = dg
            du_ref[rows, :] = du
            return [dbs[0] + _fold8(dg), dbs[1] + _fold8(du)]

        zero8 = jnp.zeros((8, tc), F32)
        dbs = _row_chunks(tt, stage, [zero8, zero8])
        dgn, dun = cotangents(cvn_ref[0].astype(F32), cvn_ref[1].astype(F32), dn_ref[...].astype(F32))
        dg_ref[pl.ds(tt, HALO), :] = jnp.where(i < nt - 1, dgn, 0.0)
        du_ref[pl.ds(tt, HALO), :] = jnp.where(i < nt - 1, dun, 0.0)
        for s, (sd_ref, x_ref, w_ref) in enumerate(((dg_ref, g_ref, wg_ref), (du_ref, u_ref, wu_ref))):
            def chunk(r0, rc, dws, s=s, sd_ref=sd_ref, x_ref=x_ref, w_ref=w_ref):
                dx, dws = _conv_bwd_chunk(sd_ref, x_ref[pl.ds(r0, rc), :].astype(F32), w_ref, kw, r0, rc, dws)
                dh_ref[s, pl.ds(r0, rc), :] = dx.astype(dh_ref.dtype)
                return dws

            dws = _row_chunks(tt, chunk, [zero8] * kw)
            for k in range(kw):
                dw_ref[s, k:k + 1, :] += jnp.sum(dws[k], axis=0, keepdims=True)
            db_ref[s, 0:1, :] += jnp.sum(dbs[s], axis=0, keepdims=True)

    acc = pl.BlockSpec((2, 8, tc), lambda j, i: (0, 0, j))
    dsc = pltpu.VMEM((tt + HALO, tc), F32)
    nxt = lambda j, i: (0, jnp.minimum((i + 1) * (tt // HALO), t // HALO - 1), j)
    return pl.pallas_call(
        body, grid=(nj, nt),
        in_specs=[pl.BlockSpec((tt, tc), lambda j, i: (i, j)), pl.BlockSpec((tt, tc), lambda j, i: (i, j + nj)),
                  pl.BlockSpec((kw, tc), lambda j, i: (0, j)), pl.BlockSpec((kw, tc), lambda j, i: (0, j + nj)),
                  pl.BlockSpec((2, tt, tc), lambda j, i: (0, i, j)), pl.BlockSpec((2, HALO, tc), nxt),
                  pl.BlockSpec((tt, tc), lambda j, i: (i, j)), _next_spec(tt, tc, 0, t)],
        out_specs=[pl.BlockSpec((2, tt, tc), lambda j, i: (0, i, j)), acc, acc],
        out_shape=[S((2, t, f), MXU_DTYPE), S((2, 8, f), F32), S((2, 8, f), F32)],
        scratch_shapes=[dsc, dsc], name=name,
        compiler_params=_cp(("parallel", "arbitrary")))(hid, hid, w, w, cv, cv, dact, dact)


SSD_SUB = 2
SSD_FWD_BLOCKS = 2


def _ssd_chunk(xs, bm, cm, dtraw, hin, bias, alog, dskip):
    n = bm.shape[0]
    row = lax.broadcasted_iota(jnp.int32, (n, n), 0)
    col = lax.broadcasted_iota(jnp.int32, (n, n), 1)
    causal = row >= col
    lane = lax.broadcasted_iota(jnp.int32, (1, 128), 1)
    sub = lax.broadcasted_iota(jnp.int32, (128, 1), 0)
    last = (lax.broadcasted_iota(jnp.int32, (n, 1), 0) == n - 1).astype(F32)
    dt = jax.nn.softplus(dtraw + bias)
    adt = dt * (-jnp.exp(alog))
    tri = causal.astype(F32)
    acs = jnp.dot(tri, adt, preferred_element_type=F32, precision=lax.Precision.HIGHEST)
    ch = lax.broadcasted_iota(jnp.int32, (128, GROUP_CH), 1)
    hd = lax.broadcasted_iota(jnp.int32, (128, GROUP_CH), 0) * HEAD_DIM
    expand = ((ch >= hd) & (ch < hd + HEAD_DIM)).astype(F32)
    acs_x = jnp.dot(acs, expand, preferred_element_type=F32, precision=lax.Precision.HIGHEST)
    alast_x = jnp.sum(acs_x * last, axis=0, keepdims=True)
    acs_t = acs.T
    scores = lax.dot_general(_mx(cm), _mx(bm), (((1,), (1,)), ((), ())), preferred_element_type=F32)
    yds, xts = [], []
    for r in range(HEADS_PER_GROUP):
        pick = (lane == r).astype(F32)
        acol = jnp.sum(acs * pick, axis=1, keepdims=True)
        arow = jnp.sum(acs_t * (sub == r).astype(F32), axis=0, keepdims=True)
        dtc = jnp.sum(dt * pick, axis=1, keepdims=True)
        lm = jnp.exp(jnp.where(causal, acol - arow, -1e30))
        xts.append(xs[r] * dtc)
        yds.append(jnp.dot(_mx(scores * lm), _mx(xts[r]), preferred_element_type=F32))
    xt = jnp.concatenate(xts, axis=1)
    yo = jnp.exp(acs_x) * jnp.dot(_mx(cm), _mx(hin), preferred_element_type=F32)
    st = lax.dot_general(_mx(bm), _mx(xt * jnp.exp(alast_x - acs_x)), (((0,), (0,)), ((), ())), preferred_element_type=F32)
    hout = jnp.exp(alast_x) * hin + st
    return jnp.concatenate(yds, axis=1) + yo + dskip * jnp.concatenate(xs, axis=1), hout


def _ssd_post(y, z, normg):
    return _rms(y * jax.nn.silu(z), normg)


def _ssd_block(datas, hin, consts):
    ys = []
    for data in datas:
        y, hin = _ssd_chunk(*data, hin, *consts)
        ys.append(y)
    return ys, hin


def _ssd_specs(d_ssm, ngrp, zcol0, rev, nb, sub=SSD_SUB):
    cc = (lambda c: nb - 1 - c) if rev else (lambda c: c)
    rows = sub * CHUNK
    xj, bj, cj, zj = 0, d_ssm // 128, d_ssm // 128 + ngrp, zcol0 // GROUP_CH
    const = lambda w: pl.BlockSpec((None, 8, w), lambda g, c: (g, 0, 0))
    return cc, [
        pl.BlockSpec((rows, GROUP_CH), lambda g, c: (cc(c), xj + g)),
        pl.BlockSpec((rows, 128), lambda g, c: (cc(c), bj + g)),
        pl.BlockSpec((rows, 128), lambda g, c: (cc(c), cj + g)),
        pl.BlockSpec((rows, 128), lambda g, c: (cc(c), g)),
        pl.BlockSpec((rows, GROUP_CH), lambda g, c: (cc(c), zj + g)),
        const(128), const(128), const(GROUP_CH), const(GROUP_CH)]


def _sub_rows(s):
    return slice(CHUNK * s, CHUNK * (s + 1))


def _ssd_load(x_ref, b_ref, c_ref, dt_ref, z_ref, bias_ref, alog_ref, dsk_ref, ng_ref, sub=SSD_SUB):
    datas, zs = [], []
    for s in range(sub):
        rows = _sub_rows(s)
        xs = [x_ref[rows, HEAD_DIM * r:HEAD_DIM * (r + 1)].astype(F32) for r in range(HEADS_PER_GROUP)]
        datas.append((xs, b_ref[rows, :].astype(F32), c_ref[rows, :].astype(F32), dt_ref[rows, :]))
        zs.append(z_ref[rows, :].astype(F32))
    return datas, zs, (bias_ref[0:1, :], alog_ref[0:1, :], dsk_ref[0:1, :]), ng_ref[0:1, :]


def _ssd_fwd(xbc, dtp, proj, zcol0, bias_p, alog_p, dskip_x, normg_x, cat, cat_t, name):
    t = xbc.shape[0]
    ngrp = bias_p.shape[0]
    d_ssm = ngrp * GROUP_CH
    fb = SSD_FWD_BLOCKS if t % (SSD_FWD_BLOCKS * SSD_SUB * CHUNK) == 0 else 1
    sub = fb * SSD_SUB
    rows = sub * CHUNK
    nb = t // rows
    d_a = cat.shape[1] - d_ssm
    assert d_a % GROUP_CH == 0 and zcol0 % GROUP_CH == 0 and t % rows == 0
    _, specs = _ssd_specs(d_ssm, ngrp, zcol0, False, nb, sub)

    def body(x_ref, b_ref, c_ref, dt_ref, z_ref, bias_ref, alog_ref, dsk_ref, ng_ref, cat_in, catt_in,
             yn_ref, ynt_ref, y_ref, hs_ref, h_ref):
        del cat_in, catt_in

        @pl.when(pl.program_id(1) == 0)
        def _():
            h_ref[...] = jnp.zeros_like(h_ref)

        datas, zs, consts, normg = _ssd_load(x_ref, b_ref, c_ref, dt_ref, z_ref, bias_ref, alog_ref, dsk_ref, ng_ref, sub)
        h, ys = h_ref[...], []
        for k in range(fb):
            hs_ref[k] = h
            ys_k, h = _ssd_block(datas[k * SSD_SUB:(k + 1) * SSD_SUB], h, consts)
            ys += ys_k
        for s in range(sub):
            y_ref[_sub_rows(s), :] = ys[s].astype(y_ref.dtype)
            yn = _ssd_post(ys[s], zs[s], normg).astype(yn_ref.dtype)
            yn_ref[_sub_rows(s), :] = yn
            ynt_ref[:, _sub_rows(s)] = yn.T
        h_ref[...] = h

    hbm = pl.BlockSpec(memory_space=pl.ANY)
    return pl.pallas_call(
        body, grid=(ngrp, nb), in_specs=specs + [hbm, hbm],
        out_specs=[pl.BlockSpec((rows, GROUP_CH), lambda g, c: (c, d_a // GROUP_CH + g)),
                   pl.BlockSpec((GROUP_CH, rows), lambda g, c: (d_a // GROUP_CH + g, c)),
                   pl.BlockSpec((rows, GROUP_CH), lambda g, c: (c, g)),
                   pl.BlockSpec((fb, None, D_STATE, GROUP_CH), lambda g, c: (c, g, 0, 0))],
        out_shape=[S(cat.shape, cat.dtype), S(cat_t.shape, cat_t.dtype), S((t, d_ssm), ACT_DTYPE),
                   S((nb * fb, ngrp, D_STATE, GROUP_CH), F32)],
        scratch_shapes=[pltpu.VMEM((D_STATE, GROUP_CH), F32)],
        input_output_aliases={9: 0, 10: 1}, name=name,
        compiler_params=_cp(("parallel", "arbitrary")))(xbc, xbc, xbc, dtp, proj, bias_p, alog_p, dskip_x, normg_x, cat, cat_t)


def _ssd_bwd(xbc, dtp, proj, zcol0, bias_p, alog_p, dskip_x, normg_x, hs, ypre, dcat, dproj, name):
    t = xbc.shape[0]
    ngrp = bias_p.shape[0]
    d_ssm = ngrp * GROUP_CH
    rows = SSD_SUB * CHUNK
    nb = t // rows
    d_a = dcat.shape[1] - d_ssm
    cc, specs = _ssd_specs(d_ssm, ngrp, zcol0, True, nb)

    def body(x_ref, b_ref, c_ref, dt_ref, z_ref, bias_ref, alog_ref, dsk_ref, ng_ref, hs_ref, yp_ref, dy_ref, dp_in,
             dz_ref, dx_ref, db_ref, dc_ref, ddt_ref, dbias_ref, dalog_ref, ddsk_ref, dng_ref, dh_ref):
        del dp_in

        @pl.when(pl.program_id(1) == 0)
        def _():
            dh_ref[...] = jnp.zeros_like(dh_ref)
            for r in (dbias_ref, dalog_ref, ddsk_ref, dng_ref):
                r[...] = jnp.zeros_like(r)

        datas, zs, consts, normg = _ssd_load(x_ref, b_ref, c_ref, dt_ref, z_ref, bias_ref, alog_ref, dsk_ref, ng_ref)
        dys, dng = [], jnp.zeros_like(normg)
        for s in range(SSD_SUB):
            rws = _sub_rows(s)
            _, vjp_post = jax.vjp(_ssd_post, yp_ref[rws, :].astype(F32), zs[s], normg)
            dy, dz, dg = vjp_post(dy_ref[rws, :].astype(F32))
            dys.append(dy)
            dng = dng + dg
            dz_ref[rws, :] = dz.astype(dz_ref.dtype)
        _, vjp = jax.vjp(_ssd_block, datas, hs_ref[...], consts)
        ddatas, dhin, (dbias, dalog, ddsk) = vjp((dys, dh_ref[...]))
        for s, (dxs, dbm, dcm, ddt) in enumerate(ddatas):
            rws = _sub_rows(s)
            for r in range(HEADS_PER_GROUP):
                dx_ref[rws, HEAD_DIM * r:HEAD_DIM * (r + 1)] = dxs[r].astype(dx_ref.dtype)
            db_ref[rws, :] = dbm.astype(db_ref.dtype)
            dc_ref[rws, :] = dcm.astype(dc_ref.dtype)
            ddt_ref[rws, :] = ddt
        dh_ref[...] = dhin
        dbias_ref[0:1, :] += dbias
        dalog_ref[0:1, :] += dalog
        ddsk_ref[0:1, :] += ddsk
        dng_ref[0:1, :] += dng

    acc = lambda w: pl.BlockSpec((None, 8, w), lambda g, c: (g, 0, 0))
    blk = lambda w: pl.BlockSpec((rows, w), lambda g, c: (cc(c), g))
    return pl.pallas_call(
        body, grid=(ngrp, nb),
        in_specs=specs + [pl.BlockSpec((None, None, D_STATE, GROUP_CH), lambda g, c: (cc(c), g, 0, 0)),
                          blk(GROUP_CH),
                          pl.BlockSpec((rows, GROUP_CH), lambda g, c: (cc(c), d_a // GROUP_CH + g)),
                          pl.BlockSpec(memory_space=pl.ANY)],
        out_specs=[pl.BlockSpec((rows, GROUP_CH), lambda g, c: (cc(c), zcol0 // GROUP_CH + g)),
                   blk(GROUP_CH), blk(128), blk(128), blk(128), acc(128), acc(128), acc(GROUP_CH), acc(GROUP_CH)],
        out_shape=[S(dproj.shape, dproj.dtype), S((t, d_ssm), XBC_DTYPE), S((t, ngrp * 128), XBC_DTYPE),
                   S((t, ngrp * 128), XBC_DTYPE), S((t, ngrp * 128), F32), S((ngrp, 8, 128), F32),
                   S((ngrp, 8, 128), F32), S((ngrp, 8, GROUP_CH), F32), S((ngrp, 8, GROUP_CH), F32)],
        scratch_shapes=[pltpu.VMEM((D_STATE, GROUP_CH), F32)],
        input_output_aliases={12: 0}, name=name,
        compiler_params=_cp(("parallel", "arbitrary")))(xbc, xbc, xbc, dtp, proj, bias_p, alog_p, dskip_x, normg_x, hs, ypre,
                                                        dcat, dproj)


def _adamw(parts, w, m, v, after, name):
    r, c = w.shape
    tr = _pick(r, (256, 128, 64, 32, 16, 8)) if c * 4 * 256 <= 4 * 1024 * 1024 else _pick(r, (64, 32, 16, 8))

    def body(p_ref, w_ref, m_ref, v_ref, after_ref, g_ref, d_ref, nm_ref, nv_ref):
        del after_ref
        g = p_ref[0].astype(F32)
        for k in range(1, N_DEV):
            g = g + p_ref[k].astype(F32)
        mm = ADAM_B1 * m_ref[...] + (1.0 - ADAM_B1) * g
        vv = ADAM_B2 * v_ref[...] + (1.0 - ADAM_B2) * jnp.square(g)
        m_hat = mm / (1.0 - ADAM_B1 ** ADAM_STEP)
        v_hat = vv / (1.0 - ADAM_B2 ** ADAM_STEP)
        g_ref[...] = g
        d_ref[...] = -ADAM_LR * (m_hat / (jnp.sqrt(v_hat) + ADAM_EPS) + ADAM_WD * w_ref[...])
        nm_ref[...] = mm
        nv_ref[...] = vv

    blk = pl.BlockSpec((tr, c), lambda i: (i, 0))
    return pl.pallas_call(
        body, grid=(r // tr,),
        in_specs=[pl.BlockSpec((N_DEV, tr, c), lambda i: (0, i, 0)), blk, blk, blk, pl.BlockSpec(memory_space=pl.ANY)],
        out_specs=[blk, blk, blk, blk], out_shape=[S((r, c), F32)] * 4, name=name,
        compiler_params=_cp(("parallel",)))(parts, w, m, v, after)


def _mesh_pos():
    return lax.axis_index("x"), lax.axis_index("y"), lax.axis_index("c")


def _peer(d, x, y, c):
    return (1 - x if (d >> 2) & 1 else x, 1 - y if (d >> 1) & 1 else y, 1 - c if d & 1 else c)


def _gather_two_level(blocks, name):
    n = len(blocks)

    def body(*refs):
        srcs, outs = refs[:n], refs[n:2 * n]
        send_sems, recv_sems, loc_sems = refs[2 * n:]
        x, y, c = _mesh_pos()
        lin = lambda px, py, pc: 4 * px + 2 * py + pc
        me, sibling = (x, y, c), (x, y, 1 - c)
        chips = [(1 - x, y), (x, 1 - y), (1 - x, 1 - y)]

        def copy(a, k, block, to, src=None):
            slab = outs[a].at[lin(*block)]
            return pltpu.make_async_remote_copy(
                src_ref=slab if src is None else src, dst_ref=slab, send_sem=send_sems.at[a, k],
                recv_sem=recv_sems.at[a, k], device_id=to, device_id_type=pl.DeviceIdType.MESH)

        mine = [pltpu.make_async_copy(srcs[a], outs[a].at[lin(*me)], loc_sems.at[a]) for a in range(n)]
        first = [copy(a, 0, me, sibling, src=srcs[a]) for a in range(n)]
        first += [copy(a, 1 + j, me, (*chip, c), src=srcs[a]) for j, chip in enumerate(chips) for a in range(n)]
        for cp in mine + first:
            cp.start()
        passed = []
        for j, chip in enumerate(chips):
            for a in range(n):
                copy(a, 1 + j, (*chip, c), me).wait_recv()
                passed.append(copy(a, 4 + j, (*chip, c), sibling))
                passed[-1].start()
        for a in range(n):
            copy(a, 0, sibling, me).wait_recv()
            for j, chip in enumerate(chips):
                copy(a, 4 + j, (*chip, 1 - c), me).wait_recv()
        for cp in first + passed:
            cp.wait_send()
        for cp in mine:
            cp.wait()

    hbm = pl.BlockSpec(memory_space=pl.ANY)
    return pl.pallas_call(
        body, in_specs=[hbm] * n, out_specs=[hbm] * n, out_shape=[S((N_DEV,) + b.shape, b.dtype) for b in blocks],
        scratch_shapes=[pltpu.SemaphoreType.DMA((n, N_DEV - 1)), pltpu.SemaphoreType.DMA((n, N_DEV - 1)),
                        pltpu.SemaphoreType.DMA((n,))],
        name=name, compiler_params=pltpu.CompilerParams(has_side_effects=True))(*blocks)


def _xfer_start(items, name, after=None):
    n = len(items)
    kinds = [k for k, _ in items]
    srcs = [pltpu.with_memory_space_constraint(a, pltpu.HBM) for _, a in items]
    land_shapes = [((N_DEV,) + a.shape if k == 'gather' else a.shape, a.dtype) for k, a in items]
    lands = [pltpu.with_memory_space_constraint(lax.empty(s, dt), pltpu.HBM) for s, dt in land_shapes]
    extra = [] if after is None else [after]

    def body(*refs):
        src_refs, land_refs = refs[:n], refs[n:2 * n]
        outs = refs[2 * n + len(extra):]
        sems = outs[:2 * n]
        token = outs[4 * n]
        x, y, c = _mesh_pos()
        me = 4 * x + 2 * y + c
        for a in range(n):
            for d in range(1, N_DEV):
                px, py, pc = _peer(d, x, y, c)
                src = src_refs[a] if kinds[a] == 'gather' else src_refs[a].at[4 * px + 2 * py + pc]
                pltpu.make_async_remote_copy(
                    src_ref=src, dst_ref=land_refs[a].at[me], send_sem=sems[2 * a].at[d - 1],
                    recv_sem=sems[2 * a + 1].at[d - 1], device_id=(px, py, pc),
                    device_id_type=pl.DeviceIdType.MESH).start()
        token[...] = jnp.zeros_like(token)

    hbm = pl.BlockSpec(memory_space=pltpu.HBM)
    sem = pl.BlockSpec(memory_space=pltpu.SEMAPHORE)
    out_shape = ([pltpu.SemaphoreType.DMA((N_DEV - 1,))] * (2 * n)
                 + [pltpu.HBM(a.shape, a.dtype) for a in srcs] + [pltpu.HBM(s, dt) for s, dt in land_shapes]
                 + [S((8, 128), F32)])
    res = pl.pallas_call(
        body, name=name, out_shape=out_shape,
        in_specs=[hbm] * (2 * n) + [pl.BlockSpec(memory_space=pl.ANY)] * len(extra),
        out_specs=[sem] * (2 * n) + [hbm] * (2 * n) + [pl.BlockSpec(memory_space=pltpu.VMEM)],
        input_output_aliases={**{a: 2 * n + a for a in range(n)}, **{n + a: 3 * n + a for a in range(n)}},
        compiler_params=pltpu.CompilerParams(has_side_effects=pltpu.SideEffectType.DATAFLOW_SIDE_EFFECTING),
    )(*srcs, *lands, *extra)
    return (kinds, res[:2 * n], res[2 * n:3 * n], res[3 * n:4 * n]), res[4 * n]


def _xfer_wait(handle, after, name):
    kinds, sems, src_thru, land_thru = handle
    n = len(kinds)

    def body(*refs):
        land_refs = refs[n:2 * n]
        sem_refs = refs[2 * n:4 * n]
        x, y, c = _mesh_pos()
        me = 4 * x + 2 * y + c
        for a in range(n):
            for d in range(1, N_DEV):
                slab = land_refs[a].at[me]
                cp = pltpu.make_async_remote_copy(
                    src_ref=slab, dst_ref=slab, send_sem=sem_refs[2 * a].at[d - 1], recv_sem=sem_refs[2 * a + 1].at[d - 1],
                    device_id=_peer(d, x, y, c), device_id_type=pl.DeviceIdType.MESH)
                cp.wait_send()
                cp.wait_recv()

    hbm = pl.BlockSpec(memory_space=pltpu.HBM)
    sem = pl.BlockSpec(memory_space=pltpu.SEMAPHORE)
    res = pl.pallas_call(
        body, name=name,
        out_shape=[pltpu.HBM(a.shape, a.dtype) for a in src_thru] + [pltpu.HBM(a.shape, a.dtype) for a in land_thru],
        in_specs=[hbm] * (2 * n) + [sem] * (2 * n) + [pl.BlockSpec(memory_space=pl.ANY)],
        out_specs=[hbm] * (2 * n), input_output_aliases={a: a for a in range(2 * n)},
        compiler_params=pltpu.CompilerParams(has_side_effects=pltpu.SideEffectType.DATAFLOW_SIDE_EFFECTING),
    )(*src_thru, *land_thru, *sems, after)
    x, y, c = _mesh_pos()
    me = 4 * x + 2 * y + c
    out = []
    for a in range(n):
        src = res[a]
        own = src[None] if kinds[a] == 'gather' else lax.dynamic_index_in_dim(src, me, 0, keepdims=True)
        out.append(lax.dynamic_update_index_in_dim(res[n + a], own, me, 0))
    return out


def _stack_to_full(kind, st):
    if kind == 'row':
        return st.reshape(st.shape[0] * st.shape[1], st.shape[2])
    return jnp.concatenate([st[k] for k in range(st.shape[0])], axis=1)


def _full_to_stack(kind, full):
    r, c = full.shape
    if kind == 'row':
        return full.reshape(N_DEV, r // N_DEV, c)
    w = c // N_DEV
    return jnp.stack([full[:, k * w:(k + 1) * w] for k in range(N_DEV)], axis=0)


SMALL_ROWS = 256


def _pack_small(named):
    layout = [(a.shape, a.size, -(-a.size // 1024) * 8) for a in named]
    total = -(-sum(nr for _, _, nr in layout) // SMALL_ROWS) * SMALL_ROWS * 128
    packed, off = None, 0
    for a, (_, n, nr) in zip(named, layout):
        part = jnp.pad(a.reshape(-1).astype(F32), (off, total - off - n))
        packed = part if packed is None else packed + part
        off += nr * 128
    return packed.reshape(total // 128, 128), layout


def _unpack_small(packed, layout):
    out, r0 = [], 0
    for shape, n, nr in layout:
        out.append(packed[r0:r0 + nr].reshape(-1)[:n].reshape(shape))
        r0 += nr
    return out


def _row0(acc):
    return acc[0]


def _local_step(x, p, tgt, sm, comm):
    t, d = x.shape
    h_n = sm['dt_bias'].shape[-1]
    ngrp = h_n // HEADS_PER_GROUP
    d_ssm = h_n * HEAD_DIM
    d_a = sm['ln_a_g'].shape[-1]
    d_mix = d_a + d_ssm
    d_xbc = sm['conv_ssm_b'].shape[-1]
    d_main = 2 * d_a + d_ssm + d_xbc
    assert d_xbc == d_ssm + 2 * ngrp * D_STATE and h_n <= 128
    zcol0, xcol0 = 2 * d_a, 2 * d_a + d_ssm
    vec = lambda v: v.reshape(1, -1)

    bst = jnp.pad(sm['b_s'].T, ((0, 0), (0, 128 - sm['b_s'].shape[0])))
    grp = lambda v, w: jnp.broadcast_to(jnp.pad(v.reshape(ngrp, 1, -1), ((0, 0), (0, 0), (0, w - v.size // ngrp))), (ngrp, 8, w))
    bias_p, alog_p = grp(sm['dt_bias'], 128), grp(sm['a_log'], 128)
    dskip_x = grp(jnp.repeat(sm['d_skip'], HEAD_DIM), GROUP_CH)
    normg_x = grp(sm['ssm_norm_g'], GROUP_CH)
    pad_dt = lambda v: jnp.pad(v[:, :h_n].reshape(t, ngrp, HEADS_PER_GROUP),
                               ((0, 0), (0, 0), (0, 128 - HEADS_PER_GROUP))).reshape(t, ngrp * 128)

    g_mix = vec(sm['norm_mix_g']) + comm.tok0
    a_n, a_t = _rms_fwd(x, g_mix, "rms_mix")
    wf = comm.weights('a', a_n)
    slabs = [wf['w_in'][k] for k in range(N_DEV)]
    w_main = jnp.concatenate(slabs[:-1] + [slabs[-1][:, :slabs[-1].shape[1] - h_n]], axis=1)
    w_dt = jnp.pad(slabs[-1][:, slabs[-1].shape[1] - h_n:], ((0, 0), (0, 128 - h_n)))
    proj = _mm_nn(a_n, w_main, out_dtype=ACT_DTYPE, name="mm_in")
    dtp = pad_dt(_mm_nn(a_n, w_dt, out_dtype=F32, name="mm_dt"))
    cat, cat_t = _gmlp_fwd(proj, vec(sm['ln_a_g']), vec(sm['ln_a_b']), sm['w_s'], bst, vec(sm['norm_a_g']), d_mix, "gmlp_fwd")
    xbc, cpre = _conv_ssm_fwd(proj, xcol0, wf['conv_ssm_w'], vec(sm['conv_ssm_b']), "conv_ssm_fwd")
    cat, cat_t, ypre, hs = _ssd_fwd(xbc, dtp, proj, zcol0, bias_p, alog_p, dskip_x, normg_x, cat, cat_t, "ssd_fwd")
    wf.update(comm.weights('b', hs))
    h1 = _mm_nn(cat, wf['w_out'], out_dtype=F32, name="mm_out", res=x)
    f_n, f_t = _rms_fwd(h1, vec(sm['norm_ffn_g']), "rms_ffn")
    hid = _mm_nn(f_n, wf['w_up'], out_dtype=ACT_DTYPE, name="mm_up")
    act, act_t, cv = _conv_ffn_fwd(hid, wf['conv_ffn_w'], vec(sm['conv_ffn_b']), "conv_ffn_fwd")
    h2 = _mm_nn(act, wf['w_down'], out_dtype=F32, name="mm_down", res=h1)
    r_n, r_t = _rms_fwd(h2, vec(sm['norm_ple_g']), "rms_ple")
    q = _mm_nn(r_n, wf['w_ple_gate'], out_dtype=ACT_DTYPE, name="mm_pg")
    p_m = p.astype(MXU_DTYPE)
    pe = _mm_nn(p_m, wf['w_ple'], out_dtype=ACT_DTYPE, name="mm_ple")

    loss, dh3, dq, dpe, dgf = _head(h2, q, pe, tgt, vec(sm['norm_final_g']), "head")
    wgrad = lambda act_t, g, name, **kw: _mm_nn(act_t, g, out_dtype=WIRE_DTYPE, name=name, wide=True, **kw)
    gs = {}
    gs['norm_final_g'] = _row0(dgf)
    g_ple = wgrad(p_m.T, dpe, "wg_ple")
    g_pg = wgrad(r_t, dq, "wg_pg")
    dr = _mm_nt(dq, wf['w_ple_gate'], out_dtype=ACT_DTYPE, name="dg_pg")
    dh2m, dg = _rms_bwd(h2, vec(sm['norm_ple_g']), dr, dh3, "rms_ple_bwd", DRES_DTYPE)
    gs['norm_ple_g'] = _row0(dg)
    g_down = wgrad(act_t, dh2m, "wg_down")
    tok = comm.send('1', {'w_ple': g_ple, 'w_ple_gate': g_pg, 'w_down': g_down})
    dact = _mm_nt(dh2m, wf['w_down'], out_dtype=ACT_DTYPE, name="dg_down")
    dhid, dcw, dcb = _conv_ffn_bwd(hid, wf['conv_ffn_w'] + tok, cv, dact, "conv_ffn_bwd")
    kf = wf['conv_ffn_w'].shape[0]
    g_cf = jnp.concatenate([dcw[0, :kf], dcw[1, :kf]], axis=1)
    gs['conv_ffn_b'] = jnp.concatenate([dcb[0, 0], dcb[1, 0]], axis=0)
    g_up = wgrad(f_t, dhid, "wg_up", b_split=2, out_slabs=N_DEV)
    df = _mm_nt(dhid, wf['w_up'], out_dtype=ACT_DTYPE, name="dg_up", a_split=2)
    dh1m, dg = _rms_bwd(h1, vec(sm['norm_ffn_g']), df, dh2m, "rms_ffn_bwd", DRES_DTYPE)
    gs['norm_ffn_g'] = _row0(dg)
    g_out = wgrad(cat_t, dh1m, "wg_out")
    tok = comm.send('2', {'conv_ffn_w': g_cf, 'w_up': g_up, 'w_out': g_out}, stacked=('w_up',))
    dcat = _mm_nt(dh1m, wf['w_out'], out_dtype=ACT_DTYPE, name="dg_out")
    dproj, dlng, dlnb, dws, dbst, dng = _gmlp_bwd(proj, vec(sm['ln_a_g']) + tok, vec(sm['ln_a_b']), sm['w_s'], bst,
                                                  vec(sm['norm_a_g']), dcat, d_main, "gmlp_bwd")
    gs['ln_a_g'], gs['ln_a_b'], gs['w_s'], gs['norm_a_g'] = _row0(dlng), _row0(dlnb), dws, _row0(dng)
    gs['b_s'] = dbst[:, :sm['b_s'].shape[0]].T
    dproj, dxs, dbm, dcm, ddtp, dbias, dalog, ddsk, dsng = _ssd_bwd(
        xbc, dtp, proj, zcol0, bias_p, alog_p, dskip_x, normg_x, hs, ypre, dcat, dproj, "ssd_bwd")
    gs['dt_bias'] = dbias[:, 0, :HEADS_PER_GROUP].reshape(h_n)
    gs['a_log'] = dalog[:, 0, :HEADS_PER_GROUP].reshape(h_n)
    gs['d_skip'] = ddsk[:, 0, :].reshape(h_n, HEAD_DIM).sum(axis=-1)
    gs['ssm_norm_g'] = dsng[:, 0, :].reshape(d_ssm)
    dws_c, dbs_c = [], []
    off = 0
    for nm, dpart in (("x", dxs), ("b", dbm), ("c", dcm)):
        dproj, dw_c, db_c = _conv_ssm_bwd(proj, xcol0 + off, wf['conv_ssm_w'], cpre, off, dpart, dproj,
                                          xcol0 + off, "conv_ssm_bwd_" + nm)
        dws_c.append(dw_c[:wf['conv_ssm_w'].shape[0]])
        dbs_c.append(db_c[0])
        off += dpart.shape[1]
    g_cs = jnp.concatenate(dws_c, axis=1)
    gs['conv_ssm_b'] = jnp.concatenate(dbs_c, axis=0)
    ddt = jnp.pad(ddtp.reshape(t, ngrp, 128)[:, :, :HEADS_PER_GROUP].reshape(t, h_n), ((0, 0), (0, 128 - h_n))).astype(MXU_DTYPE)
    g_in = jnp.concatenate([wgrad(a_t, dproj, "wg_in"), wgrad(a_t, ddt, "wg_dt")[:, :h_n]], axis=1)
    tok = comm.send('3', {'conv_ssm_w': g_cs, 'w_in': g_in}, [(n, gs[n]) for n in REPLICATED if n != 'norm_mix_g'])
    da = _mm_nt(ddt + tok.astype(ddt.dtype), w_dt, out_dtype=F32, name="dg_dt")
    da = _mm_nt(dproj, w_main, out_dtype=ACT_DTYPE, name="dg_in", res=da)
    dx, dg = _rms_bwd(x, g_mix + tok, da, dh1m, "rms_mix_bwd", F32)
    return dx, comm.send('4', {}, [('norm_mix_g', _row0(dg)), ('loss', loss[0, 0:1])])


def kernel(x, p, norm_mix_g, w_in, ln_a_g, ln_a_b, w_s, b_s, norm_a_g, conv_ssm_w, conv_ssm_b, dt_bias, a_log, d_skip, ssm_norm_g, w_out, norm_ffn_g, w_up, conv_ffn_w, conv_ffn_b, w_down, norm_ple_g, w_ple_gate, w_ple, norm_final_g, loss_target, m_norm_mix_g, m_w_in, m_ln_a_g, m_ln_a_b, m_w_s, m_b_s, m_norm_a_g, m_conv_ssm_w, m_conv_ssm_b, m_dt_bias, m_a_log, m_d_skip, m_ssm_norm_g, m_w_out, m_norm_ffn_g, m_w_up, m_conv_ffn_w, m_conv_ffn_b, m_w_down, m_norm_ple_g, m_w_ple_gate, m_w_ple, m_norm_final_g, v_norm_mix_g, v_w_in, v_ln_a_g, v_ln_a_b, v_w_s, v_b_s, v_norm_a_g, v_conv_ssm_w, v_conv_ssm_b, v_dt_bias, v_a_log, v_d_skip, v_ssm_norm_g, v_w_out, v_norm_ffn_g, v_w_up, v_conv_ffn_w, v_conv_ffn_b, v_w_down, v_norm_ple_g, v_w_ple_gate, v_w_ple, v_norm_final_g):
    given = dict(locals())
    wts = {n: given[n] for n in WEIGHTS}
    ms = {n: given["m_" + n] for n in WEIGHTS}
    vs = {n: given["v_" + n] for n in WEIGHTS}
    sm = {n: (wts[n][0] if wts[n].ndim > 1 else wts[n]) for n in REPLICATED}
    comm = _Comm({n: wts[n][0] for n in SHARDED})
    dx, _ = _local_step(x[0], p[0, 0], loss_target[0], sm, comm)

    out, loss_out, after = {}, None, comm.last_token
    for tag, names, small_names, layout, handle in comm.sent:
        recv = _xfer_wait(handle, after, "grads_%s_wait" % tag)
        for n, parts in zip(names, recv):
            out[n] = _adamw(parts, wts[n][0], ms[n][0], vs[n][0], after, "adamw_" + n)
            after = out[n][1]
        if small_names:
            pick = lambda src, fill: _pack_small([src[n] if n in src else jnp.full((1,), fill, F32) for n in small_names])[0]
            res = _adamw(recv[-1], pick(wts, 0.0), pick(ms, 0.0), pick(vs, 1.0), after, "adamw_small_" + tag)
            after = res[1]
            res = [_unpack_small(o, layout) for o in res]
            for i, n in enumerate(small_names):
                if n == 'loss':
                    loss_out = res[0][i].reshape(())
                else:
                    out[n] = [res[k][i] for k in range(4)]
    return (loss_out, dx[None], *[out[n][k].reshape(wts[n].shape) for k in range(4) for n in WEIGHTS])


class _Comm:
    GATHER_GROUPS = {'a': ('w_in', 'conv_ssm_w'), 'b': ('w_out', 'w_up', 'conv_ffn_w', 'w_down', 'w_ple_gate', 'w_ple')}

    def __init__(self, blocks):
        wired = lambda grp: [blocks[n].astype(_wire(n)) for n in self.GATHER_GROUPS[grp]]
        self.stacks_a = _gather_two_level(wired('a'), "gather_a")
        self.handle_b, tok = _xfer_start([('gather', b) for b in wired('b')], "gather_b_start", after=self.stacks_a[0])
        self.tok0 = tok[0, 0]
        self.sent = []

    def weights(self, grp, after):
        stacks = self.stacks_a if grp == 'a' else _xfer_wait(self.handle_b, after, "gather_b_wait")
        return {n: st if n == 'w_in' else _stack_to_full(SHARDED[n], st) for n, st in zip(self.GATHER_GROUPS[grp], stacks)}

    def send(self, tag, gw, small=None, stacked=()):
        items = [('scatter', g if n in stacked else _full_to_stack(SHARDED[n], g.astype(_wire(n)))) for n, g in gw.items()]
        layout, small_names = None, []
        if small:
            packed, layout = _pack_small([a for _, a in small])
            small_names = [n for n, _ in small]
            items.append(('gather', packed))
        handle, self.last_token = _xfer_start(items, "grads_%s_start" % tag)
        self.sent.append((tag, list(gw), small_names, layout, handle))
        return self.last_token[0, 0]


def _wire(name):
    return F32 if name in F32_ON_WIRE else WIRE_DTYPE
```

```python
import jax
import jax.numpy as jnp
from jax import lax
from jax.experimental import pallas as pl
from jax.experimental.pallas import tpu as pltpu

F32 = jnp.float32
MXU_DTYPE = jnp.bfloat16
ACT_DTYPE = jnp.bfloat16
XBC_DTYPE = jnp.bfloat16
WIRE_DTYPE = jnp.bfloat16
DRES_DTYPE = jnp.bfloat16
EPS = 1e-6
CHUNK = 128
D_STATE = 128
HEAD_DIM = 64
HEADS_PER_GROUP = 4
GROUP_CH = HEAD_DIM * HEADS_PER_GROUP
HALO = 16
N_DEV = 8
VMEM_LIMIT = 56 * 1024 * 1024

ADAM_LR = 0.001
ADAM_B1 = 0.9
ADAM_B2 = 0.999
ADAM_EPS = 1e-08
ADAM_WD = 0.01
ADAM_STEP = 10

WEIGHTS = ['norm_mix_g', 'w_in', 'ln_a_g', 'ln_a_b', 'w_s', 'b_s', 'norm_a_g', 'conv_ssm_w', 'conv_ssm_b', 'dt_bias',
           'a_log', 'd_skip', 'ssm_norm_g', 'w_out', 'norm_ffn_g', 'w_up', 'conv_ffn_w', 'conv_ffn_b', 'w_down',
           'norm_ple_g', 'w_ple_gate', 'w_ple', 'norm_final_g']
SHARDED = {'w_in': 'col', 'conv_ssm_w': 'col', 'w_out': 'row', 'w_up': 'col', 'conv_ffn_w': 'col', 'w_down': 'row',
           'w_ple_gate': 'row', 'w_ple': 'col'}
F32_ON_WIRE = ('conv_ssm_w', 'conv_ffn_w')
REPLICATED = [n for n in WEIGHTS if n not in SHARDED]

S = jax.ShapeDtypeStruct


def _pick(dim, cands):
    for c in cands:
        if c <= dim and dim % c == 0:
            return c
    return dim


def _cp(sem, vmem=VMEM_LIMIT):
    return pltpu.CompilerParams(dimension_semantics=sem, vmem_limit_bytes=vmem)


def _mx(v):
    return v.astype(MXU_DTYPE)


def _rms(v, g):
    return v * lax.rsqrt(jnp.mean(v * v, axis=-1, keepdims=True) + EPS) * g


MM_VMEM_BUDGET = 42 * 1024 * 1024


def _mm_tiles(m, n, k, out_bytes, has_res, tn_cands=(512, 256, 128), k_mult=1, tm_cands=(1024, 512), tn_alts=2):
    tns = [c for c in tn_cands if c <= n and n % c == 0][:tn_alts] or [n]
    ks = k // k_mult
    best = None
    for tn in tns:
        for tm in [c for c in tm_cands if m % c == 0] or [_pick(m, (256, 128))]:
            for nk in range(1, ks // 128 + 1):
                if ks % nk or (ks // nk) % 128:
                    continue
                tk = ks // nk
                need = 2 * 2 * (tm * tk + tk * tn) + tm * tn * (4 + 2 * out_bytes + (8 if has_res else 0))
                if need <= MM_VMEM_BUDGET:
                    if best is None or (nk, -tm, -tn) < best[0]:
                        best = ((nk, -tm, -tn), (tm, tn, tk))
                    break
    return best[1] if best else (_pick(m, (512, 256, 128)), tns[0], _pick(ks, (128,)))


def _mm_body(dot, nk, has_res):
    def body(*refs):
        if has_res:
            a_ref, b_ref, r_ref, o_ref, acc_ref = refs
        else:
            a_ref, b_ref, o_ref, acc_ref = refs
            r_ref = None
        kk = pl.program_id(2)
        d = dot(a_ref[...], b_ref[...])

        def fin(acc):
            if r_ref is not None:
                acc = acc + r_ref[...]
            o_ref[...] = acc.astype(o_ref.dtype)

        if nk == 1:
            fin(d)
        else:
            @pl.when(kk == 0)
            def _():
                acc_ref[...] = d

            if nk > 2:
                @pl.when((kk > 0) & (kk < nk - 1))
                def _():
                    acc_ref[...] += d

            @pl.when(kk == nk - 1)
            def _():
                fin(acc_ref[...] + d)

    return body


def _mm_call(body, grid, a_spec, b_spec, tm, tn, m, n, out_dtype, name, args, res, out_slabs=1):
    in_specs = [a_spec, b_spec]
    if res is not None:
        in_specs.append(pl.BlockSpec((tm, tn), lambda i, j, kk: (i, j)))
        args = args + [res]
    if out_slabs == 1:
        out_spec, out_shape = pl.BlockSpec((tm, tn), lambda i, j, kk: (i, j)), S((m, n), out_dtype)
    else:
        out_spec, out_shape = pl.BlockSpec((None, tm, tn), lambda i, j, kk: (j, i, 0)), S((out_slabs, m, tn), out_dtype)
    return pl.pallas_call(
        body, grid=grid, in_specs=in_specs, out_specs=out_spec, out_shape=out_shape,
        scratch_shapes=[pltpu.VMEM((tm, tn), F32)], name=name,
        compiler_params=_cp(("parallel", "parallel", "arbitrary")))(*args)


def _mm_nn(a, b, *, out_dtype, name, res=None, b_split=1, wide=False, out_slabs=1):
    m, k = a.shape
    n = b.shape[1] if b_split == 1 else b.shape[2] * b_split
    tn_cands = (n // out_slabs,) if out_slabs > 1 else (1024, 512, 256, 128) if wide else (512, 256, 128)
    tm, tn, tk = _mm_tiles(m, n // b_split, k, jnp.dtype(out_dtype).itemsize, res is not None, tn_cands=tn_cands,
                           tn_alts=1 if wide or out_slabs > 1 else 2,
                           tm_cands=(1024, 512) if wide or out_slabs > 1 else (2048, 1024, 512))
    assert out_slabs == 1 or (tn * out_slabs == n and tn % 128 == 0)
    nk = k // tk
    njs = (n // b_split) // tn
    body = _mm_body(lambda x, y: jnp.dot(x, y, preferred_element_type=F32), nk, res is not None)
    if b_split == 1:
        b_spec = pl.BlockSpec((tk, tn), lambda i, j, kk: (kk, j))
    else:
        b_spec = pl.BlockSpec((None, tk, tn), lambda i, j, kk: (j // njs, kk, j % njs))
    return _mm_call(body, (m // tm, n // tn, nk), pl.BlockSpec((tm, tk), lambda i, j, kk: (i, kk)), b_spec,
                    tm, tn, m, n, out_dtype, name, [a, b], res, out_slabs)


def _mm_nt(a, b, *, out_dtype, name, res=None, a_split=1):
    if a_split == 1:
        m, k = a.shape
    else:
        m, k = a.shape[1], a.shape[2] * a_split
    n = b.shape[0]
    tm, tn, tk = _mm_tiles(m, n, k, jnp.dtype(out_dtype).itemsize, res is not None, k_mult=a_split,
                           tm_cands=(2048, 1024))
    nk = k // tk
    nks = nk // a_split
    body = _mm_body(lambda x, y: lax.dot_general(x, y, (((1,), (1,)), ((), ())), preferred_element_type=F32),
                    nk, res is not None)
    if a_split == 1:
        a_spec = pl.BlockSpec((tm, tk), lambda i, j, kk: (i, kk))
    else:
        a_spec = pl.BlockSpec((None, tm, tk), lambda i, j, kk: (kk // nks, i, kk % nks))
    return _mm_call(body, (m // tm, n // tn, nk), a_spec, pl.BlockSpec((tn, tk), lambda i, j, kk: (j, kk)),
                    tm, tn, m, n, out_dtype, name, [a, b], res)


def _rms_fwd(x, g, name):
    t, d = x.shape
    tr = _pick(t, (512, 256, 128))

    def body(x_ref, g_ref, o_ref, ot_ref):
        y = _rms(x_ref[...], g_ref[...]).astype(o_ref.dtype)
        o_ref[...] = y
        ot_ref[...] = y.T

    return pl.pallas_call(
        body, grid=(t // tr,),
        in_specs=[pl.BlockSpec((tr, d), lambda i: (i, 0)), pl.BlockSpec((1, d), lambda i: (0, 0))],
        out_specs=[pl.BlockSpec((tr, d), lambda i: (i, 0)), pl.BlockSpec((d, tr), lambda i: (0, i))],
        out_shape=[S((t, d), ACT_DTYPE), S((d, t), ACT_DTYPE)], name=name,
        compiler_params=_cp(("parallel",)))(x, g)


def _rms_bwd(xin, g, dn, dres, name, out_dtype):
    t, d = xin.shape
    tr = _pick(t, (256, 128))

    def body(x_ref, g_ref, dn_ref, dr_ref, dx_ref, dg_ref):
        @pl.when(pl.program_id(0) == 0)
        def _():
            dg_ref[...] = jnp.zeros_like(dg_ref)

        _, vjp = jax.vjp(_rms, x_ref[...], g_ref[...])
        dx, dg = vjp(dn_ref[...].astype(F32))
        dx_ref[...] = (dr_ref[...].astype(F32) + dx).astype(dx_ref.dtype)
        dg_ref[0:1, :] += dg

    row = pl.BlockSpec((tr, d), lambda i: (i, 0))
    return pl.pallas_call(
        body, grid=(t // tr,),
        in_specs=[row, pl.BlockSpec((1, d), lambda i: (0, 0)), row, row],
        out_specs=[row, pl.BlockSpec((8, d), lambda i: (0, 0))],
        out_shape=[S((t, d), out_dtype), S((8, d), F32)], name=name,
        compiler_params=_cp(("arbitrary",)))(xin, g, dn, dres)


def _head(h2, q, pe, tgt, gf, name):
    t, d = h2.shape
    tr = _pick(t, (256, 128))

    def f(h2v, qv, pev, gfv, tv):
        h3 = h2v + jax.nn.sigmoid(qv) * pev
        y = _rms(h3, gfv)
        return 0.5 * jnp.sum(jnp.mean(jnp.square(y - tv), axis=-1))

    def body(h2_ref, q_ref, pe_ref, t_ref, g_ref, loss_ref, dh_ref, dq_ref, dpe_ref, dg_ref):
        @pl.when(pl.program_id(0) == 0)
        def _():
            loss_ref[...] = jnp.zeros_like(loss_ref)
            dg_ref[...] = jnp.zeros_like(dg_ref)

        tv = t_ref[...]
        loss, vjp = jax.vjp(lambda a, b, c, e: f(a, b, c, e, tv), h2_ref[...], q_ref[...].astype(F32),
                            pe_ref[...].astype(F32), g_ref[...])
        dh, dq, dpe, dg = vjp(jnp.ones((), F32))
        loss_ref[...] += jnp.full(loss_ref.shape, loss, F32)
        dh_ref[...] = dh.astype(dh_ref.dtype)
        dq_ref[...] = dq.astype(dq_ref.dtype)
        dpe_ref[...] = dpe.astype(dpe_ref.dtype)
        dg_ref[0:1, :] += dg

    row = pl.BlockSpec((tr, d), lambda i: (i, 0))
    return pl.pallas_call(
        body, grid=(t // tr,),
        in_specs=[row, row, row, row, pl.BlockSpec((1, d), lambda i: (0, 0))],
        out_specs=[pl.BlockSpec((8, 128), lambda i: (0, 0)), row, row, row, pl.BlockSpec((8, d), lambda i: (0, 0))],
        out_shape=[S((8, 128), F32), S((t, d), DRES_DTYPE), S((t, d), MXU_DTYPE), S((t, d), MXU_DTYPE), S((8, d), F32)],
        name=name, compiler_params=_cp(("arbitrary",)))(h2, q, pe, tgt, gf)


def _gmlp_block(us, vs, lng, lnb, wss, bss, ng):
    n = us[0].shape[0]
    row = lax.broadcasted_iota(jnp.int32, (n, n), 0)
    col = lax.broadcasted_iota(jnp.int32, (n, n), 1)
    outs = []
    for u0, v0, lg, lb, ws, bs in zip(us, vs, lng, lnb, wss, bss):
        u = jax.nn.gelu(u0)
        v = jax.nn.gelu(v0)
        mu = jnp.mean(v, axis=-1, keepdims=True)
        var = jnp.mean(jnp.square(v - mu), axis=-1, keepdims=True)
        vn = (v - mu) * lax.rsqrt(var + EPS) * lg + lb
        w = jnp.where(row >= col, ws, 0.0)
        sg = jnp.dot(_mx(w), _mx(vn), preferred_element_type=F32) + bs
        outs.append(u * sg)
    return _rms(jnp.concatenate(outs, axis=1), ng)


def _gmlp_load(proj_ref, lng_ref, lnb_ref, ws_ref, bst_ref, d_a, ng):
    sl = lambda g: slice(CHUNK * g, CHUNK * (g + 1))
    us = [proj_ref[:, sl(g)].astype(F32) for g in range(ng)]
    vs = [proj_ref[:, d_a + CHUNK * g: d_a + CHUNK * (g + 1)].astype(F32) for g in range(ng)]
    lng = [lng_ref[:, sl(g)] for g in range(ng)]
    lnb = [lnb_ref[:, sl(g)] for g in range(ng)]
    wss = [ws_ref[g] for g in range(ng)]
    bss = [bst_ref[:, g:g + 1] for g in range(ng)]
    return us, vs, lng, lnb, wss, bss


def _gmlp_fwd(proj, ln_g, ln_b, w_s, bst, norm_g, d_mix, name):
    t = proj.shape[0]
    ng = w_s.shape[0]
    d_a = ng * CHUNK

    def body(proj_ref, lng_ref, lnb_ref, ws_ref, bst_ref, ng_ref, o_ref, ot_ref):
        args = _gmlp_load(proj_ref, lng_ref, lnb_ref, ws_ref, bst_ref, d_a, ng)
        y = _gmlp_block(*args, ng_ref[...]).astype(o_ref.dtype)
        o_ref[...] = y
        ot_ref[...] = y.T

    vec = pl.BlockSpec((1, d_a), lambda c: (0, 0))
    return pl.pallas_call(
        body, grid=(t // CHUNK,),
        in_specs=[pl.BlockSpec((CHUNK, 2 * d_a), lambda c: (c, 0)), vec, vec,
                  pl.BlockSpec((ng, CHUNK, CHUNK), lambda c: (0, 0, 0)), pl.BlockSpec((CHUNK, 128), lambda c: (0, 0)), vec],
        out_specs=[pl.BlockSpec((CHUNK, d_a), lambda c: (c, 0)), pl.BlockSpec((d_a, CHUNK), lambda c: (0, c))],
        out_shape=[S((t, d_mix), ACT_DTYPE), S((d_mix, t), ACT_DTYPE)],
        name=name, compiler_params=_cp(("parallel",)))(proj, ln_g, ln_b, w_s, bst, norm_g)


def _gmlp_bwd(proj, ln_g, ln_b, w_s, bst, norm_g, dcat, d_proj, name):
    t = proj.shape[0]
    ng = w_s.shape[0]
    d_a = ng * CHUNK

    def body(proj_ref, lng_ref, lnb_ref, ws_ref, bst_ref, ng_ref, dy_ref,
             dp_ref, dlng_ref, dlnb_ref, dws_ref, dbst_ref, dng_ref):
        @pl.when(pl.program_id(0) == 0)
        def _():
            for r in (dlng_ref, dlnb_ref, dws_ref, dbst_ref, dng_ref):
                r[...] = jnp.zeros_like(r)

        args = _gmlp_load(proj_ref, lng_ref, lnb_ref, ws_ref, bst_ref, d_a, ng)
        _, vjp = jax.vjp(_gmlp_block, *args, ng_ref[...])
        dus, dvs, dlng, dlnb, dwss, dbss, dng = vjp(dy_ref[...].astype(F32))
        lane = lax.broadcasted_iota(jnp.int32, (1, 128), 1)
        dbst = jnp.zeros((CHUNK, 128), F32)
        for g in range(ng):
            dp_ref[:, CHUNK * g:CHUNK * (g + 1)] = dus[g].astype(dp_ref.dtype)
            dp_ref[:, d_a + CHUNK * g:d_a + CHUNK * (g + 1)] = dvs[g].astype(dp_ref.dtype)
            dlng_ref[0:1, CHUNK * g:CHUNK * (g + 1)] += dlng[g]
            dlnb_ref[0:1, CHUNK * g:CHUNK * (g + 1)] += dlnb[g]
            dws_ref[g] += dwss[g]
            dbst = dbst + dbss[g] * (lane == g).astype(F32)
        dbst_ref[...] += dbst
        dng_ref[0:1, :] += dng

    vec = pl.BlockSpec((1, d_a), lambda c: (0, 0))
    acc = pl.BlockSpec((8, d_a), lambda c: (0, 0))
    wspec = pl.BlockSpec((ng, CHUNK, CHUNK), lambda c: (0, 0, 0))
    bspec = pl.BlockSpec((CHUNK, 128), lambda c: (0, 0))
    return pl.pallas_call(
        body, grid=(t // CHUNK,),
        in_specs=[pl.BlockSpec((CHUNK, 2 * d_a), lambda c: (c, 0)), vec, vec, wspec, bspec, vec,
                  pl.BlockSpec((CHUNK, d_a), lambda c: (c, 0))],
        out_specs=[pl.BlockSpec((CHUNK, 2 * d_a), lambda c: (c, 0)), acc, acc, wspec, bspec, acc],
        out_shape=[S((t, d_proj), ACT_DTYPE), S((8, d_a), F32), S((8, d_a), F32), S((ng, CHUNK, CHUNK), F32),
                   S((CHUNK, 128), F32), S((8, d_a), F32)],
        name=name, compiler_params=_cp(("arbitrary",)))(proj, ln_g, ln_b, w_s, bst, norm_g, dcat)


def _silu_grad(c):
    s = jax.nn.sigmoid(c)
    return s * (1.0 + c * (1.0 - s))


def _fill_prev_main(s_ref, prev_ref, main_ref, i, tt):
    s_ref[pl.ds(0, HALO), :] = jnp.where(i > 0, prev_ref[...].astype(F32), 0.0)
    s_ref[pl.ds(HALO, tt), :] = main_ref[...].astype(F32)


def _prev_spec(tt, tc, joff):
    return pl.BlockSpec((HALO, tc), lambda j, i: (jnp.maximum(i * (tt // HALO) - 1, 0), j + joff))


def _next_spec(tt, tc, joff, t):
    return pl.BlockSpec((HALO, tc), lambda j, i: (jnp.minimum((i + 1) * (tt // HALO), t // HALO - 1), j + joff))


CONV_RC = 32


def _conv_tiles(t, c):
    return _pick(t, (4096, 2048, 1024, 512, 256, 128)), _pick(c, (256, 128))


def _row_chunks(tt, fn, init=0):
    rc = min(CONV_RC, tt)
    return lax.fori_loop(0, tt // rc, lambda q, c: fn(pl.multiple_of(q * rc, rc), rc, c), init)


def _fold8(p):
    acc = p[0:8]
    for r in range(8, p.shape[0], 8):
        acc = acc + p[r:r + 8]
    return acc


def _taps_chunk(s_ref, w_ref, kw, r0, rc):
    xe = s_ref[pl.ds(HALO - 8 + r0, rc + 8), :]
    acc = w_ref[0:1, :] * xe[8 - (kw - 1):8 - (kw - 1) + rc]
    for k in range(1, kw):
        acc = acc + w_ref[k:k + 1, :] * xe[8 - (kw - 1) + k:8 - (kw - 1) + k + rc]
    return acc


def _conv_bwd_chunk(sd_ref, x, w_ref, kw, r0, rc, dws):
    de = sd_ref[pl.ds(r0, rc + 8), :]
    dx, out = None, list(dws)
    for j in range(kw):
        d = de[j:j + rc]
        k = kw - 1 - j
        term = w_ref[k:k + 1, :] * d
        dx = term if dx is None else dx + term
        out[k] = out[k] + _fold8(x * d)
    return dx, out


def _conv_ssm_fwd(proj, col0, w, b, name):
    t = proj.shape[0]
    kw, c = w.shape
    tt, tc = _conv_tiles(t, c)
    joff = col0 // tc
    assert col0 % tc == 0

    def body(x_ref, xp_ref, w_ref, b_ref, o_ref, c_ref, s_ref):
        _fill_prev_main(s_ref, xp_ref, x_ref, pl.program_id(1), tt)

        def chunk(r0, rc, carry):
            cpre = _taps_chunk(s_ref, w_ref, kw, r0, rc) + b_ref[...]
            o_ref[pl.ds(r0, rc), :] = jax.nn.silu(cpre).astype(o_ref.dtype)
            c_ref[pl.ds(r0, rc), :] = cpre.astype(c_ref.dtype)
            return carry

        _row_chunks(tt, chunk)

    out = pl.BlockSpec((tt, tc), lambda j, i: (i, j))
    return pl.pallas_call(
        body, grid=(c // tc, t // tt),
        in_specs=[pl.BlockSpec((tt, tc), lambda j, i: (i, j + joff)), _prev_spec(tt, tc, joff),
                  pl.BlockSpec((kw, tc), lambda j, i: (0, j)), pl.BlockSpec((1, tc), lambda j, i: (0, j))],
        out_specs=[out, out], out_shape=[S((t, c), XBC_DTYPE), S((t, c), ACT_DTYPE)],
        scratch_shapes=[pltpu.VMEM((HALO + tt, tc), F32)], name=name,
        compiler_params=_cp(("parallel", "arbitrary")))(proj, proj, w, b)


def _conv_ssm_bwd(proj, col0, w, cpre, wcol0, dact, dproj, out_col0, name):
    t = proj.shape[0]
    kw = w.shape[0]
    c = dact.shape[1]
    tt, tc = _conv_tiles(t, c)
    assert col0 % tc == 0 and wcol0 % tc == 0 and out_col0 % tc == 0
    joff, wj, oj = col0 // tc, wcol0 // tc, out_col0 // tc
    nt = t // tt

    def body(x_ref, w_ref, c_ref, cn_ref, d_ref, dn_ref, dp_in, dx_ref, dw_ref, db_ref, sd_ref):
        del dp_in
        i = pl.program_id(1)

        @pl.when(i == 0)
        def _():
            dw_ref[...] = jnp.zeros_like(dw_ref)
            db_ref[...] = jnp.zeros_like(db_ref)

        def stage(r0, rc, db):
            rows = pl.ds(r0, rc)
            d = d_ref[rows, :].astype(F32) * _silu_grad(c_ref[rows, :].astype(F32))
            sd_ref[rows, :] = d
            return db + _fold8(d)

        zero8 = jnp.zeros((8, tc), F32)
        db = _row_chunks(tt, stage, zero8)
        sd_ref[pl.ds(tt, HALO), :] = jnp.where(
            i < nt - 1, dn_ref[...].astype(F32) * _silu_grad(cn_ref[...].astype(F32)), 0.0)

        def chunk(r0, rc, dws):
            dx, dws = _conv_bwd_chunk(sd_ref, x_ref[pl.ds(r0, rc), :].astype(F32), w_ref, kw, r0, rc, dws)
            dx_ref[pl.ds(r0, rc), :] = dx.astype(dx_ref.dtype)
            return dws

        dws = _row_chunks(tt, chunk, [zero8] * kw)
        for k in range(kw):
            dw_ref[k:k + 1, :] += jnp.sum(dws[k], axis=0, keepdims=True)
        db_ref[0:1, :] += jnp.sum(db, axis=0, keepdims=True)

    acc = pl.BlockSpec((8, tc), lambda j, i: (0, j))
    return pl.pallas_call(
        body, grid=(c // tc, nt),
        in_specs=[pl.BlockSpec((tt, tc), lambda j, i: (i, j + joff)), pl.BlockSpec((kw, tc), lambda j, i: (0, j + wj)),
                  pl.BlockSpec((tt, tc), lambda j, i: (i, j + wj)), _next_spec(tt, tc, wj, t),
                  pl.BlockSpec((tt, tc), lambda j, i: (i, j)), _next_spec(tt, tc, 0, t),
                  pl.BlockSpec(memory_space=pl.ANY)],
        out_specs=[pl.BlockSpec((tt, tc), lambda j, i: (i, j + oj)), acc, acc],
        out_shape=[S(dproj.shape, dproj.dtype), S((8, c), F32), S((8, c), F32)],
        scratch_shapes=[pltpu.VMEM((tt + HALO, tc), F32)],
        input_output_aliases={6: 0}, name=name,
        compiler_params=_cp(("parallel", "arbitrary")))(proj, w, cpre, cpre, dact, dact, dproj)


def _conv_ffn_fwd(hid, w, b, name):
    t, f2 = hid.shape
    f = f2 // 2
    kw = w.shape[0]
    tt, tc = _conv_tiles(t, f)
    nj = f // tc

    def body(g_ref, gp_ref, u_ref, up_ref, wg_ref, wu_ref, bg_ref, bu_ref, o_ref, ot_ref, cv_ref, sg_ref, su_ref):
        i = pl.program_id(1)
        _fill_prev_main(sg_ref, gp_ref, g_ref, i, tt)
        _fill_prev_main(su_ref, up_ref, u_ref, i, tt)

        def chunk(r0, rc, carry):
            rows = pl.ds(r0, rc)
            gate = _taps_chunk(sg_ref, wg_ref, kw, r0, rc) + bg_ref[...]
            up = _taps_chunk(su_ref, wu_ref, kw, r0, rc) + bu_ref[...]
            o_ref[rows, :] = (jax.nn.silu(gate) * up).astype(o_ref.dtype)
            cv_ref[0, rows, :] = gate.astype(cv_ref.dtype)
            cv_ref[1, rows, :] = up.astype(cv_ref.dtype)
            return carry

        _row_chunks(tt, chunk)
        ot_ref[...] = o_ref[...].T

    return pl.pallas_call(
        body, grid=(nj, t // tt),
        in_specs=[pl.BlockSpec((tt, tc), lambda j, i: (i, j)), _prev_spec(tt, tc, 0),
                  pl.BlockSpec((tt, tc), lambda j, i: (i, j + nj)), _prev_spec(tt, tc, nj),
                  pl.BlockSpec((kw, tc), lambda j, i: (0, j)), pl.BlockSpec((kw, tc), lambda j, i: (0, j + nj)),
                  pl.BlockSpec((1, tc), lambda j, i: (0, j)), pl.BlockSpec((1, tc), lambda j, i: (0, j + nj))],
        out_specs=[pl.BlockSpec((tt, tc), lambda j, i: (i, j)), pl.BlockSpec((tc, tt), lambda j, i: (j, i)),
                   pl.BlockSpec((2, tt, tc), lambda j, i: (0, i, j))],
        out_shape=[S((t, f), ACT_DTYPE), S((f, t), ACT_DTYPE), S((2, t, f), ACT_DTYPE)],
        scratch_shapes=[pltpu.VMEM((HALO + tt, tc), F32), pltpu.VMEM((HALO + tt, tc), F32)], name=name,
        compiler_params=_cp(("parallel", "arbitrary")))(hid, hid, hid, hid, w, w, b, b)


def _conv_ffn_bwd(hid, w, cv, dact, name):
    t, f2 = hid.shape
    f = f2 // 2
    kw = w.shape[0]
    tt, tc = _conv_tiles(t, f)
    nj = f // tc
    nt = t // tt

    def body(g_ref, u_ref, wg_ref, wu_ref, cv_ref, cvn_ref, d_ref, dn_ref, dh_ref, dw_ref, db_ref, dg_ref, du_ref):
        i = pl.program_id(1)

        @pl.when(i == 0)
        def _():
            dw_ref[...] = jnp.zeros_like(dw_ref)
            db_ref[...] = jnp.zeros_like(db_ref)

        def cotangents(gate, up, dact_v):
            sg = jax.nn.sigmoid(gate)
            return dact_v * up * (sg * (1.0 + gate * (1.0 - sg))), dact_v * (gate * sg)

        def stage(r0, rc, dbs):
            rows = pl.ds(r0, rc)
            dg, du = cotangents(cv_ref[0, rows, :].astype(F32), cv_ref[1, rows, :].astype(F32), d_ref[rows, :].astype(F32))
            dg_ref[rows, :] = dg
            du_ref[rows, :] = du
            return [dbs[0] + _fold8(dg), dbs[1] + _fold8(du)]

        zero8 = jnp.zeros((8, tc), F32)
        dbs = _row_chunks(tt, stage, [zero8, zero8])
        dgn, dun = cotangents(cvn_ref[0].astype(F32), cvn_ref[1].astype(F32), dn_ref[...].astype(F32))
        dg_ref[pl.ds(tt, HALO), :] = jnp.where(i < nt - 1, dgn, 0.0)
        du_ref[pl.ds(tt, HALO), :] = jnp.where(i < nt - 1, dun, 0.0)
        for s, (sd_ref, x_ref, w_ref) in enumerate(((dg_ref, g_ref, wg_ref), (du_ref, u_ref, wu_ref))):
            def chunk(r0, rc, dws, s=s, sd_ref=sd_ref, x_ref=x_ref, w_ref=w_ref):
                dx, dws = _conv_bwd_chunk(sd_ref, x_ref[pl.ds(r0, rc), :].astype(F32), w_ref, kw, r0, rc, dws)
                dh_ref[s, pl.ds(r0, rc), :] = dx.astype(dh_ref.dtype)
                return dws

            dws = _row_chunks(tt, chunk, [zero8] * kw)
            for k in range(kw):
                dw_ref[s, k:k + 1, :] += jnp.sum(dws[k], axis=0, keepdims=True)
            db_ref[s, 0:1, :] += jnp.sum(dbs[s], axis=0, keepdims=True)

    acc = pl.BlockSpec((2, 8, tc), lambda j, i: (0, 0, j))
    dsc = pltpu.VMEM((tt + HALO, tc), F32)
    nxt = lambda j, i: (0, jnp.minimum((i + 1) * (tt // HALO), t // HALO - 1), j)
    return pl.pallas_call(
        body, grid=(nj, nt),
        in_specs=[pl.BlockSpec((tt, tc), lambda j, i: (i, j)), pl.BlockSpec((tt, tc), lambda j, i: (i, j + nj)),
                  pl.BlockSpec((kw, tc), lambda j, i: (0, j)), pl.BlockSpec((kw, tc), lambda j, i: (0, j + nj)),
                  pl.BlockSpec((2, tt, tc), lambda j, i: (0, i, j)), pl.BlockSpec((2, HALO, tc), nxt),
                  pl.BlockSpec((tt, tc), lambda j, i: (i, j)), _next_spec(tt, tc, 0, t)],
        out_specs=[pl.BlockSpec((2, tt, tc), lambda j, i: (0, i, j)), acc, acc],
        out_shape=[S((2, t, f), MXU_DTYPE), S((2, 8, f), F32), S((2, 8, f), F32)],
        scratch_shapes=[dsc, dsc], name=name,
        compiler_params=_cp(("parallel", "arbitrary")))(hid, hid, w, w, cv, cv, dact, dact)


SSD_SUB = 2
SSD_FWD_BLOCKS = 4


def _ssd_chunk(xs, bm, cm, dtraw, hin, bias, alog, dskip):
    n = bm.shape[0]
    row = lax.broadcasted_iota(jnp.int32, (n, n), 0)
    col = lax.broadcasted_iota(jnp.int32, (n, n), 1)
    causal = row >= col
    lane = lax.broadcasted_iota(jnp.int32, (1, 128), 1)
    sub = lax.broadcasted_iota(jnp.int32, (128, 1), 0)
    last = (lax.broadcasted_iota(jnp.int32, (n, 1), 0) == n - 1).astype(F32)
    dt = jax.nn.softplus(dtraw + bias)
    adt = dt * (-jnp.exp(alog))
    tri = causal.astype(F32)
    acs = jnp.dot(tri, adt, preferred_element_type=F32, precision=lax.Precision.HIGHEST)
    ch = lax.broadcasted_iota(jnp.int32, (128, GROUP_CH), 1)
    hd = lax.broadcasted_iota(jnp.int32, (128, GROUP_CH), 0) * HEAD_DIM
    expand = ((ch >= hd) & (ch < hd + HEAD_DIM)).astype(F32)
    acs_x = jnp.dot(acs, expand, preferred_element_type=F32, precision=lax.Precision.HIGHEST)
    alast_x = jnp.sum(acs_x * last, axis=0, keepdims=True)
    acs_t = acs.T
    scores = lax.dot_general(_mx(cm), _mx(bm), (((1,), (1,)), ((), ())), preferred_element_type=F32)
    yds, xts = [], []
    for r in range(HEADS_PER_GROUP):
        pick = (lane == r).astype(F32)
        acol = jnp.sum(acs * pick, axis=1, keepdims=True)
        arow = jnp.sum(acs_t * (sub == r).astype(F32), axis=0, keepdims=True)
        dtc = jnp.sum(dt * pick, axis=1, keepdims=True)
        lm = jnp.exp(jnp.where(causal, acol - arow, -1e30))
        xts.append(xs[r] * dtc)
        yds.append(jnp.dot(_mx(scores * lm), _mx(xts[r]), preferred_element_type=F32))
    xt = jnp.concatenate(xts, axis=1)
    yo = jnp.exp(acs_x) * jnp.dot(_mx(cm), _mx(hin), preferred_element_type=F32)
    st = lax.dot_general(_mx(bm), _mx(xt * jnp.exp(alast_x - acs_x)), (((0,), (0,)), ((), ())), preferred_element_type=F32)
    hout = jnp.exp(alast_x) * hin + st
    return jnp.concatenate(yds, axis=1) + yo + dskip * jnp.concatenate(xs, axis=1), hout


def _ssd_post(y, z, normg):
    return _rms(y * jax.nn.silu(z), normg)


def _ssd_block(datas, hin, consts):
    ys = []
    for data in datas:
        y, hin = _ssd_chunk(*data, hin, *consts)
        ys.append(y)
    return ys, hin


def _ssd_specs(d_ssm, ngrp, zcol0, rev, nb, sub=SSD_SUB):
    cc = (lambda c: nb - 1 - c) if rev else (lambda c: c)
    rows = sub * CHUNK
    xj, bj, cj, zj = 0, d_ssm // 128, d_ssm // 128 + ngrp, zcol0 // GROUP_CH
    const = lambda w: pl.BlockSpec((None, 8, w), lambda g, c: (g, 0, 0))
    return cc, [
        pl.BlockSpec((rows, GROUP_CH), lambda g, c: (cc(c), xj + g)),
        pl.BlockSpec((rows, 128), lambda g, c: (cc(c), bj + g)),
        pl.BlockSpec((rows, 128), lambda g, c: (cc(c), cj + g)),
        pl.BlockSpec((rows, 128), lambda g, c: (cc(c), g)),
        pl.BlockSpec((rows, GROUP_CH), lambda g, c: (cc(c), zj + g)),
        const(128), const(128), const(GROUP_CH), const(GROUP_CH)]


def _sub_rows(s):
    return slice(CHUNK * s, CHUNK * (s + 1))


def _ssd_load(x_ref, b_ref, c_ref, dt_ref, z_ref, bias_ref, alog_ref, dsk_ref, ng_ref, sub=SSD_SUB):
    datas, zs = [], []
    for s in range(sub):
        rows = _sub_rows(s)
        xs = [x_ref[rows, HEAD_DIM * r:HEAD_DIM * (r + 1)].astype(F32) for r in range(HEADS_PER_GROUP)]
        datas.append((xs, b_ref[rows, :].astype(F32), c_ref[rows, :].astype(F32), dt_ref[rows, :]))
        zs.append(z_ref[rows, :].astype(F32))
    return datas, zs, (bias_ref[0:1, :], alog_ref[0:1, :], dsk_ref[0:1, :]), ng_ref[0:1, :]


def _ssd_fwd(xbc, dtp, proj, zcol0, bias_p, alog_p, dskip_x, normg_x, cat, cat_t, name):
    t = xbc.shape[0]
    ngrp = bias_p.shape[0]
    d_ssm = ngrp * GROUP_CH
    fb = SSD_FWD_BLOCKS if t % (SSD_FWD_BLOCKS * SSD_SUB * CHUNK) == 0 else 1
    sub = fb * SSD_SUB
    rows = sub * CHUNK
    nb = t // rows
    d_a = cat.shape[1] - d_ssm
    assert d_a % GROUP_CH == 0 and zcol0 % GROUP_CH == 0 and t % rows == 0
    _, specs = _ssd_specs(d_ssm, ngrp, zcol0, False, nb, sub)

    def body(x_ref, b_ref, c_ref, dt_ref, z_ref, bias_ref, alog_ref, dsk_ref, ng_ref, cat_in, catt_in,
             yn_ref, ynt_ref, y_ref, hs_ref, h_ref):
        del cat_in, catt_in

        @pl.when(pl.program_id(1) == 0)
        def _():
            h_ref[...] = jnp.zeros_like(h_ref)

        datas, zs, consts, normg = _ssd_load(x_ref, b_ref, c_ref, dt_ref, z_ref, bias_ref, alog_ref, dsk_ref, ng_ref, sub)
        h, ys = h_ref[...], []
        for k in range(fb):
            hs_ref[k] = h
            ys_k, h = _ssd_block(datas[k * SSD_SUB:(k + 1) * SSD_SUB], h, consts)
            ys += ys_k
        for s in range(sub):
            y_ref[_sub_rows(s), :] = ys[s].astype(y_ref.dtype)
            yn = _ssd_post(ys[s], zs[s], normg).astype(yn_ref.dtype)
            yn_ref[_sub_rows(s), :] = yn
            ynt_ref[:, _sub_rows(s)] = yn.T
        h_ref[...] = h

    hbm = pl.BlockSpec(memory_space=pl.ANY)
    return pl.pallas_call(
        body, grid=(ngrp, nb), in_specs=specs + [hbm, hbm],
        out_specs=[pl.BlockSpec((rows, GROUP_CH), lambda g, c: (c, d_a // GROUP_CH + g)),
                   pl.BlockSpec((GROUP_CH, rows), lambda g, c: (d_a // GROUP_CH + g, c)),
                   pl.BlockSpec((rows, GROUP_CH), lambda g, c: (c, g)),
                   pl.BlockSpec((fb, None, D_STATE, GROUP_CH), lambda g, c: (c, g, 0, 0))],
        out_shape=[S(cat.shape, cat.dtype), S(cat_t.shape, cat_t.dtype), S((t, d_ssm), ACT_DTYPE),
                   S((nb * fb, ngrp, D_STATE, GROUP_CH), F32)],
        scratch_shapes=[pltpu.VMEM((D_STATE, GROUP_CH), F32)],
        input_output_aliases={9: 0, 10: 1}, name=name,
        compiler_params=_cp(("parallel", "arbitrary")))(xbc, xbc, xbc, dtp, proj, bias_p, alog_p, dskip_x, normg_x, cat, cat_t)


def _ssd_bwd(xbc, dtp, proj, zcol0, bias_p, alog_p, dskip_x, normg_x, hs, ypre, dcat, dproj, name):
    t = xbc.shape[0]
    ngrp = bias_p.shape[0]
    d_ssm = ngrp * GROUP_CH
    rows = SSD_SUB * CHUNK
    nb = t // rows
    d_a = dcat.shape[1] - d_ssm
    cc, specs = _ssd_specs(d_ssm, ngrp, zcol0, True, nb)

    def body(x_ref, b_ref, c_ref, dt_ref, z_ref, bias_ref, alog_ref, dsk_ref, ng_ref, hs_ref, yp_ref, dy_ref, dp_in,
             dz_ref, dx_ref, db_ref, dc_ref, ddt_ref, dbias_ref, dalog_ref, ddsk_ref, dng_ref, dh_ref):
        del dp_in

        @pl.when(pl.program_id(1) == 0)
        def _():
            dh_ref[...] = jnp.zeros_like(dh_ref)
            for r in (dbias_ref, dalog_ref, ddsk_ref, dng_ref):
                r[...] = jnp.zeros_like(r)

        datas, zs, consts, normg = _ssd_load(x_ref, b_ref, c_ref, dt_ref, z_ref, bias_ref, alog_ref, dsk_ref, ng_ref)
        dys, dng = [], jnp.zeros_like(normg)
        for s in range(SSD_SUB):
            rws = _sub_rows(s)
            _, vjp_post = jax.vjp(_ssd_post, yp_ref[rws, :].astype(F32), zs[s], normg)
            dy, dz, dg = vjp_post(dy_ref[rws, :].astype(F32))
            dys.append(dy)
            dng = dng + dg
            dz_ref[rws, :] = dz.astype(dz_ref.dtype)
        _, vjp = jax.vjp(_ssd_block, datas, hs_ref[...], consts)
        ddatas, dhin, (dbias, dalog, ddsk) = vjp((dys, dh_ref[...]))
        for s, (dxs, dbm, dcm, ddt) in enumerate(ddatas):
            rws = _sub_rows(s)
            for r in range(HEADS_PER_GROUP):
                dx_ref[rws, HEAD_DIM * r:HEAD_DIM * (r + 1)] = dxs[r].astype(dx_ref.dtype)
            db_ref[rws, :] = dbm.astype(db_ref.dtype)
            dc_ref[rws, :] = dcm.astype(dc_ref.dtype)
            ddt_ref[rws, :] = ddt
        dh_ref[...] = dhin
        dbias_ref[0:1, :] += dbias
        dalog_ref[0:1, :] += dalog
        ddsk_ref[0:1, :] += ddsk
        dng_ref[0:1, :] += dng

    acc = lambda w: pl.BlockSpec((None, 8, w), lambda g, c: (g, 0, 0))
    blk = lambda w: pl.BlockSpec((rows, w), lambda g, c: (cc(c), g))
    return pl.pallas_call(
        body, grid=(ngrp, nb),
        in_specs=specs + [pl.BlockSpec((None, None, D_STATE, GROUP_CH), lambda g, c: (cc(c), g, 0, 0)),
                          blk(GROUP_CH),
                          pl.BlockSpec((rows, GROUP_CH), lambda g, c: (cc(c), d_a // GROUP_CH + g)),
                          pl.BlockSpec(memory_space=pl.ANY)],
        out_specs=[pl.BlockSpec((rows, GROUP_CH), lambda g, c: (cc(c), zcol0 // GROUP_CH + g)),
                   blk(GROUP_CH), blk(128), blk(128), blk(128), acc(128), acc(128), acc(GROUP_CH), acc(GROUP_CH)],
        out_shape=[S(dproj.shape, dproj.dtype), S((t, d_ssm), XBC_DTYPE), S((t, ngrp * 128), XBC_DTYPE),
                   S((t, ngrp * 128), XBC_DTYPE), S((t, ngrp * 128), F32), S((ngrp, 8, 128), F32),
                   S((ngrp, 8, 128), F32), S((ngrp, 8, GROUP_CH), F32), S((ngrp, 8, GROUP_CH), F32)],
        scratch_shapes=[pltpu.VMEM((D_STATE, GROUP_CH), F32)],
        input_output_aliases={12: 0}, name=name,
        compiler_params=_cp(("parallel", "arbitrary")))(xbc, xbc, xbc, dtp, proj, bias_p, alog_p, dskip_x, normg_x, hs, ypre,
                                                        dcat, dproj)


def _adamw(parts, w, m, v, after, name):
    r, c = w.shape
    tr = _pick(r, (256, 128, 64, 32, 16, 8)) if c * 4 * 256 <= 4 * 1024 * 1024 else _pick(r, (64, 32, 16, 8))

    def body(p_ref, w_ref, m_ref, v_ref, after_ref, g_ref, d_ref, nm_ref, nv_ref):
        del after_ref
        g = p_ref[0].astype(F32)
        for k in range(1, N_DEV):
            g = g + p_ref[k].astype(F32)
        mm = ADAM_B1 * m_ref[...] + (1.0 - ADAM_B1) * g
        vv = ADAM_B2 * v_ref[...] + (1.0 - ADAM_B2) * jnp.square(g)
        m_hat = mm / (1.0 - ADAM_B1 ** ADAM_STEP)
        v_hat = vv / (1.0 - ADAM_B2 ** ADAM_STEP)
        g_ref[...] = g
        d_ref[...] = -ADAM_LR * (m_hat / (jnp.sqrt(v_hat) + ADAM_EPS) + ADAM_WD * w_ref[...])
        nm_ref[...] = mm
        nv_ref[...] = vv

    blk = pl.BlockSpec((tr, c), lambda i: (i, 0))
    return pl.pallas_call(
        body, grid=(r // tr,),
        in_specs=[pl.BlockSpec((N_DEV, tr, c), lambda i: (0, i, 0)), blk, blk, blk, pl.BlockSpec(memory_space=pl.ANY)],
        out_specs=[blk, blk, blk, blk], out_shape=[S((r, c), F32)] * 4, name=name,
        compiler_params=_cp(("parallel",)))(parts, w, m, v, after)


def _mesh_pos():
    return lax.axis_index("x"), lax.axis_index("y"), lax.axis_index("c")


def _peer(d, x, y, c):
    return (1 - x if (d >> 2) & 1 else x, 1 - y if (d >> 1) & 1 else y, 1 - c if d & 1 else c)


def _gather_two_level(blocks, name):
    n = len(blocks)

    def body(*refs):
        srcs, outs = refs[:n], refs[n:2 * n]
        send_sems, recv_sems, loc_sems = refs[2 * n:]
        x, y, c = _mesh_pos()
        lin = lambda px, py, pc: 4 * px + 2 * py + pc
        me, sibling = (x, y, c), (x, y, 1 - c)
        chips = [(1 - x, y), (x, 1 - y), (1 - x, 1 - y)]

        def copy(a, k, block, to, src=None):
            slab = outs[a].at[lin(*block)]
            return pltpu.make_async_remote_copy(
                src_ref=slab if src is None else src, dst_ref=slab, send_sem=send_sems.at[a, k],
                recv_sem=recv_sems.at[a, k], device_id=to, device_id_type=pl.DeviceIdType.MESH)

        mine = [pltpu.make_async_copy(srcs[a], outs[a].at[lin(*me)], loc_sems.at[a]) for a in range(n)]
        first = [copy(a, 0, me, sibling, src=srcs[a]) for a in range(n)]
        first += [copy(a, 1 + j, me, (*chip, c), src=srcs[a]) for j, chip in enumerate(chips) for a in range(n)]
        for cp in mine + first:
            cp.start()
        passed = []
        for j, chip in enumerate(chips):
            for a in range(n):
                copy(a, 1 + j, (*chip, c), me).wait_recv()
                passed.append(copy(a, 4 + j, (*chip, c), sibling))
                passed[-1].start()
        for a in range(n):
            copy(a, 0, sibling, me).wait_recv()
            for j, chip in enumerate(chips):
                copy(a, 4 + j, (*chip, 1 - c), me).wait_recv()
        for cp in first + passed:
            cp.wait_send()
        for cp in mine:
            cp.wait()

    hbm = pl.BlockSpec(memory_space=pl.ANY)
    return pl.pallas_call(
        body, in_specs=[hbm] * n, out_specs=[hbm] * n, out_shape=[S((N_DEV,) + b.shape, b.dtype) for b in blocks],
        scratch_shapes=[pltpu.SemaphoreType.DMA((n, N_DEV - 1)), pltpu.SemaphoreType.DMA((n, N_DEV - 1)),
                        pltpu.SemaphoreType.DMA((n,))],
        name=name, compiler_params=pltpu.CompilerParams(has_side_effects=True))(*blocks)


def _xfer_start(items, name, after=None):
    n = len(items)
    kinds = [k for k, _ in items]
    srcs = [pltpu.with_memory_space_constraint(a, pltpu.HBM) for _, a in items]
    land_shapes = [((N_DEV,) + a.shape if k == 'gather' else a.shape, a.dtype) for k, a in items]
    lands = [pltpu.with_memory_space_constraint(lax.empty(s, dt), pltpu.HBM) for s, dt in land_shapes]
    extra = [] if after is None else [after]

    def body(*refs):
        src_refs, land_refs = refs[:n], refs[n:2 * n]
        outs = refs[2 * n + len(extra):]
        sems = outs[:2 * n]
        token = outs[4 * n]
        x, y, c = _mesh_pos()
        me = 4 * x + 2 * y + c
        for a in range(n):
            for d in range(1, N_DEV):
                px, py, pc = _peer(d, x, y, c)
                src = src_refs[a] if kinds[a] == 'gather' else src_refs[a].at[4 * px + 2 * py + pc]
                pltpu.make_async_remote_copy(
                    src_ref=src, dst_ref=land_refs[a].at[me], send_sem=sems[2 * a].at[d - 1],
                    recv_sem=sems[2 * a + 1].at[d - 1], device_id=(px, py, pc),
                    device_id_type=pl.DeviceIdType.MESH).start()
        token[...] = jnp.zeros_like(token)

    hbm = pl.BlockSpec(memory_space=pltpu.HBM)
    sem = pl.BlockSpec(memory_space=pltpu.SEMAPHORE)
    out_shape = ([pltpu.SemaphoreType.DMA((N_DEV - 1,))] * (2 * n)
                 + [pltpu.HBM(a.shape, a.dtype) for a in srcs] + [pltpu.HBM(s, dt) for s, dt in land_shapes]
                 + [S((8, 128), F32)])
    res = pl.pallas_call(
        body, name=name, out_shape=out_shape,
        in_specs=[hbm] * (2 * n) + [pl.BlockSpec(memory_space=pl.ANY)] * len(extra),
        out_specs=[sem] * (2 * n) + [hbm] * (2 * n) + [pl.BlockSpec(memory_space=pltpu.VMEM)],
        input_output_aliases={**{a: 2 * n + a for a in range(n)}, **{n + a: 3 * n + a for a in range(n)}},
        compiler_params=pltpu.CompilerParams(has_side_effects=pltpu.SideEffectType.DATAFLOW_SIDE_EFFECTING),
    )(*srcs, *lands, *extra)
    return (kinds, res[:2 * n], res[2 * n:3 * n], res[3 * n:4 * n]), res[4 * n]


def _xfer_wait(handle, after, name):
    kinds, sems, src_thru, land_thru = handle
    n = len(kinds)

    def body(*refs):
        land_refs = refs[n:2 * n]
        sem_refs = refs[2 * n:4 * n]
        x, y, c = _mesh_pos()
        me = 4 * x + 2 * y + c
        for a in range(n):
            for d in range(1, N_DEV):
                slab = land_refs[a].at[me]
                cp = pltpu.make_async_remote_copy(
                    src_ref=slab, dst_ref=slab, send_sem=sem_refs[2 * a].at[d - 1], recv_sem=sem_refs[2 * a + 1].at[d - 1],
                    device_id=_peer(d, x, y, c), device_id_type=pl.DeviceIdType.MESH)
                cp.wait_send()
                cp.wait_recv()

    hbm = pl.BlockSpec(memory_space=pltpu.HBM)
    sem = pl.BlockSpec(memory_space=pltpu.SEMAPHORE)
    res = pl.pallas_call(
        body, name=name,
        out_shape=[pltpu.HBM(a.shape, a.dtype) for a in src_thru] + [pltpu.HBM(a.shape, a.dtype) for a in land_thru],
        in_specs=[hbm] * (2 * n) + [sem] * (2 * n) + [pl.BlockSpec(memory_space=pl.ANY)],
        out_specs=[hbm] * (2 * n), input_output_aliases={a: a for a in range(2 * n)},
        compiler_params=pltpu.CompilerParams(has_side_effects=pltpu.SideEffectType.DATAFLOW_SIDE_EFFECTING),
    )(*src_thru, *land_thru, *sems, after)
    x, y, c = _mesh_pos()
    me = 4 * x + 2 * y + c
    out = []
    for a in range(n):
        src = res[a]
        own = src[None] if kinds[a] == 'gather' else lax.dynamic_index_in_dim(src, me, 0, keepdims=True)
        out.append(lax.dynamic_update_index_in_dim(res[n + a], own, me, 0))
    return out


def _stack_to_full(kind, st):
    if kind == 'row':
        return st.reshape(st.shape[0] * st.shape[1], st.shape[2])
    return jnp.concatenate([st[k] for k in range(st.shape[0])], axis=1)


def _full_to_stack(kind, full):
    r, c = full.shape
    if kind == 'row':
        return full.reshape(N_DEV, r // N_DEV, c)
    w = c // N_DEV
    return jnp.stack([full[:, k * w:(k + 1) * w] for k in range(N_DEV)], axis=0)


SMALL_ROWS = 256


def _pack_small(named):
    layout = [(a.shape, a.size, -(-a.size // 1024) * 8) for a in named]
    total = -(-sum(nr for _, _, nr in layout) // SMALL_ROWS) * SMALL_ROWS * 128
    packed, off = None, 0
    for a, (_, n, nr) in zip(named, layout):
        part = jnp.pad(a.reshape(-1).astype(F32), (off, total - off - n))
        packed = part if packed is None else packed + part
        off += nr * 128
    return packed.reshape(total // 128, 128), layout


def _unpack_small(packed, layout):
    out, r0 = [], 0
    for shape, n, nr in layout:
        out.append(packed[r0:r0 + nr].reshape(-1)[:n].reshape(shape))
        r0 += nr
    return out


def _row0(acc):
    return acc[0]


def _local_step(x, p, tgt, sm, comm):
    t, d = x.shape
    h_n = sm['dt_bias'].shape[-1]
    ngrp = h_n // HEADS_PER_GROUP
    d_ssm = h_n * HEAD_DIM
    d_a = sm['ln_a_g'].shape[-1]
    d_mix = d_a + d_ssm
    d_xbc = sm['conv_ssm_b'].shape[-1]
    d_main = 2 * d_a + d_ssm + d_xbc
    assert d_xbc == d_ssm + 2 * ngrp * D_STATE and h_n <= 128
    zcol0, xcol0 = 2 * d_a, 2 * d_a + d_ssm
    vec = lambda v: v.reshape(1, -1)

    bst = jnp.pad(sm['b_s'].T, ((0, 0), (0, 128 - sm['b_s'].shape[0])))
    grp = lambda v, w: jnp.broadcast_to(jnp.pad(v.reshape(ngrp, 1, -1), ((0, 0), (0, 0), (0, w - v.size // ngrp))), (ngrp, 8, w))
    bias_p, alog_p = grp(sm['dt_bias'], 128), grp(sm['a_log'], 128)
    dskip_x = grp(jnp.repeat(sm['d_skip'], HEAD_DIM), GROUP_CH)
    normg_x = grp(sm['ssm_norm_g'], GROUP_CH)
    pad_dt = lambda v: jnp.pad(v[:, :h_n].reshape(t, ngrp, HEADS_PER_GROUP),
                               ((0, 0), (0, 0), (0, 128 - HEADS_PER_GROUP))).reshape(t, ngrp * 128)

    g_mix = vec(sm['norm_mix_g']) + comm.tok0
    a_n, a_t = _rms_fwd(x, g_mix, "rms_mix")
    wf = comm.weights('a', a_n)
    slabs = [wf['w_in'][k] for k in range(N_DEV)]
    w_main = jnp.concatenate(slabs[:-1] + [slabs[-1][:, :slabs[-1].shape[1] - h_n]], axis=1)
    w_dt = jnp.pad(slabs[-1][:, slabs[-1].shape[1] - h_n:], ((0, 0), (0, 128 - h_n)))
    proj = _mm_nn(a_n, w_main, out_dtype=ACT_DTYPE, name="mm_in")
    dtp = pad_dt(_mm_nn(a_n, w_dt, out_dtype=F32, name="mm_dt"))
    cat, cat_t = _gmlp_fwd(proj, vec(sm['ln_a_g']), vec(sm['ln_a_b']), sm['w_s'], bst, vec(sm['norm_a_g']), d_mix, "gmlp_fwd")
    xbc, cpre = _conv_ssm_fwd(proj, xcol0, wf['conv_ssm_w'], vec(sm['conv_ssm_b']), "conv_ssm_fwd")
    cat, cat_t, ypre, hs = _ssd_fwd(xbc, dtp, proj, zcol0, bias_p, alog_p, dskip_x, normg_x, cat, cat_t, "ssd_fwd")
    wf.update(comm.weights('b', hs))
    h1 = _mm_nn(cat, wf['w_out'], out_dtype=F32, name="mm_out", res=x)
    f_n, f_t = _rms_fwd(h1, vec(sm['norm_ffn_g']), "rms_ffn")
    hid = _mm_nn(f_n, wf['w_up'], out_dtype=ACT_DTYPE, name="mm_up")
    act, act_t, cv = _conv_ffn_fwd(hid, wf['conv_ffn_w'], vec(sm['conv_ffn_b']), "conv_ffn_fwd")
    h2 = _mm_nn(act, wf['w_down'], out_dtype=F32, name="mm_down", res=h1)
    r_n, r_t = _rms_fwd(h2, vec(sm['norm_ple_g']), "rms_ple")
    q = _mm_nn(r_n, wf['w_ple_gate'], out_dtype=ACT_DTYPE, name="mm_pg")
    p_m = p.astype(MXU_DTYPE)
    pe = _mm_nn(p_m, wf['w_ple'], out_dtype=ACT_DTYPE, name="mm_ple")

    loss, dh3, dq, dpe, dgf = _head(h2, q, pe, tgt, vec(sm['norm_final_g']), "head")
    wgrad = lambda act_t, g, name, **kw: _mm_nn(act_t, g, out_dtype=WIRE_DTYPE, name=name, wide=True, **kw)
    gs = {}
    gs['norm_final_g'] = _row0(dgf)
    g_ple = wgrad(p_m.T, dpe, "wg_ple")
    g_pg = wgrad(r_t, dq, "wg_pg")
    dr = _mm_nt(dq, wf['w_ple_gate'], out_dtype=ACT_DTYPE, name="dg_pg")
    dh2m, dg = _rms_bwd(h2, vec(sm['norm_ple_g']), dr, dh3, "rms_ple_bwd", DRES_DTYPE)
    gs['norm_ple_g'] = _row0(dg)
    g_down = wgrad(act_t, dh2m, "wg_down")
    tok = comm.send('1', {'w_ple': g_ple, 'w_ple_gate': g_pg, 'w_down': g_down})
    dact = _mm_nt(dh2m, wf['w_down'], out_dtype=ACT_DTYPE, name="dg_down")
    dhid, dcw, dcb = _conv_ffn_bwd(hid, wf['conv_ffn_w'] + tok, cv, dact, "conv_ffn_bwd")
    kf = wf['conv_ffn_w'].shape[0]
    g_cf = jnp.concatenate([dcw[0, :kf], dcw[1, :kf]], axis=1)
    gs['conv_ffn_b'] = jnp.concatenate([dcb[0, 0], dcb[1, 0]], axis=0)
    g_up = wgrad(f_t, dhid, "wg_up", b_split=2, out_slabs=N_DEV)
    df = _mm_nt(dhid, wf['w_up'], out_dtype=ACT_DTYPE, name="dg_up", a_split=2)
    dh1m, dg = _rms_bwd(h1, vec(sm['norm_ffn_g']), df, dh2m, "rms_ffn_bwd", DRES_DTYPE)
    gs['norm_ffn_g'] = _row0(dg)
    g_out = wgrad(cat_t, dh1m, "wg_out")
    tok = comm.send('2', {'conv_ffn_w': g_cf, 'w_up': g_up, 'w_out': g_out}, stacked=('w_up',))
    dcat = _mm_nt(dh1m, wf['w_out'], out_dtype=ACT_DTYPE, name="dg_out")
    dproj, dlng, dlnb, dws, dbst, dng = _gmlp_bwd(proj, vec(sm['ln_a_g']) + tok, vec(sm['ln_a_b']), sm['w_s'], bst,
                                                  vec(sm['norm_a_g']), dcat, d_main, "gmlp_bwd")
    gs['ln_a_g'], gs['ln_a_b'], gs['w_s'], gs['norm_a_g'] = _row0(dlng), _row0(dlnb), dws, _row0(dng)
    gs['b_s'] = dbst[:, :sm['b_s'].shape[0]].T
    dproj, dxs, dbm, dcm, ddtp, dbias, dalog, ddsk, dsng = _ssd_bwd(
        xbc, dtp, proj, zcol0, bias_p, alog_p, dskip_x, normg_x, hs, ypre, dcat, dproj, "ssd_bwd")
    gs['dt_bias'] = dbias[:, 0, :HEADS_PER_GROUP].reshape(h_n)
    gs['a_log'] = dalog[:, 0, :HEADS_PER_GROUP].reshape(h_n)
    gs['d_skip'] = ddsk[:, 0, :].reshape(h_n, HEAD_DIM).sum(axis=-1)
    gs['ssm_norm_g'] = dsng[:, 0, :].reshape(d_ssm)
    dws_c, dbs_c = [], []
    off = 0
    for nm, dpart in (("x", dxs), ("b", dbm), ("c", dcm)):
        dproj, dw_c, db_c = _conv_ssm_bwd(proj, xcol0 + off, wf['conv_ssm_w'], cpre, off, dpart, dproj,
                                          xcol0 + off, "conv_ssm_bwd_" + nm)
        dws_c.append(dw_c[:wf['conv_ssm_w'].shape[0]])
        dbs_c.append(db_c[0])
        off += dpart.shape[1]
    g_cs = jnp.concatenate(dws_c, axis=1)
    gs['conv_ssm_b'] = jnp.concatenate(dbs_c, axis=0)
    ddt = jnp.pad(ddtp.reshape(t, ngrp, 128)[:, :, :HEADS_PER_GROUP].reshape(t, h_n), ((0, 0), (0, 128 - h_n))).astype(MXU_DTYPE)
    g_in = jnp.concatenate([wgrad(a_t, dproj, "wg_in"), wgrad(a_t, ddt, "wg_dt")[:, :h_n]], axis=1)
    tok = comm.send('3', {'conv_ssm_w': g_cs, 'w_in': g_in}, [(n, gs[n]) for n in REPLICATED if n != 'norm_mix_g'])
    da = _mm_nt(ddt + tok.astype(ddt.dtype), w_dt, out_dtype=F32, name="dg_dt")
    da = _mm_nt(dproj, w_main, out_dtype=ACT_DTYPE, name="dg_in", res=da)
    dx, dg = _rms_bwd(x, g_mix + tok, da, dh1m, "rms_mix_bwd", F32)
    return dx, comm.send('4', {}, [('norm_mix_g', _row0(dg)), ('loss', loss[0, 0:1])])


def kernel(x, p, norm_mix_g, w_in, ln_a_g, ln_a_b, w_s, b_s, norm_a_g, conv_ssm_w, conv_ssm_b, dt_bias, a_log, d_skip, ssm_norm_g, w_out, norm_ffn_g, w_up, conv_ffn_w, conv_ffn_b, w_down, norm_ple_g, w_ple_gate, w_ple, norm_final_g, loss_target, m_norm_mix_g, m_w_in, m_ln_a_g, m_ln_a_b, m_w_s, m_b_s, m_norm_a_g, m_conv_ssm_w, m_conv_ssm_b, m_dt_bias, m_a_log, m_d_skip, m_ssm_norm_g, m_w_out, m_norm_ffn_g, m_w_up, m_conv_ffn_w, m_conv_ffn_b, m_w_down, m_norm_ple_g, m_w_ple_gate, m_w_ple, m_norm_final_g, v_norm_mix_g, v_w_in, v_ln_a_g, v_ln_a_b, v_w_s, v_b_s, v_norm_a_g, v_conv_ssm_w, v_conv_ssm_b, v_dt_bias, v_a_log, v_d_skip, v_ssm_norm_g, v_w_out, v_norm_ffn_g, v_w_up, v_conv_ffn_w, v_conv_ffn_b, v_w_down, v_norm_ple_g, v_w_ple_gate, v_w_ple, v_norm_final_g):
    given = dict(locals())
    wts = {n: given[n] for n in WEIGHTS}
    ms = {n: given["m_" + n] for n in WEIGHTS}
    vs = {n: given["v_" + n] for n in WEIGHTS}
    sm = {n: (wts[n][0] if wts[n].ndim > 1 else wts[n]) for n in REPLICATED}
    comm = _Comm({n: wts[n][0] for n in SHARDED})
    dx, _ = _local_step(x[0], p[0, 0], loss_target[0], sm, comm)

    out, loss_out, after = {}, None, comm.last_token
    for tag, names, small_names, layout, handle in comm.sent:
        recv = _xfer_wait(handle, after, "grads_%s_wait" % tag)
        for n, parts in zip(names, recv):
            out[n] = _adamw(parts, wts[n][0], ms[n][0], vs[n][0], after, "adamw_" + n)
            after = out[n][1]
        if small_names:
            pick = lambda src, fill: _pack_small([src[n] if n in src else jnp.full((1,), fill, F32) for n in small_names])[0]
            res = _adamw(recv[-1], pick(wts, 0.0), pick(ms, 0.0), pick(vs, 1.0), after, "adamw_small_" + tag)
            after = res[1]
            res = [_unpack_small(o, layout) for o in res]
            for i, n in enumerate(small_names):
                if n == 'loss':
                    loss_out = res[0][i].reshape(())
                else:
                    out[n] = [res[k][i] for k in range(4)]
    return (loss_out, dx[None], *[out[n][k].reshape(wts[n].shape) for k in range(4) for n in WEIGHTS])


class _Comm:
    GATHER_GROUPS = {'a': ('w_in', 'conv_ssm_w'), 'b': ('w_out', 'w_up', 'conv_ffn_w', 'w_down', 'w_ple_gate', 'w_ple')}

    def __init__(self, blocks):
        wired = lambda grp: [blocks[n].astype(_wire(n)) for n in self.GATHER_GROUPS[grp]]
        self.stacks_a = _gather_two_level(wired('a'), "gather_a")
        self.handle_b, tok = _xfer_start([('gather', b) for b in wired('b')], "gather_b_start", after=self.stacks_a[0])
        self.tok0 = tok[0, 0]
        self.sent = []

    def weights(self, grp, after):
        stacks = self.stacks_a if grp == 'a' else _xfer_wait(self.handle_b, after, "gather_b_wait")
        return {n: st if n == 'w_in' else _stack_to_full(SHARDED[n], st) for n, st in zip(self.GATHER_GROUPS[grp], stacks)}

    def send(self, tag, gw, small=None, stacked=()):
        items = [('scatter', g if n in stacked else _full_to_stack(SHARDED[n], g.astype(_wire(n)))) for n, g in gw.items()]
        layout, small_names = None, []
        if small:
            packed, layout = _pack_small([a for _, a in small])
            small_names = [n for n, _ in small]
            items.append(('gather', packed))
        handle, self.last_token = _xfer_start(items, "grads_%s_start" % tag)
        self.sent.append((tag, list(gw), small_names, layout, handle))
        return self.last_token[0, 0]


def _wire(name):
    return F32 if name in F32_ON_WIRE else WIRE_DTYPE
```

```python
import jax
import jax.numpy as jnp
from jax import lax
from jax.experimental import pallas as pl
from jax.experimental.pallas import tpu as pltpu

F32 = jnp.float32
MXU_DTYPE = jnp.bfloat16
ACT_DTYPE = jnp.bfloat16
XBC_DTYPE = jnp.bfloat16
WIRE_DTYPE = jnp.bfloat16
DRES_DTYPE = jnp.bfloat16
EPS = 1e-6
CHUNK = 128
D_STATE = 128
HEAD_DIM = 64
HEADS_PER_GROUP = 4
GROUP_CH = HEAD_DIM * HEADS_PER_GROUP
HALO = 16
N_DEV = 8
VMEM_LIMIT = 56 * 1024 * 1024

ADAM_LR = 0.001
ADAM_B1 = 0.9
ADAM_B2 = 0.999
ADAM_EPS = 1e-08
ADAM_WD = 0.01
ADAM_STEP = 10

WEIGHTS = ['norm_mix_g', 'w_in', 'ln_a_g', 'ln_a_b', 'w_s', 'b_s', 'norm_a_g', 'conv_ssm_w', 'conv_ssm_b', 'dt_bias',
           'a_log', 'd_skip', 'ssm_norm_g', 'w_out', 'norm_ffn_g', 'w_up', 'conv_ffn_w', 'conv_ffn_b', 'w_down',
           'norm_ple_g', 'w_ple_gate', 'w_ple', 'norm_final_g']
SHARDED = {'w_in': 'col', 'conv_ssm_w': 'col', 'w_out': 'row', 'w_up': 'col', 'conv_ffn_w': 'col', 'w_down': 'row',
           'w_ple_gate': 'row', 'w_ple': 'col'}
F32_ON_WIRE = ('conv_ssm_w', 'conv_ffn_w')
REPLICATED = [n for n in WEIGHTS if n not in SHARDED]

S = jax.ShapeDtypeStruct


def _pick(dim, cands):
    for c in cands:
        if c <= dim and dim % c == 0:
            return c
    return dim


def _cp(sem, vmem=VMEM_LIMIT):
    return pltpu.CompilerParams(dimension_semantics=sem, vmem_limit_bytes=vmem)


def _mx(v):
    return v.astype(MXU_DTYPE)


def _rms(v, g):
    return v * lax.rsqrt(jnp.mean(v * v, axis=-1, keepdims=True) + EPS) * g


MM_VMEM_BUDGET = 42 * 1024 * 1024


def _mm_tiles(m, n, k, out_bytes, has_res, tn_cands=(512, 256, 128), k_mult=1, tm_cands=(1024, 512), tn_alts=2):
    tns = [c for c in tn_cands if c <= n and n % c == 0][:tn_alts] or [n]
    ks = k // k_mult
    best = None
    for tn in tns:
        for tm in [c for c in tm_cands if m % c == 0] or [_pick(m, (256, 128))]:
            for nk in range(1, ks // 128 + 1):
                if ks % nk or (ks // nk) % 128:
                    continue
                tk = ks // nk
                need = 2 * 2 * (tm * tk + tk * tn) + tm * tn * (4 + 2 * out_bytes + (8 if has_res else 0))
                if need <= MM_VMEM_BUDGET:
                    if best is None or (nk, -tm, -tn) < best[0]:
                        best = ((nk, -tm, -tn), (tm, tn, tk))
                    break
    return best[1] if best else (_pick(m, (512, 256, 128)), tns[0], _pick(ks, (128,)))


def _mm_body(dot, nk, has_res):
    def body(*refs):
        if has_res:
            a_ref, b_ref, r_ref, o_ref, acc_ref = refs
        else:
            a_ref, b_ref, o_ref, acc_ref = refs
            r_ref = None
        kk = pl.program_id(2)
        d = dot(a_ref[...], b_ref[...])

        def fin(acc):
            if r_ref is not None:
                acc = acc + r_ref[...]
            o_ref[...] = acc.astype(o_ref.dtype)

        if nk == 1:
            fin(d)
        else:
            @pl.when(kk == 0)
            def _():
                acc_ref[...] = d

            if nk > 2:
                @pl.when((kk > 0) & (kk < nk - 1))
                def _():
                    acc_ref[...] += d

            @pl.when(kk == nk - 1)
            def _():
                fin(acc_ref[...] + d)

    return body


def _mm_call(body, grid, a_spec, b_spec, tm, tn, m, n, out_dtype, name, args, res, out_slabs=1):
    in_specs = [a_spec, b_spec]
    if res is not None:
        in_specs.append(pl.BlockSpec((tm, tn), lambda i, j, kk: (i, j)))
        args = args + [res]
    if out_slabs == 1:
        out_spec, out_shape = pl.BlockSpec((tm, tn), lambda i, j, kk: (i, j)), S((m, n), out_dtype)
    else:
        out_spec, out_shape = pl.BlockSpec((None, tm, tn), lambda i, j, kk: (j, i, 0)), S((out_slabs, m, tn), out_dtype)
    return pl.pallas_call(
        body, grid=grid, in_specs=in_specs, out_specs=out_spec, out_shape=out_shape,
        scratch_shapes=[pltpu.VMEM((tm, tn), F32)], name=name,
        compiler_params=_cp(("parallel", "parallel", "arbitrary")))(*args)


def _mm_nn(a, b, *, out_dtype, name, res=None, b_split=1, wide=False, out_slabs=1):
    m, k = a.shape
    n = b.shape[1] if b_split == 1 else b.shape[2] * b_split
    tn_cands = (n // out_slabs,) if out_slabs > 1 else (1024, 512, 256, 128) if wide else (512, 256, 128)
    tm, tn, tk = _mm_tiles(m, n // b_split, k, jnp.dtype(out_dtype).itemsize, res is not None, tn_cands=tn_cands,
                           tn_alts=1 if wide or out_slabs > 1 else 2,
                           tm_cands=(1024, 512) if wide or out_slabs > 1 else (2048, 1024, 512))
    assert out_slabs == 1 or (tn * out_slabs == n and tn % 128 == 0)
    nk = k // tk
    njs = (n // b_split) // tn
    body = _mm_body(lambda x, y: jnp.dot(x, y, preferred_element_type=F32), nk, res is not None)
    if b_split == 1:
        b_spec = pl.BlockSpec((tk, tn), lambda i, j, kk: (kk, j))
    else:
        b_spec = pl.BlockSpec((None, tk, tn), lambda i, j, kk: (j // njs, kk, j % njs))
    return _mm_call(body, (m // tm, n // tn, nk), pl.BlockSpec((tm, tk), lambda i, j, kk: (i, kk)), b_spec,
                    tm, tn, m, n, out_dtype, name, [a, b], res, out_slabs)


def _mm_nt(a, b, *, out_dtype, name, res=None, a_split=1):
    if a_split == 1:
        m, k = a.shape
    else:
        m, k = a.shape[1], a.shape[2] * a_split
    n = b.shape[0]
    tm, tn, tk = _mm_tiles(m, n, k, jnp.dtype(out_dtype).itemsize, res is not None, k_mult=a_split,
                           tm_cands=(2048, 1024))
    nk = k // tk
    nks = nk // a_split
    body = _mm_body(lambda x, y: lax.dot_general(x, y, (((1,), (1,)), ((), ())), preferred_element_type=F32),
                    nk, res is not None)
    if a_split == 1:
        a_spec = pl.BlockSpec((tm, tk), lambda i, j, kk: (i, kk))
    else:
        a_spec = pl.BlockSpec((None, tm, tk), lambda i, j, kk: (kk // nks, i, kk % nks))
    return _mm_call(body, (m // tm, n // tn, nk), a_spec, pl.BlockSpec((tn, tk), lambda i, j, kk: (j, kk)),
                    tm, tn, m, n, out_dtype, name, [a, b], res)


def _rms_fwd(x, g, name):
    t, d = x.shape
    tr = _pick(t, (512, 256, 128))

    def body(x_ref, g_ref, o_ref, ot_ref):
        y = _rms(x_ref[...], g_ref[...]).astype(o_ref.dtype)
        o_ref[...] = y
        ot_ref[...] = y.T

    return pl.pallas_call(
        body, grid=(t // tr,),
        in_specs=[pl.BlockSpec((tr, d), lambda i: (i, 0)), pl.BlockSpec((1, d), lambda i: (0, 0))],
        out_specs=[pl.BlockSpec((tr, d), lambda i: (i, 0)), pl.BlockSpec((d, tr), lambda i: (0, i))],
        out_shape=[S((t, d), ACT_DTYPE), S((d, t), ACT_DTYPE)], name=name,
        compiler_params=_cp(("parallel",)))(x, g)


def _rms_bwd(xin, g, dn, dres, name, out_dtype):
    t, d = xin.shape
    tr = _pick(t, (256, 128))

    def body(x_ref, g_ref, dn_ref, dr_ref, dx_ref, dg_ref):
        @pl.when(pl.program_id(0) == 0)
        def _():
            dg_ref[...] = jnp.zeros_like(dg_ref)

        _, vjp = jax.vjp(_rms, x_ref[...], g_ref[...])
        dx, dg = vjp(dn_ref[...].astype(F32))
        dx_ref[...] = (dr_ref[...].astype(F32) + dx).astype(dx_ref.dtype)
        dg_ref[0:1, :] += dg

    row = pl.BlockSpec((tr, d), lambda i: (i, 0))
    return pl.pallas_call(
        body, grid=(t // tr,),
        in_specs=[row, pl.BlockSpec((1, d), lambda i: (0, 0)), row, row],
        out_specs=[row, pl.BlockSpec((8, d), lambda i: (0, 0))],
        out_shape=[S((t, d), out_dtype), S((8, d), F32)], name=name,
        compiler_params=_cp(("arbitrary",)))(xin, g, dn, dres)


def _head(h2, q, pe, tgt, gf, name):
    t, d = h2.shape
    tr = _pick(t, (256, 128))

    def f(h2v, qv, pev, gfv, tv):
        h3 = h2v + jax.nn.sigmoid(qv) * pev
        y = _rms(h3, gfv)
        return 0.5 * jnp.sum(jnp.mean(jnp.square(y - tv), axis=-1))

    def body(h2_ref, q_ref, pe_ref, t_ref, g_ref, loss_ref, dh_ref, dq_ref, dpe_ref, dg_ref):
        @pl.when(pl.program_id(0) == 0)
        def _():
            loss_ref[...] = jnp.zeros_like(loss_ref)
            dg_ref[...] = jnp.zeros_like(dg_ref)

        tv = t_ref[...]
        loss, vjp = jax.vjp(lambda a, b, c, e: f(a, b, c, e, tv), h2_ref[...], q_ref[...].astype(F32),
                            pe_ref[...].astype(F32), g_ref[...])
        dh, dq, dpe, dg = vjp(jnp.ones((), F32))
        loss_ref[...] += jnp.full(loss_ref.shape, loss, F32)
        dh_ref[...] = dh.astype(dh_ref.dtype)
        dq_ref[...] = dq.astype(dq_ref.dtype)
        dpe_ref[...] = dpe.astype(dpe_ref.dtype)
        dg_ref[0:1, :] += dg

    row = pl.BlockSpec((tr, d), lambda i: (i, 0))
    return pl.pallas_call(
        body, grid=(t // tr,),
        in_specs=[row, row, row, row, pl.BlockSpec((1, d), lambda i: (0, 0))],
        out_specs=[pl.BlockSpec((8, 128), lambda i: (0, 0)), row, row, row, pl.BlockSpec((8, d), lambda i: (0, 0))],
        out_shape=[S((8, 128), F32), S((t, d), DRES_DTYPE), S((t, d), MXU_DTYPE), S((t, d), MXU_DTYPE), S((8, d), F32)],
        name=name, compiler_params=_cp(("arbitrary",)))(h2, q, pe, tgt, gf)


def _gmlp_block(us, vs, lng, lnb, wss, bss, ng):
    n = us[0].shape[0]
    row = lax.broadcasted_iota(jnp.int32, (n, n), 0)
    col = lax.broadcasted_iota(jnp.int32, (n, n), 1)
    outs = []
    for u0, v0, lg, lb, ws, bs in zip(us, vs, lng, lnb, wss, bss):
        u = jax.nn.gelu(u0)
        v = jax.nn.gelu(v0)
        mu = jnp.mean(v, axis=-1, keepdims=True)
        var = jnp.mean(jnp.square(v - mu), axis=-1, keepdims=True)
        vn = (v - mu) * lax.rsqrt(var + EPS) * lg + lb
        w = jnp.where(row >= col, ws, 0.0)
        sg = jnp.dot(_mx(w), _mx(vn), preferred_element_type=F32) + bs
        outs.append(u * sg)
    return _rms(jnp.concatenate(outs, axis=1), ng)


def _gmlp_load(proj_ref, lng_ref, lnb_ref, ws_ref, bst_ref, d_a, ng):
    sl = lambda g: slice(CHUNK * g, CHUNK * (g + 1))
    us = [proj_ref[:, sl(g)].astype(F32) for g in range(ng)]
    vs = [proj_ref[:, d_a + CHUNK * g: d_a + CHUNK * (g + 1)].astype(F32) for g in range(ng)]
    lng = [lng_ref[:, sl(g)] for g in range(ng)]
    lnb = [lnb_ref[:, sl(g)] for g in range(ng)]
    wss = [ws_ref[g] for g in range(ng)]
    bss = [bst_ref[:, g:g + 1] for g in range(ng)]
    return us, vs, lng, lnb, wss, bss


def _gmlp_fwd(proj, ln_g, ln_b, w_s, bst, norm_g, d_mix, name):
    t = proj.shape[0]
    ng = w_s.shape[0]
    d_a = ng * CHUNK

    def body(proj_ref, lng_ref, lnb_ref, ws_ref, bst_ref, ng_ref, o_ref, ot_ref):
        args = _gmlp_load(proj_ref, lng_ref, lnb_ref, ws_ref, bst_ref, d_a, ng)
        y = _gmlp_block(*args, ng_ref[...]).astype(o_ref.dtype)
        o_ref[...] = y
        ot_ref[...] = y.T

    vec = pl.BlockSpec((1, d_a), lambda c: (0, 0))
    return pl.pallas_call(
        body, grid=(t // CHUNK,),
        in_specs=[pl.BlockSpec((CHUNK, 2 * d_a), lambda c: (c, 0)), vec, vec,
                  pl.BlockSpec((ng, CHUNK, CHUNK), lambda c: (0, 0, 0)), pl.BlockSpec((CHUNK, 128), lambda c: (0, 0)), vec],
        out_specs=[pl.BlockSpec((CHUNK, d_a), lambda c: (c, 0)), pl.BlockSpec((d_a, CHUNK), lambda c: (0, c))],
        out_shape=[S((t, d_mix), ACT_DTYPE), S((d_mix, t), ACT_DTYPE)],
        name=name, compiler_params=_cp(("parallel",)))(proj, ln_g, ln_b, w_s, bst, norm_g)


def _gmlp_bwd(proj, ln_g, ln_b, w_s, bst, norm_g, dcat, d_proj, name):
    t = proj.shape[0]
    ng = w_s.shape[0]
    d_a = ng * CHUNK

    def body(proj_ref, lng_ref, lnb_ref, ws_ref, bst_ref, ng_ref, dy_ref,
             dp_ref, dlng_ref, dlnb_ref, dws_ref, dbst_ref, dng_ref):
        @pl.when(pl.program_id(0) == 0)
        def _():
            for r in (dlng_ref, dlnb_ref, dws_ref, dbst_ref, dng_ref):
                r[...] = jnp.zeros_like(r)

        args = _gmlp_load(proj_ref, lng_ref, lnb_ref, ws_ref, bst_ref, d_a, ng)
        _, vjp = jax.vjp(_gmlp_block, *args, ng_ref[...])
        dus, dvs, dlng, dlnb, dwss, dbss, dng = vjp(dy_ref[...].astype(F32))
        lane = lax.broadcasted_iota(jnp.int32, (1, 128), 1)
        dbst = jnp.zeros((CHUNK, 128), F32)
        for g in range(ng):
            dp_ref[:, CHUNK * g:CHUNK * (g + 1)] = dus[g].astype(dp_ref.dtype)
            dp_ref[:, d_a + CHUNK * g:d_a + CHUNK * (g + 1)] = dvs[g].astype(dp_ref.dtype)
            dlng_ref[0:1, CHUNK * g:CHUNK * (g + 1)] += dlng[g]
            dlnb_ref[0:1, CHUNK * g:CHUNK * (g + 1)] += dlnb[g]
            dws_ref[g] += dwss[g]
            dbst = dbst + dbss[g] * (lane == g).astype(F32)
        dbst_ref[...] += dbst
        dng_ref[0:1, :] += dng

    vec = pl.BlockSpec((1, d_a), lambda c: (0, 0))
    acc = pl.BlockSpec((8, d_a), lambda c: (0, 0))
    wspec = pl.BlockSpec((ng, CHUNK, CHUNK), lambda c: (0, 0, 0))
    bspec = pl.BlockSpec((CHUNK, 128), lambda c: (0, 0))
    return pl.pallas_call(
        body, grid=(t // CHUNK,),
        in_specs=[pl.BlockSpec((CHUNK, 2 * d_a), lambda c: (c, 0)), vec, vec, wspec, bspec, vec,
                  pl.BlockSpec((CHUNK, d_a), lambda c: (c, 0))],
        out_specs=[pl.BlockSpec((CHUNK, 2 * d_a), lambda c: (c, 0)), acc, acc, wspec, bspec, acc],
        out_shape=[S((t, d_proj), ACT_DTYPE), S((8, d_a), F32), S((8, d_a), F32), S((ng, CHUNK, CHUNK), F32),
                   S((CHUNK, 128), F32), S((8, d_a), F32)],
        name=name, compiler_params=_cp(("arbitrary",)))(proj, ln_g, ln_b, w_s, bst, norm_g, dcat)


def _silu_grad(c):
    s = jax.nn.sigmoid(c)
    return s * (1.0 + c * (1.0 - s))


def _fill_prev_main(s_ref, prev_ref, main_ref, i, tt):
    s_ref[pl.ds(0, HALO), :] = jnp.where(i > 0, prev_ref[...].astype(F32), 0.0)
    s_ref[pl.ds(HALO, tt), :] = main_ref[...].astype(F32)


def _prev_spec(tt, tc, joff):
    return pl.BlockSpec((HALO, tc), lambda j, i: (jnp.maximum(i * (tt // HALO) - 1, 0), j + joff))


def _next_spec(tt, tc, joff, t):
    return pl.BlockSpec((HALO, tc), lambda j, i: (jnp.minimum((i + 1) * (tt // HALO), t // HALO - 1), j + joff))


CONV_RC = 128


def _conv_tiles(t, c):
    return _pick(t, (4096, 2048, 1024, 512, 256, 128)), _pick(c, (256, 128))


def _row_chunks(tt, fn, init=0):
    rc = min(CONV_RC, tt)
    return lax.fori_loop(0, tt // rc, lambda q, c: fn(pl.multiple_of(q * rc, rc), rc, c), init)


def _fold8(p):
    acc = p[0:8]
    for r in range(8, p.shape[0], 8):
        acc = acc + p[r:r + 8]
    return acc


def _taps_chunk(s_ref, w_ref, kw, r0, rc):
    xe = s_ref[pl.ds(HALO - 8 + r0, rc + 8), :]
    acc = w_ref[0:1, :] * xe[8 - (kw - 1):8 - (kw - 1) + rc]
    for k in range(1, kw):
        acc = acc + w_ref[k:k + 1, :] * xe[8 - (kw - 1) + k:8 - (kw - 1) + k + rc]
    return acc


def _conv_bwd_chunk(sd_ref, x, w_ref, kw, r0, rc, dws):
    de = sd_ref[pl.ds(r0, rc + 8), :]
    dx, out = None, list(dws)
    for j in range(kw):
        d = de[j:j + rc]
        k = kw - 1 - j
        term = w_ref[k:k + 1, :] * d
        dx = term if dx is None else dx + term
        out[k] = out[k] + _fold8(x * d)
    return dx, out


def _conv_ssm_fwd(proj, col0, w, b, name):
    t = proj.shape[0]
    kw, c = w.shape
    tt, tc = _conv_tiles(t, c)
    joff = col0 // tc
    assert col0 % tc == 0

    def body(x_ref, xp_ref, w_ref, b_ref, o_ref, c_ref, s_ref):
        _fill_prev_main(s_ref, xp_ref, x_ref, pl.program_id(1), tt)

        def chunk(r0, rc, carry):
            cpre = _taps_chunk(s_ref, w_ref, kw, r0, rc) + b_ref[...]
            o_ref[pl.ds(r0, rc), :] = jax.nn.silu(cpre).astype(o_ref.dtype)
            c_ref[pl.ds(r0, rc), :] = cpre.astype(c_ref.dtype)
            return carry

        _row_chunks(tt, chunk)

    out = pl.BlockSpec((tt, tc), lambda j, i: (i, j))
    return pl.pallas_call(
        body, grid=(c // tc, t // tt),
        in_specs=[pl.BlockSpec((tt, tc), lambda j, i: (i, j + joff)), _prev_spec(tt, tc, joff),
                  pl.BlockSpec((kw, tc), lambda j, i: (0, j)), pl.BlockSpec((1, tc), lambda j, i: (0, j))],
        out_specs=[out, out], out_shape=[S((t, c), XBC_DTYPE), S((t, c), ACT_DTYPE)],
        scratch_shapes=[pltpu.VMEM((HALO + tt, tc), F32)], name=name,
        compiler_params=_cp(("parallel", "arbitrary")))(proj, proj, w, b)


def _conv_ssm_bwd(proj, col0, w, cpre, wcol0, dact, dproj, out_col0, name):
    t = proj.shape[0]
    kw = w.shape[0]
    c = dact.shape[1]
    tt, tc = _conv_tiles(t, c)
    assert col0 % tc == 0 and wcol0 % tc == 0 and out_col0 % tc == 0
    joff, wj, oj = col0 // tc, wcol0 // tc, out_col0 // tc
    nt = t // tt

    def body(x_ref, w_ref, c_ref, cn_ref, d_ref, dn_ref, dp_in, dx_ref, dw_ref, db_ref, sd_ref):
        del dp_in
        i = pl.program_id(1)

        @pl.when(i == 0)
        def _():
            dw_ref[...] = jnp.zeros_like(dw_ref)
            db_ref[...] = jnp.zeros_like(db_ref)

        def stage(r0, rc, db):
            rows = pl.ds(r0, rc)
            d = d_ref[rows, :].astype(F32) * _silu_grad(c_ref[rows, :].astype(F32))
            sd_ref[rows, :] = d
            return db + _fold8(d)

        zero8 = jnp.zeros((8, tc), F32)
        db = _row_chunks(tt, stage, zero8)
        sd_ref[pl.ds(tt, HALO), :] = jnp.where(
            i < nt - 1, dn_ref[...].astype(F32) * _silu_grad(cn_ref[...].astype(F32)), 0.0)

        def chunk(r0, rc, dws):
            dx, dws = _conv_bwd_chunk(sd_ref, x_ref[pl.ds(r0, rc), :].astype(F32), w_ref, kw, r0, rc, dws)
            dx_ref[pl.ds(r0, rc), :] = dx.astype(dx_ref.dtype)
            return dws

        dws = _row_chunks(tt, chunk, [zero8] * kw)
        for k in range(kw):
            dw_ref[k:k + 1, :] += jnp.sum(dws[k], axis=0, keepdims=True)
        db_ref[0:1, :] += jnp.sum(db, axis=0, keepdims=True)

    acc = pl.BlockSpec((8, tc), lambda j, i: (0, j))
    return pl.pallas_call(
        body, grid=(c // tc, nt),
        in_specs=[pl.BlockSpec((tt, tc), lambda j, i: (i, j + joff)), pl.BlockSpec((kw, tc), lambda j, i: (0, j + wj)),
                  pl.BlockSpec((tt, tc), lambda j, i: (i, j + wj)), _next_spec(tt, tc, wj, t),
                  pl.BlockSpec((tt, tc), lambda j, i: (i, j)), _next_spec(tt, tc, 0, t),
                  pl.BlockSpec(memory_space=pl.ANY)],
        out_specs=[pl.BlockSpec((tt, tc), lambda j, i: (i, j + oj)), acc, acc],
        out_shape=[S(dproj.shape, dproj.dtype), S((8, c), F32), S((8, c), F32)],
        scratch_shapes=[pltpu.VMEM((tt + HALO, tc), F32)],
        input_output_aliases={6: 0}, name=name,
        compiler_params=_cp(("parallel", "arbitrary")))(proj, w, cpre, cpre, dact, dact, dproj)


def _conv_ffn_fwd(hid, w, b, name):
    t, f2 = hid.shape
    f = f2 // 2
    kw = w.shape[0]
    tt, tc = _conv_tiles(t, f)
    nj = f // tc

    def body(g_ref, gp_ref, u_ref, up_ref, wg_ref, wu_ref, bg_ref, bu_ref, o_ref, ot_ref, cv_ref, sg_ref, su_ref):
        i = pl.program_id(1)
        _fill_prev_main(sg_ref, gp_ref, g_ref, i, tt)
        _fill_prev_main(su_ref, up_ref, u_ref, i, tt)

        def chunk(r0, rc, carry):
            rows = pl.ds(r0, rc)
            gate = _taps_chunk(sg_ref, wg_ref, kw, r0, rc) + bg_ref[...]
            up = _taps_chunk(su_ref, wu_ref, kw, r0, rc) + bu_ref[...]
            o_ref[rows, :] = (jax.nn.silu(gate) * up).astype(o_ref.dtype)
            cv_ref[0, rows, :] = gate.astype(cv_ref.dtype)
            cv_ref[1, rows, :] = up.astype(cv_ref.dtype)
            return carry

        _row_chunks(tt, chunk)
        ot_ref[...] = o_ref[...].T

    return pl.pallas_call(
        body, grid=(nj, t // tt),
        in_specs=[pl.BlockSpec((tt, tc), lambda j, i: (i, j)), _prev_spec(tt, tc, 0),
                  pl.BlockSpec((tt, tc), lambda j, i: (i, j + nj)), _prev_spec(tt, tc, nj),
                  pl.BlockSpec((kw, tc), lambda j, i: (0, j)), pl.BlockSpec((kw, tc), lambda j, i: (0, j + nj)),
                  pl.BlockSpec((1, tc), lambda j, i: (0, j)), pl.BlockSpec((1, tc), lambda j, i: (0, j + nj))],
        out_specs=[pl.BlockSpec((tt, tc), lambda j, i: (i, j)), pl.BlockSpec((tc, tt), lambda j, i: (j, i)),
                   pl.BlockSpec((2, tt, tc), lambda j, i: (0, i, j))],
        out_shape=[S((t, f), ACT_DTYPE), S((f, t), ACT_DTYPE), S((2, t, f), ACT_DTYPE)],
        scratch_shapes=[pltpu.VMEM((HALO + tt, tc), F32), pltpu.VMEM((HALO + tt, tc), F32)], name=name,
        compiler_params=_cp(("parallel", "arbitrary")))(hid, hid, hid, hid, w, w, b, b)


def _conv_ffn_bwd(hid, w, cv, dact, name):
    t, f2 = hid.shape
    f = f2 // 2
    kw = w.shape[0]
    tt, tc = _conv_tiles(t, f)
    nj = f // tc
    nt = t // tt

    def body(g_ref, u_ref, wg_ref, wu_ref, cv_ref, cvn_ref, d_ref, dn_ref, dh_ref, dw_ref, db_ref, dg_ref, du_ref):
        i = pl.program_id(1)

        @pl.when(i == 0)
        def _():
            dw_ref[...] = jnp.zeros_like(dw_ref)
            db_ref[...] = jnp.zeros_like(db_ref)

        def cotangents(gate, up, dact_v):
            sg = jax.nn.sigmoid(gate)
            return dact_v * up * (sg * (1.0 + gate * (1.0 - sg))), dact_v * (gate * sg)

        def stage(r0, rc, dbs):
            rows = pl.ds(r0, rc)
            dg, du = cotangents(cv_ref[0, rows, :].astype(F32), cv_ref[1, rows, :].astype(F32), d_ref[rows, :].astype(F32))
            dg_ref[rows, :] = dg
            du_ref[rows, :] = du
            return [dbs[0] + _fold8(dg), dbs[1] + _fold8(du)]

        zero8 = jnp.zeros((8, tc), F32)
        dbs = _row_chunks(tt, stage, [zero8, zero8])
        dgn, dun = cotangents(cvn_ref[0].astype(F32), cvn_ref[1].astype(F32), dn_ref[...].astype(F32))
        dg_ref[pl.ds(tt, HALO), :] = jnp.where(i < nt - 1, dgn, 0.0)
        du_ref[pl.ds(tt, HALO), :] = jnp.where(i < nt - 1, dun, 0.0)
        for s, (sd_ref, x_ref, w_ref) in enumerate(((dg_ref, g_ref, wg_ref), (du_ref, u_ref, wu_ref))):
            def chunk(r0, rc, dws, s=s, sd_ref=sd_ref, x_ref=x_ref, w_ref=w_ref):
                dx, dws = _conv_bwd_chunk(sd_ref, x_ref[pl.ds(r0, rc), :].astype(F32), w_ref, kw, r0, rc, dws)
                dh_ref[s, pl.ds(r0, rc), :] = dx.astype(dh_ref.dtype)
                return dws

            dws = _row_chunks(tt, chunk, [zero8] * kw)
            for k in range(kw):
                dw_ref[s, k:k + 1, :] += jnp.sum(dws[k], axis=0, keepdims=True)
            db_ref[s, 0:1, :] += jnp.sum(dbs[s], axis=0, keepdims=True)

    acc = pl.BlockSpec((2, 8, tc), lambda j, i: (0, 0, j))
    dsc = pltpu.VMEM((tt + HALO, tc), F32)
    nxt = lambda j, i: (0, jnp.minimum((i + 1) * (tt // HALO), t // HALO - 1), j)
    return pl.pallas_call(
        body, grid=(nj, nt),
        in_specs=[pl.BlockSpec((tt, tc), lambda j, i: (i, j)), pl.BlockSpec((tt, tc), lambda j, i: (i, j + nj)),
                  pl.BlockSpec((kw, tc), lambda j, i: (0, j)), pl.BlockSpec((kw, tc), lambda j, i: (0, j + nj)),
                  pl.BlockSpec((2, tt, tc), lambda j, i: (0, i, j)), pl.BlockSpec((2, HALO, tc), nxt),
                  pl.BlockSpec((tt, tc), lambda j, i: (i, j)), _next_spec(tt, tc, 0, t)],
        out_specs=[pl.BlockSpec((2, tt, tc), lambda j, i: (0, i, j)), acc, acc],
        out_shape=[S((2, t, f), MXU_DTYPE), S((2, 8, f), F32), S((2, 8, f), F32)],
        scratch_shapes=[dsc, dsc], name=name,
        compiler_params=_cp(("parallel", "arbitrary")))(hid, hid, w, w, cv, cv, dact, dact)


SSD_SUB = 2
SSD_FWD_BLOCKS = 4


def _ssd_chunk(xs, bm, cm, dtraw, hin, bias, alog, dskip):
    n = bm.shape[0]
    row = lax.broadcasted_iota(jnp.int32, (n, n), 0)
    col = lax.broadcasted_iota(jnp.int32, (n, n), 1)
    causal = row >= col
    lane = lax.broadcasted_iota(jnp.int32, (1, 128), 1)
    sub = lax.broadcasted_iota(jnp.int32, (128, 1), 0)
    last = (lax.broadcasted_iota(jnp.int32, (n, 1), 0) == n - 1).astype(F32)
    dt = jax.nn.softplus(dtraw + bias)
    adt = dt * (-jnp.exp(alog))
    tri = causal.astype(F32)
    acs = jnp.dot(tri, adt, preferred_element_type=F32, precision=lax.Precision.HIGHEST)
    ch = lax.broadcasted_iota(jnp.int32, (128, GROUP_CH), 1)
    hd = lax.broadcasted_iota(jnp.int32, (128, GROUP_CH), 0) * HEAD_DIM
    expand = ((ch >= hd) & (ch < hd + HEAD_DIM)).astype(F32)
    acs_x = jnp.dot(acs, expand, preferred_element_type=F32, precision=lax.Precision.HIGHEST)
    alast_x = jnp.sum(acs_x * last, axis=0, keepdims=True)
    acs_t = acs.T
    scores = lax.dot_general(_mx(cm), _mx(bm), (((1,), (1,)), ((), ())), preferred_element_type=F32)
    yds, xts = [], []
    for r in range(HEADS_PER_GROUP):
        pick = (lane == r).astype(F32)
        acol = jnp.sum(acs * pick, axis=1, keepdims=True)
        arow = jnp.sum(acs_t * (sub == r).astype(F32), axis=0, keepdims=True)
        dtc = jnp.sum(dt * pick, axis=1, keepdims=True)
        lm = jnp.exp(jnp.where(causal, acol - arow, -1e30))
        xts.append(xs[r] * dtc)
        yds.append(jnp.dot(_mx(scores * lm), _mx(xts[r]), preferred_element_type=F32))
    xt = jnp.concatenate(xts, axis=1)
    yo = jnp.exp(acs_x) * jnp.dot(_mx(cm), _mx(hin), preferred_element_type=F32)
    st = lax.dot_general(_mx(bm), _mx(xt * jnp.exp(alast_x - acs_x)), (((0,), (0,)), ((), ())), preferred_element_type=F32)
    hout = jnp.exp(alast_x) * hin + st
    return jnp.concatenate(yds, axis=1) + yo + dskip * jnp.concatenate(xs, axis=1), hout


def _ssd_post(y, z, normg):
    return _rms(y * jax.nn.silu(z), normg)


def _ssd_block(datas, hin, consts):
    ys = []
    for data in datas:
        y, hin = _ssd_chunk(*data, hin, *consts)
        ys.append(y)
    return ys, hin


def _ssd_specs(d_ssm, ngrp, zcol0, rev, nb, sub=SSD_SUB):
    cc = (lambda c: nb - 1 - c) if rev else (lambda c: c)
    rows = sub * CHUNK
    xj, bj, cj, zj = 0, d_ssm // 128, d_ssm // 128 + ngrp, zcol0 // GROUP_CH
    const = lambda w: pl.BlockSpec((None, 8, w), lambda g, c: (g, 0, 0))
    return cc, [
        pl.BlockSpec((rows, GROUP_CH), lambda g, c: (cc(c), xj + g)),
        pl.BlockSpec((rows, 128), lambda g, c: (cc(c), bj + g)),
        pl.BlockSpec((rows, 128), lambda g, c: (cc(c), cj + g)),
        pl.BlockSpec((rows, 128), lambda g, c: (cc(c), g)),
        pl.BlockSpec((rows, GROUP_CH), lambda g, c: (cc(c), zj + g)),
        const(128), const(128), const(GROUP_CH), const(GROUP_CH)]


def _sub_rows(s):
    return slice(CHUNK * s, CHUNK * (s + 1))


def _ssd_load(x_ref, b_ref, c_ref, dt_ref, z_ref, bias_ref, alog_ref, dsk_ref, ng_ref, sub=SSD_SUB):
    datas, zs = [], []
    for s in range(sub):
        rows = _sub_rows(s)
        xs = [x_ref[rows, HEAD_DIM * r:HEAD_DIM * (r + 1)].astype(F32) for r in range(HEADS_PER_GROUP)]
        datas.append((xs, b_ref[rows, :].astype(F32), c_ref[rows, :].astype(F32), dt_ref[rows, :]))
        zs.append(z_ref[rows, :].astype(F32))
    return datas, zs, (bias_ref[0:1, :], alog_ref[0:1, :], dsk_ref[0:1, :]), ng_ref[0:1, :]


def _ssd_fwd(xbc, dtp, proj, zcol0, bias_p, alog_p, dskip_x, normg_x, cat, cat_t, name):
    t = xbc.shape[0]
    ngrp = bias_p.shape[0]
    d_ssm = ngrp * GROUP_CH
    fb = SSD_FWD_BLOCKS if t % (SSD_FWD_BLOCKS * SSD_SUB * CHUNK) == 0 else 1
    sub = fb * SSD_SUB
    rows = sub * CHUNK
    nb = t // rows
    d_a = cat.shape[1] - d_ssm
    assert d_a % GROUP_CH == 0 and zcol0 % GROUP_CH == 0 and t % rows == 0
    _, specs = _ssd_specs(d_ssm, ngrp, zcol0, False, nb, sub)

    def body(x_ref, b_ref, c_ref, dt_ref, z_ref, bias_ref, alog_ref, dsk_ref, ng_ref, cat_in, catt_in,
             yn_ref, ynt_ref, y_ref, hs_ref, h_ref):
        del cat_in, catt_in

        @pl.when(pl.program_id(1) == 0)
        def _():
            h_ref[...] = jnp.zeros_like(h_ref)

        datas, zs, consts, normg = _ssd_load(x_ref, b_ref, c_ref, dt_ref, z_ref, bias_ref, alog_ref, dsk_ref, ng_ref, sub)
        h, ys = h_ref[...], []
        for k in range(fb):
            hs_ref[k] = h
            ys_k, h = _ssd_block(datas[k * SSD_SUB:(k + 1) * SSD_SUB], h, consts)
            ys += ys_k
        for s in range(sub):
            y_ref[_sub_rows(s), :] = ys[s].astype(y_ref.dtype)
            yn = _ssd_post(ys[s], zs[s], normg).astype(yn_ref.dtype)
            yn_ref[_sub_rows(s), :] = yn
            ynt_ref[:, _sub_rows(s)] = yn.T
        h_ref[...] = h

    hbm = pl.BlockSpec(memory_space=pl.ANY)
    return pl.pallas_call(
        body, grid=(ngrp, nb), in_specs=specs + [hbm, hbm],
        out_specs=[pl.BlockSpec((rows, GROUP_CH), lambda g, c: (c, d_a // GROUP_CH + g)),
                   pl.BlockSpec((GROUP_CH, rows), lambda g, c: (d_a // GROUP_CH + g, c)),
                   pl.BlockSpec((rows, GROUP_CH), lambda g, c: (c, g)),
                   pl.BlockSpec((fb, None, D_STATE, GROUP_CH), lambda g, c: (c, g, 0, 0))],
        out_shape=[S(cat.shape, cat.dtype), S(cat_t.shape, cat_t.dtype), S((t, d_ssm), ACT_DTYPE),
                   S((nb * fb, ngrp, D_STATE, GROUP_CH), F32)],
        scratch_shapes=[pltpu.VMEM((D_STATE, GROUP_CH), F32)],
        input_output_aliases={9: 0, 10: 1}, name=name,
        compiler_params=_cp(("parallel", "arbitrary")))(xbc, xbc, xbc, dtp, proj, bias_p, alog_p, dskip_x, normg_x, cat, cat_t)


def _ssd_bwd(xbc, dtp, proj, zcol0, bias_p, alog_p, dskip_x, normg_x, hs, ypre, dcat, dproj, name):
    t = xbc.shape[0]
    ngrp = bias_p.shape[0]
    d_ssm = ngrp * GROUP_CH
    rows = SSD_SUB * CHUNK
    nb = t // rows
    d_a = dcat.shape[1] - d_ssm
    cc, specs = _ssd_specs(d_ssm, ngrp, zcol0, True, nb)

    def body(x_ref, b_ref, c_ref, dt_ref, z_ref, bias_ref, alog_ref, dsk_ref, ng_ref, hs_ref, yp_ref, dy_ref, dp_in,
             dz_ref, dx_ref, db_ref, dc_ref, ddt_ref, dbias_ref, dalog_ref, ddsk_ref, dng_ref, dh_ref):
        del dp_in

        @pl.when(pl.program_id(1) == 0)
        def _():
            dh_ref[...] = jnp.zeros_like(dh_ref)
            for r in (dbias_ref, dalog_ref, ddsk_ref, dng_ref):
                r[...] = jnp.zeros_like(r)

        datas, zs, consts, normg = _ssd_load(x_ref, b_ref, c_ref, dt_ref, z_ref, bias_ref, alog_ref, dsk_ref, ng_ref)
        dys, dng = [], jnp.zeros_like(normg)
        for s in range(SSD_SUB):
            rws = _sub_rows(s)
            _, vjp_post = jax.vjp(_ssd_post, yp_ref[rws, :].astype(F32), zs[s], normg)
            dy, dz, dg = vjp_post(dy_ref[rws, :].astype(F32))
            dys.append(dy)
            dng = dng + dg
            dz_ref[rws, :] = dz.astype(dz_ref.dtype)
        _, vjp = jax.vjp(_ssd_block, datas, hs_ref[...], consts)
        ddatas, dhin, (dbias, dalog, ddsk) = vjp((dys, dh_ref[...]))
        for s, (dxs, dbm, dcm, ddt) in enumerate(ddatas):
            rws = _sub_rows(s)
            for r in range(HEADS_PER_GROUP):
                dx_ref[rws, HEAD_DIM * r:HEAD_DIM * (r + 1)] = dxs[r].astype(dx_ref.dtype)
            db_ref[rws, :] = dbm.astype(db_ref.dtype)
            dc_ref[rws, :] = dcm.astype(dc_ref.dtype)
            ddt_ref[rws, :] = ddt
        dh_ref[...] = dhin
        dbias_ref[0:1, :] += dbias
        dalog_ref[0:1, :] += dalog
        ddsk_ref[0:1, :] += ddsk
        dng_ref[0:1, :] += dng

    acc = lambda w: pl.BlockSpec((None, 8, w), lambda g, c: (g, 0, 0))
    blk = lambda w: pl.BlockSpec((rows, w), lambda g, c: (cc(c), g))
    return pl.pallas_call(
        body, grid=(ngrp, nb),
        in_specs=specs + [pl.BlockSpec((None, None, D_STATE, GROUP_CH), lambda g, c: (cc(c), g, 0, 0)),
                          blk(GROUP_CH),
                          pl.BlockSpec((rows, GROUP_CH), lambda g, c: (cc(c), d_a // GROUP_CH + g)),
                          pl.BlockSpec(memory_space=pl.ANY)],
        out_specs=[pl.BlockSpec((rows, GROUP_CH), lambda g, c: (cc(c), zcol0 // GROUP_CH + g)),
                   blk(GROUP_CH), blk(128), blk(128), blk(128), acc(128), acc(128), acc(GROUP_CH), acc(GROUP_CH)],
        out_shape=[S(dproj.shape, dproj.dtype), S((t, d_ssm), XBC_DTYPE), S((t, ngrp * 128), XBC_DTYPE),
                   S((t, ngrp * 128), XBC_DTYPE), S((t, ngrp * 128), F32), S((ngrp, 8, 128), F32),
                   S((ngrp, 8, 128), F32), S((ngrp, 8, GROUP_CH), F32), S((ngrp, 8, GROUP_CH), F32)],
        scratch_shapes=[pltpu.VMEM((D_STATE, GROUP_CH), F32)],
        input_output_aliases={12: 0}, name=name,
        compiler_params=_cp(("parallel", "arbitrary")))(xbc, xbc, xbc, dtp, proj, bias_p, alog_p, dskip_x, normg_x, hs, ypre,
                                                        dcat, dproj)


def _adamw(parts, w, m, v, after, name):
    r, c = w.shape
    tr = _pick(r, (256, 128, 64, 32, 16, 8)) if c * 4 * 256 <= 4 * 1024 * 1024 else _pick(r, (64, 32, 16, 8))

    def body(p_ref, w_ref, m_ref, v_ref, after_ref, g_ref, d_ref, nm_ref, nv_ref):
        del after_ref
        g = p_ref[0].astype(F32)
        for k in range(1, N_DEV):
            g = g + p_ref[k].astype(F32)
        mm = ADAM_B1 * m_ref[...] + (1.0 - ADAM_B1) * g
        vv = ADAM_B2 * v_ref[...] + (1.0 - ADAM_B2) * jnp.square(g)
        m_hat = mm / (1.0 - ADAM_B1 ** ADAM_STEP)
        v_hat = vv / (1.0 - ADAM_B2 ** ADAM_STEP)
        g_ref[...] = g
        d_ref[...] = -ADAM_LR * (m_hat / (jnp.sqrt(v_hat) + ADAM_EPS) + ADAM_WD * w_ref[...])
        nm_ref[...] = mm
        nv_ref[...] = vv

    blk = pl.BlockSpec((tr, c), lambda i: (i, 0))
    return pl.pallas_call(
        body, grid=(r // tr,),
        in_specs=[pl.BlockSpec((N_DEV, tr, c), lambda i: (0, i, 0)), blk, blk, blk, pl.BlockSpec(memory_space=pl.ANY)],
        out_specs=[blk, blk, blk, blk], out_shape=[S((r, c), F32)] * 4, name=name,
        compiler_params=_cp(("parallel",)))(parts, w, m, v, after)


def _mesh_pos():
    return lax.axis_index("x"), lax.axis_index("y"), lax.axis_index("c")


def _peer(d, x, y, c):
    return (1 - x if (d >> 2) & 1 else x, 1 - y if (d >> 1) & 1 else y, 1 - c if d & 1 else c)


def _gather_two_level(blocks, name):
    n = len(blocks)

    def body(*refs):
        srcs, outs = refs[:n], refs[n:2 * n]
        send_sems, recv_sems, loc_sems = refs[2 * n:]
        x, y, c = _mesh_pos()
        lin = lambda px, py, pc: 4 * px + 2 * py + pc
        me, sibling = (x, y, c), (x, y, 1 - c)
        chips = [(1 - x, y), (x, 1 - y), (1 - x, 1 - y)]

        def copy(a, k, block, to, src=None):
            slab = outs[a].at[lin(*block)]
            return pltpu.make_async_remote_copy(
                src_ref=slab if src is None else src, dst_ref=slab, send_sem=send_sems.at[a, k],
                recv_sem=recv_sems.at[a, k], device_id=to, device_id_type=pl.DeviceIdType.MESH)

        mine = [pltpu.make_async_copy(srcs[a], outs[a].at[lin(*me)], loc_sems.at[a]) for a in range(n)]
        first = [copy(a, 0, me, sibling, src=srcs[a]) for a in range(n)]
        first += [copy(a, 1 + j, me, (*chip, c), src=srcs[a]) for j, chip in enumerate(chips) for a in range(n)]
        for cp in mine + first:
            cp.start()
        passed = []
        for j, chip in enumerate(chips):
            for a in range(n):
                copy(a, 1 + j, (*chip, c), me).wait_recv()
                passed.append(copy(a, 4 + j, (*chip, c), sibling))
                passed[-1].start()
        for a in range(n):
            copy(a, 0, sibling, me).wait_recv()
            for j, chip in enumerate(chips):
                copy(a, 4 + j, (*chip, 1 - c), me).wait_recv()
        for cp in first + passed:
            cp.wait_send()
        for cp in mine:
            cp.wait()

    hbm = pl.BlockSpec(memory_space=pl.ANY)
    return pl.pallas_call(
        body, in_specs=[hbm] * n, out_specs=[hbm] * n, out_shape=[S((N_DEV,) + b.shape, b.dtype) for b in blocks],
        scratch_shapes=[pltpu.SemaphoreType.DMA((n, N_DEV - 1)), pltpu.SemaphoreType.DMA((n, N_DEV - 1)),
                        pltpu.SemaphoreType.DMA((n,))],
        name=name, compiler_params=pltpu.CompilerParams(has_side_effects=True))(*blocks)


def _xfer_start(items, name, after=None):
    n = len(items)
    kinds = [k for k, _ in items]
    srcs = [pltpu.with_memory_space_constraint(a, pltpu.HBM) for _, a in items]
    land_shapes = [((N_DEV,) + a.shape if k == 'gather' else a.shape, a.dtype) for k, a in items]
    lands = [pltpu.with_memory_space_constraint(lax.empty(s, dt), pltpu.HBM) for s, dt in land_shapes]
    extra = [] if after is None else [after]

    def body(*refs):
        src_refs, land_refs = refs[:n], refs[n:2 * n]
        outs = refs[2 * n + len(extra):]
        sems = outs[:2 * n]
        token = outs[4 * n]
        x, y, c = _mesh_pos()
        me = 4 * x + 2 * y + c
        for a in range(n):
            for d in range(1, N_DEV):
                px, py, pc = _peer(d, x, y, c)
                src = src_refs[a] if kinds[a] == 'gather' else src_refs[a].at[4 * px + 2 * py + pc]
                pltpu.make_async_remote_copy(
                    src_ref=src, dst_ref=land_refs[a].at[me], send_sem=sems[2 * a].at[d - 1],
                    recv_sem=sems[2 * a + 1].at[d - 1], device_id=(px, py, pc),
                    device_id_type=pl.DeviceIdType.MESH).start()
        token[...] = jnp.zeros_like(token)

    hbm = pl.BlockSpec(memory_space=pltpu.HBM)
    sem = pl.BlockSpec(memory_space=pltpu.SEMAPHORE)
    out_shape = ([pltpu.SemaphoreType.DMA((N_DEV - 1,))] * (2 * n)
                 + [pltpu.HBM(a.shape, a.dtype) for a in srcs] + [pltpu.HBM(s, dt) for s, dt in land_shapes]
                 + [S((8, 128), F32)])
    res = pl.pallas_call(
        body, name=name, out_shape=out_shape,
        in_specs=[hbm] * (2 * n) + [pl.BlockSpec(memory_space=pl.ANY)] * len(extra),
        out_specs=[sem] * (2 * n) + [hbm] * (2 * n) + [pl.BlockSpec(memory_space=pltpu.VMEM)],
        input_output_aliases={**{a: 2 * n + a for a in range(n)}, **{n + a: 3 * n + a for a in range(n)}},
        compiler_params=pltpu.CompilerParams(has_side_effects=pltpu.SideEffectType.DATAFLOW_SIDE_EFFECTING),
    )(*srcs, *lands, *extra)
    return (kinds, res[:2 * n], res[2 * n:3 * n], res[3 * n:4 * n]), res[4 * n]


def _xfer_wait(handle, after, name):
    kinds, sems, src_thru, land_thru = handle
    n = len(kinds)

    def body(*refs):
        land_refs = refs[n:2 * n]
        sem_refs = refs[2 * n:4 * n]
        x, y, c = _mesh_pos()
        me = 4 * x + 2 * y + c
        for a in range(n):
            for d in range(1, N_DEV):
                slab = land_refs[a].at[me]
                cp = pltpu.make_async_remote_copy(
                    src_ref=slab, dst_ref=slab, send_sem=sem_refs[2 * a].at[d - 1], recv_sem=sem_refs[2 * a + 1].at[d - 1],
                    device_id=_peer(d, x, y, c), device_id_type=pl.DeviceIdType.MESH)
                cp.wait_send()
                cp.wait_recv()

    hbm = pl.BlockSpec(memory_space=pltpu.HBM)
    sem = pl.BlockSpec(memory_space=pltpu.SEMAPHORE)
    res = pl.pallas_call(
        body, name=name,
        out_shape=[pltpu.HBM(a.shape, a.dtype) for a in src_thru] + [pltpu.HBM(a.shape, a.dtype) for a in land_thru],
        in_specs=[hbm] * (2 * n) + [sem] * (2 * n) + [pl.BlockSpec(memory_space=pl.ANY)],
        out_specs=[hbm] * (2 * n), input_output_aliases={a: a for a in range(2 * n)},
        compiler_params=pltpu.CompilerParams(has_side_effects=pltpu.SideEffectType.DATAFLOW_SIDE_EFFECTING),
    )(*src_thru, *land_thru, *sems, after)
    x, y, c = _mesh_pos()
    me = 4 * x + 2 * y + c
    out = []
    for a in range(n):
        src = res[a]
        own = src[None] if kinds[a] == 'gather' else lax.dynamic_index_in_dim(src, me, 0, keepdims=True)
        out.append(lax.dynamic_update_index_in_dim(res[n + a], own, me, 0))
    return out


def _stack_to_full(kind, st):
    if kind == 'row':
        return st.reshape(st.shape[0] * st.shape[1], st.shape[2])
    return jnp.concatenate([st[k] for k in range(st.shape[0])], axis=1)


def _full_to_stack(kind, full):
    r, c = full.shape
    if kind == 'row':
        return full.reshape(N_DEV, r // N_DEV, c)
    w = c // N_DEV
    return jnp.stack([full[:, k * w:(k + 1) * w] for k in range(N_DEV)], axis=0)


SMALL_ROWS = 256


def _pack_small(named):
    layout = [(a.shape, a.size, -(-a.size // 1024) * 8) for a in named]
    total = -(-sum(nr for _, _, nr in layout) // SMALL_ROWS) * SMALL_ROWS * 128
    packed, off = None, 0
    for a, (_, n, nr) in zip(named, layout):
        part = jnp.pad(a.reshape(-1).astype(F32), (off, total - off - n))
        packed = part if packed is None else packed + part
        off += nr * 128
    return packed.reshape(total // 128, 128), layout


def _unpack_small(packed, layout):
    out, r0 = [], 0
    for shape, n, nr in layout:
        out.append(packed[r0:r0 + nr].reshape(-1)[:n].reshape(shape))
        r0 += nr
    return out


def _row0(acc):
    return acc[0]


def _local_step(x, p, tgt, sm, comm):
    t, d = x.shape
    h_n = sm['dt_bias'].shape[-1]
    ngrp = h_n // HEADS_PER_GROUP
    d_ssm = h_n * HEAD_DIM
    d_a = sm['ln_a_g'].shape[-1]
    d_mix = d_a + d_ssm
    d_xbc = sm['conv_ssm_b'].shape[-1]
    d_main = 2 * d_a + d_ssm + d_xbc
    assert d_xbc == d_ssm + 2 * ngrp * D_STATE and h_n <= 128
    zcol0, xcol0 = 2 * d_a, 2 * d_a + d_ssm
    vec = lambda v: v.reshape(1, -1)

    bst = jnp.pad(sm['b_s'].T, ((0, 0), (0, 128 - sm['b_s'].shape[0])))
    grp = lambda v, w: jnp.broadcast_to(jnp.pad(v.reshape(ngrp, 1, -1), ((0, 0), (0, 0), (0, w - v.size // ngrp))), (ngrp, 8, w))
    bias_p, alog_p = grp(sm['dt_bias'], 128), grp(sm['a_log'], 128)
    dskip_x = grp(jnp.repeat(sm['d_skip'], HEAD_DIM), GROUP_CH)
    normg_x = grp(sm['ssm_norm_g'], GROUP_CH)
    pad_dt = lambda v: jnp.pad(v[:, :h_n].reshape(t, ngrp, HEADS_PER_GROUP),
                               ((0, 0), (0, 0), (0, 128 - HEADS_PER_GROUP))).reshape(t, ngrp * 128)

    g_mix = vec(sm['norm_mix_g']) + comm.tok0
    a_n, a_t = _rms_fwd(x, g_mix, "rms_mix")
    wf = comm.weights('a', a_n)
    slabs = [wf['w_in'][k] for k in range(N_DEV)]
    w_main = jnp.concatenate(slabs[:-1] + [slabs[-1][:, :slabs[-1].shape[1] - h_n]], axis=1)
    w_dt = jnp.pad(slabs[-1][:, slabs[-1].shape[1] - h_n:], ((0, 0), (0, 128 - h_n)))
    proj = _mm_nn(a_n, w_main, out_dtype=ACT_DTYPE, name="mm_in")
    dtp = pad_dt(_mm_nn(a_n, w_dt, out_dtype=F32, name="mm_dt"))
    cat, cat_t = _gmlp_fwd(proj, vec(sm['ln_a_g']), vec(sm['ln_a_b']), sm['w_s'], bst, vec(sm['norm_a_g']), d_mix, "gmlp_fwd")
    xbc, cpre = _conv_ssm_fwd(proj, xcol0, wf['conv_ssm_w'], vec(sm['conv_ssm_b']), "conv_ssm_fwd")
    cat, cat_t, ypre, hs = _ssd_fwd(xbc, dtp, proj, zcol0, bias_p, alog_p, dskip_x, normg_x, cat, cat_t, "ssd_fwd")
    wf.update(comm.weights('b', hs))
    h1 = _mm_nn(cat, wf['w_out'], out_dtype=F32, name="mm_out", res=x)
    f_n, f_t = _rms_fwd(h1, vec(sm['norm_ffn_g']), "rms_ffn")
    hid = _mm_nn(f_n, wf['w_up'], out_dtype=ACT_DTYPE, name="mm_up")
    act, act_t, cv = _conv_ffn_fwd(hid, wf['conv_ffn_w'], vec(sm['conv_ffn_b']), "conv_ffn_fwd")
    h2 = _mm_nn(act, wf['w_down'], out_dtype=F32, name="mm_down", res=h1)
    r_n, r_t = _rms_fwd(h2, vec(sm['norm_ple_g']), "rms_ple")
    q = _mm_nn(r_n, wf['w_ple_gate'], out_dtype=ACT_DTYPE, name="mm_pg")
    p_m = p.astype(MXU_DTYPE)
    pe = _mm_nn(p_m, wf['w_ple'], out_dtype=ACT_DTYPE, name="mm_ple")

    loss, dh3, dq, dpe, dgf = _head(h2, q, pe, tgt, vec(sm['norm_final_g']), "head")
    wgrad = lambda act_t, g, name, **kw: _mm_nn(act_t, g, out_dtype=WIRE_DTYPE, name=name, wide=True, **kw)
    gs = {}
    gs['norm_final_g'] = _row0(dgf)
    g_ple = wgrad(p_m.T, dpe, "wg_ple")
    g_pg = wgrad(r_t, dq, "wg_pg")
    dr = _mm_nt(dq, wf['w_ple_gate'], out_dtype=ACT_DTYPE, name="dg_pg")
    dh2m, dg = _rms_bwd(h2, vec(sm['norm_ple_g']), dr, dh3, "rms_ple_bwd", DRES_DTYPE)
    gs['norm_ple_g'] = _row0(dg)
    g_down = wgrad(act_t, dh2m, "wg_down")
    tok = comm.send('1', {'w_ple': g_ple, 'w_ple_gate': g_pg, 'w_down': g_down})
    dact = _mm_nt(dh2m, wf['w_down'], out_dtype=ACT_DTYPE, name="dg_down")
    dhid, dcw, dcb = _conv_ffn_bwd(hid, wf['conv_ffn_w'] + tok, cv, dact, "conv_ffn_bwd")
    kf = wf['conv_ffn_w'].shape[0]
    g_cf = jnp.concatenate([dcw[0, :kf], dcw[1, :kf]], axis=1)
    gs['conv_ffn_b'] = jnp.concatenate([dcb[0, 0], dcb[1, 0]], axis=0)
    g_up = wgrad(f_t, dhid, "wg_up", b_split=2, out_slabs=N_DEV)
    df = _mm_nt(dhid, wf['w_up'], out_dtype=ACT_DTYPE, name="dg_up", a_split=2)
    dh1m, dg = _rms_bwd(h1, vec(sm['norm_ffn_g']), df, dh2m, "rms_ffn_bwd", DRES_DTYPE)
    gs['norm_ffn_g'] = _row0(dg)
    g_out = wgrad(cat_t, dh1m, "wg_out")
    tok = comm.send('2', {'conv_ffn_w': g_cf, 'w_up': g_up, 'w_out': g_out}, stacked=('w_up',))
    dcat = _mm_nt(dh1m, wf['w_out'], out_dtype=ACT_DTYPE, name="dg_out")
    dproj, dlng, dlnb, dws, dbst, dng = _gmlp_bwd(proj, vec(sm['ln_a_g']) + tok, vec(sm['ln_a_b']), sm['w_s'], bst,
                                                  vec(sm['norm_a_g']), dcat, d_main, "gmlp_bwd")
    gs['ln_a_g'], gs['ln_a_b'], gs['w_s'], gs['norm_a_g'] = _row0(dlng), _row0(dlnb), dws, _row0(dng)
    gs['b_s'] = dbst[:, :sm['b_s'].shape[0]].T
    dproj, dxs, dbm, dcm, ddtp, dbias, dalog, ddsk, dsng = _ssd_bwd(
        xbc, dtp, proj, zcol0, bias_p, alog_p, dskip_x, normg_x, hs, ypre, dcat, dproj, "ssd_bwd")
    gs['dt_bias'] = dbias[:, 0, :HEADS_PER_GROUP].reshape(h_n)
    gs['a_log'] = dalog[:, 0, :HEADS_PER_GROUP].reshape(h_n)
    gs['d_skip'] = ddsk[:, 0, :].reshape(h_n, HEAD_DIM).sum(axis=-1)
    gs['ssm_norm_g'] = dsng[:, 0, :].reshape(d_ssm)
    dws_c, dbs_c = [], []
    off = 0
    for nm, dpart in (("x", dxs), ("b", dbm), ("c", dcm)):
        dproj, dw_c, db_c = _conv_ssm_bwd(proj, xcol0 + off, wf['conv_ssm_w'], cpre, off, dpart, dproj,
                                          xcol0 + off, "conv_ssm_bwd_" + nm)
        dws_c.append(dw_c[:wf['conv_ssm_w'].shape[0]])
        dbs_c.append(db_c[0])
        off += dpart.shape[1]
    g_cs = jnp.concatenate(dws_c, axis=1)
    gs['conv_ssm_b'] = jnp.concatenate(dbs_c, axis=0)
    ddt = jnp.pad(ddtp.reshape(t, ngrp, 128)[:, :, :HEADS_PER_GROUP].reshape(t, h_n), ((0, 0), (0, 128 - h_n))).astype(MXU_DTYPE)
    g_in = jnp.concatenate([wgrad(a_t, dproj, "wg_in"), wgrad(a_t, ddt, "wg_dt")[:, :h_n]], axis=1)
    tok = comm.send('3', {'conv_ssm_w': g_cs, 'w_in': g_in}, [(n, gs[n]) for n in REPLICATED if n != 'norm_mix_g'])
    da = _mm_nt(ddt + tok.astype(ddt.dtype), w_dt, out_dtype=F32, name="dg_dt")
    da = _mm_nt(dproj, w_main, out_dtype=ACT_DTYPE, name="dg_in", res=da)
    dx, dg = _rms_bwd(x, g_mix + tok, da, dh1m, "rms_mix_bwd", F32)
    return dx, comm.send('4', {}, [('norm_mix_g', _row0(dg)), ('loss', loss[0, 0:1])])


def kernel(x, p, norm_mix_g, w_in, ln_a_g, ln_a_b, w_s, b_s, norm_a_g, conv_ssm_w, conv_ssm_b, dt_bias, a_log, d_skip, ssm_norm_g, w_out, norm_ffn_g, w_up, conv_ffn_w, conv_ffn_b, w_down, norm_ple_g, w_ple_gate, w_ple, norm_final_g, loss_target, m_norm_mix_g, m_w_in, m_ln_a_g, m_ln_a_b, m_w_s, m_b_s, m_norm_a_g, m_conv_ssm_w, m_conv_ssm_b, m_dt_bias, m_a_log, m_d_skip, m_ssm_norm_g, m_w_out, m_norm_ffn_g, m_w_up, m_conv_ffn_w, m_conv_ffn_b, m_w_down, m_norm_ple_g, m_w_ple_gate, m_w_ple, m_norm_final_g, v_norm_mix_g, v_w_in, v_ln_a_g, v_ln_a_b, v_w_s, v_b_s, v_norm_a_g, v_conv_ssm_w, v_conv_ssm_b, v_dt_bias, v_a_log, v_d_skip, v_ssm_norm_g, v_w_out, v_norm_ffn_g, v_w_up, v_conv_ffn_w, v_conv_ffn_b, v_w_down, v_norm_ple_g, v_w_ple_gate, v_w_ple, v_norm_final_g):
    given = dict(locals())
    wts = {n: given[n] for n in WEIGHTS}
    ms = {n: given["m_" + n] for n in WEIGHTS}
    vs = {n: given["v_" + n] for n in WEIGHTS}
    sm = {n: (wts[n][0] if wts[n].ndim > 1 else wts[n]) for n in REPLICATED}
    comm = _Comm({n: wts[n][0] for n in SHARDED})
    dx, _ = _local_step(x[0], p[0, 0], loss_target[0], sm, comm)

    out, loss_out, after = {}, None, comm.last_token
    for tag, names, small_names, layout, handle in comm.sent:
        recv = _xfer_wait(handle, after, "grads_%s_wait" % tag)
        for n, parts in zip(names, recv):
            out[n] = _adamw(parts, wts[n][0], ms[n][0], vs[n][0], after, "adamw_" + n)
            after = out[n][1]
        if small_names:
            pick = lambda src, fill: _pack_small([src[n] if n in src else jnp.full((1,), fill, F32) for n in small_names])[0]
            res = _adamw(recv[-1], pick(wts, 0.0), pick(ms, 0.0), pick(vs, 1.0), after, "adamw_small_" + tag)
            after = res[1]
            res = [_unpack_small(o, layout) for o in res]
            for i, n in enumerate(small_names):
                if n == 'loss':
                    loss_out = res[0][i].reshape(())
                else:
                    out[n] = [res[k][i] for k in range(4)]
    return (loss_out, dx[None], *[out[n][k].reshape(wts[n].shape) for k in range(4) for n in WEIGHTS])


class _Comm:
    GATHER_GROUPS = {'a': ('w_in', 'conv_ssm_w'), 'b': ('w_out', 'w_up', 'conv_ffn_w', 'w_down', 'w_ple_gate', 'w_ple')}

    def __init__(self, blocks):
        wired = lambda grp: [blocks[n].astype(_wire(n)) for n in self.GATHER_GROUPS[grp]]
        self.stacks_a = _gather_two_level(wired('a'), "gather_a")
        self.handle_b, tok = _xfer_start([('gather', b) for b in wired('b')], "gather_b_start", after=self.stacks_a[0])
        self.tok0 = tok[0, 0]
        self.sent = []

    def weights(self, grp, after):
        stacks = self.stacks_a if grp == 'a' else _xfer_wait(self.handle_b, after, "gather_b_wait")
        return {n: st if n == 'w_in' else _stack_to_full(SHARDED[n], st) for n, st in zip(self.GATHER_GROUPS[grp], stacks)}

    def send(self, tag, gw, small=None, stacked=()):
        items = [('scatter', g if n in stacked else _full_to_stack(SHARDED[n], g.astype(_wire(n)))) for n, g in gw.items()]
        layout, small_names = None, []
        if small:
            packed, layout = _pack_small([a for _, a in small])
            small_names = [n for n, _ in small]
            items.append(('gather', packed))
        handle, self.last_token = _xfer_start(items, "grads_%s_start" % tag)
        self.sent.append((tag, list(gw), small_names, layout, handle))
        return self.last_token[0, 0]


def _wire(name):
    return F32 if name in F32_ON_WIRE else WIRE_DTYPE
```

```python
import jax
import jax.numpy as jnp
from jax import lax
from jax.experimental import pallas as pl
from jax.experimental.pallas import tpu as pltpu

F32 = jnp.float32
MXU_DTYPE = jnp.bfloat16
ACT_DTYPE = jnp.bfloat16
XBC_DTYPE = jnp.bfloat16
WIRE_DTYPE = jnp.bfloat16
DRES_DTYPE = jnp.bfloat16
EPS = 1e-6
CHUNK = 128
D_STATE = 128
HEAD_DIM = 64
HEADS_PER_GROUP = 4
GROUP_CH = HEAD_DIM * HEADS_PER_GROUP
HALO = 16
N_DEV = 8
VMEM_LIMIT = 56 * 1024 * 1024

ADAM_LR = 0.001
ADAM_B1 = 0.9
ADAM_B2 = 0.999
ADAM_EPS = 1e-08
ADAM_WD = 0.01
ADAM_STEP = 10

WEIGHTS = ['norm_mix_g', 'w_in', 'ln_a_g', 'ln_a_b', 'w_s', 'b_s', 'norm_a_g', 'conv_ssm_w', 'conv_ssm_b', 'dt_bias',
           'a_log', 'd_skip', 'ssm_norm_g', 'w_out', 'norm_ffn_g', 'w_up', 'conv_ffn_w', 'conv_ffn_b', 'w_down',
           'norm_ple_g', 'w_ple_gate', 'w_ple', 'norm_final_g']
SHARDED = {'w_in': 'col', 'conv_ssm_w': 'col', 'w_out': 'row', 'w_up': 'col', 'conv_ffn_w': 'col', 'w_down': 'row',
           'w_ple_gate': 'row', 'w_ple': 'col'}
F32_ON_WIRE = ('conv_ssm_w', 'conv_ffn_w')
REPLICATED = [n for n in WEIGHTS if n not in SHARDED]

S = jax.ShapeDtypeStruct


def _pick(dim, cands):
    for c in cands:
        if c <= dim and dim % c == 0:
            return c
    return dim


def _cp(sem, vmem=VMEM_LIMIT):
    return pltpu.CompilerParams(dimension_semantics=sem, vmem_limit_bytes=vmem)


def _mx(v):
    return v.astype(MXU_DTYPE)


def _rms(v, g):
    return v * lax.rsqrt(jnp.mean(v * v, axis=-1, keepdims=True) + EPS) * g


MM_VMEM_BUDGET = 42 * 1024 * 1024


def _mm_tiles(m, n, k, out_bytes, has_res, tn_cands=(512, 256, 128), k_mult=1, tm_cands=(1024, 512), tn_alts=2):
    tns = [c for c in tn_cands if c <= n and n % c == 0][:tn_alts] or [n]
    ks = k // k_mult
    best = None
    for tn in tns:
        for tm in [c for c in tm_cands if m % c == 0] or [_pick(m, (256, 128))]:
            for nk in range(1, ks // 128 + 1):
                if ks % nk or (ks // nk) % 128:
                    continue
                tk = ks // nk
                need = 2 * 2 * (tm * tk + tk * tn) + tm * tn * (4 + 2 * out_bytes + (8 if has_res else 0))
                if need <= MM_VMEM_BUDGET:
                    if best is None or (nk, -tm, -tn) < best[0]:
                        best = ((nk, -tm, -tn), (tm, tn, tk))
                    break
    return best[1] if best else (_pick(m, (512, 256, 128)), tns[0], _pick(ks, (128,)))


def _mm_body(dot, nk, has_res):
    def body(*refs):
        if has_res:
            a_ref, b_ref, r_ref, o_ref, acc_ref = refs
        else:
            a_ref, b_ref, o_ref, acc_ref = refs
            r_ref = None
        kk = pl.program_id(2)
        d = dot(a_ref[...], b_ref[...])

        def fin(acc):
            if r_ref is not None:
                acc = acc + r_ref[...]
            o_ref[...] = acc.astype(o_ref.dtype)

        if nk == 1:
            fin(d)
        else:
            @pl.when(kk == 0)
            def _():
                acc_ref[...] = d

            if nk > 2:
                @pl.when((kk > 0) & (kk < nk - 1))
                def _():
                    acc_ref[...] += d

            @pl.when(kk == nk - 1)
            def _():
                fin(acc_ref[...] + d)

    return body


def _mm_call(body, grid, a_spec, b_spec, tm, tn, m, n, out_dtype, name, args, res, out_slabs=1):
    in_specs = [a_spec, b_spec]
    if res is not None:
        in_specs.append(pl.BlockSpec((tm, tn), lambda i, j, kk: (i, j)))
        args = args + [res]
    if out_slabs == 1:
        out_spec, out_shape = pl.BlockSpec((tm, tn), lambda i, j, kk: (i, j)), S((m, n), out_dtype)
    else:
        out_spec, out_shape = pl.BlockSpec((None, tm, tn), lambda i, j, kk: (j, i, 0)), S((out_slabs, m, tn), out_dtype)
    return pl.pallas_call(
        body, grid=grid, in_specs=in_specs, out_specs=out_spec, out_shape=out_shape,
        scratch_shapes=[pltpu.VMEM((tm, tn), F32)], name=name,
        compiler_params=_cp(("parallel", "parallel", "arbitrary")))(*args)


def _mm_nn(a, b, *, out_dtype, name, res=None, b_split=1, wide=False, out_slabs=1):
    m, k = a.shape
    n = b.shape[1] if b_split == 1 else b.shape[2] * b_split
    tn_cands = (n // out_slabs,) if out_slabs > 1 else (1024, 512, 256, 128) if wide else (512, 256, 128)
    tm, tn, tk = _mm_tiles(m, n // b_split, k, jnp.dtype(out_dtype).itemsize, res is not None, tn_cands=tn_cands,
                           tn_alts=1 if wide or out_slabs > 1 else 2,
                           tm_cands=(1024, 512) if wide or out_slabs > 1 else (2048, 1024, 512))
    assert out_slabs == 1 or (tn * out_slabs == n and tn % 128 == 0)
    nk = k // tk
    njs = (n // b_split) // tn
    body = _mm_body(lambda x, y: jnp.dot(x, y, preferred_element_type=F32), nk, res is not None)
    if b_split == 1:
        b_spec = pl.BlockSpec((tk, tn), lambda i, j, kk: (kk, j))
    else:
        b_spec = pl.BlockSpec((None, tk, tn), lambda i, j, kk: (j // njs, kk, j % njs))
    return _mm_call(body, (m // tm, n // tn, nk), pl.BlockSpec((tm, tk), lambda i, j, kk: (i, kk)), b_spec,
                    tm, tn, m, n, out_dtype, name, [a, b], res, out_slabs)


def _mm_nt(a, b, *, out_dtype, name, res=None, a_split=1):
    if a_split == 1:
        m, k = a.shape
    else:
        m, k = a.shape[1], a.shape[2] * a_split
    n = b.shape[0]
    tm, tn, tk = _mm_tiles(m, n, k, jnp.dtype(out_dtype).itemsize, res is not None, k_mult=a_split,
                           tm_cands=(2048, 1024))
    nk = k // tk
    nks = nk // a_split
    body = _mm_body(lambda x, y: lax.dot_general(x, y, (((1,), (1,)), ((), ())), preferred_element_type=F32),
                    nk, res is not None)
    if a_split == 1:
        a_spec = pl.BlockSpec((tm, tk), lambda i, j, kk: (i, kk))
    else:
        a_spec = pl.BlockSpec((None, tm, tk), lambda i, j, kk: (kk // nks, i, kk % nks))
    return _mm_call(body, (m // tm, n // tn, nk), a_spec, pl.BlockSpec((tn, tk), lambda i, j, kk: (j, kk)),
                    tm, tn, m, n, out_dtype, name, [a, b], res)


def _rms_fwd(x, g, name):
    t, d = x.shape
    tr = _pick(t, (512, 256, 128))

    def body(x_ref, g_ref, o_ref, ot_ref):
        y = _rms(x_ref[...], g_ref[...]).astype(o_ref.dtype)
        o_ref[...] = y
        ot_ref[...] = y.T

    return pl.pallas_call(
        body, grid=(t // tr,),
        in_specs=[pl.BlockSpec((tr, d), lambda i: (i, 0)), pl.BlockSpec((1, d), lambda i: (0, 0))],
        out_specs=[pl.BlockSpec((tr, d), lambda i: (i, 0)), pl.BlockSpec((d, tr), lambda i: (0, i))],
        out_shape=[S((t, d), ACT_DTYPE), S((d, t), ACT_DTYPE)], name=name,
        compiler_params=_cp(("parallel",)))(x, g)


def _rms_bwd(xin, g, dn, dres, name, out_dtype):
    t, d = xin.shape
    tr = _pick(t, (256, 128))

    def body(x_ref, g_ref, dn_ref, dr_ref, dx_ref, dg_ref):
        @pl.when(pl.program_id(0) == 0)
        def _():
            dg_ref[...] = jnp.zeros_like(dg_ref)

        _, vjp = jax.vjp(_rms, x_ref[...], g_ref[...])
        dx, dg = vjp(dn_ref[...].astype(F32))
        dx_ref[...] = (dr_ref[...].astype(F32) + dx).astype(dx_ref.dtype)
        dg_ref[0:1, :] += dg

    row = pl.BlockSpec((tr, d), lambda i: (i, 0))
    return pl.pallas_call(
        body, grid=(t // tr,),
        in_specs=[row, pl.BlockSpec((1, d), lambda i: (0, 0)), row, row],
        out_specs=[row, pl.BlockSpec((8, d), lambda i: (0, 0))],
        out_shape=[S((t, d), out_dtype), S((8, d), F32)], name=name,
        compiler_params=_cp(("arbitrary",)))(xin, g, dn, dres)


def _head(h2, q, pe, tgt, gf, name):
    t, d = h2.shape
    tr = _pick(t, (256, 128))

    def f(h2v, qv, pev, gfv, tv):
        h3 = h2v + jax.nn.sigmoid(qv) * pev
        y = _rms(h3, gfv)
        return 0.5 * jnp.sum(jnp.mean(jnp.square(y - tv), axis=-1))

    def body(h2_ref, q_ref, pe_ref, t_ref, g_ref, loss_ref, dh_ref, dq_ref, dpe_ref, dg_ref):
        @pl.when(pl.program_id(0) == 0)
        def _():
            loss_ref[...] = jnp.zeros_like(loss_ref)
            dg_ref[...] = jnp.zeros_like(dg_ref)

        tv = t_ref[...]
        loss, vjp = jax.vjp(lambda a, b, c, e: f(a, b, c, e, tv), h2_ref[...], q_ref[...].astype(F32),
                            pe_ref[...].astype(F32), g_ref[...])
        dh, dq, dpe, dg = vjp(jnp.ones((), F32))
        loss_ref[...] += jnp.full(loss_ref.shape, loss, F32)
        dh_ref[...] = dh.astype(dh_ref.dtype)
        dq_ref[...] = dq.astype(dq_ref.dtype)
        dpe_ref[...] = dpe.astype(dpe_ref.dtype)
        dg_ref[0:1, :] += dg

    row = pl.BlockSpec((tr, d), lambda i: (i, 0))
    return pl.pallas_call(
        body, grid=(t // tr,),
        in_specs=[row, row, row, row, pl.BlockSpec((1, d), lambda i: (0, 0))],
        out_specs=[pl.BlockSpec((8, 128), lambda i: (0, 0)), row, row, row, pl.BlockSpec((8, d), lambda i: (0, 0))],
        out_shape=[S((8, 128), F32), S((t, d), DRES_DTYPE), S((t, d), MXU_DTYPE), S((t, d), MXU_DTYPE), S((8, d), F32)],
        name=name, compiler_params=_cp(("arbitrary",)))(h2, q, pe, tgt, gf)


def _gmlp_block(us, vs, lng, lnb, wss, bss, ng):
    n = us[0].shape[0]
    row = lax.broadcasted_iota(jnp.int32, (n, n), 0)
    col = lax.broadcasted_iota(jnp.int32, (n, n), 1)
    outs = []
    for u0, v0, lg, lb, ws, bs in zip(us, vs, lng, lnb, wss, bss):
        u = jax.nn.gelu(u0)
        v = jax.nn.gelu(v0)
        mu = jnp.mean(v, axis=-1, keepdims=True)
        var = jnp.mean(jnp.square(v - mu), axis=-1, keepdims=True)
        vn = (v - mu) * lax.rsqrt(var + EPS) * lg + lb
        w = jnp.where(row >= col, ws, 0.0)
        sg = jnp.dot(_mx(w), _mx(vn), preferred_element_type=F32) + bs
        outs.append(u * sg)
    return _rms(jnp.concatenate(outs, axis=1), ng)


def _gmlp_load(proj_ref, lng_ref, lnb_ref, ws_ref, bst_ref, d_a, ng):
    sl = lambda g: slice(CHUNK * g, CHUNK * (g + 1))
    us = [proj_ref[:, sl(g)].astype(F32) for g in range(ng)]
    vs = [proj_ref[:, d_a + CHUNK * g: d_a + CHUNK * (g + 1)].astype(F32) for g in range(ng)]
    lng = [lng_ref[:, sl(g)] for g in range(ng)]
    lnb = [lnb_ref[:, sl(g)] for g in range(ng)]
    wss = [ws_ref[g] for g in range(ng)]
    bss = [bst_ref[:, g:g + 1] for g in range(ng)]
    return us, vs, lng, lnb, wss, bss


def _gmlp_fwd(proj, ln_g, ln_b, w_s, bst, norm_g, d_mix, name):
    t = proj.shape[0]
    ng = w_s.shape[0]
    d_a = ng * CHUNK

    def body(proj_ref, lng_ref, lnb_ref, ws_ref, bst_ref, ng_ref, o_ref, ot_ref):
        args = _gmlp_load(proj_ref, lng_ref, lnb_ref, ws_ref, bst_ref, d_a, ng)
        y = _gmlp_block(*args, ng_ref[...]).astype(o_ref.dtype)
        o_ref[...] = y
        ot_ref[...] = y.T

    vec = pl.BlockSpec((1, d_a), lambda c: (0, 0))
    return pl.pallas_call(
        body, grid=(t // CHUNK,),
        in_specs=[pl.BlockSpec((CHUNK, 2 * d_a), lambda c: (c, 0)), vec, vec,
                  pl.BlockSpec((ng, CHUNK, CHUNK), lambda c: (0, 0, 0)), pl.BlockSpec((CHUNK, 128), lambda c: (0, 0)), vec],
        out_specs=[pl.BlockSpec((CHUNK, d_a), lambda c: (c, 0)), pl.BlockSpec((d_a, CHUNK), lambda c: (0, c))],
        out_shape=[S((t, d_mix), ACT_DTYPE), S((d_mix, t), ACT_DTYPE)],
        name=name, compiler_params=_cp(("parallel",)))(proj, ln_g, ln_b, w_s, bst, norm_g)


def _gmlp_bwd(proj, ln_g, ln_b, w_s, bst, norm_g, dcat, d_proj, name):
    t = proj.shape[0]
    ng = w_s.shape[0]
    d_a = ng * CHUNK

    def body(proj_ref, lng_ref, lnb_ref, ws_ref, bst_ref, ng_ref, dy_ref,
             dp_ref, dlng_ref, dlnb_ref, dws_ref, dbst_ref, dng_ref):
        @pl.when(pl.program_id(0) == 0)
        def _():
            for r in (dlng_ref, dlnb_ref, dws_ref, dbst_ref, dng_ref):
                r[...] = jnp.zeros_like(r)

        args = _gmlp_load(proj_ref, lng_ref, lnb_ref, ws_ref, bst_ref, d_a, ng)
        _, vjp = jax.vjp(_gmlp_block, *args, ng_ref[...])
        dus, dvs, dlng, dlnb, dwss, dbss, dng = vjp(dy_ref[...].astype(F32))
        lane = lax.broadcasted_iota(jnp.int32, (1, 128), 1)
        dbst = jnp.zeros((CHUNK, 128), F32)
        for g in range(ng):
            dp_ref[:, CHUNK * g:CHUNK * (g + 1)] = dus[g].astype(dp_ref.dtype)
            dp_ref[:, d_a + CHUNK * g:d_a + CHUNK * (g + 1)] = dvs[g].astype(dp_ref.dtype)
            dlng_ref[0:1, CHUNK * g:CHUNK * (g + 1)] += dlng[g]
            dlnb_ref[0:1, CHUNK * g:CHUNK * (g + 1)] += dlnb[g]
            dws_ref[g] += dwss[g]
            dbst = dbst + dbss[g] * (lane == g).astype(F32)
        dbst_ref[...] += dbst
        dng_ref[0:1, :] += dng

    vec = pl.BlockSpec((1, d_a), lambda c: (0, 0))
    acc = pl.BlockSpec((8, d_a), lambda c: (0, 0))
    wspec = pl.BlockSpec((ng, CHUNK, CHUNK), lambda c: (0, 0, 0))
    bspec = pl.BlockSpec((CHUNK, 128), lambda c: (0, 0))
    return pl.pallas_call(
        body, grid=(t // CHUNK,),
        in_specs=[pl.BlockSpec((CHUNK, 2 * d_a), lambda c: (c, 0)), vec, vec, wspec, bspec, vec,
                  pl.BlockSpec((CHUNK, d_a), lambda c: (c, 0))],
        out_specs=[pl.BlockSpec((CHUNK, 2 * d_a), lambda c: (c, 0)), acc, acc, wspec, bspec, acc],
        out_shape=[S((t, d_proj), ACT_DTYPE), S((8, d_a), F32), S((8, d_a), F32), S((ng, CHUNK, CHUNK), F32),
                   S((CHUNK, 128), F32), S((8, d_a), F32)],
        name=name, compiler_params=_cp(("arbitrary",)))(proj, ln_g, ln_b, w_s, bst, norm_g, dcat)


def _silu_grad(c):
    s = jax.nn.sigmoid(c)
    return s * (1.0 + c * (1.0 - s))


def _fill_prev_main(s_ref, prev_ref, main_ref, i, tt):
    s_ref[pl.ds(0, HALO), :] = jnp.where(i > 0, prev_ref[...].astype(F32), 0.0)
    s_ref[pl.ds(HALO, tt), :] = main_ref[...].astype(F32)


def _prev_spec(tt, tc, joff):
    return pl.BlockSpec((HALO, tc), lambda j, i: (jnp.maximum(i * (tt // HALO) - 1, 0), j + joff))


def _next_spec(tt, tc, joff, t):
    return pl.BlockSpec((HALO, tc), lambda j, i: (jnp.minimum((i + 1) * (tt // HALO), t // HALO - 1), j + joff))


CONV_RC = 256


def _conv_tiles(t, c):
    return _pick(t, (4096, 2048, 1024, 512, 256, 128)), _pick(c, (256, 128))


def _row_chunks(tt, fn, init=0):
    rc = min(CONV_RC, tt)
    return lax.fori_loop(0, tt // rc, lambda q, c: fn(pl.multiple_of(q * rc, rc), rc, c), init)


def _fold8(p):
    acc = p[0:8]
    for r in range(8, p.shape[0], 8):
        acc = acc + p[r:r + 8]
    return acc


def _taps_chunk(s_ref, w_ref, kw, r0, rc):
    xe = s_ref[pl.ds(HALO - 8 + r0, rc + 8), :]
    acc = w_ref[0:1, :] * xe[8 - (kw - 1):8 - (kw - 1) + rc]
    for k in range(1, kw):
        acc = acc + w_ref[k:k + 1, :] * xe[8 - (kw - 1) + k:8 - (kw - 1) + k + rc]
    return acc


def _conv_bwd_chunk(sd_ref, x, w_ref, kw, r0, rc, dws):
    de = sd_ref[pl.ds(r0, rc + 8), :]
    dx, out = None, list(dws)
    for j in range(kw):
        d = de[j:j + rc]
        k = kw - 1 - j
        term = w_ref[k:k + 1, :] * d
        dx = term if dx is None else dx + term
        out[k] = out[k] + _fold8(x * d)
    return dx, out


def _conv_ssm_fwd(proj, col0, w, b, name):
    t = proj.shape[0]
    kw, c = w.shape
    tt, tc = _conv_tiles(t, c)
    joff = col0 // tc
    assert col0 % tc == 0

    def body(x_ref, xp_ref, w_ref, b_ref, o_ref, c_ref, s_ref):
        _fill_prev_main(s_ref, xp_ref, x_ref, pl.program_id(1), tt)

        def chunk(r0, rc, carry):
            cpre = _taps_chunk(s_ref, w_ref, kw, r0, rc) + b_ref[...]
            o_ref[pl.ds(r0, rc), :] = jax.nn.silu(cpre).astype(o_ref.dtype)
            c_ref[pl.ds(r0, rc), :] = cpre.astype(c_ref.dtype)
            return carry

        _row_chunks(tt, chunk)

    out = pl.BlockSpec((tt, tc), lambda j, i: (i, j))
    return pl.pallas_call(
        body, grid=(c // tc, t // tt),
        in_specs=[pl.BlockSpec((tt, tc), lambda j, i: (i, j + joff)), _prev_spec(tt, tc, joff),
                  pl.BlockSpec((kw, tc), lambda j, i: (0, j)), pl.BlockSpec((1, tc), lambda j, i: (0, j))],
        out_specs=[out, out], out_shape=[S((t, c), XBC_DTYPE), S((t, c), ACT_DTYPE)],
        scratch_shapes=[pltpu.VMEM((HALO + tt, tc), F32)], name=name,
        compiler_params=_cp(("parallel", "arbitrary")))(proj, proj, w, b)


def _conv_ssm_bwd(proj, col0, w, cpre, wcol0, dact, dproj, out_col0, name):
    t = proj.shape[0]
    kw = w.shape[0]
    c = dact.shape[1]
    tt, tc = _conv_tiles(t, c)
    assert col0 % tc == 0 and wcol0 % tc == 0 and out_col0 % tc == 0
    joff, wj, oj = col0 // tc, wcol0 // tc, out_col0 // tc
    nt = t // tt

    def body(x_ref, w_ref, c_ref, cn_ref, d_ref, dn_ref, dp_in, dx_ref, dw_ref, db_ref, sd_ref):
        del dp_in
        i = pl.program_id(1)

        @pl.when(i == 0)
        def _():
            dw_ref[...] = jnp.zeros_like(dw_ref)
            db_ref[...] = jnp.zeros_like(db_ref)

        def stage(r0, rc, db):
            rows = pl.ds(r0, rc)
            d = d_ref[rows, :].astype(F32) * _silu_grad(c_ref[rows, :].astype(F32))
            sd_ref[rows, :] = d
            return db + _fold8(d)

        zero8 = jnp.zeros((8, tc), F32)
        db = _row_chunks(tt, stage, zero8)
        sd_ref[pl.ds(tt, HALO), :] = jnp.where(
            i < nt - 1, dn_ref[...].astype(F32) * _silu_grad(cn_ref[...].astype(F32)), 0.0)

        def chunk(r0, rc, dws):
            dx, dws = _conv_bwd_chunk(sd_ref, x_ref[pl.ds(r0, rc), :].astype(F32), w_ref, kw, r0, rc, dws)
            dx_ref[pl.ds(r0, rc), :] = dx.astype(dx_ref.dtype)
            return dws

        dws = _row_chunks(tt, chunk, [zero8] * kw)
        for k in range(kw):
            dw_ref[k:k + 1, :] += jnp.sum(dws[k], axis=0, keepdims=True)
        db_ref[0:1, :] += jnp.sum(db, axis=0, keepdims=True)

    acc = pl.BlockSpec((8, tc), lambda j, i: (0, j))
    return pl.pallas_call(
        body, grid=(c // tc, nt),
        in_specs=[pl.BlockSpec((tt, tc), lambda j, i: (i, j + joff)), pl.BlockSpec((kw, tc), lambda j, i: (0, j + wj)),
                  pl.BlockSpec((tt, tc), lambda j, i: (i, j + wj)), _next_spec(tt, tc, wj, t),
                  pl.BlockSpec((tt, tc), lambda j, i: (i, j)), _next_spec(tt, tc, 0, t),
                  pl.BlockSpec(memory_space=pl.ANY)],
        out_specs=[pl.BlockSpec((tt, tc), lambda j, i: (i, j + oj)), acc, acc],
        out_shape=[S(dproj.shape, dproj.dtype), S((8, c), F32), S((8, c), F32)],
        scratch_shapes=[pltpu.VMEM((tt + HALO, tc), F32)],
        input_output_aliases={6: 0}, name=name,
        compiler_params=_cp(("parallel", "arbitrary")))(proj, w, cpre, cpre, dact, dact, dproj)


def _conv_ffn_fwd(hid, w, b, name):
    t, f2 = hid.shape
    f = f2 // 2
    kw = w.shape[0]
    tt, tc = _conv_tiles(t, f)
    nj = f // tc

    def body(g_ref, gp_ref, u_ref, up_ref, wg_ref, wu_ref, bg_ref, bu_ref, o_ref, ot_ref, cv_ref, sg_ref, su_ref):
        i = pl.program_id(1)
        _fill_prev_main(sg_ref, gp_ref, g_ref, i, tt)
        _fill_prev_main(su_ref, up_ref, u_ref, i, tt)

        def chunk(r0, rc, carry):
            rows = pl.ds(r0, rc)
            gate = _taps_chunk(sg_ref, wg_ref, kw, r0, rc) + bg_ref[...]
            up = _taps_chunk(su_ref, wu_ref, kw, r0, rc) + bu_ref[...]
            o_ref[rows, :] = (jax.nn.silu(gate) * up).astype(o_ref.dtype)
            cv_ref[0, rows, :] = gate.astype(cv_ref.dtype)
            cv_ref[1, rows, :] = up.astype(cv_ref.dtype)
            return carry

        _row_chunks(tt, chunk)
        ot_ref[...] = o_ref[...].T

    return pl.pallas_call(
        body, grid=(nj, t // tt),
        in_specs=[pl.BlockSpec((tt, tc), lambda j, i: (i, j)), _prev_spec(tt, tc, 0),
                  pl.BlockSpec((tt, tc), lambda j, i: (i, j + nj)), _prev_spec(tt, tc, nj),
                  pl.BlockSpec((kw, tc), lambda j, i: (0, j)), pl.BlockSpec((kw, tc), lambda j, i: (0, j + nj)),
                  pl.BlockSpec((1, tc), lambda j, i: (0, j)), pl.BlockSpec((1, tc), lambda j, i: (0, j + nj))],
        out_specs=[pl.BlockSpec((tt, tc), lambda j, i: (i, j)), pl.BlockSpec((tc, tt), lambda j, i: (j, i)),
                   pl.BlockSpec((2, tt, tc), lambda j, i: (0, i, j))],
        out_shape=[S((t, f), ACT_DTYPE), S((f, t), ACT_DTYPE), S((2, t, f), ACT_DTYPE)],
        scratch_shapes=[pltpu.VMEM((HALO + tt, tc), F32), pltpu.VMEM((HALO + tt, tc), F32)], name=name,
        compiler_params=_cp(("parallel", "arbitrary")))(hid, hid, hid, hid, w, w, b, b)


def _conv_ffn_bwd(hid, w, cv, dact, name):
    t, f2 = hid.shape
    f = f2 // 2
    kw = w.shape[0]
    tt, tc = _conv_tiles(t, f)
    nj = f // tc
    nt = t // tt

    def body(g_ref, u_ref, wg_ref, wu_ref, cv_ref, cvn_ref, d_ref, dn_ref, dh_ref, dw_ref, db_ref, dg_ref, du_ref):
        i = pl.program_id(1)

        @pl.when(i == 0)
        def _():
            dw_ref[...] = jnp.zeros_like(dw_ref)
            db_ref[...] = jnp.zeros_like(db_ref)

        def cotangents(gate, up, dact_v):
            sg = jax.nn.sigmoid(gate)
            return dact_v * up * (sg * (1.0 + gate * (1.0 - sg))), dact_v * (gate * sg)

        def stage(r0, rc, dbs):
            rows = pl.ds(r0, rc)
            dg, du = cotangents(cv_ref[0, rows, :].astype(F32), cv_ref[1, rows, :].astype(F32), d_ref[rows, :].astype(F32))
            dg_ref[rows, :] = dg
            du_ref[rows, :] = du
            return [dbs[0] + _fold8(dg), dbs[1] + _fold8(du)]

        zero8 = jnp.zeros((8, tc), F32)
        dbs = _row_chunks(tt, stage, [zero8, zero8])
        dgn, dun = cotangents(cvn_ref[0].astype(F32), cvn_ref[1].astype(F32), dn_ref[...].astype(F32))
        dg_ref[pl.ds(tt, HALO), :] = jnp.where(i < nt - 1, dgn, 0.0)
        du_ref[pl.ds(tt, HALO), :] = jnp.where(i < nt - 1, dun, 0.0)
        for s, (sd_ref, x_ref, w_ref) in enumerate(((dg_ref, g_ref, wg_ref), (du_ref, u_ref, wu_ref))):
            def chunk(r0, rc, dws, s=s, sd_ref=sd_ref, x_ref=x_ref, w_ref=w_ref):
                dx, dws = _conv_bwd_chunk(sd_ref, x_ref[pl.ds(r0, rc), :].astype(F32), w_ref, kw, r0, rc, dws)
                dh_ref[s, pl.ds(r0, rc), :] = dx.astype(dh_ref.dtype)
                return dws

            dws = _row_chunks(tt, chunk, [zero8] * kw)
            for k in range(kw):
                dw_ref[s, k:k + 1, :] += jnp.sum(dws[k], axis=0, keepdims=True)
            db_ref[s, 0:1, :] += jnp.sum(dbs[s], axis=0, keepdims=True)

    acc = pl.BlockSpec((2, 8, tc), lambda j, i: (0, 0, j))
    dsc = pltpu.VMEM((tt + HALO, tc), F32)
    nxt = lambda j, i: (0, jnp.minimum((i + 1) * (tt // HALO), t // HALO - 1), j)
    return pl.pallas_call(
        body, grid=(nj, nt),
        in_specs=[pl.BlockSpec((tt, tc), lambda j, i: (i, j)), pl.BlockSpec((tt, tc), lambda j, i: (i, j + nj)),
                  pl.BlockSpec((kw, tc), lambda j, i: (0, j)), pl.BlockSpec((kw, tc), lambda j, i: (0, j + nj)),
                  pl.BlockSpec((2, tt, tc), lambda j, i: (0, i, j)), pl.BlockSpec((2, HALO, tc), nxt),
                  pl.BlockSpec((tt, tc), lambda j, i: (i, j)), _next_spec(tt, tc, 0, t)],
        out_specs=[pl.BlockSpec((2, tt, tc), lambda j, i: (0, i, j)), acc, acc],
        out_shape=[S((2, t, f), MXU_DTYPE), S((2, 8, f), F32), S((2, 8, f), F32)],
        scratch_shapes=[dsc, dsc], name=name,
        compiler_params=_cp(("parallel", "arbitrary")))(hid, hid, w, w, cv, cv, dact, dact)


SSD_SUB = 2
SSD_FWD_BLOCKS = 4


def _ssd_chunk(xs, bm, cm, dtraw, hin, bias, alog, dskip):
    n = bm.shape[0]
    row = lax.broadcasted_iota(jnp.int32, (n, n), 0)
    col = lax.broadcasted_iota(jnp.int32, (n, n), 1)
    causal = row >= col
    lane = lax.broadcasted_iota(jnp.int32, (1, 128), 1)
    sub = lax.broadcasted_iota(jnp.int32, (128, 1), 0)
    last = (lax.broadcasted_iota(jnp.int32, (n, 1), 0) == n - 1).astype(F32)
    dt = jax.nn.softplus(dtraw + bias)
    adt = dt * (-jnp.exp(alog))
    tri = causal.astype(F32)
    acs = jnp.dot(tri, adt, preferred_element_type=F32, precision=lax.Precision.HIGHEST)
    ch = lax.broadcasted_iota(jnp.int32, (128, GROUP_CH), 1)
    hd = lax.broadcasted_iota(jnp.int32, (128, GROUP_CH), 0) * HEAD_DIM
    expand = ((ch >= hd) & (ch < hd + HEAD_DIM)).astype(F32)
    acs_x = jnp.dot(acs, expand, preferred_element_type=F32, precision=lax.Precision.HIGHEST)
    alast_x = jnp.sum(acs_x * last, axis=0, keepdims=True)
    acs_t = acs.T
    scores = lax.dot_general(_mx(cm), _mx(bm), (((1,), (1,)), ((), ())), preferred_element_type=F32)
    yds, xts = [], []
    for r in range(HEADS_PER_GROUP):
        pick = (lane == r).astype(F32)
        acol = jnp.sum(acs * pick, axis=1, keepdims=True)
        arow = jnp.sum(acs_t * (sub == r).astype(F32), axis=0, keepdims=True)
        dtc = jnp.sum(dt * pick, axis=1, keepdims=True)
        lm = jnp.exp(jnp.where(causal, acol - arow, -1e30))
        xts.append(xs[r] * dtc)
        yds.append(jnp.dot(_mx(scores * lm), _mx(xts[r]), preferred_element_type=F32))
    xt = jnp.concatenate(xts, axis=1)
    yo = jnp.exp(acs_x) * jnp.dot(_mx(cm), _mx(hin), preferred_element_type=F32)
    st = lax.dot_general(_mx(bm), _mx(xt * jnp.exp(alast_x - acs_x)), (((0,), (0,)), ((), ())), preferred_element_type=F32)
    hout = jnp.exp(alast_x) * hin + st
    return jnp.concatenate(yds, axis=1) + yo + dskip * jnp.concatenate(xs, axis=1), hout


def _ssd_post(y, z, normg):
    return _rms(y * jax.nn.silu(z), normg)


def _ssd_block(datas, hin, consts):
    ys = []
    for data in datas:
        y, hin = _ssd_chunk(*data, hin, *consts)
        ys.append(y)
    return ys, hin


def _ssd_specs(d_ssm, ngrp, zcol0, rev, nb, sub=SSD_SUB):
    cc = (lambda c: nb - 1 - c) if rev else (lambda c: c)
    rows = sub * CHUNK
    xj, bj, cj, zj = 0, d_ssm // 128, d_ssm // 128 + ngrp, zcol0 // GROUP_CH
    const = lambda w: pl.BlockSpec((None, 8, w), lambda g, c: (g, 0, 0))
    return cc, [
        pl.BlockSpec((rows, GROUP_CH), lambda g, c: (cc(c), xj + g)),
        pl.BlockSpec((rows, 128), lambda g, c: (cc(c), bj + g)),
        pl.BlockSpec((rows, 128), lambda g, c: (cc(c), cj + g)),
        pl.BlockSpec((rows, 128), lambda g, c: (cc(c), g)),
        pl.BlockSpec((rows, GROUP_CH), lambda g, c: (cc(c), zj + g)),
        const(128), const(128), const(GROUP_CH), const(GROUP_CH)]


def _sub_rows(s):
    return slice(CHUNK * s, CHUNK * (s + 1))


def _ssd_load(x_ref, b_ref, c_ref, dt_ref, z_ref, bias_ref, alog_ref, dsk_ref, ng_ref, sub=SSD_SUB):
    datas, zs = [], []
    for s in range(sub):
        rows = _sub_rows(s)
        xs = [x_ref[rows, HEAD_DIM * r:HEAD_DIM * (r + 1)].astype(F32) for r in range(HEADS_PER_GROUP)]
        datas.append((xs, b_ref[rows, :].astype(F32), c_ref[rows, :].astype(F32), dt_ref[rows, :]))
        zs.append(z_ref[rows, :].astype(F32))
    return datas, zs, (bias_ref[0:1, :], alog_ref[0:1, :], dsk_ref[0:1, :]), ng_ref[0:1, :]


def _ssd_fwd(xbc, dtp, proj, zcol0, bias_p, alog_p, dskip_x, normg_x, cat, cat_t, name):
    t = xbc.shape[0]
    ngrp = bias_p.shape[0]
    d_ssm = ngrp * GROUP_CH
    fb = SSD_FWD_BLOCKS if t % (SSD_FWD_BLOCKS * SSD_SUB * CHUNK) == 0 else 1
    sub = fb * SSD_SUB
    rows = sub * CHUNK
    nb = t // rows
    d_a = cat.shape[1] - d_ssm
    assert d_a % GROUP_CH == 0 and zcol0 % GROUP_CH == 0 and t % rows == 0
    _, specs = _ssd_specs(d_ssm, ngrp, zcol0, False, nb, sub)

    def body(x_ref, b_ref, c_ref, dt_ref, z_ref, bias_ref, alog_ref, dsk_ref, ng_ref, cat_in, catt_in,
             yn_ref, ynt_ref, y_ref, hs_ref, h_ref):
        del cat_in, catt_in

        @pl.when(pl.program_id(1) == 0)
        def _():
            h_ref[...] = jnp.zeros_like(h_ref)

        datas, zs, consts, normg = _ssd_load(x_ref, b_ref, c_ref, dt_ref, z_ref, bias_ref, alog_ref, dsk_ref, ng_ref, sub)
        h, ys = h_ref[...], []
        for k in range(fb):
            hs_ref[k] = h
            ys_k, h = _ssd_block(datas[k * SSD_SUB:(k + 1) * SSD_SUB], h, consts)
            ys += ys_k
        for s in range(sub):
            y_ref[_sub_rows(s), :] = ys[s].astype(y_ref.dtype)
            yn = _ssd_post(ys[s], zs[s], normg).astype(yn_ref.dtype)
            yn_ref[_sub_rows(s), :] = yn
            ynt_ref[:, _sub_rows(s)] = yn.T
        h_ref[...] = h

    hbm = pl.BlockSpec(memory_space=pl.ANY)
    return pl.pallas_call(
        body, grid=(ngrp, nb), in_specs=specs + [hbm, hbm],
        out_specs=[pl.BlockSpec((rows, GROUP_CH), lambda g, c: (c, d_a // GROUP_CH + g)),
                   pl.BlockSpec((GROUP_CH, rows), lambda g, c: (d_a // GROUP_CH + g, c)),
                   pl.BlockSpec((rows, GROUP_CH), lambda g, c: (c, g)),
                   pl.BlockSpec((fb, None, D_STATE, GROUP_CH), lambda g, c: (c, g, 0, 0))],
        out_shape=[S(cat.shape, cat.dtype), S(cat_t.shape, cat_t.dtype), S((t, d_ssm), ACT_DTYPE),
                   S((nb * fb, ngrp, D_STATE, GROUP_CH), F32)],
        scratch_shapes=[pltpu.VMEM((D_STATE, GROUP_CH), F32)],
        input_output_aliases={9: 0, 10: 1}, name=name,
        compiler_params=_cp(("parallel", "arbitrary")))(xbc, xbc, xbc, dtp, proj, bias_p, alog_p, dskip_x, normg_x, cat, cat_t)


def _ssd_bwd(xbc, dtp, proj, zcol0, bias_p, alog_p, dskip_x, normg_x, hs, ypre, dcat, dproj, name):
    t = xbc.shape[0]
    ngrp = bias_p.shape[0]
    d_ssm = ngrp * GROUP_CH
    rows = SSD_SUB * CHUNK
    nb = t // rows
    d_a = dcat.shape[1] - d_ssm
    cc, specs = _ssd_specs(d_ssm, ngrp, zcol0, True, nb)

    def body(x_ref, b_ref, c_ref, dt_ref, z_ref, bias_ref, alog_ref, dsk_ref, ng_ref, hs_ref, yp_ref, dy_ref, dp_in,
             dz_ref, dx_ref, db_ref, dc_ref, ddt_ref, dbias_ref, dalog_ref, ddsk_ref, dng_ref, dh_ref):
        del dp_in

        @pl.when(pl.program_id(1) == 0)
        def _():
            dh_ref[...] = jnp.zeros_like(dh_ref)
            for r in (dbias_ref, dalog_ref, ddsk_ref, dng_ref):
                r[...] = jnp.zeros_like(r)

        datas, zs, consts, normg = _ssd_load(x_ref, b_ref, c_ref, dt_ref, z_ref, bias_ref, alog_ref, dsk_ref, ng_ref)
        dys, dng = [], jnp.zeros_like(normg)
        for s in range(SSD_SUB):
            rws = _sub_rows(s)
            _, vjp_post = jax.vjp(_ssd_post, yp_ref[rws, :].astype(F32), zs[s], normg)
            dy, dz, dg = vjp_post(dy_ref[rws, :].astype(F32))
            dys.append(dy)
            dng = dng + dg
            dz_ref[rws, :] = dz.astype(dz_ref.dtype)
        _, vjp = jax.vjp(_ssd_block, datas, hs_ref[...], consts)
        ddatas, dhin, (dbias, dalog, ddsk) = vjp((dys, dh_ref[...]))
        for s, (dxs, dbm, dcm, ddt) in enumerate(ddatas):
            rws = _sub_rows(s)
            for r in range(HEADS_PER_GROUP):
                dx_ref[rws, HEAD_DIM * r:HEAD_DIM * (r + 1)] = dxs[r].astype(dx_ref.dtype)
            db_ref[rws, :] = dbm.astype(db_ref.dtype)
            dc_ref[rws, :] = dcm.astype(dc_ref.dtype)
            ddt_ref[rws, :] = ddt
        dh_ref[...] = dhin
        dbias_ref[0:1, :] += dbias
        dalog_ref[0:1, :] += dalog
        ddsk_ref[0:1, :] += ddsk
        dng_ref[0:1, :] += dng

    acc = lambda w: pl.BlockSpec((None, 8, w), lambda g, c: (g, 0, 0))
    blk = lambda w: pl.BlockSpec((rows, w), lambda g, c: (cc(c), g))
    return pl.pallas_call(
        body, grid=(ngrp, nb),
        in_specs=specs + [pl.BlockSpec((None, None, D_STATE, GROUP_CH), lambda g, c: (cc(c), g, 0, 0)),
                          blk(GROUP_CH),
                          pl.BlockSpec((rows, GROUP_CH), lambda g, c: (cc(c), d_a // GROUP_CH + g)),
                          pl.BlockSpec(memory_space=pl.ANY)],
        out_specs=[pl.BlockSpec((rows, GROUP_CH), lambda g, c: (cc(c), zcol0 // GROUP_CH + g)),
                   blk(GROUP_CH), blk(128), blk(128), blk(128), acc(128), acc(128), acc(GROUP_CH), acc(GROUP_CH)],
        out_shape=[S(dproj.shape, dproj.dtype), S((t, d_ssm), XBC_DTYPE), S((t, ngrp * 128), XBC_DTYPE),
                   S((t, ngrp * 128), XBC_DTYPE), S((t, ngrp * 128), F32), S((ngrp, 8, 128), F32),
                   S((ngrp, 8, 128), F32), S((ngrp, 8, GROUP_CH), F32), S((ngrp, 8, GROUP_CH), F32)],
        scratch_shapes=[pltpu.VMEM((D_STATE, GROUP_CH), F32)],
        input_output_aliases={12: 0}, name=name,
        compiler_params=_cp(("parallel", "arbitrary")))(xbc, xbc, xbc, dtp, proj, bias_p, alog_p, dskip_x, normg_x, hs, ypre,
                                                        dcat, dproj)


def _adamw(parts, w, m, v, after, name):
    r, c = w.shape
    tr = _pick(r, (256, 128, 64, 32, 16, 8)) if c * 4 * 256 <= 4 * 1024 * 1024 else _pick(r, (64, 32, 16, 8))

    def body(p_ref, w_ref, m_ref, v_ref, after_ref, g_ref, d_ref, nm_ref, nv_ref):
        del after_ref
        g = p_ref[0].astype(F32)
        for k in range(1, N_DEV):
            g = g + p_ref[k].astype(F32)
        mm = ADAM_B1 * m_ref[...] + (1.0 - ADAM_B1) * g
        vv = ADAM_B2 * v_ref[...] + (1.0 - ADAM_B2) * jnp.square(g)
        m_hat = mm / (1.0 - ADAM_B1 ** ADAM_STEP)
        v_hat = vv / (1.0 - ADAM_B2 ** ADAM_STEP)
        g_ref[...] = g
        d_ref[...] = -ADAM_LR * (m_hat / (jnp.sqrt(v_hat) + ADAM_EPS) + ADAM_WD * w_ref[...])
        nm_ref[...] = mm
        nv_ref[...] = vv

    blk = pl.BlockSpec((tr, c), lambda i: (i, 0))
    return pl.pallas_call(
        body, grid=(r // tr,),
        in_specs=[pl.BlockSpec((N_DEV, tr, c), lambda i: (0, i, 0)), blk, blk, blk, pl.BlockSpec(memory_space=pl.ANY)],
        out_specs=[blk, blk, blk, blk], out_shape=[S((r, c), F32)] * 4, name=name,
        compiler_params=_cp(("parallel",)))(parts, w, m, v, after)


def _mesh_pos():
    return lax.axis_index("x"), lax.axis_index("y"), lax.axis_index("c")


def _peer(d, x, y, c):
    return (1 - x if (d >> 2) & 1 else x, 1 - y if (d >> 1) & 1 else y, 1 - c if d & 1 else c)


def _gather_two_level(blocks, name):
    n = len(blocks)

    def body(*refs):
        srcs, outs = refs[:n], refs[n:2 * n]
        send_sems, recv_sems, loc_sems = refs[2 * n:]
        x, y, c = _mesh_pos()
        lin = lambda px, py, pc: 4 * px + 2 * py + pc
        me, sibling = (x, y, c), (x, y, 1 - c)
        chips = [(1 - x, y), (x, 1 - y), (1 - x, 1 - y)]

        def copy(a, k, block, to, src=None):
            slab = outs[a].at[lin(*block)]
            return pltpu.make_async_remote_copy(
                src_ref=slab if src is None else src, dst_ref=slab, send_sem=send_sems.at[a, k],
                recv_sem=recv_sems.at[a, k], device_id=to, device_id_type=pl.DeviceIdType.MESH)

        mine = [pltpu.make_async_copy(srcs[a], outs[a].at[lin(*me)], loc_sems.at[a]) for a in range(n)]
        first = [copy(a, 0, me, sibling, src=srcs[a]) for a in range(n)]
        first += [copy(a, 1 + j, me, (*chip, c), src=srcs[a]) for j, chip in enumerate(chips) for a in range(n)]
        for cp in mine + first:
            cp.start()
        passed = []
        for j, chip in enumerate(chips):
            for a in range(n):
                copy(a, 1 + j, (*chip, c), me).wait_recv()
                passed.append(copy(a, 4 + j, (*chip, c), sibling))
                passed[-1].start()
        for a in range(n):
            copy(a, 0, sibling, me).wait_recv()
            for j, chip in enumerate(chips):
                copy(a, 4 + j, (*chip, 1 - c), me).wait_recv()
        for cp in first + passed:
            cp.wait_send()
        for cp in mine:
            cp.wait()

    hbm = pl.BlockSpec(memory_space=pl.ANY)
    return pl.pallas_call(
        body, in_specs=[hbm] * n, out_specs=[hbm] * n, out_shape=[S((N_DEV,) + b.shape, b.dtype) for b in blocks],
        scratch_shapes=[pltpu.SemaphoreType.DMA((n, N_DEV - 1)), pltpu.SemaphoreType.DMA((n, N_DEV - 1)),
                        pltpu.SemaphoreType.DMA((n,))],
        name=name, compiler_params=pltpu.CompilerParams(has_side_effects=True))(*blocks)


def _xfer_start(items, name, after=None):
    n = len(items)
    kinds = [k for k, _ in items]
    srcs = [pltpu.with_memory_space_constraint(a, pltpu.HBM) for _, a in items]
    land_shapes = [((N_DEV,) + a.shape if k == 'gather' else a.shape, a.dtype) for k, a in items]
    lands = [pltpu.with_memory_space_constraint(lax.empty(s, dt), pltpu.HBM) for s, dt in land_shapes]
    extra = [] if after is None else [after]

    def body(*refs):
        src_refs, land_refs = refs[:n], refs[n:2 * n]
        outs = refs[2 * n + len(extra):]
        sems = outs[:2 * n]
        token = outs[4 * n]
        x, y, c = _mesh_pos()
        me = 4 * x + 2 * y + c
        for a in range(n):
            for d in range(1, N_DEV):
                px, py, pc = _peer(d, x, y, c)
                src = src_refs[a] if kinds[a] == 'gather' else src_refs[a].at[4 * px + 2 * py + pc]
                pltpu.make_async_remote_copy(
                    src_ref=src, dst_ref=land_refs[a].at[me], send_sem=sems[2 * a].at[d - 1],
                    recv_sem=sems[2 * a + 1].at[d - 1], device_id=(px, py, pc),
                    device_id_type=pl.DeviceIdType.MESH).start()
        token[...] = jnp.zeros_like(token)

    hbm = pl.BlockSpec(memory_space=pltpu.HBM)
    sem = pl.BlockSpec(memory_space=pltpu.SEMAPHORE)
    out_shape = ([pltpu.SemaphoreType.DMA((N_DEV - 1,))] * (2 * n)
                 + [pltpu.HBM(a.shape, a.dtype) for a in srcs] + [pltpu.HBM(s, dt) for s, dt in land_shapes]
                 + [S((8, 128), F32)])
    res = pl.pallas_call(
        body, name=name, out_shape=out_shape,
        in_specs=[hbm] * (2 * n) + [pl.BlockSpec(memory_space=pl.ANY)] * len(extra),
        out_specs=[sem] * (2 * n) + [hbm] * (2 * n) + [pl.BlockSpec(memory_space=pltpu.VMEM)],
        input_output_aliases={**{a: 2 * n + a for a in range(n)}, **{n + a: 3 * n + a for a in range(n)}},
        compiler_params=pltpu.CompilerParams(has_side_effects=pltpu.SideEffectType.DATAFLOW_SIDE_EFFECTING),
    )(*srcs, *lands, *extra)
    return (kinds, res[:2 * n], res[2 * n:3 * n], res[3 * n:4 * n]), res[4 * n]


def _xfer_wait(handle, after, name):
    kinds, sems, src_thru, land_thru = handle
    n = len(kinds)

    def body(*refs):
        land_refs = refs[n:2 * n]
        sem_refs = refs[2 * n:4 * n]
        x, y, c = _mesh_pos()
        me = 4 * x + 2 * y + c
        for a in range(n):
            for d in range(1, N_DEV):
                slab = land_refs[a].at[me]
                cp = pltpu.make_async_remote_copy(
                    src_ref=slab, dst_ref=slab, send_sem=sem_refs[2 * a].at[d - 1], recv_sem=sem_refs[2 * a + 1].at[d - 1],
                    device_id=_peer(d, x, y, c), device_id_type=pl.DeviceIdType.MESH)
                cp.wait_send()
                cp.wait_recv()

    hbm = pl.BlockSpec(memory_space=pltpu.HBM)
    sem = pl.BlockSpec(memory_space=pltpu.SEMAPHORE)
    res = pl.pallas_call(
        body, name=name,
        out_shape=[pltpu.HBM(a.shape, a.dtype) for a in src_thru] + [pltpu.HBM(a.shape, a.dtype) for a in land_thru],
        in_specs=[hbm] * (2 * n) + [sem] * (2 * n) + [pl.BlockSpec(memory_space=pl.ANY)],
        out_specs=[hbm] * (2 * n), input_output_aliases={a: a for a in range(2 * n)},
        compiler_params=pltpu.CompilerParams(has_side_effects=pltpu.SideEffectType.DATAFLOW_SIDE_EFFECTING),
    )(*src_thru, *land_thru, *sems, after)
    x, y, c = _mesh_pos()
    me = 4 * x + 2 * y + c
    out = []
    for a in range(n):
        src = res[a]
        own = src[None] if kinds[a] == 'gather' else lax.dynamic_index_in_dim(src, me, 0, keepdims=True)
        out.append(lax.dynamic_update_index_in_dim(res[n + a], own, me, 0))
    return out


def _stack_to_full(kind, st):
    if kind == 'row':
        return st.reshape(st.shape[0] * st.shape[1], st.shape[2])
    return jnp.concatenate([st[k] for k in range(st.shape[0])], axis=1)


def _full_to_stack(kind, full):
    r, c = full.shape
    if kind == 'row':
        return full.reshape(N_DEV, r // N_DEV, c)
    w = c // N_DEV
    return jnp.stack([full[:, k * w:(k + 1) * w] for k in range(N_DEV)], axis=0)


SMALL_ROWS = 256


def _pack_small(named):
    layout = [(a.shape, a.size, -(-a.size // 1024) * 8) for a in named]
    total = -(-sum(nr for _, _, nr in layout) // SMALL_ROWS) * SMALL_ROWS * 128
    packed, off = None, 0
    for a, (_, n, nr) in zip(named, layout):
        part = jnp.pad(a.reshape(-1).astype(F32), (off, total - off - n))
        packed = part if packed is None else packed + part
        off += nr * 128
    return packed.reshape(total // 128, 128), layout


def _unpack_small(packed, layout):
    out, r0 = [], 0
    for shape, n, nr in layout:
        out.append(packed[r0:r0 + nr].reshape(-1)[:n].reshape(shape))
        r0 += nr
    return out


def _row0(acc):
    return acc[0]


def _local_step(x, p, tgt, sm, comm):
    t, d = x.shape
    h_n = sm['dt_bias'].shape[-1]
    ngrp = h_n // HEADS_PER_GROUP
    d_ssm = h_n * HEAD_DIM
    d_a = sm['ln_a_g'].shape[-1]
    d_mix = d_a + d_ssm
    d_xbc = sm['conv_ssm_b'].shape[-1]
    d_main = 2 * d_a + d_ssm + d_xbc
    assert d_xbc == d_ssm + 2 * ngrp * D_STATE and h_n <= 128
    zcol0, xcol0 = 2 * d_a, 2 * d_a + d_ssm
    vec = lambda v: v.reshape(1, -1)

    bst = jnp.pad(sm['b_s'].T, ((0, 0), (0, 128 - sm['b_s'].shape[0])))
    grp = lambda v, w: jnp.broadcast_to(jnp.pad(v.reshape(ngrp, 1, -1), ((0, 0), (0, 0), (0, w - v.size // ngrp))), (ngrp, 8, w))
    bias_p, alog_p = grp(sm['dt_bias'], 128), grp(sm['a_log'], 128)
    dskip_x = grp(jnp.repeat(sm['d_skip'], HEAD_DIM), GROUP_CH)
    normg_x = grp(sm['ssm_norm_g'], GROUP_CH)
    pad_dt = lambda v: jnp.pad(v[:, :h_n].reshape(t, ngrp, HEADS_PER_GROUP),
                               ((0, 0), (0, 0), (0, 128 - HEADS_PER_GROUP))).reshape(t, ngrp * 128)

    g_mix = vec(sm['norm_mix_g']) + comm.tok0
    a_n, a_t = _rms_fwd(x, g_mix, "rms_mix")
    wf = comm.weights('a', a_n)
    slabs = [wf['w_in'][k] for k in range(N_DEV)]
    w_main = jnp.concatenate(slabs[:-1] + [slabs[-1][:, :slabs[-1].shape[1] - h_n]], axis=1)
    w_dt = jnp.pad(slabs[-1][:, slabs[-1].shape[1] - h_n:], ((0, 0), (0, 128 - h_n)))
    proj = _mm_nn(a_n, w_main, out_dtype=ACT_DTYPE, name="mm_in")
    dtp = pad_dt(_mm_nn(a_n, w_dt, out_dtype=F32, name="mm_dt"))
    cat, cat_t = _gmlp_fwd(proj, vec(sm['ln_a_g']), vec(sm['ln_a_b']), sm['w_s'], bst, vec(sm['norm_a_g']), d_mix, "gmlp_fwd")
    xbc, cpre = _conv_ssm_fwd(proj, xcol0, wf['conv_ssm_w'], vec(sm['conv_ssm_b']), "conv_ssm_fwd")
    cat, cat_t, ypre, hs = _ssd_fwd(xbc, dtp, proj, zcol0, bias_p, alog_p, dskip_x, normg_x, cat, cat_t, "ssd_fwd")
    wf.update(comm.weights('b', hs))
    h1 = _mm_nn(cat, wf['w_out'], out_dtype=F32, name="mm_out", res=x)
    f_n, f_t = _rms_fwd(h1, vec(sm['norm_ffn_g']), "rms_ffn")
    hid = _mm_nn(f_n, wf['w_up'], out_dtype=ACT_DTYPE, name="mm_up")
    act, act_t, cv = _conv_ffn_fwd(hid, wf['conv_ffn_w'], vec(sm['conv_ffn_b']), "conv_ffn_fwd")
    h2 = _mm_nn(act, wf['w_down'], out_dtype=F32, name="mm_down", res=h1)
    r_n, r_t = _rms_fwd(h2, vec(sm['norm_ple_g']), "rms_ple")
    q = _mm_nn(r_n, wf['w_ple_gate'], out_dtype=ACT_DTYPE, name="mm_pg")
    p_m = p.astype(MXU_DTYPE)
    pe = _mm_nn(p_m, wf['w_ple'], out_dtype=ACT_DTYPE, name="mm_ple")

    loss, dh3, dq, dpe, dgf = _head(h2, q, pe, tgt, vec(sm['norm_final_g']), "head")
    wgrad = lambda act_t, g, name, **kw: _mm_nn(act_t, g, out_dtype=WIRE_DTYPE, name=name, wide=True, **kw)
    gs = {}
    gs['norm_final_g'] = _row0(dgf)
    g_ple = wgrad(p_m.T, dpe, "wg_ple")
    g_pg = wgrad(r_t, dq, "wg_pg")
    dr = _mm_nt(dq, wf['w_ple_gate'], out_dtype=ACT_DTYPE, name="dg_pg")
    dh2m, dg = _rms_bwd(h2, vec(sm['norm_ple_g']), dr, dh3, "rms_ple_bwd", DRES_DTYPE)
    gs['norm_ple_g'] = _row0(dg)
    g_down = wgrad(act_t, dh2m, "wg_down")
    tok = comm.send('1', {'w_ple': g_ple, 'w_ple_gate': g_pg, 'w_down': g_down})
    dact = _mm_nt(dh2m, wf['w_down'], out_dtype=ACT_DTYPE, name="dg_down")
    dhid, dcw, dcb = _conv_ffn_bwd(hid, wf['conv_ffn_w'] + tok, cv, dact, "conv_ffn_bwd")
    kf = wf['conv_ffn_w'].shape[0]
    g_cf = jnp.concatenate([dcw[0, :kf], dcw[1, :kf]], axis=1)
    gs['conv_ffn_b'] = jnp.concatenate([dcb[0, 0], dcb[1, 0]], axis=0)
    g_up = wgrad(f_t, dhid, "wg_up", b_split=2, out_slabs=N_DEV)
    df = _mm_nt(dhid, wf['w_up'], out_dtype=ACT_DTYPE, name="dg_up", a_split=2)
    dh1m, dg = _rms_bwd(h1, vec(sm['norm_ffn_g']), df, dh2m, "rms_ffn_bwd", DRES_DTYPE)
    gs['norm_ffn_g'] = _row0(dg)
    g_out = wgrad(cat_t, dh1m, "wg_out")
    tok = comm.send('2', {'conv_ffn_w': g_cf, 'w_up': g_up, 'w_out': g_out}, stacked=('w_up',))
    dcat = _mm_nt(dh1m, wf['w_out'], out_dtype=ACT_DTYPE, name="dg_out")
    dproj, dlng, dlnb, dws, dbst, dng = _gmlp_bwd(proj, vec(sm['ln_a_g']) + tok, vec(sm['ln_a_b']), sm['w_s'], bst,
                                                  vec(sm['norm_a_g']), dcat, d_main, "gmlp_bwd")
    gs['ln_a_g'], gs['ln_a_b'], gs['w_s'], gs['norm_a_g'] = _row0(dlng), _row0(dlnb), dws, _row0(dng)
    gs['b_s'] = dbst[:, :sm['b_s'].shape[0]].T
    dproj, dxs, dbm, dcm, ddtp, dbias, dalog, ddsk, dsng = _ssd_bwd(
        xbc, dtp, proj, zcol0, bias_p, alog_p, dskip_x, normg_x, hs, ypre, dcat, dproj, "ssd_bwd")
    gs['dt_bias'] = dbias[:, 0, :HEADS_PER_GROUP].reshape(h_n)
    gs['a_log'] = dalog[:, 0, :HEADS_PER_GROUP].reshape(h_n)
    gs['d_skip'] = ddsk[:, 0, :].reshape(h_n, HEAD_DIM).sum(axis=-1)
    gs['ssm_norm_g'] = dsng[:, 0, :].reshape(d_ssm)
    dws_c, dbs_c = [], []
    off = 0
    for nm, dpart in (("x", dxs), ("b", dbm), ("c", dcm)):
        dproj, dw_c, db_c = _conv_ssm_bwd(proj, xcol0 + off, wf['conv_ssm_w'], cpre, off, dpart, dproj,
                                          xcol0 + off, "conv_ssm_bwd_" + nm)
        dws_c.append(dw_c[:wf['conv_ssm_w'].shape[0]])
        dbs_c.append(db_c[0])
        off += dpart.shape[1]
    g_cs = jnp.concatenate(dws_c, axis=1)
    gs['conv_ssm_b'] = jnp.concatenate(dbs_c, axis=0)
    ddt = jnp.pad(ddtp.reshape(t, ngrp, 128)[:, :, :HEADS_PER_GROUP].reshape(t, h_n), ((0, 0), (0, 128 - h_n))).astype(MXU_DTYPE)
    g_in = jnp.concatenate([wgrad(a_t, dproj, "wg_in"), wgrad(a_t, ddt, "wg_dt")[:, :h_n]], axis=1)
    tok = comm.send('3', {'conv_ssm_w': g_cs, 'w_in': g_in}, [(n, gs[n]) for n in REPLICATED if n != 'norm_mix_g'])
    da = _mm_nt(ddt + tok.astype(ddt.dtype), w_dt, out_dtype=F32, name="dg_dt")
    da = _mm_nt(dproj, w_main, out_dtype=ACT_DTYPE, name="dg_in", res=da)
    dx, dg = _rms_bwd(x, g_mix + tok, da, dh1m, "rms_mix_bwd", F32)
    return dx, comm.send('4', {}, [('norm_mix_g', _row0(dg)), ('loss', loss[0, 0:1])])


def kernel(x, p, norm_mix_g, w_in, ln_a_g, ln_a_b, w_s, b_s, norm_a_g, conv_ssm_w, conv_ssm_b, dt_bias, a_log, d_skip, ssm_norm_g, w_out, norm_ffn_g, w_up, conv_ffn_w, conv_ffn_b, w_down, norm_ple_g, w_ple_gate, w_ple, norm_final_g, loss_target, m_norm_mix_g, m_w_in, m_ln_a_g, m_ln_a_b, m_w_s, m_b_s, m_norm_a_g, m_conv_ssm_w, m_conv_ssm_b, m_dt_bias, m_a_log, m_d_skip, m_ssm_norm_g, m_w_out, m_norm_ffn_g, m_w_up, m_conv_ffn_w, m_conv_ffn_b, m_w_down, m_norm_ple_g, m_w_ple_gate, m_w_ple, m_norm_final_g, v_norm_mix_g, v_w_in, v_ln_a_g, v_ln_a_b, v_w_s, v_b_s, v_norm_a_g, v_conv_ssm_w, v_conv_ssm_b, v_dt_bias, v_a_log, v_d_skip, v_ssm_norm_g, v_w_out, v_norm_ffn_g, v_w_up, v_conv_ffn_w, v_conv_ffn_b, v_w_down, v_norm_ple_g, v_w_ple_gate, v_w_ple, v_norm_final_g):
    given = dict(locals())
    wts = {n: given[n] for n in WEIGHTS}
    ms = {n: given["m_" + n] for n in WEIGHTS}
    vs = {n: given["v_" + n] for n in WEIGHTS}
    sm = {n: (wts[n][0] if wts[n].ndim > 1 else wts[n]) for n in REPLICATED}
    comm = _Comm({n: wts[n][0] for n in SHARDED})
    dx, _ = _local_step(x[0], p[0, 0], loss_target[0], sm, comm)

    out, loss_out, after = {}, None, comm.last_token
    for tag, names, small_names, layout, handle in comm.sent:
        recv = _xfer_wait(handle, after, "grads_%s_wait" % tag)
        for n, parts in zip(names, recv):
            out[n] = _adamw(parts, wts[n][0], ms[n][0], vs[n][0], after, "adamw_" + n)
            after = out[n][1]
        if small_names:
            pick = lambda src, fill: _pack_small([src[n] if n in src else jnp.full((1,), fill, F32) for n in small_names])[0]
            res = _adamw(recv[-1], pick(wts, 0.0), pick(ms, 0.0), pick(vs, 1.0), after, "adamw_small_" + tag)
            after = res[1]
            res = [_unpack_small(o, layout) for o in res]
            for i, n in enumerate(small_names):
                if n == 'loss':
                    loss_out = res[0][i].reshape(())
                else:
                    out[n] = [res[k][i] for k in range(4)]
    return (loss_out, dx[None], *[out[n][k].reshape(wts[n].shape) for k in range(4) for n in WEIGHTS])


class _Comm:
    GATHER_GROUPS = {'a': ('w_in', 'conv_ssm_w'), 'b': ('w_out', 'w_up', 'conv_ffn_w', 'w_down', 'w_ple_gate', 'w_ple')}

    def __init__(self, blocks):
        wired = lambda grp: [blocks[n].astype(_wire(n)) for n in self.GATHER_GROUPS[grp]]
        self.stacks_a = _gather_two_level(wired('a'), "gather_a")
        self.handle_b, tok = _xfer_start([('gather', b) for b in wired('b')], "gather_b_start", after=self.stacks_a[0])
        self.tok0 = tok[0, 0]
        self.sent = []

    def weights(self, grp, after):
        stacks = self.stacks_a if grp == 'a' else _xfer_wait(self.handle_b, after, "gather_b_wait")
        return {n: st if n == 'w_in' else _stack_to_full(SHARDED[n], st) for n, st in zip(self.GATHER_GROUPS[grp], stacks)}

    def send(self, tag, gw, small=None, stacked=()):
        items = [('scatter', g if n in stacked else _full_to_stack(SHARDED[n], g.astype(_wire(n)))) for n, g in gw.items()]
        layout, small_names = None, []
        if small:
            packed, layout = _pack_small([a for _, a in small])
            small_names = [n for n, _ in small]
            items.append(('gather', packed))
        handle, self.last_token = _xfer_start(items, "grads_%s_start" % tag)
        self.sent.append((tag, list(gw), small_names, layout, handle))
        return self.last_token[0, 0]


def _wire(name):
    return F32 if name in F32_ON_WIRE else WIRE_DTYPE
```

```python
import jax
import jax.numpy as jnp
from jax import lax
from jax.experimental import pallas as pl
from jax.experimental.pallas import tpu as pltpu

F32 = jnp.float32
MXU_DTYPE = jnp.bfloat16
ACT_DTYPE = jnp.bfloat16
XBC_DTYPE = jnp.bfloat16
WIRE_DTYPE = jnp.bfloat16
DRES_DTYPE = jnp.bfloat16
EPS = 1e-6
CHUNK = 128
D_STATE = 128
HEAD_DIM = 64
HEADS_PER_GROUP = 4
GROUP_CH = HEAD_DIM * HEADS_PER_GROUP
HALO = 16
N_DEV = 8
VMEM_LIMIT = 56 * 1024 * 1024

ADAM_LR = 0.001
ADAM_B1 = 0.9
ADAM_B2 = 0.999
ADAM_EPS = 1e-08
ADAM_WD = 0.01
ADAM_STEP = 10

WEIGHTS = ['norm_mix_g', 'w_in', 'ln_a_g', 'ln_a_b', 'w_s', 'b_s', 'norm_a_g', 'conv_ssm_w', 'conv_ssm_b', 'dt_bias',
           'a_log', 'd_skip', 'ssm_norm_g', 'w_out', 'norm_ffn_g', 'w_up', 'conv_ffn_w', 'conv_ffn_b', 'w_down',
           'norm_ple_g', 'w_ple_gate', 'w_ple', 'norm_final_g']
SHARDED = {'w_in': 'col', 'conv_ssm_w': 'col', 'w_out': 'row', 'w_up': 'col', 'conv_ffn_w': 'col', 'w_down': 'row',
           'w_ple_gate': 'row', 'w_ple': 'col'}
F32_ON_WIRE = ('conv_ssm_w', 'conv_ffn_w')
REPLICATED = [n for n in WEIGHTS if n not in SHARDED]

S = jax.ShapeDtypeStruct


def _pick(dim, cands):
    for c in cands:
        if c <= dim and dim % c == 0:
            return c
    return dim


def _cp(sem, vmem=VMEM_LIMIT):
    return pltpu.CompilerParams(dimension_semantics=sem, vmem_limit_bytes=vmem)


def _mx(v):
    return v.astype(MXU_DTYPE)


def _rms(v, g):
    return v * lax.rsqrt(jnp.mean(v * v, axis=-1, keepdims=True) + EPS) * g


MM_VMEM_BUDGET = 42 * 1024 * 1024


def _mm_tiles(m, n, k, out_bytes, has_res, tn_cands=(512, 256, 128), k_mult=1, tm_cands=(1024, 512), tn_alts=2):
    tns = [c for c in tn_cands if c <= n and n % c == 0][:tn_alts] or [n]
    ks = k // k_mult
    best = None
    for tn in tns:
        for tm in [c for c in tm_cands if m % c == 0] or [_pick(m, (256, 128))]:
            for nk in range(1, ks // 128 + 1):
                if ks % nk or (ks // nk) % 128:
                    continue
                tk = ks // nk
                need = 2 * 2 * (tm * tk + tk * tn) + tm * tn * (4 + 2 * out_bytes + (8 if has_res else 0))
                if need <= MM_VMEM_BUDGET:
                    if best is None or (nk, -tm, -tn) < best[0]:
                        best = ((nk, -tm, -tn), (tm, tn, tk))
                    break
    return best[1] if best else (_pick(m, (512, 256, 128)), tns[0], _pick(ks, (128,)))


def _mm_body(dot, nk, has_res):
    def body(*refs):
        if has_res:
            a_ref, b_ref, r_ref, o_ref, acc_ref = refs
        else:
            a_ref, b_ref, o_ref, acc_ref = refs
            r_ref = None
        kk = pl.program_id(2)
        d = dot(a_ref[...], b_ref[...])

        def fin(acc):
            if r_ref is not None:
                acc = acc + r_ref[...]
            o_ref[...] = acc.astype(o_ref.dtype)

        if nk == 1:
            fin(d)
        else:
            @pl.when(kk == 0)
            def _():
                acc_ref[...] = d

            if nk > 2:
                @pl.when((kk > 0) & (kk < nk - 1))
                def _():
                    acc_ref[...] += d

            @pl.when(kk == nk - 1)
            def _():
                fin(acc_ref[...] + d)

    return body


def _mm_call(body, grid, a_spec, b_spec, tm, tn, m, n, out_dtype, name, args, res, out_slabs=1):
    in_specs = [a_spec, b_spec]
    if res is not None:
        in_specs.append(pl.BlockSpec((tm, tn), lambda i, j, kk: (i, j)))
        args = args + [res]
    if out_slabs == 1:
        out_spec, out_shape = pl.BlockSpec((tm, tn), lambda i, j, kk: (i, j)), S((m, n), out_dtype)
    else:
        out_spec, out_shape = pl.BlockSpec((None, tm, tn), lambda i, j, kk: (j, i, 0)), S((out_slabs, m, tn), out_dtype)
    return pl.pallas_call(
        body, grid=grid, in_specs=in_specs, out_specs=out_spec, out_shape=out_shape,
        scratch_shapes=[pltpu.VMEM((tm, tn), F32)], name=name,
        compiler_params=_cp(("parallel", "parallel", "arbitrary")))(*args)


def _mm_nn(a, b, *, out_dtype, name, res=None, b_split=1, wide=False, out_slabs=1):
    m, k = a.shape
    n = b.shape[1] if b_split == 1 else b.shape[2] * b_split
    tn_cands = (n // out_slabs,) if out_slabs > 1 else (1024, 512, 256, 128) if wide else (512, 256, 128)
    tm, tn, tk = _mm_tiles(m, n // b_split, k, jnp.dtype(out_dtype).itemsize, res is not None, tn_cands=tn_cands,
                           tn_alts=1 if wide or out_slabs > 1 else 2,
                           tm_cands=(1024, 512) if wide or out_slabs > 1 else (2048, 1024, 512))
    assert out_slabs == 1 or (tn * out_slabs == n and tn % 128 == 0)
    nk = k // tk
    njs = (n // b_split) // tn
    body = _mm_body(lambda x, y: jnp.dot(x, y, preferred_element_type=F32), nk, res is not None)
    if b_split == 1:
        b_spec = pl.BlockSpec((tk, tn), lambda i, j, kk: (kk, j))
    else:
        b_spec = pl.BlockSpec((None, tk, tn), lambda i, j, kk: (j // njs, kk, j % njs))
    return _mm_call(body, (m // tm, n // tn, nk), pl.BlockSpec((tm, tk), lambda i, j, kk: (i, kk)), b_spec,
                    tm, tn, m, n, out_dtype, name, [a, b], res, out_slabs)


def _mm_nt(a, b, *, out_dtype, name, res=None, a_split=1):
    if a_split == 1:
        m, k = a.shape
    else:
        m, k = a.shape[1], a.shape[2] * a_split
    n = b.shape[0]
    tm, tn, tk = _mm_tiles(m, n, k, jnp.dtype(out_dtype).itemsize, res is not None, k_mult=a_split,
                           tm_cands=(2048, 1024))
    nk = k // tk
    nks = nk // a_split
    body = _mm_body(lambda x, y: lax.dot_general(x, y, (((1,), (1,)), ((), ())), preferred_element_type=F32),
                    nk, res is not None)
    if a_split == 1:
        a_spec = pl.BlockSpec((tm, tk), lambda i, j, kk: (i, kk))
    else:
        a_spec = pl.BlockSpec((None, tm, tk), lambda i, j, kk: (kk // nks, i, kk % nks))
    return _mm_call(body, (m // tm, n // tn, nk), a_spec, pl.BlockSpec((tn, tk), lambda i, j, kk: (j, kk)),
                    tm, tn, m, n, out_dtype, name, [a, b], res)


def _rms_fwd(x, g, name):
    t, d = x.shape
    tr = _pick(t, (512, 256, 128))

    def body(x_ref, g_ref, o_ref, ot_ref):
        y = _rms(x_ref[...], g_ref[...]).astype(o_ref.dtype)
        o_ref[...] = y
        ot_ref[...] = y.T

    return pl.pallas_call(
        body, grid=(t // tr,),
        in_specs=[pl.BlockSpec((tr, d), lambda i: (i, 0)), pl.BlockSpec((1, d), lambda i: (0, 0))],
        out_specs=[pl.BlockSpec((tr, d), lambda i: (i, 0)), pl.BlockSpec((d, tr), lambda i: (0, i))],
        out_shape=[S((t, d), ACT_DTYPE), S((d, t), ACT_DTYPE)], name=name,
        compiler_params=_cp(("parallel",)))(x, g)


def _rms_bwd(xin, g, dn, dres, name, out_dtype):
    t, d = xin.shape
    tr = _pick(t, (256, 128))

    def body(x_ref, g_ref, dn_ref, dr_ref, dx_ref, dg_ref):
        @pl.when(pl.program_id(0) == 0)
        def _():
            dg_ref[...] = jnp.zeros_like(dg_ref)

        _, vjp = jax.vjp(_rms, x_ref[...], g_ref[...])
        dx, dg = vjp(dn_ref[...].astype(F32))
        dx_ref[...] = (dr_ref[...].astype(F32) + dx).astype(dx_ref.dtype)
        dg_ref[0:1, :] += dg

    row = pl.BlockSpec((tr, d), lambda i: (i, 0))
    return pl.pallas_call(
        body, grid=(t // tr,),
        in_specs=[row, pl.BlockSpec((1, d), lambda i: (0, 0)), row, row],
        out_specs=[row, pl.BlockSpec((8, d), lambda i: (0, 0))],
        out_shape=[S((t, d), out_dtype), S((8, d), F32)], name=name,
        compiler_params=_cp(("arbitrary",)))(xin, g, dn, dres)


def _head(h2, q, pe, tgt, gf, name):
    t, d = h2.shape
    tr = _pick(t, (256, 128))

    def f(h2v, qv, pev, gfv, tv):
        h3 = h2v + jax.nn.sigmoid(qv) * pev
        y = _rms(h3, gfv)
        return 0.5 * jnp.sum(jnp.mean(jnp.square(y - tv), axis=-1))

    def body(h2_ref, q_ref, pe_ref, t_ref, g_ref, loss_ref, dh_ref, dq_ref, dpe_ref, dg_ref):
        @pl.when(pl.program_id(0) == 0)
        def _():
            loss_ref[...] = jnp.zeros_like(loss_ref)
            dg_ref[...] = jnp.zeros_like(dg_ref)

        tv = t_ref[...]
        loss, vjp = jax.vjp(lambda a, b, c, e: f(a, b, c, e, tv), h2_ref[...], q_ref[...].astype(F32),
                            pe_ref[...].astype(F32), g_ref[...])
        dh, dq, dpe, dg = vjp(jnp.ones((), F32))
        loss_ref[...] += jnp.full(loss_ref.shape, loss, F32)
        dh_ref[...] = dh.astype(dh_ref.dtype)
        dq_ref[...] = dq.astype(dq_ref.dtype)
        dpe_ref[...] = dpe.astype(dpe_ref.dtype)
        dg_ref[0:1, :] += dg

    row = pl.BlockSpec((tr, d), lambda i: (i, 0))
    return pl.pallas_call(
        body, grid=(t // tr,),
        in_specs=[row, row, row, row, pl.BlockSpec((1, d), lambda i: (0, 0))],
        out_specs=[pl.BlockSpec((8, 128), lambda i: (0, 0)), row, row, row, pl.BlockSpec((8, d), lambda i: (0, 0))],
        out_shape=[S((8, 128), F32), S((t, d), DRES_DTYPE), S((t, d), MXU_DTYPE), S((t, d), MXU_DTYPE), S((8, d), F32)],
        name=name, compiler_params=_cp(("arbitrary",)))(h2, q, pe, tgt, gf)


def _gmlp_block(us, vs, lng, lnb, wss, bss, ng):
    n = us[0].shape[0]
    row = lax.broadcasted_iota(jnp.int32, (n, n), 0)
    col = lax.broadcasted_iota(jnp.int32, (n, n), 1)
    outs = []
    for u0, v0, lg, lb, ws, bs in zip(us, vs, lng, lnb, wss, bss):
        u = jax.nn.gelu(u0)
        v = jax.nn.gelu(v0)
        mu = jnp.mean(v, axis=-1, keepdims=True)
        var = jnp.mean(jnp.square(v - mu), axis=-1, keepdims=True)
        vn = (v - mu) * lax.rsqrt(var + EPS) * lg + lb
        w = jnp.where(row >= col, ws, 0.0)
        sg = jnp.dot(_mx(w), _mx(vn), preferred_element_type=F32) + bs
        outs.append(u * sg)
    return _rms(jnp.concatenate(outs, axis=1), ng)


def _gmlp_load(proj_ref, lng_ref, lnb_ref, ws_ref, bst_ref, d_a, ng):
    sl = lambda g: slice(CHUNK * g, CHUNK * (g + 1))
    us = [proj_ref[:, sl(g)].astype(F32) for g in range(ng)]
    vs = [proj_ref[:, d_a + CHUNK * g: d_a + CHUNK * (g + 1)].astype(F32) for g in range(ng)]
    lng = [lng_ref[:, sl(g)] for g in range(ng)]
    lnb = [lnb_ref[:, sl(g)] for g in range(ng)]
    wss = [ws_ref[g] for g in range(ng)]
    bss = [bst_ref[:, g:g + 1] for g in range(ng)]
    return us, vs, lng, lnb, wss, bss


def _gmlp_fwd(proj, ln_g, ln_b, w_s, bst, norm_g, d_mix, name):
    t = proj.shape[0]
    ng = w_s.shape[0]
    d_a = ng * CHUNK

    def body(proj_ref, lng_ref, lnb_ref, ws_ref, bst_ref, ng_ref, o_ref, ot_ref):
        args = _gmlp_load(proj_ref, lng_ref, lnb_ref, ws_ref, bst_ref, d_a, ng)
        y = _gmlp_block(*args, ng_ref[...]).astype(o_ref.dtype)
        o_ref[...] = y
        ot_ref[...] = y.T

    vec = pl.BlockSpec((1, d_a), lambda c: (0, 0))
    return pl.pallas_call(
        body, grid=(t // CHUNK,),
        in_specs=[pl.BlockSpec((CHUNK, 2 * d_a), lambda c: (c, 0)), vec, vec,
                  pl.BlockSpec((ng, CHUNK, CHUNK), lambda c: (0, 0, 0)), pl.BlockSpec((CHUNK, 128), lambda c: (0, 0)), vec],
        out_specs=[pl.BlockSpec((CHUNK, d_a), lambda c: (c, 0)), pl.BlockSpec((d_a, CHUNK), lambda c: (0, c))],
        out_shape=[S((t, d_mix), ACT_DTYPE), S((d_mix, t), ACT_DTYPE)],
        name=name, compiler_params=_cp(("parallel",)))(proj, ln_g, ln_b, w_s, bst, norm_g)


def _gmlp_bwd(proj, ln_g, ln_b, w_s, bst, norm_g, dcat, d_proj, name):
    t = proj.shape[0]
    ng = w_s.shape[0]
    d_a = ng * CHUNK

    def body(proj_ref, lng_ref, lnb_ref, ws_ref, bst_ref, ng_ref, dy_ref,
             dp_ref, dlng_ref, dlnb_ref, dws_ref, dbst_ref, dng_ref):
        @pl.when(pl.program_id(0) == 0)
        def _():
            for r in (dlng_ref, dlnb_ref, dws_ref, dbst_ref, dng_ref):
                r[...] = jnp.zeros_like(r)

        args = _gmlp_load(proj_ref, lng_ref, lnb_ref, ws_ref, bst_ref, d_a, ng)
        _, vjp = jax.vjp(_gmlp_block, *args, ng_ref[...])
        dus, dvs, dlng, dlnb, dwss, dbss, dng = vjp(dy_ref[...].astype(F32))
        lane = lax.broadcasted_iota(jnp.int32, (1, 128), 1)
        dbst = jnp.zeros((CHUNK, 128), F32)
        for g in range(ng):
            dp_ref[:, CHUNK * g:CHUNK * (g + 1)] = dus[g].astype(dp_ref.dtype)
            dp_ref[:, d_a + CHUNK * g:d_a + CHUNK * (g + 1)] = dvs[g].astype(dp_ref.dtype)
            dlng_ref[0:1, CHUNK * g:CHUNK * (g + 1)] += dlng[g]
            dlnb_ref[0:1, CHUNK * g:CHUNK * (g + 1)] += dlnb[g]
            dws_ref[g] += dwss[g]
            dbst = dbst + dbss[g] * (lane == g).astype(F32)
        dbst_ref[...] += dbst
        dng_ref[0:1, :] += dng

    vec = pl.BlockSpec((1, d_a), lambda c: (0, 0))
    acc = pl.BlockSpec((8, d_a), lambda c: (0, 0))
    wspec = pl.BlockSpec((ng, CHUNK, CHUNK), lambda c: (0, 0, 0))
    bspec = pl.BlockSpec((CHUNK, 128), lambda c: (0, 0))
    return pl.pallas_call(
        body, grid=(t // CHUNK,),
        in_specs=[pl.BlockSpec((CHUNK, 2 * d_a), lambda c: (c, 0)), vec, vec, wspec, bspec, vec,
                  pl.BlockSpec((CHUNK, d_a), lambda c: (c, 0))],
        out_specs=[pl.BlockSpec((CHUNK, 2 * d_a), lambda c: (c, 0)), acc, acc, wspec, bspec, acc],
        out_shape=[S((t, d_proj), ACT_DTYPE), S((8, d_a), F32), S((8, d_a), F32), S((ng, CHUNK, CHUNK), F32),
                   S((CHUNK, 128), F32), S((8, d_a), F32)],
        name=name, compiler_params=_cp(("arbitrary",)))(proj, ln_g, ln_b, w_s, bst, norm_g, dcat)


def _silu_grad(c):
    s = jax.nn.sigmoid(c)
    return s * (1.0 + c * (1.0 - s))


def _fill_prev_main(s_ref, prev_ref, main_ref, i, tt):
    s_ref[pl.ds(0, HALO), :] = jnp.where(i > 0, prev_ref[...].astype(F32), 0.0)
    s_ref[pl.ds(HALO, tt), :] = main_ref[...].astype(F32)


def _prev_spec(tt, tc, joff):
    return pl.BlockSpec((HALO, tc), lambda j, i: (jnp.maximum(i * (tt // HALO) - 1, 0), j + joff))


def _next_spec(tt, tc, joff, t):
    return pl.BlockSpec((HALO, tc), lambda j, i: (jnp.minimum((i + 1) * (tt // HALO), t // HALO - 1), j + joff))


CONV_RC = 512


def _conv_tiles(t, c):
    return _pick(t, (4096, 2048, 1024, 512, 256, 128)), _pick(c, (256, 128))


def _row_chunks(tt, fn, init=0):
    rc = min(CONV_RC, tt)
    return lax.fori_loop(0, tt // rc, lambda q, c: fn(pl.multiple_of(q * rc, rc), rc, c), init)


def _fold8(p):
    acc = p[0:8]
    for r in range(8, p.shape[0], 8):
        acc = acc + p[r:r + 8]
    return acc


def _taps_chunk(s_ref, w_ref, kw, r0, rc):
    xe = s_ref[pl.ds(HALO - 8 + r0, rc + 8), :]
    acc = w_ref[0:1, :] * xe[8 - (kw - 1):8 - (kw - 1) + rc]
    for k in range(1, kw):
        acc = acc + w_ref[k:k + 1, :] * xe[8 - (kw - 1) + k:8 - (kw - 1) + k + rc]
    return acc


def _conv_bwd_chunk(sd_ref, x, w_ref, kw, r0, rc, dws):
    de = sd_ref[pl.ds(r0, rc + 8), :]
    dx, out = None, list(dws)
    for j in range(kw):
        d = de[j:j + rc]
        k = kw - 1 - j
        term = w_ref[k:k + 1, :] * d
        dx = term if dx is None else dx + term
        out[k] = out[k] + _fold8(x * d)
    return dx, out


def _conv_ssm_fwd(proj, col0, w, b, name):
    t = proj.shape[0]
    kw, c = w.shape
    tt, tc = _conv_tiles(t, c)
    joff = col0 // tc
    assert col0 % tc == 0

    def body(x_ref, xp_ref, w_ref, b_ref, o_ref, c_ref, s_ref):
        _fill_prev_main(s_ref, xp_ref, x_ref, pl.program_id(1), tt)

        def chunk(r0, rc, carry):
            cpre = _taps_chunk(s_ref, w_ref, kw, r0, rc) + b_ref[...]
            o_ref[pl.ds(r0, rc), :] = jax.nn.silu(cpre).astype(o_ref.dtype)
            c_ref[pl.ds(r0, rc), :] = cpre.astype(c_ref.dtype)
            return carry

        _row_chunks(tt, chunk)

    out = pl.BlockSpec((tt, tc), lambda j, i: (i, j))
    return pl.pallas_call(
        body, grid=(c // tc, t // tt),
        in_specs=[pl.BlockSpec((tt, tc), lambda j, i: (i, j + joff)), _prev_spec(tt, tc, joff),
                  pl.BlockSpec((kw, tc), lambda j, i: (0, j)), pl.BlockSpec((1, tc), lambda j, i: (0, j))],
        out_specs=[out, out], out_shape=[S((t, c), XBC_DTYPE), S((t, c), ACT_DTYPE)],
        scratch_shapes=[pltpu.VMEM((HALO + tt, tc), F32)], name=name,
        compiler_params=_cp(("parallel", "arbitrary")))(proj, proj, w, b)


def _conv_ssm_bwd(proj, col0, w, cpre, wcol0, dact, dproj, out_col0, name):
    t = proj.shape[0]
    kw = w.shape[0]
    c = dact.shape[1]
    tt, tc = _conv_tiles(t, c)
    assert col0 % tc == 0 and wcol0 % tc == 0 and out_col0 % tc == 0
    joff, wj, oj = col0 // tc, wcol0 // tc, out_col0 // tc
    nt = t // tt

    def body(x_ref, w_ref, c_ref, cn_ref, d_ref, dn_ref, dp_in, dx_ref, dw_ref, db_ref, sd_ref):
        del dp_in
        i = pl.program_id(1)

        @pl.when(i == 0)
        def _():
            dw_ref[...] = jnp.zeros_like(dw_ref)
            db_ref[...] = jnp.zeros_like(db_ref)

        def stage(r0, rc, db):
            rows = pl.ds(r0, rc)
            d = d_ref[rows, :].astype(F32) * _silu_grad(c_ref[rows, :].astype(F32))
            sd_ref[rows, :] = d
            return db + _fold8(d)

        zero8 = jnp.zeros((8, tc), F32)
        db = _row_chunks(tt, stage, zero8)
        sd_ref[pl.ds(tt, HALO), :] = jnp.where(
            i < nt - 1, dn_ref[...].astype(F32) * _silu_grad(cn_ref[...].astype(F32)), 0.0)

        def chunk(r0, rc, dws):
            dx, dws = _conv_bwd_chunk(sd_ref, x_ref[pl.ds(r0, rc), :].astype(F32), w_ref, kw, r0, rc, dws)
            dx_ref[pl.ds(r0, rc), :] = dx.astype(dx_ref.dtype)
            return dws

        dws = _row_chunks(tt, chunk, [zero8] * kw)
        for k in range(kw):
            dw_ref[k:k + 1, :] += jnp.sum(dws[k], axis=0, keepdims=True)
        db_ref[0:1, :] += jnp.sum(db, axis=0, keepdims=True)

    acc = pl.BlockSpec((8, tc), lambda j, i: (0, j))
    return pl.pallas_call(
        body, grid=(c // tc, nt),
        in_specs=[pl.BlockSpec((tt, tc), lambda j, i: (i, j + joff)), pl.BlockSpec((kw, tc), lambda j, i: (0, j + wj)),
                  pl.BlockSpec((tt, tc), lambda j, i: (i, j + wj)), _next_spec(tt, tc, wj, t),
                  pl.BlockSpec((tt, tc), lambda j, i: (i, j)), _next_spec(tt, tc, 0, t),
                  pl.BlockSpec(memory_space=pl.ANY)],
        out_specs=[pl.BlockSpec((tt, tc), lambda j, i: (i, j + oj)), acc, acc],
        out_shape=[S(dproj.shape, dproj.dtype), S((8, c), F32), S((8, c), F32)],
        scratch_shapes=[pltpu.VMEM((tt + HALO, tc), F32)],
        input_output_aliases={6: 0}, name=name,
        compiler_params=_cp(("parallel", "arbitrary")))(proj, w, cpre, cpre, dact, dact, dproj)


def _conv_ffn_fwd(hid, w, b, name):
    t, f2 = hid.shape
    f = f2 // 2
    kw = w.shape[0]
    tt, tc = _conv_tiles(t, f)
    nj = f // tc

    def body(g_ref, gp_ref, u_ref, up_ref, wg_ref, wu_ref, bg_ref, bu_ref, o_ref, ot_ref, cv_ref, sg_ref, su_ref):
        i = pl.program_id(1)
        _fill_prev_main(sg_ref, gp_ref, g_ref, i, tt)
        _fill_prev_main(su_ref, up_ref, u_ref, i, tt)

        def chunk(r0, rc, carry):
            rows = pl.ds(r0, rc)
            gate = _taps_chunk(sg_ref, wg_ref, kw, r0, rc) + bg_ref[...]
            up = _taps_chunk(su_ref, wu_ref, kw, r0, rc) + bu_ref[...]
            o_ref[rows, :] = (jax.nn.silu(gate) * up).astype(o_ref.dtype)
            cv_ref[0, rows, :] = gate.astype(cv_ref.dtype)
            cv_ref[1, rows, :] = up.astype(cv_ref.dtype)
            return carry

        _row_chunks(tt, chunk)
        ot_ref[...] = o_ref[...].T

    return pl.pallas_call(
        body, grid=(nj, t // tt),
        in_specs=[pl.BlockSpec((tt, tc), lambda j, i: (i, j)), _prev_spec(tt, tc, 0),
                  pl.BlockSpec((tt, tc), lambda j, i: (i, j + nj)), _prev_spec(tt, tc, nj),
                  pl.BlockSpec((kw, tc), lambda j, i: (0, j)), pl.BlockSpec((kw, tc), lambda j, i: (0, j + nj)),
                  pl.BlockSpec((1, tc), lambda j, i: (0, j)), pl.BlockSpec((1, tc), lambda j, i: (0, j + nj))],
        out_specs=[pl.BlockSpec((tt, tc), lambda j, i: (i, j)), pl.BlockSpec((tc, tt), lambda j, i: (j, i)),
                   pl.BlockSpec((2, tt, tc), lambda j, i: (0, i, j))],
        out_shape=[S((t, f), ACT_DTYPE), S((f, t), ACT_DTYPE), S((2, t, f), ACT_DTYPE)],
        scratch_shapes=[pltpu.VMEM((HALO + tt, tc), F32), pltpu.VMEM((HALO + tt, tc), F32)], name=name,
        compiler_params=_cp(("parallel", "arbitrary")))(hid, hid, hid, hid, w, w, b, b)


def _conv_ffn_bwd(hid, w, cv, dact, name):
    t, f2 = hid.shape
    f = f2 // 2
    kw = w.shape[0]
    tt, tc = _conv_tiles(t, f)
    nj = f // tc
    nt = t // tt

    def body(g_ref, u_ref, wg_ref, wu_ref, cv_ref, cvn_ref, d_ref, dn_ref, dh_ref, dw_ref, db_ref, dg_ref, du_ref):
        i = pl.program_id(1)

        @pl.when(i == 0)
        def _():
            dw_ref[...] = jnp.zeros_like(dw_ref)
            db_ref[...] = jnp.zeros_like(db_ref)

        def cotangents(gate, up, dact_v):
            sg = jax.nn.sigmoid(gate)
            return dact_v * up * (sg * (1.0 + gate * (1.0 - sg))), dact_v * (gate * sg)

        def stage(r0, rc, dbs):
            rows = pl.ds(r0, rc)
            dg, du = cotangents(cv_ref[0, rows, :].astype(F32), cv_ref[1, rows, :].astype(F32), d_ref[rows, :].astype(F32))
            dg_ref[rows, :] = dg
            du_ref[rows, :] = du
            return [dbs[0] + _fold8(dg), dbs[1] + _fold8(du)]

        zero8 = jnp.zeros((8, tc), F32)
        dbs = _row_chunks(tt, stage, [zero8, zero8])
        dgn, dun = cotangents(cvn_ref[0].astype(F32), cvn_ref[1].astype(F32), dn_ref[...].astype(F32))
        dg_ref[pl.ds(tt, HALO), :] = jnp.where(i < nt - 1, dgn, 0.0)
        du_ref[pl.ds(tt, HALO), :] = jnp.where(i < nt - 1, dun, 0.0)
        for s, (sd_ref, x_ref, w_ref) in enumerate(((dg_ref, g_ref, wg_ref), (du_ref, u_ref, wu_ref))):
            def chunk(r0, rc, dws, s=s, sd_ref=sd_ref, x_ref=x_ref, w_ref=w_ref):
                dx, dws = _conv_bwd_chunk(sd_ref, x_ref[pl.ds(r0, rc), :].astype(F32), w_ref, kw, r0, rc, dws)
                dh_ref[s, pl.ds(r0, rc), :] = dx.astype(dh_ref.dtype)
                return dws

            dws = _row_chunks(tt, chunk, [zero8] * kw)
            for k in range(kw):
                dw_ref[s, k:k + 1, :] += jnp.sum(dws[k], axis=0, keepdims=True)
            db_ref[s, 0:1, :] += jnp.sum(dbs[s], axis=0, keepdims=True)

    acc = pl.BlockSpec((2, 8, tc), lambda j, i: (0, 0, j))
    dsc = pltpu.VMEM((tt + HALO, tc), F32)
    nxt = lambda j, i: (0, jnp.minimum((i + 1) * (tt // HALO), t // HALO - 1), j)
    return pl.pallas_call(
        body, grid=(nj, nt),
        in_specs=[pl.BlockSpec((tt, tc), lambda j, i: (i, j)), pl.BlockSpec((tt, tc), lambda j, i: (i, j + nj)),
                  pl.BlockSpec((kw, tc), lambda j, i: (0, j)), pl.BlockSpec((kw, tc), lambda j, i: (0, j + nj)),
                  pl.BlockSpec((2, tt, tc), lambda j, i: (0, i, j)), pl.BlockSpec((2, HALO, tc), nxt),
                  pl.BlockSpec((tt, tc), lambda j, i: (i, j)), _next_spec(tt, tc, 0, t)],
        out_specs=[pl.BlockSpec((2, tt, tc), lambda j, i: (0, i, j)), acc, acc],
        out_shape=[S((2, t, f), MXU_DTYPE), S((2, 8, f), F32), S((2, 8, f), F32)],
        scratch_shapes=[dsc, dsc], name=name,
        compiler_params=_cp(("parallel", "arbitrary")))(hid, hid, w, w, cv, cv, dact, dact)


SSD_SUB = 2
SSD_FWD_BLOCKS = 4


def _ssd_chunk(xs, bm, cm, dtraw, hin, bias, alog, dskip):
    n = bm.shape[0]
    row = lax.broadcasted_iota(jnp.int32, (n, n), 0)
    col = lax.broadcasted_iota(jnp.int32, (n, n), 1)
    causal = row >= col
    lane = lax.broadcasted_iota(jnp.int32, (1, 128), 1)
    sub = lax.broadcasted_iota(jnp.int32, (128, 1), 0)
    last = (lax.broadcasted_iota(jnp.int32, (n, 1), 0) == n - 1).astype(F32)
    dt = jax.nn.softplus(dtraw + bias)
    adt = dt * (-jnp.exp(alog))
    tri = causal.astype(F32)
    acs = jnp.dot(tri, adt, preferred_element_type=F32, precision=lax.Precision.HIGHEST)
    ch = lax.broadcasted_iota(jnp.int32, (128, GROUP_CH), 1)
    hd = lax.broadcasted_iota(jnp.int32, (128, GROUP_CH), 0) * HEAD_DIM
    expand = ((ch >= hd) & (ch < hd + HEAD_DIM)).astype(F32)
    acs_x = jnp.dot(acs, expand, preferred_element_type=F32, precision=lax.Precision.HIGHEST)
    alast_x = jnp.sum(acs_x * last, axis=0, keepdims=True)
    acs_t = acs.T
    scores = lax.dot_general(_mx(cm), _mx(bm), (((1,), (1,)), ((), ())), preferred_element_type=F32)
    yds, xts = [], []
    for r in range(HEADS_PER_GROUP):
        pick = (lane == r).astype(F32)
        acol = jnp.sum(acs * pick, axis=1, keepdims=True)
        arow = jnp.sum(acs_t * (sub == r).astype(F32), axis=0, keepdims=True)
        dtc = jnp.sum(dt * pick, axis=1, keepdims=True)
        lm = jnp.exp(jnp.where(causal, acol - arow, -1e30))
        xts.append(xs[r] * dtc)
        yds.append(jnp.dot(_mx(scores * lm), _mx(xts[r]), preferred_element_type=F32))
    xt = jnp.concatenate(xts, axis=1)
    yo = jnp.exp(acs_x) * jnp.dot(_mx(cm), _mx(hin), preferred_element_type=F32)
    st = lax.dot_general(_mx(bm), _mx(xt * jnp.exp(alast_x - acs_x)), (((0,), (0,)), ((), ())), preferred_element_type=F32)
    hout = jnp.exp(alast_x) * hin + st
    return jnp.concatenate(yds, axis=1) + yo + dskip * jnp.concatenate(xs, axis=1), hout


def _ssd_post(y, z, normg):
    return _rms(y * jax.nn.silu(z), normg)


def _ssd_block(datas, hin, consts):
    ys = []
    for data in datas:
        y, hin = _ssd_chunk(*data, hin, *consts)
        ys.append(y)
    return ys, hin


def _ssd_specs(d_ssm, ngrp, zcol0, rev, nb, sub=SSD_SUB):
    cc = (lambda c: nb - 1 - c) if rev else (lambda c: c)
    rows = sub * CHUNK
    xj, bj, cj, zj = 0, d_ssm // 128, d_ssm // 128 + ngrp, zcol0 // GROUP_CH
    const = lambda w: pl.BlockSpec((None, 8, w), lambda g, c: (g, 0, 0))
    return cc, [
        pl.BlockSpec((rows, GROUP_CH), lambda g, c: (cc(c), xj + g)),
        pl.BlockSpec((rows, 128), lambda g, c: (cc(c), bj + g)),
        pl.BlockSpec((rows, 128), lambda g, c: (cc(c), cj + g)),
        pl.BlockSpec((rows, 128), lambda g, c: (cc(c), g)),
        pl.BlockSpec((rows, GROUP_CH), lambda g, c: (cc(c), zj + g)),
        const(128), const(128), const(GROUP_CH), const(GROUP_CH)]


def _sub_rows(s):
    return slice(CHUNK * s, CHUNK * (s + 1))


def _ssd_load(x_ref, b_ref, c_ref, dt_ref, z_ref, bias_ref, alog_ref, dsk_ref, ng_ref, sub=SSD_SUB):
    datas, zs = [], []
    for s in range(sub):
        rows = _sub_rows(s)
        xs = [x_ref[rows, HEAD_DIM * r:HEAD_DIM * (r + 1)].astype(F32) for r in range(HEADS_PER_GROUP)]
        datas.append((xs, b_ref[rows, :].astype(F32), c_ref[rows, :].astype(F32), dt_ref[rows, :]))
        zs.append(z_ref[rows, :].astype(F32))
    return datas, zs, (bias_ref[0:1, :], alog_ref[0:1, :], dsk_ref[0:1, :]), ng_ref[0:1, :]


def _ssd_fwd(xbc, dtp, proj, zcol0, bias_p, alog_p, dskip_x, normg_x, cat, cat_t, name):
    t = xbc.shape[0]
    ngrp = bias_p.shape[0]
    d_ssm = ngrp * GROUP_CH
    fb = SSD_FWD_BLOCKS if t % (SSD_FWD_BLOCKS * SSD_SUB * CHUNK) == 0 else 1
    sub = fb * SSD_SUB
    rows = sub * CHUNK
    nb = t // rows
    d_a = cat.shape[1] - d_ssm
    assert d_a % GROUP_CH == 0 and zcol0 % GROUP_CH == 0 and t % rows == 0
    _, specs = _ssd_specs(d_ssm, ngrp, zcol0, False, nb, sub)

    def body(x_ref, b_ref, c_ref, dt_ref, z_ref, bias_ref, alog_ref, dsk_ref, ng_ref, cat_in, catt_in,
             yn_ref, ynt_ref, y_ref, hs_ref, h_ref):
        del cat_in, catt_in

        @pl.when(pl.program_id(1) == 0)
        def _():
            h_ref[...] = jnp.zeros_like(h_ref)

        datas, zs, consts, normg = _ssd_load(x_ref, b_ref, c_ref, dt_ref, z_ref, bias_ref, alog_ref, dsk_ref, ng_ref, sub)
        h, ys = h_ref[...], []
        for k in range(fb):
            hs_ref[k] = h
            ys_k, h = _ssd_block(datas[k * SSD_SUB:(k + 1) * SSD_SUB], h, consts)
            ys += ys_k
        for s in range(sub):
            y_ref[_sub_rows(s), :] = ys[s].astype(y_ref.dtype)
            yn = _ssd_post(ys[s], zs[s], normg).astype(yn_ref.dtype)
            yn_ref[_sub_rows(s), :] = yn
            ynt_ref[:, _sub_rows(s)] = yn.T
        h_ref[...] = h

    hbm = pl.BlockSpec(memory_space=pl.ANY)
    return pl.pallas_call(
        body, grid=(ngrp, nb), in_specs=specs + [hbm, hbm],
        out_specs=[pl.BlockSpec((rows, GROUP_CH), lambda g, c: (c, d_a // GROUP_CH + g)),
                   pl.BlockSpec((GROUP_CH, rows), lambda g, c: (d_a // GROUP_CH + g, c)),
                   pl.BlockSpec((rows, GROUP_CH), lambda g, c: (c, g)),
                   pl.BlockSpec((fb, None, D_STATE, GROUP_CH), lambda g, c: (c, g, 0, 0))],
        out_shape=[S(cat.shape, cat.dtype), S(cat_t.shape, cat_t.dtype), S((t, d_ssm), ACT_DTYPE),
                   S((nb * fb, ngrp, D_STATE, GROUP_CH), F32)],
        scratch_shapes=[pltpu.VMEM((D_STATE, GROUP_CH), F32)],
        input_output_aliases={9: 0, 10: 1}, name=name,
        compiler_params=_cp(("parallel", "arbitrary")))(xbc, xbc, xbc, dtp, proj, bias_p, alog_p, dskip_x, normg_x, cat, cat_t)


def _ssd_bwd(xbc, dtp, proj, zcol0, bias_p, alog_p, dskip_x, normg_x, hs, ypre, dcat, dproj, name):
    t = xbc.shape[0]
    ngrp = bias_p.shape[0]
    d_ssm = ngrp * GROUP_CH
    rows = SSD_SUB * CHUNK
    nb = t // rows
    d_a = dcat.shape[1] - d_ssm
    cc, specs = _ssd_specs(d_ssm, ngrp, zcol0, True, nb)

    def body(x_ref, b_ref, c_ref, dt_ref, z_ref, bias_ref, alog_ref, dsk_ref, ng_ref, hs_ref, yp_ref, dy_ref, dp_in,
             dz_ref, dx_ref, db_ref, dc_ref, ddt_ref, dbias_ref, dalog_ref, ddsk_ref, dng_ref, dh_ref):
        del dp_in

        @pl.when(pl.program_id(1) == 0)
        def _():
            dh_ref[...] = jnp.zeros_like(dh_ref)
            for r in (dbias_ref, dalog_ref, ddsk_ref, dng_ref):
                r[...] = jnp.zeros_like(r)

        datas, zs, consts, normg = _ssd_load(x_ref, b_ref, c_ref, dt_ref, z_ref, bias_ref, alog_ref, dsk_ref, ng_ref)
        dys, dng = [], jnp.zeros_like(normg)
        for s in range(SSD_SUB):
            rws = _sub_rows(s)
            _, vjp_post = jax.vjp(_ssd_post, yp_ref[rws, :].astype(F32), zs[s], normg)
            dy, dz, dg = vjp_post(dy_ref[rws, :].astype(F32))
            dys.append(dy)
            dng = dng + dg
            dz_ref[rws, :] = dz.astype(dz_ref.dtype)
        _, vjp = jax.vjp(_ssd_block, datas, hs_ref[...], consts)
        ddatas, dhin, (dbias, dalog, ddsk) = vjp((dys, dh_ref[...]))
        for s, (dxs, dbm, dcm, ddt) in enumerate(ddatas):
            rws = _sub_rows(s)
            for r in range(HEADS_PER_GROUP):
                dx_ref[rws, HEAD_DIM * r:HEAD_DIM * (r + 1)] = dxs[r].astype(dx_ref.dtype)
            db_ref[rws, :] = dbm.astype(db_ref.dtype)
            dc_ref[rws, :] = dcm.astype(dc_ref.dtype)
            ddt_ref[rws, :] = ddt
        dh_ref[...] = dhin
        dbias_ref[0:1, :] += dbias
        dalog_ref[0:1, :] += dalog
        ddsk_ref[0:1, :] += ddsk
        dng_ref[0:1, :] += dng

    acc = lambda w: pl.BlockSpec((None, 8, w), lambda g, c: (g, 0, 0))
    blk = lambda w: pl.BlockSpec((rows, w), lambda g, c: (cc(c), g))
    return pl.pallas_call(
        body, grid=(ngrp, nb),
        in_specs=specs + [pl.BlockSpec((None, None, D_STATE, GROUP_CH), lambda g, c: (cc(c), g, 0, 0)),
                          blk(GROUP_CH),
                          pl.BlockSpec((rows, GROUP_CH), lambda g, c: (cc(c), d_a // GROUP_CH + g)),
                          pl.BlockSpec(memory_space=pl.ANY)],
        out_specs=[pl.BlockSpec((rows, GROUP_CH), lambda g, c: (cc(c), zcol0 // GROUP_CH + g)),
                   blk(GROUP_CH), blk(128), blk(128), blk(128), acc(128), acc(128), acc(GROUP_CH), acc(GROUP_CH)],
        out_shape=[S(dproj.shape, dproj.dtype), S((t, d_ssm), XBC_DTYPE), S((t, ngrp * 128), XBC_DTYPE),
                   S((t, ngrp * 128), XBC_DTYPE), S((t, ngrp * 128), F32), S((ngrp, 8, 128), F32),
                   S((ngrp, 8, 128), F32), S((ngrp, 8, GROUP_CH), F32), S((ngrp, 8, GROUP_CH), F32)],
        scratch_shapes=[pltpu.VMEM((D_STATE, GROUP_CH), F32)],
        input_output_aliases={12: 0}, name=name,
        compiler_params=_cp(("parallel", "arbitrary")))(xbc, xbc, xbc, dtp, proj, bias_p, alog_p, dskip_x, normg_x, hs, ypre,
                                                        dcat, dproj)


def _adamw(parts, w, m, v, after, name):
    r, c = w.shape
    tr = _pick(r, (256, 128, 64, 32, 16, 8)) if c * 4 * 256 <= 4 * 1024 * 1024 else _pick(r, (64, 32, 16, 8))

    def body(p_ref, w_ref, m_ref, v_ref, after_ref, g_ref, d_ref, nm_ref, nv_ref):
        del after_ref
        g = p_ref[0].astype(F32)
        for k in range(1, N_DEV):
            g = g + p_ref[k].astype(F32)
        mm = ADAM_B1 * m_ref[...] + (1.0 - ADAM_B1) * g
        vv = ADAM_B2 * v_ref[...] + (1.0 - ADAM_B2) * jnp.square(g)
        m_hat = mm / (1.0 - ADAM_B1 ** ADAM_STEP)
        v_hat = vv / (1.0 - ADAM_B2 ** ADAM_STEP)
        g_ref[...] = g
        d_ref[...] = -ADAM_LR * (m_hat / (jnp.sqrt(v_hat) + ADAM_EPS) + ADAM_WD * w_ref[...])
        nm_ref[...] = mm
        nv_ref[...] = vv

    blk = pl.BlockSpec((tr, c), lambda i: (i, 0))
    return pl.pallas_call(
        body, grid=(r // tr,),
        in_specs=[pl.BlockSpec((N_DEV, tr, c), lambda i: (0, i, 0)), blk, blk, blk, pl.BlockSpec(memory_space=pl.ANY)],
        out_specs=[blk, blk, blk, blk], out_shape=[S((r, c), F32)] * 4, name=name,
        compiler_params=_cp(("parallel",)))(parts, w, m, v, after)


def _mesh_pos():
    return lax.axis_index("x"), lax.axis_index("y"), lax.axis_index("c")


def _peer(d, x, y, c):
    return (1 - x if (d >> 2) & 1 else x, 1 - y if (d >> 1) & 1 else y, 1 - c if d & 1 else c)


def _gather_two_level(blocks, name):
    n = len(blocks)

    def body(*refs):
        srcs, outs = refs[:n], refs[n:2 * n]
        send_sems, recv_sems, loc_sems = refs[2 * n:]
        x, y, c = _mesh_pos()
        lin = lambda px, py, pc: 4 * px + 2 * py + pc
        me, sibling = (x, y, c), (x, y, 1 - c)
        chips = [(1 - x, y), (x, 1 - y), (1 - x, 1 - y)]

        def copy(a, k, block, to, src=None):
            slab = outs[a].at[lin(*block)]
            return pltpu.make_async_remote_copy(
                src_ref=slab if src is None else src, dst_ref=slab, send_sem=send_sems.at[a, k],
                recv_sem=recv_sems.at[a, k], device_id=to, device_id_type=pl.DeviceIdType.MESH)

        mine = [pltpu.make_async_copy(srcs[a], outs[a].at[lin(*me)], loc_sems.at[a]) for a in range(n)]
        first = [copy(a, 0, me, sibling, src=srcs[a]) for a in range(n)]
        first += [copy(a, 1 + j, me, (*chip, c), src=srcs[a]) for j, chip in enumerate(chips) for a in range(n)]
        for cp in mine + first:
            cp.start()
        passed = []
        for j, chip in enumerate(chips):
            for a in range(n):
                copy(a, 1 + j, (*chip, c), me).wait_recv()
                passed.append(copy(a, 4 + j, (*chip, c), sibling))
                passed[-1].start()
        for a in range(n):
            copy(a, 0, sibling, me).wait_recv()
            for j, chip in enumerate(chips):
                copy(a, 4 + j, (*chip, 1 - c), me).wait_recv()
        for cp in first + passed:
            cp.wait_send()
        for cp in mine:
            cp.wait()

    hbm = pl.BlockSpec(memory_space=pl.ANY)
    return pl.pallas_call(
        body, in_specs=[hbm] * n, out_specs=[hbm] * n, out_shape=[S((N_DEV,) + b.shape, b.dtype) for b in blocks],
        scratch_shapes=[pltpu.SemaphoreType.DMA((n, N_DEV - 1)), pltpu.SemaphoreType.DMA((n, N_DEV - 1)),
                        pltpu.SemaphoreType.DMA((n,))],
        name=name, compiler_params=pltpu.CompilerParams(has_side_effects=True))(*blocks)


def _xfer_start(items, name, after=None):
    n = len(items)
    kinds = [k for k, _ in items]
    srcs = [pltpu.with_memory_space_constraint(a, pltpu.HBM) for _, a in items]
    land_shapes = [((N_DEV,) + a.shape if k == 'gather' else a.shape, a.dtype) for k, a in items]
    lands = [pltpu.with_memory_space_constraint(lax.empty(s, dt), pltpu.HBM) for s, dt in land_shapes]
    extra = [] if after is None else [after]

    def body(*refs):
        src_refs, land_refs = refs[:n], refs[n:2 * n]
        outs = refs[2 * n + len(extra):]
        sems = outs[:2 * n]
        token = outs[4 * n]
        x, y, c = _mesh_pos()
        me = 4 * x + 2 * y + c
        for a in range(n):
            for d in range(1, N_DEV):
                px, py, pc = _peer(d, x, y, c)
                src = src_refs[a] if kinds[a] == 'gather' else src_refs[a].at[4 * px + 2 * py + pc]
                pltpu.make_async_remote_copy(
                    src_ref=src, dst_ref=land_refs[a].at[me], send_sem=sems[2 * a].at[d - 1],
                    recv_sem=sems[2 * a + 1].at[d - 1], device_id=(px, py, pc),
                    device_id_type=pl.DeviceIdType.MESH).start()
        token[...] = jnp.zeros_like(token)

    hbm = pl.BlockSpec(memory_space=pltpu.HBM)
    sem = pl.BlockSpec(memory_space=pltpu.SEMAPHORE)
    out_shape = ([pltpu.SemaphoreType.DMA((N_DEV - 1,))] * (2 * n)
                 + [pltpu.HBM(a.shape, a.dtype) for a in srcs] + [pltpu.HBM(s, dt) for s, dt in land_shapes]
                 + [S((8, 128), F32)])
    res = pl.pallas_call(
        body, name=name, out_shape=out_shape,
        in_specs=[hbm] * (2 * n) + [pl.BlockSpec(memory_space=pl.ANY)] * len(extra),
        out_specs=[sem] * (2 * n) + [hbm] * (2 * n) + [pl.BlockSpec(memory_space=pltpu.VMEM)],
        input_output_aliases={**{a: 2 * n + a for a in range(n)}, **{n + a: 3 * n + a for a in range(n)}},
        compiler_params=pltpu.CompilerParams(has_side_effects=pltpu.SideEffectType.DATAFLOW_SIDE_EFFECTING),
    )(*srcs, *lands, *extra)
    return (kinds, res[:2 * n], res[2 * n:3 * n], res[3 * n:4 * n]), res[4 * n]


def _xfer_wait(handle, after, name):
    kinds, sems, src_thru, land_thru = handle
    n = len(kinds)

    def body(*refs):
        land_refs = refs[n:2 * n]
        sem_refs = refs[2 * n:4 * n]
        x, y, c = _mesh_pos()
        me = 4 * x + 2 * y + c
        for a in range(n):
            for d in range(1, N_DEV):
                slab = land_refs[a].at[me]
                cp = pltpu.make_async_remote_copy(
                    src_ref=slab, dst_ref=slab, send_sem=sem_refs[2 * a].at[d - 1], recv_sem=sem_refs[2 * a + 1].at[d - 1],
                    device_id=_peer(d, x, y, c), device_id_type=pl.DeviceIdType.MESH)
                cp.wait_send()
                cp.wait_recv()

    hbm = pl.BlockSpec(memory_space=pltpu.HBM)
    sem = pl.BlockSpec(memory_space=pltpu.SEMAPHORE)
    res = pl.pallas_call(
        body, name=name,
        out_shape=[pltpu.HBM(a.shape, a.dtype) for a in src_thru] + [pltpu.HBM(a.shape, a.dtype) for a in land_thru],
        in_specs=[hbm] * (2 * n) + [sem] * (2 * n) + [pl.BlockSpec(memory_space=pl.ANY)],
        out_specs=[hbm] * (2 * n), input_output_aliases={a: a for a in range(2 * n)},
        compiler_params=pltpu.CompilerParams(has_side_effects=pltpu.SideEffectType.DATAFLOW_SIDE_EFFECTING),
    )(*src_thru, *land_thru, *sems, after)
    x, y, c = _mesh_pos()
    me = 4 * x + 2 * y + c
    out = []
    for a in range(n):
        src = res[a]
        own = src[None] if kinds[a] == 'gather' else lax.dynamic_index_in_dim(src, me, 0, keepdims=True)
        out.append(lax.dynamic_update_index_in_dim(res[n + a], own, me, 0))
    return out


def _stack_to_full(kind, st):
    if kind == 'row':
        return st.reshape(st.shape[0] * st.shape[1], st.shape[2])
    return jnp.concatenate([st[k] for k in range(st.shape[0])], axis=1)


def _full_to_stack(kind, full):
    r, c = full.shape
    if kind == 'row':
        return full.reshape(N_DEV, r // N_DEV, c)
    w = c // N_DEV
    return jnp.stack([full[:, k * w:(k + 1) * w] for k in range(N_DEV)], axis=0)


SMALL_ROWS = 256


def _pack_small(named):
    layout = [(a.shape, a.size, -(-a.size // 1024) * 8) for a in named]
    total = -(-sum(nr for _, _, nr in layout) // SMALL_ROWS) * SMALL_ROWS * 128
    packed, off = None, 0
    for a, (_, n, nr) in zip(named, layout):
        part = jnp.pad(a.reshape(-1).astype(F32), (off, total - off - n))
        packed = part if packed is None else packed + part
        off += nr * 128
    return packed.reshape(total // 128, 128), layout


def _unpack_small(packed, layout):
    out, r0 = [], 0
    for shape, n, nr in layout:
        out.append(packed[r0:r0 + nr].reshape(-1)[:n].reshape(shape))
        r0 += nr
    return out


def _row0(acc):
    return acc[0]


def _local_step(x, p, tgt, sm, comm):
    t, d = x.shape
    h_n = sm['dt_bias'].shape[-1]
    ngrp = h_n // HEADS_PER_GROUP
    d_ssm = h_n * HEAD_DIM
    d_a = sm['ln_a_g'].shape[-1]
    d_mix = d_a + d_ssm
    d_xbc = sm['conv_ssm_b'].shape[-1]
    d_main = 2 * d_a + d_ssm + d_xbc
    assert d_xbc == d_ssm + 2 * ngrp * D_STATE and h_n <= 128
    zcol0, xcol0 = 2 * d_a, 2 * d_a + d_ssm
    vec = lambda v: v.reshape(1, -1)

    bst = jnp.pad(sm['b_s'].T, ((0, 0), (0, 128 - sm['b_s'].shape[0])))
    grp = lambda v, w: jnp.broadcast_to(jnp.pad(v.reshape(ngrp, 1, -1), ((0, 0), (0, 0), (0, w - v.size // ngrp))), (ngrp, 8, w))
    bias_p, alog_p = grp(sm['dt_bias'], 128), grp(sm['a_log'], 128)
    dskip_x = grp(jnp.repeat(sm['d_skip'], HEAD_DIM), GROUP_CH)
    normg_x = grp(sm['ssm_norm_g'], GROUP_CH)
    pad_dt = lambda v: jnp.pad(v[:, :h_n].reshape(t, ngrp, HEADS_PER_GROUP),
                               ((0, 0), (0, 0), (0, 128 - HEADS_PER_GROUP))).reshape(t, ngrp * 128)

    g_mix = vec(sm['norm_mix_g']) + comm.tok0
    a_n, a_t = _rms_fwd(x, g_mix, "rms_mix")
    wf = comm.weights('a', a_n)
    slabs = [wf['w_in'][k] for k in range(N_DEV)]
    w_main = jnp.concatenate(slabs[:-1] + [slabs[-1][:, :slabs[-1].shape[1] - h_n]], axis=1)
    w_dt = jnp.pad(slabs[-1][:, slabs[-1].shape[1] - h_n:], ((0, 0), (0, 128 - h_n)))
    proj = _mm_nn(a_n, w_main, out_dtype=ACT_DTYPE, name="mm_in")
    dtp = pad_dt(_mm_nn(a_n, w_dt, out_dtype=F32, name="mm_dt"))
    cat, cat_t = _gmlp_fwd(proj, vec(sm['ln_a_g']), vec(sm['ln_a_b']), sm['w_s'], bst, vec(sm['norm_a_g']), d_mix, "gmlp_fwd")
    xbc, cpre = _conv_ssm_fwd(proj, xcol0, wf['conv_ssm_w'], vec(sm['conv_ssm_b']), "conv_ssm_fwd")
    cat, cat_t, ypre, hs = _ssd_fwd(xbc, dtp, proj, zcol0, bias_p, alog_p, dskip_x, normg_x, cat, cat_t, "ssd_fwd")
    wf.update(comm.weights('b', hs))
    h1 = _mm_nn(cat, wf['w_out'], out_dtype=F32, name="mm_out", res=x)
    f_n, f_t = _rms_fwd(h1, vec(sm['norm_ffn_g']), "rms_ffn")
    hid = _mm_nn(f_n, wf['w_up'], out_dtype=ACT_DTYPE, name="mm_up")
    act, act_t, cv = _conv_ffn_fwd(hid, wf['conv_ffn_w'], vec(sm['conv_ffn_b']), "conv_ffn_fwd")
    h2 = _mm_nn(act, wf['w_down'], out_dtype=F32, name="mm_down", res=h1)
    r_n, r_t = _rms_fwd(h2, vec(sm['norm_ple_g']), "rms_ple")
    q = _mm_nn(r_n, wf['w_ple_gate'], out_dtype=ACT_DTYPE, name="mm_pg")
    p_m = p.astype(MXU_DTYPE)
    pe = _mm_nn(p_m, wf['w_ple'], out_dtype=ACT_DTYPE, name="mm_ple")

    loss, dh3, dq, dpe, dgf = _head(h2, q, pe, tgt, vec(sm['norm_final_g']), "head")
    wgrad = lambda act_t, g, name, **kw: _mm_nn(act_t, g, out_dtype=WIRE_DTYPE, name=name, wide=True, **kw)
    gs = {}
    gs['norm_final_g'] = _row0(dgf)
    g_ple = wgrad(p_m.T, dpe, "wg_ple")
    g_pg = wgrad(r_t, dq, "wg_pg")
    dr = _mm_nt(dq, wf['w_ple_gate'], out_dtype=ACT_DTYPE, name="dg_pg")
    dh2m, dg = _rms_bwd(h2, vec(sm['norm_ple_g']), dr, dh3, "rms_ple_bwd", DRES_DTYPE)
    gs['norm_ple_g'] = _row0(dg)
    g_down = wgrad(act_t, dh2m, "wg_down")
    tok = comm.send('1', {'w_ple': g_ple, 'w_ple_gate': g_pg, 'w_down': g_down})
    dact = _mm_nt(dh2m, wf['w_down'], out_dtype=ACT_DTYPE, name="dg_down")
    dhid, dcw, dcb = _conv_ffn_bwd(hid, wf['conv_ffn_w'] + tok, cv, dact, "conv_ffn_bwd")
    kf = wf['conv_ffn_w'].shape[0]
    g_cf = jnp.concatenate([dcw[0, :kf], dcw[1, :kf]], axis=1)
    gs['conv_ffn_b'] = jnp.concatenate([dcb[0, 0], dcb[1, 0]], axis=0)
    g_up = wgrad(f_t, dhid, "wg_up", b_split=2, out_slabs=N_DEV)
    df = _mm_nt(dhid, wf['w_up'], out_dtype=ACT_DTYPE, name="dg_up", a_split=2)
    dh1m, dg = _rms_bwd(h1, vec(sm['norm_ffn_g']), df, dh2m, "rms_ffn_bwd", DRES_DTYPE)
    gs['norm_ffn_g'] = _row0(dg)
    g_out = wgrad(cat_t, dh1m, "wg_out")
    tok = comm.send('2', {'conv_ffn_w': g_cf, 'w_up': g_up, 'w_out': g_out}, stacked=('w_up',))
    dcat = _mm_nt(dh1m, wf['w_out'], out_dtype=ACT_DTYPE, name="dg_out")
    dproj, dlng, dlnb, dws, dbst, dng = _gmlp_bwd(proj, vec(sm['ln_a_g']) + tok, vec(sm['ln_a_b']), sm['w_s'], bst,
                                                  vec(sm['norm_a_g']), dcat, d_main, "gmlp_bwd")
    gs['ln_a_g'], gs['ln_a_b'], gs['w_s'], gs['norm_a_g'] = _row0(dlng), _row0(dlnb), dws, _row0(dng)
    gs['b_s'] = dbst[:, :sm['b_s'].shape[0]].T
    dproj, dxs, dbm, dcm, ddtp, dbias, dalog, ddsk, dsng = _ssd_bwd(
        xbc, dtp, proj, zcol0, bias_p, alog_p, dskip_x, normg_x, hs, ypre, dcat, dproj, "ssd_bwd")
    gs['dt_bias'] = dbias[:, 0, :HEADS_PER_GROUP].reshape(h_n)
    gs['a_log'] = dalog[:, 0, :HEADS_PER_GROUP].reshape(h_n)
    gs['d_skip'] = ddsk[:, 0, :].reshape(h_n, HEAD_DIM).sum(axis=-1)
    gs['ssm_norm_g'] = dsng[:, 0, :].reshape(d_ssm)
    dws_c, dbs_c = [], []
    off = 0
    for nm, dpart in (("x", dxs), ("b", dbm), ("c", dcm)):
        dproj, dw_c, db_c = _conv_ssm_bwd(proj, xcol0 + off, wf['conv_ssm_w'], cpre, off, dpart, dproj,
                                          xcol0 + off, "conv_ssm_bwd_" + nm)
        dws_c.append(dw_c[:wf['conv_ssm_w'].shape[0]])
        dbs_c.append(db_c[0])
        off += dpart.shape[1]
    g_cs = jnp.concatenate(dws_c, axis=1)
    gs['conv_ssm_b'] = jnp.concatenate(dbs_c, axis=0)
    ddt = jnp.pad(ddtp.reshape(t, ngrp, 128)[:, :, :HEADS_PER_GROUP].reshape(t, h_n), ((0, 0), (0, 128 - h_n))).astype(MXU_DTYPE)
    g_in = jnp.concatenate([wgrad(a_t, dproj, "wg_in"), wgrad(a_t, ddt, "wg_dt")[:, :h_n]], axis=1)
    tok = comm.send('3', {'conv_ssm_w': g_cs, 'w_in': g_in}, [(n, gs[n]) for n in REPLICATED if n != 'norm_mix_g'])
    da = _mm_nt(ddt + tok.astype(ddt.dtype), w_dt, out_dtype=F32, name="dg_dt")
    da = _mm_nt(dproj, w_main, out_dtype=ACT_DTYPE, name="dg_in", res=da)
    dx, dg = _rms_bwd(x, g_mix + tok, da, dh1m, "rms_mix_bwd", F32)
    return dx, comm.send('4', {}, [('norm_mix_g', _row0(dg)), ('loss', loss[0, 0:1])])


def kernel(x, p, norm_mix_g, w_in, ln_a_g, ln_a_b, w_s, b_s, norm_a_g, conv_ssm_w, conv_ssm_b, dt_bias, a_log, d_skip, ssm_norm_g, w_out, norm_ffn_g, w_up, conv_ffn_w, conv_ffn_b, w_down, norm_ple_g, w_ple_gate, w_ple, norm_final_g, loss_target, m_norm_mix_g, m_w_in, m_ln_a_g, m_ln_a_b, m_w_s, m_b_s, m_norm_a_g, m_conv_ssm_w, m_conv_ssm_b, m_dt_bias, m_a_log, m_d_skip, m_ssm_norm_g, m_w_out, m_norm_ffn_g, m_w_up, m_conv_ffn_w, m_conv_ffn_b, m_w_down, m_norm_ple_g, m_w_ple_gate, m_w_ple, m_norm_final_g, v_norm_mix_g, v_w_in, v_ln_a_g, v_ln_a_b, v_w_s, v_b_s, v_norm_a_g, v_conv_ssm_w, v_conv_ssm_b, v_dt_bias, v_a_log, v_d_skip, v_ssm_norm_g, v_w_out, v_norm_ffn_g, v_w_up, v_conv_ffn_w, v_conv_ffn_b, v_w_down, v_norm_ple_g, v_w_ple_gate, v_w_ple, v_norm_final_g):
    given = dict(locals())
    wts = {n: given[n] for n in WEIGHTS}
    ms = {n: given["m_" + n] for n in WEIGHTS}
    vs = {n: given["v_" + n] for n in WEIGHTS}
    sm = {n: (wts[n][0] if wts[n].ndim > 1 else wts[n]) for n in REPLICATED}
    comm = _Comm({n: wts[n][0] for n in SHARDED})
    dx, _ = _local_step(x[0], p[0, 0], loss_target[0], sm, comm)

    out, loss_out, after = {}, None, comm.last_token
    for tag, names, small_names, layout, handle in comm.sent:
        recv = _xfer_wait(handle, after, "grads_%s_wait" % tag)
        for n, parts in zip(names, recv):
            out[n] = _adamw(parts, wts[n][0], ms[n][0], vs[n][0], after, "adamw_" + n)
            after = out[n][1]
        if small_names:
            pick = lambda src, fill: _pack_small([src[n] if n in src else jnp.full((1,), fill, F32) for n in small_names])[0]
            res = _adamw(recv[-1], pick(wts, 0.0), pick(ms, 0.0), pick(vs, 1.0), after, "adamw_small_" + tag)
            after = res[1]
            res = [_unpack_small(o, layout) for o in res]
            for i, n in enumerate(small_names):
                if n == 'loss':
                    loss_out = res[0][i].reshape(())
                else:
                    out[n] = [res[k][i] for k in range(4)]
    return (loss_out, dx[None], *[out[n][k].reshape(wts[n].shape) for k in range(4) for n in WEIGHTS])


class _Comm:
    GATHER_GROUPS = {'a': ('w_in', 'conv_ssm_w'), 'b': ('w_out', 'w_up', 'conv_ffn_w', 'w_down', 'w_ple_gate', 'w_ple')}

    def __init__(self, blocks):
        wired = lambda grp: [blocks[n].astype(_wire(n)) for n in self.GATHER_GROUPS[grp]]
        self.stacks_a = _gather_two_level(wired('a'), "gather_a")
        self.handle_b, tok = _xfer_start([('gather', b) for b in wired('b')], "gather_b_start", after=self.stacks_a[0])
        self.tok0 = tok[0, 0]
        self.sent = []

    def weights(self, grp, after):
        stacks = self.stacks_a if grp == 'a' else _xfer_wait(self.handle_b, after, "gather_b_wait")
        return {n: st if n == 'w_in' else _stack_to_full(SHARDED[n], st) for n, st in zip(self.GATHER_GROUPS[grp], stacks)}

    def send(self, tag, gw, small=None, stacked=()):
        items = [('scatter', g if n in stacked else _full_to_stack(SHARDED[n], g.astype(_wire(n)))) for n, g in gw.items()]
        layout, small_names = None, []
        if small:
            packed, layout = _pack_small([a for _, a in small])
            small_names = [n for n, _ in small]
            items.append(('gather', packed))
        handle, self.last_token = _xfer_start(items, "grads_%s_start" % tag)
        self.sent.append((tag, list(gw), small_names, layout, handle))
        return self.last_token[0, 0]


def _wire(name):
    return F32 if name in F32_ON_WIRE else WIRE_DTYPE
```
